```python
import jax, jax.numpy as jnp
from jax import lax
import numpy as np

D_MODEL = 1024
BATCH = 8
SEQ = 4096
DEPTH = 2

N_MIXERS = 2
D_FF = 2816
CONV_WIDTH = 31
ATTN_GROUPS = ((128, 1), (512, 4), (2048, 16))
N_GROUPS = len(ATTN_GROUPS)
HEADS_PER_GROUP = 8
HEAD_DIM = 128
QKV_COLS = N_GROUPS * 3 * HEADS_PER_GROUP * HEAD_DIM
ATTN_OUT = HEADS_PER_GROUP * HEAD_DIM
ATTN_BLOCK = 128
NORM_EPS = 1e-6
N_CONV_LAYERS = (DEPTH + N_MIXERS - 1) // N_MIXERS
N_ATTN_LAYERS = DEPTH // N_MIXERS

kernel_name = "hybrid_conformer_conv_dilated_attn_macaron"


def rmsnorm(x, g):
    xf = x.astype(jnp.float32)
    y = xf * lax.rsqrt(jnp.mean(xf * xf, axis=-1, keepdims=True) + NORM_EPS)
    return (y * g.astype(jnp.float32)).astype(x.dtype)


def swiglu_ffn(h, w_in, w_out):
    gate, up = jnp.split(h @ w_in, 2, axis=-1)
    return (jax.nn.silu(gate) * up) @ w_out


def conformer_conv_module(h, w_pw1, b_pw1, w_dw, b_dw, norm_g, w_pw2, b_pw2):
    a, gate = jnp.split(h @ w_pw1 + b_pw1, 2, axis=-1)
    u = a * jax.nn.sigmoid(gate)
    u = lax.conv_general_dilated(
        u, w_dw[:, None, :].astype(u.dtype), window_strides=(1,),
        padding=((CONV_WIDTH - 1, 0),),
        dimension_numbers=("NWC", "WIO", "NWC"),
        feature_group_count=D_MODEL) + b_dw
    u = jax.nn.silu(rmsnorm(u, norm_g))
    return u @ w_pw2 + b_pw2


def _band_mask(n_blocks, steps):
    i = jnp.arange(ATTN_BLOCK)[:, None]
    j = jnp.arange(2 * ATTN_BLOCK)[None, :]
    diff = ATTN_BLOCK + i - j
    band = (diff >= 0) & (diff <= steps)
    key_pos = jnp.arange(n_blocks)[:, None, None] * ATTN_BLOCK - ATTN_BLOCK + j[None]
    return band[None] & (key_pos >= 0)


def dilated_window_attention(q, k, v, window, dilation):
    B, S, H, E = q.shape
    steps = window // dilation
    L = S // dilation
    nb = -(-L // ATTN_BLOCK)
    Lp = nb * ATTN_BLOCK

    def to_streams(t):
        t = t.reshape(B, L, dilation, H, E).transpose(0, 2, 1, 3, 4)
        t = jnp.pad(t, ((0, 0), (0, 0), (0, Lp - L), (0, 0), (0, 0)))
        return t.reshape(B, dilation, nb, ATTN_BLOCK, H, E)

    def with_prev(t):
        prev = jnp.concatenate([jnp.zeros_like(t[:, :, :1]), t[:, :, :-1]], axis=2)
        return jnp.concatenate([prev, t], axis=3)

    qb = to_streams(q)
    kw = with_prev(to_streams(k))
    vw = with_prev(to_streams(v))
    s = jnp.einsum("bdnqhe,bdnkhe->bdnhqk", qb, kw).astype(jnp.float32) * (HEAD_DIM ** -0.5)
    valid = _band_mask(nb, steps)[None, None, :, None]
    s = jnp.where(valid, s, -jnp.inf)
    m = jnp.max(s, axis=-1, keepdims=True)
    p = jnp.exp(s - m)
    l = jnp.sum(p, axis=-1, keepdims=True)
    o = jnp.einsum("bdnhqk,bdnkhe->bdnqhe", p / l, vw.astype(jnp.float32))
    lse = (m + jnp.log(l))[..., 0]
    o = o.reshape(B, dilation, Lp, H, E)[:, :, :L].transpose(0, 2, 1, 3, 4).reshape(B, S, H, E)
    lse = lse.transpose(0, 1, 2, 4, 3).reshape(B, dilation, Lp, H)[:, :, :L]
    lse = lse.transpose(0, 2, 1, 3).reshape(B, S, H)
    return o, lse


def dilated_attention_mixer(h, w_qkv, q_norm, k_norm, w_o):
    B, S, _ = h.shape
    qkv = (h @ w_qkv).reshape(B, S, N_GROUPS, 3, HEADS_PER_GROUP, HEAD_DIM)
    outs, lses = [], []
    for g, (window, dilation) in enumerate(ATTN_GROUPS):
        q = rmsnorm(qkv[:, :, g, 0], q_norm[g])
        k = rmsnorm(qkv[:, :, g, 1], k_norm[g])
        o, lse = dilated_window_attention(q, k, qkv[:, :, g, 2], window, dilation)
        outs.append(o)
        lses.append(lse)
    wts = jax.nn.softmax(jnp.stack(lses), axis=0)
    o = jnp.einsum("gbsh,gbshe->bshe", wts, jnp.stack(outs))
    return o.reshape(B, S, ATTN_OUT).astype(h.dtype) @ w_o


def _fwd_setup_inputs(seed: int = 0) -> dict:
    key = jax.random.key(seed)
    ks = jax.random.split(key, 16)
    nrm = lambda k, shape, scale: jax.random.normal(k, shape, jnp.float32) * scale
    return {
        "x": nrm(ks[0], (BATCH, SEQ, D_MODEL), 1.0),
        "norm_g": 1.0 + nrm(ks[1], (DEPTH, 3, D_MODEL), 0.02),
        "ffn_w_in": nrm(ks[2], (DEPTH, 2, D_MODEL, 2 * D_FF), D_MODEL ** -0.5),
        "ffn_w_out": nrm(ks[3], (DEPTH, 2, D_FF, D_MODEL), D_FF ** -0.5),
        "conv_w_pw1": nrm(ks[4], (N_CONV_LAYERS, D_MODEL, 2 * D_MODEL), D_MODEL ** -0.5),
        "conv_b_pw1": nrm(ks[5], (N_CONV_LAYERS, 2 * D_MODEL), 0.01),
        "conv_w_dw": nrm(ks[6], (N_CONV_LAYERS, CONV_WIDTH, D_MODEL), CONV_WIDTH ** -0.5),
        "conv_b_dw": nrm(ks[7], (N_CONV_LAYERS, D_MODEL), 0.01),
        "conv_norm_g": 1.0 + nrm(ks[8], (N_CONV_LAYERS, D_MODEL), 0.02),
        "conv_w_pw2": nrm(ks[9], (N_CONV_LAYERS, D_MODEL, D_MODEL), D_MODEL ** -0.5),
        "conv_b_pw2": nrm(ks[10], (N_CONV_LAYERS, D_MODEL), 0.01),
        "attn_w_qkv": nrm(ks[11], (N_ATTN_LAYERS, D_MODEL, QKV_COLS), D_MODEL ** -0.5),
        "attn_q_norm": 1.0 + nrm(ks[12], (N_ATTN_LAYERS, N_GROUPS, HEAD_DIM), 0.02),
        "attn_k_norm": 1.0 + nrm(ks[13], (N_ATTN_LAYERS, N_GROUPS, HEAD_DIM), 0.02),
        "attn_w_o": nrm(ks[14], (N_ATTN_LAYERS, ATTN_OUT, D_MODEL), ATTN_OUT ** -0.5),
    }


def _fwd_reference(x, norm_g, ffn_w_in, ffn_w_out, conv_w_pw1, conv_b_pw1, conv_w_dw, conv_b_dw,
              conv_norm_g, conv_w_pw2, conv_b_pw2, attn_w_qkv, attn_q_norm, attn_k_norm, attn_w_o):
    for layer in range(DEPTH):
        x = x + 0.5 * swiglu_ffn(rmsnorm(x, norm_g[layer, 0]), ffn_w_in[layer, 0], ffn_w_out[layer, 0])
        h = rmsnorm(x, norm_g[layer, 1])
        idx = layer // N_MIXERS
        if layer % N_MIXERS == 0:
            mix = conformer_conv_module(h, conv_w_pw1[idx], conv_b_pw1[idx], conv_w_dw[idx], conv_b_dw[idx],
                                        conv_norm_g[idx], conv_w_pw2[idx], conv_b_pw2[idx])
        else:
            mix = dilated_attention_mixer(h, attn_w_qkv[idx], attn_q_norm[idx], attn_k_norm[idx], attn_w_o[idx])
        x = x + mix
        x = x + 0.5 * swiglu_ffn(rmsnorm(x, norm_g[layer, 2]), ffn_w_in[layer, 1], ffn_w_out[layer, 1])
    return x


import jax as _jax
import jax.numpy as _jnp

TWIN_FORMAT = 'train_step'
FWD_PARAMS = ['x', 'norm_g', 'ffn_w_in', 'ffn_w_out', 'conv_w_pw1', 'conv_b_pw1', 'conv_w_dw', 'conv_b_dw', 'conv_norm_g', 'conv_w_pw2', 'conv_b_pw2', 'attn_w_qkv', 'attn_q_norm', 'attn_k_norm', 'attn_w_o']
TWIN_WEIGHTS = ['norm_g', 'ffn_w_in', 'ffn_w_out', 'conv_w_pw1', 'conv_b_pw1', 'conv_w_dw', 'conv_b_dw', 'conv_norm_g', 'conv_w_pw2', 'conv_b_pw2', 'attn_w_qkv', 'attn_q_norm', 'attn_k_norm', 'attn_w_o']
TWIN_DIFF_INPUT = 'x'
TWIN_INPUTS = ['x', 'norm_g', 'ffn_w_in', 'ffn_w_out', 'conv_w_pw1', 'conv_b_pw1', 'conv_w_dw', 'conv_b_dw', 'conv_norm_g', 'conv_w_pw2', 'conv_b_pw2', 'attn_w_qkv', 'attn_q_norm', 'attn_k_norm', 'attn_w_o', 'loss_target', 'm_norm_g', 'm_ffn_w_in', 'm_ffn_w_out', 'm_conv_w_pw1', 'm_conv_b_pw1', 'm_conv_w_dw', 'm_conv_b_dw', 'm_conv_norm_g', 'm_conv_w_pw2', 'm_conv_b_pw2', 'm_attn_w_qkv', 'm_attn_q_norm', 'm_attn_k_norm', 'm_attn_w_o', 'v_norm_g', 'v_ffn_w_in', 'v_ffn_w_out', 'v_conv_w_pw1', 'v_conv_b_pw1', 'v_conv_w_dw', 'v_conv_b_dw', 'v_conv_norm_g', 'v_conv_w_pw2', 'v_conv_b_pw2', 'v_attn_w_qkv', 'v_attn_q_norm', 'v_attn_k_norm', 'v_attn_w_o']
TWIN_OUTPUTS = ['loss', 'grad_x', 'grad_norm_g', 'grad_ffn_w_in', 'grad_ffn_w_out', 'grad_conv_w_pw1', 'grad_conv_b_pw1', 'grad_conv_w_dw', 'grad_conv_b_dw', 'grad_conv_norm_g', 'grad_conv_w_pw2', 'grad_conv_b_pw2', 'grad_attn_w_qkv', 'grad_attn_q_norm', 'grad_attn_k_norm', 'grad_attn_w_o', 'delta_norm_g', 'delta_ffn_w_in', 'delta_ffn_w_out', 'delta_conv_w_pw1', 'delta_conv_b_pw1', 'delta_conv_w_dw', 'delta_conv_b_dw', 'delta_conv_norm_g', 'delta_conv_w_pw2', 'delta_conv_b_pw2', 'delta_attn_w_qkv', 'delta_attn_q_norm', 'delta_attn_k_norm', 'delta_attn_w_o', 'new_m_norm_g', 'new_m_ffn_w_in', 'new_m_ffn_w_out', 'new_m_conv_w_pw1', 'new_m_conv_b_pw1', 'new_m_conv_w_dw', 'new_m_conv_b_dw', 'new_m_conv_norm_g', 'new_m_conv_w_pw2', 'new_m_conv_b_pw2', 'new_m_attn_w_qkv', 'new_m_attn_q_norm', 'new_m_attn_k_norm', 'new_m_attn_w_o', 'new_v_norm_g', 'new_v_ffn_w_in', 'new_v_ffn_w_out', 'new_v_conv_w_pw1', 'new_v_conv_b_pw1', 'new_v_conv_w_dw', 'new_v_conv_b_dw', 'new_v_conv_norm_g', 'new_v_conv_w_pw2', 'new_v_conv_b_pw2', 'new_v_attn_w_qkv', 'new_v_attn_q_norm', 'new_v_attn_k_norm', 'new_v_attn_w_o']
TWIN_LEAF_KINDS = {'loss': 'loss', 'grad_x': 'grad_x', 'grad_norm_g': 'grad_w', 'grad_ffn_w_in': 'grad_w', 'grad_ffn_w_out': 'grad_w', 'grad_conv_w_pw1': 'grad_w', 'grad_conv_b_pw1': 'grad_w', 'grad_conv_w_dw': 'grad_w', 'grad_conv_b_dw': 'grad_w', 'grad_conv_norm_g': 'grad_w', 'grad_conv_w_pw2': 'grad_w', 'grad_conv_b_pw2': 'grad_w', 'grad_attn_w_qkv': 'grad_w', 'grad_attn_q_norm': 'grad_w', 'grad_attn_k_norm': 'grad_w', 'grad_attn_w_o': 'grad_w', 'delta_norm_g': 'delta_w', 'delta_ffn_w_in': 'delta_w', 'delta_ffn_w_out': 'delta_w', 'delta_conv_w_pw1': 'delta_w', 'delta_conv_b_pw1': 'delta_w', 'delta_conv_w_dw': 'delta_w', 'delta_conv_b_dw': 'delta_w', 'delta_conv_norm_g': 'delta_w', 'delta_conv_w_pw2': 'delta_w', 'delta_conv_b_pw2': 'delta_w', 'delta_attn_w_qkv': 'delta_w', 'delta_attn_q_norm': 'delta_w', 'delta_attn_k_norm': 'delta_w', 'delta_attn_w_o': 'delta_w', 'new_m_norm_g': 'new_m', 'new_m_ffn_w_in': 'new_m', 'new_m_ffn_w_out': 'new_m', 'new_m_conv_w_pw1': 'new_m', 'new_m_conv_b_pw1': 'new_m', 'new_m_conv_w_dw': 'new_m', 'new_m_conv_b_dw': 'new_m', 'new_m_conv_norm_g': 'new_m', 'new_m_conv_w_pw2': 'new_m', 'new_m_conv_b_pw2': 'new_m', 'new_m_attn_w_qkv': 'new_m', 'new_m_attn_q_norm': 'new_m', 'new_m_attn_k_norm': 'new_m', 'new_m_attn_w_o': 'new_m', 'new_v_norm_g': 'new_v', 'new_v_ffn_w_in': 'new_v', 'new_v_ffn_w_out': 'new_v', 'new_v_conv_w_pw1': 'new_v', 'new_v_conv_b_pw1': 'new_v', 'new_v_conv_w_dw': 'new_v', 'new_v_conv_b_dw': 'new_v', 'new_v_conv_norm_g': 'new_v', 'new_v_conv_w_pw2': 'new_v', 'new_v_conv_b_pw2': 'new_v', 'new_v_attn_w_qkv': 'new_v', 'new_v_attn_q_norm': 'new_v', 'new_v_attn_k_norm': 'new_v', 'new_v_attn_w_o': 'new_v'}


def _forward(args):
    return _fwd_reference(*[args[k] for k in FWD_PARAMS])


def _output_shape():
    out = _jax.eval_shape(lambda: _forward(_fwd_setup_inputs(0)))
    return out.shape, out.dtype

N_MICROBATCH = 1
ADAM_LR = 0.001
ADAM_B1 = 0.9
ADAM_B2 = 0.999
ADAM_EPS = 1e-08
ADAM_WD = 0.01
ADAM_STEP = 10
PER_EXAMPLE_BATCH_AXIS = {'x': 0, 'loss_target': 0}
SHARED_INPUTS = []
_WEIGHT_DTYPES = {'norm_g': _jnp.float32, 'ffn_w_in': _jnp.float32, 'ffn_w_out': _jnp.float32, 'conv_w_pw1': _jnp.float32, 'conv_b_pw1': _jnp.float32, 'conv_w_dw': _jnp.float32, 'conv_b_dw': _jnp.float32, 'conv_norm_g': _jnp.float32, 'conv_w_pw2': _jnp.float32, 'conv_b_pw2': _jnp.float32, 'attn_w_qkv': _jnp.float32, 'attn_q_norm': _jnp.float32, 'attn_k_norm': _jnp.float32, 'attn_w_o': _jnp.float32}
MOMENT_SCALE = {'norm_g': 4.897823e+00, 'ffn_w_in': 1.088448e-01, 'ffn_w_out': 1.742121e-01, 'conv_w_pw1': 1.985507e-01, 'conv_b_pw1': 7.350514e+00, 'conv_w_dw': 4.917500e-01, 'conv_b_dw': 1.811628e+01, 'conv_norm_g': 1.483390e+01, 'conv_w_pw2': 1.845822e+00, 'conv_b_pw2': 9.315008e+00, 'attn_w_qkv': 1.965605e-01, 'attn_q_norm': 4.769978e-01, 'attn_k_norm': 4.781005e-01, 'attn_w_o': 5.993947e-01}


def _to_microbatches(a, axis):
    t = _jnp.moveaxis(a, axis, 0)
    t = t.reshape((N_MICROBATCH, t.shape[0] // N_MICROBATCH) + t.shape[1:])
    return _jnp.moveaxis(t, 1, axis + 1)


def setup_inputs(seed: int = 0) -> dict:
    inp = _fwd_setup_inputs(seed)
    key = _jax.random.fold_in(_jax.random.key(seed), 7919)
    shape, _ = _output_shape()
    out = dict(inp)
    out["loss_target"] = _jax.random.normal(_jax.random.fold_in(key, 0), shape, _jnp.float32)
    for i, name in enumerate(TWIN_WEIGHTS):
        w = inp[name].astype(_jnp.float32)
        if MOMENT_SCALE is None:
            s = _jnp.sqrt(_jnp.mean(_jnp.square(w)) + 1e-30)
        else:
            s = MOMENT_SCALE[name]
        km, kv = _jax.random.split(_jax.random.fold_in(key, i + 1))
        out[name] = w
        out["m_" + name] = s * _jax.random.normal(km, w.shape, _jnp.float32)
        out["v_" + name] = (s * s) * _jax.random.uniform(kv, w.shape, _jnp.float32, 0.5, 1.5)
    if N_MICROBATCH > 1:
        for name, axis in PER_EXAMPLE_BATCH_AXIS.items():
            out[name] = _to_microbatches(out[name], axis)
    return {'x': out['x'], 'norm_g': out['norm_g'], 'ffn_w_in': out['ffn_w_in'], 'ffn_w_out': out['ffn_w_out'], 'conv_w_pw1': out['conv_w_pw1'], 'conv_b_pw1': out['conv_b_pw1'], 'conv_w_dw': out['conv_w_dw'], 'conv_b_dw': out['conv_b_dw'], 'conv_norm_g': out['conv_norm_g'], 'conv_w_pw2': out['conv_w_pw2'], 'conv_b_pw2': out['conv_b_pw2'], 'attn_w_qkv': out['attn_w_qkv'], 'attn_q_norm': out['attn_q_norm'], 'attn_k_norm': out['attn_k_norm'], 'attn_w_o': out['attn_w_o'], 'loss_target': out['loss_target'], 'm_norm_g': out['m_norm_g'], 'm_ffn_w_in': out['m_ffn_w_in'], 'm_ffn_w_out': out['m_ffn_w_out'], 'm_conv_w_pw1': out['m_conv_w_pw1'], 'm_conv_b_pw1': out['m_conv_b_pw1'], 'm_conv_w_dw': out['m_conv_w_dw'], 'm_conv_b_dw': out['m_conv_b_dw'], 'm_conv_norm_g': out['m_conv_norm_g'], 'm_conv_w_pw2': out['m_conv_w_pw2'], 'm_conv_b_pw2': out['m_conv_b_pw2'], 'm_attn_w_qkv': out['m_attn_w_qkv'], 'm_attn_q_norm': out['m_attn_q_norm'], 'm_attn_k_norm': out['m_attn_k_norm'], 'm_attn_w_o': out['m_attn_w_o'], 'v_norm_g': out['v_norm_g'], 'v_ffn_w_in': out['v_ffn_w_in'], 'v_ffn_w_out': out['v_ffn_w_out'], 'v_conv_w_pw1': out['v_conv_w_pw1'], 'v_conv_b_pw1': out['v_conv_b_pw1'], 'v_conv_w_dw': out['v_conv_w_dw'], 'v_conv_b_dw': out['v_conv_b_dw'], 'v_conv_norm_g': out['v_conv_norm_g'], 'v_conv_w_pw2': out['v_conv_w_pw2'], 'v_conv_b_pw2': out['v_conv_b_pw2'], 'v_attn_w_qkv': out['v_attn_w_qkv'], 'v_attn_q_norm': out['v_attn_q_norm'], 'v_attn_k_norm': out['v_attn_k_norm'], 'v_attn_w_o': out['v_attn_w_o']}


def _loss(weights, diff, rest, loss_target):
    with _jax.named_scope("forward"):
        args = {**rest, TWIN_DIFF_INPUT: diff, **{k: w.astype(_WEIGHT_DTYPES[k]) for k, w in weights.items()}}
        y = _forward(args)
    with _jax.named_scope("loss_head"):
        err = _jnp.square(y.astype(_jnp.float32) - loss_target)
        return 0.5 * _jnp.sum(_jnp.mean(err, axis=-1)) if err.ndim else 0.5 * err


def _adamw(w, g, m, v):
    m = ADAM_B1 * m + (1.0 - ADAM_B1) * g
    v = ADAM_B2 * v + (1.0 - ADAM_B2) * _jnp.square(g)
    m_hat = m / (1.0 - ADAM_B1 ** ADAM_STEP)
    v_hat = v / (1.0 - ADAM_B2 ** ADAM_STEP)
    delta = -ADAM_LR * (m_hat / (_jnp.sqrt(v_hat) + ADAM_EPS) + ADAM_WD * w)
    return delta, m, v


def reference(x, norm_g, ffn_w_in, ffn_w_out, conv_w_pw1, conv_b_pw1, conv_w_dw, conv_b_dw, conv_norm_g, conv_w_pw2, conv_b_pw2, attn_w_qkv, attn_q_norm, attn_k_norm, attn_w_o, loss_target, m_norm_g, m_ffn_w_in, m_ffn_w_out, m_conv_w_pw1, m_conv_b_pw1, m_conv_w_dw, m_conv_b_dw, m_conv_norm_g, m_conv_w_pw2, m_conv_b_pw2, m_attn_w_qkv, m_attn_q_norm, m_attn_k_norm, m_attn_w_o, v_norm_g, v_ffn_w_in, v_ffn_w_out, v_conv_w_pw1, v_conv_b_pw1, v_conv_w_dw, v_conv_b_dw, v_conv_norm_g, v_conv_w_pw2, v_conv_b_pw2, v_attn_w_qkv, v_attn_q_norm, v_attn_k_norm, v_attn_w_o):
    given = dict(x=x, norm_g=norm_g, ffn_w_in=ffn_w_in, ffn_w_out=ffn_w_out, conv_w_pw1=conv_w_pw1, conv_b_pw1=conv_b_pw1, conv_w_dw=conv_w_dw, conv_b_dw=conv_b_dw, conv_norm_g=conv_norm_g, conv_w_pw2=conv_w_pw2, conv_b_pw2=conv_b_pw2, attn_w_qkv=attn_w_qkv, attn_q_norm=attn_q_norm, attn_k_norm=attn_k_norm, attn_w_o=attn_w_o, loss_target=loss_target, m_norm_g=m_norm_g, m_ffn_w_in=m_ffn_w_in, m_ffn_w_out=m_ffn_w_out, m_conv_w_pw1=m_conv_w_pw1, m_conv_b_pw1=m_conv_b_pw1, m_conv_w_dw=m_conv_w_dw, m_conv_b_dw=m_conv_b_dw, m_conv_norm_g=m_conv_norm_g, m_conv_w_pw2=m_conv_w_pw2, m_conv_b_pw2=m_conv_b_pw2, m_attn_w_qkv=m_attn_w_qkv, m_attn_q_norm=m_attn_q_norm, m_attn_k_norm=m_attn_k_norm, m_attn_w_o=m_attn_w_o, v_norm_g=v_norm_g, v_ffn_w_in=v_ffn_w_in, v_ffn_w_out=v_ffn_w_out, v_conv_w_pw1=v_conv_w_pw1, v_conv_b_pw1=v_conv_b_pw1, v_conv_w_dw=v_conv_w_dw, v_conv_b_dw=v_conv_b_dw, v_conv_norm_g=v_conv_norm_g, v_conv_w_pw2=v_conv_w_pw2, v_conv_b_pw2=v_conv_b_pw2, v_attn_w_qkv=v_attn_w_qkv, v_attn_q_norm=v_attn_q_norm, v_attn_k_norm=v_attn_k_norm, v_attn_w_o=v_attn_w_o)
    weights = {n: given[n] for n in TWIN_WEIGHTS}
    shared = {n: given[n] for n in SHARED_INPUTS}
    per_example = {n: given[n] for n in ['x']}
    grad_fn = _jax.value_and_grad(_loss, argnums=(0, 1))

    def one_microbatch(ex, loss_target):
        ex = dict(ex)
        diff = ex.pop(TWIN_DIFF_INPUT)
        return grad_fn(weights, diff, {**shared, **ex}, loss_target)

    if N_MICROBATCH == 1:
        loss, (grad_w, grad_x) = one_microbatch(per_example, given["loss_target"])
    else:
        def body(carry, xs):
            loss_sum, grad_sum = carry
            l_k, (gw_k, gx_k) = one_microbatch(xs[0], xs[1])
            with _jax.named_scope("update"):
                return (loss_sum + l_k, _jax.tree.map(_jnp.add, grad_sum, gw_k)), gx_k

        init = (_jnp.zeros((), _jnp.float32), _jax.tree.map(_jnp.zeros_like, weights))
        (loss, grad_w), grad_x = _jax.lax.scan(body, init, (per_example, given["loss_target"]))
    with _jax.named_scope("update"):
        delta_w, new_m, new_v = {}, {}, {}
        for n in TWIN_WEIGHTS:
            delta_w[n], new_m[n], new_v[n] = _adamw(weights[n], grad_w[n], given["m_" + n], given["v_" + n])
    return (loss, grad_x, *[grad_w[n] for n in TWIN_WEIGHTS], *[delta_w[n] for n in TWIN_WEIGHTS],
            *[new_m[n] for n in TWIN_WEIGHTS], *[new_v[n] for n in TWIN_WEIGHTS])
```

```python
import functools

import jax
import jax.numpy as jnp
from jax import lax
from jax.experimental import pallas as pl
from jax.experimental.pallas import tpu as pltpu

F32 = jnp.float32
BF16 = jnp.bfloat16
MESH = pl.DeviceIdType.MESH

NORM_EPS = 1e-6
CONV_WIDTH = 31
ATTN_GROUPS = ((128, 1), (512, 4), (2048, 16))
ATTN_BLOCK = 128
HEAD_DIM = 128
N_CHIPS = 4

ADAM_LR = 0.001
ADAM_B1 = 0.9
ADAM_B2 = 0.999
ADAM_EPS = 1e-08
ADAM_WD = 0.01
ADAM_STEP = 10

VMEM_LIMIT = 56 * 1024 * 1024
NT_DIMS = (((1,), (1,)), ((), ()))
TN_DIMS = (((0,), (0,)), ((), ()))


def _pick(n, pref, mult):
    t = (min(n, pref) // mult) * mult
    while t >= mult:
        if n % t == 0:
            return t
        t -= mult
    return n


def _call(body, name, grid, in_specs, out_specs, out_shape, scratch=(), aliases=None, prefetch=0):
    params = pltpu.CompilerParams(dimension_semantics=("arbitrary",) * len(grid), vmem_limit_bytes=VMEM_LIMIT)
    if prefetch:
        spec = pltpu.PrefetchScalarGridSpec(
            num_scalar_prefetch=prefetch, grid=grid, in_specs=in_specs, out_specs=out_specs, scratch_shapes=list(scratch)
        )
        return pl.pallas_call(body, name=name, grid_spec=spec, out_shape=out_shape, compiler_params=params,
                              input_output_aliases=aliases or {})
    return pl.pallas_call(body, name=name, grid=grid, in_specs=in_specs, out_specs=out_specs, out_shape=out_shape,
                          scratch_shapes=list(scratch), compiler_params=params, input_output_aliases=aliases or {})


def _sds(shape, dtype):
    return jax.ShapeDtypeStruct(shape, dtype)


def _sig(x):
    return 1.0 / (1.0 + jnp.exp(-x))


def _rstd(x):
    return lax.rsqrt(jnp.mean(x * x, axis=-1, keepdims=True) + NORM_EPS)


def _norm_bwd(dy, xhat, r, g):
    dxh = dy * g
    return r * (dxh - xhat * jnp.mean(dxh * xhat, axis=-1, keepdims=True))


def _dot(a, b):
    return jnp.dot(a, b, preferred_element_type=F32)


def _dot_nt(a, b):
    return lax.dot_general(a, b, NT_DIMS, preferred_element_type=F32)


def _dot_tn(a, b):
    return lax.dot_general(a, b, TN_DIMS, preferred_element_type=F32)


def _ffn_tile(F):
    return _pick(F, 256, 128)


def _ffn_fwd(x, g, w_in, w_out, lf, name):
    T, D = x.shape
    F = w_out.shape[1]
    tf = _ffn_tile(F)
    nf = F // tf
    tm = _pick(T, 1024, 8)

    def body(x_ref, g_ref, wg_ref, wu_ref, wo_ref, xo_ref, gu_ref, h_sc, acc_sc):
        j = pl.program_id(1)

        @pl.when(j == 0)
        def _():
            xv = x_ref[...]
            h_sc[...] = (xv * _rstd(xv) * g_ref[...]).astype(BF16)
            acc_sc[...] = jnp.zeros_like(acc_sc)

        h = h_sc[...]
        gate = _dot(h, wg_ref[...])
        up = _dot(h, wu_ref[...])
        gu_ref[:, :tf] = gate.astype(BF16)
        gu_ref[:, tf:] = up.astype(BF16)
        a = (gate * _sig(gate) * up).astype(BF16)
        acc_sc[...] += _dot(a, wo_ref[...])

        @pl.when(j == nf - 1)
        def _():
            xo_ref[...] = x_ref[...] + 0.5 * acc_sc[...]

    return _call(
        body, name, (T // tm, nf),
        [pl.BlockSpec((tm, D), lambda i, j: (i, 0)),
         pl.BlockSpec((1, D), lambda i, j: (0, 0)),
         pl.BlockSpec((None, D, tf), lambda i, j: (lf, 0, j)),
         pl.BlockSpec((None, D, tf), lambda i, j: (lf, 0, nf + j)),
         pl.BlockSpec((None, tf, D), lambda i, j: (lf, j, 0))],
        [pl.BlockSpec((tm, D), lambda i, j: (i, 0)),
         pl.BlockSpec((tm, 2 * tf), lambda i, j: (i, j))],
        [_sds((T, D), F32), _sds((T, 2 * F), BF16)],
        scratch=[pltpu.VMEM((tm, D), BF16), pltpu.VMEM((tm, D), F32)],
    )(x, g, w_in, w_in, w_out)


def _ffn_bwd(x, g, dy, gu, w_in, w_out, lf, name):
    T, D = x.shape
    F = w_out.shape[1]
    tf = _ffn_tile(F)
    nf = F // tf
    tm = _pick(T, 512, 8)

    def body(x_ref, g_ref, dy_ref, gu_ref, wg_ref, wu_ref, wo_ref, dx_ref, dg_ref, dgu_ref, a_ref, h_ref, dyb_ref, dh_sc):
        i = pl.program_id(0)
        j = pl.program_id(1)

        @pl.when(j == 0)
        def _():
            xv = x_ref[...]
            h_ref[...] = (xv * _rstd(xv) * g_ref[...]).astype(BF16)
            dyb_ref[...] = (0.5 * dy_ref[...]).astype(BF16)
            dh_sc[...] = jnp.zeros_like(dh_sc)

        @pl.when((i == 0) & (j == 0))
        def _():
            dg_ref[...] = jnp.zeros_like(dg_ref)

        gate = gu_ref[:, :tf].astype(F32)
        up = gu_ref[:, tf:].astype(F32)
        sg = _sig(gate)
        sl = gate * sg
        a_ref[...] = (sl * up).astype(BF16)
        da = _dot_nt(dyb_ref[...], wo_ref[...])
        dgate = (da * up * (sg * (1.0 + gate * (1.0 - sg)))).astype(BF16)
        dup = (da * sl).astype(BF16)
        dgu_ref[:, :tf] = dgate
        dgu_ref[:, tf:] = dup
        dh_sc[...] += _dot_nt(dgate, wg_ref[...]) + _dot_nt(dup, wu_ref[...])

        @pl.when(j == nf - 1)
        def _():
            xv = x_ref[...]
            r = _rstd(xv)
            xh = xv * r
            dh = dh_sc[...]
            dx_ref[...] = dy_ref[...] + _norm_bwd(dh, xh, r, g_ref[...])
            dg_ref[...] += jnp.sum(dh * xh, axis=0, keepdims=True)

    return _call(
        body, name, (T // tm, nf),
        [pl.BlockSpec((tm, D), lambda i, j: (i, 0)),
         pl.BlockSpec((1, D), lambda i, j: (0, 0)),
         pl.BlockSpec((tm, D), lambda i, j: (i, 0)),
         pl.BlockSpec((tm, 2 * tf), lambda i, j: (i, j)),
         pl.BlockSpec((None, D, tf), lambda i, j: (lf, 0, j)),
         pl.BlockSpec((None, D, tf), lambda i, j: (lf, 0, nf + j)),
         pl.BlockSpec((None, tf, D), lambda i, j: (lf, j, 0))],
        [pl.BlockSpec((tm, D), lambda i, j: (i, 0)),
         pl.BlockSpec((1, D), lambda i, j: (0, 0)),
         pl.BlockSpec((tm, 2 * tf), lambda i, j: (i, j)),
         pl.BlockSpec((tm, tf), lambda i, j: (i, j)),
         pl.BlockSpec((tm, D), lambda i, j: (i, 0)),
         pl.BlockSpec((tm, D), lambda i, j: (i, 0))],
        [_sds((T, D), F32), _sds((1, D), F32), _sds((T, 2 * F), BF16), _sds((T, F), BF16), _sds((T, D), BF16),
         _sds((T, D), BF16)],
        scratch=[pltpu.VMEM((tm, D), F32)],
    )(x, g, dy, gu, w_in, w_in, w_out)


def _mm_tn(a, b, bm, bn, out_shape, out_block, out_map, scale, name, prev=None):
    K, M = a.shape
    N = b.shape[1]

    def body(a_ref, b_ref, *rest):
        o_ref, ob_ref = rest[-2:]
        o = _dot_tn(a_ref[...].astype(BF16), b_ref[...].astype(BF16)) * scale
        o_ref[...] = o
        ob_ref[...] = o.astype(BF16)

    in_specs = [pl.BlockSpec((K, bm), lambda mi, ni: (0, mi)), pl.BlockSpec((K, bn), lambda mi, ni: (0, ni))]
    args = [a, b]
    aliases = None
    if prev is not None:
        in_specs += [pl.BlockSpec(memory_space=pl.ANY)] * 2
        args += list(prev)
        aliases = {2: 0, 3: 1}
    ospec = pl.BlockSpec(out_block, out_map)
    return _call(body, name, (M // bm, N // bn), in_specs, [ospec, ospec],
                 [_sds(out_shape, F32), _sds(out_shape, BF16)], aliases=aliases)(*args)


def _conv_pre(x, g, w1, b1, name):
    T, D = x.shape
    tm = _pick(T, 512, 8)

    def body(x_ref, g_ref, w_ref, b_ref, ag_ref, u_ref, h_ref):
        xv = x_ref[...]
        h = (xv * _rstd(xv) * g_ref[...]).astype(BF16)
        h_ref[...] = h
        ag = _dot(h, w_ref[...]) + b_ref[...]
        ag_ref[...] = ag.astype(BF16)
        u_ref[...] = ag[:, :D] * _sig(ag[:, D:])

    return _call(
        body, name, (T // tm,),
        [pl.BlockSpec((tm, D), lambda i: (i, 0)), pl.BlockSpec((1, D), lambda i: (0, 0)),
         pl.BlockSpec((D, 2 * D), lambda i: (0, 0)), pl.BlockSpec((1, 2 * D), lambda i: (0, 0))],
        [pl.BlockSpec((tm, 2 * D), lambda i: (i, 0)), pl.BlockSpec((tm, D), lambda i: (i, 0)),
         pl.BlockSpec((tm, D), lambda i: (i, 0))],
        [_sds((T, 2 * D), BF16), _sds((T, D), F32), _sds((T, D), BF16)],
    )(x, g, w1, b1)


_DW_PAD = 32
_DW_CHUNK = 256


def _dwconv(u, w, b, name):
    T, D = u.shape
    K = w.shape[0]
    ch = _pick(T, _DW_CHUNK, 8)
    lead = _DW_PAD - (K - 1)

    def body(u_ref, w_ref, b_ref, c_ref, ext):
        ext[pl.ds(0, _DW_PAD), :] = jnp.zeros((_DW_PAD, 128), F32)
        ext[pl.ds(_DW_PAD, T), :] = u_ref[...]
        for c0 in range(0, T, ch):
            acc = jnp.zeros((ch, 128), F32) + b_ref[...]
            for k in range(K):
                acc = acc + w_ref[pl.ds(k, 1), :] * ext[pl.ds(c0 + lead + k, ch), :]
            c_ref[pl.ds(c0, ch), :] = acc

    return _call(
        body, name, (D // 128,),
        [pl.BlockSpec((T, 128), lambda i: (0, i)), pl.BlockSpec((K, 128), lambda i: (0, i)),
         pl.BlockSpec((1, 128), lambda i: (0, i))],
        [pl.BlockSpec((T, 128), lambda i: (0, i))],
        [_sds((T, D), F32)],
        scratch=[pltpu.VMEM((T + _DW_PAD, 128), F32)],
    )(u, w, b)[0]


def _dwconv_bwd(dc, u, w, name):
    T, D = u.shape
    K = w.shape[0]
    ch = _pick(T, _DW_CHUNK, 8)
    lead = _DW_PAD - (K - 1)

    def body(dc_ref, u_ref, w_ref, du_ref, dw_ref, db_ref, uext, dext):
        uext[pl.ds(0, _DW_PAD), :] = jnp.zeros((_DW_PAD, 128), F32)
        uext[pl.ds(_DW_PAD, T), :] = u_ref[...]
        dext[pl.ds(0, T), :] = dc_ref[...]
        dext[pl.ds(T, _DW_PAD), :] = jnp.zeros((_DW_PAD, 128), F32)
        dws =[jnp.zeros((8, 128), F32) for _ in range(K)]
        dbs = jnp.zeros((8, 128), F32)
        for c0 in range(0, T, ch):
            dcv = dext[pl.ds(c0, ch), :]
            dbs = dbs + jnp.sum(dcv.reshape(ch // 8, 8, 128), axis=0)
            acc = jnp.zeros((ch, 128), F32)
            for k in range(K):
                acc = acc + w_ref[pl.ds(k, 1), :] * dext[pl.ds(c0 + (K - 1) - k, ch), :]
                prod = dcv * uext[pl.ds(c0 + lead + k, ch), :]
                dws[k] = dws[k] + jnp.sum(prod.reshape(ch // 8, 8, 128), axis=0)
            du_ref[pl.ds(c0, ch), :] = acc
        for k in range(K):
            dw_ref[pl.ds(k, 1), :] = jnp.sum(dws[k], axis=0, keepdims=True)
        db_ref[...] = jnp.sum(dbs, axis=0, keepdims=True)

    return _call(
        body, name, (D // 128,),
        [pl.BlockSpec((T, 128), lambda i: (0, i)), pl.BlockSpec((T, 128), lambda i: (0, i)),
         pl.BlockSpec((K, 128), lambda i: (0, i))],
        [pl.BlockSpec((T, 128), lambda i: (0, i)), pl.BlockSpec((K, 128), lambda i: (0, i)),
         pl.BlockSpec((1, 128), lambda i: (0, i))],
        [_sds((T, D), F32), _sds((K, D), F32), _sds((1, D), F32)],
        scratch=[pltpu.VMEM((T + _DW_PAD, 128), F32), pltpu.VMEM((T + _DW_PAD, 128), F32)],
    )(dc, u, w)


def _conv_post(c, x, ng, w2, b2, name):
    T, D = x.shape
    tm = _pick(T, 512, 8)

    def body(c_ref, x_ref, ng_ref, w_ref, b_ref, xo_ref, s_ref):
        cv = c_ref[...]
        n = cv * _rstd(cv) * ng_ref[...]
        s = (n * _sig(n)).astype(BF16)
        s_ref[...] = s
        xo_ref[...] = x_ref[...] + _dot(s, w_ref[...]) + b_ref[...]

    row = lambda i: (i, 0)
    fix = lambda i: (0, 0)
    return _call(
        body, name, (T // tm,),
        [pl.BlockSpec((tm, D), row), pl.BlockSpec((tm, D), row), pl.BlockSpec((1, D), fix),
         pl.BlockSpec((D, D), fix), pl.BlockSpec((1, D), fix)],
        [pl.BlockSpec((tm, D), row), pl.BlockSpec((tm, D), row)],
        [_sds((T, D), F32), _sds((T, D), BF16)],
    )(c, x, ng, w2, b2)


def _conv_post_bwd(dy, c, ng, w2, name):
    T, D = dy.shape
    tm = _pick(T, 512, 8)

    def body(dy_ref, c_ref, ng_ref, w_ref, dc_ref, dng_ref, db_ref):
        @pl.when(pl.program_id(0) == 0)
        def _():
            dng_ref[...] = jnp.zeros_like(dng_ref)
            db_ref[...] = jnp.zeros_like(db_ref)

        dyv = dy_ref[...]
        ds = _dot_nt(dyv.astype(BF16), w_ref[...])
        cv = c_ref[...]
        r = _rstd(cv)
        ch = cv * r
        n = ch * ng_ref[...]
        sg = _sig(n)
        dn = ds * (sg * (1.0 + n * (1.0 - sg)))
        dc_ref[...] = _norm_bwd(dn, ch, r, ng_ref[...])
        dng_ref[...] += jnp.sum(dn * ch, axis=0, keepdims=True)
        db_ref[...] += jnp.sum(dyv, axis=0, keepdims=True)

    row = lambda i: (i, 0)
    fix = lambda i: (0, 0)
    return _call(
        body, name, (T // tm,),
        [pl.BlockSpec((tm, D), row), pl.BlockSpec((tm, D), row), pl.BlockSpec((1, D), fix), pl.BlockSpec((D, D), fix)],
        [pl.BlockSpec((tm, D), row), pl.BlockSpec((1, D), fix), pl.BlockSpec((1, D), fix)],
        [_sds((T, D), F32), _sds((1, D), F32), _sds((1, D), F32)],
    )(dy, c, ng, w2)


def _conv_pre_bwd(du, ag, x, g, dy, w1, name):
    T, D = x.shape
    tm = _pick(T, 512, 8)

    def body(du_ref, ag_ref, x_ref, g_ref, dy_ref, w_ref, dx_ref, dg_ref, dag_ref, db_ref):
        @pl.when(pl.program_id(0) == 0)
        def _():
            dg_ref[...] = jnp.zeros_like(dg_ref)
            db_ref[...] = jnp.zeros_like(db_ref)

        duv = du_ref[...]
        a = ag_ref[:, :D].astype(F32)
        gt = ag_ref[:, D:].astype(F32)
        sg = _sig(gt)
        da = duv * sg
        dgt = duv * a * sg * (1.0 - sg)
        db_ref[:, :D] += jnp.sum(da, axis=0, keepdims=True)
        db_ref[:, D:] += jnp.sum(dgt, axis=0, keepdims=True)
        dab = da.astype(BF16)
        dgb = dgt.astype(BF16)
        dag_ref[:, :D] = dab
        dag_ref[:, D:] = dgb
        dh = _dot_nt(dab, w_ref[:, :D]) + _dot_nt(dgb, w_ref[:, D:])
        xv = x_ref[...]
        r = _rstd(xv)
        xh = xv * r
        dx_ref[...] = dy_ref[...] + _norm_bwd(dh, xh, r, g_ref[...])
        dg_ref[...] += jnp.sum(dh * xh, axis=0, keepdims=True)

    row = lambda i: (i, 0)
    fix = lambda i: (0, 0)
    return _call(
        body, name, (T // tm,),
        [pl.BlockSpec((tm, D), row), pl.BlockSpec((tm, 2 * D), row), pl.BlockSpec((tm, D), row), pl.BlockSpec((1, D), fix),
         pl.BlockSpec((tm, D), row), pl.BlockSpec((D, 2 * D), fix)],
        [pl.BlockSpec((tm, D), row), pl.BlockSpec((1, D), fix), pl.BlockSpec((tm, 2 * D), row),
         pl.BlockSpec((1, 2 * D), fix)],
        [_sds((T, D), F32), _sds((1, D), F32), _sds((T, 2 * D), BF16), _sds((1, 2 * D), F32)],
    )(du, ag, x, g, dy, w1)


def _attn_qkv(x, g, wqkv, name):
    T, D = x.shape
    N = wqkv.shape[1]
    tn = N // 9
    tm = _pick(T, 512, 8)

    def body(x_ref, g_ref, w_ref, o_ref, h_ref):
        @pl.when(pl.program_id(1) == 0)
        def _():
            xv = x_ref[...]
            h_ref[...] = (xv * _rstd(xv) * g_ref[...]).astype(BF16)

        o_ref[...] = _dot(h_ref[...], w_ref[...]).astype(BF16)

    return _call(
        body, name, (T // tm, 9),
        [pl.BlockSpec((tm, D), lambda i, j: (i, 0)), pl.BlockSpec((1, D), lambda i, j: (0, 0)),
         pl.BlockSpec((D, tn), lambda i, j: (0, j))],
        [pl.BlockSpec((tm, tn), lambda i, j: (i, j)), pl.BlockSpec((tm, D), lambda i, j: (i, 0))],
        [_sds((T, N), BF16), _sds((T, D), BF16)],
    )(x, g, wqkv)


def _band_masks(q, steps, nblk):
    i = lax.broadcasted_iota(jnp.int32, (q, q), 0)
    j = lax.broadcasted_iota(jnp.int32, (q, q), 1)
    cur = i - j >= 0
    if steps < q - 1:
        cur = cur & (i - j <= steps)
    prev = q + i - j <= jnp.where(nblk > 0, steps, -1)
    return cur, prev


def _attn_fwd(qkv, qn_g, kn_g, grp, window, dil, name):
    T, C = qkv.shape
    HE = C // 9
    H = HE // HEAD_DIM
    E = HEAD_DIM
    Q = ATTN_BLOCK
    L = T // dil
    nb = L // Q
    steps = window // dil
    scale = E ** -0.5
    view = qkv.reshape(L, dil * C)

    def body(q_ref, kc_ref, kp_ref, vc_ref, vp_ref, gq_ref, gk_ref, o_ref, l_ref):
        n = pl.program_id(1)
        cur, prev = _band_masks(Q, steps, n)
        gq = gq_ref[...]
        gk = gk_ref[...]
        for h in range(H):
            hs = slice(h * E, (h + 1) * E)
            q = q_ref[:, hs].astype(F32)
            kc = kc_ref[:, hs].astype(F32)
            kp = kp_ref[:, hs].astype(F32)
            qn = (q * _rstd(q) * gq).astype(BF16)
            kcn = (kc * _rstd(kc) * gk).astype(BF16)
            kpn = (kp * _rstd(kp) * gk).astype(BF16)
            sc = jnp.where(cur, _dot_nt(qn, kcn) * scale, -1e30)
            sp = jnp.where(prev, _dot_nt(qn, kpn) * scale, -1e30)
            m = jnp.maximum(jnp.max(sc, axis=-1, keepdims=True), jnp.max(sp, axis=-1, keepdims=True))
            pc = jnp.exp(sc - m)
            pp = jnp.exp(sp - m)
            l = jnp.sum(pc, axis=-1, keepdims=True) + jnp.sum(pp, axis=-1, keepdims=True)
            inv = 1.0 / l
            o = _dot((pc * inv).astype(BF16), vc_ref[:, hs]) + _dot((pp * inv).astype(BF16), vp_ref[:, hs])
            o_ref[:, hs] = o
            l_ref[:, hs] = jnp.broadcast_to(m + jnp.log(l), (Q, E))

    base = grp * 3
    blk = lambda s, back: pl.BlockSpec((Q, HE), lambda r, n: (jnp.maximum(n - back, 0), r * 9 + base + s))
    vec = pl.BlockSpec((1, E), lambda r, n: (0, 0))
    out = pl.BlockSpec((Q, HE), lambda r, n: (n, r))
    o, lse = _call(
        body, name, (dil, nb),
        [blk(0, 0), blk(1, 0), blk(1, 1), blk(2, 0), blk(2, 1), vec, vec],
        [out, out],
        [_sds((L, dil * HE), F32), _sds((L, dil * HE), F32)],
    )(view, view, view, view, view, qn_g, kn_g)
    return o.reshape(T, HE), lse.reshape(T, HE)


def _attn_merge(os, lses, x, wo, name):
    T, D = x.shape
    HE = wo.shape[0]
    tm = _pick(T, 512, 8)
    ng = len(os)

    def body(*refs):
        o_refs = refs[:ng]
        l_refs = refs[ng:2 * ng]
        x_ref, w_ref, xo_ref, om_ref, lt_ref = refs[2 * ng:]
        ls = [r[...] for r in l_refs]
        m = functools.reduce(jnp.maximum, ls)
        es = [jnp.exp(l - m) for l in ls]
        tot = functools.reduce(lambda a, b: a + b, es)
        inv = 1.0 / tot
        om = functools.reduce(lambda a, b: a + b, [e * inv * r[...] for e, r in zip(es, o_refs)])
        omb = om.astype(BF16)
        om_ref[...] = omb
        lt_ref[...] = m + jnp.log(tot)
        xo_ref[...] = x_ref[...] + _dot(omb, w_ref[...])

    row = lambda i: (i, 0)
    fix = lambda i: (0, 0)
    return _call(
        body, name, (T // tm,),
        [pl.BlockSpec((tm, HE), row)] * (2 * ng) + [pl.BlockSpec((tm, D), row), pl.BlockSpec((HE, D), fix)],
        [pl.BlockSpec((tm, D), row), pl.BlockSpec((tm, HE), row), pl.BlockSpec((tm, HE), row)],
        [_sds((T, D), F32), _sds((T, HE), BF16), _sds((T, HE), F32)],
    )(*os, *lses, x, wo)


def _attn_out_bwd(dy, om, wo, name):
    T, D = dy.shape
    HE = wo.shape[0]
    E = HEAD_DIM
    tm = _pick(T, 512, 8)

    def body(dy_ref, om_ref, w_ref, dom_ref, dl_ref):
        dom = _dot_nt(dy_ref[...].astype(BF16), w_ref[...])
        dom_ref[...] = dom.astype(BF16)
        prod = dom * om_ref[...].astype(F32)
        for h in range(HE // E):
            hs = slice(h * E, (h + 1) * E)
            dl_ref[:, hs] = jnp.broadcast_to(jnp.sum(prod[:, hs], axis=-1, keepdims=True), (tm, E))

    row = lambda i: (i, 0)
    return _call(
        body, name, (T // tm,),
        [pl.BlockSpec((tm, D), row), pl.BlockSpec((tm, HE), row), pl.BlockSpec((HE, D), lambda i: (0, 0))],
        [pl.BlockSpec((tm, HE), row), pl.BlockSpec((tm, HE), row)],
        [_sds((T, HE), BF16), _sds((T, HE), F32)],
    )(dy, om, wo)


def _attn_bwd(qkv, dom, lse, delta, qn_g, kn_g, grp, window, dil, name):
    T, C = qkv.shape
    HE = C // 9
    H = HE // HEAD_DIM
    E = HEAD_DIM
    Q = ATTN_BLOCK
    L = T // dil
    nb = L // Q
    steps = window // dil
    scale = E ** -0.5
    view = qkv.reshape(L, dil * C)
    sview = lambda a: a.reshape(L, dil * HE)

    def body(q_ref, kc_ref, kp_ref, vc_ref, vp_ref, do_ref, l_ref, dl_ref, gq_ref, gk_ref,
             dq_ref, dk_ref, dv_ref, dgq_ref, dgk_ref, ck_sc, cv_sc):
        r_id = pl.program_id(0)
        n = pl.program_id(1)

        @pl.when((r_id == 0) & (n == 0))
        def _():
            dgq_ref[...] = jnp.zeros_like(dgq_ref)
            dgk_ref[...] = jnp.zeros_like(dgk_ref)

        @pl.when(n == 0)
        def _():
            ck_sc[...] = jnp.zeros_like(ck_sc)
            cv_sc[...] = jnp.zeros_like(cv_sc)

        @pl.when(n < nb)
        def _():
            cur, prev = _band_masks(Q, steps, n)
            gq = gq_ref[...]
            gk = gk_ref[...]
            dgq = jnp.zeros((1, E), F32)
            dgk = jnp.zeros((1, E), F32)
            for h in range(H):
                hs = slice(h * E, (h + 1) * E)
                q = q_ref[:, hs].astype(F32)
                kc = kc_ref[:, hs].astype(F32)
                kp = kp_ref[:, hs].astype(F32)
                rq, rc, rp = _rstd(q), _rstd(kc), _rstd(kp)
                qh, kch, kph = q * rq, kc * rc, kp * rp
                qn = (qh * gq).astype(BF16)
                kcn = (kch * gk).astype(BF16)
                kpn = (kph * gk).astype(BF16)
                lcol = l_ref[:, hs][:, :1]
                dcol = dl_ref[:, hs][:, :1]
                do = do_ref[:, hs]
                pc = jnp.where(cur, jnp.exp(_dot_nt(qn, kcn) * scale - lcol), 0.0)
                pp = jnp.where(prev, jnp.exp(_dot_nt(qn, kpn) * scale - lcol), 0.0)
                dsc = (pc * (_dot_nt(do, vc_ref[:, hs]) - dcol) * scale).astype(BF16)
                dsp = (pp * (_dot_nt(do, vp_ref[:, hs]) - dcol) * scale).astype(BF16)
                dvc = _dot_tn(pc.astype(BF16), do)
                dvp = _dot_tn(pp.astype(BF16), do)
                dqn = _dot(dsc, kcn) + _dot(dsp, kpn)
                dkcn = _dot_tn(dsc, qn)
                dkpn = _dot_tn(dsp, qn)
                dq_ref[:, hs] = _norm_bwd(dqn, qh, rq, gq).astype(BF16)
                dgq = dgq + jnp.sum(dqn * qh, axis=0, keepdims=True)
                dgk = dgk + jnp.sum(dkcn * kch + dkpn * kph, axis=0, keepdims=True)
                dk_ref[:, hs] = (ck_sc[:, hs] + _norm_bwd(dkpn, kph, rp, gk)).astype(BF16)
                dv_ref[:, hs] = (cv_sc[:, hs] + dvp).astype(BF16)
                ck_sc[:, hs] = _norm_bwd(dkcn, kch, rc, gk)
                cv_sc[:, hs] = dvc
            dgq_ref[...] += dgq
            dgk_ref[...] += dgk

        @pl.when(n == nb)
        def _():
            dk_ref[...] = ck_sc[...].astype(BF16)
            dv_ref[...] = cv_sc[...].astype(BF16)

    base = grp * 3
    nq = lambda n: jnp.minimum(n, nb - 1)
    blk = lambda s, back: pl.BlockSpec((Q, HE), lambda r, n: (jnp.maximum(nq(n) - back, 0), r * 9 + base + s))
    qblk = pl.BlockSpec((Q, HE), lambda r, n: (nq(n), r))
    kblk = pl.BlockSpec((Q, HE), lambda r, n: (jnp.maximum(n - 1, 0), r))
    vec = pl.BlockSpec((1, E), lambda r, n: (0, 0))
    dq, dk, dv, dgq, dgk = _call(
        body, name, (dil, nb + 1),
        [blk(0, 0), blk(1, 0), blk(1, 1), blk(2, 0), blk(2, 1), qblk, qblk, qblk, vec, vec],
        [qblk, kblk, kblk, vec, vec],
        [_sds((L, dil * HE), BF16)] * 3 + [_sds((1, E), F32)] * 2,
        scratch=[pltpu.VMEM((Q, HE), F32), pltpu.VMEM((Q, HE), F32)],
    )(view, view, view, view, view, sview(dom), sview(lse), sview(delta), qn_g, kn_g)
    return dq.reshape(T, HE), dk.reshape(T, HE), dv.reshape(T, HE), dgq, dgk


def _attn_qkv_bwd(dqkv, x, g, dy, wqkv, name):
    T, D = x.shape
    HE = wqkv.shape[1] // 9
    tm = _pick(T, 512, 8)

    def body(*refs):
        d_refs = refs[:9]
        x_ref, g_ref, dy_ref, w_ref, dx_ref, dg_ref, dh_sc = refs[9:]
        i = pl.program_id(0)
        j = pl.program_id(1)

        @pl.when((i == 0) & (j == 0))
        def _():
            dg_ref[...] = jnp.zeros_like(dg_ref)

        @pl.when(j == 0)
        def _():
            dh_sc[...] = jnp.zeros_like(dh_sc)

        for s in range(9):
            @pl.when(j == s)
            def _(s=s):
                dh_sc[...] += _dot_nt(d_refs[s][...], w_ref[...])

        @pl.when(j == 8)
        def _():
            xv = x_ref[...]
            r = _rstd(xv)
            xh = xv * r
            dh = dh_sc[...]
            dx_ref[...] = dy_ref[...] + _norm_bwd(dh, xh, r, g_ref[...])
            dg_ref[...] += jnp.sum(dh * xh, axis=0, keepdims=True)

    row = lambda i, j: (i, 0)
    fix = lambda i, j: (0, 0)
    return _call(
        body, name, (T // tm, 9),
        [pl.BlockSpec((tm, HE), row)] * 9 + [pl.BlockSpec((tm, D), row), pl.BlockSpec((1, D), fix), pl.BlockSpec((tm, D), row),
                                           pl.BlockSpec((D, HE), lambda i, j: (0, j))],
        [pl.BlockSpec((tm, D), row), pl.BlockSpec((1, D), fix)],
        [_sds((T, D), F32), _sds((1, D), F32)],
        scratch=[pltpu.VMEM((tm, D), F32)],
    )(*dqkv, x, g, dy, wqkv)


def _loss_head(y, target, name):
    T, D = y.shape
    tm = _pick(T, 512, 8)

    def body(y_ref, t_ref, dy_ref, sq_ref):
        @pl.when(pl.program_id(0) == 0)
        def _():
            sq_ref[...] = jnp.zeros_like(sq_ref)

        err = y_ref[...] - t_ref[...]
        dy_ref[...] = err * (1.0 / D)
        sq_ref[...] += jnp.sum(err * err, axis=0, keepdims=True)

    row = lambda i: (i, 0)
    return _call(
        body, name, (T // tm,),
        [pl.BlockSpec((tm, D), row), pl.BlockSpec((tm, D), row)],
        [pl.BlockSpec((tm, D), row), pl.BlockSpec((1, D), lambda i: (0, 0))],
        [_sds((T, D), F32), _sds((1, D), F32)],
    )(y, target)


def _local_step(x, target, norm_g, w_in, w_out, pw1, b_pw1, w_dw, b_dw, cng, pw2, b_pw2, wqkv, qn, kn, wo):
    T, D = x.shape
    F = w_out.shape[1]
    HE = wo.shape[0]
    ng = lambda l, k: norm_g[l, k][None, :]
    tf = _ffn_tile(F)
    nf = F // tf
    bn = _pick(D, 256, 128)
    bh = _pick(HE, 512, 128)

    x1, gu0 = _ffn_fwd(x, ng(0, 0), w_in, w_out, 0, "ffn_fwd_0")
    ag, u, hc = _conv_pre(x1, ng(0, 1), pw1, b_pw1, "conv_pre")
    c = _dwconv(u, w_dw, b_dw, "dwconv")
    x2, s = _conv_post(c, x1, cng, pw2, b_pw2, "conv_post")
    x3, gu1 = _ffn_fwd(x2, ng(0, 2), w_in, w_out, 1, "ffn_fwd_1")
    x4, gu2 = _ffn_fwd(x3, ng(1, 0), w_in, w_out, 2, "ffn_fwd_2")
    qkv, ha = _attn_qkv(x4, ng(1, 1), wqkv, "attn_qkv")
    os, lses = [], []
    for gi, (window, dil) in enumerate(ATTN_GROUPS):
        o, l = _attn_fwd(qkv, qn[gi][None, :], kn[gi][None, :], gi, window, dil, f"attn_fwd_{gi}")
        os.append(o)
        lses.append(l)
    x5, om, lse = _attn_merge(os, lses, x4, wo, "attn_merge")
    x6, gu3 = _ffn_fwd(x5, ng(1, 2), w_in, w_out, 3, "ffn_fwd_3")
    dy, sq = _loss_head(x6, target, "loss_head")

    grads = {}
    dnorm = [[None] * 3 for _ in range(2)]
    d_in = d_out = None

    def ffn_back(lf, xin, gvec, dy, gu, d_in, d_out):
        dx, dg, dgu, a, h, dyb = _ffn_bwd(xin, gvec, dy, gu, w_in, w_out, lf, f"ffn_bwd_{lf}")
        d_in = _mm_tn(h, dgu, D, tf, (4, D, 2 * F), (None, D, tf), lambda mi, ni: (lf, 0, (ni % 2) * nf + ni // 2), 1.0,
                      f"ffn_dw_in_{lf}", prev=d_in)
        d_out = _mm_tn(a, dyb, tf, D, (4, F, D), (None, tf, D), lambda mi, ni: (lf, mi, 0), 1.0,
                       f"ffn_dw_out_{lf}", prev=d_out)
        return dx, dg, d_in, d_out

    dx, dnorm[1][2], d_in, d_out = ffn_back(3, x5, ng(1, 2), dy, gu3, d_in, d_out)
    dom, delta = _attn_out_bwd(dx, om, wo, "attn_out_bwd")
    grads["attn_w_o"] = _mm_tn(om, dx, HE, bn, (HE, D), (HE, bn), lambda mi, ni: (0, ni), 1.0, "attn_dw_o")
    dqkv, dgq, dgk = [], [], []
    for gi, (window, dil) in enumerate(ATTN_GROUPS):
        dq, dk, dv, a_, b_ = _attn_bwd(qkv, dom, lse, delta, qn[gi][None, :], kn[gi][None, :], gi, window, dil, f"attn_bwd_{gi}")
        dqkv += [dq, dk, dv]
        dgq.append(a_)
        dgk.append(b_)
    grads["attn_q_norm"] = jnp.concatenate(dgq, axis=0)
    grads["attn_k_norm"] = jnp.concatenate(dgk, axis=0)
    d_qkv = None
    for s9 in range(9):
        d_qkv = _mm_tn(ha, dqkv[s9], D, bh, (D, 9 * HE), (D, bh), lambda mi, ni, s9=s9: (0, s9 * (HE // bh) + ni), 1.0,
                       f"attn_dw_qkv_{s9}", prev=d_qkv)
    grads["attn_w_qkv"] = d_qkv
    dx, dnorm[1][1] = _attn_qkv_bwd(dqkv, x4, ng(1, 1), dx, wqkv, "attn_qkv_bwd")
    dx, dnorm[1][0], d_in, d_out = ffn_back(2, x3, ng(1, 0), dx, gu2, d_in, d_out)

    dx, dnorm[0][2], d_in, d_out = ffn_back(1, x2, ng(0, 2), dx, gu1, d_in, d_out)
    dc, grads["conv_norm_g"], grads["conv_b_pw2"] = _conv_post_bwd(dx, c, cng, pw2, "conv_post_bwd")
    grads["conv_w_pw2"] = _mm_tn(s, dx, D, bn, (D, D), (D, bn), lambda mi, ni: (0, ni), 1.0, "conv_dw_pw2")
    du, grads["conv_w_dw"], grads["conv_b_dw"] = _dwconv_bwd(dc, u, w_dw, "dwconv_bwd")
    dx, dnorm[0][1], dag, grads["conv_b_pw1"] = _conv_pre_bwd(du, ag, x1, ng(0, 1), dx, pw1, "conv_pre_bwd")
    grads["conv_w_pw1"] = _mm_tn(hc, dag, D, 2 * bn, (D, 2 * D), (D, 2 * bn), lambda mi, ni: (0, ni), 1.0, "conv_dw_pw1")
    dx, dnorm[0][0], d_in, d_out = ffn_back(0, x, ng(0, 0), dx, gu0, d_in, d_out)

    grads["ffn_w_in"] = d_in
    grads["ffn_w_out"] = d_out
    grads["norm_g"] = jnp.concatenate([jnp.concatenate(r, axis=0)[None] for r in dnorm], axis=0)
    return sq, dx, grads


class _Sharded:
    def __init__(self, name, full3, half_axis, shard_axis):
        self.name, self.full3, self.half_axis, self.shard_axis = name, tuple(full3), half_axis, shard_axis

    def _cut(self, shape, axis, parts):
        s = list(shape)
        s[axis] //= parts
        return tuple(s)

    @property
    def shard3(self):
        return self._cut(self.full3, self.shard_axis, N_CHIPS)

    @property
    def pair3(self):
        return self._cut(self.full3, self.half_axis, 2)

    @property
    def part3(self):
        return self._cut(self.shard3, self.half_axis, 2)

    @staticmethod
    def _slice(ref, axis, idx, parts):
        n = ref.shape[axis] // parts
        start = idx * n
        minor = len(ref.shape) - 1 - axis
        if minor < 2 and not isinstance(start, int):
            start = pl.multiple_of(start, 128 if minor == 0 else (16 if n % 16 == 0 else 8))
        sl = [slice(None)] * len(ref.shape)
        sl[axis] = pl.ds(start, n)
        return ref.at[tuple(sl)]

    def half(self, ref, h):
        return self._slice(ref, self.half_axis, h, 2)

    def shard(self, ref, j):
        return self._slice(ref, self.shard_axis, j, N_CHIPS)


def _place():
    x, y, c = lax.axis_index("x"), lax.axis_index("y"), lax.axis_index("c")
    return x, y, c, 2 * x + y


_RELS = (1, 2, 3)


def _peer(x, y, rel):
    px = 1 - x if rel & 2 else x
    py = 1 - y if rel & 1 else y
    return px, py, 2 * px + py


ANY = pl.BlockSpec(memory_space=pl.ANY)


def _comm_call(body, name, n_in, out_shape, n_sems, aliases=None):
    return pl.pallas_call(
        body, name=name, in_specs=[ANY] * n_in, out_specs=[ANY] * len(out_shape), out_shape=out_shape,
        scratch_shapes=[pltpu.SemaphoreType.DMA((n,)) for n in n_sems],
        input_output_aliases=aliases or {},
        compiler_params=pltpu.CompilerParams(has_side_effects=True),
    )


def _gather_weights(descs, shards, small_shards):
    nb, ns = len(descs), len(small_shards)

    def body(*refs):
        ins, outs = refs[:nb + ns], refs[nb + ns:2 * (nb + ns)]
        lsem, ssem, rsem, fsem, gsem = refs[2 * (nb + ns):]
        x, y, c, me = _place()
        sib = (x, y, 1 - c)

        def remote(src, dst, s, r, dev):
            return pltpu.make_async_remote_copy(src_ref=src, dst_ref=dst, send_sem=s, recv_sem=r, device_id=dev,
                                                device_id_type=MESH)

        def small_cols(ref, j):
            return _Sharded._slice(ref, 1, j, N_CHIPS)

        local, sends, fwds = [], [], []
        for i, d in enumerate(descs):
            local.append(pltpu.make_async_copy(ins[i], d.shard(outs[i], me), lsem.at[i]))
        for i in range(ns):
            local.append(pltpu.make_async_copy(ins[nb + i], small_cols(outs[nb + i], me), lsem.at[nb + i]))
        for cp in local:
            cp.start()
        for i, d in enumerate(descs):
            for k, rel in enumerate(_RELS):
                px, py, _ = _peer(x, y, rel)
                sends.append(remote(d.half(ins[i], c), d.half(d.shard(outs[i], me), c), ssem.at[3 * i + k], rsem.at[3 * i + k], (px, py, c)))
        for i in range(ns):
            for k, rel in enumerate(_RELS):
                px, py, _ = _peer(x, y, rel)
                j = 3 * (nb + i) + k
                sends.append(remote(ins[nb + i], small_cols(outs[nb + i], me), ssem.at[j], rsem.at[j], (px, py, c)))
        for cp in sends:
            cp.start()
        for i, d in enumerate(descs):
            for k, rel in enumerate(_RELS):
                _, _, pj = _peer(x, y, rel)
                got = d.half(d.shard(outs[i], pj), c)
                remote(got, got, ssem.at[3 * i + k], rsem.at[3 * i + k], sib).wait_recv()
                f = remote(got, got, fsem.at[3 * i + k], gsem.at[3 * i + k], sib)
                f.start()
                fwds.append(f)
        for i in range(ns):
            for k, rel in enumerate(_RELS):
                _, _, pj = _peer(x, y, rel)
                j = 3 * (nb + i) + k
                got = small_cols(outs[nb + i], pj)
                remote(got, got, ssem.at[j], rsem.at[j], sib).wait_recv()
        for i, d in enumerate(descs):
            for k, rel in enumerate(_RELS):
                _, _, pj = _peer(x, y, rel)
                got = d.half(d.shard(outs[i], pj), 1 - c)
                remote(got, got, fsem.at[3 * i + k], gsem.at[3 * i + k], sib).wait_recv()
        for cp in sends + fwds:
            cp.wait_send()
        for cp in local:
            cp.wait()

    out_shape = [_sds(d.full3, BF16) for d in descs] + [_sds((s.shape[0], s.shape[1] * N_CHIPS), F32) for s in small_shards]
    n = 3 * (nb + ns)
    return _comm_call(body, "gather_weights", nb + ns, out_shape, [nb + ns, n, n, 3 * nb, 3 * nb])(*shards, *small_shards)


def _pair_send_halves(descs, fulls):
    nb = len(descs)

    def body(*refs):
        ins, outs = refs[:nb], refs[nb:2 * nb]
        ssem, rsem = refs[2 * nb:]
        x, y, c, _ = _place()
        cps = [pltpu.make_async_remote_copy(src_ref=d.half(ins[i], 1 - c), dst_ref=outs[i], send_sem=ssem.at[i], recv_sem=rsem.at[i],
                                            device_id=(x, y, 1 - c), device_id_type=MESH) for i, d in enumerate(descs)]
        for cp in cps:
            cp.start()
        for cp in cps:
            cp.wait_recv()
        for cp in cps:
            cp.wait_send()

    return _comm_call(body, "grads_pair_exchange", nb, [_sds(d.pair3, BF16) for d in descs], [nb, nb])(*fulls)


def _chip_send_parts(descs, pairs):
    nb = len(descs)

    def body(*refs):
        ins, outs = refs[:nb], refs[nb:2 * nb]
        ssem, rsem = refs[2 * nb:]
        x, y, c, _ = _place()
        cps = []
        for i, d in enumerate(descs):
            for k, rel in enumerate(_RELS):
                px, py, pj = _peer(x, y, rel)
                cps.append(pltpu.make_async_remote_copy(src_ref=d.shard(ins[i], pj), dst_ref=outs[i].at[k], send_sem=ssem.at[3 * i + k],
                                                        recv_sem=rsem.at[3 * i + k], device_id=(px, py, c), device_id_type=MESH))
        for cp in cps:
            cp.start()
        for cp in cps:
            cp.wait_recv()
        for cp in cps:
            cp.wait_send()

    return _comm_call(body, "grads_chip_exchange", nb, [_sds((3,) + d.part3, BF16) for d in descs], [3 * nb, 3 * nb])(*pairs)


def _pair_fill(descs, shards):
    nb = len(descs)

    def body(*refs):
        outs = refs[nb:2 * nb]
        ssem, rsem = refs[2 * nb:]
        x, y, c, _ = _place()
        cps = [pltpu.make_async_remote_copy(src_ref=d.half(outs[i], c), dst_ref=d.half(outs[i], c), send_sem=ssem.at[i], recv_sem=rsem.at[i],
                                            device_id=(x, y, 1 - c), device_id_type=MESH) for i, d in enumerate(descs)]
        for cp in cps:
            cp.start()
        for i, d in enumerate(descs):
            got = d.half(outs[i], 1 - c)
            pltpu.make_async_remote_copy(src_ref=got, dst_ref=got, send_sem=ssem.at[i], recv_sem=rsem.at[i], device_id=(x, y, 1 - c),
                                         device_id_type=MESH).wait_recv()
        for cp in cps:
            cp.wait_send()

    return _comm_call(body, "grads_pair_fill", nb, [_sds(d.shard3, F32) for d in descs], [nb, nb],
                      aliases={i: i for i in range(nb)})(*shards)


def _ew_tiles(d):
    _, rows, cols = d.part3
    return _pick(rows, 256, 16), cols


def _pair_add(d, g_full, got, scal):
    tr, tc = _ew_tiles(d)
    a_n, r_n, c_n = d.pair3
    ha = d.half_axis

    def body(sc_ref, g_ref, r_ref, o_ref, ob_ref):
        s = g_ref[...] + r_ref[...].astype(F32)
        o_ref[...] = s
        ob_ref[...] = s.astype(BF16)

    blk = (1, tr, tc)
    same = pl.BlockSpec(blk, lambda a, rb, cb, sc: (a, rb, cb))
    mine = pl.BlockSpec(blk, lambda a, rb, cb, sc: (a + sc[0] * (a_n if ha == 0 else 0), rb + sc[0] * (r_n // tr if ha == 1 else 0), cb))
    return _call(body, f"pair_add_{d.name}", (a_n, r_n // tr, c_n // tc), [mine, same], [same, same],
                 [_sds(d.pair3, F32), _sds(d.pair3, BF16)], prefetch=1)(scal, g_full, got)


def _chip_reduce(d, pair_f32, got, scal):
    tr, tc = _ew_tiles(d)
    a_n, r_n, c_n = d.part3
    ha, sa = d.half_axis, d.shard_axis

    def body(sc_ref, p_ref, r0, r1, r2, o_ref):
        o_ref[...] = ((p_ref[...] + r0[...].astype(F32)) + r1[...].astype(F32)) + r2[...].astype(F32)

    blk = (1, tr, tc)
    own = pl.BlockSpec(blk, lambda a, rb, sc: (a + sc[1] * (a_n if sa == 0 else 0), rb + sc[1] * (r_n // tr if sa == 1 else 0),
                                               sc[1] if sa == 2 else 0))
    slot = lambda k: pl.BlockSpec((None,) + blk, lambda a, rb, sc: (k, a, rb, 0))
    out = pl.BlockSpec(blk, lambda a, rb, sc: (a + sc[0] * (a_n if ha == 0 else 0), rb + sc[0] * (r_n // tr if ha == 1 else 0), 0))
    return _call(body, f"chip_reduce_{d.name}", (a_n, r_n // tr), [own, slot(0), slot(1), slot(2)], [out],
                 [_sds(d.shard3, F32)], prefetch=1)(scal, pair_f32, got, got, got)[0]


def _adam_math(g, w, m, v):
    m = ADAM_B1 * m + (1.0 - ADAM_B1) * g
    v = ADAM_B2 * v + (1.0 - ADAM_B2) * (g * g)
    m_hat = m / (1.0 - ADAM_B1 ** ADAM_STEP)
    v_hat = v / (1.0 - ADAM_B2 ** ADAM_STEP)
    delta = -ADAM_LR * (m_hat / (jnp.sqrt(v_hat) + ADAM_EPS) + ADAM_WD * w)
    return delta, m, v


def _adam(d, g, w, m, v):
    tr, tc = _ew_tiles(d)
    a_n, r_n, c_n = d.shard3

    def body(g_ref, w_ref, m_ref, v_ref, d_ref, mo_ref, vo_ref):
        d_ref[...], mo_ref[...], vo_ref[...] = _adam_math(g_ref[...], w_ref[...], m_ref[...], v_ref[...])

    spec = pl.BlockSpec((1, tr, tc), lambda a, rb: (a, rb, 0))
    return _call(body, f"adam_{d.name}", (a_n, r_n // tr), [spec] * 4, [spec] * 3, [_sds(d.shard3, F32)] * 3)(g, w, m, v)


def _adam_small(gs, ws, ms, vs):
    n = len(gs)

    def body(*refs):
        for i in range(n):
            g, w, m, v = (refs[k * n + i][...] for k in range(4))
            d, mo, vo = _adam_math(g, w, m, v)
            refs[4 * n + i][...] = d
            refs[5 * n + i][...] = mo
            refs[6 * n + i][...] = vo

    vm = pl.BlockSpec(memory_space=pltpu.VMEM)
    outs = pl.pallas_call(body, name="adam_small", in_specs=[vm] * (4 * n), out_specs=[vm] * (3 * n),
                          out_shape=[_sds(g.shape, F32) for g in gs] * 3)(*gs, *ws, *ms, *vs)
    return outs[:n], outs[n:2 * n], outs[2 * n:]


def _allreduce_small(packed):
    rows, cols = packed.shape
    others = [(dx, dy, dc) for dx in (0, 1) for dy in (0, 1) for dc in (0, 1) if (dx, dy, dc) != (0, 0, 0)]

    def body(in_ref, out_ref, buf, ssem, rsem):
        x, y, c, _ = _place()
        lin = 4 * x + 2 * y + c
        buf[lin] = in_ref[...]
        cps = []
        for k, (dx, dy, dc) in enumerate(others):
            px = 1 - x if dx else x
            py = 1 - y if dy else y
            pc = 1 - c if dc else c
            cps.append((pltpu.make_async_remote_copy(src_ref=in_ref, dst_ref=buf.at[lin], send_sem=ssem.at[k], recv_sem=rsem.at[k],
                                                     device_id=(px, py, pc), device_id_type=MESH), 4 * px + 2 * py + pc))
        for cp, _ in cps:
            cp.start()
        for k, (cp, plin) in enumerate(cps):
            pltpu.make_async_remote_copy(src_ref=in_ref, dst_ref=buf.at[plin], send_sem=ssem.at[k], recv_sem=rsem.at[k],
                                         device_id=(x, y, c), device_id_type=MESH).wait_recv()
        for cp, _ in cps:
            cp.wait_send()
        acc = buf[0]
        for dev in range(1, 8):
            acc = acc + buf[dev]
        out_ref[...] = acc

    vm = pl.BlockSpec(memory_space=pltpu.VMEM)
    return pl.pallas_call(
        body, name="allreduce_small", in_specs=[vm], out_specs=vm, out_shape=_sds((rows, cols), F32),
        scratch_shapes=[pltpu.VMEM((8, rows, cols), F32), pltpu.SemaphoreType.DMA((7,)), pltpu.SemaphoreType.DMA((7,))],
        compiler_params=pltpu.CompilerParams(has_side_effects=True),
    )(packed)


_BIG = ("ffn_w_in", "ffn_w_out", "conv_w_pw1", "conv_w_pw2", "attn_w_qkv", "attn_w_o")
_SMALL = ("norm_g", "conv_b_pw1", "conv_w_dw", "conv_b_dw", "conv_norm_g", "conv_b_pw2", "attn_q_norm", "attn_k_norm")
_NAMES = ("norm_g", "ffn_w_in", "ffn_w_out", "conv_w_pw1", "conv_b_pw1", "conv_w_dw", "conv_b_dw", "conv_norm_g", "conv_w_pw2",
          "conv_b_pw2", "attn_w_qkv", "attn_q_norm", "attn_k_norm", "attn_w_o")


def _rows8(a, width):
    a = a.reshape(-1, min(a.shape[-1], width))
    return jnp.pad(a, ((0, -a.shape[0] % 8), (0, width - a.shape[1])))


def kernel(x, norm_g, ffn_w_in, ffn_w_out, conv_w_pw1, conv_b_pw1, conv_w_dw, conv_b_dw, conv_norm_g, conv_w_pw2, conv_b_pw2, attn_w_qkv, attn_q_norm, attn_k_norm, attn_w_o, loss_target, m_norm_g, m_ffn_w_in, m_ffn_w_out, m_conv_w_pw1, m_conv_b_pw1, m_conv_w_dw, m_conv_b_dw, m_conv_norm_g, m_conv_w_pw2, m_conv_b_pw2, m_attn_w_qkv, m_attn_q_norm, m_attn_k_norm, m_attn_w_o, v_norm_g, v_ffn_w_in, v_ffn_w_out, v_conv_w_pw1, v_conv_b_pw1, v_conv_w_dw, v_conv_b_dw, v_conv_norm_g, v_conv_w_pw2, v_conv_b_pw2, v_attn_w_qkv, v_attn_q_norm, v_attn_k_norm, v_attn_w_o):
    w = dict(zip(_NAMES, (norm_g, ffn_w_in, ffn_w_out, conv_w_pw1, conv_b_pw1, conv_w_dw, conv_b_dw, conv_norm_g, conv_w_pw2,
                          conv_b_pw2, attn_w_qkv, attn_q_norm, attn_k_norm, attn_w_o)))
    m = dict(zip(_NAMES, (m_norm_g, m_ffn_w_in, m_ffn_w_out, m_conv_w_pw1, m_conv_b_pw1, m_conv_w_dw, m_conv_b_dw, m_conv_norm_g,
                          m_conv_w_pw2, m_conv_b_pw2, m_attn_w_qkv, m_attn_q_norm, m_attn_k_norm, m_attn_w_o)))
    v = dict(zip(_NAMES, (v_norm_g, v_ffn_w_in, v_ffn_w_out, v_conv_w_pw1, v_conv_b_pw1, v_conv_w_dw, v_conv_b_dw, v_conv_norm_g,
                          v_conv_w_pw2, v_conv_b_pw2, v_attn_w_qkv, v_attn_q_norm, v_attn_k_norm, v_attn_w_o)))
    T, D = x.shape[1:]
    F = ffn_w_out.shape[2] * N_CHIPS
    HE = attn_w_o.shape[1] * N_CHIPS
    cx, cy, cc = lax.axis_index("x"), lax.axis_index("y"), lax.axis_index("c")
    me = 2 * cx + cy
    scal = jnp.stack([cc, me]).astype(jnp.int32)

    descs = {
        "ffn_w_in": _Sharded("ffn_w_in", (4, D, 2 * F), 0, 2),
        "ffn_w_out": _Sharded("ffn_w_out", (4, F, D), 0, 1),
        "conv_w_pw1": _Sharded("conv_w_pw1", (2, D // 2, 2 * D), 0, 2),
        "conv_w_pw2": _Sharded("conv_w_pw2", (4, D // 4, D), 1, 0),
        "attn_w_qkv": _Sharded("attn_w_qkv", (2, D // 2, 9 * HE), 0, 2),
        "attn_w_o": _Sharded("attn_w_o", (4, HE // 4, D), 1, 0),
    }
    dl = [descs[n] for n in _BIG]
    as3 = lambda a, n: a.reshape(descs[n].shard3)

    full = _gather_weights(dl, [as3(w[n], n).astype(BF16) for n in _BIG],
                           [norm_g.reshape(6, D // 4), conv_w_dw.reshape(CONV_WIDTH, D // 4)])
    fw = dict(zip(_BIG, full[:len(_BIG)]))
    norm_full, dw_full = full[len(_BIG):]

    sq, dx, grads = _local_step(
        x[0], loss_target[0], norm_full.reshape(2, 3, D), fw["ffn_w_in"], fw["ffn_w_out"], fw["conv_w_pw1"].reshape(D, 2 * D),
        conv_b_pw1, dw_full, conv_b_dw, conv_norm_g, fw["conv_w_pw2"].reshape(D, D), conv_b_pw2,
        fw["attn_w_qkv"].reshape(D, 9 * HE), attn_q_norm[0], attn_k_norm[0], fw["attn_w_o"].reshape(HE, D))
    loss = lax.psum(0.5 * jnp.sum(sq) / D, ("x", "y", "c"))

    g32 = [grads[n][0].reshape(descs[n].full3) for n in _BIG]
    g16 = [grads[n][1].reshape(descs[n].full3) for n in _BIG]
    got = _pair_send_halves(dl, g16)
    pairs = [_pair_add(d, g, r, scal) for d, g, r in zip(dl, g32, got)]
    got = _chip_send_parts(dl, [p[1] for p in pairs])
    red = [_chip_reduce(d, p[0], r, scal) for d, p, r in zip(dl, pairs, got)]
    gsh = dict(zip(_BIG, _pair_fill(dl, red)))

    out_g, out_d, out_m, out_v = {}, {}, {}, {}
    for n in _BIG:
        dlt, mo, vo = _adam(descs[n], gsh[n], as3(w[n], n), as3(m[n], n), as3(v[n], n))
        out_g[n], out_d[n], out_m[n], out_v[n] = (a.reshape(w[n].shape) for a in (gsh[n], dlt, mo, vo))

    parts = [_rows8(grads[n], D) for n in _SMALL]
    tot = _allreduce_small(jnp.concatenate(parts, axis=0))
    sg, r0 = {}, 0
    for n, p in zip(_SMALL, parts):
        last = grads[n].shape[-1]
        g = tot[r0:r0 + grads[n].size // min(last, D), :min(last, D)].reshape(-1, last)
        r0 += p.shape[0]
        if n in ("norm_g", "conv_w_dw"):
            g = lax.dynamic_slice_in_dim(g, me * (D // 4), D // 4, axis=1)
        sg[n] = g
    flat = lambda a: a.reshape(-1, a.shape[-1])
    ds, ms, vs = _adam_small([sg[n] for n in _SMALL], [flat(w[n]) for n in _SMALL], [flat(m[n]) for n in _SMALL],
                             [flat(v[n]) for n in _SMALL])
    for i, n in enumerate(_SMALL):
        out_g[n], out_d[n], out_m[n], out_v[n] = (a.reshape(w[n].shape) for a in (sg[n], ds[i], ms[i], vs[i]))

    return (loss, dx[None], *[out_g[n] for n in _NAMES], *[out_d[n] for n in _NAMES], *[out_m[n] for n in _NAMES],
            *[out_v[n] for n in _NAMES])
```

```python
import functools

import jax
import jax.numpy as jnp
from jax import lax
from jax.experimental import pallas as pl
from jax.experimental.pallas import tpu as pltpu

F32 = jnp.float32
BF16 = jnp.bfloat16
MESH = pl.DeviceIdType.MESH

NORM_EPS = 1e-6
CONV_WIDTH = 31
ATTN_GROUPS = ((128, 1), (512, 4), (2048, 16))
ATTN_BLOCK = 128
HEAD_DIM = 128
N_CHIPS = 4

ADAM_LR = 0.001
ADAM_B1 = 0.9
ADAM_B2 = 0.999
ADAM_EPS = 1e-08
ADAM_WD = 0.01
ADAM_STEP = 10

VMEM_LIMIT = 56 * 1024 * 1024
NT_DIMS = (((1,), (1,)), ((), ()))
TN_DIMS = (((0,), (0,)), ((), ()))


def _pick(n, pref, mult):
    t = (min(n, pref) // mult) * mult
    while t >= mult:
        if n % t == 0:
            return t
        t -= mult
    return n


def _call(body, name, grid, in_specs, out_specs, out_shape, scratch=(), aliases=None, prefetch=0):
    params = pltpu.CompilerParams(dimension_semantics=("arbitrary",) * len(grid), vmem_limit_bytes=VMEM_LIMIT)
    if prefetch:
        spec = pltpu.PrefetchScalarGridSpec(
            num_scalar_prefetch=prefetch, grid=grid, in_specs=in_specs, out_specs=out_specs, scratch_shapes=list(scratch)
        )
        return pl.pallas_call(body, name=name, grid_spec=spec, out_shape=out_shape, compiler_params=params,
                              input_output_aliases=aliases or {})
    return pl.pallas_call(body, name=name, grid=grid, in_specs=in_specs, out_specs=out_specs, out_shape=out_shape,
                          scratch_shapes=list(scratch), compiler_params=params, input_output_aliases=aliases or {})


def _sds(shape, dtype):
    return jax.ShapeDtypeStruct(shape, dtype)


def _sig(x):
    return 1.0 / (1.0 + jnp.exp(-x))


def _rstd(x):
    return lax.rsqrt(jnp.mean(x * x, axis=-1, keepdims=True) + NORM_EPS)


def _norm_bwd(dy, xhat, r, g):
    dxh = dy * g
    return r * (dxh - xhat * jnp.mean(dxh * xhat, axis=-1, keepdims=True))


def _dot(a, b):
    return jnp.dot(a, b, preferred_element_type=F32)


def _dot_nt(a, b):
    return lax.dot_general(a, b, NT_DIMS, preferred_element_type=F32)


def _dot_tn(a, b):
    return lax.dot_general(a, b, TN_DIMS, preferred_element_type=F32)


def _ffn_tile(F):
    return _pick(F, 256, 128)


def _ffn_fwd(x, g, w_in, w_out, name):
    T, D = x.shape
    F = w_out.shape[0]
    tf = _ffn_tile(F)
    nf = F // tf
    tm = _pick(T, 1024, 8)

    def body(x_ref, g_ref, wg_ref, wu_ref, wo_ref, xo_ref, gu_ref, h_sc, acc_sc):
        j = pl.program_id(1)

        @pl.when(j == 0)
        def _():
            xv = x_ref[...]
            h_sc[...] = (xv * _rstd(xv) * g_ref[...]).astype(BF16)
            acc_sc[...] = jnp.zeros_like(acc_sc)

        h = h_sc[...]
        gate = _dot(h, wg_ref[...])
        up = _dot(h, wu_ref[...])
        gu_ref[:, :tf] = gate.astype(BF16)
        gu_ref[:, tf:] = up.astype(BF16)
        a = (gate * _sig(gate) * up).astype(BF16)
        acc_sc[...] += _dot(a, wo_ref[...])

        @pl.when(j == nf - 1)
        def _():
            xo_ref[...] = x_ref[...] + 0.5 * acc_sc[...]

    return _call(
        body, name, (T // tm, nf),
        [pl.BlockSpec((tm, D), lambda i, j: (i, 0)),
         pl.BlockSpec((1, D), lambda i, j: (0, 0)),
         pl.BlockSpec((D, tf), lambda i, j: (0, j)),
         pl.BlockSpec((D, tf), lambda i, j: (0, nf + j)),
         pl.BlockSpec((tf, D), lambda i, j: (j, 0))],
        [pl.BlockSpec((tm, D), lambda i, j: (i, 0)),
         pl.BlockSpec((tm, 2 * tf), lambda i, j: (i, j))],
        [_sds((T, D), F32), _sds((T, 2 * F), BF16)],
        scratch=[pltpu.VMEM((tm, D), BF16), pltpu.VMEM((tm, D), F32)],
    )(x, g, w_in, w_in, w_out)


def _ffn_bwd(x, g, dy, gu, w_in, w_out, name):
    T, D = x.shape
    F = w_out.shape[0]
    tf = _ffn_tile(F)
    nf = F // tf
    tm = _pick(T, 512, 8)

    def body(x_ref, g_ref, dy_ref, gu_ref, wg_ref, wu_ref, wo_ref, dx_ref, dg_ref, dgu_ref, a_ref, h_ref, dyb_ref, dh_sc):
        i = pl.program_id(0)
        j = pl.program_id(1)

        @pl.when(j == 0)
        def _():
            xv = x_ref[...]
            h_ref[...] = (xv * _rstd(xv) * g_ref[...]).astype(BF16)
            dyb_ref[...] = (0.5 * dy_ref[...]).astype(BF16)
            dh_sc[...] = jnp.zeros_like(dh_sc)

        @pl.when((i == 0) & (j == 0))
        def _():
            dg_ref[...] = jnp.zeros_like(dg_ref)

        gate = gu_ref[:, :tf].astype(F32)
        up = gu_ref[:, tf:].astype(F32)
        sg = _sig(gate)
        sl = gate * sg
        a_ref[...] = (sl * up).astype(BF16)
        da = _dot_nt(dyb_ref[...], wo_ref[...])
        dgate = (da * up * (sg * (1.0 + gate * (1.0 - sg)))).astype(BF16)
        dup = (da * sl).astype(BF16)
        dgu_ref[:, :tf] = dgate
        dgu_ref[:, tf:] = dup
        dh_sc[...] += _dot_nt(dgate, wg_ref[...]) + _dot_nt(dup, wu_ref[...])

        @pl.when(j == nf - 1)
        def _():
            xv = x_ref[...]
            r = _rstd(xv)
            xh = xv * r
            dh = dh_sc[...]
            dx_ref[...] = dy_ref[...] + _norm_bwd(dh, xh, r, g_ref[...])
            dg_ref[...] += jnp.sum(dh * xh, axis=0, keepdims=True)

    return _call(
        body, name, (T // tm, nf),
        [pl.BlockSpec((tm, D), lambda i, j: (i, 0)),
         pl.BlockSpec((1, D), lambda i, j: (0, 0)),
         pl.BlockSpec((tm, D), lambda i, j: (i, 0)),
         pl.BlockSpec((tm, 2 * tf), lambda i, j: (i, j)),
         pl.BlockSpec((D, tf), lambda i, j: (0, j)),
         pl.BlockSpec((D, tf), lambda i, j: (0, nf + j)),
         pl.BlockSpec((tf, D), lambda i, j: (j, 0))],
        [pl.BlockSpec((tm, D), lambda i, j: (i, 0)),
         pl.BlockSpec((1, D), lambda i, j: (0, 0)),
         pl.BlockSpec((tm, 2 * tf), lambda i, j: (i, j)),
         pl.BlockSpec((tm, tf), lambda i, j: (i, j)),
         pl.BlockSpec((tm, D), lambda i, j: (i, 0)),
         pl.BlockSpec((tm, D), lambda i, j: (i, 0))],
        [_sds((T, D), F32), _sds((1, D), F32), _sds((T, 2 * F), BF16), _sds((T, F), BF16), _sds((T, D), BF16),
         _sds((T, D), BF16)],
        scratch=[pltpu.VMEM((tm, D), F32)],
    )(x, g, dy, gu, w_in, w_in, w_out)


def _mm_tn(a, b, bm, bn, out_shape, out_block, out_map, scale, name, prev=None):
    K, M = a.shape
    N = b.shape[1]

    def body(a_ref, b_ref, *rest):
        o_ref, ob_ref = rest[-2:]
        o = _dot_tn(a_ref[...].astype(BF16), b_ref[...].astype(BF16)) * scale
        o_ref[...] = o
        ob_ref[...] = o.astype(BF16)

    in_specs = [pl.BlockSpec((K, bm), lambda mi, ni: (0, mi)), pl.BlockSpec((K, bn), lambda mi, ni: (0, ni))]
    args = [a, b]
    aliases = None
    if prev is not None:
        in_specs += [pl.BlockSpec(memory_space=pl.ANY)] * 2
        args += list(prev)
        aliases = {2: 0, 3: 1}
    ospec = pl.BlockSpec(out_block, out_map)
    return _call(body, name, (M // bm, N // bn), in_specs, [ospec, ospec],
                 [_sds(out_shape, F32), _sds(out_shape, BF16)], aliases=aliases)(*args)


def _conv_pre(x, g, w1, b1, name):
    T, D = x.shape
    tm = _pick(T, 512, 8)

    def body(x_ref, g_ref, w_ref, b_ref, ag_ref, u_ref, h_ref):
        xv = x_ref[...]
        h = (xv * _rstd(xv) * g_ref[...]).astype(BF16)
        h_ref[...] = h
        ag = _dot(h, w_ref[...]) + b_ref[...]
        ag_ref[...] = ag.astype(BF16)
        u_ref[...] = ag[:, :D] * _sig(ag[:, D:])

    return _call(
        body, name, (T // tm,),
        [pl.BlockSpec((tm, D), lambda i: (i, 0)), pl.BlockSpec((1, D), lambda i: (0, 0)),
         pl.BlockSpec((D, 2 * D), lambda i: (0, 0)), pl.BlockSpec((1, 2 * D), lambda i: (0, 0))],
        [pl.BlockSpec((tm, 2 * D), lambda i: (i, 0)), pl.BlockSpec((tm, D), lambda i: (i, 0)),
         pl.BlockSpec((tm, D), lambda i: (i, 0))],
        [_sds((T, 2 * D), BF16), _sds((T, D), F32), _sds((T, D), BF16)],
    )(x, g, w1, b1)


_DW_PAD = 32
_DW_CHUNK = 256


def _dwconv(u, w, b, name):
    T, D = u.shape
    K = w.shape[0]
    ch = _pick(T, _DW_CHUNK, 8)
    lead = _DW_PAD - (K - 1)

    def body(u_ref, w_ref, b_ref, c_ref, ext):
        ext[pl.ds(0, _DW_PAD), :] = jnp.zeros((_DW_PAD, 128), F32)
        ext[pl.ds(_DW_PAD, T), :] = u_ref[...]
        for c0 in range(0, T, ch):
            acc = jnp.zeros((ch, 128), F32) + b_ref[...]
            for k in range(K):
                acc = acc + w_ref[pl.ds(k, 1), :] * ext[pl.ds(c0 + lead + k, ch), :]
            c_ref[pl.ds(c0, ch), :] = acc

    return _call(
        body, name, (D // 128,),
        [pl.BlockSpec((T, 128), lambda i: (0, i)), pl.BlockSpec((K, 128), lambda i: (0, i)),
         pl.BlockSpec((1, 128), lambda i: (0, i))],
        [pl.BlockSpec((T, 128), lambda i: (0, i))],
        [_sds((T, D), F32)],
        scratch=[pltpu.VMEM((T + _DW_PAD, 128), F32)],
    )(u, w, b)[0]


def _dwconv_bwd(dc, u, w, name):
    T, D = u.shape
    K = w.shape[0]
    ch = _pick(T, _DW_CHUNK, 8)
    lead = _DW_PAD - (K - 1)

    def body(dc_ref, u_ref, w_ref, du_ref, dw_ref, db_ref, uext, dext):
        uext[pl.ds(0, _DW_PAD), :] = jnp.zeros((_DW_PAD, 128), F32)
        uext[pl.ds(_DW_PAD, T), :] = u_ref[...]
        dext[pl.ds(0, T), :] = dc_ref[...]
        dext[pl.ds(T, _DW_PAD), :] = jnp.zeros((_DW_PAD, 128), F32)
        dws =[jnp.zeros((8, 128), F32) for _ in range(K)]
        dbs = jnp.zeros((8, 128), F32)
        for c0 in range(0, T, ch):
            dcv = dext[pl.ds(c0, ch), :]
            dbs = dbs + jnp.sum(dcv.reshape(ch // 8, 8, 128), axis=0)
            acc = jnp.zeros((ch, 128), F32)
            for k in range(K):
                acc = acc + w_ref[pl.ds(k, 1), :] * dext[pl.ds(c0 + (K - 1) - k, ch), :]
                prod = dcv * uext[pl.ds(c0 + lead + k, ch), :]
                dws[k] = dws[k] + jnp.sum(prod.reshape(ch // 8, 8, 128), axis=0)
            du_ref[pl.ds(c0, ch), :] = acc
        for k in range(K):
            dw_ref[pl.ds(k, 1), :] = jnp.sum(dws[k], axis=0, keepdims=True)
        db_ref[...] = jnp.sum(dbs, axis=0, keepdims=True)

    return _call(
        body, name, (D // 128,),
        [pl.BlockSpec((T, 128), lambda i: (0, i)), pl.BlockSpec((T, 128), lambda i: (0, i)),
         pl.BlockSpec((K, 128), lambda i: (0, i))],
        [pl.BlockSpec((T, 128), lambda i: (0, i)), pl.BlockSpec((K, 128), lambda i: (0, i)),
         pl.BlockSpec((1, 128), lambda i: (0, i))],
        [_sds((T, D), F32), _sds((K, D), F32), _sds((1, D), F32)],
        scratch=[pltpu.VMEM((T + _DW_PAD, 128), F32), pltpu.VMEM((T + _DW_PAD, 128), F32)],
    )(dc, u, w)


def _conv_post(c, x, ng, w2, b2, name):
    T, D = x.shape
    tm = _pick(T, 512, 8)

    def body(c_ref, x_ref, ng_ref, w_ref, b_ref, xo_ref, s_ref):
        cv = c_ref[...]
        n = cv * _rstd(cv) * ng_ref[...]
        s = (n * _sig(n)).astype(BF16)
        s_ref[...] = s
        xo_ref[...] = x_ref[...] + _dot(s, w_ref[...]) + b_ref[...]

    row = lambda i: (i, 0)
    fix = lambda i: (0, 0)
    return _call(
        body, name, (T // tm,),
        [pl.BlockSpec((tm, D), row), pl.BlockSpec((tm, D), row), pl.BlockSpec((1, D), fix),
         pl.BlockSpec((D, D), fix), pl.BlockSpec((1, D), fix)],
        [pl.BlockSpec((tm, D), row), pl.BlockSpec((tm, D), row)],
        [_sds((T, D), F32), _sds((T, D), BF16)],
    )(c, x, ng, w2, b2)


def _conv_post_bwd(dy, c, ng, w2, name):
    T, D = dy.shape
    tm = _pick(T, 512, 8)

    def body(dy_ref, c_ref, ng_ref, w_ref, dc_ref, dng_ref, db_ref):
        @pl.when(pl.program_id(0) == 0)
        def _():
            dng_ref[...] = jnp.zeros_like(dng_ref)
            db_ref[...] = jnp.zeros_like(db_ref)

        dyv = dy_ref[...]
        ds = _dot_nt(dyv.astype(BF16), w_ref[...])
        cv = c_ref[...]
        r = _rstd(cv)
        ch = cv * r
        n = ch * ng_ref[...]
        sg = _sig(n)
        dn = ds * (sg * (1.0 + n * (1.0 - sg)))
        dc_ref[...] = _norm_bwd(dn, ch, r, ng_ref[...])
        dng_ref[...] += jnp.sum(dn * ch, axis=0, keepdims=True)
        db_ref[...] += jnp.sum(dyv, axis=0, keepdims=True)

    row = lambda i: (i, 0)
    fix = lambda i: (0, 0)
    return _call(
        body, name, (T // tm,),
        [pl.BlockSpec((tm, D), row), pl.BlockSpec((tm, D), row), pl.BlockSpec((1, D), fix), pl.BlockSpec((D, D), fix)],
        [pl.BlockSpec((tm, D), row), pl.BlockSpec((1, D), fix), pl.BlockSpec((1, D), fix)],
        [_sds((T, D), F32), _sds((1, D), F32), _sds((1, D), F32)],
    )(dy, c, ng, w2)


def _conv_pre_bwd(du, ag, x, g, dy, w1, name):
    T, D = x.shape
    tm = _pick(T, 512, 8)

    def body(du_ref, ag_ref, x_ref, g_ref, dy_ref, w_ref, dx_ref, dg_ref, dag_ref, db_ref):
        @pl.when(pl.program_id(0) == 0)
        def _():
            dg_ref[...] = jnp.zeros_like(dg_ref)
            db_ref[...] = jnp.zeros_like(db_ref)

        duv = du_ref[...]
        a = ag_ref[:, :D].astype(F32)
        gt = ag_ref[:, D:].astype(F32)
        sg = _sig(gt)
        da = duv * sg
        dgt = duv * a * sg * (1.0 - sg)
        db_ref[:, :D] += jnp.sum(da, axis=0, keepdims=True)
        db_ref[:, D:] += jnp.sum(dgt, axis=0, keepdims=True)
        dab = da.astype(BF16)
        dgb = dgt.astype(BF16)
        dag_ref[:, :D] = dab
        dag_ref[:, D:] = dgb
        dh = _dot_nt(dab, w_ref[:, :D]) + _dot_nt(dgb, w_ref[:, D:])
        xv = x_ref[...]
        r = _rstd(xv)
        xh = xv * r
        dx_ref[...] = dy_ref[...] + _norm_bwd(dh, xh, r, g_ref[...])
        dg_ref[...] += jnp.sum(dh * xh, axis=0, keepdims=True)

    row = lambda i: (i, 0)
    fix = lambda i: (0, 0)
    return _call(
        body, name, (T // tm,),
        [pl.BlockSpec((tm, D), row), pl.BlockSpec((tm, 2 * D), row), pl.BlockSpec((tm, D), row), pl.BlockSpec((1, D), fix),
         pl.BlockSpec((tm, D), row), pl.BlockSpec((D, 2 * D), fix)],
        [pl.BlockSpec((tm, D), row), pl.BlockSpec((1, D), fix), pl.BlockSpec((tm, 2 * D), row),
         pl.BlockSpec((1, 2 * D), fix)],
        [_sds((T, D), F32), _sds((1, D), F32), _sds((T, 2 * D), BF16), _sds((1, 2 * D), F32)],
    )(du, ag, x, g, dy, w1)


def _attn_qkv(x, g, wqkv, name):
    T, D = x.shape
    N = wqkv.shape[1]
    tn = N // 9
    tm = _pick(T, 512, 8)

    def body(x_ref, g_ref, w_ref, o_ref, h_ref):
        @pl.when(pl.program_id(1) == 0)
        def _():
            xv = x_ref[...]
            h_ref[...] = (xv * _rstd(xv) * g_ref[...]).astype(BF16)

        o_ref[...] = _dot(h_ref[...], w_ref[...]).astype(BF16)

    return _call(
        body, name, (T // tm, 9),
        [pl.BlockSpec((tm, D), lambda i, j: (i, 0)), pl.BlockSpec((1, D), lambda i, j: (0, 0)),
         pl.BlockSpec((D, tn), lambda i, j: (0, j))],
        [pl.BlockSpec((tm, tn), lambda i, j: (i, j)), pl.BlockSpec((tm, D), lambda i, j: (i, 0))],
        [_sds((T, N), BF16), _sds((T, D), BF16)],
    )(x, g, wqkv)


def _band_masks(q, steps, nblk):
    i = lax.broadcasted_iota(jnp.int32, (q, q), 0)
    j = lax.broadcasted_iota(jnp.int32, (q, q), 1)
    cur = i - j >= 0
    if steps < q - 1:
        cur = cur & (i - j <= steps)
    prev = q + i - j <= jnp.where(nblk > 0, steps, -1)
    return cur, prev


def _attn_fwd(qkv, qn_g, kn_g, grp, window, dil, name):
    T, C = qkv.shape
    HE = C // 9
    H = HE // HEAD_DIM
    E = HEAD_DIM
    Q = ATTN_BLOCK
    L = T // dil
    nb = L // Q
    steps = window // dil
    scale = E ** -0.5
    view = qkv.reshape(L, dil * C)

    def body(q_ref, kc_ref, kp_ref, vc_ref, vp_ref, gq_ref, gk_ref, o_ref, l_ref):
        n = pl.program_id(1)
        cur, prev = _band_masks(Q, steps, n)
        gq = gq_ref[...]
        gk = gk_ref[...]
        for h in range(H):
            hs = slice(h * E, (h + 1) * E)
            q = q_ref[:, hs].astype(F32)
            kc = kc_ref[:, hs].astype(F32)
            kp = kp_ref[:, hs].astype(F32)
            qn = (q * _rstd(q) * gq).astype(BF16)
            kcn = (kc * _rstd(kc) * gk).astype(BF16)
            kpn = (kp * _rstd(kp) * gk).astype(BF16)
            sc = jnp.where(cur, _dot_nt(qn, kcn) * scale, -1e30)
            sp = jnp.where(prev, _dot_nt(qn, kpn) * scale, -1e30)
            m = jnp.maximum(jnp.max(sc, axis=-1, keepdims=True), jnp.max(sp, axis=-1, keepdims=True))
            pc = jnp.exp(sc - m)
            pp = jnp.exp(sp - m)
            l = jnp.sum(pc, axis=-1, keepdims=True) + jnp.sum(pp, axis=-1, keepdims=True)
            inv = 1.0 / l
            o = _dot((pc * inv).astype(BF16), vc_ref[:, hs]) + _dot((pp * inv).astype(BF16), vp_ref[:, hs])
            o_ref[:, hs] = o
            l_ref[:, hs] = jnp.broadcast_to(m + jnp.log(l), (Q, E))

    base = grp * 3
    blk = lambda s, back: pl.BlockSpec((Q, HE), lambda r, n: (jnp.maximum(n - back, 0), r * 9 + base + s))
    vec = pl.BlockSpec((1, E), lambda r, n: (0, 0))
    out = pl.BlockSpec((Q, HE), lambda r, n: (n, r))
    o, lse = _call(
        body, name, (dil, nb),
        [blk(0, 0), blk(1, 0), blk(1, 1), blk(2, 0), blk(2, 1), vec, vec],
        [out, out],
        [_sds((L, dil * HE), F32), _sds((L, dil * HE), F32)],
    )(view, view, view, view, view, qn_g, kn_g)
    return o.reshape(T, HE), lse.reshape(T, HE)


def _attn_merge(os, lses, x, wo, name):
    T, D = x.shape
    HE = wo.shape[0]
    tm = _pick(T, 512, 8)
    ng = len(os)

    def body(*refs):
        o_refs = refs[:ng]
        l_refs = refs[ng:2 * ng]
        x_ref, w_ref, xo_ref, om_ref, lt_ref = refs[2 * ng:]
        ls = [r[...] for r in l_refs]
        m = functools.reduce(jnp.maximum, ls)
        es = [jnp.exp(l - m) for l in ls]
        tot = functools.reduce(lambda a, b: a + b, es)
        inv = 1.0 / tot
        om = functools.reduce(lambda a, b: a + b, [e * inv * r[...] for e, r in zip(es, o_refs)])
        omb = om.astype(BF16)
        om_ref[...] = omb
        lt_ref[...] = m + jnp.log(tot)
        xo_ref[...] = x_ref[...] + _dot(omb, w_ref[...])

    row = lambda i: (i, 0)
    fix = lambda i: (0, 0)
    return _call(
        body, name, (T // tm,),
        [pl.BlockSpec((tm, HE), row)] * (2 * ng) + [pl.BlockSpec((tm, D), row), pl.BlockSpec((HE, D), fix)],
        [pl.BlockSpec((tm, D), row), pl.BlockSpec((tm, HE), row), pl.BlockSpec((tm, HE), row)],
        [_sds((T, D), F32), _sds((T, HE), BF16), _sds((T, HE), F32)],
    )(*os, *lses, x, wo)


def _attn_out_bwd(dy, om, wo, name):
    T, D = dy.shape
    HE = wo.shape[0]
    E = HEAD_DIM
    tm = _pick(T, 512, 8)

    def body(dy_ref, om_ref, w_ref, dom_ref, dl_ref):
        dom = _dot_nt(dy_ref[...].astype(BF16), w_ref[...])
        dom_ref[...] = dom.astype(BF16)
        prod = dom * om_ref[...].astype(F32)
        for h in range(HE // E):
            hs = slice(h * E, (h + 1) * E)
            dl_ref[:, hs] = jnp.broadcast_to(jnp.sum(prod[:, hs], axis=-1, keepdims=True), (tm, E))

    row = lambda i: (i, 0)
    return _call(
        body, name, (T // tm,),
        [pl.BlockSpec((tm, D), row), pl.BlockSpec((tm, HE), row), pl.BlockSpec((HE, D), lambda i: (0, 0))],
        [pl.BlockSpec((tm, HE), row), pl.BlockSpec((tm, HE), row)],
        [_sds((T, HE), BF16), _sds((T, HE), F32)],
    )(dy, om, wo)


def _attn_bwd(qkv, dom, lse, delta, qn_g, kn_g, grp, window, dil, name):
    T, C = qkv.shape
    HE = C // 9
    H = HE // HEAD_DIM
    E = HEAD_DIM
    Q = ATTN_BLOCK
    L = T // dil
    nb = L // Q
    steps = window // dil
    scale = E ** -0.5
    view = qkv.reshape(L, dil * C)
    sview = lambda a: a.reshape(L, dil * HE)

    def body(q_ref, kc_ref, kp_ref, vc_ref, vp_ref, do_ref, l_ref, dl_ref, gq_ref, gk_ref,
             dq_ref, dk_ref, dv_ref, dgq_ref, dgk_ref, ck_sc, cv_sc):
        r_id = pl.program_id(0)
        n = pl.program_id(1)

        @pl.when((r_id == 0) & (n == 0))
        def _():
            dgq_ref[...] = jnp.zeros_like(dgq_ref)
            dgk_ref[...] = jnp.zeros_like(dgk_ref)

        @pl.when(n == 0)
        def _():
            ck_sc[...] = jnp.zeros_like(ck_sc)
            cv_sc[...] = jnp.zeros_like(cv_sc)

        @pl.when(n < nb)
        def _():
            cur, prev = _band_masks(Q, steps, n)
            gq = gq_ref[...]
            gk = gk_ref[...]
            dgq = jnp.zeros((1, E), F32)
            dgk = jnp.zeros((1, E), F32)
            for h in range(H):
                hs = slice(h * E, (h + 1) * E)
                q = q_ref[:, hs].astype(F32)
                kc = kc_ref[:, hs].astype(F32)
                kp = kp_ref[:, hs].astype(F32)
                rq, rc, rp = _rstd(q), _rstd(kc), _rstd(kp)
                qh, kch, kph = q * rq, kc * rc, kp * rp
                qn = (qh * gq).astype(BF16)
                kcn = (kch * gk).astype(BF16)
                kpn = (kph * gk).astype(BF16)
                lcol = l_ref[:, hs][:, :1]
                dcol = dl_ref[:, hs][:, :1]
                do = do_ref[:, hs]
                pc = jnp.where(cur, jnp.exp(_dot_nt(qn, kcn) * scale - lcol), 0.0)
                pp = jnp.where(prev, jnp.exp(_dot_nt(qn, kpn) * scale - lcol), 0.0)
                dsc = (pc * (_dot_nt(do, vc_ref[:, hs]) - dcol) * scale).astype(BF16)
                dsp = (pp * (_dot_nt(do, vp_ref[:, hs]) - dcol) * scale).astype(BF16)
                dvc = _dot_tn(pc.astype(BF16), do)
                dvp = _dot_tn(pp.astype(BF16), do)
                dqn = _dot(dsc, kcn) + _dot(dsp, kpn)
                dkcn = _dot_tn(dsc, qn)
                dkpn = _dot_tn(dsp, qn)
                dq_ref[:, hs] = _norm_bwd(dqn, qh, rq, gq).astype(BF16)
                dgq = dgq + jnp.sum(dqn * qh, axis=0, keepdims=True)
                dgk = dgk + jnp.sum(dkcn * kch + dkpn * kph, axis=0, keepdims=True)
                dk_ref[:, hs] = (ck_sc[:, hs] + _norm_bwd(dkpn, kph, rp, gk)).astype(BF16)
                dv_ref[:, hs] = (cv_sc[:, hs] + dvp).astype(BF16)
                ck_sc[:, hs] = _norm_bwd(dkcn, kch, rc, gk)
                cv_sc[:, hs] = dvc
            dgq_ref[...] += dgq
            dgk_ref[...] += dgk

        @pl.when(n == nb)
        def _():
            dk_ref[...] = ck_sc[...].astype(BF16)
            dv_ref[...] = cv_sc[...].astype(BF16)

    base = grp * 3
    nq = lambda n: jnp.minimum(n, nb - 1)
    blk = lambda s, back: pl.BlockSpec((Q, HE), lambda r, n: (jnp.maximum(nq(n) - back, 0), r * 9 + base + s))
    qblk = pl.BlockSpec((Q, HE), lambda r, n: (nq(n), r))
    kblk = pl.BlockSpec((Q, HE), lambda r, n: (jnp.maximum(n - 1, 0), r))
    vec = pl.BlockSpec((1, E), lambda r, n: (0, 0))
    dq, dk, dv, dgq, dgk = _call(
        body, name, (dil, nb + 1),
        [blk(0, 0), blk(1, 0), blk(1, 1), blk(2, 0), blk(2, 1), qblk, qblk, qblk, vec, vec],
        [qblk, kblk, kblk, vec, vec],
        [_sds((L, dil * HE), BF16)] * 3 + [_sds((1, E), F32)] * 2,
        scratch=[pltpu.VMEM((Q, HE), F32), pltpu.VMEM((Q, HE), F32)],
    )(view, view, view, view, view, sview(dom), sview(lse), sview(delta), qn_g, kn_g)
    return dq.reshape(T, HE), dk.reshape(T, HE), dv.reshape(T, HE), dgq, dgk


def _attn_qkv_bwd(dqkv, x, g, dy, wqkv, name):
    T, D = x.shape
    HE = wqkv.shape[1] // 9
    tm = _pick(T, 512, 8)

    def body(*refs):
        d_refs = refs[:9]
        x_ref, g_ref, dy_ref, w_ref, dx_ref, dg_ref, dh_sc = refs[9:]
        i = pl.program_id(0)
        j = pl.program_id(1)

        @pl.when((i == 0) & (j == 0))
        def _():
            dg_ref[...] = jnp.zeros_like(dg_ref)

        @pl.when(j == 0)
        def _():
            dh_sc[...] = jnp.zeros_like(dh_sc)

        for s in range(9):
            @pl.when(j == s)
            def _(s=s):
                dh_sc[...] += _dot_nt(d_refs[s][...], w_ref[...])

        @pl.when(j == 8)
        def _():
            xv = x_ref[...]
            r = _rstd(xv)
            xh = xv * r
            dh = dh_sc[...]
            dx_ref[...] = dy_ref[...] + _norm_bwd(dh, xh, r, g_ref[...])
            dg_ref[...] += jnp.sum(dh * xh, axis=0, keepdims=True)

    row = lambda i, j: (i, 0)
    fix = lambda i, j: (0, 0)
    return _call(
        body, name, (T // tm, 9),
        [pl.BlockSpec((tm, HE), row)] * 9 + [pl.BlockSpec((tm, D), row), pl.BlockSpec((1, D), fix), pl.BlockSpec((tm, D), row),
                                           pl.BlockSpec((D, HE), lambda i, j: (0, j))],
        [pl.BlockSpec((tm, D), row), pl.BlockSpec((1, D), fix)],
        [_sds((T, D), F32), _sds((1, D), F32)],
        scratch=[pltpu.VMEM((tm, D), F32)],
    )(*dqkv, x, g, dy, wqkv)


def _loss_head(y, target, name):
    T, D = y.shape
    tm = _pick(T, 512, 8)

    def body(y_ref, t_ref, dy_ref, sq_ref):
        @pl.when(pl.program_id(0) == 0)
        def _():
            sq_ref[...] = jnp.zeros_like(sq_ref)

        err = y_ref[...] - t_ref[...]
        dy_ref[...] = err * (1.0 / D)
        sq_ref[...] += jnp.sum(err * err, axis=0, keepdims=True)

    row = lambda i: (i, 0)
    return _call(
        body, name, (T // tm,),
        [pl.BlockSpec((tm, D), row), pl.BlockSpec((tm, D), row)],
        [pl.BlockSpec((tm, D), row), pl.BlockSpec((1, D), lambda i: (0, 0))],
        [_sds((T, D), F32), _sds((1, D), F32)],
    )(y, target)


def _local_step(x, target, norm_g, b_pw1, w_dw, b_dw, cng, b_pw2, qn, kn, block_weights):
    T, D = x.shape
    ng = lambda l, k: norm_g[l, k][None, :]
    bn = _pick(D, 256, 128)

    w_in, w_out = [None] * 4, [None] * 4
    w_in[0], w_out[0] = block_weights(0, x)
    x1, gu0 = _ffn_fwd(x, ng(0, 0), w_in[0], w_out[0], "ffn_fwd_0")
    pw1, pw2 = block_weights(1, x1)
    ag, u, hc = _conv_pre(x1, ng(0, 1), pw1, b_pw1, "conv_pre")
    c = _dwconv(u, w_dw, b_dw, "dwconv")
    x2, s = _conv_post(c, x1, cng, pw2, b_pw2, "conv_post")
    w_in[1], w_out[1] = block_weights(2, x2)
    x3, gu1 = _ffn_fwd(x2, ng(0, 2), w_in[1], w_out[1], "ffn_fwd_1")
    w_in[2], w_out[2] = block_weights(3, x3)
    x4, gu2 = _ffn_fwd(x3, ng(1, 0), w_in[2], w_out[2], "ffn_fwd_2")
    wqkv, wo = block_weights(4, x4)
    HE = wo.shape[0]
    F = w_out[0].shape[0]
    tf = _ffn_tile(F)
    nf = F // tf
    bh = _pick(HE, 512, 128)
    qkv, ha = _attn_qkv(x4, ng(1, 1), wqkv, "attn_qkv")
    os, lses = [], []
    for gi, (window, dil) in enumerate(ATTN_GROUPS):
        o, l = _attn_fwd(qkv, qn[gi][None, :], kn[gi][None, :], gi, window, dil, f"attn_fwd_{gi}")
        os.append(o)
        lses.append(l)
    x5, om, lse = _attn_merge(os, lses, x4, wo, "attn_merge")
    w_in[3], w_out[3] = block_weights(5, x5)
    x6, gu3 = _ffn_fwd(x5, ng(1, 2), w_in[3], w_out[3], "ffn_fwd_3")
    dy, sq = _loss_head(x6, target, "loss_head")

    grads = {"ffn_w_in": [None] * 4, "ffn_w_out": [None] * 4}
    dnorm = [[None] * 3 for _ in range(2)]

    def ffn_back(lf, xin, gvec, dy, gu):
        dx, dg, dgu, a, h, dyb = _ffn_bwd(xin, gvec, dy, gu, w_in[lf], w_out[lf], f"ffn_bwd_{lf}")
        grads["ffn_w_in"][lf] = _mm_tn(h, dgu, D, tf, (D, 2 * F), (D, tf), lambda mi, ni: (0, (ni % 2) * nf + ni // 2), 1.0,
                                       f"ffn_dw_in_{lf}")
        grads["ffn_w_out"][lf] = _mm_tn(a, dyb, tf, D, (F, D), (tf, D), lambda mi, ni: (mi, 0), 1.0, f"ffn_dw_out_{lf}")
        return dx, dg

    dx, dnorm[1][2] = ffn_back(3, x5, ng(1, 2), dy, gu3)
    dom, delta = _attn_out_bwd(dx, om, wo, "attn_out_bwd")
    grads["attn_w_o"] = _mm_tn(om, dx, HE, bn, (HE, D), (HE, bn), lambda mi, ni: (0, ni), 1.0, "attn_dw_o")
    dqkv, dgq, dgk = [], [], []
    for gi, (window, dil) in enumerate(ATTN_GROUPS):
        dq, dk, dv, a_, b_ = _attn_bwd(qkv, dom, lse, delta, qn[gi][None, :], kn[gi][None, :], gi, window, dil, f"attn_bwd_{gi}")
        dqkv += [dq, dk, dv]
        dgq.append(a_)
        dgk.append(b_)
    grads["attn_q_norm"] = jnp.concatenate(dgq, axis=0)
    grads["attn_k_norm"] = jnp.concatenate(dgk, axis=0)
    d_qkv = None
    for s9 in range(9):
        d_qkv = _mm_tn(ha, dqkv[s9], D, bh, (D, 9 * HE), (D, bh), lambda mi, ni, s9=s9: (0, s9 * (HE // bh) + ni), 1.0,
                       f"attn_dw_qkv_{s9}", prev=d_qkv)
    grads["attn_w_qkv"] = d_qkv
    dx, dnorm[1][1] = _attn_qkv_bwd(dqkv, x4, ng(1, 1), dx, wqkv, "attn_qkv_bwd")
    dx, dnorm[1][0] = ffn_back(2, x3, ng(1, 0), dx, gu2)

    dx, dnorm[0][2] = ffn_back(1, x2, ng(0, 2), dx, gu1)
    dc, grads["conv_norm_g"], grads["conv_b_pw2"] = _conv_post_bwd(dx, c, cng, pw2, "conv_post_bwd")
    grads["conv_w_pw2"] = _mm_tn(s, dx, D, bn, (D, D), (D, bn), lambda mi, ni: (0, ni), 1.0, "conv_dw_pw2")
    du, grads["conv_w_dw"], grads["conv_b_dw"] = _dwconv_bwd(dc, u, w_dw, "dwconv_bwd")
    dx, dnorm[0][1], dag, grads["conv_b_pw1"] = _conv_pre_bwd(du, ag, x1, ng(0, 1), dx, pw1, "conv_pre_bwd")
    grads["conv_w_pw1"] = _mm_tn(hc, dag, D, 2 * bn, (D, 2 * D), (D, 2 * bn), lambda mi, ni: (0, ni), 1.0, "conv_dw_pw1")
    dx, dnorm[0][0] = ffn_back(0, x, ng(0, 0), dx, gu0)

    grads["norm_g"] = jnp.concatenate([jnp.concatenate(r, axis=0)[None] for r in dnorm], axis=0)
    return sq, dx, grads


class _Sharded:
    def __init__(self, name, full3, half_axis, shard_axis, src, slab=None):
        self.name, self.full3, self.half_axis, self.shard_axis = name, tuple(full3), half_axis, shard_axis
        self.src, self.slab = src, slab

    def source(self, refs):
        return refs[self.src] if self.slab is None else refs[self.src].at[self.slab]

    def _cut(self, shape, axis, parts):
        s = list(shape)
        s[axis] //= parts
        return tuple(s)

    @property
    def shard3(self):
        return self._cut(self.full3, self.shard_axis, N_CHIPS)

    @property
    def pair3(self):
        return self._cut(self.full3, self.half_axis, 2)

    @property
    def part3(self):
        return self._cut(self.shard3, self.half_axis, 2)

    @staticmethod
    def _slice(ref, axis, idx, parts):
        n = ref.shape[axis] // parts
        start = idx * n
        minor = len(ref.shape) - 1 - axis
        if minor < 2 and not isinstance(start, int):
            start = pl.multiple_of(start, 128 if minor == 0 else (16 if n % 16 == 0 else 8))
        sl = [slice(None)] * len(ref.shape)
        sl[axis] = pl.ds(start, n)
        return ref.at[tuple(sl)]

    def half(self, ref, h):
        return self._slice(ref, self.half_axis, h, 2)

    def shard(self, ref, j):
        return self._slice(ref, self.shard_axis, j, N_CHIPS)


def _place():
    x, y, c = lax.axis_index("x"), lax.axis_index("y"), lax.axis_index("c")
    return x, y, c, 2 * x + y


_RELS = (1, 2, 3)


def _peer(x, y, rel):
    px = 1 - x if rel & 2 else x
    py = 1 - y if rel & 1 else y
    return px, py, 2 * px + py


ANY = pl.BlockSpec(memory_space=pl.ANY)


def _comm_call(body, name, n_in, out_shape, n_sems, aliases=None):
    return pl.pallas_call(
        body, name=name, in_specs=[ANY] * n_in, out_specs=[ANY] * len(out_shape), out_shape=out_shape,
        scratch_shapes=[pltpu.SemaphoreType.DMA((n,)) for n in n_sems],
        input_output_aliases=aliases or {},
        compiler_params=pltpu.CompilerParams(has_side_effects=True),
    )


def _remote(src, dst, send_sem, recv_sem, dev):
    return pltpu.make_async_remote_copy(src_ref=src, dst_ref=dst, send_sem=send_sem, recv_sem=recv_sem, device_id=dev,
                                        device_id_type=MESH)


def _gather_small(small_shards):
    ns = len(small_shards)

    def body(*refs):
        ins, outs = refs[:ns], refs[ns:2 * ns]
        lsem, ssem, rsem = refs[2 * ns:]
        x, y, c, me = _place()
        cols = lambda ref, j: _Sharded._slice(ref, 1, j, N_CHIPS)
        local = [pltpu.make_async_copy(ins[i], cols(outs[i], me), lsem.at[i]) for i in range(ns)]
        sends = []
        for i in range(ns):
            for k, rel in enumerate(_RELS):
                px, py, _ = _peer(x, y, rel)
                sends.append(_remote(ins[i], cols(outs[i], me), ssem.at[3 * i + k], rsem.at[3 * i + k], (px, py, c)))
        for cp in local + sends:
            cp.start()
        for i in range(ns):
            for k, rel in enumerate(_RELS):
                _, _, pj = _peer(x, y, rel)
                got = cols(outs[i], pj)
                _remote(got, got, ssem.at[3 * i + k], rsem.at[3 * i + k], (x, y, c)).wait_recv()
        for cp in sends:
            cp.wait_send()
        for cp in local:
            cp.wait()

    out_shape = [_sds((s.shape[0], s.shape[1] * N_CHIPS), F32) for s in small_shards]
    return _comm_call(body, "gather_small", ns, out_shape, [ns, 3 * ns, 3 * ns])(*small_shards)


HBM = pl.BlockSpec(memory_space=pltpu.HBM)
SEM = pl.BlockSpec(memory_space=pltpu.SEMAPHORE)
DATAFLOW = pltpu.SideEffectType.DATAFLOW_SIDE_EFFECTING


def _in_hbm(a):
    return pltpu.with_memory_space_constraint(a, pltpu.HBM)


def _place_shards(items, shards):
    names = list(shards)
    ni, ns = len(items), len(names)

    def body(*refs):
        src = dict(zip(names, refs[:ns]))
        outs, sem = refs[ns:ns + ni], refs[ns + ni]
        _, _, _, me = _place()
        cps = [pltpu.make_async_copy(it.source(src), it.shard(outs[i], me), sem.at[i]) for i, it in enumerate(items)]
        for cp in cps:
            cp.start()
        for cp in cps:
            cp.wait()

    return pl.pallas_call(body, name="place_shards", in_specs=[ANY] * ns, out_specs=[ANY] * ni,
                          out_shape=[_sds(it.full3, BF16) for it in items],
                          scratch_shapes=[pltpu.SemaphoreType.DMA((ni,))])(*[shards[n] for n in names])


def _gather_start(items, shards, fulls):
    names = list(shards)
    ni, ns = len(items), len(names)

    def body(*refs):
        outs = refs[ns + ni:]
        ssem, rsem = outs[:ni], outs[ni:2 * ni]
        src = dict(zip(names, outs[2 * ni:2 * ni + ns]))
        full = outs[2 * ni + ns:2 * ni + ns + ni]
        token = outs[-1]
        x, y, c, me = _place()
        for i, it in enumerate(items):
            for k, rel in enumerate(_RELS):
                px, py, _ = _peer(x, y, rel)
                _remote(it.half(it.source(src), c), it.half(it.shard(full[i], me), c), ssem[i].at[k], rsem[i].at[k], (px, py, c)).start()
        token[...] = jnp.zeros_like(token)

    sems = [pltpu.SemaphoreType.DMA((3,))] * (2 * ni)
    thru = [pltpu.HBM(shards[n].shape, BF16) for n in names] + [pltpu.HBM(it.full3, BF16) for it in items]
    outs = pl.pallas_call(
        body, name="gather_start", in_specs=[HBM] * (ns + ni),
        out_specs=[SEM] * (2 * ni) + [HBM] * (ns + ni) + [pl.BlockSpec(memory_space=pltpu.VMEM)],
        out_shape=sems + thru + [_sds((8, 128), F32)],
        input_output_aliases={j: 2 * ni + j for j in range(ns + ni)},
        compiler_params=pltpu.CompilerParams(has_side_effects=DATAFLOW),
    )(*[_in_hbm(shards[n]) for n in names], *[_in_hbm(f) for f in fulls])
    return outs[:ni], outs[ni:2 * ni], dict(zip(names, outs[2 * ni:2 * ni + ns])), outs[2 * ni + ns:2 * ni + ns + ni]


def _gather_forward(items, fulls, ssems, rsems, shards, after, name):
    names = list(shards)
    ni, ns = len(items), len(names)

    def body(*refs):
        ssem, rsem = refs[ni:2 * ni], refs[2 * ni:3 * ni]
        src = dict(zip(names, refs[3 * ni:3 * ni + ns]))
        outs = refs[3 * ni + ns + 1:]
        full, fsem, gsem = outs[:ni], outs[ni:2 * ni], outs[2 * ni:3 * ni]
        x, y, c, _ = _place()
        sib = (x, y, 1 - c)
        for i, it in enumerate(items):
            for k, rel in enumerate(_RELS):
                _, _, pj = _peer(x, y, rel)
                got = it.half(it.shard(full[i], pj), c)
                _remote(got, got, ssem[i].at[k], rsem[i].at[k], sib).wait_recv()
                _remote(got, got, fsem[i].at[k], gsem[i].at[k], sib).start()
        for i, it in enumerate(items):
            mine = it.half(it.source(src), c)
            for k in range(3):
                _remote(mine, mine, ssem[i].at[k], rsem[i].at[k], sib).wait_send()

    outs = pl.pallas_call(
        body, name=name, in_specs=[HBM] * ni + [SEM] * (2 * ni) + [HBM] * ns + [ANY],
        out_specs=[HBM] * ni + [SEM] * (2 * ni),
        out_shape=[pltpu.HBM(it.full3, BF16) for it in items] + [pltpu.SemaphoreType.DMA((3,))] * (2 * ni),
        input_output_aliases={i: i for i in range(ni)},
        compiler_params=pltpu.CompilerParams(has_side_effects=DATAFLOW),
    )(*fulls, *ssems, *rsems, *[shards[n] for n in names], after)
    return outs[:ni], outs[ni:2 * ni], outs[2 * ni:]


def _gather_finish(items, fulls, fsems, gsems, name):
    ni = len(items)

    def body(*refs):
        fsem, gsem = refs[ni:2 * ni], refs[2 * ni:3 * ni]
        full = refs[3 * ni:]
        x, y, c, _ = _place()
        sib = (x, y, 1 - c)
        for i, it in enumerate(items):
            for k, rel in enumerate(_RELS):
                _, _, pj = _peer(x, y, rel)
                got = it.half(it.shard(full[i], pj), 1 - c)
                _remote(got, got, fsem[i].at[k], gsem[i].at[k], sib).wait_recv()
                sent = it.half(it.shard(full[i], pj), c)
                _remote(sent, sent, fsem[i].at[k], gsem[i].at[k], sib).wait_send()

    return pl.pallas_call(
        body, name=name, in_specs=[HBM] * ni + [SEM] * (2 * ni), out_specs=[HBM] * ni,
        out_shape=[pltpu.HBM(it.full3, BF16) for it in items],
        input_output_aliases={i: i for i in range(ni)},
        compiler_params=pltpu.CompilerParams(has_side_effects=DATAFLOW),
    )(*fulls, *fsems, *gsems)


def _pair_send_halves(descs, fulls):
    nb = len(descs)

    def body(*refs):
        ins, outs = refs[:nb], refs[nb:2 * nb]
        ssem, rsem = refs[2 * nb:]
        x, y, c, _ = _place()
        cps = [pltpu.make_async_remote_copy(src_ref=d.half(ins[i], 1 - c), dst_ref=outs[i], send_sem=ssem.at[i], recv_sem=rsem.at[i],
                                            device_id=(x, y, 1 - c), device_id_type=MESH) for i, d in enumerate(descs)]
        for cp in cps:
            cp.start()
        for cp in cps:
            cp.wait_recv()
        for cp in cps:
            cp.wait_send()

    return _comm_call(body, "grads_pair_exchange", nb, [_sds(d.pair3, BF16) for d in descs], [nb, nb])(*fulls)


def _chip_send_parts(descs, pairs):
    nb = len(descs)

    def body(*refs):
        ins, outs = refs[:nb], refs[nb:2 * nb]
        ssem, rsem = refs[2 * nb:]
        x, y, c, _ = _place()
        cps = []
        for i, d in enumerate(descs):
            for k, rel in enumerate(_RELS):
                px, py, pj = _peer(x, y, rel)
                cps.append(pltpu.make_async_remote_copy(src_ref=d.shard(ins[i], pj), dst_ref=outs[i].at[k], send_sem=ssem.at[3 * i + k],
                                                        recv_sem=rsem.at[3 * i + k], device_id=(px, py, c), device_id_type=MESH))
        for cp in cps:
            cp.start()
        for cp in cps:
            cp.wait_recv()
        for cp in cps:
            cp.wait_send()

    return _comm_call(body, "grads_chip_exchange", nb, [_sds((3,) + d.part3, BF16) for d in descs], [3 * nb, 3 * nb])(*pairs)


def _pair_fill(descs, shards):
    nb = len(descs)

    def body(*refs):
        outs = refs[nb:2 * nb]
        ssem, rsem = refs[2 * nb:]
        x, y, c, _ = _place()
        cps = [pltpu.make_async_remote_copy(src_ref=d.half(outs[i], c), dst_ref=d.half(outs[i], c), send_sem=ssem.at[i], recv_sem=rsem.at[i],
                                            device_id=(x, y, 1 - c), device_id_type=MESH) for i, d in enumerate(descs)]
        for cp in cps:
            cp.start()
        for i, d in enumerate(descs):
            got = d.half(outs[i], 1 - c)
            pltpu.make_async_remote_copy(src_ref=got, dst_ref=got, send_sem=ssem.at[i], recv_sem=rsem.at[i], device_id=(x, y, 1 - c),
                                         device_id_type=MESH).wait_recv()
        for cp in cps:
            cp.wait_send()

    return _comm_call(body, "grads_pair_fill", nb, [_sds(d.shard3, F32) for d in descs], [nb, nb],
                      aliases={i: i for i in range(nb)})(*shards)


def _ew_tiles(d):
    _, rows, cols = d.part3
    return _pick(rows, 256, 16), cols


def _pair_add(d, g_full, got, scal):
    tr, tc = _ew_tiles(d)
    a_n, r_n, c_n = d.pair3
    ha = d.half_axis

    def body(sc_ref, g_ref, r_ref, o_ref, ob_ref):
        s = g_ref[...] + r_ref[...].astype(F32)
        o_ref[...] = s
        ob_ref[...] = s.astype(BF16)

    blk = (1, tr, tc)
    same = pl.BlockSpec(blk, lambda a, rb, cb, sc: (a, rb, cb))
    mine = pl.BlockSpec(blk, lambda a, rb, cb, sc: (a + sc[0] * (a_n if ha == 0 else 0), rb + sc[0] * (r_n // tr if ha == 1 else 0), cb))
    return _call(body, f"pair_add_{d.name}", (a_n, r_n // tr, c_n // tc), [mine, same], [same, same],
                 [_sds(d.pair3, F32), _sds(d.pair3, BF16)], prefetch=1)(scal, g_full, got)


def _chip_reduce(d, pair_f32, got, scal):
    tr, tc = _ew_tiles(d)
    a_n, r_n, c_n = d.part3
    ha, sa = d.half_axis, d.shard_axis

    def body(sc_ref, p_ref, r0, r1, r2, o_ref):
        o_ref[...] = ((p_ref[...] + r0[...].astype(F32)) + r1[...].astype(F32)) + r2[...].astype(F32)

    blk = (1, tr, tc)
    own = pl.BlockSpec(blk, lambda a, rb, sc: (a + sc[1] * (a_n if sa == 0 else 0), rb + sc[1] * (r_n // tr if sa == 1 else 0),
                                               sc[1] if sa == 2 else 0))
    slot = lambda k: pl.BlockSpec((None,) + blk, lambda a, rb, sc: (k, a, rb, 0))
    out = pl.BlockSpec(blk, lambda a, rb, sc: (a + sc[0] * (a_n if ha == 0 else 0), rb + sc[0] * (r_n // tr if ha == 1 else 0), 0))
    return _call(body, f"chip_reduce_{d.name}", (a_n, r_n // tr), [own, slot(0), slot(1), slot(2)], [out],
                 [_sds(d.shard3, F32)], prefetch=1)(scal, pair_f32, got, got, got)[0]


def _adam_math(g, w, m, v):
    m = ADAM_B1 * m + (1.0 - ADAM_B1) * g
    v = ADAM_B2 * v + (1.0 - ADAM_B2) * (g * g)
    m_hat = m / (1.0 - ADAM_B1 ** ADAM_STEP)
    v_hat = v / (1.0 - ADAM_B2 ** ADAM_STEP)
    delta = -ADAM_LR * (m_hat / (jnp.sqrt(v_hat) + ADAM_EPS) + ADAM_WD * w)
    return delta, m, v


def _adam(d, g, w, m, v, prev=None):
    tr, tc = _ew_tiles(d)
    a_n, r_n, c_n = d.shard3
    n_prev = 0 if prev is None else 4

    def body(g_ref, w_ref, m_ref, v_ref, *rest):
        go_ref, d_ref, mo_ref, vo_ref = rest[n_prev:]
        gv = g_ref[...]
        go_ref[...] = gv
        d_ref[...], mo_ref[...], vo_ref[...] = _adam_math(gv, w_ref[...], m_ref[...], v_ref[...])

    plain = pl.BlockSpec((1, tr, tc), lambda a, rb: (a, rb, 0))
    if d.slab is None:
        wspec, shape = plain, d.shard3
    else:
        wspec, shape = pl.BlockSpec((None, 1, tr, tc), lambda a, rb: (d.slab, a, rb, 0)), (4,) + d.shard3
    in_specs = [plain] + [wspec] * 3
    args = [g, w, m, v]
    aliases = None
    if prev is not None:
        in_specs += [pl.BlockSpec(memory_space=pl.ANY)] * 4
        args += list(prev)
        aliases = {4 + k: k for k in range(4)}
    return _call(body, f"adam_{d.name}", (a_n, r_n // tr), in_specs, [wspec] * 4, [_sds(shape, F32)] * 4, aliases=aliases)(*args)


def _adam_small(gs, ws, ms, vs):
    n = len(gs)

    def body(*refs):
        for i in range(n):
            g, w, m, v = (refs[k * n + i][...] for k in range(4))
            d, mo, vo = _adam_math(g, w, m, v)
            refs[4 * n + i][...] = d
            refs[5 * n + i][...] = mo
            refs[6 * n + i][...] = vo

    vm = pl.BlockSpec(memory_space=pltpu.VMEM)
    outs = pl.pallas_call(body, name="adam_small", in_specs=[vm] * (4 * n), out_specs=[vm] * (3 * n),
                          out_shape=[_sds(g.shape, F32) for g in gs] * 3)(*gs, *ws, *ms, *vs)
    return outs[:n], outs[n:2 * n], outs[2 * n:]


def _allreduce_small(packed):
    rows, cols = packed.shape
    others = [(dx, dy, dc) for dx in (0, 1) for dy in (0, 1) for dc in (0, 1) if (dx, dy, dc) != (0, 0, 0)]

    def body(in_ref, out_ref, buf, ssem, rsem):
        x, y, c, _ = _place()
        lin = 4 * x + 2 * y + c
        buf[lin] = in_ref[...]
        cps = []
        for k, (dx, dy, dc) in enumerate(others):
            px = 1 - x if dx else x
            py = 1 - y if dy else y
            pc = 1 - c if dc else c
            cps.append((pltpu.make_async_remote_copy(src_ref=in_ref, dst_ref=buf.at[lin], send_sem=ssem.at[k], recv_sem=rsem.at[k],
                                                     device_id=(px, py, pc), device_id_type=MESH), 4 * px + 2 * py + pc))
        for cp, _ in cps:
            cp.start()
        for k, (cp, plin) in enumerate(cps):
            pltpu.make_async_remote_copy(src_ref=in_ref, dst_ref=buf.at[plin], send_sem=ssem.at[k], recv_sem=rsem.at[k],
                                         device_id=(x, y, c), device_id_type=MESH).wait_recv()
        for cp, _ in cps:
            cp.wait_send()
        acc = buf[0]
        for dev in range(1, 8):
            acc = acc + buf[dev]
        out_ref[...] = acc

    vm = pl.BlockSpec(memory_space=pltpu.VMEM)
    return pl.pallas_call(
        body, name="allreduce_small", in_specs=[vm], out_specs=vm, out_shape=_sds((rows, cols), F32),
        scratch_shapes=[pltpu.VMEM((8, rows, cols), F32), pltpu.SemaphoreType.DMA((7,)), pltpu.SemaphoreType.DMA((7,))],
        compiler_params=pltpu.CompilerParams(has_side_effects=True),
    )(packed)


_BIG = ("ffn_w_in", "ffn_w_out", "conv_w_pw1", "conv_w_pw2", "attn_w_qkv", "attn_w_o")
_SMALL = ("norm_g", "conv_b_pw1", "conv_w_dw", "conv_b_dw", "conv_norm_g", "conv_b_pw2", "attn_q_norm", "attn_k_norm")
_NAMES = ("norm_g", "ffn_w_in", "ffn_w_out", "conv_w_pw1", "conv_b_pw1", "conv_w_dw", "conv_b_dw", "conv_norm_g", "conv_w_pw2",
          "conv_b_pw2", "attn_w_qkv", "attn_q_norm", "attn_k_norm", "attn_w_o")


def _rows8(a, width):
    a = a.reshape(-1, min(a.shape[-1], width))
    return jnp.pad(a, ((0, -a.shape[0] % 8), (0, width - a.shape[1])))


def kernel(x, norm_g, ffn_w_in, ffn_w_out, conv_w_pw1, conv_b_pw1, conv_w_dw, conv_b_dw, conv_norm_g, conv_w_pw2, conv_b_pw2, attn_w_qkv, attn_q_norm, attn_k_norm, attn_w_o, loss_target, m_norm_g, m_ffn_w_in, m_ffn_w_out, m_conv_w_pw1, m_conv_b_pw1, m_conv_w_dw, m_conv_b_dw, m_conv_norm_g, m_conv_w_pw2, m_conv_b_pw2, m_attn_w_qkv, m_attn_q_norm, m_attn_k_norm, m_attn_w_o, v_norm_g, v_ffn_w_in, v_ffn_w_out, v_conv_w_pw1, v_conv_b_pw1, v_conv_w_dw, v_conv_b_dw, v_conv_norm_g, v_conv_w_pw2, v_conv_b_pw2, v_attn_w_qkv, v_attn_q_norm, v_attn_k_norm, v_attn_w_o):
    w = dict(zip(_NAMES, (norm_g, ffn_w_in, ffn_w_out, conv_w_pw1, conv_b_pw1, conv_w_dw, conv_b_dw, conv_norm_g, conv_w_pw2,
                          conv_b_pw2, attn_w_qkv, attn_q_norm, attn_k_norm, attn_w_o)))
    m = dict(zip(_NAMES, (m_norm_g, m_ffn_w_in, m_ffn_w_out, m_conv_w_pw1, m_conv_b_pw1, m_conv_w_dw, m_conv_b_dw, m_conv_norm_g,
                          m_conv_w_pw2, m_conv_b_pw2, m_attn_w_qkv, m_attn_q_norm, m_attn_k_norm, m_attn_w_o)))
    v = dict(zip(_NAMES, (v_norm_g, v_ffn_w_in, v_ffn_w_out, v_conv_w_pw1, v_conv_b_pw1, v_conv_w_dw, v_conv_b_dw, v_conv_norm_g,
                          v_conv_w_pw2, v_conv_b_pw2, v_attn_w_qkv, v_attn_q_norm, v_attn_k_norm, v_attn_w_o)))
    T, D = x.shape[1:]
    F = ffn_w_out.shape[2] * N_CHIPS
    HE = attn_w_o.shape[1] * N_CHIPS
    cx, cy, cc = lax.axis_index("x"), lax.axis_index("y"), lax.axis_index("c")
    me = 2 * cx + cy
    scal = jnp.stack([cc, me]).astype(jnp.int32)

    ffn_in = lambda lf: _Sharded(f"ffn_w_in_{lf}", (2, D // 2, 2 * F), 0, 2, "ffn_w_in", lf)
    ffn_out = lambda lf: _Sharded(f"ffn_w_out_{lf}", (4, F // 4, D), 1, 0, "ffn_w_out", lf)
    items = [
        ffn_in(0), ffn_out(0),
        _Sharded("conv_w_pw1", (2, D // 2, 2 * D), 0, 2, "conv_w_pw1"), _Sharded("conv_w_pw2", (4, D // 4, D), 1, 0, "conv_w_pw2"),
        ffn_in(1), ffn_out(1), ffn_in(2), ffn_out(2),
        _Sharded("attn_w_qkv", (2, D // 2, 9 * HE), 0, 2, "attn_w_qkv"), _Sharded("attn_w_o", (4, HE // 4, D), 1, 0, "attn_w_o"),
        ffn_in(3), ffn_out(3),
    ]
    mat_shapes = {"ffn_w_in": (D, 2 * F), "ffn_w_out": (F, D), "conv_w_pw1": (D, 2 * D), "conv_w_pw2": (D, D),
                  "attn_w_qkv": (D, 9 * HE), "attn_w_o": (HE, D)}

    def as_shards(a, n):
        it = next(i for i in items if i.src == n)
        return a.reshape(((4,) if it.slab is not None else ()) + it.shard3)

    norm_full, dw_full = _gather_small([norm_g.reshape(6, D // 4), conv_w_dw.reshape(CONV_WIDTH, D // 4)])
    shards = {n: as_shards(w[n], n).astype(BF16) for n in _BIG}
    ssems, rsems, shards, fulls = _gather_start(items, shards, _place_shards(items, shards))

    def block_weights(k, after):
        sel = slice(2 * k, 2 * k + 2)
        got, fsems, gsems = _gather_forward(items[sel], fulls[sel], ssems[sel], rsems[sel], shards, after, f"gather_forward_{k}")
        done = _gather_finish(items[sel], got, fsems, gsems, f"gather_finish_{k}")
        return [a.reshape(mat_shapes[it.src]) for a, it in zip(done, items[sel])]

    sq, dx, grads = _local_step(x[0], loss_target[0], norm_full.reshape(2, 3, D), conv_b_pw1, dw_full, conv_b_dw, conv_norm_g,
                                conv_b_pw2, attn_q_norm[0], attn_k_norm[0], block_weights)
    loss = lax.psum(0.5 * jnp.sum(sq) / D, ("x", "y", "c"))

    pick = lambda it, k: (grads[it.src] if it.slab is None else grads[it.src][it.slab])[k].reshape(it.full3)
    g32 = [pick(it, 0) for it in items]
    g16 = [pick(it, 1) for it in items]
    got = _pair_send_halves(items, g16)
    pairs = [_pair_add(it, g, r, scal) for it, g, r in zip(items, g32, got)]
    got = _chip_send_parts(items, [p[1] for p in pairs])
    red = [_chip_reduce(it, p[0], r, scal) for it, p, r in zip(items, pairs, got)]
    gsh = _pair_fill(items, red)

    out_g, out_d, out_m, out_v = {}, {}, {}, {}
    res = {}
    for it, g in zip(items, gsh):
        n = it.src
        res[n] = _adam(it, g, as_shards(w[n], n), as_shards(m[n], n), as_shards(v[n], n), prev=res.get(n))
    for n in _BIG:
        out_g[n], out_d[n], out_m[n], out_v[n] = (a.reshape(w[n].shape) for a in res[n])

    parts = [_rows8(grads[n], D) for n in _SMALL]
    tot = _allreduce_small(jnp.concatenate(parts, axis=0))
    sg, r0 = {}, 0
    for n, p in zip(_SMALL, parts):
        last = grads[n].shape[-1]
        g = tot[r0:r0 + grads[n].size // min(last, D), :min(last, D)].reshape(-1, last)
        r0 += p.shape[0]
        if n in ("norm_g", "conv_w_dw"):
            g = lax.dynamic_slice_in_dim(g, me * (D // 4), D // 4, axis=1)
        sg[n] = g
    flat = lambda a: a.reshape(-1, a.shape[-1])
    ds, ms, vs = _adam_small([sg[n] for n in _SMALL], [flat(w[n]) for n in _SMALL], [flat(m[n]) for n in _SMALL],
                             [flat(v[n]) for n in _SMALL])
    for i, n in enumerate(_SMALL):
        out_g[n], out_d[n], out_m[n], out_v[n] = (a.reshape(w[n].shape) for a in (sg[n], ds[i], ms[i], vs[i]))

    return (loss, dx[None], *[out_g[n] for n in _NAMES], *[out_d[n] for n in _NAMES], *[out_m[n] for n in _NAMES],
            *[out_v[n] for n in _NAMES])
```

```python
import functools

import jax
import jax.numpy as jnp
from jax import lax
from jax.experimental import pallas as pl
from jax.experimental.pallas import tpu as pltpu

F32 = jnp.float32
BF16 = jnp.bfloat16
MESH = pl.DeviceIdType.MESH

NORM_EPS = 1e-6
CONV_WIDTH = 31
ATTN_GROUPS = ((128, 1), (512, 4), (2048, 16))
ATTN_BLOCK = 128
HEAD_DIM = 128
N_CHIPS = 4

ADAM_LR = 0.001
ADAM_B1 = 0.9
ADAM_B2 = 0.999
ADAM_EPS = 1e-08
ADAM_WD = 0.01
ADAM_STEP = 10

VMEM_LIMIT = 56 * 1024 * 1024
NT_DIMS = (((1,), (1,)), ((), ()))
TN_DIMS = (((0,), (0,)), ((), ()))


def _pick(n, pref, mult):
    t = (min(n, pref) // mult) * mult
    while t >= mult:
        if n % t == 0:
            return t
        t -= mult
    return n


def _call(body, name, grid, in_specs, out_specs, out_shape, scratch=(), aliases=None, prefetch=0, after=()):
    params = pltpu.CompilerParams(dimension_semantics=("arbitrary",) * len(grid), vmem_limit_bytes=VMEM_LIMIT)
    after = tuple(after)
    if after:
        inner, n_in = body, prefetch + len(in_specs)

        def body(*refs):
            return inner(*refs[:n_in], *refs[n_in + len(after):])

        in_specs = list(in_specs) + [pl.BlockSpec(memory_space=pl.ANY)] * len(after)
    if prefetch:
        spec = pltpu.PrefetchScalarGridSpec(
            num_scalar_prefetch=prefetch, grid=grid, in_specs=in_specs, out_specs=out_specs, scratch_shapes=list(scratch)
        )
        call = pl.pallas_call(body, name=name, grid_spec=spec, out_shape=out_shape, compiler_params=params,
                              input_output_aliases=aliases or {})
    else:
        call = pl.pallas_call(body, name=name, grid=grid, in_specs=in_specs, out_specs=out_specs, out_shape=out_shape,
                              scratch_shapes=list(scratch), compiler_params=params, input_output_aliases=aliases or {})
    return lambda *args: call(*args, *after)


def _sds(shape, dtype):
    return jax.ShapeDtypeStruct(shape, dtype)


def _sig(x):
    return 1.0 / (1.0 + jnp.exp(-x))


def _rstd(x):
    return lax.rsqrt(jnp.mean(x * x, axis=-1, keepdims=True) + NORM_EPS)


def _norm_bwd(dy, xhat, r, g):
    dxh = dy * g
    return r * (dxh - xhat * jnp.mean(dxh * xhat, axis=-1, keepdims=True))


def _dot(a, b):
    return jnp.dot(a, b, preferred_element_type=F32)


def _dot_nt(a, b):
    return lax.dot_general(a, b, NT_DIMS, preferred_element_type=F32)


def _dot_tn(a, b):
    return lax.dot_general(a, b, TN_DIMS, preferred_element_type=F32)


def _ffn_tile(F):
    return _pick(F, 256, 128)


def _ffn_fwd(x, g, w_in, w_out, name):
    T, D = x.shape
    F = w_out.shape[0]
    tf = _ffn_tile(F)
    nf = F // tf
    tm = _pick(T, 1024, 8)

    def body(x_ref, g_ref, wg_ref, wu_ref, wo_ref, xo_ref, gu_ref, h_sc, acc_sc):
        j = pl.program_id(1)

        @pl.when(j == 0)
        def _():
            xv = x_ref[...]
            h_sc[...] = (xv * _rstd(xv) * g_ref[...]).astype(BF16)
            acc_sc[...] = jnp.zeros_like(acc_sc)

        h = h_sc[...]
        gate = _dot(h, wg_ref[...])
        up = _dot(h, wu_ref[...])
        gu_ref[:, :tf] = gate.astype(BF16)
        gu_ref[:, tf:] = up.astype(BF16)
        a = (gate * _sig(gate) * up).astype(BF16)
        acc_sc[...] += _dot(a, wo_ref[...])

        @pl.when(j == nf - 1)
        def _():
            xo_ref[...] = x_ref[...] + 0.5 * acc_sc[...]

    return _call(
        body, name, (T // tm, nf),
        [pl.BlockSpec((tm, D), lambda i, j: (i, 0)),
         pl.BlockSpec((1, D), lambda i, j: (0, 0)),
         pl.BlockSpec((D, tf), lambda i, j: (0, j)),
         pl.BlockSpec((D, tf), lambda i, j: (0, nf + j)),
         pl.BlockSpec((tf, D), lambda i, j: (j, 0))],
        [pl.BlockSpec((tm, D), lambda i, j: (i, 0)),
         pl.BlockSpec((tm, 2 * tf), lambda i, j: (i, j))],
        [_sds((T, D), F32), _sds((T, 2 * F), BF16)],
        scratch=[pltpu.VMEM((tm, D), BF16), pltpu.VMEM((tm, D), F32)],
    )(x, g, w_in, w_in, w_out)


def _ffn_bwd(x, g, dy, gu, w_in, w_out, name, after=()):
    T, D = x.shape
    F = w_out.shape[0]
    tf = _ffn_tile(F)
    nf = F // tf
    tm = _pick(T, 512, 8)

    def body(x_ref, g_ref, dy_ref, gu_ref, wg_ref, wu_ref, wo_ref, dx_ref, dg_ref, dgu_ref, a_ref, h_ref, dyb_ref, dh_sc):
        i = pl.program_id(0)
        j = pl.program_id(1)

        @pl.when(j == 0)
        def _():
            xv = x_ref[...]
            h_ref[...] = (xv * _rstd(xv) * g_ref[...]).astype(BF16)
            dyb_ref[...] = (0.5 * dy_ref[...]).astype(BF16)
            dh_sc[...] = jnp.zeros_like(dh_sc)

        @pl.when((i == 0) & (j == 0))
        def _():
            dg_ref[...] = jnp.zeros_like(dg_ref)

        gate = gu_ref[:, :tf].astype(F32)
        up = gu_ref[:, tf:].astype(F32)
        sg = _sig(gate)
        sl = gate * sg
        a_ref[...] = (sl * up).astype(BF16)
        da = _dot_nt(dyb_ref[...], wo_ref[...])
        dgate = (da * up * (sg * (1.0 + gate * (1.0 - sg)))).astype(BF16)
        dup = (da * sl).astype(BF16)
        dgu_ref[:, :tf] = dgate
        dgu_ref[:, tf:] = dup
        dh_sc[...] += _dot_nt(dgate, wg_ref[...]) + _dot_nt(dup, wu_ref[...])

        @pl.when(j == nf - 1)
        def _():
            xv = x_ref[...]
            r = _rstd(xv)
            xh = xv * r
            dh = dh_sc[...]
            dx_ref[...] = dy_ref[...] + _norm_bwd(dh, xh, r, g_ref[...])
            dg_ref[...] += jnp.sum(dh * xh, axis=0, keepdims=True)

    return _call(
        body, name, (T // tm, nf),
        [pl.BlockSpec((tm, D), lambda i, j: (i, 0)),
         pl.BlockSpec((1, D), lambda i, j: (0, 0)),
         pl.BlockSpec((tm, D), lambda i, j: (i, 0)),
         pl.BlockSpec((tm, 2 * tf), lambda i, j: (i, j)),
         pl.BlockSpec((D, tf), lambda i, j: (0, j)),
         pl.BlockSpec((D, tf), lambda i, j: (0, nf + j)),
         pl.BlockSpec((tf, D), lambda i, j: (j, 0))],
        [pl.BlockSpec((tm, D), lambda i, j: (i, 0)),
         pl.BlockSpec((1, D), lambda i, j: (0, 0)),
         pl.BlockSpec((tm, 2 * tf), lambda i, j: (i, j)),
         pl.BlockSpec((tm, tf), lambda i, j: (i, j)),
         pl.BlockSpec((tm, D), lambda i, j: (i, 0)),
         pl.BlockSpec((tm, D), lambda i, j: (i, 0))],
        [_sds((T, D), F32), _sds((1, D), F32), _sds((T, 2 * F), BF16), _sds((T, F), BF16), _sds((T, D), BF16),
         _sds((T, D), BF16)],
        scratch=[pltpu.VMEM((tm, D), F32)], after=after,
    )(x, g, dy, gu, w_in, w_in, w_out)


def _mm_tn(a, b, bm, bn, out_shape, out_block, out_map, scale, name, prev=None):
    K, M = a.shape
    N = b.shape[1]

    def body(a_ref, b_ref, *rest):
        o_ref, ob_ref = rest[-2:]
        o = _dot_tn(a_ref[...].astype(BF16), b_ref[...].astype(BF16)) * scale
        o_ref[...] = o
        ob_ref[...] = o.astype(BF16)

    in_specs = [pl.BlockSpec((K, bm), lambda mi, ni: (0, mi)), pl.BlockSpec((K, bn), lambda mi, ni: (0, ni))]
    args = [a, b]
    aliases = None
    if prev is not None:
        in_specs += [pl.BlockSpec(memory_space=pl.ANY)] * 2
        args += list(prev)
        aliases = {2: 0, 3: 1}
    ospec = pl.BlockSpec(out_block, out_map)
    return _call(body, name, (M // bm, N // bn), in_specs, [ospec, ospec],
                 [_sds(out_shape, F32), _sds(out_shape, BF16)], aliases=aliases)(*args)


def _conv_pre(x, g, w1, b1, name):
    T, D = x.shape
    tm = _pick(T, 512, 8)

    def body(x_ref, g_ref, w_ref, b_ref, ag_ref, u_ref, h_ref):
        xv = x_ref[...]
        h = (xv * _rstd(xv) * g_ref[...]).astype(BF16)
        h_ref[...] = h
        ag = _dot(h, w_ref[...]) + b_ref[...]
        ag_ref[...] = ag.astype(BF16)
        u_ref[...] = ag[:, :D] * _sig(ag[:, D:])

    return _call(
        body, name, (T // tm,),
        [pl.BlockSpec((tm, D), lambda i: (i, 0)), pl.BlockSpec((1, D), lambda i: (0, 0)),
         pl.BlockSpec((D, 2 * D), lambda i: (0, 0)), pl.BlockSpec((1, 2 * D), lambda i: (0, 0))],
        [pl.BlockSpec((tm, 2 * D), lambda i: (i, 0)), pl.BlockSpec((tm, D), lambda i: (i, 0)),
         pl.BlockSpec((tm, D), lambda i: (i, 0))],
        [_sds((T, 2 * D), BF16), _sds((T, D), F32), _sds((T, D), BF16)],
    )(x, g, w1, b1)


_DW_PAD = 32
_DW_CHUNK = 256


def _dwconv(u, w, b, name):
    T, D = u.shape
    K = w.shape[0]
    ch = _pick(T, _DW_CHUNK, 8)
    lead = _DW_PAD - (K - 1)

    def body(u_ref, w_ref, b_ref, c_ref, ext):
        ext[pl.ds(0, _DW_PAD), :] = jnp.zeros((_DW_PAD, 128), F32)
        ext[pl.ds(_DW_PAD, T), :] = u_ref[...]
        for c0 in range(0, T, ch):
            acc = jnp.zeros((ch, 128), F32) + b_ref[...]
            for k in range(K):
                acc = acc + w_ref[pl.ds(k, 1), :] * ext[pl.ds(c0 + lead + k, ch), :]
            c_ref[pl.ds(c0, ch), :] = acc

    return _call(
        body, name, (D // 128,),
        [pl.BlockSpec((T, 128), lambda i: (0, i)), pl.BlockSpec((K, 128), lambda i: (0, i)),
         pl.BlockSpec((1, 128), lambda i: (0, i))],
        [pl.BlockSpec((T, 128), lambda i: (0, i))],
        [_sds((T, D), F32)],
        scratch=[pltpu.VMEM((T + _DW_PAD, 128), F32)],
    )(u, w, b)[0]


def _dwconv_bwd(dc, u, w, name):
    T, D = u.shape
    K = w.shape[0]
    ch = _pick(T, _DW_CHUNK, 8)
    lead = _DW_PAD - (K - 1)

    def body(dc_ref, u_ref, w_ref, du_ref, dw_ref, db_ref, uext, dext):
        uext[pl.ds(0, _DW_PAD), :] = jnp.zeros((_DW_PAD, 128), F32)
        uext[pl.ds(_DW_PAD, T), :] = u_ref[...]
        dext[pl.ds(0, T), :] = dc_ref[...]
        dext[pl.ds(T, _DW_PAD), :] = jnp.zeros((_DW_PAD, 128), F32)
        dws =[jnp.zeros((8, 128), F32) for _ in range(K)]
        dbs = jnp.zeros((8, 128), F32)
        for c0 in range(0, T, ch):
            dcv = dext[pl.ds(c0, ch), :]
            dbs = dbs + jnp.sum(dcv.reshape(ch // 8, 8, 128), axis=0)
            acc = jnp.zeros((ch, 128), F32)
            for k in range(K):
                acc = acc + w_ref[pl.ds(k, 1), :] * dext[pl.ds(c0 + (K - 1) - k, ch), :]
                prod = dcv * uext[pl.ds(c0 + lead + k, ch), :]
                dws[k] = dws[k] + jnp.sum(prod.reshape(ch // 8, 8, 128), axis=0)
            du_ref[pl.ds(c0, ch), :] = acc
        for k in range(K):
            dw_ref[pl.ds(k, 1), :] = jnp.sum(dws[k], axis=0, keepdims=True)
        db_ref[...] = jnp.sum(dbs, axis=0, keepdims=True)

    return _call(
        body, name, (D // 128,),
        [pl.BlockSpec((T, 128), lambda i: (0, i)), pl.BlockSpec((T, 128), lambda i: (0, i)),
         pl.BlockSpec((K, 128), lambda i: (0, i))],
        [pl.BlockSpec((T, 128), lambda i: (0, i)), pl.BlockSpec((K, 128), lambda i: (0, i)),
         pl.BlockSpec((1, 128), lambda i: (0, i))],
        [_sds((T, D), F32), _sds((K, D), F32), _sds((1, D), F32)],
        scratch=[pltpu.VMEM((T + _DW_PAD, 128), F32), pltpu.VMEM((T + _DW_PAD, 128), F32)],
    )(dc, u, w)


def _conv_post(c, x, ng, w2, b2, name):
    T, D = x.shape
    tm = _pick(T, 512, 8)

    def body(c_ref, x_ref, ng_ref, w_ref, b_ref, xo_ref, s_ref):
        cv = c_ref[...]
        n = cv * _rstd(cv) * ng_ref[...]
        s = (n * _sig(n)).astype(BF16)
        s_ref[...] = s
        xo_ref[...] = x_ref[...] + _dot(s, w_ref[...]) + b_ref[...]

    row = lambda i: (i, 0)
    fix = lambda i: (0, 0)
    return _call(
        body, name, (T // tm,),
        [pl.BlockSpec((tm, D), row), pl.BlockSpec((tm, D), row), pl.BlockSpec((1, D), fix),
         pl.BlockSpec((D, D), fix), pl.BlockSpec((1, D), fix)],
        [pl.BlockSpec((tm, D), row), pl.BlockSpec((tm, D), row)],
        [_sds((T, D), F32), _sds((T, D), BF16)],
    )(c, x, ng, w2, b2)


def _conv_post_bwd(dy, c, ng, w2, name, after=()):
    T, D = dy.shape
    tm = _pick(T, 512, 8)

    def body(dy_ref, c_ref, ng_ref, w_ref, dc_ref, dng_ref, db_ref):
        @pl.when(pl.program_id(0) == 0)
        def _():
            dng_ref[...] = jnp.zeros_like(dng_ref)
            db_ref[...] = jnp.zeros_like(db_ref)

        dyv = dy_ref[...]
        ds = _dot_nt(dyv.astype(BF16), w_ref[...])
        cv = c_ref[...]
        r = _rstd(cv)
        ch = cv * r
        n = ch * ng_ref[...]
        sg = _sig(n)
        dn = ds * (sg * (1.0 + n * (1.0 - sg)))
        dc_ref[...] = _norm_bwd(dn, ch, r, ng_ref[...])
        dng_ref[...] += jnp.sum(dn * ch, axis=0, keepdims=True)
        db_ref[...] += jnp.sum(dyv, axis=0, keepdims=True)

    row = lambda i: (i, 0)
    fix = lambda i: (0, 0)
    return _call(
        body, name, (T // tm,),
        [pl.BlockSpec((tm, D), row), pl.BlockSpec((tm, D), row), pl.BlockSpec((1, D), fix), pl.BlockSpec((D, D), fix)],
        [pl.BlockSpec((tm, D), row), pl.BlockSpec((1, D), fix), pl.BlockSpec((1, D), fix)],
        [_sds((T, D), F32), _sds((1, D), F32), _sds((1, D), F32)], after=after,
    )(dy, c, ng, w2)


def _conv_pre_bwd(du, ag, x, g, dy, w1, name):
    T, D = x.shape
    tm = _pick(T, 512, 8)

    def body(du_ref, ag_ref, x_ref, g_ref, dy_ref, w_ref, dx_ref, dg_ref, dag_ref, db_ref):
        @pl.when(pl.program_id(0) == 0)
        def _():
            dg_ref[...] = jnp.zeros_like(dg_ref)
            db_ref[...] = jnp.zeros_like(db_ref)

        duv = du_ref[...]
        a = ag_ref[:, :D].astype(F32)
        gt = ag_ref[:, D:].astype(F32)
        sg = _sig(gt)
        da = duv * sg
        dgt = duv * a * sg * (1.0 - sg)
        db_ref[:, :D] += jnp.sum(da, axis=0, keepdims=True)
        db_ref[:, D:] += jnp.sum(dgt, axis=0, keepdims=True)
        dab = da.astype(BF16)
        dgb = dgt.astype(BF16)
        dag_ref[:, :D] = dab
        dag_ref[:, D:] = dgb
        dh = _dot_nt(dab, w_ref[:, :D]) + _dot_nt(dgb, w_ref[:, D:])
        xv = x_ref[...]
        r = _rstd(xv)
        xh = xv * r
        dx_ref[...] = dy_ref[...] + _norm_bwd(dh, xh, r, g_ref[...])
        dg_ref[...] += jnp.sum(dh * xh, axis=0, keepdims=True)

    row = lambda i: (i, 0)
    fix = lambda i: (0, 0)
    return _call(
        body, name, (T // tm,),
        [pl.BlockSpec((tm, D), row), pl.BlockSpec((tm, 2 * D), row), pl.BlockSpec((tm, D), row), pl.BlockSpec((1, D), fix),
         pl.BlockSpec((tm, D), row), pl.BlockSpec((D, 2 * D), fix)],
        [pl.BlockSpec((tm, D), row), pl.BlockSpec((1, D), fix), pl.BlockSpec((tm, 2 * D), row),
         pl.BlockSpec((1, 2 * D), fix)],
        [_sds((T, D), F32), _sds((1, D), F32), _sds((T, 2 * D), BF16), _sds((1, 2 * D), F32)],
    )(du, ag, x, g, dy, w1)


def _attn_qkv(x, g, wqkv, name):
    T, D = x.shape
    N = wqkv.shape[1]
    tn = N // 9
    tm = _pick(T, 512, 8)

    def body(x_ref, g_ref, w_ref, o_ref, h_ref):
        @pl.when(pl.program_id(1) == 0)
        def _():
            xv = x_ref[...]
            h_ref[...] = (xv * _rstd(xv) * g_ref[...]).astype(BF16)

        o_ref[...] = _dot(h_ref[...], w_ref[...]).astype(BF16)

    return _call(
        body, name, (T // tm, 9),
        [pl.BlockSpec((tm, D), lambda i, j: (i, 0)), pl.BlockSpec((1, D), lambda i, j: (0, 0)),
         pl.BlockSpec((D, tn), lambda i, j: (0, j))],
        [pl.BlockSpec((tm, tn), lambda i, j: (i, j)), pl.BlockSpec((tm, D), lambda i, j: (i, 0))],
        [_sds((T, N), BF16), _sds((T, D), BF16)],
    )(x, g, wqkv)


def _band_masks(q, steps, nblk):
    i = lax.broadcasted_iota(jnp.int32, (q, q), 0)
    j = lax.broadcasted_iota(jnp.int32, (q, q), 1)
    cur = i - j >= 0
    if steps < q - 1:
        cur = cur & (i - j <= steps)
    prev = q + i - j <= jnp.where(nblk > 0, steps, -1)
    return cur, prev


def _attn_fwd(qkv, qn_g, kn_g, grp, window, dil, name):
    T, C = qkv.shape
    HE = C // 9
    H = HE // HEAD_DIM
    E = HEAD_DIM
    Q = ATTN_BLOCK
    L = T // dil
    nb = L // Q
    steps = window // dil
    scale = E ** -0.5
    view = qkv.reshape(L, dil * C)

    def body(q_ref, kc_ref, kp_ref, vc_ref, vp_ref, gq_ref, gk_ref, o_ref, l_ref):
        n = pl.program_id(1)
        cur, prev = _band_masks(Q, steps, n)
        gq = gq_ref[...]
        gk = gk_ref[...]
        for h in range(H):
            hs = slice(h * E, (h + 1) * E)
            q = q_ref[:, hs].astype(F32)
            kc = kc_ref[:, hs].astype(F32)
            kp = kp_ref[:, hs].astype(F32)
            qn = (q * _rstd(q) * gq).astype(BF16)
            kcn = (kc * _rstd(kc) * gk).astype(BF16)
            kpn = (kp * _rstd(kp) * gk).astype(BF16)
            sc = jnp.where(cur, _dot_nt(qn, kcn) * scale, -1e30)
            sp = jnp.where(prev, _dot_nt(qn, kpn) * scale, -1e30)
            m = jnp.maximum(jnp.max(sc, axis=-1, keepdims=True), jnp.max(sp, axis=-1, keepdims=True))
            pc = jnp.exp(sc - m)
            pp = jnp.exp(sp - m)
            l = jnp.sum(pc, axis=-1, keepdims=True) + jnp.sum(pp, axis=-1, keepdims=True)
            inv = 1.0 / l
            o = _dot((pc * inv).astype(BF16), vc_ref[:, hs]) + _dot((pp * inv).astype(BF16), vp_ref[:, hs])
            o_ref[:, hs] = o
            l_ref[:, hs] = jnp.broadcast_to(m + jnp.log(l), (Q, E))

    base = grp * 3
    blk = lambda s, back: pl.BlockSpec((Q, HE), lambda r, n: (jnp.maximum(n - back, 0), r * 9 + base + s))
    vec = pl.BlockSpec((1, E), lambda r, n: (0, 0))
    out = pl.BlockSpec((Q, HE), lambda r, n: (n, r))
    o, lse = _call(
        body, name, (dil, nb),
        [blk(0, 0), blk(1, 0), blk(1, 1), blk(2, 0), blk(2, 1), vec, vec],
        [out, out],
        [_sds((L, dil * HE), F32), _sds((L, dil * HE), F32)],
    )(view, view, view, view, view, qn_g, kn_g)
    return o.reshape(T, HE), lse.reshape(T, HE)


def _attn_merge(os, lses, x, wo, name):
    T, D = x.shape
    HE = wo.shape[0]
    tm = _pick(T, 512, 8)
    ng = len(os)

    def body(*refs):
        o_refs = refs[:ng]
        l_refs = refs[ng:2 * ng]
        x_ref, w_ref, xo_ref, om_ref, lt_ref = refs[2 * ng:]
        ls = [r[...] for r in l_refs]
        m = functools.reduce(jnp.maximum, ls)
        es = [jnp.exp(l - m) for l in ls]
        tot = functools.reduce(lambda a, b: a + b, es)
        inv = 1.0 / tot
        om = functools.reduce(lambda a, b: a + b, [e * inv * r[...] for e, r in zip(es, o_refs)])
        omb = om.astype(BF16)
        om_ref[...] = omb
        lt_ref[...] = m + jnp.log(tot)
        xo_ref[...] = x_ref[...] + _dot(omb, w_ref[...])

    row = lambda i: (i, 0)
    fix = lambda i: (0, 0)
    return _call(
        body, name, (T // tm,),
        [pl.BlockSpec((tm, HE), row)] * (2 * ng) + [pl.BlockSpec((tm, D), row), pl.BlockSpec((HE, D), fix)],
        [pl.BlockSpec((tm, D), row), pl.BlockSpec((tm, HE), row), pl.BlockSpec((tm, HE), row)],
        [_sds((T, D), F32), _sds((T, HE), BF16), _sds((T, HE), F32)],
    )(*os, *lses, x, wo)


def _attn_out_bwd(dy, om, wo, name, after=()):
    T, D = dy.shape
    HE = wo.shape[0]
    E = HEAD_DIM
    tm = _pick(T, 512, 8)

    def body(dy_ref, om_ref, w_ref, dom_ref, dl_ref):
        dom = _dot_nt(dy_ref[...].astype(BF16), w_ref[...])
        dom_ref[...] = dom.astype(BF16)
        prod = dom * om_ref[...].astype(F32)
        for h in range(HE // E):
            hs = slice(h * E, (h + 1) * E)
            dl_ref[:, hs] = jnp.broadcast_to(jnp.sum(prod[:, hs], axis=-1, keepdims=True), (tm, E))

    row = lambda i: (i, 0)
    return _call(
        body, name, (T // tm,),
        [pl.BlockSpec((tm, D), row), pl.BlockSpec((tm, HE), row), pl.BlockSpec((HE, D), lambda i: (0, 0))],
        [pl.BlockSpec((tm, HE), row), pl.BlockSpec((tm, HE), row)],
        [_sds((T, HE), BF16), _sds((T, HE), F32)], after=after,
    )(dy, om, wo)


def _attn_bwd(qkv, dom, lse, delta, qn_g, kn_g, grp, window, dil, name):
    T, C = qkv.shape
    HE = C // 9
    H = HE // HEAD_DIM
    E = HEAD_DIM
    Q = ATTN_BLOCK
    L = T // dil
    nb = L // Q
    steps = window // dil
    scale = E ** -0.5
    view = qkv.reshape(L, dil * C)
    sview = lambda a: a.reshape(L, dil * HE)

    def body(q_ref, kc_ref, kp_ref, vc_ref, vp_ref, do_ref, l_ref, dl_ref, gq_ref, gk_ref,
             dq_ref, dk_ref, dv_ref, dgq_ref, dgk_ref, ck_sc, cv_sc):
        r_id = pl.program_id(0)
        n = pl.program_id(1)

        @pl.when((r_id == 0) & (n == 0))
        def _():
            dgq_ref[...] = jnp.zeros_like(dgq_ref)
            dgk_ref[...] = jnp.zeros_like(dgk_ref)

        @pl.when(n == 0)
        def _():
            ck_sc[...] = jnp.zeros_like(ck_sc)
            cv_sc[...] = jnp.zeros_like(cv_sc)

        @pl.when(n < nb)
        def _():
            cur, prev = _band_masks(Q, steps, n)
            gq = gq_ref[...]
            gk = gk_ref[...]
            dgq = jnp.zeros((1, E), F32)
            dgk = jnp.zeros((1, E), F32)
            for h in range(H):
                hs = slice(h * E, (h + 1) * E)
                q = q_ref[:, hs].astype(F32)
                kc = kc_ref[:, hs].astype(F32)
                kp = kp_ref[:, hs].astype(F32)
                rq, rc, rp = _rstd(q), _rstd(kc), _rstd(kp)
                qh, kch, kph = q * rq, kc * rc, kp * rp
                qn = (qh * gq).astype(BF16)
                kcn = (kch * gk).astype(BF16)
                kpn = (kph * gk).astype(BF16)
                lcol = l_ref[:, hs][:, :1]
                dcol = dl_ref[:, hs][:, :1]
                do = do_ref[:, hs]
                pc = jnp.where(cur, jnp.exp(_dot_nt(qn, kcn) * scale - lcol), 0.0)
                pp = jnp.where(prev, jnp.exp(_dot_nt(qn, kpn) * scale - lcol), 0.0)
                dsc = (pc * (_dot_nt(do, vc_ref[:, hs]) - dcol) * scale).astype(BF16)
                dsp = (pp * (_dot_nt(do, vp_ref[:, hs]) - dcol) * scale).astype(BF16)
                dvc = _dot_tn(pc.astype(BF16), do)
                dvp = _dot_tn(pp.astype(BF16), do)
                dqn = _dot(dsc, kcn) + _dot(dsp, kpn)
                dkcn = _dot_tn(dsc, qn)
                dkpn = _dot_tn(dsp, qn)
                dq_ref[:, hs] = _norm_bwd(dqn, qh, rq, gq).astype(BF16)
                dgq = dgq + jnp.sum(dqn * qh, axis=0, keepdims=True)
                dgk = dgk + jnp.sum(dkcn * kch + dkpn * kph, axis=0, keepdims=True)
                dk_ref[:, hs] = (ck_sc[:, hs] + _norm_bwd(dkpn, kph, rp, gk)).astype(BF16)
                dv_ref[:, hs] = (cv_sc[:, hs] + dvp).astype(BF16)
                ck_sc[:, hs] = _norm_bwd(dkcn, kch, rc, gk)
                cv_sc[:, hs] = dvc
            dgq_ref[...] += dgq
            dgk_ref[...] += dgk

        @pl.when(n == nb)
        def _():
            dk_ref[...] = ck_sc[...].astype(BF16)
            dv_ref[...] = cv_sc[...].astype(BF16)

    base = grp * 3
    nq = lambda n: jnp.minimum(n, nb - 1)
    blk = lambda s, back: pl.BlockSpec((Q, HE), lambda r, n: (jnp.maximum(nq(n) - back, 0), r * 9 + base + s))
    qblk = pl.BlockSpec((Q, HE), lambda r, n: (nq(n), r))
    kblk = pl.BlockSpec((Q, HE), lambda r, n: (jnp.maximum(n - 1, 0), r))
    vec = pl.BlockSpec((1, E), lambda r, n: (0, 0))
    dq, dk, dv, dgq, dgk = _call(
        body, name, (dil, nb + 1),
        [blk(0, 0), blk(1, 0), blk(1, 1), blk(2, 0), blk(2, 1), qblk, qblk, qblk, vec, vec],
        [qblk, kblk, kblk, vec, vec],
        [_sds((L, dil * HE), BF16)] * 3 + [_sds((1, E), F32)] * 2,
        scratch=[pltpu.VMEM((Q, HE), F32), pltpu.VMEM((Q, HE), F32)],
    )(view, view, view, view, view, sview(dom), sview(lse), sview(delta), qn_g, kn_g)
    return dq.reshape(T, HE), dk.reshape(T, HE), dv.reshape(T, HE), dgq, dgk


def _attn_qkv_bwd(dqkv, x, g, dy, wqkv, name):
    T, D = x.shape
    HE = wqkv.shape[1] // 9
    tm = _pick(T, 512, 8)

    def body(*refs):
        d_refs = refs[:9]
        x_ref, g_ref, dy_ref, w_ref, dx_ref, dg_ref, dh_sc = refs[9:]
        i = pl.program_id(0)
        j = pl.program_id(1)

        @pl.when((i == 0) & (j == 0))
        def _():
            dg_ref[...] = jnp.zeros_like(dg_ref)

        @pl.when(j == 0)
        def _():
            dh_sc[...] = jnp.zeros_like(dh_sc)

        for s in range(9):
            @pl.when(j == s)
            def _(s=s):
                dh_sc[...] += _dot_nt(d_refs[s][...], w_ref[...])

        @pl.when(j == 8)
        def _():
            xv = x_ref[...]
            r = _rstd(xv)
            xh = xv * r
            dh = dh_sc[...]
            dx_ref[...] = dy_ref[...] + _norm_bwd(dh, xh, r, g_ref[...])
            dg_ref[...] += jnp.sum(dh * xh, axis=0, keepdims=True)

    row = lambda i, j: (i, 0)
    fix = lambda i, j: (0, 0)
    return _call(
        body, name, (T // tm, 9),
        [pl.BlockSpec((tm, HE), row)] * 9 + [pl.BlockSpec((tm, D), row), pl.BlockSpec((1, D), fix), pl.BlockSpec((tm, D), row),
                                           pl.BlockSpec((D, HE), lambda i, j: (0, j))],
        [pl.BlockSpec((tm, D), row), pl.BlockSpec((1, D), fix)],
        [_sds((T, D), F32), _sds((1, D), F32)],
        scratch=[pltpu.VMEM((tm, D), F32)],
    )(*dqkv, x, g, dy, wqkv)


def _loss_head(y, target, name):
    T, D = y.shape
    tm = _pick(T, 512, 8)

    def body(y_ref, t_ref, dy_ref, sq_ref):
        @pl.when(pl.program_id(0) == 0)
        def _():
            sq_ref[...] = jnp.zeros_like(sq_ref)

        err = y_ref[...] - t_ref[...]
        dy_ref[...] = err * (1.0 / D)
        sq_ref[...] += jnp.sum(err * err, axis=0, keepdims=True)

    row = lambda i: (i, 0)
    return _call(
        body, name, (T // tm,),
        [pl.BlockSpec((tm, D), row), pl.BlockSpec((tm, D), row)],
        [pl.BlockSpec((tm, D), row), pl.BlockSpec((1, D), lambda i: (0, 0))],
        [_sds((T, D), F32), _sds((1, D), F32)],
    )(y, target)


def _local_step(x, target, norm_g, b_pw1, w_dw, b_dw, cng, b_pw2, qn, kn, block_weights, grads_ready):
    T, D = x.shape
    ng = lambda l, k: norm_g[l, k][None, :]
    bn = _pick(D, 256, 128)

    w_in, w_out = [None] * 4, [None] * 4
    w_in[0], w_out[0] = block_weights(0, x)
    x1, gu0 = _ffn_fwd(x, ng(0, 0), w_in[0], w_out[0], "ffn_fwd_0")
    pw1, pw2 = block_weights(1, x1)
    ag, u, hc = _conv_pre(x1, ng(0, 1), pw1, b_pw1, "conv_pre")
    c = _dwconv(u, w_dw, b_dw, "dwconv")
    x2, s = _conv_post(c, x1, cng, pw2, b_pw2, "conv_post")
    w_in[1], w_out[1] = block_weights(2, x2)
    x3, gu1 = _ffn_fwd(x2, ng(0, 2), w_in[1], w_out[1], "ffn_fwd_1")
    w_in[2], w_out[2] = block_weights(3, x3)
    x4, gu2 = _ffn_fwd(x3, ng(1, 0), w_in[2], w_out[2], "ffn_fwd_2")
    wqkv, wo = block_weights(4, x4)
    HE = wo.shape[0]
    F = w_out[0].shape[0]
    tf = _ffn_tile(F)
    nf = F // tf
    bh = _pick(HE, 512, 128)
    qkv, ha = _attn_qkv(x4, ng(1, 1), wqkv, "attn_qkv")
    os, lses = [], []
    for gi, (window, dil) in enumerate(ATTN_GROUPS):
        o, l = _attn_fwd(qkv, qn[gi][None, :], kn[gi][None, :], gi, window, dil, f"attn_fwd_{gi}")
        os.append(o)
        lses.append(l)
    x5, om, lse = _attn_merge(os, lses, x4, wo, "attn_merge")
    w_in[3], w_out[3] = block_weights(5, x5)
    x6, gu3 = _ffn_fwd(x5, ng(1, 2), w_in[3], w_out[3], "ffn_fwd_3")
    dy, sq = _loss_head(x6, target, "loss_head")

    grads = {"ffn_w_in": [None] * 4, "ffn_w_out": [None] * 4}
    dnorm = [[None] * 3 for _ in range(2)]

    def ffn_back(lf, k, xin, gvec, dy, gu, after):
        dx, dg, dgu, a, h, dyb = _ffn_bwd(xin, gvec, dy, gu, w_in[lf], w_out[lf], f"ffn_bwd_{lf}", after=after)
        grads["ffn_w_in"][lf] = _mm_tn(h, dgu, D, tf, (D, 2 * F), (D, tf), lambda mi, ni: (0, (ni % 2) * nf + ni // 2), 1.0,
                                       f"ffn_dw_in_{lf}")
        grads["ffn_w_out"][lf] = _mm_tn(a, dyb, tf, D, (F, D), (tf, D), lambda mi, ni: (mi, 0), 1.0, f"ffn_dw_out_{lf}")
        return dx, dg, grads_ready(k, (grads["ffn_w_in"][lf], grads["ffn_w_out"][lf]), dx)

    dx, dnorm[1][2], tok = ffn_back(3, 5, x5, ng(1, 2), dy, gu3, ())
    dom, delta = _attn_out_bwd(dx, om, wo, "attn_out_bwd", after=tok)
    grads["attn_w_o"] = _mm_tn(om, dx, HE, bn, (HE, D), (HE, bn), lambda mi, ni: (0, ni), 1.0, "attn_dw_o")
    dqkv, dgq, dgk = [], [], []
    for gi, (window, dil) in enumerate(ATTN_GROUPS):
        dq, dk, dv, a_, b_ = _attn_bwd(qkv, dom, lse, delta, qn[gi][None, :], kn[gi][None, :], gi, window, dil, f"attn_bwd_{gi}")
        dqkv += [dq, dk, dv]
        dgq.append(a_)
        dgk.append(b_)
    grads["attn_q_norm"] = jnp.concatenate(dgq, axis=0)
    grads["attn_k_norm"] = jnp.concatenate(dgk, axis=0)
    d_qkv = None
    for s9 in range(9):
        d_qkv = _mm_tn(ha, dqkv[s9], D, bh, (D, 9 * HE), (D, bh), lambda mi, ni, s9=s9: (0, s9 * (HE // bh) + ni), 1.0,
                       f"attn_dw_qkv_{s9}", prev=d_qkv)
    grads["attn_w_qkv"] = d_qkv
    dx, dnorm[1][1] = _attn_qkv_bwd(dqkv, x4, ng(1, 1), dx, wqkv, "attn_qkv_bwd")
    tok = grads_ready(4, (grads["attn_w_qkv"], grads["attn_w_o"]), dx)
    dx, dnorm[1][0], tok = ffn_back(2, 3, x3, ng(1, 0), dx, gu2, tok)

    dx, dnorm[0][2], tok = ffn_back(1, 2, x2, ng(0, 2), dx, gu1, tok)
    dc, grads["conv_norm_g"], grads["conv_b_pw2"] = _conv_post_bwd(dx, c, cng, pw2, "conv_post_bwd", after=tok)
    grads["conv_w_pw2"] = _mm_tn(s, dx, D, bn, (D, D), (D, bn), lambda mi, ni: (0, ni), 1.0, "conv_dw_pw2")
    du, grads["conv_w_dw"], grads["conv_b_dw"] = _dwconv_bwd(dc, u, w_dw, "dwconv_bwd")
    dx, dnorm[0][1], dag, grads["conv_b_pw1"] = _conv_pre_bwd(du, ag, x1, ng(0, 1), dx, pw1, "conv_pre_bwd")
    grads["conv_w_pw1"] = _mm_tn(hc, dag, D, 2 * bn, (D, 2 * D), (D, 2 * bn), lambda mi, ni: (0, ni), 1.0, "conv_dw_pw1")
    tok = grads_ready(1, (grads["conv_w_pw1"], grads["conv_w_pw2"]), dx)
    dx, dnorm[0][0], _ = ffn_back(0, 0, x, ng(0, 0), dx, gu0, tok)

    grads["norm_g"] = jnp.concatenate([jnp.concatenate(r, axis=0)[None] for r in dnorm], axis=0)
    return sq, dx, grads


class _Sharded:
    def __init__(self, name, full3, half_axis, shard_axis, src, slab=None):
        self.name, self.full3, self.half_axis, self.shard_axis = name, tuple(full3), half_axis, shard_axis
        self.src, self.slab = src, slab

    def source(self, refs):
        return refs[self.src] if self.slab is None else refs[self.src].at[self.slab]

    def _cut(self, shape, axis, parts):
        s = list(shape)
        s[axis] //= parts
        return tuple(s)

    @property
    def shard3(self):
        return self._cut(self.full3, self.shard_axis, N_CHIPS)

    @property
    def pair3(self):
        return self._cut(self.full3, self.half_axis, 2)

    @property
    def part3(self):
        return self._cut(self.shard3, self.half_axis, 2)

    @staticmethod
    def _slice(ref, axis, idx, parts):
        n = ref.shape[axis] // parts
        start = idx * n
        minor = len(ref.shape) - 1 - axis
        if minor < 2 and not isinstance(start, int):
            start = pl.multiple_of(start, 128 if minor == 0 else (16 if n % 16 == 0 else 8))
        sl = [slice(None)] * len(ref.shape)
        sl[axis] = pl.ds(start, n)
        return ref.at[tuple(sl)]

    def half(self, ref, h):
        return self._slice(ref, self.half_axis, h, 2)

    def shard(self, ref, j):
        return self._slice(ref, self.shard_axis, j, N_CHIPS)


def _place():
    x, y, c = lax.axis_index("x"), lax.axis_index("y"), lax.axis_index("c")
    return x, y, c, 2 * x + y


_RELS = (1, 2, 3)


def _peer(x, y, rel):
    px = 1 - x if rel & 2 else x
    py = 1 - y if rel & 1 else y
    return px, py, 2 * px + py


ANY = pl.BlockSpec(memory_space=pl.ANY)


def _comm_call(body, name, n_in, out_shape, n_sems, aliases=None):
    return pl.pallas_call(
        body, name=name, in_specs=[ANY] * n_in, out_specs=[ANY] * len(out_shape), out_shape=out_shape,
        scratch_shapes=[pltpu.SemaphoreType.DMA((n,)) for n in n_sems],
        input_output_aliases=aliases or {},
        compiler_params=pltpu.CompilerParams(has_side_effects=True),
    )


def _remote(src, dst, send_sem, recv_sem, dev):
    return pltpu.make_async_remote_copy(src_ref=src, dst_ref=dst, send_sem=send_sem, recv_sem=recv_sem, device_id=dev,
                                        device_id_type=MESH)


def _gather_small(small_shards):
    ns = len(small_shards)

    def body(*refs):
        ins, outs = refs[:ns], refs[ns:2 * ns]
        lsem, ssem, rsem = refs[2 * ns:]
        x, y, c, me = _place()
        cols = lambda ref, j: _Sharded._slice(ref, 1, j, N_CHIPS)
        local = [pltpu.make_async_copy(ins[i], cols(outs[i], me), lsem.at[i]) for i in range(ns)]
        sends = []
        for i in range(ns):
            for k, rel in enumerate(_RELS):
                px, py, _ = _peer(x, y, rel)
                sends.append(_remote(ins[i], cols(outs[i], me), ssem.at[3 * i + k], rsem.at[3 * i + k], (px, py, c)))
        for cp in local + sends:
            cp.start()
        for i in range(ns):
            for k, rel in enumerate(_RELS):
                _, _, pj = _peer(x, y, rel)
                got = cols(outs[i], pj)
                _remote(got, got, ssem.at[3 * i + k], rsem.at[3 * i + k], (x, y, c)).wait_recv()
        for cp in sends:
            cp.wait_send()
        for cp in local:
            cp.wait()

    out_shape = [_sds((s.shape[0], s.shape[1] * N_CHIPS), F32) for s in small_shards]
    return _comm_call(body, "gather_small", ns, out_shape, [ns, 3 * ns, 3 * ns])(*small_shards)


HBM = pl.BlockSpec(memory_space=pltpu.HBM)
SEM = pl.BlockSpec(memory_space=pltpu.SEMAPHORE)
DATAFLOW = pltpu.SideEffectType.DATAFLOW_SIDE_EFFECTING


def _in_hbm(a):
    return pltpu.with_memory_space_constraint(a, pltpu.HBM)


def _cast_place(it, shard, scal):
    a_n, r_n, c_n = it.shard3
    tr = _pick(r_n, 256, 16)
    sa = it.shard_axis

    def body(sc_ref, s_ref, o_ref):
        o_ref[...] = s_ref[...].astype(BF16)

    if it.slab is None:
        src = pl.BlockSpec((1, tr, c_n), lambda a, rb, sc: (a, rb, 0))
    else:
        src = pl.BlockSpec((None, 1, tr, c_n), lambda a, rb, sc: (it.slab, a, rb, 0))
    dst = pl.BlockSpec((1, tr, c_n), lambda a, rb, sc: (a + sc[1] * (a_n if sa == 0 else 0), rb + sc[1] * (r_n // tr if sa == 1 else 0),
                                                       sc[1] if sa == 2 else 0))
    return _call(body, f"cast_place_{it.name}", (a_n, r_n // tr), [src], [dst], [_sds(it.full3, BF16)], prefetch=1)(scal, shard)[0]


def _gather_start(items, fulls):
    ni = len(items)

    def body(*refs):
        outs = refs[ni:]
        ssem, rsem, full = outs[:ni], outs[ni:2 * ni], outs[2 * ni:3 * ni]
        x, y, c, me = _place()
        for i, it in enumerate(items):
            mine = it.half(it.shard(full[i], me), c)
            for k, rel in enumerate(_RELS):
                px, py, _ = _peer(x, y, rel)
                _remote(mine, mine, ssem[i].at[k], rsem[i].at[k], (px, py, c)).start()

    outs = pl.pallas_call(
        body, name="gather_start", in_specs=[HBM] * ni, out_specs=[SEM] * (2 * ni) + [HBM] * ni,
        out_shape=[pltpu.SemaphoreType.DMA((3,))] * (2 * ni) + [pltpu.HBM(it.full3, BF16) for it in items],
        input_output_aliases={j: 2 * ni + j for j in range(ni)},
        compiler_params=pltpu.CompilerParams(has_side_effects=DATAFLOW),
    )(*[_in_hbm(f) for f in fulls])
    return outs[:ni], outs[ni:2 * ni], outs[2 * ni:]


def _gather_forward(items, fulls, ssems, rsems, after, name):
    ni = len(items)

    def body(*refs):
        ssem, rsem = refs[ni:2 * ni], refs[2 * ni:3 * ni]
        outs = refs[3 * ni + 1:]
        full, fsem, gsem = outs[:ni], outs[ni:2 * ni], outs[2 * ni:3 * ni]
        x, y, c, me = _place()
        sib = (x, y, 1 - c)
        for i, it in enumerate(items):
            for k, rel in enumerate(_RELS):
                _, _, pj = _peer(x, y, rel)
                got = it.half(it.shard(full[i], pj), c)
                _remote(got, got, ssem[i].at[k], rsem[i].at[k], sib).wait_recv()
                _remote(got, got, fsem[i].at[k], gsem[i].at[k], sib).start()
        for i, it in enumerate(items):
            mine = it.half(it.shard(full[i], me), c)
            for k in range(3):
                _remote(mine, mine, ssem[i].at[k], rsem[i].at[k], sib).wait_send()

    outs = pl.pallas_call(
        body, name=name, in_specs=[HBM] * ni + [SEM] * (2 * ni) + [ANY],
        out_specs=[HBM] * ni + [SEM] * (2 * ni),
        out_shape=[pltpu.HBM(it.full3, BF16) for it in items] + [pltpu.SemaphoreType.DMA((3,))] * (2 * ni),
        input_output_aliases={i: i for i in range(ni)},
        compiler_params=pltpu.CompilerParams(has_side_effects=DATAFLOW),
    )(*fulls, *ssems, *rsems, after)
    return outs[:ni], outs[ni:2 * ni], outs[2 * ni:]


def _gather_finish(items, fulls, fsems, gsems, name):
    ni = len(items)

    def body(*refs):
        fsem, gsem = refs[ni:2 * ni], refs[2 * ni:3 * ni]
        full = refs[3 * ni:]
        x, y, c, _ = _place()
        sib = (x, y, 1 - c)
        for i, it in enumerate(items):
            for k, rel in enumerate(_RELS):
                _, _, pj = _peer(x, y, rel)
                got = it.half(it.shard(full[i], pj), 1 - c)
                _remote(got, got, fsem[i].at[k], gsem[i].at[k], sib).wait_recv()
                sent = it.half(it.shard(full[i], pj), c)
                _remote(sent, sent, fsem[i].at[k], gsem[i].at[k], sib).wait_send()

    return pl.pallas_call(
        body, name=name, in_specs=[HBM] * ni + [SEM] * (2 * ni), out_specs=[HBM] * ni,
        out_shape=[pltpu.HBM(it.full3, BF16) for it in items],
        input_output_aliases={i: i for i in range(ni)},
        compiler_params=pltpu.CompilerParams(has_side_effects=DATAFLOW),
    )(*fulls, *fsems, *gsems)


def _exchange_start(name, arrays, n_sems, copies):
    na, ns = len(arrays), len(n_sems)

    def body(*refs):
        outs = refs[na:]
        ssem, rsem, thru, token = outs[:ns], outs[ns:2 * ns], outs[2 * ns:2 * ns + na], outs[-1]
        for send, _ in copies(thru, ssem, rsem):
            send.start()
        token[...] = jnp.zeros_like(token)

    outs = pl.pallas_call(
        body, name=name, in_specs=[HBM] * na,
        out_specs=[SEM] * (2 * ns) + [HBM] * na + [pl.BlockSpec(memory_space=pltpu.VMEM)],
        out_shape=[pltpu.SemaphoreType.DMA((n,)) for n in n_sems] * 2 + [pltpu.HBM(a.shape, a.dtype) for a in arrays]
        + [_sds((8, 128), F32)],
        input_output_aliases={j: 2 * ns + j for j in range(na)},
        compiler_params=pltpu.CompilerParams(has_side_effects=DATAFLOW),
    )(*[_in_hbm(a) for a in arrays])
    return outs[:ns], outs[ns:2 * ns], outs[2 * ns:2 * ns + na], outs[-1]


def _exchange_wait(name, arrays, ssems, rsems, copies, after):
    na, ns = len(arrays), len(ssems)

    def body(*refs):
        ssem, rsem = refs[na:na + ns], refs[na + ns:na + 2 * ns]
        thru = refs[na + 2 * ns + 1:]
        for send, recv in copies(thru, ssem, rsem):
            recv.wait_recv()
            send.wait_send()

    return pl.pallas_call(
        body, name=name, in_specs=[HBM] * na + [SEM] * (2 * ns) + [ANY], out_specs=[HBM] * na,
        out_shape=[pltpu.HBM(a.shape, a.dtype) for a in arrays],
        input_output_aliases={i: i for i in range(na)},
        compiler_params=pltpu.CompilerParams(has_side_effects=DATAFLOW),
    )(*arrays, *ssems, *rsems, after)


def _pair_copies(items):
    n = len(items)

    def copies(a, ssem, rsem):
        x, y, c, _ = _place()
        cps = [_remote(it.half(a[i], 1 - c), a[n + i], ssem[i].at[0], rsem[i].at[0], (x, y, 1 - c)) for i, it in enumerate(items)]
        return [(cp, cp) for cp in cps]

    return copies


def _chip_copies(items):
    n = len(items)

    def copies(a, ssem, rsem):
        x, y, c, _ = _place()
        cps = []
        for i, it in enumerate(items):
            for k, rel in enumerate(_RELS):
                px, py, pj = _peer(x, y, rel)
                cps.append(_remote(it.shard(a[i], pj), a[n + i].at[k], ssem[i].at[k], rsem[i].at[k], (px, py, c)))
        return [(cp, cp) for cp in cps]

    return copies


def _fill_copies(items):
    def copies(a, ssem, rsem):
        x, y, c, _ = _place()
        sib = (x, y, 1 - c)
        out = []
        for i, it in enumerate(items):
            mine, other = it.half(a[i], c), it.half(a[i], 1 - c)
            out.append((_remote(mine, mine, ssem[i].at[0], rsem[i].at[0], sib), _remote(other, other, ssem[i].at[0], rsem[i].at[0], sib)))
        return out

    return copies


def _ew_tiles(d):
    _, rows, cols = d.part3
    return _pick(rows, 256, 16), cols


def _pair_add(d, g_full, got, scal):
    tr, tc = _ew_tiles(d)
    a_n, r_n, c_n = d.pair3
    ha = d.half_axis

    def body(sc_ref, g_ref, r_ref, o_ref, ob_ref):
        s = g_ref[...] + r_ref[...].astype(F32)
        o_ref[...] = s
        ob_ref[...] = s.astype(BF16)

    blk = (1, tr, tc)
    same = pl.BlockSpec(blk, lambda a, rb, cb, sc: (a, rb, cb))
    mine = pl.BlockSpec(blk, lambda a, rb, cb, sc: (a + sc[0] * (a_n if ha == 0 else 0), rb + sc[0] * (r_n // tr if ha == 1 else 0), cb))
    return _call(body, f"pair_add_{d.name}", (a_n, r_n // tr, c_n // tc), [mine, same], [same, same],
                 [_sds(d.pair3, F32), _sds(d.pair3, BF16)], prefetch=1)(scal, g_full, got)


def _chip_reduce(d, pair_f32, got, scal):
    tr, tc = _ew_tiles(d)
    a_n, r_n, c_n = d.part3
    ha, sa = d.half_axis, d.shard_axis

    def body(sc_ref, p_ref, r0, r1, r2, o_ref):
        o_ref[...] = ((p_ref[...] + r0[...].astype(F32)) + r1[...].astype(F32)) + r2[...].astype(F32)

    blk = (1, tr, tc)
    own = pl.BlockSpec(blk, lambda a, rb, sc: (a + sc[1] * (a_n if sa == 0 else 0), rb + sc[1] * (r_n // tr if sa == 1 else 0),
                                               sc[1] if sa == 2 else 0))
    slot = lambda k: pl.BlockSpec((None,) + blk, lambda a, rb, sc: (k, a, rb, 0))
    out = pl.BlockSpec(blk, lambda a, rb, sc: (a + sc[0] * (a_n if ha == 0 else 0), rb + sc[0] * (r_n // tr if ha == 1 else 0), 0))
    return _call(body, f"chip_reduce_{d.name}", (a_n, r_n // tr), [own, slot(0), slot(1), slot(2)], [out],
                 [_sds(d.shard3, F32)], prefetch=1)(scal, pair_f32, got, got, got)[0]


def _adam_math(g, w, m, v):
    m = ADAM_B1 * m + (1.0 - ADAM_B1) * g
    v = ADAM_B2 * v + (1.0 - ADAM_B2) * (g * g)
    m_hat = m / (1.0 - ADAM_B1 ** ADAM_STEP)
    v_hat = v / (1.0 - ADAM_B2 ** ADAM_STEP)
    delta = -ADAM_LR * (m_hat / (jnp.sqrt(v_hat) + ADAM_EPS) + ADAM_WD * w)
    return delta, m, v


def _adam(d, g, w, m, v, prev=None):
    tr, tc = _ew_tiles(d)
    a_n, r_n, c_n = d.shard3
    n_prev = 0 if prev is None else 4

    def body(g_ref, w_ref, m_ref, v_ref, *rest):
        go_ref, d_ref, mo_ref, vo_ref = rest[n_prev:]
        gv = g_ref[...]
        go_ref[...] = gv
        d_ref[...], mo_ref[...], vo_ref[...] = _adam_math(gv, w_ref[...], m_ref[...], v_ref[...])

    plain = pl.BlockSpec((1, tr, tc), lambda a, rb: (a, rb, 0))
    if d.slab is None:
        wspec, shape = plain, d.shard3
    else:
        wspec, shape = pl.BlockSpec((None, 1, tr, tc), lambda a, rb: (d.slab, a, rb, 0)), (4,) + d.shard3
    in_specs = [plain] + [wspec] * 3
    args = [g, w, m, v]
    aliases = None
    if prev is not None:
        in_specs += [pl.BlockSpec(memory_space=pl.ANY)] * 4
        args += list(prev)
        aliases = {4 + k: k for k in range(4)}
    return _call(body, f"adam_{d.name}", (a_n, r_n // tr), in_specs, [wspec] * 4, [_sds(shape, F32)] * 4, aliases=aliases)(*args)


def _adam_small(gs, ws, ms, vs):
    n = len(gs)

    def body(*refs):
        for i in range(n):
            g, w, m, v = (refs[k * n + i][...] for k in range(4))
            d, mo, vo = _adam_math(g, w, m, v)
            refs[4 * n + i][...] = d
            refs[5 * n + i][...] = mo
            refs[6 * n + i][...] = vo

    vm = pl.BlockSpec(memory_space=pltpu.VMEM)
    outs = pl.pallas_call(body, name="adam_small", in_specs=[vm] * (4 * n), out_specs=[vm] * (3 * n),
                          out_shape=[_sds(g.shape, F32) for g in gs] * 3)(*gs, *ws, *ms, *vs)
    return outs[:n], outs[n:2 * n], outs[2 * n:]


def _allreduce_small(packed):
    rows, cols = packed.shape
    others = [(dx, dy, dc) for dx in (0, 1) for dy in (0, 1) for dc in (0, 1) if (dx, dy, dc) != (0, 0, 0)]

    def body(in_ref, out_ref, buf, ssem, rsem):
        x, y, c, _ = _place()
        lin = 4 * x + 2 * y + c
        buf[lin] = in_ref[...]
        cps = []
        for k, (dx, dy, dc) in enumerate(others):
            px = 1 - x if dx else x
            py = 1 - y if dy else y
            pc = 1 - c if dc else c
            cps.append((pltpu.make_async_remote_copy(src_ref=in_ref, dst_ref=buf.at[lin], send_sem=ssem.at[k], recv_sem=rsem.at[k],
                                                     device_id=(px, py, pc), device_id_type=MESH), 4 * px + 2 * py + pc))
        for cp, _ in cps:
            cp.start()
        for k, (cp, plin) in enumerate(cps):
            pltpu.make_async_remote_copy(src_ref=in_ref, dst_ref=buf.at[plin], send_sem=ssem.at[k], recv_sem=rsem.at[k],
                                         device_id=(x, y, c), device_id_type=MESH).wait_recv()
        for cp, _ in cps:
            cp.wait_send()
        acc = buf[0]
        for dev in range(1, 8):
            acc = acc + buf[dev]
        out_ref[...] = acc

    vm = pl.BlockSpec(memory_space=pltpu.VMEM)
    return pl.pallas_call(
        body, name="allreduce_small", in_specs=[vm], out_specs=vm, out_shape=_sds((rows, cols), F32),
        scratch_shapes=[pltpu.VMEM((8, rows, cols), F32), pltpu.SemaphoreType.DMA((7,)), pltpu.SemaphoreType.DMA((7,))],
        compiler_params=pltpu.CompilerParams(has_side_effects=True),
    )(packed)


_BIG = ("ffn_w_in", "ffn_w_out", "conv_w_pw1", "conv_w_pw2", "attn_w_qkv", "attn_w_o")
_SMALL = ("norm_g", "conv_b_pw1", "conv_w_dw", "conv_b_dw", "conv_norm_g", "conv_b_pw2", "attn_q_norm", "attn_k_norm")
_NAMES = ("norm_g", "ffn_w_in", "ffn_w_out", "conv_w_pw1", "conv_b_pw1", "conv_w_dw", "conv_b_dw", "conv_norm_g", "conv_w_pw2",
          "conv_b_pw2", "attn_w_qkv", "attn_q_norm", "attn_k_norm", "attn_w_o")


def _rows8(a, width):
    a = a.reshape(-1, min(a.shape[-1], width))
    return jnp.pad(a, ((0, -a.shape[0] % 8), (0, width - a.shape[1])))


def kernel(x, norm_g, ffn_w_in, ffn_w_out, conv_w_pw1, conv_b_pw1, conv_w_dw, conv_b_dw, conv_norm_g, conv_w_pw2, conv_b_pw2, attn_w_qkv, attn_q_norm, attn_k_norm, attn_w_o, loss_target, m_norm_g, m_ffn_w_in, m_ffn_w_out, m_conv_w_pw1, m_conv_b_pw1, m_conv_w_dw, m_conv_b_dw, m_conv_norm_g, m_conv_w_pw2, m_conv_b_pw2, m_attn_w_qkv, m_attn_q_norm, m_attn_k_norm, m_attn_w_o, v_norm_g, v_ffn_w_in, v_ffn_w_out, v_conv_w_pw1, v_conv_b_pw1, v_conv_w_dw, v_conv_b_dw, v_conv_norm_g, v_conv_w_pw2, v_conv_b_pw2, v_attn_w_qkv, v_attn_q_norm, v_attn_k_norm, v_attn_w_o):
    w = dict(zip(_NAMES, (norm_g, ffn_w_in, ffn_w_out, conv_w_pw1, conv_b_pw1, conv_w_dw, conv_b_dw, conv_norm_g, conv_w_pw2,
                          conv_b_pw2, attn_w_qkv, attn_q_norm, attn_k_norm, attn_w_o)))
    m = dict(zip(_NAMES, (m_norm_g, m_ffn_w_in, m_ffn_w_out, m_conv_w_pw1, m_conv_b_pw1, m_conv_w_dw, m_conv_b_dw, m_conv_norm_g,
                          m_conv_w_pw2, m_conv_b_pw2, m_attn_w_qkv, m_attn_q_norm, m_attn_k_norm, m_attn_w_o)))
    v = dict(zip(_NAMES, (v_norm_g, v_ffn_w_in, v_ffn_w_out, v_conv_w_pw1, v_conv_b_pw1, v_conv_w_dw, v_conv_b_dw, v_conv_norm_g,
                          v_conv_w_pw2, v_conv_b_pw2, v_attn_w_qkv, v_attn_q_norm, v_attn_k_norm, v_attn_w_o)))
    T, D = x.shape[1:]
    F = ffn_w_out.shape[2] * N_CHIPS
    HE = attn_w_o.shape[1] * N_CHIPS
    cx, cy, cc = lax.axis_index("x"), lax.axis_index("y"), lax.axis_index("c")
    me = 2 * cx + cy
    scal = jnp.stack([cc, me]).astype(jnp.int32)

    ffn_in = lambda lf: _Sharded(f"ffn_w_in_{lf}", (2, D // 2, 2 * F), 0, 2, "ffn_w_in", lf)
    ffn_out = lambda lf: _Sharded(f"ffn_w_out_{lf}", (4, F // 4, D), 1, 0, "ffn_w_out", lf)
    items = [
        ffn_in(0), ffn_out(0),
        _Sharded("conv_w_pw1", (2, D // 2, 2 * D), 0, 2, "conv_w_pw1"), _Sharded("conv_w_pw2", (4, D // 4, D), 1, 0, "conv_w_pw2"),
        ffn_in(1), ffn_out(1), ffn_in(2), ffn_out(2),
        _Sharded("attn_w_qkv", (2, D // 2, 9 * HE), 0, 2, "attn_w_qkv"), _Sharded("attn_w_o", (4, HE // 4, D), 1, 0, "attn_w_o"),
        ffn_in(3), ffn_out(3),
    ]
    mat_shapes = {"ffn_w_in": (D, 2 * F), "ffn_w_out": (F, D), "conv_w_pw1": (D, 2 * D), "conv_w_pw2": (D, D),
                  "attn_w_qkv": (D, 9 * HE), "attn_w_o": (HE, D)}

    def as_shards(a, n):
        it = next(i for i in items if i.src == n)
        return a.reshape(((4,) if it.slab is not None else ()) + it.shard3)

    norm_full, dw_full = _gather_small([norm_g.reshape(6, D // 4), conv_w_dw.reshape(CONV_WIDTH, D // 4)])
    ssems, rsems, fulls = _gather_start(items, [_cast_place(it, as_shards(w[it.src], it.src), scal) for it in items])

    def block_weights(k, after):
        sel = slice(2 * k, 2 * k + 2)
        got, fsems, gsems = _gather_forward(items[sel], fulls[sel], ssems[sel], rsems[sel], after, f"gather_forward_{k}")
        done = _gather_finish(items[sel], got, fsems, gsems, f"gather_finish_{k}")
        return [a.reshape(mat_shapes[it.src]) for a, it in zip(done, items[sel])]

    res = {}
    flight = []

    def advance(entry, k, dx):
        stage, its, st = entry
        n = len(its)
        if stage == 1:
            ssem, rsem, arrs, g32 = st
            got = _exchange_wait(f"grads_pair_wait_{k}", arrs, ssem, rsem, _pair_copies(its), dx)[n:]
            sums = [_pair_add(it, g, r, scal) for it, g, r in zip(its, g32, got)]
            land = [lax.empty((3,) + it.part3, BF16) for it in its]
            ssem, rsem, arrs, tok = _exchange_start(f"grads_chip_start_{k}", [p[1] for p in sums] + land, [3] * n, _chip_copies(its))
            return (2, its, (ssem, rsem, arrs, [p[0] for p in sums])), tok
        if stage == 2:
            ssem, rsem, arrs, p32 = st
            got = _exchange_wait(f"grads_chip_wait_{k}", arrs, ssem, rsem, _chip_copies(its), dx)[n:]
            red = [_chip_reduce(it, p, r, scal) for it, p, r in zip(its, p32, got)]
            ssem, rsem, arrs, tok = _exchange_start(f"grads_fill_start_{k}", red, [1] * n, _fill_copies(its))
            return (3, its, (ssem, rsem, arrs)), tok
        ssem, rsem, arrs = st
        for it, g in zip(its, _exchange_wait(f"grads_fill_wait_{k}", arrs, ssem, rsem, _fill_copies(its), dx)):
            nm = it.src
            res[nm] = _adam(it, g, as_shards(w[nm], nm), as_shards(m[nm], nm), as_shards(v[nm], nm), prev=res.get(nm))
        return None, None

    def step_flight(dx):
        toks, left = [], []
        for k, entry in flight:
            entry, tok = advance(entry, k, dx)
            if entry is not None:
                left.append((k, entry))
                toks.append(tok)
        flight[:] = left
        return toks

    def grads_ready(k, pairs, dx):
        its = items[2 * k:2 * k + 2]
        toks = step_flight(dx)
        g32 = [p[0].reshape(it.full3) for p, it in zip(pairs, its)]
        g16 = [p[1].reshape(it.full3) for p, it in zip(pairs, its)]
        land = [lax.empty(it.pair3, BF16) for it in its]
        ssem, rsem, arrs, tok = _exchange_start(f"grads_pair_start_{k}", g16 + land, [1] * len(its), _pair_copies(its))
        flight.append((k, (1, its, (ssem, rsem, arrs, g32))))
        return toks + [tok]

    sq, dx, grads = _local_step(x[0], loss_target[0], norm_full.reshape(2, 3, D), conv_b_pw1, dw_full, conv_b_dw, conv_norm_g,
                                conv_b_pw2, attn_q_norm[0], attn_k_norm[0], block_weights, grads_ready)
    loss = lax.psum(0.5 * jnp.sum(sq) / D, ("x", "y", "c"))
    while flight:
        step_flight(dx)

    out_g, out_d, out_m, out_v = {}, {}, {}, {}
    for n in _BIG:
        out_g[n], out_d[n], out_m[n], out_v[n] = (a.reshape(w[n].shape) for a in res[n])

    parts = [_rows8(grads[n], D) for n in _SMALL]
    tot = _allreduce_small(jnp.concatenate(parts, axis=0))
    sg, r0 = {}, 0
    for n, p in zip(_SMALL, parts):
        last = grads[n].shape[-1]
        g = tot[r0:r0 + grads[n].size // min(last, D), :min(last, D)].reshape(-1, last)
        r0 += p.shape[0]
        if n in ("norm_g", "conv_w_dw"):
            g = lax.dynamic_slice_in_dim(g, me * (D // 4), D // 4, axis=1)
        sg[n] = g
    flat = lambda a: a.reshape(-1, a.shape[-1])
    ds, ms, vs = _adam_small([sg[n] for n in _SMALL], [flat(w[n]) for n in _SMALL], [flat(m[n]) for n in _SMALL],
                             [flat(v[n]) for n in _SMALL])
    for i, n in enumerate(_SMALL):
        out_g[n], out_d[n], out_m[n], out_v[n] = (a.reshape(w[n].shape) for a in (sg[n], ds[i], ms[i], vs[i]))

    return (loss, dx[None], *[out_g[n] for n in _NAMES], *[out_d[n] for n in _NAMES], *[out_m[n] for n in _NAMES],
            *[out_v[n] for n in _NAMES])
```

```python
import functools

import jax
import jax.numpy as jnp
from jax import lax
from jax.experimental import pallas as pl
from jax.experimental.pallas import tpu as pltpu

F32 = jnp.float32
BF16 = jnp.bfloat16
MESH = pl.DeviceIdType.MESH

NORM_EPS = 1e-6
CONV_WIDTH = 31
ATTN_GROUPS = ((128, 1), (512, 4), (2048, 16))
ATTN_BLOCK = 128
HEAD_DIM = 128
N_CHIPS = 4

ADAM_LR = 0.001
ADAM_B1 = 0.9
ADAM_B2 = 0.999
ADAM_EPS = 1e-08
ADAM_WD = 0.01
ADAM_STEP = 10

VMEM_LIMIT = 56 * 1024 * 1024
NT_DIMS = (((1,), (1,)), ((), ()))
TN_DIMS = (((0,), (0,)), ((), ()))


def _pick(n, pref, mult):
    t = (min(n, pref) // mult) * mult
    while t >= mult:
        if n % t == 0:
            return t
        t -= mult
    return n


def _call(body, name, grid, in_specs, out_specs, out_shape, scratch=(), aliases=None, prefetch=0, after=()):
    params = pltpu.CompilerParams(dimension_semantics=("arbitrary",) * len(grid), vmem_limit_bytes=VMEM_LIMIT)
    after = tuple(after)
    if after:
        inner, n_in = body, prefetch + len(in_specs)

        def body(*refs):
            return inner(*refs[:n_in], *refs[n_in + len(after):])

        in_specs = list(in_specs) + [pl.BlockSpec(memory_space=pl.ANY)] * len(after)
    if prefetch:
        spec = pltpu.PrefetchScalarGridSpec(
            num_scalar_prefetch=prefetch, grid=grid, in_specs=in_specs, out_specs=out_specs, scratch_shapes=list(scratch)
        )
        call = pl.pallas_call(body, name=name, grid_spec=spec, out_shape=out_shape, compiler_params=params,
                              input_output_aliases=aliases or {})
    else:
        call = pl.pallas_call(body, name=name, grid=grid, in_specs=in_specs, out_specs=out_specs, out_shape=out_shape,
                              scratch_shapes=list(scratch), compiler_params=params, input_output_aliases=aliases or {})
    return lambda *args: call(*args, *after)


def _sds(shape, dtype):
    return jax.ShapeDtypeStruct(shape, dtype)


def _sig(x):
    return 1.0 / (1.0 + jnp.exp(-x))


def _rstd(x):
    return lax.rsqrt(jnp.mean(x * x, axis=-1, keepdims=True) + NORM_EPS)


def _norm_bwd(dy, xhat, r, g):
    dxh = dy * g
    return r * (dxh - xhat * jnp.mean(dxh * xhat, axis=-1, keepdims=True))


def _dot(a, b):
    return jnp.dot(a, b, preferred_element_type=F32)


def _dot_nt(a, b):
    return lax.dot_general(a, b, NT_DIMS, preferred_element_type=F32)


def _dot_tn(a, b):
    return lax.dot_general(a, b, TN_DIMS, preferred_element_type=F32)


def _ffn_tile(F):
    return _pick(F, 1408, 128)


def _ffn_fwd(x, g, w_in, w_out, name):
    T, D = x.shape
    F = w_out.shape[0]
    tf = _ffn_tile(F)
    nf = F // tf
    tm = _pick(T, 512, 8)

    def body(x_ref, g_ref, wg_ref, wu_ref, wo_ref, xo_ref, gate_ref, up_ref, h_sc, acc_sc):
        j = pl.program_id(1)

        @pl.when(j == 0)
        def _():
            xv = x_ref[...]
            h_sc[...] = (xv * _rstd(xv) * g_ref[...]).astype(BF16)
            acc_sc[...] = jnp.zeros_like(acc_sc)

        h = h_sc[...]
        gate = _dot(h, wg_ref[...])
        up = _dot(h, wu_ref[...])
        gate_ref[...] = gate.astype(BF16)
        up_ref[...] = up.astype(BF16)
        a = (gate * _sig(gate) * up).astype(BF16)
        acc_sc[...] += _dot(a, wo_ref[...])

        @pl.when(j == nf - 1)
        def _():
            xo_ref[...] = x_ref[...] + 0.5 * acc_sc[...]

    return _call(
        body, name, (T // tm, nf),
        [pl.BlockSpec((tm, D), lambda i, j: (i, 0)),
         pl.BlockSpec((1, D), lambda i, j: (0, 0)),
         pl.BlockSpec((D, tf), lambda i, j: (0, j)),
         pl.BlockSpec((D, tf), lambda i, j: (0, nf + j)),
         pl.BlockSpec((tf, D), lambda i, j: (j, 0))],
        [pl.BlockSpec((tm, D), lambda i, j: (i, 0)),
         pl.BlockSpec((tm, tf), lambda i, j: (i, j)),
         pl.BlockSpec((tm, tf), lambda i, j: (i, j))],
        [_sds((T, D), F32), _sds((T, F), BF16), _sds((T, F), BF16)],
        scratch=[pltpu.VMEM((tm, D), BF16), pltpu.VMEM((tm, D), F32)],
    )(x, g, w_in, w_in, w_out)


def _ffn_bwd(x, g, dy, gate, up, w_in, w_out, name, after=()):
    T, D = x.shape
    F = w_out.shape[0]
    tf = _ffn_tile(F)
    nf = F // tf
    tm = _pick(T, 256, 8)

    def body(x_ref, g_ref, dy_ref, gate_ref, up_ref, wg_ref, wu_ref, wo_ref, dx_ref, dg_ref, dgate_ref, dup_ref, a_ref, h_ref,
             dyb_ref, dh_sc):
        i = pl.program_id(0)
        j = pl.program_id(1)

        @pl.when(j == 0)
        def _():
            xv = x_ref[...]
            h_ref[...] = (xv * _rstd(xv) * g_ref[...]).astype(BF16)
            dyb_ref[...] = (0.5 * dy_ref[...]).astype(BF16)
            dh_sc[...] = jnp.zeros_like(dh_sc)

        @pl.when((i == 0) & (j == 0))
        def _():
            dg_ref[...] = jnp.zeros_like(dg_ref)

        gate = gate_ref[...].astype(F32)
        up = up_ref[...].astype(F32)
        sg = _sig(gate)
        sl = gate * sg
        a_ref[...] = (sl * up).astype(BF16)
        da = _dot_nt(dyb_ref[...], wo_ref[...])
        dgate = (da * up * (sg * (1.0 + gate * (1.0 - sg)))).astype(BF16)
        dup = (da * sl).astype(BF16)
        dgate_ref[...] = dgate
        dup_ref[...] = dup
        dh_sc[...] += _dot_nt(dgate, wg_ref[...]) + _dot_nt(dup, wu_ref[...])

        @pl.when(j == nf - 1)
        def _():
            xv = x_ref[...]
            r = _rstd(xv)
            xh = xv * r
            dh = dh_sc[...]
            dx_ref[...] = dy_ref[...] + _norm_bwd(dh, xh, r, g_ref[...])
            dg_ref[...] += jnp.sum(dh * xh, axis=0, keepdims=True)

    return _call(
        body, name, (T // tm, nf),
        [pl.BlockSpec((tm, D), lambda i, j: (i, 0)),
         pl.BlockSpec((1, D), lambda i, j: (0, 0)),
         pl.BlockSpec((tm, D), lambda i, j: (i, 0)),
         pl.BlockSpec((tm, tf), lambda i, j: (i, j)),
         pl.BlockSpec((tm, tf), lambda i, j: (i, j)),
         pl.BlockSpec((D, tf), lambda i, j: (0, j)),
         pl.BlockSpec((D, tf), lambda i, j: (0, nf + j)),
         pl.BlockSpec((tf, D), lambda i, j: (j, 0))],
        [pl.BlockSpec((tm, D), lambda i, j: (i, 0)),
         pl.BlockSpec((1, D), lambda i, j: (0, 0)),
         pl.BlockSpec((tm, tf), lambda i, j: (i, j)),
         pl.BlockSpec((tm, tf), lambda i, j: (i, j)),
         pl.BlockSpec((tm, tf), lambda i, j: (i, j)),
         pl.BlockSpec((tm, D), lambda i, j: (i, 0)),
         pl.BlockSpec((tm, D), lambda i, j: (i, 0))],
        [_sds((T, D), F32), _sds((1, D), F32), _sds((T, F), BF16), _sds((T, F), BF16), _sds((T, F), BF16), _sds((T, D), BF16),
         _sds((T, D), BF16)],
        scratch=[pltpu.VMEM((tm, D), F32)], after=after,
    )(x, g, dy, gate, up, w_in, w_in, w_out)


def _mm_tn(a, b, bm, bn, out_shape, out_block, out_map, scale, name, prev=None):
    K, M = a.shape
    N = b.shape[1]

    def body(a_ref, b_ref, *rest):
        o_ref, ob_ref = rest[-2:]
        o = _dot_tn(a_ref[...].astype(BF16), b_ref[...].astype(BF16)) * scale
        o_ref[...] = o
        ob_ref[...] = o.astype(BF16)

    in_specs = [pl.BlockSpec((K, bm), lambda mi, ni: (0, mi)), pl.BlockSpec((K, bn), lambda mi, ni: (0, ni))]
    args = [a, b]
    aliases = None
    if prev is not None:
        in_specs += [pl.BlockSpec(memory_space=pl.ANY)] * 2
        args += list(prev)
        aliases = {2: 0, 3: 1}
    ospec = pl.BlockSpec(out_block, out_map)
    return _call(body, name, (M // bm, N // bn), in_specs, [ospec, ospec],
                 [_sds(out_shape, F32), _sds(out_shape, BF16)], aliases=aliases)(*args)


def _conv_pre(x, g, w1, b1, name):
    T, D = x.shape
    tm = _pick(T, 512, 8)

    def body(x_ref, g_ref, w_ref, b_ref, ag_ref, u_ref, h_ref):
        xv = x_ref[...]
        h = (xv * _rstd(xv) * g_ref[...]).astype(BF16)
        h_ref[...] = h
        ag = _dot(h, w_ref[...]) + b_ref[...]
        ag_ref[...] = ag.astype(BF16)
        u_ref[...] = ag[:, :D] * _sig(ag[:, D:])

    return _call(
        body, name, (T // tm,),
        [pl.BlockSpec((tm, D), lambda i: (i, 0)), pl.BlockSpec((1, D), lambda i: (0, 0)),
         pl.BlockSpec((D, 2 * D), lambda i: (0, 0)), pl.BlockSpec((1, 2 * D), lambda i: (0, 0))],
        [pl.BlockSpec((tm, 2 * D), lambda i: (i, 0)), pl.BlockSpec((tm, D), lambda i: (i, 0)),
         pl.BlockSpec((tm, D), lambda i: (i, 0))],
        [_sds((T, 2 * D), BF16), _sds((T, D), F32), _sds((T, D), BF16)],
    )(x, g, w1, b1)


_DW_PAD = 32
_DW_CHUNK = 256


def _dwconv(u, w, b, name):
    T, D = u.shape
    K = w.shape[0]
    ch = _pick(T, _DW_CHUNK, 8)
    lead = _DW_PAD - (K - 1)

    def body(u_ref, w_ref, b_ref, c_ref, ext):
        ext[pl.ds(0, _DW_PAD), :] = jnp.zeros((_DW_PAD, 128), F32)
        ext[pl.ds(_DW_PAD, T), :] = u_ref[...]
        for c0 in range(0, T, ch):
            acc = jnp.zeros((ch, 128), F32) + b_ref[...]
            for k in range(K):
                acc = acc + w_ref[pl.ds(k, 1), :] * ext[pl.ds(c0 + lead + k, ch), :]
            c_ref[pl.ds(c0, ch), :] = acc

    return _call(
        body, name, (D // 128,),
        [pl.BlockSpec((T, 128), lambda i: (0, i)), pl.BlockSpec((K, 128), lambda i: (0, i)),
         pl.BlockSpec((1, 128), lambda i: (0, i))],
        [pl.BlockSpec((T, 128), lambda i: (0, i))],
        [_sds((T, D), F32)],
        scratch=[pltpu.VMEM((T + _DW_PAD, 128), F32)],
    )(u, w, b)[0]


def _dwconv_bwd(dc, u, w, name):
    T, D = u.shape
    K = w.shape[0]
    ch = _pick(T, _DW_CHUNK, 8)
    lead = _DW_PAD - (K - 1)

    def body(dc_ref, u_ref, w_ref, du_ref, dw_ref, db_ref, uext, dext):
        uext[pl.ds(0, _DW_PAD), :] = jnp.zeros((_DW_PAD, 128), F32)
        uext[pl.ds(_DW_PAD, T), :] = u_ref[...]
        dext[pl.ds(0, T), :] = dc_ref[...]
        dext[pl.ds(T, _DW_PAD), :] = jnp.zeros((_DW_PAD, 128), F32)
        dws =[jnp.zeros((8, 128), F32) for _ in range(K)]
        dbs = jnp.zeros((8, 128), F32)
        for c0 in range(0, T, ch):
            dcv = dext[pl.ds(c0, ch), :]
            dbs = dbs + jnp.sum(dcv.reshape(ch // 8, 8, 128), axis=0)
            acc = jnp.zeros((ch, 128), F32)
            for k in range(K):
                acc = acc + w_ref[pl.ds(k, 1), :] * dext[pl.ds(c0 + (K - 1) - k, ch), :]
                prod = dcv * uext[pl.ds(c0 + lead + k, ch), :]
                dws[k] = dws[k] + jnp.sum(prod.reshape(ch // 8, 8, 128), axis=0)
            du_ref[pl.ds(c0, ch), :] = acc
        for k in range(K):
            dw_ref[pl.ds(k, 1), :] = jnp.sum(dws[k], axis=0, keepdims=True)
        db_ref[...] = jnp.sum(dbs, axis=0, keepdims=True)

    return _call(
        body, name, (D // 128,),
        [pl.BlockSpec((T, 128), lambda i: (0, i)), pl.BlockSpec((T, 128), lambda i: (0, i)),
         pl.BlockSpec((K, 128), lambda i: (0, i))],
        [pl.BlockSpec((T, 128), lambda i: (0, i)), pl.BlockSpec((K, 128), lambda i: (0, i)),
         pl.BlockSpec((1, 128), lambda i: (0, i))],
        [_sds((T, D), F32), _sds((K, D), F32), _sds((1, D), F32)],
        scratch=[pltpu.VMEM((T + _DW_PAD, 128), F32), pltpu.VMEM((T + _DW_PAD, 128), F32)],
    )(dc, u, w)


def _conv_post(c, x, ng, w2, b2, name):
    T, D = x.shape
    tm = _pick(T, 512, 8)

    def body(c_ref, x_ref, ng_ref, w_ref, b_ref, xo_ref, s_ref):
        cv = c_ref[...]
        n = cv * _rstd(cv) * ng_ref[...]
        s = (n * _sig(n)).astype(BF16)
        s_ref[...] = s
        xo_ref[...] = x_ref[...] + _dot(s, w_ref[...]) + b_ref[...]

    row = lambda i: (i, 0)
    fix = lambda i: (0, 0)
    return _call(
        body, name, (T // tm,),
        [pl.BlockSpec((tm, D), row), pl.BlockSpec((tm, D), row), pl.BlockSpec((1, D), fix),
         pl.BlockSpec((D, D), fix), pl.BlockSpec((1, D), fix)],
        [pl.BlockSpec((tm, D), row), pl.BlockSpec((tm, D), row)],
        [_sds((T, D), F32), _sds((T, D), BF16)],
    )(c, x, ng, w2, b2)


def _conv_post_bwd(dy, c, ng, w2, name, after=()):
    T, D = dy.shape
    tm = _pick(T, 512, 8)

    def body(dy_ref, c_ref, ng_ref, w_ref, dc_ref, dng_ref, db_ref):
        @pl.when(pl.program_id(0) == 0)
        def _():
            dng_ref[...] = jnp.zeros_like(dng_ref)
            db_ref[...] = jnp.zeros_like(db_ref)

        dyv = dy_ref[...]
        ds = _dot_nt(dyv.astype(BF16), w_ref[...])
        cv = c_ref[...]
        r = _rstd(cv)
        ch = cv * r
        n = ch * ng_ref[...]
        sg = _sig(n)
        dn = ds * (sg * (1.0 + n * (1.0 - sg)))
        dc_ref[...] = _norm_bwd(dn, ch, r, ng_ref[...])
        dng_ref[...] += jnp.sum(dn * ch, axis=0, keepdims=True)
        db_ref[...] += jnp.sum(dyv, axis=0, keepdims=True)

    row = lambda i: (i, 0)
    fix = lambda i: (0, 0)
    return _call(
        body, name, (T // tm,),
        [pl.BlockSpec((tm, D), row), pl.BlockSpec((tm, D), row), pl.BlockSpec((1, D), fix), pl.BlockSpec((D, D), fix)],
        [pl.BlockSpec((tm, D), row), pl.BlockSpec((1, D), fix), pl.BlockSpec((1, D), fix)],
        [_sds((T, D), F32), _sds((1, D), F32), _sds((1, D), F32)], after=after,
    )(dy, c, ng, w2)


def _conv_pre_bwd(du, ag, x, g, dy, w1, name):
    T, D = x.shape
    tm = _pick(T, 512, 8)

    def body(du_ref, ag_ref, x_ref, g_ref, dy_ref, w_ref, dx_ref, dg_ref, dag_ref, db_ref):
        @pl.when(pl.program_id(0) == 0)
        def _():
            dg_ref[...] = jnp.zeros_like(dg_ref)
            db_ref[...] = jnp.zeros_like(db_ref)

        duv = du_ref[...]
        a = ag_ref[:, :D].astype(F32)
        gt = ag_ref[:, D:].astype(F32)
        sg = _sig(gt)
        da = duv * sg
        dgt = duv * a * sg * (1.0 - sg)
        db_ref[:, :D] += jnp.sum(da, axis=0, keepdims=True)
        db_ref[:, D:] += jnp.sum(dgt, axis=0, keepdims=True)
        dab = da.astype(BF16)
        dgb = dgt.astype(BF16)
        dag_ref[:, :D] = dab
        dag_ref[:, D:] = dgb
        dh = _dot_nt(dab, w_ref[:, :D]) + _dot_nt(dgb, w_ref[:, D:])
        xv = x_ref[...]
        r = _rstd(xv)
        xh = xv * r
        dx_ref[...] = dy_ref[...] + _norm_bwd(dh, xh, r, g_ref[...])
        dg_ref[...] += jnp.sum(dh * xh, axis=0, keepdims=True)

    row = lambda i: (i, 0)
    fix = lambda i: (0, 0)
    return _call(
        body, name, (T // tm,),
        [pl.BlockSpec((tm, D), row), pl.BlockSpec((tm, 2 * D), row), pl.BlockSpec((tm, D), row), pl.BlockSpec((1, D), fix),
         pl.BlockSpec((tm, D), row), pl.BlockSpec((D, 2 * D), fix)],
        [pl.BlockSpec((tm, D), row), pl.BlockSpec((1, D), fix), pl.BlockSpec((tm, 2 * D), row),
         pl.BlockSpec((1, 2 * D), fix)],
        [_sds((T, D), F32), _sds((1, D), F32), _sds((T, 2 * D), BF16), _sds((1, 2 * D), F32)],
    )(du, ag, x, g, dy, w1)


def _attn_qkv(x, g, wqkv, name):
    T, D = x.shape
    N = wqkv.shape[1]
    tn = N // 9
    tm = _pick(T, 512, 8)

    def body(x_ref, g_ref, w_ref, o_ref, h_ref):
        @pl.when(pl.program_id(1) == 0)
        def _():
            xv = x_ref[...]
            h_ref[...] = (xv * _rstd(xv) * g_ref[...]).astype(BF16)

        o_ref[...] = _dot(h_ref[...], w_ref[...]).astype(BF16)

    return _call(
        body, name, (T // tm, 9),
        [pl.BlockSpec((tm, D), lambda i, j: (i, 0)), pl.BlockSpec((1, D), lambda i, j: (0, 0)),
         pl.BlockSpec((D, tn), lambda i, j: (0, j))],
        [pl.BlockSpec((tm, tn), lambda i, j: (i, j)), pl.BlockSpec((tm, D), lambda i, j: (i, 0))],
        [_sds((T, N), BF16), _sds((T, D), BF16)],
    )(x, g, wqkv)


def _band_masks(q, steps, nblk):
    i = lax.broadcasted_iota(jnp.int32, (q, q), 0)
    j = lax.broadcasted_iota(jnp.int32, (q, q), 1)
    cur = i - j >= 0
    if steps < q - 1:
        cur = cur & (i - j <= steps)
    prev = q + i - j <= jnp.where(nblk > 0, steps, -1)
    return cur, prev


def _attn_fwd(qkv, qn_g, kn_g, grp, window, dil, name):
    T, C = qkv.shape
    HE = C // 9
    H = HE // HEAD_DIM
    E = HEAD_DIM
    Q = ATTN_BLOCK
    L = T // dil
    nb = L // Q
    steps = window // dil
    scale = E ** -0.5
    view = qkv.reshape(L, dil * C)

    def body(q_ref, kc_ref, kp_ref, vc_ref, vp_ref, gq_ref, gk_ref, o_ref, l_ref):
        n = pl.program_id(1)
        cur, prev = _band_masks(Q, steps, n)
        gq = gq_ref[...]
        gk = gk_ref[...]
        for h in range(H):
            hs = slice(h * E, (h + 1) * E)
            q = q_ref[:, hs].astype(F32)
            kc = kc_ref[:, hs].astype(F32)
            kp = kp_ref[:, hs].astype(F32)
            qn = (q * _rstd(q) * gq).astype(BF16)
            kcn = (kc * _rstd(kc) * gk).astype(BF16)
            kpn = (kp * _rstd(kp) * gk).astype(BF16)
            sc = jnp.where(cur, _dot_nt(qn, kcn) * scale, -1e30)
            sp = jnp.where(prev, _dot_nt(qn, kpn) * scale, -1e30)
            m = jnp.maximum(jnp.max(sc, axis=-1, keepdims=True), jnp.max(sp, axis=-1, keepdims=True))
            pc = jnp.exp(sc - m)
            pp = jnp.exp(sp - m)
            l = jnp.sum(pc, axis=-1, keepdims=True) + jnp.sum(pp, axis=-1, keepdims=True)
            inv = 1.0 / l
            o = _dot((pc * inv).astype(BF16), vc_ref[:, hs]) + _dot((pp * inv).astype(BF16), vp_ref[:, hs])
            o_ref[:, hs] = o
            l_ref[:, hs] = jnp.broadcast_to(m + jnp.log(l), (Q, E))

    base = grp * 3
    blk = lambda s, back: pl.BlockSpec((Q, HE), lambda r, n: (jnp.maximum(n - back, 0), r * 9 + base + s))
    vec = pl.BlockSpec((1, E), lambda r, n: (0, 0))
    out = pl.BlockSpec((Q, HE), lambda r, n: (n, r))
    o, lse = _call(
        body, name, (dil, nb),
        [blk(0, 0), blk(1, 0), blk(1, 1), blk(2, 0), blk(2, 1), vec, vec],
        [out, out],
        [_sds((L, dil * HE), F32), _sds((L, dil * HE), F32)],
    )(view, view, view, view, view, qn_g, kn_g)
    return o.reshape(T, HE), lse.reshape(T, HE)


def _attn_merge(os, lses, x, wo, name):
    T, D = x.shape
    HE = wo.shape[0]
    tm = _pick(T, 512, 8)
    ng = len(os)

    def body(*refs):
        o_refs = refs[:ng]
        l_refs = refs[ng:2 * ng]
        x_ref, w_ref, xo_ref, om_ref, lt_ref = refs[2 * ng:]
        ls = [r[...] for r in l_refs]
        m = functools.reduce(jnp.maximum, ls)
        es = [jnp.exp(l - m) for l in ls]
        tot = functools.reduce(lambda a, b: a + b, es)
        inv = 1.0 / tot
        om = functools.reduce(lambda a, b: a + b, [e * inv * r[...] for e, r in zip(es, o_refs)])
        omb = om.astype(BF16)
        om_ref[...] = omb
        lt_ref[...] = m + jnp.log(tot)
        xo_ref[...] = x_ref[...] + _dot(omb, w_ref[...])

    row = lambda i: (i, 0)
    fix = lambda i: (0, 0)
    return _call(
        body, name, (T // tm,),
        [pl.BlockSpec((tm, HE), row)] * (2 * ng) + [pl.BlockSpec((tm, D), row), pl.BlockSpec((HE, D), fix)],
        [pl.BlockSpec((tm, D), row), pl.BlockSpec((tm, HE), row), pl.BlockSpec((tm, HE), row)],
        [_sds((T, D), F32), _sds((T, HE), BF16), _sds((T, HE), F32)],
    )(*os, *lses, x, wo)


def _attn_out_bwd(dy, om, wo, name, after=()):
    T, D = dy.shape
    HE = wo.shape[0]
    E = HEAD_DIM
    tm = _pick(T, 512, 8)

    def body(dy_ref, om_ref, w_ref, dom_ref, dl_ref):
        dom = _dot_nt(dy_ref[...].astype(BF16), w_ref[...])
        dom_ref[...] = dom.astype(BF16)
        prod = dom * om_ref[...].astype(F32)
        for h in range(HE // E):
            hs = slice(h * E, (h + 1) * E)
            dl_ref[:, hs] = jnp.broadcast_to(jnp.sum(prod[:, hs], axis=-1, keepdims=True), (tm, E))

    row = lambda i: (i, 0)
    return _call(
        body, name, (T // tm,),
        [pl.BlockSpec((tm, D), row), pl.BlockSpec((tm, HE), row), pl.BlockSpec((HE, D), lambda i: (0, 0))],
        [pl.BlockSpec((tm, HE), row), pl.BlockSpec((tm, HE), row)],
        [_sds((T, HE), BF16), _sds((T, HE), F32)], after=after,
    )(dy, om, wo)


def _attn_bwd(qkv, dom, lse, delta, qn_g, kn_g, grp, window, dil, name):
    T, C = qkv.shape
    HE = C // 9
    H = HE // HEAD_DIM
    E = HEAD_DIM
    Q = ATTN_BLOCK
    L = T // dil
    nb = L // Q
    steps = window // dil
    scale = E ** -0.5
    view = qkv.reshape(L, dil * C)
    sview = lambda a: a.reshape(L, dil * HE)

    def body(q_ref, kc_ref, kp_ref, vc_ref, vp_ref, do_ref, l_ref, dl_ref, gq_ref, gk_ref,
             dq_ref, dk_ref, dv_ref, dgq_ref, dgk_ref, ck_sc, cv_sc):
        r_id = pl.program_id(0)
        n = pl.program_id(1)

        @pl.when((r_id == 0) & (n == 0))
        def _():
            dgq_ref[...] = jnp.zeros_like(dgq_ref)
            dgk_ref[...] = jnp.zeros_like(dgk_ref)

        @pl.when(n == 0)
        def _():
            ck_sc[...] = jnp.zeros_like(ck_sc)
            cv_sc[...] = jnp.zeros_like(cv_sc)

        @pl.when(n < nb)
        def _():
            cur, prev = _band_masks(Q, steps, n)
            gq = gq_ref[...]
            gk = gk_ref[...]
            dgq = jnp.zeros((1, E), F32)
            dgk = jnp.zeros((1, E), F32)
            for h in range(H):
                hs = slice(h * E, (h + 1) * E)
                q = q_ref[:, hs].astype(F32)
                kc = kc_ref[:, hs].astype(F32)
                kp = kp_ref[:, hs].astype(F32)
                rq, rc, rp = _rstd(q), _rstd(kc), _rstd(kp)
                qh, kch, kph = q * rq, kc * rc, kp * rp
                qn = (qh * gq).astype(BF16)
                kcn = (kch * gk).astype(BF16)
                kpn = (kph * gk).astype(BF16)
                lcol = l_ref[:, hs][:, :1]
                dcol = dl_ref[:, hs][:, :1]
                do = do_ref[:, hs]
                pc = jnp.where(cur, jnp.exp(_dot_nt(qn, kcn) * scale - lcol), 0.0)
                pp = jnp.where(prev, jnp.exp(_dot_nt(qn, kpn) * scale - lcol), 0.0)
                dsc = (pc * (_dot_nt(do, vc_ref[:, hs]) - dcol) * scale).astype(BF16)
                dsp = (pp * (_dot_nt(do, vp_ref[:, hs]) - dcol) * scale).astype(BF16)
                dvc = _dot_tn(pc.astype(BF16), do)
                dvp = _dot_tn(pp.astype(BF16), do)
                dqn = _dot(dsc, kcn) + _dot(dsp, kpn)
                dkcn = _dot_tn(dsc, qn)
                dkpn = _dot_tn(dsp, qn)
                dq_ref[:, hs] = _norm_bwd(dqn, qh, rq, gq).astype(BF16)
                dgq = dgq + jnp.sum(dqn * qh, axis=0, keepdims=True)
                dgk = dgk + jnp.sum(dkcn * kch + dkpn * kph, axis=0, keepdims=True)
                dk_ref[:, hs] = (ck_sc[:, hs] + _norm_bwd(dkpn, kph, rp, gk)).astype(BF16)
                dv_ref[:, hs] = (cv_sc[:, hs] + dvp).astype(BF16)
                ck_sc[:, hs] = _norm_bwd(dkcn, kch, rc, gk)
                cv_sc[:, hs] = dvc
            dgq_ref[...] += dgq
            dgk_ref[...] += dgk

        @pl.when(n == nb)
        def _():
            dk_ref[...] = ck_sc[...].astype(BF16)
            dv_ref[...] = cv_sc[...].astype(BF16)

    base = grp * 3
    nq = lambda n: jnp.minimum(n, nb - 1)
    blk = lambda s, back: pl.BlockSpec((Q, HE), lambda r, n: (jnp.maximum(nq(n) - back, 0), r * 9 + base + s))
    qblk = pl.BlockSpec((Q, HE), lambda r, n: (nq(n), r))
    kblk = pl.BlockSpec((Q, HE), lambda r, n: (jnp.maximum(n - 1, 0), r))
    vec = pl.BlockSpec((1, E), lambda r, n: (0, 0))
    dq, dk, dv, dgq, dgk = _call(
        body, name, (dil, nb + 1),
        [blk(0, 0), blk(1, 0), blk(1, 1), blk(2, 0), blk(2, 1), qblk, qblk, qblk, vec, vec],
        [qblk, kblk, kblk, vec, vec],
        [_sds((L, dil * HE), BF16)] * 3 + [_sds((1, E), F32)] * 2,
        scratch=[pltpu.VMEM((Q, HE), F32), pltpu.VMEM((Q, HE), F32)],
    )(view, view, view, view, view, sview(dom), sview(lse), sview(delta), qn_g, kn_g)
    return dq.reshape(T, HE), dk.reshape(T, HE), dv.reshape(T, HE), dgq, dgk


def _attn_qkv_bwd(dqkv, x, g, dy, wqkv, name):
    T, D = x.shape
    HE = wqkv.shape[1] // 9
    tm = _pick(T, 512, 8)

    def body(*refs):
        d_refs = refs[:9]
        x_ref, g_ref, dy_ref, w_ref, dx_ref, dg_ref, dh_sc = refs[9:]
        i = pl.program_id(0)
        j = pl.program_id(1)

        @pl.when((i == 0) & (j == 0))
        def _():
            dg_ref[...] = jnp.zeros_like(dg_ref)

        @pl.when(j == 0)
        def _():
            dh_sc[...] = jnp.zeros_like(dh_sc)

        for s in range(9):
            @pl.when(j == s)
            def _(s=s):
                dh_sc[...] += _dot_nt(d_refs[s][...], w_ref[...])

        @pl.when(j == 8)
        def _():
            xv = x_ref[...]
            r = _rstd(xv)
            xh = xv * r
            dh = dh_sc[...]
            dx_ref[...] = dy_ref[...] + _norm_bwd(dh, xh, r, g_ref[...])
            dg_ref[...] += jnp.sum(dh * xh, axis=0, keepdims=True)

    row = lambda i, j: (i, 0)
    fix = lambda i, j: (0, 0)
    return _call(
        body, name, (T // tm, 9),
        [pl.BlockSpec((tm, HE), row)] * 9 + [pl.BlockSpec((tm, D), row), pl.BlockSpec((1, D), fix), pl.BlockSpec((tm, D), row),
                                           pl.BlockSpec((D, HE), lambda i, j: (0, j))],
        [pl.BlockSpec((tm, D), row), pl.BlockSpec((1, D), fix)],
        [_sds((T, D), F32), _sds((1, D), F32)],
        scratch=[pltpu.VMEM((tm, D), F32)],
    )(*dqkv, x, g, dy, wqkv)


def _loss_head(y, target, name):
    T, D = y.shape
    tm = _pick(T, 512, 8)

    def body(y_ref, t_ref, dy_ref, sq_ref):
        @pl.when(pl.program_id(0) == 0)
        def _():
            sq_ref[...] = jnp.zeros_like(sq_ref)

        err = y_ref[...] - t_ref[...]
        dy_ref[...] = err * (1.0 / D)
        sq_ref[...] += jnp.sum(err * err, axis=0, keepdims=True)

    row = lambda i: (i, 0)
    return _call(
        body, name, (T // tm,),
        [pl.BlockSpec((tm, D), row), pl.BlockSpec((tm, D), row)],
        [pl.BlockSpec((tm, D), row), pl.BlockSpec((1, D), lambda i: (0, 0))],
        [_sds((T, D), F32), _sds((1, D), F32)],
    )(y, target)


def _local_step(x, target, norm_g, b_pw1, w_dw, b_dw, cng, b_pw2, qn, kn, block_weights, grads_ready):
    T, D = x.shape
    ng = lambda l, k: norm_g[l, k][None, :]
    bn = _pick(D, 256, 128)

    w_in, w_out = [None] * 4, [None] * 4
    w_in[0], w_out[0] = block_weights(0, x)
    x1, *gu0 = _ffn_fwd(x, ng(0, 0), w_in[0], w_out[0], "ffn_fwd_0")
    pw1, pw2 = block_weights(1, x1)
    ag, u, hc = _conv_pre(x1, ng(0, 1), pw1, b_pw1, "conv_pre")
    c = _dwconv(u, w_dw, b_dw, "dwconv")
    x2, s = _conv_post(c, x1, cng, pw2, b_pw2, "conv_post")
    w_in[1], w_out[1] = block_weights(2, x2)
    x3, *gu1 = _ffn_fwd(x2, ng(0, 2), w_in[1], w_out[1], "ffn_fwd_1")
    w_in[2], w_out[2] = block_weights(3, x3)
    x4, *gu2 = _ffn_fwd(x3, ng(1, 0), w_in[2], w_out[2], "ffn_fwd_2")
    wqkv, wo = block_weights(4, x4)
    HE = wo.shape[0]
    F = w_out[0].shape[0]
    bf = _pick(F, 256, 128)
    bh = _pick(HE, 512, 128)
    qkv, ha = _attn_qkv(x4, ng(1, 1), wqkv, "attn_qkv")
    os, lses = [], []
    for gi, (window, dil) in enumerate(ATTN_GROUPS):
        o, l = _attn_fwd(qkv, qn[gi][None, :], kn[gi][None, :], gi, window, dil, f"attn_fwd_{gi}")
        os.append(o)
        lses.append(l)
    x5, om, lse = _attn_merge(os, lses, x4, wo, "attn_merge")
    w_in[3], w_out[3] = block_weights(5, x5)
    x6, *gu3 = _ffn_fwd(x5, ng(1, 2), w_in[3], w_out[3], "ffn_fwd_3")
    dy, sq = _loss_head(x6, target, "loss_head")

    grads = {"ffn_w_in": [None] * 4, "ffn_w_out": [None] * 4}
    dnorm = [[None] * 3 for _ in range(2)]

    def ffn_back(lf, k, xin, gvec, dy, gu, after):
        dx, dg, dgate, dup, a, h, dyb = _ffn_bwd(xin, gvec, dy, gu[0], gu[1], w_in[lf], w_out[lf], f"ffn_bwd_{lf}", after=after)
        d_in = _mm_tn(h, dgate, D, bf, (D, 2 * F), (D, bf), lambda mi, ni: (0, ni), 1.0, f"ffn_dw_gate_{lf}")
        grads["ffn_w_in"][lf] = _mm_tn(h, dup, D, bf, (D, 2 * F), (D, bf), lambda mi, ni: (0, F // bf + ni), 1.0,
                                       f"ffn_dw_up_{lf}", prev=d_in)
        grads["ffn_w_out"][lf] = _mm_tn(a, dyb, bf, D, (F, D), (bf, D), lambda mi, ni: (mi, 0), 1.0, f"ffn_dw_out_{lf}")
        return dx, dg, grads_ready(k, (grads["ffn_w_in"][lf], grads["ffn_w_out"][lf]), dx)

    dx, dnorm[1][2], tok = ffn_back(3, 5, x5, ng(1, 2), dy, gu3, ())
    dom, delta = _attn_out_bwd(dx, om, wo, "attn_out_bwd", after=tok)
    grads["attn_w_o"] = _mm_tn(om, dx, HE, bn, (HE, D), (HE, bn), lambda mi, ni: (0, ni), 1.0, "attn_dw_o")
    dqkv, dgq, dgk = [], [], []
    for gi, (window, dil) in enumerate(ATTN_GROUPS):
        dq, dk, dv, a_, b_ = _attn_bwd(qkv, dom, lse, delta, qn[gi][None, :], kn[gi][None, :], gi, window, dil, f"attn_bwd_{gi}")
        dqkv += [dq, dk, dv]
        dgq.append(a_)
        dgk.append(b_)
    grads["attn_q_norm"] = jnp.concatenate(dgq, axis=0)
    grads["attn_k_norm"] = jnp.concatenate(dgk, axis=0)
    d_qkv = None
    for s9 in range(9):
        d_qkv = _mm_tn(ha, dqkv[s9], D, bh, (D, 9 * HE), (D, bh), lambda mi, ni, s9=s9: (0, s9 * (HE // bh) + ni), 1.0,
                       f"attn_dw_qkv_{s9}", prev=d_qkv)
    grads["attn_w_qkv"] = d_qkv
    dx, dnorm[1][1] = _attn_qkv_bwd(dqkv, x4, ng(1, 1), dx, wqkv, "attn_qkv_bwd")
    tok = grads_ready(4, (grads["attn_w_qkv"], grads["attn_w_o"]), dx)
    dx, dnorm[1][0], tok = ffn_back(2, 3, x3, ng(1, 0), dx, gu2, tok)

    dx, dnorm[0][2], tok = ffn_back(1, 2, x2, ng(0, 2), dx, gu1, tok)
    dc, grads["conv_norm_g"], grads["conv_b_pw2"] = _conv_post_bwd(dx, c, cng, pw2, "conv_post_bwd", after=tok)
    grads["conv_w_pw2"] = _mm_tn(s, dx, D, bn, (D, D), (D, bn), lambda mi, ni: (0, ni), 1.0, "conv_dw_pw2")
    du, grads["conv_w_dw"], grads["conv_b_dw"] = _dwconv_bwd(dc, u, w_dw, "dwconv_bwd")
    dx, dnorm[0][1], dag, grads["conv_b_pw1"] = _conv_pre_bwd(du, ag, x1, ng(0, 1), dx, pw1, "conv_pre_bwd")
    grads["conv_w_pw1"] = _mm_tn(hc, dag, D, 2 * bn, (D, 2 * D), (D, 2 * bn), lambda mi, ni: (0, ni), 1.0, "conv_dw_pw1")
    tok = grads_ready(1, (grads["conv_w_pw1"], grads["conv_w_pw2"]), dx)
    dx, dnorm[0][0], _ = ffn_back(0, 0, x, ng(0, 0), dx, gu0, tok)

    grads["norm_g"] = jnp.concatenate([jnp.concatenate(r, axis=0)[None] for r in dnorm], axis=0)
    return sq, dx, grads


class _Sharded:
    def __init__(self, name, full3, half_axis, shard_axis, src, slab=None):
        self.name, self.full3, self.half_axis, self.shard_axis = name, tuple(full3), half_axis, shard_axis
        self.src, self.slab = src, slab

    def source(self, refs):
        return refs[self.src] if self.slab is None else refs[self.src].at[self.slab]

    def _cut(self, shape, axis, parts):
        s = list(shape)
        s[axis] //= parts
        return tuple(s)

    @property
    def shard3(self):
        return self._cut(self.full3, self.shard_axis, N_CHIPS)

    @property
    def pair3(self):
        return self._cut(self.full3, self.half_axis, 2)

    @property
    def part3(self):
        return self._cut(self.shard3, self.half_axis, 2)

    @staticmethod
    def _slice(ref, axis, idx, parts):
        n = ref.shape[axis] // parts
        start = idx * n
        minor = len(ref.shape) - 1 - axis
        if minor < 2 and not isinstance(start, int):
            start = pl.multiple_of(start, 128 if minor == 0 else (16 if n % 16 == 0 else 8))
        sl = [slice(None)] * len(ref.shape)
        sl[axis] = pl.ds(start, n)
        return ref.at[tuple(sl)]

    def half(self, ref, h):
        return self._slice(ref, self.half_axis, h, 2)

    def shard(self, ref, j):
        return self._slice(ref, self.shard_axis, j, N_CHIPS)


def _place():
    x, y, c = lax.axis_index("x"), lax.axis_index("y"), lax.axis_index("c")
    return x, y, c, 2 * x + y


_RELS = (1, 2, 3)


def _peer(x, y, rel):
    px = 1 - x if rel & 2 else x
    py = 1 - y if rel & 1 else y
    return px, py, 2 * px + py


ANY = pl.BlockSpec(memory_space=pl.ANY)


def _comm_call(body, name, n_in, out_shape, n_sems, aliases=None):
    return pl.pallas_call(
        body, name=name, in_specs=[ANY] * n_in, out_specs=[ANY] * len(out_shape), out_shape=out_shape,
        scratch_shapes=[pltpu.SemaphoreType.DMA((n,)) for n in n_sems],
        input_output_aliases=aliases or {},
        compiler_params=pltpu.CompilerParams(has_side_effects=True),
    )


def _remote(src, dst, send_sem, recv_sem, dev):
    return pltpu.make_async_remote_copy(src_ref=src, dst_ref=dst, send_sem=send_sem, recv_sem=recv_sem, device_id=dev,
                                        device_id_type=MESH)


def _gather_small(small_shards):
    ns = len(small_shards)

    def body(*refs):
        ins, outs = refs[:ns], refs[ns:2 * ns]
        lsem, ssem, rsem = refs[2 * ns:]
        x, y, c, me = _place()
        cols = lambda ref, j: _Sharded._slice(ref, 1, j, N_CHIPS)
        local = [pltpu.make_async_copy(ins[i], cols(outs[i], me), lsem.at[i]) for i in range(ns)]
        sends = []
        for i in range(ns):
            for k, rel in enumerate(_RELS):
                px, py, _ = _peer(x, y, rel)
                sends.append(_remote(ins[i], cols(outs[i], me), ssem.at[3 * i + k], rsem.at[3 * i + k], (px, py, c)))
        for cp in local + sends:
            cp.start()
        for i in range(ns):
            for k, rel in enumerate(_RELS):
                _, _, pj = _peer(x, y, rel)
                got = cols(outs[i], pj)
                _remote(got, got, ssem.at[3 * i + k], rsem.at[3 * i + k], (x, y, c)).wait_recv()
        for cp in sends:
            cp.wait_send()
        for cp in local:
            cp.wait()

    out_shape = [_sds((s.shape[0], s.shape[1] * N_CHIPS), F32) for s in small_shards]
    return _comm_call(body, "gather_small", ns, out_shape, [ns, 3 * ns, 3 * ns])(*small_shards)


HBM = pl.BlockSpec(memory_space=pltpu.HBM)
SEM = pl.BlockSpec(memory_space=pltpu.SEMAPHORE)
DATAFLOW = pltpu.SideEffectType.DATAFLOW_SIDE_EFFECTING


def _in_hbm(a):
    return pltpu.with_memory_space_constraint(a, pltpu.HBM)


def _cast_place(it, shard, scal):
    a_n, r_n, c_n = it.shard3
    tr = _pick(r_n, 256, 16)
    sa = it.shard_axis

    def body(sc_ref, s_ref, o_ref):
        o_ref[...] = s_ref[...].astype(BF16)

    if it.slab is None:
        src = pl.BlockSpec((1, tr, c_n), lambda a, rb, sc: (a, rb, 0))
    else:
        src = pl.BlockSpec((None, 1, tr, c_n), lambda a, rb, sc: (it.slab, a, rb, 0))
    dst = pl.BlockSpec((1, tr, c_n), lambda a, rb, sc: (a + sc[1] * (a_n if sa == 0 else 0), rb + sc[1] * (r_n // tr if sa == 1 else 0),
                                                       sc[1] if sa == 2 else 0))
    return _call(body, f"cast_place_{it.name}", (a_n, r_n // tr), [src], [dst], [_sds(it.full3, BF16)], prefetch=1)(scal, shard)[0]


def _gather_start(items, fulls):
    ni = len(items)

    def body(*refs):
        outs = refs[ni:]
        ssem, rsem, full = outs[:ni], outs[ni:2 * ni], outs[2 * ni:3 * ni]
        x, y, c, me = _place()
        for i, it in enumerate(items):
            mine = it.half(it.shard(full[i], me), c)
            for k, rel in enumerate(_RELS):
                px, py, _ = _peer(x, y, rel)
                _remote(mine, mine, ssem[i].at[k], rsem[i].at[k], (px, py, c)).start()

    outs = pl.pallas_call(
        body, name="gather_start", in_specs=[HBM] * ni, out_specs=[SEM] * (2 * ni) + [HBM] * ni,
        out_shape=[pltpu.SemaphoreType.DMA((3,))] * (2 * ni) + [pltpu.HBM(it.full3, BF16) for it in items],
        input_output_aliases={j: 2 * ni + j for j in range(ni)},
        compiler_params=pltpu.CompilerParams(has_side_effects=DATAFLOW),
    )(*[_in_hbm(f) for f in fulls])
    return outs[:ni], outs[ni:2 * ni], outs[2 * ni:]


def _gather_forward(items, fulls, ssems, rsems, after, name):
    ni = len(items)

    def body(*refs):
        ssem, rsem = refs[ni:2 * ni], refs[2 * ni:3 * ni]
        outs = refs[3 * ni + 1:]
        full, fsem, gsem = outs[:ni], outs[ni:2 * ni], outs[2 * ni:3 * ni]
        x, y, c, me = _place()
        sib = (x, y, 1 - c)
        for i, it in enumerate(items):
            for k, rel in enumerate(_RELS):
                _, _, pj = _peer(x, y, rel)
                got = it.half(it.shard(full[i], pj), c)
                _remote(got, got, ssem[i].at[k], rsem[i].at[k], sib).wait_recv()
                _remote(got, got, fsem[i].at[k], gsem[i].at[k], sib).start()
        for i, it in enumerate(items):
            mine = it.half(it.shard(full[i], me), c)
            for k in range(3):
                _remote(mine, mine, ssem[i].at[k], rsem[i].at[k], sib).wait_send()

    outs = pl.pallas_call(
        body, name=name, in_specs=[HBM] * ni + [SEM] * (2 * ni) + [ANY],
        out_specs=[HBM] * ni + [SEM] * (2 * ni),
        out_shape=[pltpu.HBM(it.full3, BF16) for it in items] + [pltpu.SemaphoreType.DMA((3,))] * (2 * ni),
        input_output_aliases={i: i for i in range(ni)},
        compiler_params=pltpu.CompilerParams(has_side_effects=DATAFLOW),
    )(*fulls, *ssems, *rsems, after)
    return outs[:ni], outs[ni:2 * ni], outs[2 * ni:]


def _gather_finish(items, fulls, fsems, gsems, name):
    ni = len(items)

    def body(*refs):
        fsem, gsem = refs[ni:2 * ni], refs[2 * ni:3 * ni]
        full = refs[3 * ni:]
        x, y, c, _ = _place()
        sib = (x, y, 1 - c)
        for i, it in enumerate(items):
            for k, rel in enumerate(_RELS):
                _, _, pj = _peer(x, y, rel)
                got = it.half(it.shard(full[i], pj), 1 - c)
                _remote(got, got, fsem[i].at[k], gsem[i].at[k], sib).wait_recv()
                sent = it.half(it.shard(full[i], pj), c)
                _remote(sent, sent, fsem[i].at[k], gsem[i].at[k], sib).wait_send()

    return pl.pallas_call(
        body, name=name, in_specs=[HBM] * ni + [SEM] * (2 * ni), out_specs=[HBM] * ni,
        out_shape=[pltpu.HBM(it.full3, BF16) for it in items],
        input_output_aliases={i: i for i in range(ni)},
        compiler_params=pltpu.CompilerParams(has_side_effects=DATAFLOW),
    )(*fulls, *fsems, *gsems)


def _exchange_start(name, arrays, n_sems, copies):
    na, ns = len(arrays), len(n_sems)

    def body(*refs):
        outs = refs[na:]
        ssem, rsem, thru, token = outs[:ns], outs[ns:2 * ns], outs[2 * ns:2 * ns + na], outs[-1]
        for send, _ in copies(thru, ssem, rsem):
            send.start()
        token[...] = jnp.zeros_like(token)

    outs = pl.pallas_call(
        body, name=name, in_specs=[HBM] * na,
        out_specs=[SEM] * (2 * ns) + [HBM] * na + [pl.BlockSpec(memory_space=pltpu.VMEM)],
        out_shape=[pltpu.SemaphoreType.DMA((n,)) for n in n_sems] * 2 + [pltpu.HBM(a.shape, a.dtype) for a in arrays]
        + [_sds((8, 128), F32)],
        input_output_aliases={j: 2 * ns + j for j in range(na)},
        compiler_params=pltpu.CompilerParams(has_side_effects=DATAFLOW),
    )(*[_in_hbm(a) for a in arrays])
    return outs[:ns], outs[ns:2 * ns], outs[2 * ns:2 * ns + na], outs[-1]


def _exchange_wait(name, arrays, ssems, rsems, copies, after):
    na, ns = len(arrays), len(ssems)

    def body(*refs):
        ssem, rsem = refs[na:na + ns], refs[na + ns:na + 2 * ns]
        thru = refs[na + 2 * ns + 1:]
        for send, recv in copies(thru, ssem, rsem):
            recv.wait_recv()
            send.wait_send()

    return pl.pallas_call(
        body, name=name, in_specs=[HBM] * na + [SEM] * (2 * ns) + [ANY], out_specs=[HBM] * na,
        out_shape=[pltpu.HBM(a.shape, a.dtype) for a in arrays],
        input_output_aliases={i: i for i in range(na)},
        compiler_params=pltpu.CompilerParams(has_side_effects=DATAFLOW),
    )(*arrays, *ssems, *rsems, after)


def _pair_copies(items):
    n = len(items)

    def copies(a, ssem, rsem):
        x, y, c, _ = _place()
        cps = [_remote(it.half(a[i], 1 - c), a[n + i], ssem[i].at[0], rsem[i].at[0], (x, y, 1 - c)) for i, it in enumerate(items)]
        return [(cp, cp) for cp in cps]

    return copies


def _chip_copies(items):
    n = len(items)

    def copies(a, ssem, rsem):
        x, y, c, _ = _place()
        cps = []
        for i, it in enumerate(items):
            for k, rel in enumerate(_RELS):
                px, py, pj = _peer(x, y, rel)
                cps.append(_remote(it.shard(a[i], pj), a[n + i].at[k], ssem[i].at[k], rsem[i].at[k], (px, py, c)))
        return [(cp, cp) for cp in cps]

    return copies


def _fill_copies(items):
    def copies(a, ssem, rsem):
        x, y, c, _ = _place()
        sib = (x, y, 1 - c)
        out = []
        for i, it in enumerate(items):
            mine, other = it.half(a[i], c), it.half(a[i], 1 - c)
            out.append((_remote(mine, mine, ssem[i].at[0], rsem[i].at[0], sib), _remote(other, other, ssem[i].at[0], rsem[i].at[0], sib)))
        return out

    return copies


def _ew_tiles(d):
    _, rows, cols = d.part3
    return _pick(rows, 256, 16), cols


def _pair_add(d, g_full, got, scal):
    tr, tc = _ew_tiles(d)
    a_n, r_n, c_n = d.pair3
    ha = d.half_axis

    def body(sc_ref, g_ref, r_ref, o_ref, ob_ref):
        s = g_ref[...] + r_ref[...].astype(F32)
        o_ref[...] = s
        ob_ref[...] = s.astype(BF16)

    blk = (1, tr, tc)
    same = pl.BlockSpec(blk, lambda a, rb, cb, sc: (a, rb, cb))
    mine = pl.BlockSpec(blk, lambda a, rb, cb, sc: (a + sc[0] * (a_n if ha == 0 else 0), rb + sc[0] * (r_n // tr if ha == 1 else 0), cb))
    return _call(body, f"pair_add_{d.name}", (a_n, r_n // tr, c_n // tc), [mine, same], [same, same],
                 [_sds(d.pair3, F32), _sds(d.pair3, BF16)], prefetch=1)(scal, g_full, got)


def _chip_reduce(d, pair_f32, got, scal):
    tr, tc = _ew_tiles(d)
    a_n, r_n, c_n = d.part3
    ha, sa = d.half_axis, d.shard_axis

    def body(sc_ref, p_ref, r0, r1, r2, o_ref):
        o_ref[...] = ((p_ref[...] + r0[...].astype(F32)) + r1[...].astype(F32)) + r2[...].astype(F32)

    blk = (1, tr, tc)
    own = pl.BlockSpec(blk, lambda a, rb, sc: (a + sc[1] * (a_n if sa == 0 else 0), rb + sc[1] * (r_n // tr if sa == 1 else 0),
                                               sc[1] if sa == 2 else 0))
    slot = lambda k: pl.BlockSpec((None,) + blk, lambda a, rb, sc: (k, a, rb, 0))
    out = pl.BlockSpec(blk, lambda a, rb, sc: (a + sc[0] * (a_n if ha == 0 else 0), rb + sc[0] * (r_n // tr if ha == 1 else 0), 0))
    return _call(body, f"chip_reduce_{d.name}", (a_n, r_n // tr), [own, slot(0), slot(1), slot(2)], [out],
                 [_sds(d.shard3, F32)], prefetch=1)(scal, pair_f32, got, got, got)[0]


def _adam_math(g, w, m, v):
    m = ADAM_B1 * m + (1.0 - ADAM_B1) * g
    v = ADAM_B2 * v + (1.0 - ADAM_B2) * (g * g)
    m_hat = m / (1.0 - ADAM_B1 ** ADAM_STEP)
    v_hat = v / (1.0 - ADAM_B2 ** ADAM_STEP)
    delta = -ADAM_LR * (m_hat / (jnp.sqrt(v_hat) + ADAM_EPS) + ADAM_WD * w)
    return delta, m, v


def _adam(d, g, w, m, v, prev=None):
    tr, tc = _ew_tiles(d)
    a_n, r_n, c_n = d.shard3
    n_prev = 0 if prev is None else 4

    def body(g_ref, w_ref, m_ref, v_ref, *rest):
        go_ref, d_ref, mo_ref, vo_ref = rest[n_prev:]
        gv = g_ref[...]
        go_ref[...] = gv
        d_ref[...], mo_ref[...], vo_ref[...] = _adam_math(gv, w_ref[...], m_ref[...], v_ref[...])

    plain = pl.BlockSpec((1, tr, tc), lambda a, rb: (a, rb, 0))
    if d.slab is None:
        wspec, shape = plain, d.shard3
    else:
        wspec, shape = pl.BlockSpec((None, 1, tr, tc), lambda a, rb: (d.slab, a, rb, 0)), (4,) + d.shard3
    in_specs = [plain] + [wspec] * 3
    args = [g, w, m, v]
    aliases = None
    if prev is not None:
        in_specs += [pl.BlockSpec(memory_space=pl.ANY)] * 4
        args += list(prev)
        aliases = {4 + k: k for k in range(4)}
    return _call(body, f"adam_{d.name}", (a_n, r_n // tr), in_specs, [wspec] * 4, [_sds(shape, F32)] * 4, aliases=aliases)(*args)


def _adam_small(gs, ws, ms, vs):
    n = len(gs)

    def body(*refs):
        for i in range(n):
            g, w, m, v = (refs[k * n + i][...] for k in range(4))
            d, mo, vo = _adam_math(g, w, m, v)
            refs[4 * n + i][...] = d
            refs[5 * n + i][...] = mo
            refs[6 * n + i][...] = vo

    vm = pl.BlockSpec(memory_space=pltpu.VMEM)
    outs = pl.pallas_call(body, name="adam_small", in_specs=[vm] * (4 * n), out_specs=[vm] * (3 * n),
                          out_shape=[_sds(g.shape, F32) for g in gs] * 3)(*gs, *ws, *ms, *vs)
    return outs[:n], outs[n:2 * n], outs[2 * n:]


def _allreduce_small(packed):
    rows, cols = packed.shape
    others = [(dx, dy, dc) for dx in (0, 1) for dy in (0, 1) for dc in (0, 1) if (dx, dy, dc) != (0, 0, 0)]

    def body(in_ref, out_ref, buf, ssem, rsem):
        x, y, c, _ = _place()
        lin = 4 * x + 2 * y + c
        buf[lin] = in_ref[...]
        cps = []
        for k, (dx, dy, dc) in enumerate(others):
            px = 1 - x if dx else x
            py = 1 - y if dy else y
            pc = 1 - c if dc else c
            cps.append((pltpu.make_async_remote_copy(src_ref=in_ref, dst_ref=buf.at[lin], send_sem=ssem.at[k], recv_sem=rsem.at[k],
                                                     device_id=(px, py, pc), device_id_type=MESH), 4 * px + 2 * py + pc))
        for cp, _ in cps:
            cp.start()
        for k, (cp, plin) in enumerate(cps):
            pltpu.make_async_remote_copy(src_ref=in_ref, dst_ref=buf.at[plin], send_sem=ssem.at[k], recv_sem=rsem.at[k],
                                         device_id=(x, y, c), device_id_type=MESH).wait_recv()
        for cp, _ in cps:
            cp.wait_send()
        acc = buf[0]
        for dev in range(1, 8):
            acc = acc + buf[dev]
        out_ref[...] = acc

    vm = pl.BlockSpec(memory_space=pltpu.VMEM)
    return pl.pallas_call(
        body, name="allreduce_small", in_specs=[vm], out_specs=vm, out_shape=_sds((rows, cols), F32),
        scratch_shapes=[pltpu.VMEM((8, rows, cols), F32), pltpu.SemaphoreType.DMA((7,)), pltpu.SemaphoreType.DMA((7,))],
        compiler_params=pltpu.CompilerParams(has_side_effects=True),
    )(packed)


_BIG = ("ffn_w_in", "ffn_w_out", "conv_w_pw1", "conv_w_pw2", "attn_w_qkv", "attn_w_o")
_SMALL = ("norm_g", "conv_b_pw1", "conv_w_dw", "conv_b_dw", "conv_norm_g", "conv_b_pw2", "attn_q_norm", "attn_k_norm")
_NAMES = ("norm_g", "ffn_w_in", "ffn_w_out", "conv_w_pw1", "conv_b_pw1", "conv_w_dw", "conv_b_dw", "conv_norm_g", "conv_w_pw2",
          "conv_b_pw2", "attn_w_qkv", "attn_q_norm", "attn_k_norm", "attn_w_o")


def _rows8(a, width):
    a = a.reshape(-1, min(a.shape[-1], width))
    return jnp.pad(a, ((0, -a.shape[0] % 8), (0, width - a.shape[1])))


def kernel(x, norm_g, ffn_w_in, ffn_w_out, conv_w_pw1, conv_b_pw1, conv_w_dw, conv_b_dw, conv_norm_g, conv_w_pw2, conv_b_pw2, attn_w_qkv, attn_q_norm, attn_k_norm, attn_w_o, loss_target, m_norm_g, m_ffn_w_in, m_ffn_w_out, m_conv_w_pw1, m_conv_b_pw1, m_conv_w_dw, m_conv_b_dw, m_conv_norm_g, m_conv_w_pw2, m_conv_b_pw2, m_attn_w_qkv, m_attn_q_norm, m_attn_k_norm, m_attn_w_o, v_norm_g, v_ffn_w_in, v_ffn_w_out, v_conv_w_pw1, v_conv_b_pw1, v_conv_w_dw, v_conv_b_dw, v_conv_norm_g, v_conv_w_pw2, v_conv_b_pw2, v_attn_w_qkv, v_attn_q_norm, v_attn_k_norm, v_attn_w_o):
    w = dict(zip(_NAMES, (norm_g, ffn_w_in, ffn_w_out, conv_w_pw1, conv_b_pw1, conv_w_dw, conv_b_dw, conv_norm_g, conv_w_pw2,
                          conv_b_pw2, attn_w_qkv, attn_q_norm, attn_k_norm, attn_w_o)))
    m = dict(zip(_NAMES, (m_norm_g, m_ffn_w_in, m_ffn_w_out, m_conv_w_pw1, m_conv_b_pw1, m_conv_w_dw, m_conv_b_dw, m_conv_norm_g,
                          m_conv_w_pw2, m_conv_b_pw2, m_attn_w_qkv, m_attn_q_norm, m_attn_k_norm, m_attn_w_o)))
    v = dict(zip(_NAMES, (v_norm_g, v_ffn_w_in, v_ffn_w_out, v_conv_w_pw1, v_conv_b_pw1, v_conv_w_dw, v_conv_b_dw, v_conv_norm_g,
                          v_conv_w_pw2, v_conv_b_pw2, v_attn_w_qkv, v_attn_q_norm, v_attn_k_norm, v_attn_w_o)))
    T, D = x.shape[1:]
    F = ffn_w_out.shape[2] * N_CHIPS
    HE = attn_w_o.shape[1] * N_CHIPS
    cx, cy, cc = lax.axis_index("x"), lax.axis_index("y"), lax.axis_index("c")
    me = 2 * cx + cy
    scal = jnp.stack([cc, me]).astype(jnp.int32)

    ffn_in = lambda lf: _Sharded(f"ffn_w_in_{lf}", (2, D // 2, 2 * F), 0, 2, "ffn_w_in", lf)
    ffn_out = lambda lf: _Sharded(f"ffn_w_out_{lf}", (4, F // 4, D), 1, 0, "ffn_w_out", lf)
    items = [
        ffn_in(0), ffn_out(0),
        _Sharded("conv_w_pw1", (2, D // 2, 2 * D), 0, 2, "conv_w_pw1"), _Sharded("conv_w_pw2", (4, D // 4, D), 1, 0, "conv_w_pw2"),
        ffn_in(1), ffn_out(1), ffn_in(2), ffn_out(2),
        _Sharded("attn_w_qkv", (2, D // 2, 9 * HE), 0, 2, "attn_w_qkv"), _Sharded("attn_w_o", (4, HE // 4, D), 1, 0, "attn_w_o"),
        ffn_in(3), ffn_out(3),
    ]
    mat_shapes = {"ffn_w_in": (D, 2 * F), "ffn_w_out": (F, D), "conv_w_pw1": (D, 2 * D), "conv_w_pw2": (D, D),
                  "attn_w_qkv": (D, 9 * HE), "attn_w_o": (HE, D)}

    def as_shards(a, n):
        it = next(i for i in items if i.src == n)
        return a.reshape(((4,) if it.slab is not None else ()) + it.shard3)

    norm_full, dw_full = _gather_small([norm_g.reshape(6, D // 4), conv_w_dw.reshape(CONV_WIDTH, D // 4)])
    ssems, rsems, fulls = _gather_start(items, [_cast_place(it, as_shards(w[it.src], it.src), scal) for it in items])

    def block_weights(k, after):
        sel = slice(2 * k, 2 * k + 2)
        got, fsems, gsems = _gather_forward(items[sel], fulls[sel], ssems[sel], rsems[sel], after, f"gather_forward_{k}")
        done = _gather_finish(items[sel], got, fsems, gsems, f"gather_finish_{k}")
        return [a.reshape(mat_shapes[it.src]) for a, it in zip(done, items[sel])]

    res = {}
    flight = []

    def advance(entry, k, dx):
        stage, its, st = entry
        n = len(its)
        if stage == 1:
            ssem, rsem, arrs, g32 = st
            got = _exchange_wait(f"grads_pair_wait_{k}", arrs, ssem, rsem, _pair_copies(its), dx)[n:]
            sums = [_pair_add(it, g, r, scal) for it, g, r in zip(its, g32, got)]
            land = [lax.empty((3,) + it.part3, BF16) for it in its]
            ssem, rsem, arrs, tok = _exchange_start(f"grads_chip_start_{k}", [p[1] for p in sums] + land, [3] * n, _chip_copies(its))
            return (2, its, (ssem, rsem, arrs, [p[0] for p in sums])), tok
        if stage == 2:
            ssem, rsem, arrs, p32 = st
            got = _exchange_wait(f"grads_chip_wait_{k}", arrs, ssem, rsem, _chip_copies(its), dx)[n:]
            red = [_chip_reduce(it, p, r, scal) for it, p, r in zip(its, p32, got)]
            ssem, rsem, arrs, tok = _exchange_start(f"grads_fill_start_{k}", red, [1] * n, _fill_copies(its))
            return (3, its, (ssem, rsem, arrs)), tok
        ssem, rsem, arrs = st
        for it, g in zip(its, _exchange_wait(f"grads_fill_wait_{k}", arrs, ssem, rsem, _fill_copies(its), dx)):
            nm = it.src
            res[nm] = _adam(it, g, as_shards(w[nm], nm), as_shards(m[nm], nm), as_shards(v[nm], nm), prev=res.get(nm))
        return None, None

    def step_flight(dx):
        toks, left = [], []
        for k, entry in flight:
            entry, tok = advance(entry, k, dx)
            if entry is not None:
                left.append((k, entry))
                toks.append(tok)
        flight[:] = left
        return toks

    def grads_ready(k, pairs, dx):
        its = items[2 * k:2 * k + 2]
        toks = step_flight(dx)
        g32 = [p[0].reshape(it.full3) for p, it in zip(pairs, its)]
        g16 = [p[1].reshape(it.full3) for p, it in zip(pairs, its)]
        land = [lax.empty(it.pair3, BF16) for it in its]
        ssem, rsem, arrs, tok = _exchange_start(f"grads_pair_start_{k}", g16 + land, [1] * len(its), _pair_copies(its))
        flight.append((k, (1, its, (ssem, rsem, arrs, g32))))
        return toks + [tok]

    sq, dx, grads = _local_step(x[0], loss_target[0], norm_full.reshape(2, 3, D), conv_b_pw1, dw_full, conv_b_dw, conv_norm_g,
                                conv_b_pw2, attn_q_norm[0], attn_k_norm[0], block_weights, grads_ready)
    loss = lax.psum(0.5 * jnp.sum(sq) / D, ("x", "y", "c"))
    while flight:
        step_flight(dx)

    out_g, out_d, out_m, out_v = {}, {}, {}, {}
    for n in _BIG:
        out_g[n], out_d[n], out_m[n], out_v[n] = (a.reshape(w[n].shape) for a in res[n])

    parts = [_rows8(grads[n], D) for n in _SMALL]
    tot = _allreduce_small(jnp.concatenate(parts, axis=0))
    sg, r0 = {}, 0
    for n, p in zip(_SMALL, parts):
        last = grads[n].shape[-1]
        g = tot[r0:r0 + grads[n].size // min(last, D), :min(last, D)].reshape(-1, last)
        r0 += p.shape[0]
        if n in ("norm_g", "conv_w_dw"):
            g = lax.dynamic_slice_in_dim(g, me * (D // 4), D // 4, axis=1)
        sg[n] = g
    flat = lambda a: a.reshape(-1, a.shape[-1])
    ds, ms, vs = _adam_small([sg[n] for n in _SMALL], [flat(w[n]) for n in _SMALL], [flat(m[n]) for n in _SMALL],
                             [flat(v[n]) for n in _SMALL])
    for i, n in enumerate(_SMALL):
        out_g[n], out_d[n], out_m[n], out_v[n] = (a.reshape(w[n].shape) for a in (sg[n], ds[i], ms[i], vs[i]))

    return (loss, dx[None], *[out_g[n] for n in _NAMES], *[out_d[n] for n in _NAMES], *[out_m[n] for n in _NAMES],
            *[out_v[n] for n in _NAMES])
```

```python
import functools

import jax
import jax.numpy as jnp
from jax import lax
from jax.experimental import pallas as pl
from jax.experimental.pallas import tpu as pltpu

F32 = jnp.float32
BF16 = jnp.bfloat16
MESH = pl.DeviceIdType.MESH

NORM_EPS = 1e-6
CONV_WIDTH = 31
ATTN_GROUPS = ((128, 1), (512, 4), (2048, 16))
ATTN_BLOCK = 128
HEAD_DIM = 128
N_CHIPS = 4

ADAM_LR = 0.001
ADAM_B1 = 0.9
ADAM_B2 = 0.999
ADAM_EPS = 1e-08
ADAM_WD = 0.01
ADAM_STEP = 10

VMEM_LIMIT = 56 * 1024 * 1024
NT_DIMS = (((1,), (1,)), ((), ()))
TN_DIMS = (((0,), (0,)), ((), ()))


def _pick(n, pref, mult):
    t = (min(n, pref) // mult) * mult
    while t >= mult:
        if n % t == 0:
            return t
        t -= mult
    return n


def _call(body, name, grid, in_specs, out_specs, out_shape, scratch=(), aliases=None, prefetch=0, after=()):
    params = pltpu.CompilerParams(dimension_semantics=("arbitrary",) * len(grid), vmem_limit_bytes=VMEM_LIMIT)
    after = tuple(after)
    if after:
        inner, n_in = body, prefetch + len(in_specs)

        def body(*refs):
            return inner(*refs[:n_in], *refs[n_in + len(after):])

        in_specs = list(in_specs) + [pl.BlockSpec(memory_space=pl.ANY)] * len(after)
    if prefetch:
        spec = pltpu.PrefetchScalarGridSpec(
            num_scalar_prefetch=prefetch, grid=grid, in_specs=in_specs, out_specs=out_specs, scratch_shapes=list(scratch)
        )
        call = pl.pallas_call(body, name=name, grid_spec=spec, out_shape=out_shape, compiler_params=params,
                              input_output_aliases=aliases or {})
    else:
        call = pl.pallas_call(body, name=name, grid=grid, in_specs=in_specs, out_specs=out_specs, out_shape=out_shape,
                              scratch_shapes=list(scratch), compiler_params=params, input_output_aliases=aliases or {})
    return lambda *args: call(*args, *after)


def _sds(shape, dtype):
    return jax.ShapeDtypeStruct(shape, dtype)


def _sig(x):
    return 1.0 / (1.0 + jnp.exp(-x))


def _rstd(x):
    return lax.rsqrt(jnp.mean(x * x, axis=-1, keepdims=True) + NORM_EPS)


def _norm_bwd(dy, xhat, r, g):
    dxh = dy * g
    return r * (dxh - xhat * jnp.mean(dxh * xhat, axis=-1, keepdims=True))


def _dot(a, b):
    return jnp.dot(a, b, preferred_element_type=F32)


def _dot_nt(a, b):
    return lax.dot_general(a, b, NT_DIMS, preferred_element_type=F32)


def _dot_tn(a, b):
    return lax.dot_general(a, b, TN_DIMS, preferred_element_type=F32)


def _ffn_tile(F):
    return _pick(F, 1408, 128)


def _ffn_fwd(x, g, w_in, w_out, name):
    T, D = x.shape
    F = w_out.shape[0]
    tf = _ffn_tile(F)
    nf = F // tf
    tm = _pick(T, 512, 8)

    def body(x_ref, g_ref, wg_ref, wu_ref, wo_ref, xo_ref, gate_ref, up_ref, h_sc, acc_sc):
        j = pl.program_id(1)

        @pl.when(j == 0)
        def _():
            xv = x_ref[...]
            h_sc[...] = (xv * _rstd(xv) * g_ref[...]).astype(BF16)
            acc_sc[...] = jnp.zeros_like(acc_sc)

        h = h_sc[...]
        gate = _dot(h, wg_ref[...])
        up = _dot(h, wu_ref[...])
        gate_ref[...] = gate.astype(BF16)
        up_ref[...] = up.astype(BF16)
        a = (gate * _sig(gate) * up).astype(BF16)
        acc_sc[...] += _dot(a, wo_ref[...])

        @pl.when(j == nf - 1)
        def _():
            xo_ref[...] = x_ref[...] + 0.5 * acc_sc[...]

    return _call(
        body, name, (T // tm, nf),
        [pl.BlockSpec((tm, D), lambda i, j: (i, 0)),
         pl.BlockSpec((1, D), lambda i, j: (0, 0)),
         pl.BlockSpec((D, tf), lambda i, j: (0, j)),
         pl.BlockSpec((D, tf), lambda i, j: (0, nf + j)),
         pl.BlockSpec((tf, D), lambda i, j: (j, 0))],
        [pl.BlockSpec((tm, D), lambda i, j: (i, 0)),
         pl.BlockSpec((tm, tf), lambda i, j: (i, j)),
         pl.BlockSpec((tm, tf), lambda i, j: (i, j))],
        [_sds((T, D), F32), _sds((T, F), BF16), _sds((T, F), BF16)],
        scratch=[pltpu.VMEM((tm, D), BF16), pltpu.VMEM((tm, D), F32)],
    )(x, g, w_in, w_in, w_out)


def _ffn_bwd(x, g, dy, gate, up, w_in, w_out, name, after=()):
    T, D = x.shape
    F = w_out.shape[0]
    tf = _ffn_tile(F)
    nf = F // tf
    tm = _pick(T, 256, 8)

    def body(x_ref, g_ref, dy_ref, gate_ref, up_ref, wg_ref, wu_ref, wo_ref, dx_ref, dg_ref, dgate_ref, dup_ref, a_ref, h_ref,
             dyb_ref, dh_sc):
        i = pl.program_id(0)
        j = pl.program_id(1)

        @pl.when(j == 0)
        def _():
            xv = x_ref[...]
            h_ref[...] = (xv * _rstd(xv) * g_ref[...]).astype(BF16)
            dyb_ref[...] = (0.5 * dy_ref[...]).astype(BF16)
            dh_sc[...] = jnp.zeros_like(dh_sc)

        @pl.when((i == 0) & (j == 0))
        def _():
            dg_ref[...] = jnp.zeros_like(dg_ref)

        gate = gate_ref[...].astype(F32)
        up = up_ref[...].astype(F32)
        sg = _sig(gate)
        sl = gate * sg
        a_ref[...] = (sl * up).astype(BF16)
        da = _dot_nt(dyb_ref[...], wo_ref[...])
        dgate = (da * up * (sg * (1.0 + gate * (1.0 - sg)))).astype(BF16)
        dup = (da * sl).astype(BF16)
        dgate_ref[...] = dgate
        dup_ref[...] = dup
        dh_sc[...] += _dot_nt(dgate, wg_ref[...]) + _dot_nt(dup, wu_ref[...])

        @pl.when(j == nf - 1)
        def _():
            xv = x_ref[...]
            r = _rstd(xv)
            xh = xv * r
            dh = dh_sc[...]
            dx_ref[...] = dy_ref[...] + _norm_bwd(dh, xh, r, g_ref[...])
            dg_ref[...] += jnp.sum(dh * xh, axis=0, keepdims=True)

    return _call(
        body, name, (T // tm, nf),
        [pl.BlockSpec((tm, D), lambda i, j: (i, 0)),
         pl.BlockSpec((1, D), lambda i, j: (0, 0)),
         pl.BlockSpec((tm, D), lambda i, j: (i, 0)),
         pl.BlockSpec((tm, tf), lambda i, j: (i, j)),
         pl.BlockSpec((tm, tf), lambda i, j: (i, j)),
         pl.BlockSpec((D, tf), lambda i, j: (0, j)),
         pl.BlockSpec((D, tf), lambda i, j: (0, nf + j)),
         pl.BlockSpec((tf, D), lambda i, j: (j, 0))],
        [pl.BlockSpec((tm, D), lambda i, j: (i, 0)),
         pl.BlockSpec((1, D), lambda i, j: (0, 0)),
         pl.BlockSpec((tm, tf), lambda i, j: (i, j)),
         pl.BlockSpec((tm, tf), lambda i, j: (i, j)),
         pl.BlockSpec((tm, tf), lambda i, j: (i, j)),
         pl.BlockSpec((tm, D), lambda i, j: (i, 0)),
         pl.BlockSpec((tm, D), lambda i, j: (i, 0))],
        [_sds((T, D), F32), _sds((1, D), F32), _sds((T, F), BF16), _sds((T, F), BF16), _sds((T, F), BF16), _sds((T, D), BF16),
         _sds((T, D), BF16)],
        scratch=[pltpu.VMEM((tm, D), F32)], after=after,
    )(x, g, dy, gate, up, w_in, w_in, w_out)


def _mm_tn(a, b, bm, bn, out_shape, out_block, out_map, scale, name, prev=None):
    K, M = a.shape
    N = b.shape[1]

    def body(a_ref, b_ref, *rest):
        o_ref, ob_ref = rest[-2:]
        o = _dot_tn(a_ref[...].astype(BF16), b_ref[...].astype(BF16)) * scale
        o_ref[...] = o
        ob_ref[...] = o.astype(BF16)

    in_specs = [pl.BlockSpec((K, bm), lambda mi, ni: (0, mi)), pl.BlockSpec((K, bn), lambda mi, ni: (0, ni))]
    args = [a, b]
    aliases = None
    if prev is not None:
        in_specs += [pl.BlockSpec(memory_space=pl.ANY)] * 2
        args += list(prev)
        aliases = {2: 0, 3: 1}
    ospec = pl.BlockSpec(out_block, out_map)
    return _call(body, name, (M // bm, N // bn), in_specs, [ospec, ospec],
                 [_sds(out_shape, F32), _sds(out_shape, BF16)], aliases=aliases)(*args)


def _conv_pre(x, g, w1, b1, name):
    T, D = x.shape
    tm = _pick(T, 512, 8)

    def body(x_ref, g_ref, w_ref, b_ref, ag_ref, u_ref, h_ref):
        xv = x_ref[...]
        h = (xv * _rstd(xv) * g_ref[...]).astype(BF16)
        h_ref[...] = h
        ag = _dot(h, w_ref[...]) + b_ref[...]
        ag_ref[...] = ag.astype(BF16)
        u_ref[...] = ag[:, :D] * _sig(ag[:, D:])

    return _call(
        body, name, (T // tm,),
        [pl.BlockSpec((tm, D), lambda i: (i, 0)), pl.BlockSpec((1, D), lambda i: (0, 0)),
         pl.BlockSpec((D, 2 * D), lambda i: (0, 0)), pl.BlockSpec((1, 2 * D), lambda i: (0, 0))],
        [pl.BlockSpec((tm, 2 * D), lambda i: (i, 0)), pl.BlockSpec((tm, D), lambda i: (i, 0)),
         pl.BlockSpec((tm, D), lambda i: (i, 0))],
        [_sds((T, 2 * D), BF16), _sds((T, D), F32), _sds((T, D), BF16)],
    )(x, g, w1, b1)


_DW_PAD = 32
_DW_CHUNK = 256


def _dwconv(u, w, b, name):
    T, D = u.shape
    K = w.shape[0]
    ch = _pick(T, _DW_CHUNK, 8)
    lead = _DW_PAD - (K - 1)

    def body(u_ref, w_ref, b_ref, c_ref, ext):
        ext[pl.ds(0, _DW_PAD), :] = jnp.zeros((_DW_PAD, 128), F32)
        ext[pl.ds(_DW_PAD, T), :] = u_ref[...]
        for c0 in range(0, T, ch):
            acc = jnp.zeros((ch, 128), F32) + b_ref[...]
            for k in range(K):
                acc = acc + w_ref[pl.ds(k, 1), :] * ext[pl.ds(c0 + lead + k, ch), :]
            c_ref[pl.ds(c0, ch), :] = acc

    return _call(
        body, name, (D // 128,),
        [pl.BlockSpec((T, 128), lambda i: (0, i)), pl.BlockSpec((K, 128), lambda i: (0, i)),
         pl.BlockSpec((1, 128), lambda i: (0, i))],
        [pl.BlockSpec((T, 128), lambda i: (0, i))],
        [_sds((T, D), F32)],
        scratch=[pltpu.VMEM((T + _DW_PAD, 128), F32)],
    )(u, w, b)[0]


def _dwconv_bwd(dc, u, w, name):
    T, D = u.shape
    K = w.shape[0]
    ch = _pick(T, _DW_CHUNK, 8)
    lead = _DW_PAD - (K - 1)

    def body(dc_ref, u_ref, w_ref, du_ref, dw_ref, db_ref, uext, dext):
        uext[pl.ds(0, _DW_PAD), :] = jnp.zeros((_DW_PAD, 128), F32)
        uext[pl.ds(_DW_PAD, T), :] = u_ref[...]
        dext[pl.ds(0, T), :] = dc_ref[...]
        dext[pl.ds(T, _DW_PAD), :] = jnp.zeros((_DW_PAD, 128), F32)
        dws =[jnp.zeros((8, 128), F32) for _ in range(K)]
        dbs = jnp.zeros((8, 128), F32)
        for c0 in range(0, T, ch):
            dcv = dext[pl.ds(c0, ch), :]
            dbs = dbs + jnp.sum(dcv.reshape(ch // 8, 8, 128), axis=0)
            acc = jnp.zeros((ch, 128), F32)
            for k in range(K):
                acc = acc + w_ref[pl.ds(k, 1), :] * dext[pl.ds(c0 + (K - 1) - k, ch), :]
                prod = dcv * uext[pl.ds(c0 + lead + k, ch), :]
                dws[k] = dws[k] + jnp.sum(prod.reshape(ch // 8, 8, 128), axis=0)
            du_ref[pl.ds(c0, ch), :] = acc
        for k in range(K):
            dw_ref[pl.ds(k, 1), :] = jnp.sum(dws[k], axis=0, keepdims=True)
        db_ref[...] = jnp.sum(dbs, axis=0, keepdims=True)

    return _call(
        body, name, (D // 128,),
        [pl.BlockSpec((T, 128), lambda i: (0, i)), pl.BlockSpec((T, 128), lambda i: (0, i)),
         pl.BlockSpec((K, 128), lambda i: (0, i))],
        [pl.BlockSpec((T, 128), lambda i: (0, i)), pl.BlockSpec((K, 128), lambda i: (0, i)),
         pl.BlockSpec((1, 128), lambda i: (0, i))],
        [_sds((T, D), F32), _sds((K, D), F32), _sds((1, D), F32)],
        scratch=[pltpu.VMEM((T + _DW_PAD, 128), F32), pltpu.VMEM((T + _DW_PAD, 128), F32)],
    )(dc, u, w)


def _conv_post(c, x, ng, w2, b2, name):
    T, D = x.shape
    tm = _pick(T, 512, 8)

    def body(c_ref, x_ref, ng_ref, w_ref, b_ref, xo_ref, s_ref):
        cv = c_ref[...]
        n = cv * _rstd(cv) * ng_ref[...]
        s = (n * _sig(n)).astype(BF16)
        s_ref[...] = s
        xo_ref[...] = x_ref[...] + _dot(s, w_ref[...]) + b_ref[...]

    row = lambda i: (i, 0)
    fix = lambda i: (0, 0)
    return _call(
        body, name, (T // tm,),
        [pl.BlockSpec((tm, D), row), pl.BlockSpec((tm, D), row), pl.BlockSpec((1, D), fix),
         pl.BlockSpec((D, D), fix), pl.BlockSpec((1, D), fix)],
        [pl.BlockSpec((tm, D), row), pl.BlockSpec((tm, D), row)],
        [_sds((T, D), F32), _sds((T, D), BF16)],
    )(c, x, ng, w2, b2)


def _conv_post_bwd(dy, c, ng, w2, name, after=()):
    T, D = dy.shape
    tm = _pick(T, 512, 8)

    def body(dy_ref, c_ref, ng_ref, w_ref, dc_ref, dng_ref, db_ref):
        @pl.when(pl.program_id(0) == 0)
        def _():
            dng_ref[...] = jnp.zeros_like(dng_ref)
            db_ref[...] = jnp.zeros_like(db_ref)

        dyv = dy_ref[...]
        ds = _dot_nt(dyv.astype(BF16), w_ref[...])
        cv = c_ref[...]
        r = _rstd(cv)
        ch = cv * r
        n = ch * ng_ref[...]
        sg = _sig(n)
        dn = ds * (sg * (1.0 + n * (1.0 - sg)))
        dc_ref[...] = _norm_bwd(dn, ch, r, ng_ref[...])
        dng_ref[...] += jnp.sum(dn * ch, axis=0, keepdims=True)
        db_ref[...] += jnp.sum(dyv, axis=0, keepdims=True)

    row = lambda i: (i, 0)
    fix = lambda i: (0, 0)
    return _call(
        body, name, (T // tm,),
        [pl.BlockSpec((tm, D), row), pl.BlockSpec((tm, D), row), pl.BlockSpec((1, D), fix), pl.BlockSpec((D, D), fix)],
        [pl.BlockSpec((tm, D), row), pl.BlockSpec((1, D), fix), pl.BlockSpec((1, D), fix)],
        [_sds((T, D), F32), _sds((1, D), F32), _sds((1, D), F32)], after=after,
    )(dy, c, ng, w2)


def _conv_pre_bwd(du, ag, x, g, dy, w1, name):
    T, D = x.shape
    tm = _pick(T, 512, 8)

    def body(du_ref, ag_ref, x_ref, g_ref, dy_ref, w_ref, dx_ref, dg_ref, dag_ref, db_ref):
        @pl.when(pl.program_id(0) == 0)
        def _():
            dg_ref[...] = jnp.zeros_like(dg_ref)
            db_ref[...] = jnp.zeros_like(db_ref)

        duv = du_ref[...]
        a = ag_ref[:, :D].astype(F32)
        gt = ag_ref[:, D:].astype(F32)
        sg = _sig(gt)
        da = duv * sg
        dgt = duv * a * sg * (1.0 - sg)
        db_ref[:, :D] += jnp.sum(da, axis=0, keepdims=True)
        db_ref[:, D:] += jnp.sum(dgt, axis=0, keepdims=True)
        dab = da.astype(BF16)
        dgb = dgt.astype(BF16)
        dag_ref[:, :D] = dab
        dag_ref[:, D:] = dgb
        dh = _dot_nt(dab, w_ref[:, :D]) + _dot_nt(dgb, w_ref[:, D:])
        xv = x_ref[...]
        r = _rstd(xv)
        xh = xv * r
        dx_ref[...] = dy_ref[...] + _norm_bwd(dh, xh, r, g_ref[...])
        dg_ref[...] += jnp.sum(dh * xh, axis=0, keepdims=True)

    row = lambda i: (i, 0)
    fix = lambda i: (0, 0)
    return _call(
        body, name, (T // tm,),
        [pl.BlockSpec((tm, D), row), pl.BlockSpec((tm, 2 * D), row), pl.BlockSpec((tm, D), row), pl.BlockSpec((1, D), fix),
         pl.BlockSpec((tm, D), row), pl.BlockSpec((D, 2 * D), fix)],
        [pl.BlockSpec((tm, D), row), pl.BlockSpec((1, D), fix), pl.BlockSpec((tm, 2 * D), row),
         pl.BlockSpec((1, 2 * D), fix)],
        [_sds((T, D), F32), _sds((1, D), F32), _sds((T, 2 * D), BF16), _sds((1, 2 * D), F32)],
    )(du, ag, x, g, dy, w1)


def _row_sums(a):
    ones = jnp.ones((a.shape[1], 128), BF16)
    hi = a.astype(BF16)
    lo = (a - hi.astype(F32)).astype(BF16)
    return _dot(hi, ones) + _dot(lo, ones)


def _head_rstd(x):
    return lax.rsqrt(_row_sums(x * x) * (1.0 / x.shape[1]) + NORM_EPS)


def _attn_qkv(x, g, wqkv, qn, kn, name):
    T, D = x.shape
    N = wqkv.shape[1]
    tn = N // 9
    E = HEAD_DIM
    tm = _pick(T, 512, 8)

    def body(x_ref, g_ref, w_ref, qn_ref, kn_ref, o_ref, a_ref, h_ref):
        j = pl.program_id(1)

        @pl.when(j == 0)
        def _():
            xv = x_ref[...]
            h_ref[...] = (xv * _rstd(xv) * g_ref[...]).astype(BF16)

        res = _dot(h_ref[...], w_ref[...])
        o_ref[...] = res.astype(BF16)
        part = j % 3

        @pl.when(part == 2)
        def _():
            a_ref[...] = res.astype(BF16)

        @pl.when(part < 2)
        def _():
            grp = j // 3
            c = qn_ref[pl.ds(grp, 1), :] * kn_ref[pl.ds(grp, 1), :] * (E ** -0.5)
            fac = jnp.where(part == 0, c, jnp.ones_like(c))
            for h in range(tn // E):
                hs = slice(h * E, (h + 1) * E)
                xh = res[:, hs]
                a_ref[:, hs] = (xh * _head_rstd(xh) * fac).astype(BF16)

    ng = qn.shape[0]
    return _call(
        body, name, (T // tm, 9),
        [pl.BlockSpec((tm, D), lambda i, j: (i, 0)), pl.BlockSpec((1, D), lambda i, j: (0, 0)),
         pl.BlockSpec((D, tn), lambda i, j: (0, j)), pl.BlockSpec((ng, E), lambda i, j: (0, 0)),
         pl.BlockSpec((ng, E), lambda i, j: (0, 0))],
        [pl.BlockSpec((tm, tn), lambda i, j: (i, j)), pl.BlockSpec((tm, tn), lambda i, j: (i, j)),
         pl.BlockSpec((tm, D), lambda i, j: (i, 0))],
        [_sds((T, N), BF16), _sds((T, N), BF16), _sds((T, D), BF16)],
    )(x, g, wqkv, qn, kn)


def _band_mask(q, steps, nblk):
    i = lax.broadcasted_iota(jnp.int32, (q, 2 * q), 0)
    j = lax.broadcasted_iota(jnp.int32, (q, 2 * q), 1)
    diff = q + i - j
    first_key = jnp.where(nblk > 0, 0, q)
    return (diff >= 0) & (diff <= steps) & (j >= first_key)


def _per_row(blk, width):
    e = blk.shape[1]
    if width % e == 0:
        return jnp.concatenate([blk] * (width // e), axis=1)
    return jnp.broadcast_to(blk[:, :1], (blk.shape[0], width))


def _streams(a, dil, to_streams, name, col0=0, ncols=None):
    T, C = a.shape
    ncols = C if ncols is None else ncols
    Q = ATTN_BLOCK
    run = Q * dil
    reps = max(1, min(2048 // run, T // run))
    while T % (run * reps):
        reps -= 1
    rows = run * reps
    cw = _pick(ncols, 512, 128)
    ns = cw // 128

    def body(a_ref, o_ref, scr):
        for s in range(ns):
            ls = slice(s * 128, (s + 1) * 128)
            slab = scr.at[s]
            if to_streams:
                slab[...] = a_ref[:, ls].astype(F32)
                for u in range(reps):
                    for r in range(dil):
                        o_ref[pl.ds(u * run + r * Q, Q), ls] = slab[pl.ds(u * run + r, Q, stride=dil), :].astype(a.dtype)
            else:
                for u in range(reps):
                    for r in range(dil):
                        slab[pl.ds(u * run + r, Q, stride=dil), :] = a_ref[pl.ds(u * run + r * Q, Q), ls].astype(F32)
                o_ref[:, ls] = slab[...].astype(a.dtype)

    return _call(
        body, name, (T // rows, ncols // cw),
        [pl.BlockSpec((rows, cw), lambda i, j: (i, col0 // cw + j))],
        [pl.BlockSpec((rows, cw), lambda i, j: (i, j))],
        [_sds((T, ncols), a.dtype)],
        scratch=[pltpu.VMEM((ns, rows, 128), F32)],
    )(a)[0]


def _attn_fwd(qkv, base, HE, window, dil, name):
    T = qkv.shape[0]
    H = HE // HEAD_DIM
    E = HEAD_DIM
    Q = ATTN_BLOCK
    nb = T // dil // Q
    steps = window // dil

    def body(q_ref, kc_ref, kp_ref, vc_ref, vp_ref, o_ref, l_ref):
        n = pl.program_id(1)
        valid = _band_mask(Q, steps, n)
        ones = jnp.ones((2 * Q, E), BF16)
        for h in range(H):
            hs = slice(h * E, (h + 1) * E)
            k2 = jnp.concatenate([kp_ref[:, hs], kc_ref[:, hs]], axis=0)
            v2 = jnp.concatenate([vp_ref[:, hs], vc_ref[:, hs]], axis=0)
            s = jnp.where(valid, _dot_nt(q_ref[:, hs], k2), -1e30)
            m = jnp.max(s, axis=-1, keepdims=True)
            p = jnp.exp(s - m).astype(BF16)
            acc = _dot(p, jnp.concatenate([v2, ones], axis=1))
            l = acc[:, E:]
            o_ref[:, hs] = (acc[:, :E] * (1.0 / l)).astype(o_ref.dtype)
            l_ref[:, hs] = m + jnp.log(l)

    blk = lambda s, back: pl.BlockSpec((Q, HE), lambda r, n: (jnp.maximum(n - back, 0) * dil + r, base + s))
    out = pl.BlockSpec((Q, HE), lambda r, n: (n * dil + r, 0))
    return _call(
        body, name, (dil, nb),
        [blk(0, 0), blk(1, 0), blk(1, 1), blk(2, 0), blk(2, 1)],
        [out, out],
        [_sds((T, HE), BF16), _sds((T, HE), F32)],
    )(qkv, qkv, qkv, qkv, qkv)


def _attn_merge(os, lses, x, wo, name):
    T, D = x.shape
    HE = wo.shape[0]
    tm = _pick(T, 512, 8)
    ng = len(os)

    def body(*refs):
        o_refs = refs[:ng]
        l_refs = refs[ng:2 * ng]
        x_ref, w_ref, xo_ref, om_ref, lt_ref = refs[2 * ng:]
        ls = [r[...] for r in l_refs]
        m = functools.reduce(jnp.maximum, ls)
        es = [jnp.exp(l - m) for l in ls]
        tot = functools.reduce(lambda a, b: a + b, es)
        inv = 1.0 / tot
        om = functools.reduce(lambda a, b: a + b, [e * inv * r[...] for e, r in zip(es, o_refs)])
        omb = om.astype(BF16)
        om_ref[...] = omb
        lt_ref[...] = m + jnp.log(tot)
        xo_ref[...] = x_ref[...] + _dot(omb, w_ref[...])

    row = lambda i: (i, 0)
    fix = lambda i: (0, 0)
    return _call(
        body, name, (T // tm,),
        [pl.BlockSpec((tm, HE), row)] * (2 * ng) + [pl.BlockSpec((tm, D), row), pl.BlockSpec((HE, D), fix)],
        [pl.BlockSpec((tm, D), row), pl.BlockSpec((tm, HE), row), pl.BlockSpec((tm, HE), row)],
        [_sds((T, D), F32), _sds((T, HE), BF16), _sds((T, HE), F32)],
    )(*os, *lses, x, wo)


def _attn_out_bwd(dy, om, wo, name, after=()):
    T, D = dy.shape
    HE = wo.shape[0]
    E = HEAD_DIM
    tm = _pick(T, 512, 8)

    def body(dy_ref, om_ref, w_ref, dom_ref, dl_ref):
        dom = _dot_nt(dy_ref[...].astype(BF16), w_ref[...])
        dom_ref[...] = dom.astype(BF16)
        prod = dom * om_ref[...].astype(F32)
        for h in range(HE // E):
            hs = slice(h * E, (h + 1) * E)
            dl_ref[:, hs] = jnp.broadcast_to(jnp.sum(prod[:, hs], axis=-1, keepdims=True), (tm, E))

    row = lambda i: (i, 0)
    return _call(
        body, name, (T // tm,),
        [pl.BlockSpec((tm, D), row), pl.BlockSpec((tm, HE), row), pl.BlockSpec((HE, D), lambda i: (0, 0))],
        [pl.BlockSpec((tm, HE), row), pl.BlockSpec((tm, HE), row)],
        [_sds((T, HE), BF16), _sds((T, HE), F32)], after=after,
    )(dy, om, wo)


def _attn_bwd(qkv, base, HE, dom, lse, delta, window, dil, name):
    T = qkv.shape[0]
    H = HE // HEAD_DIM
    E = HEAD_DIM
    Q = ATTN_BLOCK
    nb = T // dil // Q
    steps = window // dil

    def body(q_ref, kc_ref, kp_ref, vc_ref, vp_ref, do_ref, l_ref, dl_ref, dq_ref, dk_ref, dv_ref, ck_sc, cv_sc):
        n = pl.program_id(1)

        @pl.when(n == 0)
        def _():
            ck_sc[...] = jnp.zeros_like(ck_sc)
            cv_sc[...] = jnp.zeros_like(cv_sc)

        @pl.when(n < nb)
        def _():
            valid = _band_mask(Q, steps, n)
            for h in range(H):
                hs = slice(h * E, (h + 1) * E)
                q = q_ref[:, hs]
                do = do_ref[:, hs]
                k2 = jnp.concatenate([kp_ref[:, hs], kc_ref[:, hs]], axis=0)
                v2 = jnp.concatenate([vp_ref[:, hs], vc_ref[:, hs]], axis=0)
                p = jnp.where(valid, jnp.exp(_dot_nt(q, k2) - _per_row(l_ref[:, hs], 2 * Q)), 0.0)
                ds = (p * (_dot_nt(do, v2) - _per_row(dl_ref[:, hs], 2 * Q))).astype(BF16)
                dq_ref[:, hs] = _dot(ds, k2).astype(BF16)
                dk2 = _dot_tn(ds, q)
                dv2 = _dot_tn(p.astype(BF16), do)
                dk_ref[:, hs] = (ck_sc[:, hs] + dk2[:Q]).astype(BF16)
                dv_ref[:, hs] = (cv_sc[:, hs] + dv2[:Q]).astype(BF16)
                ck_sc[:, hs] = dk2[Q:]
                cv_sc[:, hs] = dv2[Q:]

        @pl.when(n == nb)
        def _():
            dk_ref[...] = ck_sc[...].astype(BF16)
            dv_ref[...] = cv_sc[...].astype(BF16)

    nq = lambda n: jnp.minimum(n, nb - 1)
    blk = lambda s, back: pl.BlockSpec((Q, HE), lambda r, n: (jnp.maximum(nq(n) - back, 0) * dil + r, base + s))
    qblk = pl.BlockSpec((Q, HE), lambda r, n: (nq(n) * dil + r, 0))
    kblk = pl.BlockSpec((Q, HE), lambda r, n: (jnp.maximum(n - 1, 0) * dil + r, 0))
    return _call(
        body, name, (dil, nb + 1),
        [blk(0, 0), blk(1, 0), blk(1, 1), blk(2, 0), blk(2, 1), qblk, qblk, qblk],
        [qblk, kblk, kblk],
        [_sds((T, HE), BF16)] * 3,
        scratch=[pltpu.VMEM((Q, HE), F32), pltpu.VMEM((Q, HE), F32)],
    )(qkv, qkv, qkv, qkv, qkv, dom, lse, delta)


def _qk_norm_bwd(dq, dk, qkv, base, HE, gq, gk, name):
    T = dq.shape[0]
    E = HEAD_DIM
    tm = _pick(T, 512, 8)
    scale = E ** -0.5

    def body(dq_ref, dk_ref, q_ref, k_ref, gq_ref, gk_ref, oq_ref, ok_ref, dgq_ref, dgk_ref):
        @pl.when(pl.program_id(0) == 0)
        def _():
            dgq_ref[...] = jnp.zeros_like(dgq_ref)
            dgk_ref[...] = jnp.zeros_like(dgk_ref)

        gqv = gq_ref[...]
        gkv = gk_ref[...]
        c = gqv * gkv * scale
        dc = jnp.zeros((1, E), F32)
        for h in range(HE // E):
            hs = slice(h * E, (h + 1) * E)
            q = q_ref[:, hs].astype(F32)
            rq = _head_rstd(q)
            qh = q * rq
            a = dq_ref[:, hs].astype(F32)
            dc = dc + jnp.sum(a * qh, axis=0, keepdims=True)
            dqh = a * c
            oq_ref[:, hs] = (rq * (dqh - qh * (_row_sums(dqh * qh) * (1.0 / E)))).astype(BF16)
            k = k_ref[:, hs].astype(F32)
            rk = _head_rstd(k)
            kh = k * rk
            b = dk_ref[:, hs].astype(F32)
            ok_ref[:, hs] = (rk * (b - kh * (_row_sums(b * kh) * (1.0 / E)))).astype(BF16)
        dgq_ref[...] += dc * (gkv * scale)
        dgk_ref[...] += dc * (gqv * scale)

    row = lambda i: (i, 0)
    vec = pl.BlockSpec((1, E), lambda i: (0, 0))
    return _call(
        body, name, (T // tm,),
        [pl.BlockSpec((tm, HE), row), pl.BlockSpec((tm, HE), row), pl.BlockSpec((tm, HE), lambda i: (i, base)),
         pl.BlockSpec((tm, HE), lambda i: (i, base + 1)), vec, vec],
        [pl.BlockSpec((tm, HE), row), pl.BlockSpec((tm, HE), row), vec, vec],
        [_sds((T, HE), BF16), _sds((T, HE), BF16), _sds((1, E), F32), _sds((1, E), F32)],
    )(dq, dk, qkv, qkv, gq, gk)


def _attn_qkv_bwd(dqkv, x, g, dy, wqkv, name):
    T, D = x.shape
    HE = wqkv.shape[1] // 9
    tm = _pick(T, 512, 8)

    def body(*refs):
        d_refs = refs[:9]
        x_ref, g_ref, dy_ref, w_ref, dx_ref, dg_ref, dh_sc = refs[9:]
        i = pl.program_id(0)
        j = pl.program_id(1)

        @pl.when((i == 0) & (j == 0))
        def _():
            dg_ref[...] = jnp.zeros_like(dg_ref)

        @pl.when(j == 0)
        def _():
            dh_sc[...] = jnp.zeros_like(dh_sc)

        for s in range(9):
            @pl.when(j == s)
            def _(s=s):
                dh_sc[...] += _dot_nt(d_refs[s][...], w_ref[...])

        @pl.when(j == 8)
        def _():
            xv = x_ref[...]
            r = _rstd(xv)
            xh = xv * r
            dh = dh_sc[...]
            dx_ref[...] = dy_ref[...] + _norm_bwd(dh, xh, r, g_ref[...])
            dg_ref[...] += jnp.sum(dh * xh, axis=0, keepdims=True)

    row = lambda i, j: (i, 0)
    fix = lambda i, j: (0, 0)
    return _call(
        body, name, (T // tm, 9),
        [pl.BlockSpec((tm, HE), row)] * 9 + [pl.BlockSpec((tm, D), row), pl.BlockSpec((1, D), fix), pl.BlockSpec((tm, D), row),
                                           pl.BlockSpec((D, HE), lambda i, j: (0, j))],
        [pl.BlockSpec((tm, D), row), pl.BlockSpec((1, D), fix)],
        [_sds((T, D), F32), _sds((1, D), F32)],
        scratch=[pltpu.VMEM((tm, D), F32)],
    )(*dqkv, x, g, dy, wqkv)


def _loss_head(y, target, name):
    T, D = y.shape
    tm = _pick(T, 512, 8)

    def body(y_ref, t_ref, dy_ref, sq_ref):
        @pl.when(pl.program_id(0) == 0)
        def _():
            sq_ref[...] = jnp.zeros_like(sq_ref)

        err = y_ref[...] - t_ref[...]
        dy_ref[...] = err * (1.0 / D)
        sq_ref[...] += jnp.sum(err * err, axis=0, keepdims=True)

    row = lambda i: (i, 0)
    return _call(
        body, name, (T // tm,),
        [pl.BlockSpec((tm, D), row), pl.BlockSpec((tm, D), row)],
        [pl.BlockSpec((tm, D), row), pl.BlockSpec((1, D), lambda i: (0, 0))],
        [_sds((T, D), F32), _sds((1, D), F32)],
    )(y, target)


def _local_step(x, target, norm_g, b_pw1, w_dw, b_dw, cng, b_pw2, qn, kn, block_weights, grads_ready):
    T, D = x.shape
    ng = lambda l, k: norm_g[l, k][None, :]
    bn = _pick(D, 256, 128)

    w_in, w_out = [None] * 4, [None] * 4
    w_in[0], w_out[0] = block_weights(0, x)
    x1, *gu0 = _ffn_fwd(x, ng(0, 0), w_in[0], w_out[0], "ffn_fwd_0")
    pw1, pw2 = block_weights(1, x1)
    ag, u, hc = _conv_pre(x1, ng(0, 1), pw1, b_pw1, "conv_pre")
    c = _dwconv(u, w_dw, b_dw, "dwconv")
    x2, s = _conv_post(c, x1, cng, pw2, b_pw2, "conv_post")
    w_in[1], w_out[1] = block_weights(2, x2)
    x3, *gu1 = _ffn_fwd(x2, ng(0, 2), w_in[1], w_out[1], "ffn_fwd_1")
    w_in[2], w_out[2] = block_weights(3, x3)
    x4, *gu2 = _ffn_fwd(x3, ng(1, 0), w_in[2], w_out[2], "ffn_fwd_2")
    wqkv, wo = block_weights(4, x4)
    HE = wo.shape[0]
    F = w_out[0].shape[0]
    bf = _pick(F, 256, 128)
    bh = _pick(HE, 512, 128)
    qkv, att, ha = _attn_qkv(x4, ng(1, 1), wqkv, qn, kn, "attn_qkv")
    qkv_s = [(att, 3 * gi) if dil == 1 else (_streams(att, dil, True, f"qkv_streams_{gi}", 3 * gi * HE, 3 * HE), 0)
             for gi, (_, dil) in enumerate(ATTN_GROUPS)]
    tokens = lambda a, dil, name: a if dil == 1 else _streams(a, dil, False, name)
    streams = lambda a, dil, name: a if dil == 1 else _streams(a, dil, True, name)
    os, lses = [], []
    for gi, (window, dil) in enumerate(ATTN_GROUPS):
        o, l = _attn_fwd(*qkv_s[gi], HE, window, dil, f"attn_fwd_{gi}")
        os.append(tokens(o, dil, f"o_tokens_{gi}"))
        lses.append(tokens(l, dil, f"lse_tokens_{gi}"))
    x5, om, lse = _attn_merge(os, lses, x4, wo, "attn_merge")
    w_in[3], w_out[3] = block_weights(5, x5)
    x6, *gu3 = _ffn_fwd(x5, ng(1, 2), w_in[3], w_out[3], "ffn_fwd_3")
    dy, sq = _loss_head(x6, target, "loss_head")

    grads = {"ffn_w_in": [None] * 4, "ffn_w_out": [None] * 4}
    dnorm = [[None] * 3 for _ in range(2)]

    def ffn_back(lf, k, xin, gvec, dy, gu, after):
        dx, dg, dgate, dup, a, h, dyb = _ffn_bwd(xin, gvec, dy, gu[0], gu[1], w_in[lf], w_out[lf], f"ffn_bwd_{lf}", after=after)
        d_in = _mm_tn(h, dgate, D, bf, (D, 2 * F), (D, bf), lambda mi, ni: (0, ni), 1.0, f"ffn_dw_gate_{lf}")
        grads["ffn_w_in"][lf] = _mm_tn(h, dup, D, bf, (D, 2 * F), (D, bf), lambda mi, ni: (0, F // bf + ni), 1.0,
                                       f"ffn_dw_up_{lf}", prev=d_in)
        grads["ffn_w_out"][lf] = _mm_tn(a, dyb, bf, D, (F, D), (bf, D), lambda mi, ni: (mi, 0), 1.0, f"ffn_dw_out_{lf}")
        return dx, dg, grads_ready(k, (grads["ffn_w_in"][lf], grads["ffn_w_out"][lf]), dx)

    dx, dnorm[1][2], tok = ffn_back(3, 5, x5, ng(1, 2), dy, gu3, ())
    dom, delta = _attn_out_bwd(dx, om, wo, "attn_out_bwd", after=tok)
    grads["attn_w_o"] = _mm_tn(om, dx, HE, bn, (HE, D), (HE, bn), lambda mi, ni: (0, ni), 1.0, "attn_dw_o")
    dqkv, dgq, dgk = [], [], []
    for gi, (window, dil) in enumerate(ATTN_GROUPS):
        ds = _attn_bwd(*qkv_s[gi], HE, streams(dom, dil, f"dom_streams_{gi}"), streams(lse, dil, f"lse_streams_{gi}"),
                       streams(delta, dil, f"delta_streams_{gi}"), window, dil, f"attn_bwd_{gi}")
        dq, dk, dv = [tokens(d, dil, f"d{nm}_tokens_{gi}") for d, nm in zip(ds, "qkv")]
        dq, dk, a_, b_ = _qk_norm_bwd(dq, dk, qkv, 3 * gi, HE, qn[gi][None, :], kn[gi][None, :], f"qk_norm_bwd_{gi}")
        dqkv += [dq, dk, dv]
        dgq.append(a_)
        dgk.append(b_)
    grads["attn_q_norm"] = jnp.concatenate(dgq, axis=0)
    grads["attn_k_norm"] = jnp.concatenate(dgk, axis=0)
    d_qkv = None
    for s9 in range(9):
        d_qkv = _mm_tn(ha, dqkv[s9], D, bh, (D, 9 * HE), (D, bh), lambda mi, ni, s9=s9: (0, s9 * (HE // bh) + ni), 1.0,
                       f"attn_dw_qkv_{s9}", prev=d_qkv)
    grads["attn_w_qkv"] = d_qkv
    dx, dnorm[1][1] = _attn_qkv_bwd(dqkv, x4, ng(1, 1), dx, wqkv, "attn_qkv_bwd")
    tok = grads_ready(4, (grads["attn_w_qkv"], grads["attn_w_o"]), dx)
    dx, dnorm[1][0], tok = ffn_back(2, 3, x3, ng(1, 0), dx, gu2, tok)

    dx, dnorm[0][2], tok = ffn_back(1, 2, x2, ng(0, 2), dx, gu1, tok)
    dc, grads["conv_norm_g"], grads["conv_b_pw2"] = _conv_post_bwd(dx, c, cng, pw2, "conv_post_bwd", after=tok)
    grads["conv_w_pw2"] = _mm_tn(s, dx, D, bn, (D, D), (D, bn), lambda mi, ni: (0, ni), 1.0, "conv_dw_pw2")
    du, grads["conv_w_dw"], grads["conv_b_dw"] = _dwconv_bwd(dc, u, w_dw, "dwconv_bwd")
    dx, dnorm[0][1], dag, grads["conv_b_pw1"] = _conv_pre_bwd(du, ag, x1, ng(0, 1), dx, pw1, "conv_pre_bwd")
    grads["conv_w_pw1"] = _mm_tn(hc, dag, D, 2 * bn, (D, 2 * D), (D, 2 * bn), lambda mi, ni: (0, ni), 1.0, "conv_dw_pw1")
    tok = grads_ready(1, (grads["conv_w_pw1"], grads["conv_w_pw2"]), dx)
    dx, dnorm[0][0], _ = ffn_back(0, 0, x, ng(0, 0), dx, gu0, tok)

    grads["norm_g"] = jnp.concatenate([jnp.concatenate(r, axis=0)[None] for r in dnorm], axis=0)
    return sq, dx, grads


class _Sharded:
    def __init__(self, name, full3, half_axis, shard_axis, src, slab=None):
        self.name, self.full3, self.half_axis, self.shard_axis = name, tuple(full3), half_axis, shard_axis
        self.src, self.slab = src, slab

    def source(self, refs):
        return refs[self.src] if self.slab is None else refs[self.src].at[self.slab]

    def _cut(self, shape, axis, parts):
        s = list(shape)
        s[axis] //= parts
        return tuple(s)

    @property
    def shard3(self):
        return self._cut(self.full3, self.shard_axis, N_CHIPS)

    @property
    def pair3(self):
        return self._cut(self.full3, self.half_axis, 2)

    @property
    def part3(self):
        return self._cut(self.shard3, self.half_axis, 2)

    @staticmethod
    def _slice(ref, axis, idx, parts):
        n = ref.shape[axis] // parts
        start = idx * n
        minor = len(ref.shape) - 1 - axis
        if minor < 2 and not isinstance(start, int):
            start = pl.multiple_of(start, 128 if minor == 0 else (16 if n % 16 == 0 else 8))
        sl = [slice(None)] * len(ref.shape)
        sl[axis] = pl.ds(start, n)
        return ref.at[tuple(sl)]

    def half(self, ref, h):
        return self._slice(ref, self.half_axis, h, 2)

    def shard(self, ref, j):
        return self._slice(ref, self.shard_axis, j, N_CHIPS)


def _place():
    x, y, c = lax.axis_index("x"), lax.axis_index("y"), lax.axis_index("c")
    return x, y, c, 2 * x + y


_RELS = (1, 2, 3)


def _peer(x, y, rel):
    px = 1 - x if rel & 2 else x
    py = 1 - y if rel & 1 else y
    return px, py, 2 * px + py


ANY = pl.BlockSpec(memory_space=pl.ANY)


def _comm_call(body, name, n_in, out_shape, n_sems, aliases=None):
    return pl.pallas_call(
        body, name=name, in_specs=[ANY] * n_in, out_specs=[ANY] * len(out_shape), out_shape=out_shape,
        scratch_shapes=[pltpu.SemaphoreType.DMA((n,)) for n in n_sems],
        input_output_aliases=aliases or {},
        compiler_params=pltpu.CompilerParams(has_side_effects=True),
    )


def _remote(src, dst, send_sem, recv_sem, dev):
    return pltpu.make_async_remote_copy(src_ref=src, dst_ref=dst, send_sem=send_sem, recv_sem=recv_sem, device_id=dev,
                                        device_id_type=MESH)


def _gather_small(small_shards):
    ns = len(small_shards)

    def body(*refs):
        ins, outs = refs[:ns], refs[ns:2 * ns]
        lsem, ssem, rsem = refs[2 * ns:]
        x, y, c, me = _place()
        cols = lambda ref, j: _Sharded._slice(ref, 1, j, N_CHIPS)
        local = [pltpu.make_async_copy(ins[i], cols(outs[i], me), lsem.at[i]) for i in range(ns)]
        sends = []
        for i in range(ns):
            for k, rel in enumerate(_RELS):
                px, py, _ = _peer(x, y, rel)
                sends.append(_remote(ins[i], cols(outs[i], me), ssem.at[3 * i + k], rsem.at[3 * i + k], (px, py, c)))
        for cp in local + sends:
            cp.start()
        for i in range(ns):
            for k, rel in enumerate(_RELS):
                _, _, pj = _peer(x, y, rel)
                got = cols(outs[i], pj)
                _remote(got, got, ssem.at[3 * i + k], rsem.at[3 * i + k], (x, y, c)).wait_recv()
        for cp in sends:
            cp.wait_send()
        for cp in local:
            cp.wait()

    out_shape = [_sds((s.shape[0], s.shape[1] * N_CHIPS), F32) for s in small_shards]
    return _comm_call(body, "gather_small", ns, out_shape, [ns, 3 * ns, 3 * ns])(*small_shards)


HBM = pl.BlockSpec(memory_space=pltpu.HBM)
SEM = pl.BlockSpec(memory_space=pltpu.SEMAPHORE)
DATAFLOW = pltpu.SideEffectType.DATAFLOW_SIDE_EFFECTING


def _in_hbm(a):
    return pltpu.with_memory_space_constraint(a, pltpu.HBM)


def _cast_place(it, shard, scal):
    a_n, r_n, c_n = it.shard3
    tr = _pick(r_n, 256, 16)
    sa = it.shard_axis

    def body(sc_ref, s_ref, o_ref):
        o_ref[...] = s_ref[...].astype(BF16)

    if it.slab is None:
        src = pl.BlockSpec((1, tr, c_n), lambda a, rb, sc: (a, rb, 0))
    else:
        src = pl.BlockSpec((None, 1, tr, c_n), lambda a, rb, sc: (it.slab, a, rb, 0))
    dst = pl.BlockSpec((1, tr, c_n), lambda a, rb, sc: (a + sc[1] * (a_n if sa == 0 else 0), rb + sc[1] * (r_n // tr if sa == 1 else 0),
                                                       sc[1] if sa == 2 else 0))
    return _call(body, f"cast_place_{it.name}", (a_n, r_n // tr), [src], [dst], [_sds(it.full3, BF16)], prefetch=1)(scal, shard)[0]


def _gather_start(items, fulls):
    ni = len(items)

    def body(*refs):
        outs = refs[ni:]
        ssem, rsem, full = outs[:ni], outs[ni:2 * ni], outs[2 * ni:3 * ni]
        x, y, c, me = _place()
        for i, it in enumerate(items):
            mine = it.half(it.shard(full[i], me), c)
            for k, rel in enumerate(_RELS):
                px, py, _ = _peer(x, y, rel)
                _remote(mine, mine, ssem[i].at[k], rsem[i].at[k], (px, py, c)).start()

    outs = pl.pallas_call(
        body, name="gather_start", in_specs=[HBM] * ni, out_specs=[SEM] * (2 * ni) + [HBM] * ni,
        out_shape=[pltpu.SemaphoreType.DMA((3,))] * (2 * ni) + [pltpu.HBM(it.full3, BF16) for it in items],
        input_output_aliases={j: 2 * ni + j for j in range(ni)},
        compiler_params=pltpu.CompilerParams(has_side_effects=DATAFLOW),
    )(*[_in_hbm(f) for f in fulls])
    return outs[:ni], outs[ni:2 * ni], outs[2 * ni:]


def _gather_forward(items, fulls, ssems, rsems, after, name):
    ni = len(items)

    def body(*refs):
        ssem, rsem = refs[ni:2 * ni], refs[2 * ni:3 * ni]
        outs = refs[3 * ni + 1:]
        full, fsem, gsem = outs[:ni], outs[ni:2 * ni], outs[2 * ni:3 * ni]
        x, y, c, me = _place()
        sib = (x, y, 1 - c)
        for i, it in enumerate(items):
            for k, rel in enumerate(_RELS):
                _, _, pj = _peer(x, y, rel)
                got = it.half(it.shard(full[i], pj), c)
                _remote(got, got, ssem[i].at[k], rsem[i].at[k], sib).wait_recv()
                _remote(got, got, fsem[i].at[k], gsem[i].at[k], sib).start()
        for i, it in enumerate(items):
            mine = it.half(it.shard(full[i], me), c)
            for k in range(3):
                _remote(mine, mine, ssem[i].at[k], rsem[i].at[k], sib).wait_send()

    outs = pl.pallas_call(
        body, name=name, in_specs=[HBM] * ni + [SEM] * (2 * ni) + [ANY],
        out_specs=[HBM] * ni + [SEM] * (2 * ni),
        out_shape=[pltpu.HBM(it.full3, BF16) for it in items] + [pltpu.SemaphoreType.DMA((3,))] * (2 * ni),
        input_output_aliases={i: i for i in range(ni)},
        compiler_params=pltpu.CompilerParams(has_side_effects=DATAFLOW),
    )(*fulls, *ssems, *rsems, after)
    return outs[:ni], outs[ni:2 * ni], outs[2 * ni:]


def _gather_finish(items, fulls, fsems, gsems, name):
    ni = len(items)

    def body(*refs):
        fsem, gsem = refs[ni:2 * ni], refs[2 * ni:3 * ni]
        full = refs[3 * ni:]
        x, y, c, _ = _place()
        sib = (x, y, 1 - c)
        for i, it in enumerate(items):
            for k, rel in enumerate(_RELS):
                _, _, pj = _peer(x, y, rel)
                got = it.half(it.shard(full[i], pj), 1 - c)
                _remote(got, got, fsem[i].at[k], gsem[i].at[k], sib).wait_recv()
                sent = it.half(it.shard(full[i], pj), c)
                _remote(sent, sent, fsem[i].at[k], gsem[i].at[k], sib).wait_send()

    return pl.pallas_call(
        body, name=name, in_specs=[HBM] * ni + [SEM] * (2 * ni), out_specs=[HBM] * ni,
        out_shape=[pltpu.HBM(it.full3, BF16) for it in items],
        input_output_aliases={i: i for i in range(ni)},
        compiler_params=pltpu.CompilerParams(has_side_effects=DATAFLOW),
    )(*fulls, *fsems, *gsems)


def _exchange_start(name, arrays, n_sems, copies):
    na, ns = len(arrays), len(n_sems)

    def body(*refs):
        outs = refs[na:]
        ssem, rsem, thru, token = outs[:ns], outs[ns:2 * ns], outs[2 * ns:2 * ns + na], outs[-1]
        for send, _ in copies(thru, ssem, rsem):
            send.start()
        token[...] = jnp.zeros_like(token)

    outs = pl.pallas_call(
        body, name=name, in_specs=[HBM] * na,
        out_specs=[SEM] * (2 * ns) + [HBM] * na + [pl.BlockSpec(memory_space=pltpu.VMEM)],
        out_shape=[pltpu.SemaphoreType.DMA((n,)) for n in n_sems] * 2 + [pltpu.HBM(a.shape, a.dtype) for a in arrays]
        + [_sds((8, 128), F32)],
        input_output_aliases={j: 2 * ns + j for j in range(na)},
        compiler_params=pltpu.CompilerParams(has_side_effects=DATAFLOW),
    )(*[_in_hbm(a) for a in arrays])
    return outs[:ns], outs[ns:2 * ns], outs[2 * ns:2 * ns + na], outs[-1]


def _exchange_wait(name, arrays, ssems, rsems, copies, after):
    na, ns = len(arrays), len(ssems)

    def body(*refs):
        ssem, rsem = refs[na:na + ns], refs[na + ns:na + 2 * ns]
        thru = refs[na + 2 * ns + 1:]
        for send, recv in copies(thru, ssem, rsem):
            recv.wait_recv()
            send.wait_send()

    return pl.pallas_call(
        body, name=name, in_specs=[HBM] * na + [SEM] * (2 * ns) + [ANY], out_specs=[HBM] * na,
        out_shape=[pltpu.HBM(a.shape, a.dtype) for a in arrays],
        input_output_aliases={i: i for i in range(na)},
        compiler_params=pltpu.CompilerParams(has_side_effects=DATAFLOW),
    )(*arrays, *ssems, *rsems, after)


def _pair_copies(items):
    n = len(items)

    def copies(a, ssem, rsem):
        x, y, c, _ = _place()
        cps = [_remote(it.half(a[i], 1 - c), a[n + i], ssem[i].at[0], rsem[i].at[0], (x, y, 1 - c)) for i, it in enumerate(items)]
        return [(cp, cp) for cp in cps]

    return copies


def _chip_copies(items):
    n = len(items)

    def copies(a, ssem, rsem):
        x, y, c, _ = _place()
        cps = []
        for i, it in enumerate(items):
            for k, rel in enumerate(_RELS):
                px, py, pj = _peer(x, y, rel)
                cps.append(_remote(it.shard(a[i], pj), a[n + i].at[k], ssem[i].at[k], rsem[i].at[k], (px, py, c)))
        return [(cp, cp) for cp in cps]

    return copies


def _fill_copies(items):
    def copies(a, ssem, rsem):
        x, y, c, _ = _place()
        sib = (x, y, 1 - c)
        out = []
        for i, it in enumerate(items):
            mine, other = it.half(a[i], c), it.half(a[i], 1 - c)
            out.append((_remote(mine, mine, ssem[i].at[0], rsem[i].at[0], sib), _remote(other, other, ssem[i].at[0], rsem[i].at[0], sib)))
        return out

    return copies


def _ew_tiles(d):
    _, rows, cols = d.part3
    return _pick(rows, 256, 16), cols


def _pair_add(d, g_full, got, scal):
    tr, tc = _ew_tiles(d)
    a_n, r_n, c_n = d.pair3
    ha = d.half_axis

    def body(sc_ref, g_ref, r_ref, o_ref, ob_ref):
        s = g_ref[...] + r_ref[...].astype(F32)
        o_ref[...] = s
        ob_ref[...] = s.astype(BF16)

    blk = (1, tr, tc)
    same = pl.BlockSpec(blk, lambda a, rb, cb, sc: (a, rb, cb))
    mine = pl.BlockSpec(blk, lambda a, rb, cb, sc: (a + sc[0] * (a_n if ha == 0 else 0), rb + sc[0] * (r_n // tr if ha == 1 else 0), cb))
    return _call(body, f"pair_add_{d.name}", (a_n, r_n // tr, c_n // tc), [mine, same], [same, same],
                 [_sds(d.pair3, F32), _sds(d.pair3, BF16)], prefetch=1)(scal, g_full, got)


def _chip_reduce(d, pair_f32, got, scal):
    tr, tc = _ew_tiles(d)
    a_n, r_n, c_n = d.part3
    ha, sa = d.half_axis, d.shard_axis

    def body(sc_ref, p_ref, r0, r1, r2, o_ref):
        o_ref[...] = ((p_ref[...] + r0[...].astype(F32)) + r1[...].astype(F32)) + r2[...].astype(F32)

    blk = (1, tr, tc)
    own = pl.BlockSpec(blk, lambda a, rb, sc: (a + sc[1] * (a_n if sa == 0 else 0), rb + sc[1] * (r_n // tr if sa == 1 else 0),
                                               sc[1] if sa == 2 else 0))
    slot = lambda k: pl.BlockSpec((None,) + blk, lambda a, rb, sc: (k, a, rb, 0))
    out = pl.BlockSpec(blk, lambda a, rb, sc: (a + sc[0] * (a_n if ha == 0 else 0), rb + sc[0] * (r_n // tr if ha == 1 else 0), 0))
    return _call(body, f"chip_reduce_{d.name}", (a_n, r_n // tr), [own, slot(0), slot(1), slot(2)], [out],
                 [_sds(d.shard3, F32)], prefetch=1)(scal, pair_f32, got, got, got)[0]


def _adam_math(g, w, m, v):
    m = ADAM_B1 * m + (1.0 - ADAM_B1) * g
    v = ADAM_B2 * v + (1.0 - ADAM_B2) * (g * g)
    m_hat = m / (1.0 - ADAM_B1 ** ADAM_STEP)
    v_hat = v / (1.0 - ADAM_B2 ** ADAM_STEP)
    delta = -ADAM_LR * (m_hat / (jnp.sqrt(v_hat) + ADAM_EPS) + ADAM_WD * w)
    return delta, m, v


def _adam(d, g, w, m, v, prev=None):
    tr, tc = _ew_tiles(d)
    a_n, r_n, c_n = d.shard3
    n_prev = 0 if prev is None else 4

    def body(g_ref, w_ref, m_ref, v_ref, *rest):
        go_ref, d_ref, mo_ref, vo_ref = rest[n_prev:]
        gv = g_ref[...]
        go_ref[...] = gv
        d_ref[...], mo_ref[...], vo_ref[...] = _adam_math(gv, w_ref[...], m_ref[...], v_ref[...])

    plain = pl.BlockSpec((1, tr, tc), lambda a, rb: (a, rb, 0))
    if d.slab is None:
        wspec, shape = plain, d.shard3
    else:
        wspec, shape = pl.BlockSpec((None, 1, tr, tc), lambda a, rb: (d.slab, a, rb, 0)), (4,) + d.shard3
    in_specs = [plain] + [wspec] * 3
    args = [g, w, m, v]
    aliases = None
    if prev is not None:
        in_specs += [pl.BlockSpec(memory_space=pl.ANY)] * 4
        args += list(prev)
        aliases = {4 + k: k for k in range(4)}
    return _call(body, f"adam_{d.name}", (a_n, r_n // tr), in_specs, [wspec] * 4, [_sds(shape, F32)] * 4, aliases=aliases)(*args)


def _adam_small(gs, ws, ms, vs):
    n = len(gs)

    def body(*refs):
        for i in range(n):
            g, w, m, v = (refs[k * n + i][...] for k in range(4))
            d, mo, vo = _adam_math(g, w, m, v)
            refs[4 * n + i][...] = d
            refs[5 * n + i][...] = mo
            refs[6 * n + i][...] = vo

    vm = pl.BlockSpec(memory_space=pltpu.VMEM)
    outs = pl.pallas_call(body, name="adam_small", in_specs=[vm] * (4 * n), out_specs=[vm] * (3 * n),
                          out_shape=[_sds(g.shape, F32) for g in gs] * 3)(*gs, *ws, *ms, *vs)
    return outs[:n], outs[n:2 * n], outs[2 * n:]


def _allreduce_small(packed):
    rows, cols = packed.shape
    others = [(dx, dy, dc) for dx in (0, 1) for dy in (0, 1) for dc in (0, 1) if (dx, dy, dc) != (0, 0, 0)]

    def body(in_ref, out_ref, buf, ssem, rsem):
        x, y, c, _ = _place()
        lin = 4 * x + 2 * y + c
        buf[lin] = in_ref[...]
        cps = []
        for k, (dx, dy, dc) in enumerate(others):
            px = 1 - x if dx else x
            py = 1 - y if dy else y
            pc = 1 - c if dc else c
            cps.append((pltpu.make_async_remote_copy(src_ref=in_ref, dst_ref=buf.at[lin], send_sem=ssem.at[k], recv_sem=rsem.at[k],
                                                     device_id=(px, py, pc), device_id_type=MESH), 4 * px + 2 * py + pc))
        for cp, _ in cps:
            cp.start()
        for k, (cp, plin) in enumerate(cps):
            pltpu.make_async_remote_copy(src_ref=in_ref, dst_ref=buf.at[plin], send_sem=ssem.at[k], recv_sem=rsem.at[k],
                                         device_id=(x, y, c), device_id_type=MESH).wait_recv()
        for cp, _ in cps:
            cp.wait_send()
        acc = buf[0]
        for dev in range(1, 8):
            acc = acc + buf[dev]
        out_ref[...] = acc

    vm = pl.BlockSpec(memory_space=pltpu.VMEM)
    return pl.pallas_call(
        body, name="allreduce_small", in_specs=[vm], out_specs=vm, out_shape=_sds((rows, cols), F32),
        scratch_shapes=[pltpu.VMEM((8, rows, cols), F32), pltpu.SemaphoreType.DMA((7,)), pltpu.SemaphoreType.DMA((7,))],
        compiler_params=pltpu.CompilerParams(has_side_effects=True),
    )(packed)


_BIG = ("ffn_w_in", "ffn_w_out", "conv_w_pw1", "conv_w_pw2", "attn_w_qkv", "attn_w_o")
_SMALL = ("norm_g", "conv_b_pw1", "conv_w_dw", "conv_b_dw", "conv_norm_g", "conv_b_pw2", "attn_q_norm", "attn_k_norm")
_NAMES = ("norm_g", "ffn_w_in", "ffn_w_out", "conv_w_pw1", "conv_b_pw1", "conv_w_dw", "conv_b_dw", "conv_norm_g", "conv_w_pw2",
          "conv_b_pw2", "attn_w_qkv", "attn_q_norm", "attn_k_norm", "attn_w_o")


def _rows8(a, width):
    a = a.reshape(-1, min(a.shape[-1], width))
    return jnp.pad(a, ((0, -a.shape[0] % 8), (0, width - a.shape[1])))


def kernel(x, norm_g, ffn_w_in, ffn_w_out, conv_w_pw1, conv_b_pw1, conv_w_dw, conv_b_dw, conv_norm_g, conv_w_pw2, conv_b_pw2, attn_w_qkv, attn_q_norm, attn_k_norm, attn_w_o, loss_target, m_norm_g, m_ffn_w_in, m_ffn_w_out, m_conv_w_pw1, m_conv_b_pw1, m_conv_w_dw, m_conv_b_dw, m_conv_norm_g, m_conv_w_pw2, m_conv_b_pw2, m_attn_w_qkv, m_attn_q_norm, m_attn_k_norm, m_attn_w_o, v_norm_g, v_ffn_w_in, v_ffn_w_out, v_conv_w_pw1, v_conv_b_pw1, v_conv_w_dw, v_conv_b_dw, v_conv_norm_g, v_conv_w_pw2, v_conv_b_pw2, v_attn_w_qkv, v_attn_q_norm, v_attn_k_norm, v_attn_w_o):
    w = dict(zip(_NAMES, (norm_g, ffn_w_in, ffn_w_out, conv_w_pw1, conv_b_pw1, conv_w_dw, conv_b_dw, conv_norm_g, conv_w_pw2,
                          conv_b_pw2, attn_w_qkv, attn_q_norm, attn_k_norm, attn_w_o)))
    m = dict(zip(_NAMES, (m_norm_g, m_ffn_w_in, m_ffn_w_out, m_conv_w_pw1, m_conv_b_pw1, m_conv_w_dw, m_conv_b_dw, m_conv_norm_g,
                          m_conv_w_pw2, m_conv_b_pw2, m_attn_w_qkv, m_attn_q_norm, m_attn_k_norm, m_attn_w_o)))
    v = dict(zip(_NAMES, (v_norm_g, v_ffn_w_in, v_ffn_w_out, v_conv_w_pw1, v_conv_b_pw1, v_conv_w_dw, v_conv_b_dw, v_conv_norm_g,
                          v_conv_w_pw2, v_conv_b_pw2, v_attn_w_qkv, v_attn_q_norm, v_attn_k_norm, v_attn_w_o)))
    T, D = x.shape[1:]
    F = ffn_w_out.shape[2] * N_CHIPS
    HE = attn_w_o.shape[1] * N_CHIPS
    cx, cy, cc = lax.axis_index("x"), lax.axis_index("y"), lax.axis_index("c")
    me = 2 * cx + cy
    scal = jnp.stack([cc, me]).astype(jnp.int32)

    ffn_in = lambda lf: _Sharded(f"ffn_w_in_{lf}", (2, D // 2, 2 * F), 0, 2, "ffn_w_in", lf)
    ffn_out = lambda lf: _Sharded(f"ffn_w_out_{lf}", (4, F // 4, D), 1, 0, "ffn_w_out", lf)
    items = [
        ffn_in(0), ffn_out(0),
        _Sharded("conv_w_pw1", (2, D // 2, 2 * D), 0, 2, "conv_w_pw1"), _Sharded("conv_w_pw2", (4, D // 4, D), 1, 0, "conv_w_pw2"),
        ffn_in(1), ffn_out(1), ffn_in(2), ffn_out(2),
        _Sharded("attn_w_qkv", (2, D // 2, 9 * HE), 0, 2, "attn_w_qkv"), _Sharded("attn_w_o", (4, HE // 4, D), 1, 0, "attn_w_o"),
        ffn_in(3), ffn_out(3),
    ]
    mat_shapes = {"ffn_w_in": (D, 2 * F), "ffn_w_out": (F, D), "conv_w_pw1": (D, 2 * D), "conv_w_pw2": (D, D),
                  "attn_w_qkv": (D, 9 * HE), "attn_w_o": (HE, D)}

    def as_shards(a, n):
        it = next(i for i in items if i.src == n)
        return a.reshape(((4,) if it.slab is not None else ()) + it.shard3)

    norm_full, dw_full = _gather_small([norm_g.reshape(6, D // 4), conv_w_dw.reshape(CONV_WIDTH, D // 4)])
    ssems, rsems, fulls = _gather_start(items, [_cast_place(it, as_shards(w[it.src], it.src), scal) for it in items])

    def block_weights(k, after):
        sel = slice(2 * k, 2 * k + 2)
        got, fsems, gsems = _gather_forward(items[sel], fulls[sel], ssems[sel], rsems[sel], after, f"gather_forward_{k}")
        done = _gather_finish(items[sel], got, fsems, gsems, f"gather_finish_{k}")
        return [a.reshape(mat_shapes[it.src]) for a, it in zip(done, items[sel])]

    res = {}
    flight = []

    def advance(entry, k, dx):
        stage, its, st = entry
        n = len(its)
        if stage == 1:
            ssem, rsem, arrs, g32 = st
            got = _exchange_wait(f"grads_pair_wait_{k}", arrs, ssem, rsem, _pair_copies(its), dx)[n:]
            sums = [_pair_add(it, g, r, scal) for it, g, r in zip(its, g32, got)]
            land = [lax.empty((3,) + it.part3, BF16) for it in its]
            ssem, rsem, arrs, tok = _exchange_start(f"grads_chip_start_{k}", [p[1] for p in sums] + land, [3] * n, _chip_copies(its))
            return (2, its, (ssem, rsem, arrs, [p[0] for p in sums])), tok
        if stage == 2:
            ssem, rsem, arrs, p32 = st
            got = _exchange_wait(f"grads_chip_wait_{k}", arrs, ssem, rsem, _chip_copies(its), dx)[n:]
            red = [_chip_reduce(it, p, r, scal) for it, p, r in zip(its, p32, got)]
            ssem, rsem, arrs, tok = _exchange_start(f"grads_fill_start_{k}", red, [1] * n, _fill_copies(its))
            return (3, its, (ssem, rsem, arrs)), tok
        ssem, rsem, arrs = st
        for it, g in zip(its, _exchange_wait(f"grads_fill_wait_{k}", arrs, ssem, rsem, _fill_copies(its), dx)):
            nm = it.src
            res[nm] = _adam(it, g, as_shards(w[nm], nm), as_shards(m[nm], nm), as_shards(v[nm], nm), prev=res.get(nm))
        return None, None

    def step_flight(dx):
        toks, left = [], []
        for k, entry in flight:
            entry, tok = advance(entry, k, dx)
            if entry is not None:
                left.append((k, entry))
                toks.append(tok)
        flight[:] = left
        return toks

    def grads_ready(k, pairs, dx):
        its = items[2 * k:2 * k + 2]
        toks = step_flight(dx)
        g32 = [p[0].reshape(it.full3) for p, it in zip(pairs, its)]
        g16 = [p[1].reshape(it.full3) for p, it in zip(pairs, its)]
        land = [lax.empty(it.pair3, BF16) for it in its]
        ssem, rsem, arrs, tok = _exchange_start(f"grads_pair_start_{k}", g16 + land, [1] * len(its), _pair_copies(its))
        flight.append((k, (1, its, (ssem, rsem, arrs, g32))))
        return toks + [tok]

    sq, dx, grads = _local_step(x[0], loss_target[0], norm_full.reshape(2, 3, D), conv_b_pw1, dw_full, conv_b_dw, conv_norm_g,
                                conv_b_pw2, attn_q_norm[0], attn_k_norm[0], block_weights, grads_ready)
    loss = lax.psum(0.5 * jnp.sum(sq) / D, ("x", "y", "c"))
    while flight:
        step_flight(dx)

    out_g, out_d, out_m, out_v = {}, {}, {}, {}
    for n in _BIG:
        out_g[n], out_d[n], out_m[n], out_v[n] = (a.reshape(w[n].shape) for a in res[n])

    parts = [_rows8(grads[n], D) for n in _SMALL]
    tot = _allreduce_small(jnp.concatenate(parts, axis=0))
    sg, r0 = {}, 0
    for n, p in zip(_SMALL, parts):
        last = grads[n].shape[-1]
        g = tot[r0:r0 + grads[n].size // min(last, D), :min(last, D)].reshape(-1, last)
        r0 += p.shape[0]
        if n in ("norm_g", "conv_w_dw"):
            g = lax.dynamic_slice_in_dim(g, me * (D // 4), D // 4, axis=1)
        sg[n] = g
    flat = lambda a: a.reshape(-1, a.shape[-1])
    ds, ms, vs = _adam_small([sg[n] for n in _SMALL], [flat(w[n]) for n in _SMALL], [flat(m[n]) for n in _SMALL],
                             [flat(v[n]) for n in _SMALL])
    for i, n in enumerate(_SMALL):
        out_g[n], out_d[n], out_m[n], out_v[n] = (a.reshape(w[n].shape) for a in (sg[n], ds[i], ms[i], vs[i]))

    return (loss, dx[None], *[out_g[n] for n in _NAMES], *[out_d[n] for n in _NAMES], *[out_m[n] for n in _NAMES],
            *[out_v[n] for n in _NAMES])
```

```python
import functools

import jax
import jax.numpy as jnp
from jax import lax
from jax.experimental import pallas as pl
from jax.experimental.pallas import tpu as pltpu

F32 = jnp.float32
BF16 = jnp.bfloat16
MESH = pl.DeviceIdType.MESH

NORM_EPS = 1e-6
CONV_WIDTH = 31
ATTN_GROUPS = ((128, 1), (512, 4), (2048, 16))
ATTN_BLOCK = 128
HEAD_DIM = 128
N_CHIPS = 4

ADAM_LR = 0.001
ADAM_B1 = 0.9
ADAM_B2 = 0.999
ADAM_EPS = 1e-08
ADAM_WD = 0.01
ADAM_STEP = 10

VMEM_LIMIT = 56 * 1024 * 1024
NT_DIMS = (((1,), (1,)), ((), ()))
TN_DIMS = (((0,), (0,)), ((), ()))


def _pick(n, pref, mult):
    t = (min(n, pref) // mult) * mult
    while t >= mult:
        if n % t == 0:
            return t
        t -= mult
    return n


def _call(body, name, grid, in_specs, out_specs, out_shape, scratch=(), aliases=None, prefetch=0, after=()):
    params = pltpu.CompilerParams(dimension_semantics=("arbitrary",) * len(grid), vmem_limit_bytes=VMEM_LIMIT)
    after = tuple(after)
    if after:
        inner, n_in = body, prefetch + len(in_specs)

        def body(*refs):
            return inner(*refs[:n_in], *refs[n_in + len(after):])

        in_specs = list(in_specs) + [pl.BlockSpec(memory_space=pl.ANY)] * len(after)
    if prefetch:
        spec = pltpu.PrefetchScalarGridSpec(
            num_scalar_prefetch=prefetch, grid=grid, in_specs=in_specs, out_specs=out_specs, scratch_shapes=list(scratch)
        )
        call = pl.pallas_call(body, name=name, grid_spec=spec, out_shape=out_shape, compiler_params=params,
                              input_output_aliases=aliases or {})
    else:
        call = pl.pallas_call(body, name=name, grid=grid, in_specs=in_specs, out_specs=out_specs, out_shape=out_shape,
                              scratch_shapes=list(scratch), compiler_params=params, input_output_aliases=aliases or {})
    return lambda *args: call(*args, *after)


def _sds(shape, dtype):
    return jax.ShapeDtypeStruct(shape, dtype)


def _sig(x):
    return 1.0 / (1.0 + jnp.exp(-x))


def _rstd(x):
    return lax.rsqrt(jnp.mean(x * x, axis=-1, keepdims=True) + NORM_EPS)


def _norm_bwd(dy, xhat, r, g):
    dxh = dy * g
    return r * (dxh - xhat * jnp.mean(dxh * xhat, axis=-1, keepdims=True))


def _dot(a, b):
    return jnp.dot(a, b, preferred_element_type=F32)


def _dot_nt(a, b):
    return lax.dot_general(a, b, NT_DIMS, preferred_element_type=F32)


def _dot_tn(a, b):
    return lax.dot_general(a, b, TN_DIMS, preferred_element_type=F32)


def _ffn_tile(F):
    return _pick(F, 1408, 128)


def _ffn_fwd(x, g, w_in, w_out, name):
    T, D = x.shape
    F = w_out.shape[0]
    tf = _ffn_tile(F)
    nf = F // tf
    tm = _pick(T, 512, 8)

    def body(x_ref, g_ref, wg_ref, wu_ref, wo_ref, xo_ref, gate_ref, up_ref, h_sc, acc_sc):
        j = pl.program_id(1)

        @pl.when(j == 0)
        def _():
            xv = x_ref[...]
            h_sc[...] = (xv * _rstd(xv) * g_ref[...]).astype(BF16)
            acc_sc[...] = jnp.zeros_like(acc_sc)

        h = h_sc[...]
        gate = _dot(h, wg_ref[...])
        up = _dot(h, wu_ref[...])
        gate_ref[...] = gate.astype(BF16)
        up_ref[...] = up.astype(BF16)
        a = (gate * _sig(gate) * up).astype(BF16)
        acc_sc[...] += _dot(a, wo_ref[...])

        @pl.when(j == nf - 1)
        def _():
            xo_ref[...] = x_ref[...] + 0.5 * acc_sc[...]

    return _call(
        body, name, (T // tm, nf),
        [pl.BlockSpec((tm, D), lambda i, j: (i, 0)),
         pl.BlockSpec((1, D), lambda i, j: (0, 0)),
         pl.BlockSpec((D, tf), lambda i, j: (0, j)),
         pl.BlockSpec((D, tf), lambda i, j: (0, nf + j)),
         pl.BlockSpec((tf, D), lambda i, j: (j, 0))],
        [pl.BlockSpec((tm, D), lambda i, j: (i, 0)),
         pl.BlockSpec((tm, tf), lambda i, j: (i, j)),
         pl.BlockSpec((tm, tf), lambda i, j: (i, j))],
        [_sds((T, D), F32), _sds((T, F), BF16), _sds((T, F), BF16)],
        scratch=[pltpu.VMEM((tm, D), BF16), pltpu.VMEM((tm, D), F32)],
    )(x, g, w_in, w_in, w_out)


def _ffn_bwd(x, g, dy, gate, up, w_in, w_out, name, after=()):
    T, D = x.shape
    F = w_out.shape[0]
    tf = _ffn_tile(F)
    nf = F // tf
    tm = _pick(T, 512, 8)

    def body_a(dy_ref, gate_ref, up_ref, wo_ref, dgate_ref, dup_ref, a_ref):
        dyb = (0.5 * dy_ref[...]).astype(BF16)
        gate = gate_ref[...].astype(F32)
        up = up_ref[...].astype(F32)
        sg = _sig(gate)
        sl = gate * sg
        a_ref[...] = (sl * up).astype(BF16)
        da = _dot_nt(dyb, wo_ref[...])
        dgate_ref[...] = (da * up * (sg * (1.0 + gate * (1.0 - sg)))).astype(BF16)
        dup_ref[...] = (da * sl).astype(BF16)

    tile = pl.BlockSpec((tm, tf), lambda j, i: (i, j))
    rows = pl.BlockSpec((tm, D), lambda j, i: (i, 0))
    dgate, dup, a = _call(
        body_a, name + "_hidden", (nf, T // tm),
        [rows, tile, tile, pl.BlockSpec((tf, D), lambda j, i: (j, 0))],
        [tile, tile, tile],
        [_sds((T, F), BF16)] * 3, after=after,
    )(dy, gate, up, w_out)

    def body_b(x_ref, g_ref, dy_ref, dgate_ref, dup_ref, wg_ref, wu_ref, dx_ref, dg_ref, h_ref, dyb_ref):
        @pl.when(pl.program_id(0) == 0)
        def _():
            dg_ref[...] = jnp.zeros_like(dg_ref)

        xv = x_ref[...]
        r = _rstd(xv)
        xh = xv * r
        h_ref[...] = (xh * g_ref[...]).astype(BF16)
        dyb_ref[...] = (0.5 * dy_ref[...]).astype(BF16)
        dh = _dot_nt(dgate_ref[...], wg_ref[...]) + _dot_nt(dup_ref[...], wu_ref[...])
        dx_ref[...] = dy_ref[...] + _norm_bwd(dh, xh, r, g_ref[...])
        dg_ref[...] += jnp.sum(dh * xh, axis=0, keepdims=True)

    tb = _pick(T, 512, 8)
    row = lambda i: (i, 0)
    held = lambda k: pl.BlockSpec((D, F), lambda i: (0, k), pipeline_mode=pl.Buffered(1))
    dx, dg, h, dyb = _call(
        body_b, name, (T // tb,),
        [pl.BlockSpec((tb, D), row), pl.BlockSpec((1, D), lambda i: (0, 0)), pl.BlockSpec((tb, D), row),
         pl.BlockSpec((tb, F), row), pl.BlockSpec((tb, F), row), held(0), held(1)],
        [pl.BlockSpec((tb, D), row), pl.BlockSpec((1, D), lambda i: (0, 0)), pl.BlockSpec((tb, D), row), pl.BlockSpec((tb, D), row)],
        [_sds((T, D), F32), _sds((1, D), F32), _sds((T, D), BF16), _sds((T, D), BF16)],
    )(x, g, dy, dgate, dup, w_in, w_in)
    return dx, dg, dgate, dup, a, h, dyb


def _mm_tn(a, b, bm, bn, out_shape, out_block, out_map, scale, name, prev=None):
    K, M = a.shape
    N = b.shape[1]

    def body(a_ref, b_ref, *rest):
        o_ref, ob_ref = rest[-2:]
        o = _dot_tn(a_ref[...].astype(BF16), b_ref[...].astype(BF16)) * scale
        o_ref[...] = o
        ob_ref[...] = o.astype(BF16)

    in_specs = [pl.BlockSpec((K, bm), lambda mi, ni: (0, mi)), pl.BlockSpec((K, bn), lambda mi, ni: (0, ni))]
    args = [a, b]
    aliases = None
    if prev is not None:
        in_specs += [pl.BlockSpec(memory_space=pl.ANY)] * 2
        args += list(prev)
        aliases = {2: 0, 3: 1}
    ospec = pl.BlockSpec(out_block, out_map)
    return _call(body, name, (M // bm, N // bn), in_specs, [ospec, ospec],
                 [_sds(out_shape, F32), _sds(out_shape, BF16)], aliases=aliases)(*args)


def _conv_pre(x, g, w1, b1, name):
    T, D = x.shape
    tm = _pick(T, 512, 8)

    def body(x_ref, g_ref, w_ref, b_ref, ag_ref, u_ref, h_ref):
        xv = x_ref[...]
        h = (xv * _rstd(xv) * g_ref[...]).astype(BF16)
        h_ref[...] = h
        ag = _dot(h, w_ref[...]) + b_ref[...]
        ag_ref[...] = ag.astype(BF16)
        u_ref[...] = ag[:, :D] * _sig(ag[:, D:])

    return _call(
        body, name, (T // tm,),
        [pl.BlockSpec((tm, D), lambda i: (i, 0)), pl.BlockSpec((1, D), lambda i: (0, 0)),
         pl.BlockSpec((D, 2 * D), lambda i: (0, 0)), pl.BlockSpec((1, 2 * D), lambda i: (0, 0))],
        [pl.BlockSpec((tm, 2 * D), lambda i: (i, 0)), pl.BlockSpec((tm, D), lambda i: (i, 0)),
         pl.BlockSpec((tm, D), lambda i: (i, 0))],
        [_sds((T, 2 * D), BF16), _sds((T, D), F32), _sds((T, D), BF16)],
    )(x, g, w1, b1)


_DW_PAD = 32
_DW_CHUNK = 256


def _dwconv(u, w, b, name):
    T, D = u.shape
    K = w.shape[0]
    ch = _pick(T, _DW_CHUNK, 8)
    lead = _DW_PAD - (K - 1)

    def body(u_ref, w_ref, b_ref, c_ref, ext):
        ext[pl.ds(0, _DW_PAD), :] = jnp.zeros((_DW_PAD, 128), F32)
        ext[pl.ds(_DW_PAD, T), :] = u_ref[...]
        for c0 in range(0, T, ch):
            acc = jnp.zeros((ch, 128), F32) + b_ref[...]
            for k in range(K):
                acc = acc + w_ref[pl.ds(k, 1), :] * ext[pl.ds(c0 + lead + k, ch), :]
            c_ref[pl.ds(c0, ch), :] = acc

    return _call(
        body, name, (D // 128,),
        [pl.BlockSpec((T, 128), lambda i: (0, i)), pl.BlockSpec((K, 128), lambda i: (0, i)),
         pl.BlockSpec((1, 128), lambda i: (0, i))],
        [pl.BlockSpec((T, 128), lambda i: (0, i))],
        [_sds((T, D), F32)],
        scratch=[pltpu.VMEM((T + _DW_PAD, 128), F32)],
    )(u, w, b)[0]


def _dwconv_bwd(dc, u, w, name):
    T, D = u.shape
    K = w.shape[0]
    ch = _pick(T, _DW_CHUNK, 8)
    lead = _DW_PAD - (K - 1)

    def body(dc_ref, u_ref, w_ref, du_ref, dw_ref, db_ref, uext, dext):
        uext[pl.ds(0, _DW_PAD), :] = jnp.zeros((_DW_PAD, 128), F32)
        uext[pl.ds(_DW_PAD, T), :] = u_ref[...]
        dext[pl.ds(0, T), :] = dc_ref[...]
        dext[pl.ds(T, _DW_PAD), :] = jnp.zeros((_DW_PAD, 128), F32)
        dws =[jnp.zeros((8, 128), F32) for _ in range(K)]
        dbs = jnp.zeros((8, 128), F32)
        for c0 in range(0, T, ch):
            dcv = dext[pl.ds(c0, ch), :]
            dbs = dbs + jnp.sum(dcv.reshape(ch // 8, 8, 128), axis=0)
            acc = jnp.zeros((ch, 128), F32)
            for k in range(K):
                acc = acc + w_ref[pl.ds(k, 1), :] * dext[pl.ds(c0 + (K - 1) - k, ch), :]
                prod = dcv * uext[pl.ds(c0 + lead + k, ch), :]
                dws[k] = dws[k] + jnp.sum(prod.reshape(ch // 8, 8, 128), axis=0)
            du_ref[pl.ds(c0, ch), :] = acc
        for k in range(K):
            dw_ref[pl.ds(k, 1), :] = jnp.sum(dws[k], axis=0, keepdims=True)
        db_ref[...] = jnp.sum(dbs, axis=0, keepdims=True)

    return _call(
        body, name, (D // 128,),
        [pl.BlockSpec((T, 128), lambda i: (0, i)), pl.BlockSpec((T, 128), lambda i: (0, i)),
         pl.BlockSpec((K, 128), lambda i: (0, i))],
        [pl.BlockSpec((T, 128), lambda i: (0, i)), pl.BlockSpec((K, 128), lambda i: (0, i)),
         pl.BlockSpec((1, 128), lambda i: (0, i))],
        [_sds((T, D), F32), _sds((K, D), F32), _sds((1, D), F32)],
        scratch=[pltpu.VMEM((T + _DW_PAD, 128), F32), pltpu.VMEM((T + _DW_PAD, 128), F32)],
    )(dc, u, w)


def _conv_post(c, x, ng, w2, b2, name):
    T, D = x.shape
    tm = _pick(T, 512, 8)

    def body(c_ref, x_ref, ng_ref, w_ref, b_ref, xo_ref, s_ref):
        cv = c_ref[...]
        n = cv * _rstd(cv) * ng_ref[...]
        s = (n * _sig(n)).astype(BF16)
        s_ref[...] = s
        xo_ref[...] = x_ref[...] + _dot(s, w_ref[...]) + b_ref[...]

    row = lambda i: (i, 0)
    fix = lambda i: (0, 0)
    return _call(
        body, name, (T // tm,),
        [pl.BlockSpec((tm, D), row), pl.BlockSpec((tm, D), row), pl.BlockSpec((1, D), fix),
         pl.BlockSpec((D, D), fix), pl.BlockSpec((1, D), fix)],
        [pl.BlockSpec((tm, D), row), pl.BlockSpec((tm, D), row)],
        [_sds((T, D), F32), _sds((T, D), BF16)],
    )(c, x, ng, w2, b2)


def _conv_post_bwd(dy, c, ng, w2, name, after=()):
    T, D = dy.shape
    tm = _pick(T, 512, 8)

    def body(dy_ref, c_ref, ng_ref, w_ref, dc_ref, dng_ref, db_ref):
        @pl.when(pl.program_id(0) == 0)
        def _():
            dng_ref[...] = jnp.zeros_like(dng_ref)
            db_ref[...] = jnp.zeros_like(db_ref)

        dyv = dy_ref[...]
        ds = _dot_nt(dyv.astype(BF16), w_ref[...])
        cv = c_ref[...]
        r = _rstd(cv)
        ch = cv * r
        n = ch * ng_ref[...]
        sg = _sig(n)
        dn = ds * (sg * (1.0 + n * (1.0 - sg)))
        dc_ref[...] = _norm_bwd(dn, ch, r, ng_ref[...])
        dng_ref[...] += jnp.sum(dn * ch, axis=0, keepdims=True)
        db_ref[...] += jnp.sum(dyv, axis=0, keepdims=True)

    row = lambda i: (i, 0)
    fix = lambda i: (0, 0)
    return _call(
        body, name, (T // tm,),
        [pl.BlockSpec((tm, D), row), pl.BlockSpec((tm, D), row), pl.BlockSpec((1, D), fix), pl.BlockSpec((D, D), fix)],
        [pl.BlockSpec((tm, D), row), pl.BlockSpec((1, D), fix), pl.BlockSpec((1, D), fix)],
        [_sds((T, D), F32), _sds((1, D), F32), _sds((1, D), F32)], after=after,
    )(dy, c, ng, w2)


def _conv_pre_bwd(du, ag, x, g, dy, w1, name):
    T, D = x.shape
    tm = _pick(T, 512, 8)

    def body(du_ref, ag_ref, x_ref, g_ref, dy_ref, w_ref, dx_ref, dg_ref, dag_ref, db_ref):
        @pl.when(pl.program_id(0) == 0)
        def _():
            dg_ref[...] = jnp.zeros_like(dg_ref)
            db_ref[...] = jnp.zeros_like(db_ref)

        duv = du_ref[...]
        a = ag_ref[:, :D].astype(F32)
        gt = ag_ref[:, D:].astype(F32)
        sg = _sig(gt)
        da = duv * sg
        dgt = duv * a * sg * (1.0 - sg)
        db_ref[:, :D] += jnp.sum(da, axis=0, keepdims=True)
        db_ref[:, D:] += jnp.sum(dgt, axis=0, keepdims=True)
        dab = da.astype(BF16)
        dgb = dgt.astype(BF16)
        dag_ref[:, :D] = dab
        dag_ref[:, D:] = dgb
        dh = _dot_nt(dab, w_ref[:, :D]) + _dot_nt(dgb, w_ref[:, D:])
        xv = x_ref[...]
        r = _rstd(xv)
        xh = xv * r
        dx_ref[...] = dy_ref[...] + _norm_bwd(dh, xh, r, g_ref[...])
        dg_ref[...] += jnp.sum(dh * xh, axis=0, keepdims=True)

    row = lambda i: (i, 0)
    fix = lambda i: (0, 0)
    return _call(
        body, name, (T // tm,),
        [pl.BlockSpec((tm, D), row), pl.BlockSpec((tm, 2 * D), row), pl.BlockSpec((tm, D), row), pl.BlockSpec((1, D), fix),
         pl.BlockSpec((tm, D), row), pl.BlockSpec((D, 2 * D), fix)],
        [pl.BlockSpec((tm, D), row), pl.BlockSpec((1, D), fix), pl.BlockSpec((tm, 2 * D), row),
         pl.BlockSpec((1, 2 * D), fix)],
        [_sds((T, D), F32), _sds((1, D), F32), _sds((T, 2 * D), BF16), _sds((1, 2 * D), F32)],
    )(du, ag, x, g, dy, w1)


def _row_sums(a):
    return _dot(a.astype(BF16), jnp.ones((a.shape[1], 128), BF16))


def _head_rstd(x):
    return lax.rsqrt(_row_sums(x * x) * (1.0 / x.shape[1]) + NORM_EPS)


def _attn_qkv(x, g, wqkv, qn, kn, name):
    T, D = x.shape
    N = wqkv.shape[1]
    tn = N // 9
    E = HEAD_DIM
    tm = _pick(T, 512, 8)

    def body(x_ref, g_ref, w_ref, qn_ref, kn_ref, o_ref, a_ref, h_ref):
        j = pl.program_id(1)

        @pl.when(j == 0)
        def _():
            xv = x_ref[...]
            h_ref[...] = (xv * _rstd(xv) * g_ref[...]).astype(BF16)

        res = _dot(h_ref[...], w_ref[...])
        o_ref[...] = res.astype(BF16)
        part = j % 3

        @pl.when(part == 2)
        def _():
            a_ref[...] = res.astype(BF16)

        @pl.when(part < 2)
        def _():
            grp = j // 3
            c = qn_ref[pl.ds(grp, 1), :] * kn_ref[pl.ds(grp, 1), :] * (E ** -0.5)
            fac = jnp.where(part == 0, c, jnp.ones_like(c))
            for h in range(tn // E):
                hs = slice(h * E, (h + 1) * E)
                xh = res[:, hs]
                a_ref[:, hs] = (xh * _head_rstd(xh) * fac).astype(BF16)

    ng = qn.shape[0]
    return _call(
        body, name, (T // tm, 9),
        [pl.BlockSpec((tm, D), lambda i, j: (i, 0)), pl.BlockSpec((1, D), lambda i, j: (0, 0)),
         pl.BlockSpec((D, tn), lambda i, j: (0, j)), pl.BlockSpec((ng, E), lambda i, j: (0, 0)),
         pl.BlockSpec((ng, E), lambda i, j: (0, 0))],
        [pl.BlockSpec((tm, tn), lambda i, j: (i, j)), pl.BlockSpec((tm, tn), lambda i, j: (i, j)),
         pl.BlockSpec((tm, D), lambda i, j: (i, 0))],
        [_sds((T, N), BF16), _sds((T, N), BF16), _sds((T, D), BF16)],
    )(x, g, wqkv, qn, kn)


def _band_mask(q, steps, nblk):
    i = lax.broadcasted_iota(jnp.int32, (q, 2 * q), 0)
    j = lax.broadcasted_iota(jnp.int32, (q, 2 * q), 1)
    diff = q + i - j
    first_key = jnp.where(nblk > 0, 0, q)
    return (diff >= 0) & (diff <= steps) & (j >= first_key)


def _per_row(blk, width):
    e = blk.shape[1]
    if width % e == 0:
        return jnp.concatenate([blk] * (width // e), axis=1)
    return jnp.broadcast_to(blk[:, :1], (blk.shape[0], width))


def _streams(a, dil, to_streams, name, col0=0, ncols=None):
    T, C = a.shape
    ncols = C if ncols is None else ncols
    Q = ATTN_BLOCK
    run = Q * dil
    reps = max(1, min(2048 // run, T // run))
    while T % (run * reps):
        reps -= 1
    rows = run * reps
    cw = _pick(ncols, 512, 128)
    ns = cw // 128

    def body(a_ref, o_ref, scr):
        for s in range(ns):
            ls = slice(s * 128, (s + 1) * 128)
            slab = scr.at[s]
            if to_streams:
                slab[...] = a_ref[:, ls].astype(F32)
                for u in range(reps):
                    for r in range(dil):
                        o_ref[pl.ds(u * run + r * Q, Q), ls] = slab[pl.ds(u * run + r, Q, stride=dil), :].astype(a.dtype)
            else:
                for u in range(reps):
                    for r in range(dil):
                        slab[pl.ds(u * run + r, Q, stride=dil), :] = a_ref[pl.ds(u * run + r * Q, Q), ls].astype(F32)
                o_ref[:, ls] = slab[...].astype(a.dtype)

    return _call(
        body, name, (T // rows, ncols // cw),
        [pl.BlockSpec((rows, cw), lambda i, j: (i, col0 // cw + j))],
        [pl.BlockSpec((rows, cw), lambda i, j: (i, j))],
        [_sds((T, ncols), a.dtype)],
        scratch=[pltpu.VMEM((ns, rows, 128), F32)],
    )(a)[0]


def _attn_fwd(qkv, base, HE, window, dil, name):
    T = qkv.shape[0]
    H = HE // HEAD_DIM
    E = HEAD_DIM
    Q = ATTN_BLOCK
    nb = T // dil // Q
    steps = window // dil

    def body(q_ref, kc_ref, kp_ref, vc_ref, vp_ref, o_ref, l_ref):
        n = pl.program_id(1)
        valid = _band_mask(Q, steps, n)
        ones = jnp.ones((2 * Q, E), BF16)
        outs, lses = [], []
        for h in range(H):
            hs = slice(h * E, (h + 1) * E)
            k2 = jnp.concatenate([kp_ref[:, hs], kc_ref[:, hs]], axis=0)
            v2 = jnp.concatenate([vp_ref[:, hs], vc_ref[:, hs]], axis=0)
            s = jnp.where(valid, _dot_nt(q_ref[:, hs], k2), -1e30)
            m = jnp.max(s, axis=-1, keepdims=True)
            p = jnp.exp(s - m).astype(BF16)
            acc = _dot(p, jnp.concatenate([v2, ones], axis=1))
            l = acc[:, E:]
            outs.append((acc[:, :E] * (1.0 / l)).astype(o_ref.dtype))
            lses.append(m + jnp.log(l))
        o_ref[...] = jnp.concatenate(outs, axis=1)
        l_ref[...] = jnp.concatenate(lses, axis=1)

    blk = lambda s, back: pl.BlockSpec((Q, HE), lambda r, n: (jnp.maximum(n - back, 0) * dil + r, base + s))
    out = pl.BlockSpec((Q, HE), lambda r, n: (n * dil + r, 0))
    return _call(
        body, name, (dil, nb),
        [blk(0, 0), blk(1, 0), blk(1, 1), blk(2, 0), blk(2, 1)],
        [out, out],
        [_sds((T, HE), BF16), _sds((T, HE), F32)],
    )(qkv, qkv, qkv, qkv, qkv)


def _attn_merge(os, lses, x, wo, name):
    T, D = x.shape
    HE = wo.shape[0]
    tm = _pick(T, 512, 8)
    ng = len(os)

    def body(*refs):
        o_refs = refs[:ng]
        l_refs = refs[ng:2 * ng]
        x_ref, w_ref, xo_ref, om_ref, lt_ref = refs[2 * ng:]
        ls = [r[...] for r in l_refs]
        m = functools.reduce(jnp.maximum, ls)
        es = [jnp.exp(l - m) for l in ls]
        tot = functools.reduce(lambda a, b: a + b, es)
        inv = 1.0 / tot
        om = functools.reduce(lambda a, b: a + b, [e * inv * r[...] for e, r in zip(es, o_refs)])
        omb = om.astype(BF16)
        om_ref[...] = omb
        lt_ref[...] = m + jnp.log(tot)
        xo_ref[...] = x_ref[...] + _dot(omb, w_ref[...])

    row = lambda i: (i, 0)
    fix = lambda i: (0, 0)
    return _call(
        body, name, (T // tm,),
        [pl.BlockSpec((tm, HE), row)] * (2 * ng) + [pl.BlockSpec((tm, D), row), pl.BlockSpec((HE, D), fix)],
        [pl.BlockSpec((tm, D), row), pl.BlockSpec((tm, HE), row), pl.BlockSpec((tm, HE), row)],
        [_sds((T, D), F32), _sds((T, HE), BF16), _sds((T, HE), F32)],
    )(*os, *lses, x, wo)


def _attn_out_bwd(dy, om, wo, name, after=()):
    T, D = dy.shape
    HE = wo.shape[0]
    E = HEAD_DIM
    tm = _pick(T, 512, 8)

    def body(dy_ref, om_ref, w_ref, dom_ref, dl_ref):
        dom = _dot_nt(dy_ref[...].astype(BF16), w_ref[...])
        dom_ref[...] = dom.astype(BF16)
        prod = dom * om_ref[...].astype(F32)
        for h in range(HE // E):
            hs = slice(h * E, (h + 1) * E)
            dl_ref[:, hs] = jnp.broadcast_to(jnp.sum(prod[:, hs], axis=-1, keepdims=True), (tm, E))

    row = lambda i: (i, 0)
    return _call(
        body, name, (T // tm,),
        [pl.BlockSpec((tm, D), row), pl.BlockSpec((tm, HE), row), pl.BlockSpec((HE, D), lambda i: (0, 0))],
        [pl.BlockSpec((tm, HE), row), pl.BlockSpec((tm, HE), row)],
        [_sds((T, HE), BF16), _sds((T, HE), F32)], after=after,
    )(dy, om, wo)


def _attn_bwd(qkv, base, HE, dom, lse, delta, window, dil, name):
    T = qkv.shape[0]
    H = HE // HEAD_DIM
    E = HEAD_DIM
    Q = ATTN_BLOCK
    nb = T // dil // Q
    steps = window // dil

    def body(q_ref, kc_ref, kp_ref, vc_ref, vp_ref, do_ref, l_ref, dl_ref, dq_ref, dk_ref, dv_ref, ck_sc, cv_sc):
        n = pl.program_id(1)

        @pl.when(n == 0)
        def _():
            ck_sc[...] = jnp.zeros_like(ck_sc)
            cv_sc[...] = jnp.zeros_like(cv_sc)

        @pl.when(n < nb)
        def _():
            valid = _band_mask(Q, steps, n)
            ck_old = ck_sc[...]
            cv_old = cv_sc[...]
            dqs, dks, dvs = [], [], []
            for h in range(H):
                hs = slice(h * E, (h + 1) * E)
                q = q_ref[:, hs]
                do = do_ref[:, hs]
                k2 = jnp.concatenate([kp_ref[:, hs], kc_ref[:, hs]], axis=0)
                v2 = jnp.concatenate([vp_ref[:, hs], vc_ref[:, hs]], axis=0)
                p = jnp.where(valid, jnp.exp(_dot_nt(q, k2) - _per_row(l_ref[:, hs], 2 * Q)), 0.0)
                ds = (p * (_dot_nt(do, v2) - _per_row(dl_ref[:, hs], 2 * Q))).astype(BF16)
                dqs.append(_dot(ds, k2).astype(BF16))
                dks.append(_dot_tn(ds, q))
                dvs.append(_dot_tn(p.astype(BF16), do))
            cat = lambda parts: jnp.concatenate(parts, axis=1)
            dq_ref[...] = cat(dqs)
            dk_ref[...] = (ck_old + cat([d[:Q] for d in dks])).astype(BF16)
            dv_ref[...] = (cv_old + cat([d[:Q] for d in dvs])).astype(BF16)
            ck_sc[...] = cat([d[Q:] for d in dks])
            cv_sc[...] = cat([d[Q:] for d in dvs])

        @pl.when(n == nb)
        def _():
            dk_ref[...] = ck_sc[...].astype(BF16)
            dv_ref[...] = cv_sc[...].astype(BF16)

    nq = lambda n: jnp.minimum(n, nb - 1)
    blk = lambda s, back: pl.BlockSpec((Q, HE), lambda r, n: (jnp.maximum(nq(n) - back, 0) * dil + r, base + s))
    qblk = pl.BlockSpec((Q, HE), lambda r, n: (nq(n) * dil + r, 0))
    kblk = pl.BlockSpec((Q, HE), lambda r, n: (jnp.maximum(n - 1, 0) * dil + r, 0))
    return _call(
        body, name, (dil, nb + 1),
        [blk(0, 0), blk(1, 0), blk(1, 1), blk(2, 0), blk(2, 1), qblk, qblk, qblk],
        [qblk, kblk, kblk],
        [_sds((T, HE), BF16)] * 3,
        scratch=[pltpu.VMEM((Q, HE), F32), pltpu.VMEM((Q, HE), F32)],
    )(qkv, qkv, qkv, qkv, qkv, dom, lse, delta)


def _qk_norm_bwd(dq, dk, qkv, base, HE, gq, gk, name):
    T = dq.shape[0]
    E = HEAD_DIM
    tm = _pick(T, 512, 8)
    scale = E ** -0.5

    def body(dq_ref, dk_ref, q_ref, k_ref, gq_ref, gk_ref, oq_ref, ok_ref, dgq_ref, dgk_ref):
        @pl.when(pl.program_id(0) == 0)
        def _():
            dgq_ref[...] = jnp.zeros_like(dgq_ref)
            dgk_ref[...] = jnp.zeros_like(dgk_ref)

        gqv = gq_ref[...]
        gkv = gk_ref[...]
        c = gqv * gkv * scale
        dc = jnp.zeros((1, E), F32)
        for h in range(HE // E):
            hs = slice(h * E, (h + 1) * E)
            q = q_ref[:, hs].astype(F32)
            rq = _head_rstd(q)
            qh = q * rq
            a = dq_ref[:, hs].astype(F32)
            dc = dc + jnp.sum(a * qh, axis=0, keepdims=True)
            dqh = a * c
            oq_ref[:, hs] = (rq * (dqh - qh * (_row_sums(dqh * qh) * (1.0 / E)))).astype(BF16)
            k = k_ref[:, hs].astype(F32)
            rk = _head_rstd(k)
            kh = k * rk
            b = dk_ref[:, hs].astype(F32)
            ok_ref[:, hs] = (rk * (b - kh * (_row_sums(b * kh) * (1.0 / E)))).astype(BF16)
        dgq_ref[...] += dc * (gkv * scale)
        dgk_ref[...] += dc * (gqv * scale)

    row = lambda i: (i, 0)
    vec = pl.BlockSpec((1, E), lambda i: (0, 0))
    return _call(
        body, name, (T // tm,),
        [pl.BlockSpec((tm, HE), row), pl.BlockSpec((tm, HE), row), pl.BlockSpec((tm, HE), lambda i: (i, base)),
         pl.BlockSpec((tm, HE), lambda i: (i, base + 1)), vec, vec],
        [pl.BlockSpec((tm, HE), row), pl.BlockSpec((tm, HE), row), vec, vec],
        [_sds((T, HE), BF16), _sds((T, HE), BF16), _sds((1, E), F32), _sds((1, E), F32)],
    )(dq, dk, qkv, qkv, gq, gk)


def _attn_qkv_bwd(dqkv, x, g, dy, wqkv, name):
    T, D = x.shape
    HE = wqkv.shape[1] // 9
    tm = _pick(T, 512, 8)

    def body(*refs):
        d_refs = refs[:9]
        x_ref, g_ref, dy_ref, w_ref, dx_ref, dg_ref, dh_sc = refs[9:]
        i = pl.program_id(0)
        j = pl.program_id(1)

        @pl.when((i == 0) & (j == 0))
        def _():
            dg_ref[...] = jnp.zeros_like(dg_ref)

        @pl.when(j == 0)
        def _():
            dh_sc[...] = jnp.zeros_like(dh_sc)

        for s in range(9):
            @pl.when(j == s)
            def _(s=s):
                dh_sc[...] += _dot_nt(d_refs[s][...], w_ref[...])

        @pl.when(j == 8)
        def _():
            xv = x_ref[...]
            r = _rstd(xv)
            xh = xv * r
            dh = dh_sc[...]
            dx_ref[...] = dy_ref[...] + _norm_bwd(dh, xh, r, g_ref[...])
            dg_ref[...] += jnp.sum(dh * xh, axis=0, keepdims=True)

    row = lambda i, j: (i, 0)
    fix = lambda i, j: (0, 0)
    return _call(
        body, name, (T // tm, 9),
        [pl.BlockSpec((tm, HE), row)] * 9 + [pl.BlockSpec((tm, D), row), pl.BlockSpec((1, D), fix), pl.BlockSpec((tm, D), row),
                                           pl.BlockSpec((D, HE), lambda i, j: (0, j))],
        [pl.BlockSpec((tm, D), row), pl.BlockSpec((1, D), fix)],
        [_sds((T, D), F32), _sds((1, D), F32)],
        scratch=[pltpu.VMEM((tm, D), F32)],
    )(*dqkv, x, g, dy, wqkv)


def _loss_head(y, target, name):
    T, D = y.shape
    tm = _pick(T, 512, 8)

    def body(y_ref, t_ref, dy_ref, sq_ref):
        @pl.when(pl.program_id(0) == 0)
        def _():
            sq_ref[...] = jnp.zeros_like(sq_ref)

        err = y_ref[...] - t_ref[...]
        dy_ref[...] = err * (1.0 / D)
        sq_ref[...] += jnp.sum(err * err, axis=0, keepdims=True)

    row = lambda i: (i, 0)
    return _call(
        body, name, (T // tm,),
        [pl.BlockSpec((tm, D), row), pl.BlockSpec((tm, D), row)],
        [pl.BlockSpec((tm, D), row), pl.BlockSpec((1, D), lambda i: (0, 0))],
        [_sds((T, D), F32), _sds((1, D), F32)],
    )(y, target)


def _local_step(x, target, norm_g, b_pw1, w_dw, b_dw, cng, b_pw2, qn, kn, block_weights, grads_ready):
    T, D = x.shape
    ng = lambda l, k: norm_g[l, k][None, :]
    bn = _pick(D, 256, 128)

    w_in, w_out = [None] * 4, [None] * 4
    w_in[0], w_out[0] = block_weights(0, x)
    x1, *gu0 = _ffn_fwd(x, ng(0, 0), w_in[0], w_out[0], "ffn_fwd_0")
    pw1, pw2 = block_weights(1, x1)
    ag, u, hc = _conv_pre(x1, ng(0, 1), pw1, b_pw1, "conv_pre")
    c = _dwconv(u, w_dw, b_dw, "dwconv")
    x2, s = _conv_post(c, x1, cng, pw2, b_pw2, "conv_post")
    w_in[1], w_out[1] = block_weights(2, x2)
    x3, *gu1 = _ffn_fwd(x2, ng(0, 2), w_in[1], w_out[1], "ffn_fwd_1")
    w_in[2], w_out[2] = block_weights(3, x3)
    x4, *gu2 = _ffn_fwd(x3, ng(1, 0), w_in[2], w_out[2], "ffn_fwd_2")
    wqkv, wo = block_weights(4, x4)
    HE = wo.shape[0]
    F = w_out[0].shape[0]
    bf = _pick(F, 256, 128)
    bh = _pick(HE, 512, 128)
    qkv, att, ha = _attn_qkv(x4, ng(1, 1), wqkv, qn, kn, "attn_qkv")
    qkv_s = [(att, 3 * gi) if dil == 1 else (_streams(att, dil, True, f"qkv_streams_{gi}", 3 * gi * HE, 3 * HE), 0)
             for gi, (_, dil) in enumerate(ATTN_GROUPS)]
    tokens = lambda a, dil, name: a if dil == 1 else _streams(a, dil, False, name)
    streams = lambda a, dil, name: a if dil == 1 else _streams(a, dil, True, name)
    os, lses = [], []
    for gi, (window, dil) in enumerate(ATTN_GROUPS):
        o, l = _attn_fwd(*qkv_s[gi], HE, window, dil, f"attn_fwd_{gi}")
        os.append(tokens(o, dil, f"o_tokens_{gi}"))
        lses.append(tokens(l, dil, f"lse_tokens_{gi}"))
    x5, om, lse = _attn_merge(os, lses, x4, wo, "attn_merge")
    w_in[3], w_out[3] = block_weights(5, x5)
    x6, *gu3 = _ffn_fwd(x5, ng(1, 2), w_in[3], w_out[3], "ffn_fwd_3")
    dy, sq = _loss_head(x6, target, "loss_head")

    grads = {"ffn_w_in": [None] * 4, "ffn_w_out": [None] * 4}
    dnorm = [[None] * 3 for _ in range(2)]

    def ffn_back(lf, k, xin, gvec, dy, gu, after):
        dx, dg, dgate, dup, a, h, dyb = _ffn_bwd(xin, gvec, dy, gu[0], gu[1], w_in[lf], w_out[lf], f"ffn_bwd_{lf}", after=after)
        d_in = _mm_tn(h, dgate, D, bf, (D, 2 * F), (D, bf), lambda mi, ni: (0, ni), 1.0, f"ffn_dw_gate_{lf}")
        grads["ffn_w_in"][lf] = _mm_tn(h, dup, D, bf, (D, 2 * F), (D, bf), lambda mi, ni: (0, F // bf + ni), 1.0,
                                       f"ffn_dw_up_{lf}", prev=d_in)
        grads["ffn_w_out"][lf] = _mm_tn(a, dyb, bf, D, (F, D), (bf, D), lambda mi, ni: (mi, 0), 1.0, f"ffn_dw_out_{lf}")
        return dx, dg, grads_ready(k, (grads["ffn_w_in"][lf], grads["ffn_w_out"][lf]), dx)

    dx, dnorm[1][2], tok = ffn_back(3, 5, x5, ng(1, 2), dy, gu3, ())
    dom, delta = _attn_out_bwd(dx, om, wo, "attn_out_bwd", after=tok)
    grads["attn_w_o"] = _mm_tn(om, dx, HE, bn, (HE, D), (HE, bn), lambda mi, ni: (0, ni), 1.0, "attn_dw_o")
    dqkv, dgq, dgk = [], [], []
    for gi, (window, dil) in enumerate(ATTN_GROUPS):
        ds = _attn_bwd(*qkv_s[gi], HE, streams(dom, dil, f"dom_streams_{gi}"), streams(lse, dil, f"lse_streams_{gi}"),
                       streams(delta, dil, f"delta_streams_{gi}"), window, dil, f"attn_bwd_{gi}")
        dq, dk, dv = [tokens(d, dil, f"d{nm}_tokens_{gi}") for d, nm in zip(ds, "qkv")]
        dq, dk, a_, b_ = _qk_norm_bwd(dq, dk, qkv, 3 * gi, HE, qn[gi][None, :], kn[gi][None, :], f"qk_norm_bwd_{gi}")
        dqkv += [dq, dk, dv]
        dgq.append(a_)
        dgk.append(b_)
    grads["attn_q_norm"] = jnp.concatenate(dgq, axis=0)
    grads["attn_k_norm"] = jnp.concatenate(dgk, axis=0)
    d_qkv = None
    for s9 in range(9):
        d_qkv = _mm_tn(ha, dqkv[s9], D, bh, (D, 9 * HE), (D, bh), lambda mi, ni, s9=s9: (0, s9 * (HE // bh) + ni), 1.0,
                       f"attn_dw_qkv_{s9}", prev=d_qkv)
    grads["attn_w_qkv"] = d_qkv
    dx, dnorm[1][1] = _attn_qkv_bwd(dqkv, x4, ng(1, 1), dx, wqkv, "attn_qkv_bwd")
    tok = grads_ready(4, (grads["attn_w_qkv"], grads["attn_w_o"]), dx)
    dx, dnorm[1][0], tok = ffn_back(2, 3, x3, ng(1, 0), dx, gu2, tok)

    dx, dnorm[0][2], tok = ffn_back(1, 2, x2, ng(0, 2), dx, gu1, tok)
    dc, grads["conv_norm_g"], grads["conv_b_pw2"] = _conv_post_bwd(dx, c, cng, pw2, "conv_post_bwd", after=tok)
    grads["conv_w_pw2"] = _mm_tn(s, dx, D, bn, (D, D), (D, bn), lambda mi, ni: (0, ni), 1.0, "conv_dw_pw2")
    du, grads["conv_w_dw"], grads["conv_b_dw"] = _dwconv_bwd(dc, u, w_dw, "dwconv_bwd")
    dx, dnorm[0][1], dag, grads["conv_b_pw1"] = _conv_pre_bwd(du, ag, x1, ng(0, 1), dx, pw1, "conv_pre_bwd")
    grads["conv_w_pw1"] = _mm_tn(hc, dag, D, 2 * bn, (D, 2 * D), (D, 2 * bn), lambda mi, ni: (0, ni), 1.0, "conv_dw_pw1")
    tok = grads_ready(1, (grads["conv_w_pw1"], grads["conv_w_pw2"]), dx)
    dx, dnorm[0][0], _ = ffn_back(0, 0, x, ng(0, 0), dx, gu0, tok)

    grads["norm_g"] = jnp.concatenate([jnp.concatenate(r, axis=0)[None] for r in dnorm], axis=0)
    return sq, dx, grads


class _Sharded:
    def __init__(self, name, full3, half_axis, shard_axis, src, slab=None):
        self.name, self.full3, self.half_axis, self.shard_axis = name, tuple(full3), half_axis, shard_axis
        self.src, self.slab = src, slab

    def source(self, refs):
        return refs[self.src] if self.slab is None else refs[self.src].at[self.slab]

    def _cut(self, shape, axis, parts):
        s = list(shape)
        s[axis] //= parts
        return tuple(s)

    @property
    def shard3(self):
        return self._cut(self.full3, self.shard_axis, N_CHIPS)

    @property
    def pair3(self):
        return self._cut(self.full3, self.half_axis, 2)

    @property
    def part3(self):
        return self._cut(self.shard3, self.half_axis, 2)

    @staticmethod
    def _slice(ref, axis, idx, parts):
        n = ref.shape[axis] // parts
        start = idx * n
        minor = len(ref.shape) - 1 - axis
        if minor < 2 and not isinstance(start, int):
            start = pl.multiple_of(start, 128 if minor == 0 else (16 if n % 16 == 0 else 8))
        sl = [slice(None)] * len(ref.shape)
        sl[axis] = pl.ds(start, n)
        return ref.at[tuple(sl)]

    def half(self, ref, h):
        return self._slice(ref, self.half_axis, h, 2)

    def shard(self, ref, j):
        return self._slice(ref, self.shard_axis, j, N_CHIPS)


def _place():
    x, y, c = lax.axis_index("x"), lax.axis_index("y"), lax.axis_index("c")
    return x, y, c, 2 * x + y


_RELS = (1, 2, 3)


def _peer(x, y, rel):
    px = 1 - x if rel & 2 else x
    py = 1 - y if rel & 1 else y
    return px, py, 2 * px + py


ANY = pl.BlockSpec(memory_space=pl.ANY)


def _comm_call(body, name, n_in, out_shape, n_sems, aliases=None):
    return pl.pallas_call(
        body, name=name, in_specs=[ANY] * n_in, out_specs=[ANY] * len(out_shape), out_shape=out_shape,
        scratch_shapes=[pltpu.SemaphoreType.DMA((n,)) for n in n_sems],
        input_output_aliases=aliases or {},
        compiler_params=pltpu.CompilerParams(has_side_effects=True),
    )


def _remote(src, dst, send_sem, recv_sem, dev):
    return pltpu.make_async_remote_copy(src_ref=src, dst_ref=dst, send_sem=send_sem, recv_sem=recv_sem, device_id=dev,
                                        device_id_type=MESH)


def _gather_small(small_shards):
    ns = len(small_shards)

    def body(*refs):
        ins, outs = refs[:ns], refs[ns:2 * ns]
        lsem, ssem, rsem = refs[2 * ns:]
        x, y, c, me = _place()
        cols = lambda ref, j: _Sharded._slice(ref, 1, j, N_CHIPS)
        local = [pltpu.make_async_copy(ins[i], cols(outs[i], me), lsem.at[i]) for i in range(ns)]
        sends = []
        for i in range(ns):
            for k, rel in enumerate(_RELS):
                px, py, _ = _peer(x, y, rel)
                sends.append(_remote(ins[i], cols(outs[i], me), ssem.at[3 * i + k], rsem.at[3 * i + k], (px, py, c)))
        for cp in local + sends:
            cp.start()
        for i in range(ns):
            for k, rel in enumerate(_RELS):
                _, _, pj = _peer(x, y, rel)
                got = cols(outs[i], pj)
                _remote(got, got, ssem.at[3 * i + k], rsem.at[3 * i + k], (x, y, c)).wait_recv()
        for cp in sends:
            cp.wait_send()
        for cp in local:
            cp.wait()

    out_shape = [_sds((s.shape[0], s.shape[1] * N_CHIPS), F32) for s in small_shards]
    return _comm_call(body, "gather_small", ns, out_shape, [ns, 3 * ns, 3 * ns])(*small_shards)


HBM = pl.BlockSpec(memory_space=pltpu.HBM)
SEM = pl.BlockSpec(memory_space=pltpu.SEMAPHORE)
DATAFLOW = pltpu.SideEffectType.DATAFLOW_SIDE_EFFECTING


def _in_hbm(a):
    return pltpu.with_memory_space_constraint(a, pltpu.HBM)


def _cast_place(it, shard, scal, after=()):
    a_n, r_n, c_n = it.shard3
    tr = _pick(r_n, 256, 16)
    sa = it.shard_axis

    def body(sc_ref, s_ref, o_ref):
        o_ref[...] = s_ref[...].astype(BF16)

    if it.slab is None:
        src = pl.BlockSpec((1, tr, c_n), lambda a, rb, sc: (a, rb, 0))
    else:
        src = pl.BlockSpec((None, 1, tr, c_n), lambda a, rb, sc: (it.slab, a, rb, 0))
    dst = pl.BlockSpec((1, tr, c_n), lambda a, rb, sc: (a + sc[1] * (a_n if sa == 0 else 0), rb + sc[1] * (r_n // tr if sa == 1 else 0),
                                                       sc[1] if sa == 2 else 0))
    return _call(body, f"cast_place_{it.name}", (a_n, r_n // tr), [src], [dst], [_sds(it.full3, BF16)], prefetch=1,
                 after=after)(scal, shard)[0]


def _gather_start(items, fulls, name):
    ni = len(items)

    def body(*refs):
        outs = refs[ni:]
        ssem, rsem, full = outs[:ni], outs[ni:2 * ni], outs[2 * ni:3 * ni]
        x, y, c, me = _place()
        for i, it in enumerate(items):
            mine = it.half(it.shard(full[i], me), c)
            for k, rel in enumerate(_RELS):
                px, py, _ = _peer(x, y, rel)
                _remote(mine, mine, ssem[i].at[k], rsem[i].at[k], (px, py, c)).start()

    outs = pl.pallas_call(
        body, name=name, in_specs=[HBM] * ni, out_specs=[SEM] * (2 * ni) + [HBM] * ni,
        out_shape=[pltpu.SemaphoreType.DMA((3,))] * (2 * ni) + [pltpu.HBM(it.full3, BF16) for it in items],
        input_output_aliases={j: 2 * ni + j for j in range(ni)},
        compiler_params=pltpu.CompilerParams(has_side_effects=DATAFLOW),
    )(*[_in_hbm(f) for f in fulls])
    return outs[:ni], outs[ni:2 * ni], outs[2 * ni:]


def _gather_forward(items, fulls, ssems, rsems, after, name):
    ni = len(items)

    def body(*refs):
        ssem, rsem = refs[ni:2 * ni], refs[2 * ni:3 * ni]
        outs = refs[3 * ni + 1:]
        full, fsem, gsem = outs[:ni], outs[ni:2 * ni], outs[2 * ni:3 * ni]
        x, y, c, me = _place()
        sib = (x, y, 1 - c)
        for i, it in enumerate(items):
            for k, rel in enumerate(_RELS):
                _, _, pj = _peer(x, y, rel)
                got = it.half(it.shard(full[i], pj), c)
                _remote(got, got, ssem[i].at[k], rsem[i].at[k], sib).wait_recv()
                _remote(got, got, fsem[i].at[k], gsem[i].at[k], sib).start()
        for i, it in enumerate(items):
            mine = it.half(it.shard(full[i], me), c)
            for k in range(3):
                _remote(mine, mine, ssem[i].at[k], rsem[i].at[k], sib).wait_send()

    outs = pl.pallas_call(
        body, name=name, in_specs=[HBM] * ni + [SEM] * (2 * ni) + [ANY],
        out_specs=[HBM] * ni + [SEM] * (2 * ni),
        out_shape=[pltpu.HBM(it.full3, BF16) for it in items] + [pltpu.SemaphoreType.DMA((3,))] * (2 * ni),
        input_output_aliases={i: i for i in range(ni)},
        compiler_params=pltpu.CompilerParams(has_side_effects=DATAFLOW),
    )(*fulls, *ssems, *rsems, after)
    return outs[:ni], outs[ni:2 * ni], outs[2 * ni:]


def _gather_finish(items, fulls, fsems, gsems, name):
    ni = len(items)

    def body(*refs):
        fsem, gsem = refs[ni:2 * ni], refs[2 * ni:3 * ni]
        full = refs[3 * ni:]
        x, y, c, _ = _place()
        sib = (x, y, 1 - c)
        for i, it in enumerate(items):
            for k, rel in enumerate(_RELS):
                _, _, pj = _peer(x, y, rel)
                got = it.half(it.shard(full[i], pj), 1 - c)
                _remote(got, got, fsem[i].at[k], gsem[i].at[k], sib).wait_recv()
                sent = it.half(it.shard(full[i], pj), c)
                _remote(sent, sent, fsem[i].at[k], gsem[i].at[k], sib).wait_send()

    return pl.pallas_call(
        body, name=name, in_specs=[HBM] * ni + [SEM] * (2 * ni), out_specs=[HBM] * ni,
        out_shape=[pltpu.HBM(it.full3, BF16) for it in items],
        input_output_aliases={i: i for i in range(ni)},
        compiler_params=pltpu.CompilerParams(has_side_effects=DATAFLOW),
    )(*fulls, *fsems, *gsems)


def _exchange_start(name, arrays, n_sems, copies):
    na, ns = len(arrays), len(n_sems)

    def body(*refs):
        outs = refs[na:]
        ssem, rsem, thru, token = outs[:ns], outs[ns:2 * ns], outs[2 * ns:2 * ns + na], outs[-1]
        for send, _ in copies(thru, ssem, rsem):
            send.start()
        token[...] = jnp.zeros_like(token)

    outs = pl.pallas_call(
        body, name=name, in_specs=[HBM] * na,
        out_specs=[SEM] * (2 * ns) + [HBM] * na + [pl.BlockSpec(memory_space=pltpu.VMEM)],
        out_shape=[pltpu.SemaphoreType.DMA((n,)) for n in n_sems] * 2 + [pltpu.HBM(a.shape, a.dtype) for a in arrays]
        + [_sds((8, 128), F32)],
        input_output_aliases={j: 2 * ns + j for j in range(na)},
        compiler_params=pltpu.CompilerParams(has_side_effects=DATAFLOW),
    )(*[_in_hbm(a) for a in arrays])
    return outs[:ns], outs[ns:2 * ns], outs[2 * ns:2 * ns + na], outs[-1]


def _exchange_wait(name, arrays, ssems, rsems, copies, after):
    na, ns = len(arrays), len(ssems)

    def body(*refs):
        ssem, rsem = refs[na:na + ns], refs[na + ns:na + 2 * ns]
        thru = refs[na + 2 * ns + 1:]
        for send, recv in copies(thru, ssem, rsem):
            recv.wait_recv()
            send.wait_send()

    return pl.pallas_call(
        body, name=name, in_specs=[HBM] * na + [SEM] * (2 * ns) + [ANY], out_specs=[HBM] * na,
        out_shape=[pltpu.HBM(a.shape, a.dtype) for a in arrays],
        input_output_aliases={i: i for i in range(na)},
        compiler_params=pltpu.CompilerParams(has_side_effects=DATAFLOW),
    )(*arrays, *ssems, *rsems, after)


def _pair_copies(items):
    n = len(items)

    def copies(a, ssem, rsem):
        x, y, c, _ = _place()
        cps = [_remote(it.half(a[i], 1 - c), a[n + i], ssem[i].at[0], rsem[i].at[0], (x, y, 1 - c)) for i, it in enumerate(items)]
        return [(cp, cp) for cp in cps]

    return copies


def _chip_copies(items):
    n = len(items)

    def copies(a, ssem, rsem):
        x, y, c, _ = _place()
        cps = []
        for i, it in enumerate(items):
            for k, rel in enumerate(_RELS):
                px, py, pj = _peer(x, y, rel)
                cps.append(_remote(it.shard(a[i], pj), a[n + i].at[k], ssem[i].at[k], rsem[i].at[k], (px, py, c)))
        return [(cp, cp) for cp in cps]

    return copies


def _fill_copies(items):
    def copies(a, ssem, rsem):
        x, y, c, _ = _place()
        sib = (x, y, 1 - c)
        out = []
        for i, it in enumerate(items):
            mine, other = it.half(a[i], c), it.half(a[i], 1 - c)
            out.append((_remote(mine, mine, ssem[i].at[0], rsem[i].at[0], sib), _remote(other, other, ssem[i].at[0], rsem[i].at[0], sib)))
        return out

    return copies


def _ew_tiles(d):
    _, rows, cols = d.part3
    return _pick(rows, 256, 16), cols


def _pair_add(d, g_full, got, scal):
    tr, tc = _ew_tiles(d)
    a_n, r_n, c_n = d.pair3
    ha = d.half_axis

    def body(sc_ref, g_ref, r_ref, o_ref, ob_ref):
        s = g_ref[...] + r_ref[...].astype(F32)
        o_ref[...] = s
        ob_ref[...] = s.astype(BF16)

    blk = (1, tr, tc)
    same = pl.BlockSpec(blk, lambda a, rb, cb, sc: (a, rb, cb))
    mine = pl.BlockSpec(blk, lambda a, rb, cb, sc: (a + sc[0] * (a_n if ha == 0 else 0), rb + sc[0] * (r_n // tr if ha == 1 else 0), cb))
    return _call(body, f"pair_add_{d.name}", (a_n, r_n // tr, c_n // tc), [mine, same], [same, same],
                 [_sds(d.pair3, F32), _sds(d.pair3, BF16)], prefetch=1)(scal, g_full, got)


def _chip_reduce(d, pair_f32, got, scal):
    tr, tc = _ew_tiles(d)
    a_n, r_n, c_n = d.part3
    ha, sa = d.half_axis, d.shard_axis

    def body(sc_ref, p_ref, r0, r1, r2, o_ref):
        o_ref[...] = ((p_ref[...] + r0[...].astype(F32)) + r1[...].astype(F32)) + r2[...].astype(F32)

    blk = (1, tr, tc)
    own = pl.BlockSpec(blk, lambda a, rb, sc: (a + sc[1] * (a_n if sa == 0 else 0), rb + sc[1] * (r_n // tr if sa == 1 else 0),
                                               sc[1] if sa == 2 else 0))
    slot = lambda k: pl.BlockSpec((None,) + blk, lambda a, rb, sc: (k, a, rb, 0))
    out = pl.BlockSpec(blk, lambda a, rb, sc: (a + sc[0] * (a_n if ha == 0 else 0), rb + sc[0] * (r_n // tr if ha == 1 else 0), 0))
    return _call(body, f"chip_reduce_{d.name}", (a_n, r_n // tr), [own, slot(0), slot(1), slot(2)], [out],
                 [_sds(d.shard3, F32)], prefetch=1)(scal, pair_f32, got, got, got)[0]


def _adam_math(g, w, m, v):
    m = ADAM_B1 * m + (1.0 - ADAM_B1) * g
    v = ADAM_B2 * v + (1.0 - ADAM_B2) * (g * g)
    m_hat = m / (1.0 - ADAM_B1 ** ADAM_STEP)
    v_hat = v / (1.0 - ADAM_B2 ** ADAM_STEP)
    delta = -ADAM_LR * (m_hat / (jnp.sqrt(v_hat) + ADAM_EPS) + ADAM_WD * w)
    return delta, m, v


def _adam(d, g, w, m, v, prev=None):
    tr, tc = _ew_tiles(d)
    a_n, r_n, c_n = d.shard3
    n_prev = 0 if prev is None else 4

    def body(g_ref, w_ref, m_ref, v_ref, *rest):
        go_ref, d_ref, mo_ref, vo_ref = rest[n_prev:]
        gv = g_ref[...]
        go_ref[...] = gv
        d_ref[...], mo_ref[...], vo_ref[...] = _adam_math(gv, w_ref[...], m_ref[...], v_ref[...])

    plain = pl.BlockSpec((1, tr, tc), lambda a, rb: (a, rb, 0))
    if d.slab is None:
        wspec, shape = plain, d.shard3
    else:
        wspec, shape = pl.BlockSpec((None, 1, tr, tc), lambda a, rb: (d.slab, a, rb, 0)), (4,) + d.shard3
    in_specs = [plain] + [wspec] * 3
    args = [g, w, m, v]
    aliases = None
    if prev is not None:
        in_specs += [pl.BlockSpec(memory_space=pl.ANY)] * 4
        args += list(prev)
        aliases = {4 + k: k for k in range(4)}
    return _call(body, f"adam_{d.name}", (a_n, r_n // tr), in_specs, [wspec] * 4, [_sds(shape, F32)] * 4, aliases=aliases)(*args)


def _adam_small(gs, ws, ms, vs):
    n = len(gs)

    def body(*refs):
        for i in range(n):
            g, w, m, v = (refs[k * n + i][...] for k in range(4))
            d, mo, vo = _adam_math(g, w, m, v)
            refs[4 * n + i][...] = d
            refs[5 * n + i][...] = mo
            refs[6 * n + i][...] = vo

    vm = pl.BlockSpec(memory_space=pltpu.VMEM)
    outs = pl.pallas_call(body, name="adam_small", in_specs=[vm] * (4 * n), out_specs=[vm] * (3 * n),
                          out_shape=[_sds(g.shape, F32) for g in gs] * 3)(*gs, *ws, *ms, *vs)
    return outs[:n], outs[n:2 * n], outs[2 * n:]


def _allreduce_small(packed):
    rows, cols = packed.shape
    others = [(dx, dy, dc) for dx in (0, 1) for dy in (0, 1) for dc in (0, 1) if (dx, dy, dc) != (0, 0, 0)]

    def body(in_ref, out_ref, buf, ssem, rsem):
        x, y, c, _ = _place()
        lin = 4 * x + 2 * y + c
        buf[lin] = in_ref[...]
        cps = []
        for k, (dx, dy, dc) in enumerate(others):
            px = 1 - x if dx else x
            py = 1 - y if dy else y
            pc = 1 - c if dc else c
            cps.append((pltpu.make_async_remote_copy(src_ref=in_ref, dst_ref=buf.at[lin], send_sem=ssem.at[k], recv_sem=rsem.at[k],
                                                     device_id=(px, py, pc), device_id_type=MESH), 4 * px + 2 * py + pc))
        for cp, _ in cps:
            cp.start()
        for k, (cp, plin) in enumerate(cps):
            pltpu.make_async_remote_copy(src_ref=in_ref, dst_ref=buf.at[plin], send_sem=ssem.at[k], recv_sem=rsem.at[k],
                                         device_id=(x, y, c), device_id_type=MESH).wait_recv()
        for cp, _ in cps:
            cp.wait_send()
        acc = buf[0]
        for dev in range(1, 8):
            acc = acc + buf[dev]
        out_ref[...] = acc

    vm = pl.BlockSpec(memory_space=pltpu.VMEM)
    return pl.pallas_call(
        body, name="allreduce_small", in_specs=[vm], out_specs=vm, out_shape=_sds((rows, cols), F32),
        scratch_shapes=[pltpu.VMEM((8, rows, cols), F32), pltpu.SemaphoreType.DMA((7,)), pltpu.SemaphoreType.DMA((7,))],
        compiler_params=pltpu.CompilerParams(has_side_effects=True),
    )(packed)


_BIG = ("ffn_w_in", "ffn_w_out", "conv_w_pw1", "conv_w_pw2", "attn_w_qkv", "attn_w_o")
_SMALL = ("norm_g", "conv_b_pw1", "conv_w_dw", "conv_b_dw", "conv_norm_g", "conv_b_pw2", "attn_q_norm", "attn_k_norm")
_NAMES = ("norm_g", "ffn_w_in", "ffn_w_out", "conv_w_pw1", "conv_b_pw1", "conv_w_dw", "conv_b_dw", "conv_norm_g", "conv_w_pw2",
          "conv_b_pw2", "attn_w_qkv", "attn_q_norm", "attn_k_norm", "attn_w_o")


def _rows8(a, width):
    a = a.reshape(-1, min(a.shape[-1], width))
    return jnp.pad(a, ((0, -a.shape[0] % 8), (0, width - a.shape[1])))


def kernel(x, norm_g, ffn_w_in, ffn_w_out, conv_w_pw1, conv_b_pw1, conv_w_dw, conv_b_dw, conv_norm_g, conv_w_pw2, conv_b_pw2, attn_w_qkv, attn_q_norm, attn_k_norm, attn_w_o, loss_target, m_norm_g, m_ffn_w_in, m_ffn_w_out, m_conv_w_pw1, m_conv_b_pw1, m_conv_w_dw, m_conv_b_dw, m_conv_norm_g, m_conv_w_pw2, m_conv_b_pw2, m_attn_w_qkv, m_attn_q_norm, m_attn_k_norm, m_attn_w_o, v_norm_g, v_ffn_w_in, v_ffn_w_out, v_conv_w_pw1, v_conv_b_pw1, v_conv_w_dw, v_conv_b_dw, v_conv_norm_g, v_conv_w_pw2, v_conv_b_pw2, v_attn_w_qkv, v_attn_q_norm, v_attn_k_norm, v_attn_w_o):
    w = dict(zip(_NAMES, (norm_g, ffn_w_in, ffn_w_out, conv_w_pw1, conv_b_pw1, conv_w_dw, conv_b_dw, conv_norm_g, conv_w_pw2,
                          conv_b_pw2, attn_w_qkv, attn_q_norm, attn_k_norm, attn_w_o)))
    m = dict(zip(_NAMES, (m_norm_g, m_ffn_w_in, m_ffn_w_out, m_conv_w_pw1, m_conv_b_pw1, m_conv_w_dw, m_conv_b_dw, m_conv_norm_g,
                          m_conv_w_pw2, m_conv_b_pw2, m_attn_w_qkv, m_attn_q_norm, m_attn_k_norm, m_attn_w_o)))
    v = dict(zip(_NAMES, (v_norm_g, v_ffn_w_in, v_ffn_w_out, v_conv_w_pw1, v_conv_b_pw1, v_conv_w_dw, v_conv_b_dw, v_conv_norm_g,
                          v_conv_w_pw2, v_conv_b_pw2, v_attn_w_qkv, v_attn_q_norm, v_attn_k_norm, v_attn_w_o)))
    T, D = x.shape[1:]
    F = ffn_w_out.shape[2] * N_CHIPS
    HE = attn_w_o.shape[1] * N_CHIPS
    cx, cy, cc = lax.axis_index("x"), lax.axis_index("y"), lax.axis_index("c")
    me = 2 * cx + cy
    scal = jnp.stack([cc, me]).astype(jnp.int32)

    ffn_in = lambda lf: _Sharded(f"ffn_w_in_{lf}", (2, D // 2, 2 * F), 0, 2, "ffn_w_in", lf)
    ffn_out = lambda lf: _Sharded(f"ffn_w_out_{lf}", (4, F // 4, D), 1, 0, "ffn_w_out", lf)
    items = [
        ffn_in(0), ffn_out(0),
        _Sharded("conv_w_pw1", (2, D // 2, 2 * D), 0, 2, "conv_w_pw1"), _Sharded("conv_w_pw2", (4, D // 4, D), 1, 0, "conv_w_pw2"),
        ffn_in(1), ffn_out(1), ffn_in(2), ffn_out(2),
        _Sharded("attn_w_qkv", (2, D // 2, 9 * HE), 0, 2, "attn_w_qkv"), _Sharded("attn_w_o", (4, HE // 4, D), 1, 0, "attn_w_o"),
        ffn_in(3), ffn_out(3),
    ]
    mat_shapes = {"ffn_w_in": (D, 2 * F), "ffn_w_out": (F, D), "conv_w_pw1": (D, 2 * D), "conv_w_pw2": (D, D),
                  "attn_w_qkv": (D, 9 * HE), "attn_w_o": (HE, D)}

    def as_shards(a, n):
        it = next(i for i in items if i.src == n)
        return a.reshape(((4,) if it.slab is not None else ()) + it.shard3)

    norm_full, dw_full = _gather_small([norm_g.reshape(6, D // 4), conv_w_dw.reshape(CONV_WIDTH, D // 4)])
    place = lambda its, after: [_cast_place(it, as_shards(w[it.src], it.src), scal, after) for it in its]
    ssems, rsems, fulls = _gather_start(items[:2], place(items[:2], ()), "gather_start_first")
    more = _gather_start(items[2:], place(items[2:], fulls[:1]), "gather_start_rest")
    ssems, rsems, fulls = [list(a) + list(b) for a, b in zip((ssems, rsems, fulls), more)]

    def block_weights(k, after):
        sel = slice(2 * k, 2 * k + 2)
        got, fsems, gsems = _gather_forward(items[sel], fulls[sel], ssems[sel], rsems[sel], after, f"gather_forward_{k}")
        done = _gather_finish(items[sel], got, fsems, gsems, f"gather_finish_{k}")
        return [a.reshape(mat_shapes[it.src]) for a, it in zip(done, items[sel])]

    res = {}
    flight = []

    def advance(entry, k, dx):
        stage, its, st = entry
        n = len(its)
        if stage == 1:
            ssem, rsem, arrs, g32 = st
            got = _exchange_wait(f"grads_pair_wait_{k}", arrs, ssem, rsem, _pair_copies(its), dx)[n:]
            sums = [_pair_add(it, g, r, scal) for it, g, r in zip(its, g32, got)]
            land = [lax.empty((3,) + it.part3, BF16) for it in its]
            ssem, rsem, arrs, tok = _exchange_start(f"grads_chip_start_{k}", [p[1] for p in sums] + land, [3] * n, _chip_copies(its))
            return (2, its, (ssem, rsem, arrs, [p[0] for p in sums])), tok
        if stage == 2:
            ssem, rsem, arrs, p32 = st
            got = _exchange_wait(f"grads_chip_wait_{k}", arrs, ssem, rsem, _chip_copies(its), dx)[n:]
            red = [_chip_reduce(it, p, r, scal) for it, p, r in zip(its, p32, got)]
            ssem, rsem, arrs, tok = _exchange_start(f"grads_fill_start_{k}", red, [1] * n, _fill_copies(its))
            return (3, its, (ssem, rsem, arrs)), tok
        ssem, rsem, arrs = st
        for it, g in zip(its, _exchange_wait(f"grads_fill_wait_{k}", arrs, ssem, rsem, _fill_copies(its), dx)):
            nm = it.src
            res[nm] = _adam(it, g, as_shards(w[nm], nm), as_shards(m[nm], nm), as_shards(v[nm], nm), prev=res.get(nm))
        return None, None

    def step_flight(dx):
        toks, left = [], []
        for k, entry in flight:
            entry, tok = advance(entry, k, dx)
            if entry is not None:
                left.append((k, entry))
                toks.append(tok)
        flight[:] = left
        return toks

    def grads_ready(k, pairs, dx):
        its = items[2 * k:2 * k + 2]
        toks = step_flight(dx)
        g32 = [p[0].reshape(it.full3) for p, it in zip(pairs, its)]
        g16 = [p[1].reshape(it.full3) for p, it in zip(pairs, its)]
        land = [lax.empty(it.pair3, BF16) for it in its]
        ssem, rsem, arrs, tok = _exchange_start(f"grads_pair_start_{k}", g16 + land, [1] * len(its), _pair_copies(its))
        flight.append((k, (1, its, (ssem, rsem, arrs, g32))))
        return toks + [tok]

    sq, dx, grads = _local_step(x[0], loss_target[0], norm_full.reshape(2, 3, D), conv_b_pw1, dw_full, conv_b_dw, conv_norm_g,
                                conv_b_pw2, attn_q_norm[0], attn_k_norm[0], block_weights, grads_ready)
    loss = lax.psum(0.5 * jnp.sum(sq) / D, ("x", "y", "c"))
    while flight:
        step_flight(dx)

    out_g, out_d, out_m, out_v = {}, {}, {}, {}
    for n in _BIG:
        out_g[n], out_d[n], out_m[n], out_v[n] = (a.reshape(w[n].shape) for a in res[n])

    parts = [_rows8(grads[n], D) for n in _SMALL]
    tot = _allreduce_small(jnp.concatenate(parts, axis=0))
    sg, r0 = {}, 0
    for n, p in zip(_SMALL, parts):
        last = grads[n].shape[-1]
        g = tot[r0:r0 + grads[n].size // min(last, D), :min(last, D)].reshape(-1, last)
        r0 += p.shape[0]
        if n in ("norm_g", "conv_w_dw"):
            g = lax.dynamic_slice_in_dim(g, me * (D // 4), D // 4, axis=1)
        sg[n] = g
    flat = lambda a: a.reshape(-1, a.shape[-1])
    ds, ms, vs = _adam_small([sg[n] for n in _SMALL], [flat(w[n]) for n in _SMALL], [flat(m[n]) for n in _SMALL],
                             [flat(v[n]) for n in _SMALL])
    for i, n in enumerate(_SMALL):
        out_g[n], out_d[n], out_m[n], out_v[n] = (a.reshape(w[n].shape) for a in (sg[n], ds[i], ms[i], vs[i]))

    return (loss, dx[None], *[out_g[n] for n in _NAMES], *[out_d[n] for n in _NAMES], *[out_m[n] for n in _NAMES],
            *[out_v[n] for n in _NAMES])
```

```python
import functools

import jax
import jax.numpy as jnp
from jax import lax
from jax.experimental import pallas as pl
from jax.experimental.pallas import tpu as pltpu

F32 = jnp.float32
BF16 = jnp.bfloat16
MESH = pl.DeviceIdType.MESH

NORM_EPS = 1e-6
CONV_WIDTH = 31
ATTN_GROUPS = ((128, 1), (512, 4), (2048, 16))
ATTN_BLOCK = 128
HEAD_DIM = 128
N_CHIPS = 4

ADAM_LR = 0.001
ADAM_B1 = 0.9
ADAM_B2 = 0.999
ADAM_EPS = 1e-08
ADAM_WD = 0.01
ADAM_STEP = 10

VMEM_LIMIT = 56 * 1024 * 1024
NT_DIMS = (((1,), (1,)), ((), ()))
TN_DIMS = (((0,), (0,)), ((), ()))


def _pick(n, pref, mult):
    t = (min(n, pref) // mult) * mult
    while t >= mult:
        if n % t == 0:
            return t
        t -= mult
    return n


def _call(body, name, grid, in_specs, out_specs, out_shape, scratch=(), aliases=None, prefetch=0, after=()):
    params = pltpu.CompilerParams(dimension_semantics=("arbitrary",) * len(grid), vmem_limit_bytes=VMEM_LIMIT)
    after = tuple(after)
    if after:
        inner, n_in = body, prefetch + len(in_specs)

        def body(*refs):
            return inner(*refs[:n_in], *refs[n_in + len(after):])

        in_specs = list(in_specs) + [pl.BlockSpec(memory_space=pl.ANY)] * len(after)
    if prefetch:
        spec = pltpu.PrefetchScalarGridSpec(
            num_scalar_prefetch=prefetch, grid=grid, in_specs=in_specs, out_specs=out_specs, scratch_shapes=list(scratch)
        )
        call = pl.pallas_call(body, name=name, grid_spec=spec, out_shape=out_shape, compiler_params=params,
                              input_output_aliases=aliases or {})
    else:
        call = pl.pallas_call(body, name=name, grid=grid, in_specs=in_specs, out_specs=out_specs, out_shape=out_shape,
                              scratch_shapes=list(scratch), compiler_params=params, input_output_aliases=aliases or {})
    return lambda *args: call(*args, *after)


def _sds(shape, dtype):
    return jax.ShapeDtypeStruct(shape, dtype)


def _sig(x):
    return 1.0 / (1.0 + jnp.exp(-x))


def _rstd(x):
    return lax.rsqrt(jnp.mean(x * x, axis=-1, keepdims=True) + NORM_EPS)


def _norm_bwd(dy, xhat, r, g):
    dxh = dy * g
    return r * (dxh - xhat * jnp.mean(dxh * xhat, axis=-1, keepdims=True))


def _dot(a, b):
    return jnp.dot(a, b, preferred_element_type=F32)


def _dot_nt(a, b):
    return lax.dot_general(a, b, NT_DIMS, preferred_element_type=F32)


def _dot_tn(a, b):
    return lax.dot_general(a, b, TN_DIMS, preferred_element_type=F32)


def _ffn_fwd(x, g, w_in, w_out, name):
    T, D = x.shape
    F = w_out.shape[0]
    tm = _pick(T, 256, 8)

    def body(x_ref, g_ref, wg_ref, wu_ref, wo_ref, xo_ref, gate_ref, up_ref):
        xv = x_ref[...]
        h = (xv * _rstd(xv) * g_ref[...]).astype(BF16)
        gate = _dot(h, wg_ref[...])
        up = _dot(h, wu_ref[...])
        gate_ref[...] = gate.astype(BF16)
        up_ref[...] = up.astype(BF16)
        a = (gate * _sig(gate) * up).astype(BF16)
        xo_ref[...] = xv + 0.5 * _dot(a, wo_ref[...])

    row = lambda i: (i, 0)
    held = lambda shape, k: pl.BlockSpec(shape, lambda i: (0, k), pipeline_mode=pl.Buffered(1))
    return _call(
        body, name, (T // tm,),
        [pl.BlockSpec((tm, D), row), pl.BlockSpec((1, D), lambda i: (0, 0)), held((D, F), 0), held((D, F), 1), held((F, D), 0)],
        [pl.BlockSpec((tm, D), row), pl.BlockSpec((tm, F), row), pl.BlockSpec((tm, F), row)],
        [_sds((T, D), F32), _sds((T, F), BF16), _sds((T, F), BF16)],
    )(x, g, w_in, w_in, w_out)


def _ffn_bwd(x, g, dy, gate, up, w_in, w_out, name, after=()):
    T, D = x.shape
    F = w_out.shape[0]

    def body_a(dy_ref, gate_ref, up_ref, wo_ref, dgate_ref, dup_ref, a_ref):
        dyb = (0.5 * dy_ref[...]).astype(BF16)
        gate = gate_ref[...].astype(F32)
        up = up_ref[...].astype(F32)
        sg = _sig(gate)
        sl = gate * sg
        a_ref[...] = (sl * up).astype(BF16)
        da = _dot_nt(dyb, wo_ref[...])
        dgate_ref[...] = (da * up * (sg * (1.0 + gate * (1.0 - sg)))).astype(BF16)
        dup_ref[...] = (da * sl).astype(BF16)

    ta = _pick(T, 256, 8)
    tile = pl.BlockSpec((ta, F), lambda i: (i, 0))
    dgate, dup, a = _call(
        body_a, name + "_hidden", (T // ta,),
        [pl.BlockSpec((ta, D), lambda i: (i, 0)), tile, tile,
         pl.BlockSpec((F, D), lambda i: (0, 0), pipeline_mode=pl.Buffered(1))],
        [tile, tile, tile],
        [_sds((T, F), BF16)] * 3, after=after,
    )(dy, gate, up, w_out)

    def body_b(x_ref, g_ref, dy_ref, dgate_ref, dup_ref, wg_ref, wu_ref, dx_ref, dg_ref, h_ref, dyb_ref):
        @pl.when(pl.program_id(0) == 0)
        def _():
            dg_ref[...] = jnp.zeros_like(dg_ref)

        xv = x_ref[...]
        r = _rstd(xv)
        xh = xv * r
        h_ref[...] = (xh * g_ref[...]).astype(BF16)
        dyb_ref[...] = (0.5 * dy_ref[...]).astype(BF16)
        dh = _dot_nt(dgate_ref[...], wg_ref[...]) + _dot_nt(dup_ref[...], wu_ref[...])
        dx_ref[...] = dy_ref[...] + _norm_bwd(dh, xh, r, g_ref[...])
        dg_ref[...] += jnp.sum(dh * xh, axis=0, keepdims=True)

    tb = _pick(T, 512, 8)
    row = lambda i: (i, 0)
    held = lambda k: pl.BlockSpec((D, F), lambda i: (0, k), pipeline_mode=pl.Buffered(1))
    dx, dg, h, dyb = _call(
        body_b, name, (T // tb,),
        [pl.BlockSpec((tb, D), row), pl.BlockSpec((1, D), lambda i: (0, 0)), pl.BlockSpec((tb, D), row),
         pl.BlockSpec((tb, F), row), pl.BlockSpec((tb, F), row), held(0), held(1)],
        [pl.BlockSpec((tb, D), row), pl.BlockSpec((1, D), lambda i: (0, 0)), pl.BlockSpec((tb, D), row), pl.BlockSpec((tb, D), row)],
        [_sds((T, D), F32), _sds((1, D), F32), _sds((T, D), BF16), _sds((T, D), BF16)],
    )(x, g, dy, dgate, dup, w_in, w_in)
    return dx, dg, dgate, dup, a, h, dyb


def _mm_tn(a, b, bm, bn, out_shape, out_block, out_map, scale, name, prev=None):
    K, M = a.shape
    N = b.shape[1]

    def body(a_ref, b_ref, *rest):
        o_ref, ob_ref = rest[-2:]
        o = _dot_tn(a_ref[...].astype(BF16), b_ref[...].astype(BF16)) * scale
        o_ref[...] = o
        ob_ref[...] = o.astype(BF16)

    in_specs = [pl.BlockSpec((K, bm), lambda mi, ni: (0, mi)), pl.BlockSpec((K, bn), lambda mi, ni: (0, ni))]
    args = [a, b]
    aliases = None
    if prev is not None:
        in_specs += [pl.BlockSpec(memory_space=pl.ANY)] * 2
        args += list(prev)
        aliases = {2: 0, 3: 1}
    ospec = pl.BlockSpec(out_block, out_map)
    return _call(body, name, (M // bm, N // bn), in_specs, [ospec, ospec],
                 [_sds(out_shape, F32), _sds(out_shape, BF16)], aliases=aliases)(*args)


def _conv_pre(x, g, w1, b1, name):
    T, D = x.shape
    tm = _pick(T, 512, 8)

    def body(x_ref, g_ref, w_ref, b_ref, ag_ref, u_ref, h_ref):
        xv = x_ref[...]
        h = (xv * _rstd(xv) * g_ref[...]).astype(BF16)
        h_ref[...] = h
        ag = _dot(h, w_ref[...]) + b_ref[...]
        ag_ref[...] = ag.astype(BF16)
        u_ref[...] = ag[:, :D] * _sig(ag[:, D:])

    return _call(
        body, name, (T // tm,),
        [pl.BlockSpec((tm, D), lambda i: (i, 0)), pl.BlockSpec((1, D), lambda i: (0, 0)),
         pl.BlockSpec((D, 2 * D), lambda i: (0, 0)), pl.BlockSpec((1, 2 * D), lambda i: (0, 0))],
        [pl.BlockSpec((tm, 2 * D), lambda i: (i, 0)), pl.BlockSpec((tm, D), lambda i: (i, 0)),
         pl.BlockSpec((tm, D), lambda i: (i, 0))],
        [_sds((T, 2 * D), BF16), _sds((T, D), F32), _sds((T, D), BF16)],
    )(x, g, w1, b1)


_DW_PAD = 32
_DW_CHUNK = 256


def _dwconv(u, w, b, name):
    T, D = u.shape
    K = w.shape[0]
    ch = _pick(T, _DW_CHUNK, 8)
    lead = _DW_PAD - (K - 1)

    def body(u_ref, w_ref, b_ref, c_ref, ext):
        ext[pl.ds(0, _DW_PAD), :] = jnp.zeros((_DW_PAD, 128), F32)
        ext[pl.ds(_DW_PAD, T), :] = u_ref[...]
        for c0 in range(0, T, ch):
            acc = jnp.zeros((ch, 128), F32) + b_ref[...]
            for k in range(K):
                acc = acc + w_ref[pl.ds(k, 1), :] * ext[pl.ds(c0 + lead + k, ch), :]
            c_ref[pl.ds(c0, ch), :] = acc

    return _call(
        body, name, (D // 128,),
        [pl.BlockSpec((T, 128), lambda i: (0, i)), pl.BlockSpec((K, 128), lambda i: (0, i)),
         pl.BlockSpec((1, 128), lambda i: (0, i))],
        [pl.BlockSpec((T, 128), lambda i: (0, i))],
        [_sds((T, D), F32)],
        scratch=[pltpu.VMEM((T + _DW_PAD, 128), F32)],
    )(u, w, b)[0]


def _dwconv_bwd(dc, u, w, name):
    T, D = u.shape
    K = w.shape[0]
    ch = _pick(T, _DW_CHUNK, 8)
    lead = _DW_PAD - (K - 1)

    def body(dc_ref, u_ref, w_ref, du_ref, dw_ref, db_ref, uext, dext):
        uext[pl.ds(0, _DW_PAD), :] = jnp.zeros((_DW_PAD, 128), F32)
        uext[pl.ds(_DW_PAD, T), :] = u_ref[...]
        dext[pl.ds(0, T), :] = dc_ref[...]
        dext[pl.ds(T, _DW_PAD), :] = jnp.zeros((_DW_PAD, 128), F32)
        dws =[jnp.zeros((8, 128), F32) for _ in range(K)]
        dbs = jnp.zeros((8, 128), F32)
        for c0 in range(0, T, ch):
            dcv = dext[pl.ds(c0, ch), :]
            dbs = dbs + jnp.sum(dcv.reshape(ch // 8, 8, 128), axis=0)
            acc = jnp.zeros((ch, 128), F32)
            for k in range(K):
                acc = acc + w_ref[pl.ds(k, 1), :] * dext[pl.ds(c0 + (K - 1) - k, ch), :]
                prod = dcv * uext[pl.ds(c0 + lead + k, ch), :]
                dws[k] = dws[k] + jnp.sum(prod.reshape(ch // 8, 8, 128), axis=0)
            du_ref[pl.ds(c0, ch), :] = acc
        for k in range(K):
            dw_ref[pl.ds(k, 1), :] = jnp.sum(dws[k], axis=0, keepdims=True)
        db_ref[...] = jnp.sum(dbs, axis=0, keepdims=True)

    return _call(
        body, name, (D // 128,),
        [pl.BlockSpec((T, 128), lambda i: (0, i)), pl.BlockSpec((T, 128), lambda i: (0, i)),
         pl.BlockSpec((K, 128), lambda i: (0, i))],
        [pl.BlockSpec((T, 128), lambda i: (0, i)), pl.BlockSpec((K, 128), lambda i: (0, i)),
         pl.BlockSpec((1, 128), lambda i: (0, i))],
        [_sds((T, D), F32), _sds((K, D), F32), _sds((1, D), F32)],
        scratch=[pltpu.VMEM((T + _DW_PAD, 128), F32), pltpu.VMEM((T + _DW_PAD, 128), F32)],
    )(dc, u, w)


def _conv_post(c, x, ng, w2, b2, name):
    T, D = x.shape
    tm = _pick(T, 512, 8)

    def body(c_ref, x_ref, ng_ref, w_ref, b_ref, xo_ref, s_ref):
        cv = c_ref[...]
        n = cv * _rstd(cv) * ng_ref[...]
        s = (n * _sig(n)).astype(BF16)
        s_ref[...] = s
        xo_ref[...] = x_ref[...] + _dot(s, w_ref[...]) + b_ref[...]

    row = lambda i: (i, 0)
    fix = lambda i: (0, 0)
    return _call(
        body, name, (T // tm,),
        [pl.BlockSpec((tm, D), row), pl.BlockSpec((tm, D), row), pl.BlockSpec((1, D), fix),
         pl.BlockSpec((D, D), fix), pl.BlockSpec((1, D), fix)],
        [pl.BlockSpec((tm, D), row), pl.BlockSpec((tm, D), row)],
        [_sds((T, D), F32), _sds((T, D), BF16)],
    )(c, x, ng, w2, b2)


def _conv_post_bwd(dy, c, ng, w2, name, after=()):
    T, D = dy.shape
    tm = _pick(T, 512, 8)

    def body(dy_ref, c_ref, ng_ref, w_ref, dc_ref, dng_ref, db_ref):
        @pl.when(pl.program_id(0) == 0)
        def _():
            dng_ref[...] = jnp.zeros_like(dng_ref)
            db_ref[...] = jnp.zeros_like(db_ref)

        dyv = dy_ref[...]
        ds = _dot_nt(dyv.astype(BF16), w_ref[...])
        cv = c_ref[...]
        r = _rstd(cv)
        ch = cv * r
        n = ch * ng_ref[...]
        sg = _sig(n)
        dn = ds * (sg * (1.0 + n * (1.0 - sg)))
        dc_ref[...] = _norm_bwd(dn, ch, r, ng_ref[...])
        dng_ref[...] += jnp.sum(dn * ch, axis=0, keepdims=True)
        db_ref[...] += jnp.sum(dyv, axis=0, keepdims=True)

    row = lambda i: (i, 0)
    fix = lambda i: (0, 0)
    return _call(
        body, name, (T // tm,),
        [pl.BlockSpec((tm, D), row), pl.BlockSpec((tm, D), row), pl.BlockSpec((1, D), fix), pl.BlockSpec((D, D), fix)],
        [pl.BlockSpec((tm, D), row), pl.BlockSpec((1, D), fix), pl.BlockSpec((1, D), fix)],
        [_sds((T, D), F32), _sds((1, D), F32), _sds((1, D), F32)], after=after,
    )(dy, c, ng, w2)


def _conv_pre_bwd(du, ag, x, g, dy, w1, name):
    T, D = x.shape
    tm = _pick(T, 512, 8)

    def body(du_ref, ag_ref, x_ref, g_ref, dy_ref, w_ref, dx_ref, dg_ref, dag_ref, db_ref):
        @pl.when(pl.program_id(0) == 0)
        def _():
            dg_ref[...] = jnp.zeros_like(dg_ref)
            db_ref[...] = jnp.zeros_like(db_ref)

        duv = du_ref[...]
        a = ag_ref[:, :D].astype(F32)
        gt = ag_ref[:, D:].astype(F32)
        sg = _sig(gt)
        da = duv * sg
        dgt = duv * a * sg * (1.0 - sg)
        db_ref[:, :D] += jnp.sum(da, axis=0, keepdims=True)
        db_ref[:, D:] += jnp.sum(dgt, axis=0, keepdims=True)
        dab = da.astype(BF16)
        dgb = dgt.astype(BF16)
        dag_ref[:, :D] = dab
        dag_ref[:, D:] = dgb
        dh = _dot_nt(dab, w_ref[:, :D]) + _dot_nt(dgb, w_ref[:, D:])
        xv = x_ref[...]
        r = _rstd(xv)
        xh = xv * r
        dx_ref[...] = dy_ref[...] + _norm_bwd(dh, xh, r, g_ref[...])
        dg_ref[...] += jnp.sum(dh * xh, axis=0, keepdims=True)

    row = lambda i: (i, 0)
    fix = lambda i: (0, 0)
    return _call(
        body, name, (T // tm,),
        [pl.BlockSpec((tm, D), row), pl.BlockSpec((tm, 2 * D), row), pl.BlockSpec((tm, D), row), pl.BlockSpec((1, D), fix),
         pl.BlockSpec((tm, D), row), pl.BlockSpec((D, 2 * D), fix)],
        [pl.BlockSpec((tm, D), row), pl.BlockSpec((1, D), fix), pl.BlockSpec((tm, 2 * D), row),
         pl.BlockSpec((1, 2 * D), fix)],
        [_sds((T, D), F32), _sds((1, D), F32), _sds((T, 2 * D), BF16), _sds((1, 2 * D), F32)],
    )(du, ag, x, g, dy, w1)


def _row_sums(a):
    return _dot(a.astype(BF16), jnp.ones((a.shape[1], 128), BF16))


def _head_rstd(x):
    return lax.rsqrt(_row_sums(x * x) * (1.0 / x.shape[1]) + NORM_EPS)


def _attn_qkv(x, g, wqkv, qn, kn, name):
    T, D = x.shape
    N = wqkv.shape[1]
    tn = N // 9
    E = HEAD_DIM
    tm = _pick(T, 1024, 8)

    def body(x_ref, g_ref, w_ref, qn_ref, kn_ref, o_ref, a_ref, h_ref):
        j = pl.program_id(1)

        @pl.when(j == 0)
        def _():
            xv = x_ref[...]
            h_ref[...] = (xv * _rstd(xv) * g_ref[...]).astype(BF16)

        res = _dot(h_ref[...], w_ref[...])
        o_ref[...] = res.astype(BF16)
        part = j % 3

        @pl.when(part == 2)
        def _():
            a_ref[...] = res.astype(BF16)

        @pl.when(part < 2)
        def _():
            grp = j // 3
            c = qn_ref[pl.ds(grp, 1), :] * kn_ref[pl.ds(grp, 1), :] * (E ** -0.5)
            fac = jnp.where(part == 0, c, jnp.ones_like(c))
            for h in range(tn // E):
                hs = slice(h * E, (h + 1) * E)
                xh = res[:, hs]
                a_ref[:, hs] = (xh * _head_rstd(xh) * fac).astype(BF16)

    ng = qn.shape[0]
    return _call(
        body, name, (T // tm, 9),
        [pl.BlockSpec((tm, D), lambda i, j: (i, 0)), pl.BlockSpec((1, D), lambda i, j: (0, 0)),
         pl.BlockSpec((D, tn), lambda i, j: (0, j)), pl.BlockSpec((ng, E), lambda i, j: (0, 0)),
         pl.BlockSpec((ng, E), lambda i, j: (0, 0))],
        [pl.BlockSpec((tm, tn), lambda i, j: (i, j)), pl.BlockSpec((tm, tn), lambda i, j: (i, j)),
         pl.BlockSpec((tm, D), lambda i, j: (i, 0))],
        [_sds((T, N), BF16), _sds((T, N), BF16), _sds((T, D), BF16)],
    )(x, g, wqkv, qn, kn)


def _band_mask(q, steps, nblk):
    i = lax.broadcasted_iota(jnp.int32, (q, 2 * q), 0)
    j = lax.broadcasted_iota(jnp.int32, (q, 2 * q), 1)
    diff = q + i - j
    first_key = jnp.where(nblk > 0, 0, q)
    return (diff >= 0) & (diff <= steps) & (j >= first_key)


def _per_row(blk, width):
    e = blk.shape[1]
    if width % e == 0:
        return jnp.concatenate([blk] * (width // e), axis=1)
    return jnp.broadcast_to(blk[:, :1], (blk.shape[0], width))


def _streams(a, dil, to_streams, name, col0=0, ncols=None):
    T, C = a.shape
    ncols = C if ncols is None else ncols
    Q = ATTN_BLOCK
    run = Q * dil
    reps = max(1, min(2048 // run, T // run))
    while T % (run * reps):
        reps -= 1
    rows = run * reps
    cw = _pick(ncols, 512, 128)
    ns = cw // 128

    def body(a_ref, o_ref, scr):
        for s in range(ns):
            ls = slice(s * 128, (s + 1) * 128)
            slab = scr.at[s]
            if to_streams:
                slab[...] = a_ref[:, ls].astype(F32)
                for u in range(reps):
                    for r in range(dil):
                        o_ref[pl.ds(u * run + r * Q, Q), ls] = slab[pl.ds(u * run + r, Q, stride=dil), :].astype(a.dtype)
            else:
                for u in range(reps):
                    for r in range(dil):
                        slab[pl.ds(u * run + r, Q, stride=dil), :] = a_ref[pl.ds(u * run + r * Q, Q), ls].astype(F32)
                o_ref[:, ls] = slab[...].astype(a.dtype)

    return _call(
        body, name, (T // rows, ncols // cw),
        [pl.BlockSpec((rows, cw), lambda i, j: (i, col0 // cw + j))],
        [pl.BlockSpec((rows, cw), lambda i, j: (i, j))],
        [_sds((T, ncols), a.dtype)],
        scratch=[pltpu.VMEM((ns, rows, 128), F32)],
    )(a)[0]


def _attn_fwd(qkv, base, HE, window, dil, name):
    T = qkv.shape[0]
    H = HE // HEAD_DIM
    E = HEAD_DIM
    Q = ATTN_BLOCK
    nb = T // dil // Q
    steps = window // dil

    def body(q_ref, kc_ref, kp_ref, vc_ref, vp_ref, o_ref, l_ref):
        n = pl.program_id(1)
        valid = _band_mask(Q, steps, n)
        ones = jnp.ones((2 * Q, E), BF16)
        outs, lses = [], []
        for h in range(H):
            hs = slice(h * E, (h + 1) * E)
            k2 = jnp.concatenate([kp_ref[:, hs], kc_ref[:, hs]], axis=0)
            v2 = jnp.concatenate([vp_ref[:, hs], vc_ref[:, hs]], axis=0)
            s = jnp.where(valid, _dot_nt(q_ref[:, hs], k2), -1e30)
            m = jnp.max(s, axis=-1, keepdims=True)
            p = jnp.exp(s - m).astype(BF16)
            acc = _dot(p, jnp.concatenate([v2, ones], axis=1))
            l = acc[:, E:]
            outs.append((acc[:, :E] * (1.0 / l)).astype(o_ref.dtype))
            lses.append(m + jnp.log(l))
        o_ref[...] = jnp.concatenate(outs, axis=1)
        l_ref[...] = jnp.concatenate(lses, axis=1)

    blk = lambda s, back: pl.BlockSpec((Q, HE), lambda r, n: (jnp.maximum(n - back, 0) * dil + r, base + s))
    out = pl.BlockSpec((Q, HE), lambda r, n: (n * dil + r, 0))
    return _call(
        body, name, (dil, nb),
        [blk(0, 0), blk(1, 0), blk(1, 1), blk(2, 0), blk(2, 1)],
        [out, out],
        [_sds((T, HE), BF16), _sds((T, HE), F32)],
    )(qkv, qkv, qkv, qkv, qkv)


def _attn_merge(os, lses, x, wo, name):
    T, D = x.shape
    HE = wo.shape[0]
    tm = _pick(T, 512, 8)
    ng = len(os)

    def body(*refs):
        o_refs = refs[:ng]
        l_refs = refs[ng:2 * ng]
        x_ref, w_ref, xo_ref, om_ref, lt_ref = refs[2 * ng:]
        ls = [r[...] for r in l_refs]
        m = functools.reduce(jnp.maximum, ls)
        es = [jnp.exp(l - m) for l in ls]
        tot = functools.reduce(lambda a, b: a + b, es)
        inv = 1.0 / tot
        om = functools.reduce(lambda a, b: a + b, [e * inv * r[...] for e, r in zip(es, o_refs)])
        omb = om.astype(BF16)
        om_ref[...] = omb
        lt_ref[...] = m + jnp.log(tot)
        xo_ref[...] = x_ref[...] + _dot(omb, w_ref[...])

    row = lambda i: (i, 0)
    fix = lambda i: (0, 0)
    return _call(
        body, name, (T // tm,),
        [pl.BlockSpec((tm, HE), row)] * (2 * ng) + [pl.BlockSpec((tm, D), row), pl.BlockSpec((HE, D), fix)],
        [pl.BlockSpec((tm, D), row), pl.BlockSpec((tm, HE), row), pl.BlockSpec((tm, HE), row)],
        [_sds((T, D), F32), _sds((T, HE), BF16), _sds((T, HE), F32)],
    )(*os, *lses, x, wo)


def _attn_out_bwd(dy, om, wo, name, after=()):
    T, D = dy.shape
    HE = wo.shape[0]
    E = HEAD_DIM
    tm = _pick(T, 512, 8)

    def body(dy_ref, om_ref, w_ref, dom_ref, dl_ref):
        dom = _dot_nt(dy_ref[...].astype(BF16), w_ref[...])
        dom_ref[...] = dom.astype(BF16)
        prod = dom * om_ref[...].astype(F32)
        for h in range(HE // E):
            hs = slice(h * E, (h + 1) * E)
            dl_ref[:, hs] = jnp.broadcast_to(jnp.sum(prod[:, hs], axis=-1, keepdims=True), (tm, E))

    row = lambda i: (i, 0)
    return _call(
        body, name, (T // tm,),
        [pl.BlockSpec((tm, D), row), pl.BlockSpec((tm, HE), row), pl.BlockSpec((HE, D), lambda i: (0, 0))],
        [pl.BlockSpec((tm, HE), row), pl.BlockSpec((tm, HE), row)],
        [_sds((T, HE), BF16), _sds((T, HE), F32)], after=after,
    )(dy, om, wo)


def _attn_bwd(qkv, base, HE, dom, lse, delta, window, dil, name):
    T = qkv.shape[0]
    H = HE // HEAD_DIM
    E = HEAD_DIM
    Q = ATTN_BLOCK
    nb = T // dil // Q
    steps = window // dil

    def body(q_ref, kc_ref, kp_ref, vc_ref, vp_ref, do_ref, l_ref, dl_ref, dq_ref, dk_ref, dv_ref, ck_sc, cv_sc):
        n = pl.program_id(1)

        @pl.when(n == 0)
        def _():
            ck_sc[...] = jnp.zeros_like(ck_sc)
            cv_sc[...] = jnp.zeros_like(cv_sc)

        @pl.when(n < nb)
        def _():
            valid = _band_mask(Q, steps, n)
            ck_old = ck_sc[...]
            cv_old = cv_sc[...]
            dqs, dks, dvs = [], [], []
            for h in range(H):
                hs = slice(h * E, (h + 1) * E)
                q = q_ref[:, hs]
                do = do_ref[:, hs]
                k2 = jnp.concatenate([kp_ref[:, hs], kc_ref[:, hs]], axis=0)
                v2 = jnp.concatenate([vp_ref[:, hs], vc_ref[:, hs]], axis=0)
                p = jnp.where(valid, jnp.exp(_dot_nt(q, k2) - _per_row(l_ref[:, hs], 2 * Q)), 0.0)
                ds = (p * (_dot_nt(do, v2) - _per_row(dl_ref[:, hs], 2 * Q))).astype(BF16)
                dqs.append(_dot(ds, k2).astype(BF16))
                dks.append(_dot_tn(ds, q))
                dvs.append(_dot_tn(p.astype(BF16), do))
            cat = lambda parts: jnp.concatenate(parts, axis=1)
            dq_ref[...] = cat(dqs)
            dk_ref[...] = (ck_old + cat([d[:Q] for d in dks])).astype(BF16)
            dv_ref[...] = (cv_old + cat([d[:Q] for d in dvs])).astype(BF16)
            ck_sc[...] = cat([d[Q:] for d in dks])
            cv_sc[...] = cat([d[Q:] for d in dvs])

        @pl.when(n == nb)
        def _():
            dk_ref[...] = ck_sc[...].astype(BF16)
            dv_ref[...] = cv_sc[...].astype(BF16)

    nq = lambda n: jnp.minimum(n, nb - 1)
    blk = lambda s, back: pl.BlockSpec((Q, HE), lambda r, n: (jnp.maximum(nq(n) - back, 0) * dil + r, base + s))
    qblk = pl.BlockSpec((Q, HE), lambda r, n: (nq(n) * dil + r, 0))
    kblk = pl.BlockSpec((Q, HE), lambda r, n: (jnp.maximum(n - 1, 0) * dil + r, 0))
    return _call(
        body, name, (dil, nb + 1),
        [blk(0, 0), blk(1, 0), blk(1, 1), blk(2, 0), blk(2, 1), qblk, qblk, qblk],
        [qblk, kblk, kblk],
        [_sds((T, HE), BF16)] * 3,
        scratch=[pltpu.VMEM((Q, HE), F32), pltpu.VMEM((Q, HE), F32)],
    )(qkv, qkv, qkv, qkv, qkv, dom, lse, delta)


def _qk_norm_bwd(dq, dk, qkv, base, HE, gq, gk, name):
    T = dq.shape[0]
    E = HEAD_DIM
    tm = _pick(T, 512, 8)
    scale = E ** -0.5

    def body(dq_ref, dk_ref, q_ref, k_ref, gq_ref, gk_ref, oq_ref, ok_ref, dgq_ref, dgk_ref):
        @pl.when(pl.program_id(0) == 0)
        def _():
            dgq_ref[...] = jnp.zeros_like(dgq_ref)
            dgk_ref[...] = jnp.zeros_like(dgk_ref)

        gqv = gq_ref[...]
        gkv = gk_ref[...]
        c = gqv * gkv * scale
        dc = jnp.zeros((1, E), F32)
        for h in range(HE // E):
            hs = slice(h * E, (h + 1) * E)
            q = q_ref[:, hs].astype(F32)
            rq = _head_rstd(q)
            qh = q * rq
            a = dq_ref[:, hs].astype(F32)
            dc = dc + jnp.sum(a * qh, axis=0, keepdims=True)
            dqh = a * c
            oq_ref[:, hs] = (rq * (dqh - qh * (_row_sums(dqh * qh) * (1.0 / E)))).astype(BF16)
            k = k_ref[:, hs].astype(F32)
            rk = _head_rstd(k)
            kh = k * rk
            b = dk_ref[:, hs].astype(F32)
            ok_ref[:, hs] = (rk * (b - kh * (_row_sums(b * kh) * (1.0 / E)))).astype(BF16)
        dgq_ref[...] += dc * (gkv * scale)
        dgk_ref[...] += dc * (gqv * scale)

    row = lambda i: (i, 0)
    vec = pl.BlockSpec((1, E), lambda i: (0, 0))
    return _call(
        body, name, (T // tm,),
        [pl.BlockSpec((tm, HE), row), pl.BlockSpec((tm, HE), row), pl.BlockSpec((tm, HE), lambda i: (i, base)),
         pl.BlockSpec((tm, HE), lambda i: (i, base + 1)), vec, vec],
        [pl.BlockSpec((tm, HE), row), pl.BlockSpec((tm, HE), row), vec, vec],
        [_sds((T, HE), BF16), _sds((T, HE), BF16), _sds((1, E), F32), _sds((1, E), F32)],
    )(dq, dk, qkv, qkv, gq, gk)


def _attn_qkv_bwd(dqkv, x, g, dy, wqkv, name):
    T, D = x.shape
    HE = wqkv.shape[1] // 9
    tm = _pick(T, 256, 8)

    def body(*refs):
        d_refs = refs[:9]
        x_ref, g_ref, dy_ref, w_ref, dx_ref, dg_ref = refs[9:]

        @pl.when(pl.program_id(0) == 0)
        def _():
            dg_ref[...] = jnp.zeros_like(dg_ref)

        dh = _dot_nt(d_refs[0][...], w_ref[:, :HE])
        for s in range(1, 9):
            dh = dh + _dot_nt(d_refs[s][...], w_ref[:, s * HE:(s + 1) * HE])
        xv = x_ref[...]
        r = _rstd(xv)
        xh = xv * r
        dx_ref[...] = dy_ref[...] + _norm_bwd(dh, xh, r, g_ref[...])
        dg_ref[...] += jnp.sum(dh * xh, axis=0, keepdims=True)

    row = lambda i: (i, 0)
    fix = lambda i: (0, 0)
    return _call(
        body, name, (T // tm,),
        [pl.BlockSpec((tm, HE), row)] * 9 + [pl.BlockSpec((tm, D), row), pl.BlockSpec((1, D), fix), pl.BlockSpec((tm, D), row),
                                           pl.BlockSpec((D, 9 * HE), fix, pipeline_mode=pl.Buffered(1))],
        [pl.BlockSpec((tm, D), row), pl.BlockSpec((1, D), fix)],
        [_sds((T, D), F32), _sds((1, D), F32)],
    )(*dqkv, x, g, dy, wqkv)


def _loss_head(y, target, name):
    T, D = y.shape
    tm = _pick(T, 512, 8)

    def body(y_ref, t_ref, dy_ref, sq_ref):
        @pl.when(pl.program_id(0) == 0)
        def _():
            sq_ref[...] = jnp.zeros_like(sq_ref)

        err = y_ref[...] - t_ref[...]
        dy_ref[...] = err * (1.0 / D)
        sq_ref[...] += jnp.sum(err * err, axis=0, keepdims=True)

    row = lambda i: (i, 0)
    return _call(
        body, name, (T // tm,),
        [pl.BlockSpec((tm, D), row), pl.BlockSpec((tm, D), row)],
        [pl.BlockSpec((tm, D), row), pl.BlockSpec((1, D), lambda i: (0, 0))],
        [_sds((T, D), F32), _sds((1, D), F32)],
    )(y, target)


def _local_step(x, target, norm_g, b_pw1, w_dw, b_dw, cng, b_pw2, qn, kn, block_weights, grads_ready):
    T, D = x.shape
    ng = lambda l, k: norm_g[l, k][None, :]
    bn = _pick(D, 256, 128)

    w_in, w_out = [None] * 4, [None] * 4
    w_in[0], w_out[0] = block_weights(0, x)
    x1, *gu0 = _ffn_fwd(x, ng(0, 0), w_in[0], w_out[0], "ffn_fwd_0")
    pw1, pw2 = block_weights(1, x1)
    ag, u, hc = _conv_pre(x1, ng(0, 1), pw1, b_pw1, "conv_pre")
    c = _dwconv(u, w_dw, b_dw, "dwconv")
    x2, s = _conv_post(c, x1, cng, pw2, b_pw2, "conv_post")
    w_in[1], w_out[1] = block_weights(2, x2)
    x3, *gu1 = _ffn_fwd(x2, ng(0, 2), w_in[1], w_out[1], "ffn_fwd_1")
    w_in[2], w_out[2] = block_weights(3, x3)
    x4, *gu2 = _ffn_fwd(x3, ng(1, 0), w_in[2], w_out[2], "ffn_fwd_2")
    wqkv, wo = block_weights(4, x4)
    HE = wo.shape[0]
    F = w_out[0].shape[0]
    bf = _pick(F, 256, 128)
    bh = _pick(HE, 512, 128)
    qkv, att, ha = _attn_qkv(x4, ng(1, 1), wqkv, qn, kn, "attn_qkv")
    qkv_s = [(att, 3 * gi) if dil == 1 else (_streams(att, dil, True, f"qkv_streams_{gi}", 3 * gi * HE, 3 * HE), 0)
             for gi, (_, dil) in enumerate(ATTN_GROUPS)]
    tokens = lambda a, dil, name: a if dil == 1 else _streams(a, dil, False, name)
    streams = lambda a, dil, name: a if dil == 1 else _streams(a, dil, True, name)
    os, lses = [], []
    for gi, (window, dil) in enumerate(ATTN_GROUPS):
        o, l = _attn_fwd(*qkv_s[gi], HE, window, dil, f"attn_fwd_{gi}")
        os.append(tokens(o, dil, f"o_tokens_{gi}"))
        lses.append(tokens(l, dil, f"lse_tokens_{gi}"))
    x5, om, lse = _attn_merge(os, lses, x4, wo, "attn_merge")
    w_in[3], w_out[3] = block_weights(5, x5)
    x6, *gu3 = _ffn_fwd(x5, ng(1, 2), w_in[3], w_out[3], "ffn_fwd_3")
    dy, sq = _loss_head(x6, target, "loss_head")

    grads = {"ffn_w_in": [None] * 4, "ffn_w_out": [None] * 4}
    dnorm = [[None] * 3 for _ in range(2)]

    def ffn_back(lf, k, xin, gvec, dy, gu, after):
        dx, dg, dgate, dup, a, h, dyb = _ffn_bwd(xin, gvec, dy, gu[0], gu[1], w_in[lf], w_out[lf], f"ffn_bwd_{lf}", after=after)
        d_in = _mm_tn(h, dgate, D, bf, (D, 2 * F), (D, bf), lambda mi, ni: (0, ni), 1.0, f"ffn_dw_gate_{lf}")
        grads["ffn_w_in"][lf] = _mm_tn(h, dup, D, bf, (D, 2 * F), (D, bf), lambda mi, ni: (0, F // bf + ni), 1.0,
                                       f"ffn_dw_up_{lf}", prev=d_in)
        grads["ffn_w_out"][lf] = _mm_tn(a, dyb, bf, D, (F, D), (bf, D), lambda mi, ni: (mi, 0), 1.0, f"ffn_dw_out_{lf}")
        return dx, dg, grads_ready(k, (grads["ffn_w_in"][lf], grads["ffn_w_out"][lf]), dx)

    dx, dnorm[1][2], tok = ffn_back(3, 5, x5, ng(1, 2), dy, gu3, ())
    dom, delta = _attn_out_bwd(dx, om, wo, "attn_out_bwd", after=tok)
    grads["attn_w_o"] = _mm_tn(om, dx, HE, bn, (HE, D), (HE, bn), lambda mi, ni: (0, ni), 1.0, "attn_dw_o")
    dqkv, dgq, dgk = [], [], []
    for gi, (window, dil) in enumerate(ATTN_GROUPS):
        ds = _attn_bwd(*qkv_s[gi], HE, streams(dom, dil, f"dom_streams_{gi}"), streams(lse, dil, f"lse_streams_{gi}"),
                       streams(delta, dil, f"delta_streams_{gi}"), window, dil, f"attn_bwd_{gi}")
        dq, dk, dv = [tokens(d, dil, f"d{nm}_tokens_{gi}") for d, nm in zip(ds, "qkv")]
        dq, dk, a_, b_ = _qk_norm_bwd(dq, dk, qkv, 3 * gi, HE, qn[gi][None, :], kn[gi][None, :], f"qk_norm_bwd_{gi}")
        dqkv += [dq, dk, dv]
        dgq.append(a_)
        dgk.append(b_)
    grads["attn_q_norm"] = jnp.concatenate(dgq, axis=0)
    grads["attn_k_norm"] = jnp.concatenate(dgk, axis=0)
    d_qkv = None
    for s9 in range(9):
        d_qkv = _mm_tn(ha, dqkv[s9], D, bh, (D, 9 * HE), (D, bh), lambda mi, ni, s9=s9: (0, s9 * (HE // bh) + ni), 1.0,
                       f"attn_dw_qkv_{s9}", prev=d_qkv)
    grads["attn_w_qkv"] = d_qkv
    dx, dnorm[1][1] = _attn_qkv_bwd(dqkv, x4, ng(1, 1), dx, wqkv, "attn_qkv_bwd")
    tok = grads_ready(4, (grads["attn_w_qkv"], grads["attn_w_o"]), dx)
    dx, dnorm[1][0], tok = ffn_back(2, 3, x3, ng(1, 0), dx, gu2, tok)

    dx, dnorm[0][2], tok = ffn_back(1, 2, x2, ng(0, 2), dx, gu1, tok)
    dc, grads["conv_norm_g"], grads["conv_b_pw2"] = _conv_post_bwd(dx, c, cng, pw2, "conv_post_bwd", after=tok)
    grads["conv_w_pw2"] = _mm_tn(s, dx, D, bn, (D, D), (D, bn), lambda mi, ni: (0, ni), 1.0, "conv_dw_pw2")
    du, grads["conv_w_dw"], grads["conv_b_dw"] = _dwconv_bwd(dc, u, w_dw, "dwconv_bwd")
    dx, dnorm[0][1], dag, grads["conv_b_pw1"] = _conv_pre_bwd(du, ag, x1, ng(0, 1), dx, pw1, "conv_pre_bwd")
    grads["conv_w_pw1"] = _mm_tn(hc, dag, D, 2 * bn, (D, 2 * D), (D, 2 * bn), lambda mi, ni: (0, ni), 1.0, "conv_dw_pw1")
    tok = grads_ready(1, (grads["conv_w_pw1"], grads["conv_w_pw2"]), dx)
    dx, dnorm[0][0], _ = ffn_back(0, 0, x, ng(0, 0), dx, gu0, tok)

    grads["norm_g"] = jnp.concatenate([jnp.concatenate(r, axis=0)[None] for r in dnorm], axis=0)
    return sq, dx, grads


class _Sharded:
    def __init__(self, name, full3, half_axis, shard_axis, src, slab=None):
        self.name, self.full3, self.half_axis, self.shard_axis = name, tuple(full3), half_axis, shard_axis
        self.src, self.slab = src, slab

    def source(self, refs):
        return refs[self.src] if self.slab is None else refs[self.src].at[self.slab]

    def _cut(self, shape, axis, parts):
        s = list(shape)
        s[axis] //= parts
        return tuple(s)

    @property
    def shard3(self):
        return self._cut(self.full3, self.shard_axis, N_CHIPS)

    @property
    def pair3(self):
        return self._cut(self.full3, self.half_axis, 2)

    @property
    def part3(self):
        return self._cut(self.shard3, self.half_axis, 2)

    @staticmethod
    def _slice(ref, axis, idx, parts):
        n = ref.shape[axis] // parts
        start = idx * n
        minor = len(ref.shape) - 1 - axis
        if minor < 2 and not isinstance(start, int):
            start = pl.multiple_of(start, 128 if minor == 0 else (16 if n % 16 == 0 else 8))
        sl = [slice(None)] * len(ref.shape)
        sl[axis] = pl.ds(start, n)
        return ref.at[tuple(sl)]

    def half(self, ref, h):
        return self._slice(ref, self.half_axis, h, 2)

    def shard(self, ref, j):
        return self._slice(ref, self.shard_axis, j, N_CHIPS)


def _place():
    x, y, c = lax.axis_index("x"), lax.axis_index("y"), lax.axis_index("c")
    return x, y, c, 2 * x + y


_RELS = (1, 2, 3)


def _peer(x, y, rel):
    px = 1 - x if rel & 2 else x
    py = 1 - y if rel & 1 else y
    return px, py, 2 * px + py


ANY = pl.BlockSpec(memory_space=pl.ANY)


def _comm_call(body, name, n_in, out_shape, n_sems, aliases=None):
    return pl.pallas_call(
        body, name=name, in_specs=[ANY] * n_in, out_specs=[ANY] * len(out_shape), out_shape=out_shape,
        scratch_shapes=[pltpu.SemaphoreType.DMA((n,)) for n in n_sems],
        input_output_aliases=aliases or {},
        compiler_params=pltpu.CompilerParams(has_side_effects=True),
    )


def _remote(src, dst, send_sem, recv_sem, dev):
    return pltpu.make_async_remote_copy(src_ref=src, dst_ref=dst, send_sem=send_sem, recv_sem=recv_sem, device_id=dev,
                                        device_id_type=MESH)


def _gather_small(small_shards):
    ns = len(small_shards)

    def body(*refs):
        ins, outs = refs[:ns], refs[ns:2 * ns]
        lsem, ssem, rsem = refs[2 * ns:]
        x, y, c, me = _place()
        cols = lambda ref, j: _Sharded._slice(ref, 1, j, N_CHIPS)
        local = [pltpu.make_async_copy(ins[i], cols(outs[i], me), lsem.at[i]) for i in range(ns)]
        sends = []
        for i in range(ns):
            for k, rel in enumerate(_RELS):
                px, py, _ = _peer(x, y, rel)
                sends.append(_remote(ins[i], cols(outs[i], me), ssem.at[3 * i + k], rsem.at[3 * i + k], (px, py, c)))
        for cp in local + sends:
            cp.start()
        for i in range(ns):
            for k, rel in enumerate(_RELS):
                _, _, pj = _peer(x, y, rel)
                got = cols(outs[i], pj)
                _remote(got, got, ssem.at[3 * i + k], rsem.at[3 * i + k], (x, y, c)).wait_recv()
        for cp in sends:
            cp.wait_send()
        for cp in local:
            cp.wait()

    out_shape = [_sds((s.shape[0], s.shape[1] * N_CHIPS), F32) for s in small_shards]
    return _comm_call(body, "gather_small", ns, out_shape, [ns, 3 * ns, 3 * ns])(*small_shards)


HBM = pl.BlockSpec(memory_space=pltpu.HBM)
SEM = pl.BlockSpec(memory_space=pltpu.SEMAPHORE)
DATAFLOW = pltpu.SideEffectType.DATAFLOW_SIDE_EFFECTING


def _in_hbm(a):
    return pltpu.with_memory_space_constraint(a, pltpu.HBM)


def _cast_place(it, shard, scal, after=()):
    a_n, r_n, c_n = it.shard3
    tr = _pick(r_n, 256, 16)
    sa = it.shard_axis

    def body(sc_ref, s_ref, o_ref):
        o_ref[...] = s_ref[...].astype(BF16)

    if it.slab is None:
        src = pl.BlockSpec((1, tr, c_n), lambda a, rb, sc: (a, rb, 0))
    else:
        src = pl.BlockSpec((None, 1, tr, c_n), lambda a, rb, sc: (it.slab, a, rb, 0))
    dst = pl.BlockSpec((1, tr, c_n), lambda a, rb, sc: (a + sc[1] * (a_n if sa == 0 else 0), rb + sc[1] * (r_n // tr if sa == 1 else 0),
                                                       sc[1] if sa == 2 else 0))
    return _call(body, f"cast_place_{it.name}", (a_n, r_n // tr), [src], [dst], [_sds(it.full3, BF16)], prefetch=1,
                 after=after)(scal, shard)[0]


def _gather_start(items, fulls, name):
    ni = len(items)

    def body(*refs):
        outs = refs[ni:]
        ssem, rsem, full = outs[:ni], outs[ni:2 * ni], outs[2 * ni:3 * ni]
        x, y, c, me = _place()
        for i, it in enumerate(items):
            mine = it.half(it.shard(full[i], me), c)
            for k, rel in enumerate(_RELS):
                px, py, _ = _peer(x, y, rel)
                _remote(mine, mine, ssem[i].at[k], rsem[i].at[k], (px, py, c)).start()

    outs = pl.pallas_call(
        body, name=name, in_specs=[HBM] * ni, out_specs=[SEM] * (2 * ni) + [HBM] * ni,
        out_shape=[pltpu.SemaphoreType.DMA((3,))] * (2 * ni) + [pltpu.HBM(it.full3, BF16) for it in items],
        input_output_aliases={j: 2 * ni + j for j in range(ni)},
        compiler_params=pltpu.CompilerParams(has_side_effects=DATAFLOW),
    )(*[_in_hbm(f) for f in fulls])
    return outs[:ni], outs[ni:2 * ni], outs[2 * ni:]


def _gather_forward(items, fulls, ssems, rsems, after, name):
    ni = len(items)

    def body(*refs):
        ssem, rsem = refs[ni:2 * ni], refs[2 * ni:3 * ni]
        outs = refs[3 * ni + 1:]
        full, fsem, gsem = outs[:ni], outs[ni:2 * ni], outs[2 * ni:3 * ni]
        x, y, c, me = _place()
        sib = (x, y, 1 - c)
        for i, it in enumerate(items):
            for k, rel in enumerate(_RELS):
                _, _, pj = _peer(x, y, rel)
                got = it.half(it.shard(full[i], pj), c)
                _remote(got, got, ssem[i].at[k], rsem[i].at[k], sib).wait_recv()
                _remote(got, got, fsem[i].at[k], gsem[i].at[k], sib).start()
        for i, it in enumerate(items):
            mine = it.half(it.shard(full[i], me), c)
            for k in range(3):
                _remote(mine, mine, ssem[i].at[k], rsem[i].at[k], sib).wait_send()

    outs = pl.pallas_call(
        body, name=name, in_specs=[HBM] * ni + [SEM] * (2 * ni) + [ANY],
        out_specs=[HBM] * ni + [SEM] * (2 * ni),
        out_shape=[pltpu.HBM(it.full3, BF16) for it in items] + [pltpu.SemaphoreType.DMA((3,))] * (2 * ni),
        input_output_aliases={i: i for i in range(ni)},
        compiler_params=pltpu.CompilerParams(has_side_effects=DATAFLOW),
    )(*fulls, *ssems, *rsems, after)
    return outs[:ni], outs[ni:2 * ni], outs[2 * ni:]


def _gather_finish(items, fulls, fsems, gsems, name):
    ni = len(items)

    def body(*refs):
        fsem, gsem = refs[ni:2 * ni], refs[2 * ni:3 * ni]
        full = refs[3 * ni:]
        x, y, c, _ = _place()
        sib = (x, y, 1 - c)
        for i, it in enumerate(items):
            for k, rel in enumerate(_RELS):
                _, _, pj = _peer(x, y, rel)
                got = it.half(it.shard(full[i], pj), 1 - c)
                _remote(got, got, fsem[i].at[k], gsem[i].at[k], sib).wait_recv()
                sent = it.half(it.shard(full[i], pj), c)
                _remote(sent, sent, fsem[i].at[k], gsem[i].at[k], sib).wait_send()

    return pl.pallas_call(
        body, name=name, in_specs=[HBM] * ni + [SEM] * (2 * ni), out_specs=[HBM] * ni,
        out_shape=[pltpu.HBM(it.full3, BF16) for it in items],
        input_output_aliases={i: i for i in range(ni)},
        compiler_params=pltpu.CompilerParams(has_side_effects=DATAFLOW),
    )(*fulls, *fsems, *gsems)


def _exchange_start(name, arrays, n_sems, copies):
    na, ns = len(arrays), len(n_sems)

    def body(*refs):
        outs = refs[na:]
        ssem, rsem, thru, token = outs[:ns], outs[ns:2 * ns], outs[2 * ns:2 * ns + na], outs[-1]
        for send, _ in copies(thru, ssem, rsem):
            send.start()
        token[...] = jnp.zeros_like(token)

    outs = pl.pallas_call(
        body, name=name, in_specs=[HBM] * na,
        out_specs=[SEM] * (2 * ns) + [HBM] * na + [pl.BlockSpec(memory_space=pltpu.VMEM)],
        out_shape=[pltpu.SemaphoreType.DMA((n,)) for n in n_sems] * 2 + [pltpu.HBM(a.shape, a.dtype) for a in arrays]
        + [_sds((8, 128), F32)],
        input_output_aliases={j: 2 * ns + j for j in range(na)},
        compiler_params=pltpu.CompilerParams(has_side_effects=DATAFLOW),
    )(*[_in_hbm(a) for a in arrays])
    return outs[:ns], outs[ns:2 * ns], outs[2 * ns:2 * ns + na], outs[-1]


def _exchange_wait(name, arrays, ssems, rsems, copies, after):
    na, ns = len(arrays), len(ssems)

    def body(*refs):
        ssem, rsem = refs[na:na + ns], refs[na + ns:na + 2 * ns]
        thru = refs[na + 2 * ns + 1:]
        for send, recv in copies(thru, ssem, rsem):
            recv.wait_recv()
            send.wait_send()

    return pl.pallas_call(
        body, name=name, in_specs=[HBM] * na + [SEM] * (2 * ns) + [ANY], out_specs=[HBM] * na,
        out_shape=[pltpu.HBM(a.shape, a.dtype) for a in arrays],
        input_output_aliases={i: i for i in range(na)},
        compiler_params=pltpu.CompilerParams(has_side_effects=DATAFLOW),
    )(*arrays, *ssems, *rsems, after)


def _pair_copies(items):
    n = len(items)

    def copies(a, ssem, rsem):
        x, y, c, _ = _place()
        cps = [_remote(it.half(a[i], 1 - c), a[n + i], ssem[i].at[0], rsem[i].at[0], (x, y, 1 - c)) for i, it in enumerate(items)]
        return [(cp, cp) for cp in cps]

    return copies


def _chip_copies(items):
    n = len(items)

    def copies(a, ssem, rsem):
        x, y, c, _ = _place()
        cps = []
        for i, it in enumerate(items):
            for k, rel in enumerate(_RELS):
                px, py, pj = _peer(x, y, rel)
                cps.append(_remote(it.shard(a[i], pj), a[n + i].at[k], ssem[i].at[k], rsem[i].at[k], (px, py, c)))
        return [(cp, cp) for cp in cps]

    return copies


def _fill_copies(items):
    def copies(a, ssem, rsem):
        x, y, c, _ = _place()
        sib = (x, y, 1 - c)
        out = []
        for i, it in enumerate(items):
            mine, other = it.half(a[i], c), it.half(a[i], 1 - c)
            out.append((_remote(mine, mine, ssem[i].at[0], rsem[i].at[0], sib), _remote(other, other, ssem[i].at[0], rsem[i].at[0], sib)))
        return out

    return copies


def _ew_tiles(d):
    _, rows, cols = d.part3
    return _pick(rows, 256, 16), cols


def _pair_add(d, g_full, got, scal):
    tr, tc = _ew_tiles(d)
    a_n, r_n, c_n = d.pair3
    ha = d.half_axis

    def body(sc_ref, g_ref, r_ref, o_ref, ob_ref):
        s = g_ref[...] + r_ref[...].astype(F32)
        o_ref[...] = s
        ob_ref[...] = s.astype(BF16)

    blk = (1, tr, tc)
    same = pl.BlockSpec(blk, lambda a, rb, cb, sc: (a, rb, cb))
    mine = pl.BlockSpec(blk, lambda a, rb, cb, sc: (a + sc[0] * (a_n if ha == 0 else 0), rb + sc[0] * (r_n // tr if ha == 1 else 0), cb))
    return _call(body, f"pair_add_{d.name}", (a_n, r_n // tr, c_n // tc), [mine, same], [same, same],
                 [_sds(d.pair3, F32), _sds(d.pair3, BF16)], prefetch=1)(scal, g_full, got)


def _chip_reduce(d, pair_f32, got, scal):
    tr, tc = _ew_tiles(d)
    a_n, r_n, c_n = d.part3
    ha, sa = d.half_axis, d.shard_axis

    def body(sc_ref, p_ref, r0, r1, r2, o_ref):
        o_ref[...] = ((p_ref[...] + r0[...].astype(F32)) + r1[...].astype(F32)) + r2[...].astype(F32)

    blk = (1, tr, tc)
    own = pl.BlockSpec(blk, lambda a, rb, sc: (a + sc[1] * (a_n if sa == 0 else 0), rb + sc[1] * (r_n // tr if sa == 1 else 0),
                                               sc[1] if sa == 2 else 0))
    slot = lambda k: pl.BlockSpec((None,) + blk, lambda a, rb, sc: (k, a, rb, 0))
    out = pl.BlockSpec(blk, lambda a, rb, sc: (a + sc[0] * (a_n if ha == 0 else 0), rb + sc[0] * (r_n // tr if ha == 1 else 0), 0))
    return _call(body, f"chip_reduce_{d.name}", (a_n, r_n // tr), [own, slot(0), slot(1), slot(2)], [out],
                 [_sds(d.shard3, F32)], prefetch=1)(scal, pair_f32, got, got, got)[0]


def _adam_math(g, w, m, v):
    m = ADAM_B1 * m + (1.0 - ADAM_B1) * g
    v = ADAM_B2 * v + (1.0 - ADAM_B2) * (g * g)
    m_hat = m / (1.0 - ADAM_B1 ** ADAM_STEP)
    v_hat = v / (1.0 - ADAM_B2 ** ADAM_STEP)
    delta = -ADAM_LR * (m_hat / (jnp.sqrt(v_hat) + ADAM_EPS) + ADAM_WD * w)
    return delta, m, v


def _adam(d, g, w, m, v, prev=None):
    tr, tc = _ew_tiles(d)
    a_n, r_n, c_n = d.shard3
    n_prev = 0 if prev is None else 4

    def body(g_ref, w_ref, m_ref, v_ref, *rest):
        go_ref, d_ref, mo_ref, vo_ref = rest[n_prev:]
        gv = g_ref[...]
        go_ref[...] = gv
        d_ref[...], mo_ref[...], vo_ref[...] = _adam_math(gv, w_ref[...], m_ref[...], v_ref[...])

    plain = pl.BlockSpec((1, tr, tc), lambda a, rb: (a, rb, 0))
    if d.slab is None:
        wspec, shape = plain, d.shard3
    else:
        wspec, shape = pl.BlockSpec((None, 1, tr, tc), lambda a, rb: (d.slab, a, rb, 0)), (4,) + d.shard3
    in_specs = [plain] + [wspec] * 3
    args = [g, w, m, v]
    aliases = None
    if prev is not None:
        in_specs += [pl.BlockSpec(memory_space=pl.ANY)] * 4
        args += list(prev)
        aliases = {4 + k: k for k in range(4)}
    return _call(body, f"adam_{d.name}", (a_n, r_n // tr), in_specs, [wspec] * 4, [_sds(shape, F32)] * 4, aliases=aliases)(*args)


def _adam_small(gs, ws, ms, vs):
    n = len(gs)

    def body(*refs):
        for i in range(n):
            g, w, m, v = (refs[k * n + i][...] for k in range(4))
            d, mo, vo = _adam_math(g, w, m, v)
            refs[4 * n + i][...] = d
            refs[5 * n + i][...] = mo
            refs[6 * n + i][...] = vo

    vm = pl.BlockSpec(memory_space=pltpu.VMEM)
    outs = pl.pallas_call(body, name="adam_small", in_specs=[vm] * (4 * n), out_specs=[vm] * (3 * n),
                          out_shape=[_sds(g.shape, F32) for g in gs] * 3)(*gs, *ws, *ms, *vs)
    return outs[:n], outs[n:2 * n], outs[2 * n:]


def _allreduce_small(packed):
    rows, cols = packed.shape
    others = [(dx, dy, dc) for dx in (0, 1) for dy in (0, 1) for dc in (0, 1) if (dx, dy, dc) != (0, 0, 0)]

    def body(in_ref, out_ref, buf, ssem, rsem):
        x, y, c, _ = _place()
        lin = 4 * x + 2 * y + c
        buf[lin] = in_ref[...]
        cps = []
        for k, (dx, dy, dc) in enumerate(others):
            px = 1 - x if dx else x
            py = 1 - y if dy else y
            pc = 1 - c if dc else c
            cps.append((pltpu.make_async_remote_copy(src_ref=in_ref, dst_ref=buf.at[lin], send_sem=ssem.at[k], recv_sem=rsem.at[k],
                                                     device_id=(px, py, pc), device_id_type=MESH), 4 * px + 2 * py + pc))
        for cp, _ in cps:
            cp.start()
        for k, (cp, plin) in enumerate(cps):
            pltpu.make_async_remote_copy(src_ref=in_ref, dst_ref=buf.at[plin], send_sem=ssem.at[k], recv_sem=rsem.at[k],
                                         device_id=(x, y, c), device_id_type=MESH).wait_recv()
        for cp, _ in cps:
            cp.wait_send()
        acc = buf[0]
        for dev in range(1, 8):
            acc = acc + buf[dev]
        out_ref[...] = acc

    vm = pl.BlockSpec(memory_space=pltpu.VMEM)
    return pl.pallas_call(
        body, name="allreduce_small", in_specs=[vm], out_specs=vm, out_shape=_sds((rows, cols), F32),
        scratch_shapes=[pltpu.VMEM((8, rows, cols), F32), pltpu.SemaphoreType.DMA((7,)), pltpu.SemaphoreType.DMA((7,))],
        compiler_params=pltpu.CompilerParams(has_side_effects=True),
    )(packed)


_BIG = ("ffn_w_in", "ffn_w_out", "conv_w_pw1", "conv_w_pw2", "attn_w_qkv", "attn_w_o")
_SMALL = ("norm_g", "conv_b_pw1", "conv_w_dw", "conv_b_dw", "conv_norm_g", "conv_b_pw2", "attn_q_norm", "attn_k_norm")
_NAMES = ("norm_g", "ffn_w_in", "ffn_w_out", "conv_w_pw1", "conv_b_pw1", "conv_w_dw", "conv_b_dw", "conv_norm_g", "conv_w_pw2",
          "conv_b_pw2", "attn_w_qkv", "attn_q_norm", "attn_k_norm", "attn_w_o")


def _rows8(a, width):
    a = a.reshape(-1, min(a.shape[-1], width))
    return jnp.pad(a, ((0, -a.shape[0] % 8), (0, width - a.shape[1])))


def kernel(x, norm_g, ffn_w_in, ffn_w_out, conv_w_pw1, conv_b_pw1, conv_w_dw, conv_b_dw, conv_norm_g, conv_w_pw2, conv_b_pw2, attn_w_qkv, attn_q_norm, attn_k_norm, attn_w_o, loss_target, m_norm_g, m_ffn_w_in, m_ffn_w_out, m_conv_w_pw1, m_conv_b_pw1, m_conv_w_dw, m_conv_b_dw, m_conv_norm_g, m_conv_w_pw2, m_conv_b_pw2, m_attn_w_qkv, m_attn_q_norm, m_attn_k_norm, m_attn_w_o, v_norm_g, v_ffn_w_in, v_ffn_w_out, v_conv_w_pw1, v_conv_b_pw1, v_conv_w_dw, v_conv_b_dw, v_conv_norm_g, v_conv_w_pw2, v_conv_b_pw2, v_attn_w_qkv, v_attn_q_norm, v_attn_k_norm, v_attn_w_o):
    w = dict(zip(_NAMES, (norm_g, ffn_w_in, ffn_w_out, conv_w_pw1, conv_b_pw1, conv_w_dw, conv_b_dw, conv_norm_g, conv_w_pw2,
                          conv_b_pw2, attn_w_qkv, attn_q_norm, attn_k_norm, attn_w_o)))
    m = dict(zip(_NAMES, (m_norm_g, m_ffn_w_in, m_ffn_w_out, m_conv_w_pw1, m_conv_b_pw1, m_conv_w_dw, m_conv_b_dw, m_conv_norm_g,
                          m_conv_w_pw2, m_conv_b_pw2, m_attn_w_qkv, m_attn_q_norm, m_attn_k_norm, m_attn_w_o)))
    v = dict(zip(_NAMES, (v_norm_g, v_ffn_w_in, v_ffn_w_out, v_conv_w_pw1, v_conv_b_pw1, v_conv_w_dw, v_conv_b_dw, v_conv_norm_g,
                          v_conv_w_pw2, v_conv_b_pw2, v_attn_w_qkv, v_attn_q_norm, v_attn_k_norm, v_attn_w_o)))
    T, D = x.shape[1:]
    F = ffn_w_out.shape[2] * N_CHIPS
    HE = attn_w_o.shape[1] * N_CHIPS
    cx, cy, cc = lax.axis_index("x"), lax.axis_index("y"), lax.axis_index("c")
    me = 2 * cx + cy
    scal = jnp.stack([cc, me]).astype(jnp.int32)

    ffn_in = lambda lf: _Sharded(f"ffn_w_in_{lf}", (2, D // 2, 2 * F), 0, 2, "ffn_w_in", lf)
    ffn_out = lambda lf: _Sharded(f"ffn_w_out_{lf}", (4, F // 4, D), 1, 0, "ffn_w_out", lf)
    items = [
        ffn_in(0), ffn_out(0),
        _Sharded("conv_w_pw1", (2, D // 2, 2 * D), 0, 2, "conv_w_pw1"), _Sharded("conv_w_pw2", (4, D // 4, D), 1, 0, "conv_w_pw2"),
        ffn_in(1), ffn_out(1), ffn_in(2), ffn_out(2),
        _Sharded("attn_w_qkv", (2, D // 2, 9 * HE), 0, 2, "attn_w_qkv"), _Sharded("attn_w_o", (4, HE // 4, D), 1, 0, "attn_w_o"),
        ffn_in(3), ffn_out(3),
    ]
    mat_shapes = {"ffn_w_in": (D, 2 * F), "ffn_w_out": (F, D), "conv_w_pw1": (D, 2 * D), "conv_w_pw2": (D, D),
                  "attn_w_qkv": (D, 9 * HE), "attn_w_o": (HE, D)}

    def as_shards(a, n):
        it = next(i for i in items if i.src == n)
        return a.reshape(((4,) if it.slab is not None else ()) + it.shard3)

    norm_full, dw_full = _gather_small([norm_g.reshape(6, D // 4), conv_w_dw.reshape(CONV_WIDTH, D // 4)])
    place = lambda its, after: [_cast_place(it, as_shards(w[it.src], it.src), scal, after) for it in its]
    ssems, rsems, fulls = _gather_start(items[:2], place(items[:2], ()), "gather_start_first")
    more = _gather_start(items[2:], place(items[2:], fulls[:1]), "gather_start_rest")
    ssems, rsems, fulls = [list(a) + list(b) for a, b in zip((ssems, rsems, fulls), more)]

    def block_weights(k, after):
        sel = slice(2 * k, 2 * k + 2)
        got, fsems, gsems = _gather_forward(items[sel], fulls[sel], ssems[sel], rsems[sel], after, f"gather_forward_{k}")
        done = _gather_finish(items[sel], got, fsems, gsems, f"gather_finish_{k}")
        return [a.reshape(mat_shapes[it.src]) for a, it in zip(done, items[sel])]

    res = {}
    flight = []

    def advance(entry, k, dx):
        stage, its, st = entry
        n = len(its)
        if stage == 1:
            ssem, rsem, arrs, g32 = st
            got = _exchange_wait(f"grads_pair_wait_{k}", arrs, ssem, rsem, _pair_copies(its), dx)[n:]
            sums = [_pair_add(it, g, r, scal) for it, g, r in zip(its, g32, got)]
            land = [lax.empty((3,) + it.part3, BF16) for it in its]
            ssem, rsem, arrs, tok = _exchange_start(f"grads_chip_start_{k}", [p[1] for p in sums] + land, [3] * n, _chip_copies(its))
            return (2, its, (ssem, rsem, arrs, [p[0] for p in sums])), tok
        if stage == 2:
            ssem, rsem, arrs, p32 = st
            got = _exchange_wait(f"grads_chip_wait_{k}", arrs, ssem, rsem, _chip_copies(its), dx)[n:]
            red = [_chip_reduce(it, p, r, scal) for it, p, r in zip(its, p32, got)]
            ssem, rsem, arrs, tok = _exchange_start(f"grads_fill_start_{k}", red, [1] * n, _fill_copies(its))
            return (3, its, (ssem, rsem, arrs)), tok
        ssem, rsem, arrs = st
        for it, g in zip(its, _exchange_wait(f"grads_fill_wait_{k}", arrs, ssem, rsem, _fill_copies(its), dx)):
            nm = it.src
            res[nm] = _adam(it, g, as_shards(w[nm], nm), as_shards(m[nm], nm), as_shards(v[nm], nm), prev=res.get(nm))
        return None, None

    def step_flight(dx):
        toks, left = [], []
        for k, entry in flight:
            entry, tok = advance(entry, k, dx)
            if entry is not None:
                left.append((k, entry))
                toks.append(tok)
        flight[:] = left
        return toks

    def grads_ready(k, pairs, dx):
        its = items[2 * k:2 * k + 2]
        toks = step_flight(dx)
        g32 = [p[0].reshape(it.full3) for p, it in zip(pairs, its)]
        g16 = [p[1].reshape(it.full3) for p, it in zip(pairs, its)]
        land = [lax.empty(it.pair3, BF16) for it in its]
        ssem, rsem, arrs, tok = _exchange_start(f"grads_pair_start_{k}", g16 + land, [1] * len(its), _pair_copies(its))
        flight.append((k, (1, its, (ssem, rsem, arrs, g32))))
        return toks + [tok]

    sq, dx, grads = _local_step(x[0], loss_target[0], norm_full.reshape(2, 3, D), conv_b_pw1, dw_full, conv_b_dw, conv_norm_g,
                                conv_b_pw2, attn_q_norm[0], attn_k_norm[0], block_weights, grads_ready)
    loss = lax.psum(0.5 * jnp.sum(sq) / D, ("x", "y", "c"))
    while flight:
        step_flight(dx)

    out_g, out_d, out_m, out_v = {}, {}, {}, {}
    for n in _BIG:
        out_g[n], out_d[n], out_m[n], out_v[n] = (a.reshape(w[n].shape) for a in res[n])

    parts = [_rows8(grads[n], D) for n in _SMALL]
    tot = _allreduce_small(jnp.concatenate(parts, axis=0))
    sg, r0 = {}, 0
    for n, p in zip(_SMALL, parts):
        last = grads[n].shape[-1]
        g = tot[r0:r0 + grads[n].size // min(last, D), :min(last, D)].reshape(-1, last)
        r0 += p.shape[0]
        if n in ("norm_g", "conv_w_dw"):
            g = lax.dynamic_slice_in_dim(g, me * (D // 4), D // 4, axis=1)
        sg[n] = g
    flat = lambda a: a.reshape(-1, a.shape[-1])
    ds, ms, vs = _adam_small([sg[n] for n in _SMALL], [flat(w[n]) for n in _SMALL], [flat(m[n]) for n in _SMALL],
                             [flat(v[n]) for n in _SMALL])
    for i, n in enumerate(_SMALL):
        out_g[n], out_d[n], out_m[n], out_v[n] = (a.reshape(w[n].shape) for a in (sg[n], ds[i], ms[i], vs[i]))

    return (loss, dx[None], *[out_g[n] for n in _NAMES], *[out_d[n] for n in _NAMES], *[out_m[n] for n in _NAMES],
            *[out_v[n] for n in _NAMES])
```

```python
import functools

import jax
import jax.numpy as jnp
from jax import lax
from jax.experimental import pallas as pl
from jax.experimental.pallas import tpu as pltpu

F32 = jnp.float32
BF16 = jnp.bfloat16
MESH = pl.DeviceIdType.MESH

NORM_EPS = 1e-6
CONV_WIDTH = 31
ATTN_GROUPS = ((128, 1), (512, 4), (2048, 16))
ATTN_BLOCK = 128
HEAD_DIM = 128
N_CHIPS = 4

ADAM_LR = 0.001
ADAM_B1 = 0.9
ADAM_B2 = 0.999
ADAM_EPS = 1e-08
ADAM_WD = 0.01
ADAM_STEP = 10

VMEM_LIMIT = 56 * 1024 * 1024
NT_DIMS = (((1,), (1,)), ((), ()))
TN_DIMS = (((0,), (0,)), ((), ()))


def _pick(n, pref, mult):
    t = (min(n, pref) // mult) * mult
    while t >= mult:
        if n % t == 0:
            return t
        t -= mult
    return n


def _call(body, name, grid, in_specs, out_specs, out_shape, scratch=(), aliases=None, prefetch=0, after=()):
    params = pltpu.CompilerParams(dimension_semantics=("arbitrary",) * len(grid), vmem_limit_bytes=VMEM_LIMIT)
    after = tuple(after)
    if after:
        inner, n_in = body, prefetch + len(in_specs)

        def body(*refs):
            return inner(*refs[:n_in], *refs[n_in + len(after):])

        in_specs = list(in_specs) + [pl.BlockSpec(memory_space=pl.ANY)] * len(after)
    if prefetch:
        spec = pltpu.PrefetchScalarGridSpec(
            num_scalar_prefetch=prefetch, grid=grid, in_specs=in_specs, out_specs=out_specs, scratch_shapes=list(scratch)
        )
        call = pl.pallas_call(body, name=name, grid_spec=spec, out_shape=out_shape, compiler_params=params,
                              input_output_aliases=aliases or {})
    else:
        call = pl.pallas_call(body, name=name, grid=grid, in_specs=in_specs, out_specs=out_specs, out_shape=out_shape,
                              scratch_shapes=list(scratch), compiler_params=params, input_output_aliases=aliases or {})
    return lambda *args: call(*args, *after)


def _sds(shape, dtype):
    return jax.ShapeDtypeStruct(shape, dtype)


def _sig(x):
    return 1.0 / (1.0 + jnp.exp(-x))


def _rstd(x):
    return lax.rsqrt(jnp.mean(x * x, axis=-1, keepdims=True) + NORM_EPS)


def _norm_bwd(dy, xhat, r, g):
    dxh = dy * g
    return r * (dxh - xhat * jnp.mean(dxh * xhat, axis=-1, keepdims=True))


def _dot(a, b):
    return jnp.dot(a, b, preferred_element_type=F32)


def _dot_nt(a, b):
    return lax.dot_general(a, b, NT_DIMS, preferred_element_type=F32)


def _dot_tn(a, b):
    return lax.dot_general(a, b, TN_DIMS, preferred_element_type=F32)


def _ffn_fwd(x, g, w_in, w_out, name):
    T, D = x.shape
    F = w_out.shape[0]
    tm = _pick(T, 256, 8)

    def body(x_ref, g_ref, wg_ref, wu_ref, wo_ref, xo_ref, gate_ref, up_ref):
        xv = x_ref[...]
        h = (xv * _rstd(xv) * g_ref[...]).astype(BF16)
        gate = _dot(h, wg_ref[...])
        up = _dot(h, wu_ref[...])
        gate_ref[...] = gate.astype(BF16)
        up_ref[...] = up.astype(BF16)
        a = (gate * _sig(gate) * up).astype(BF16)
        xo_ref[...] = xv + 0.5 * _dot(a, wo_ref[...])

    row = lambda i: (i, 0)
    held = lambda shape, k: pl.BlockSpec(shape, lambda i: (0, k), pipeline_mode=pl.Buffered(1))
    return _call(
        body, name, (T // tm,),
        [pl.BlockSpec((tm, D), row), pl.BlockSpec((1, D), lambda i: (0, 0)), held((D, F), 0), held((D, F), 1), held((F, D), 0)],
        [pl.BlockSpec((tm, D), row), pl.BlockSpec((tm, F), row), pl.BlockSpec((tm, F), row)],
        [_sds((T, D), F32), _sds((T, F), BF16), _sds((T, F), BF16)],
    )(x, g, w_in, w_in, w_out)


def _ffn_bwd(x, g, dy, gate, up, w_in, w_out, name, after=()):
    T, D = x.shape
    F = w_out.shape[0]

    def body_a(dy_ref, gate_ref, up_ref, wo_ref, dgate_ref, dup_ref, a_ref):
        dyb = (0.5 * dy_ref[...]).astype(BF16)
        gate = gate_ref[...].astype(F32)
        up = up_ref[...].astype(F32)
        sg = _sig(gate)
        sl = gate * sg
        a_ref[...] = (sl * up).astype(BF16)
        da = _dot_nt(dyb, wo_ref[...])
        dgate_ref[...] = (da * up * (sg * (1.0 + gate * (1.0 - sg)))).astype(BF16)
        dup_ref[...] = (da * sl).astype(BF16)

    ta = _pick(T, 256, 8)
    tile = pl.BlockSpec((ta, F), lambda i: (i, 0))
    dgate, dup, a = _call(
        body_a, name + "_hidden", (T // ta,),
        [pl.BlockSpec((ta, D), lambda i: (i, 0)), tile, tile,
         pl.BlockSpec((F, D), lambda i: (0, 0), pipeline_mode=pl.Buffered(1))],
        [tile, tile, tile],
        [_sds((T, F), BF16)] * 3, after=after,
    )(dy, gate, up, w_out)

    def body_b(x_ref, g_ref, dy_ref, dgate_ref, dup_ref, wg_ref, wu_ref, dx_ref, dg_ref, h_ref, dyb_ref):
        @pl.when(pl.program_id(0) == 0)
        def _():
            dg_ref[...] = jnp.zeros_like(dg_ref)

        xv = x_ref[...]
        r = _rstd(xv)
        xh = xv * r
        h_ref[...] = (xh * g_ref[...]).astype(BF16)
        dyb_ref[...] = (0.5 * dy_ref[...]).astype(BF16)
        dh = _dot_nt(dgate_ref[...], wg_ref[...]) + _dot_nt(dup_ref[...], wu_ref[...])
        dx_ref[...] = dy_ref[...] + _norm_bwd(dh, xh, r, g_ref[...])
        dg_ref[...] += jnp.sum(dh * xh, axis=0, keepdims=True)

    tb = _pick(T, 512, 8)
    row = lambda i: (i, 0)
    held = lambda k: pl.BlockSpec((D, F), lambda i: (0, k), pipeline_mode=pl.Buffered(1))
    dx, dg, h, dyb = _call(
        body_b, name, (T // tb,),
        [pl.BlockSpec((tb, D), row), pl.BlockSpec((1, D), lambda i: (0, 0)), pl.BlockSpec((tb, D), row),
         pl.BlockSpec((tb, F), row), pl.BlockSpec((tb, F), row), held(0), held(1)],
        [pl.BlockSpec((tb, D), row), pl.BlockSpec((1, D), lambda i: (0, 0)), pl.BlockSpec((tb, D), row), pl.BlockSpec((tb, D), row)],
        [_sds((T, D), F32), _sds((1, D), F32), _sds((T, D), BF16), _sds((T, D), BF16)],
    )(x, g, dy, dgate, dup, w_in, w_in)
    return dx, dg, dgate, dup, a, h, dyb


def _mm_tn(a, b, bm, bn, out_shape, out_block, out_map, scale, name, prev=None):
    K, M = a.shape
    N = b.shape[1]

    def body(a_ref, b_ref, *rest):
        o_ref, ob_ref = rest[-2:]
        o = _dot_tn(a_ref[...].astype(BF16), b_ref[...].astype(BF16)) * scale
        o_ref[...] = o
        ob_ref[...] = o.astype(BF16)

    in_specs = [pl.BlockSpec((K, bm), lambda mi, ni: (0, mi)), pl.BlockSpec((K, bn), lambda mi, ni: (0, ni))]
    args = [a, b]
    aliases = None
    if prev is not None:
        in_specs += [pl.BlockSpec(memory_space=pl.ANY)] * 2
        args += list(prev)
        aliases = {2: 0, 3: 1}
    ospec = pl.BlockSpec(out_block, out_map)
    return _call(body, name, (M // bm, N // bn), in_specs, [ospec, ospec],
                 [_sds(out_shape, F32), _sds(out_shape, BF16)], aliases=aliases)(*args)


def _conv_pre(x, g, w1, b1, name):
    T, D = x.shape
    tm = _pick(T, 512, 8)

    def body(x_ref, g_ref, w_ref, b_ref, ag_ref, u_ref, h_ref):
        xv = x_ref[...]
        h = (xv * _rstd(xv) * g_ref[...]).astype(BF16)
        h_ref[...] = h
        ag = _dot(h, w_ref[...]) + b_ref[...]
        ag_ref[...] = ag.astype(BF16)
        u_ref[...] = ag[:, :D] * _sig(ag[:, D:])

    return _call(
        body, name, (T // tm,),
        [pl.BlockSpec((tm, D), lambda i: (i, 0)), pl.BlockSpec((1, D), lambda i: (0, 0)),
         pl.BlockSpec((D, 2 * D), lambda i: (0, 0)), pl.BlockSpec((1, 2 * D), lambda i: (0, 0))],
        [pl.BlockSpec((tm, 2 * D), lambda i: (i, 0)), pl.BlockSpec((tm, D), lambda i: (i, 0)),
         pl.BlockSpec((tm, D), lambda i: (i, 0))],
        [_sds((T, 2 * D), BF16), _sds((T, D), F32), _sds((T, D), BF16)],
    )(x, g, w1, b1)


_DW_PAD = 32
_DW_CHUNK = 256


def _dwconv(u, w, b, name):
    T, D = u.shape
    K = w.shape[0]
    ch = _pick(T, _DW_CHUNK, 8)
    lead = _DW_PAD - (K - 1)

    def body(u_ref, w_ref, b_ref, c_ref, ext):
        ext[pl.ds(0, _DW_PAD), :] = jnp.zeros((_DW_PAD, 128), F32)
        ext[pl.ds(_DW_PAD, T), :] = u_ref[...]
        for c0 in range(0, T, ch):
            acc = jnp.zeros((ch, 128), F32) + b_ref[...]
            for k in range(K):
                acc = acc + w_ref[pl.ds(k, 1), :] * ext[pl.ds(c0 + lead + k, ch), :]
            c_ref[pl.ds(c0, ch), :] = acc

    return _call(
        body, name, (D // 128,),
        [pl.BlockSpec((T, 128), lambda i: (0, i)), pl.BlockSpec((K, 128), lambda i: (0, i)),
         pl.BlockSpec((1, 128), lambda i: (0, i))],
        [pl.BlockSpec((T, 128), lambda i: (0, i))],
        [_sds((T, D), F32)],
        scratch=[pltpu.VMEM((T + _DW_PAD, 128), F32)],
    )(u, w, b)[0]


def _dwconv_bwd(dc, u, w, name):
    T, D = u.shape
    K = w.shape[0]
    ch = _pick(T, _DW_CHUNK, 8)
    lead = _DW_PAD - (K - 1)

    def body(dc_ref, u_ref, w_ref, du_ref, dw_ref, db_ref, uext, dext):
        uext[pl.ds(0, _DW_PAD), :] = jnp.zeros((_DW_PAD, 128), F32)
        uext[pl.ds(_DW_PAD, T), :] = u_ref[...]
        dext[pl.ds(0, T), :] = dc_ref[...]
        dext[pl.ds(T, _DW_PAD), :] = jnp.zeros((_DW_PAD, 128), F32)
        dws =[jnp.zeros((8, 128), F32) for _ in range(K)]
        dbs = jnp.zeros((8, 128), F32)
        for c0 in range(0, T, ch):
            dcv = dext[pl.ds(c0, ch), :]
            dbs = dbs + jnp.sum(dcv.reshape(ch // 8, 8, 128), axis=0)
            acc = jnp.zeros((ch, 128), F32)
            for k in range(K):
                acc = acc + w_ref[pl.ds(k, 1), :] * dext[pl.ds(c0 + (K - 1) - k, ch), :]
                prod = dcv * uext[pl.ds(c0 + lead + k, ch), :]
                dws[k] = dws[k] + jnp.sum(prod.reshape(ch // 8, 8, 128), axis=0)
            du_ref[pl.ds(c0, ch), :] = acc
        for k in range(K):
            dw_ref[pl.ds(k, 1), :] = jnp.sum(dws[k], axis=0, keepdims=True)
        db_ref[...] = jnp.sum(dbs, axis=0, keepdims=True)

    return _call(
        body, name, (D // 128,),
        [pl.BlockSpec((T, 128), lambda i: (0, i)), pl.BlockSpec((T, 128), lambda i: (0, i)),
         pl.BlockSpec((K, 128), lambda i: (0, i))],
        [pl.BlockSpec((T, 128), lambda i: (0, i)), pl.BlockSpec((K, 128), lambda i: (0, i)),
         pl.BlockSpec((1, 128), lambda i: (0, i))],
        [_sds((T, D), F32), _sds((K, D), F32), _sds((1, D), F32)],
        scratch=[pltpu.VMEM((T + _DW_PAD, 128), F32), pltpu.VMEM((T + _DW_PAD, 128), F32)],
    )(dc, u, w)


def _conv_post(c, x, ng, w2, b2, name):
    T, D = x.shape
    tm = _pick(T, 512, 8)

    def body(c_ref, x_ref, ng_ref, w_ref, b_ref, xo_ref, s_ref):
        cv = c_ref[...]
        n = cv * _rstd(cv) * ng_ref[...]
        s = (n * _sig(n)).astype(BF16)
        s_ref[...] = s
        xo_ref[...] = x_ref[...] + _dot(s, w_ref[...]) + b_ref[...]

    row = lambda i: (i, 0)
    fix = lambda i: (0, 0)
    return _call(
        body, name, (T // tm,),
        [pl.BlockSpec((tm, D), row), pl.BlockSpec((tm, D), row), pl.BlockSpec((1, D), fix),
         pl.BlockSpec((D, D), fix), pl.BlockSpec((1, D), fix)],
        [pl.BlockSpec((tm, D), row), pl.BlockSpec((tm, D), row)],
        [_sds((T, D), F32), _sds((T, D), BF16)],
    )(c, x, ng, w2, b2)


def _conv_post_bwd(dy, c, ng, w2, name, after=()):
    T, D = dy.shape
    tm = _pick(T, 512, 8)

    def body(dy_ref, c_ref, ng_ref, w_ref, dc_ref, dng_ref, db_ref):
        @pl.when(pl.program_id(0) == 0)
        def _():
            dng_ref[...] = jnp.zeros_like(dng_ref)
            db_ref[...] = jnp.zeros_like(db_ref)

        dyv = dy_ref[...]
        ds = _dot_nt(dyv.astype(BF16), w_ref[...])
        cv = c_ref[...]
        r = _rstd(cv)
        ch = cv * r
        n = ch * ng_ref[...]
        sg = _sig(n)
        dn = ds * (sg * (1.0 + n * (1.0 - sg)))
        dc_ref[...] = _norm_bwd(dn, ch, r, ng_ref[...])
        dng_ref[...] += jnp.sum(dn * ch, axis=0, keepdims=True)
        db_ref[...] += jnp.sum(dyv, axis=0, keepdims=True)

    row = lambda i: (i, 0)
    fix = lambda i: (0, 0)
    return _call(
        body, name, (T // tm,),
        [pl.BlockSpec((tm, D), row), pl.BlockSpec((tm, D), row), pl.BlockSpec((1, D), fix), pl.BlockSpec((D, D), fix)],
        [pl.BlockSpec((tm, D), row), pl.BlockSpec((1, D), fix), pl.BlockSpec((1, D), fix)],
        [_sds((T, D), F32), _sds((1, D), F32), _sds((1, D), F32)], after=after,
    )(dy, c, ng, w2)


def _conv_pre_bwd(du, ag, x, g, dy, w1, name):
    T, D = x.shape
    tm = _pick(T, 512, 8)

    def body(du_ref, ag_ref, x_ref, g_ref, dy_ref, w_ref, dx_ref, dg_ref, dag_ref, db_ref):
        @pl.when(pl.program_id(0) == 0)
        def _():
            dg_ref[...] = jnp.zeros_like(dg_ref)
            db_ref[...] = jnp.zeros_like(db_ref)

        duv = du_ref[...]
        a = ag_ref[:, :D].astype(F32)
        gt = ag_ref[:, D:].astype(F32)
        sg = _sig(gt)
        da = duv * sg
        dgt = duv * a * sg * (1.0 - sg)
        db_ref[:, :D] += jnp.sum(da, axis=0, keepdims=True)
        db_ref[:, D:] += jnp.sum(dgt, axis=0, keepdims=True)
        dab = da.astype(BF16)
        dgb = dgt.astype(BF16)
        dag_ref[:, :D] = dab
        dag_ref[:, D:] = dgb
        dh = _dot_nt(dab, w_ref[:, :D]) + _dot_nt(dgb, w_ref[:, D:])
        xv = x_ref[...]
        r = _rstd(xv)
        xh = xv * r
        dx_ref[...] = dy_ref[...] + _norm_bwd(dh, xh, r, g_ref[...])
        dg_ref[...] += jnp.sum(dh * xh, axis=0, keepdims=True)

    row = lambda i: (i, 0)
    fix = lambda i: (0, 0)
    return _call(
        body, name, (T // tm,),
        [pl.BlockSpec((tm, D), row), pl.BlockSpec((tm, 2 * D), row), pl.BlockSpec((tm, D), row), pl.BlockSpec((1, D), fix),
         pl.BlockSpec((tm, D), row), pl.BlockSpec((D, 2 * D), fix)],
        [pl.BlockSpec((tm, D), row), pl.BlockSpec((1, D), fix), pl.BlockSpec((tm, 2 * D), row),
         pl.BlockSpec((1, 2 * D), fix)],
        [_sds((T, D), F32), _sds((1, D), F32), _sds((T, 2 * D), BF16), _sds((1, 2 * D), F32)],
    )(du, ag, x, g, dy, w1)


def _row_sums(a):
    return _dot(a.astype(BF16), jnp.ones((a.shape[1], 128), BF16))


def _head_rstd(x):
    return lax.rsqrt(_row_sums(x * x) * (1.0 / x.shape[1]) + NORM_EPS)


def _attn_qkv(x, g, wqkv, qn, kn, name):
    T, D = x.shape
    N = wqkv.shape[1]
    tn = N // 9
    E = HEAD_DIM
    tm = _pick(T, 1024, 8)

    def body(x_ref, g_ref, w_ref, qn_ref, kn_ref, o_ref, a_ref, h_ref):
        j = pl.program_id(1)

        @pl.when(j == 0)
        def _():
            xv = x_ref[...]
            h_ref[...] = (xv * _rstd(xv) * g_ref[...]).astype(BF16)

        res = _dot(h_ref[...], w_ref[...])
        o_ref[...] = res.astype(BF16)
        part = j % 3

        @pl.when(part == 2)
        def _():
            a_ref[...] = res.astype(BF16)

        @pl.when(part < 2)
        def _():
            grp = j // 3
            c = qn_ref[pl.ds(grp, 1), :] * kn_ref[pl.ds(grp, 1), :] * (E ** -0.5)
            fac = jnp.where(part == 0, c, jnp.ones_like(c))
            for h in range(tn // E):
                hs = slice(h * E, (h + 1) * E)
                xh = res[:, hs]
                a_ref[:, hs] = (xh * _head_rstd(xh) * fac).astype(BF16)

    ng = qn.shape[0]
    return _call(
        body, name, (T // tm, 9),
        [pl.BlockSpec((tm, D), lambda i, j: (i, 0)), pl.BlockSpec((1, D), lambda i, j: (0, 0)),
         pl.BlockSpec((D, tn), lambda i, j: (0, j)), pl.BlockSpec((ng, E), lambda i, j: (0, 0)),
         pl.BlockSpec((ng, E), lambda i, j: (0, 0))],
        [pl.BlockSpec((tm, tn), lambda i, j: (i, j)), pl.BlockSpec((tm, tn), lambda i, j: (i, j)),
         pl.BlockSpec((tm, D), lambda i, j: (i, 0))],
        [_sds((T, N), BF16), _sds((T, N), BF16), _sds((T, D), BF16)],
    )(x, g, wqkv, qn, kn)


def _band_mask(q, steps, nblk):
    i = lax.broadcasted_iota(jnp.int32, (q, 2 * q), 0)
    j = lax.broadcasted_iota(jnp.int32, (q, 2 * q), 1)
    diff = q + i - j
    first_key = jnp.where(nblk > 0, 0, q)
    return (diff >= 0) & (diff <= steps) & (j >= first_key)


def _per_row(blk, width):
    e = blk.shape[1]
    if width % e == 0:
        return jnp.concatenate([blk] * (width // e), axis=1)
    return jnp.broadcast_to(blk[:, :1], (blk.shape[0], width))


def _streams(a, dil, to_streams, name, col0=0, ncols=None):
    T, C = a.shape
    ncols = C if ncols is None else ncols
    Q = ATTN_BLOCK
    run = Q * dil
    reps = max(1, min(2048 // run, T // run))
    while T % (run * reps):
        reps -= 1
    rows = run * reps
    cw = _pick(ncols, 512, 128)
    ns = cw // 128

    def body(a_ref, o_ref, scr):
        for s in range(ns):
            ls = slice(s * 128, (s + 1) * 128)
            slab = scr.at[s]
            if to_streams:
                slab[...] = a_ref[:, ls].astype(F32)
                for u in range(reps):
                    for r in range(dil):
                        o_ref[pl.ds(u * run + r * Q, Q), ls] = slab[pl.ds(u * run + r, Q, stride=dil), :].astype(a.dtype)
            else:
                for u in range(reps):
                    for r in range(dil):
                        slab[pl.ds(u * run + r, Q, stride=dil), :] = a_ref[pl.ds(u * run + r * Q, Q), ls].astype(F32)
                o_ref[:, ls] = slab[...].astype(a.dtype)

    return _call(
        body, name, (T // rows, ncols // cw),
        [pl.BlockSpec((rows, cw), lambda i, j: (i, col0 // cw + j))],
        [pl.BlockSpec((rows, cw), lambda i, j: (i, j))],
        [_sds((T, ncols), a.dtype)],
        scratch=[pltpu.VMEM((ns, rows, 128), F32)],
    )(a)[0]


def _attn_fwd(qkv, base, HE, window, dil, name):
    T = qkv.shape[0]
    H = HE // HEAD_DIM
    E = HEAD_DIM
    Q = ATTN_BLOCK
    nb = T // dil // Q
    steps = window // dil

    def body(q_ref, kc_ref, kp_ref, vc_ref, vp_ref, o_ref, l_ref):
        n = pl.program_id(1)
        valid = _band_mask(Q, steps, n)
        ones = jnp.ones((2 * Q, E), BF16)
        outs, lses = [], []
        for h in range(H):
            hs = slice(h * E, (h + 1) * E)
            k2 = jnp.concatenate([kp_ref[:, hs], kc_ref[:, hs]], axis=0)
            v2 = jnp.concatenate([vp_ref[:, hs], vc_ref[:, hs]], axis=0)
            s = jnp.where(valid, _dot_nt(q_ref[:, hs], k2), -1e30)
            m = jnp.max(s, axis=-1, keepdims=True)
            p = jnp.exp(s - m).astype(BF16)
            acc = _dot(p, jnp.concatenate([v2, ones], axis=1))
            l = acc[:, E:]
            outs.append((acc[:, :E] * (1.0 / l)).astype(o_ref.dtype))
            lses.append(m + jnp.log(l))
        o_ref[...] = jnp.concatenate(outs, axis=1)
        l_ref[...] = jnp.concatenate(lses, axis=1)

    blk = lambda s, back: pl.BlockSpec((Q, HE), lambda r, n: (jnp.maximum(n - back, 0) * dil + r, base + s))
    out = pl.BlockSpec((Q, HE), lambda r, n: (n * dil + r, 0))
    return _call(
        body, name, (dil, nb),
        [blk(0, 0), blk(1, 0), blk(1, 1), blk(2, 0), blk(2, 1)],
        [out, out],
        [_sds((T, HE), BF16), _sds((T, HE), F32)],
    )(qkv, qkv, qkv, qkv, qkv)


def _attn_merge(os, lses, x, wo, name):
    T, D = x.shape
    HE = wo.shape[0]
    tm = _pick(T, 512, 8)
    ng = len(os)

    def body(*refs):
        o_refs = refs[:ng]
        l_refs = refs[ng:2 * ng]
        x_ref, w_ref, xo_ref, om_ref, lt_ref = refs[2 * ng:]
        ls = [r[...] for r in l_refs]
        m = functools.reduce(jnp.maximum, ls)
        es = [jnp.exp(l - m) for l in ls]
        tot = functools.reduce(lambda a, b: a + b, es)
        inv = 1.0 / tot
        om = functools.reduce(lambda a, b: a + b, [e * inv * r[...] for e, r in zip(es, o_refs)])
        omb = om.astype(BF16)
        om_ref[...] = omb
        lt_ref[...] = m + jnp.log(tot)
        xo_ref[...] = x_ref[...] + _dot(omb, w_ref[...])

    row = lambda i: (i, 0)
    fix = lambda i: (0, 0)
    return _call(
        body, name, (T // tm,),
        [pl.BlockSpec((tm, HE), row)] * (2 * ng) + [pl.BlockSpec((tm, D), row), pl.BlockSpec((HE, D), fix)],
        [pl.BlockSpec((tm, D), row), pl.BlockSpec((tm, HE), row), pl.BlockSpec((tm, HE), row)],
        [_sds((T, D), F32), _sds((T, HE), BF16), _sds((T, HE), F32)],
    )(*os, *lses, x, wo)


def _attn_out_bwd(dy, om, wo, name, after=()):
    T, D = dy.shape
    HE = wo.shape[0]
    E = HEAD_DIM
    tm = _pick(T, 512, 8)

    def body(dy_ref, om_ref, w_ref, dom_ref, dl_ref):
        dom = _dot_nt(dy_ref[...].astype(BF16), w_ref[...])
        dom_ref[...] = dom.astype(BF16)
        prod = dom * om_ref[...].astype(F32)
        for h in range(HE // E):
            hs = slice(h * E, (h + 1) * E)
            dl_ref[:, hs] = jnp.broadcast_to(jnp.sum(prod[:, hs], axis=-1, keepdims=True), (tm, E))

    row = lambda i: (i, 0)
    return _call(
        body, name, (T // tm,),
        [pl.BlockSpec((tm, D), row), pl.BlockSpec((tm, HE), row), pl.BlockSpec((HE, D), lambda i: (0, 0))],
        [pl.BlockSpec((tm, HE), row), pl.BlockSpec((tm, HE), row)],
        [_sds((T, HE), BF16), _sds((T, HE), F32)], after=after,
    )(dy, om, wo)


def _attn_bwd(qkv, base, HE, dom, lse, delta, window, dil, name):
    T = qkv.shape[0]
    H = HE // HEAD_DIM
    E = HEAD_DIM
    Q = ATTN_BLOCK
    nb = T // dil // Q
    steps = window // dil

    def body(q_ref, kc_ref, kp_ref, vc_ref, vp_ref, do_ref, l_ref, dl_ref, dq_ref, dk_ref, dv_ref, ck_sc, cv_sc):
        n = pl.program_id(1)

        @pl.when(n == 0)
        def _():
            ck_sc[...] = jnp.zeros_like(ck_sc)
            cv_sc[...] = jnp.zeros_like(cv_sc)

        @pl.when(n < nb)
        def _():
            valid = _band_mask(Q, steps, n)
            ck_old = ck_sc[...]
            cv_old = cv_sc[...]
            dqs, dks, dvs = [], [], []
            for h in range(H):
                hs = slice(h * E, (h + 1) * E)
                q = q_ref[:, hs]
                do = do_ref[:, hs]
                k2 = jnp.concatenate([kp_ref[:, hs], kc_ref[:, hs]], axis=0)
                v2 = jnp.concatenate([vp_ref[:, hs], vc_ref[:, hs]], axis=0)
                p = jnp.where(valid, jnp.exp(_dot_nt(q, k2) - _per_row(l_ref[:, hs], 2 * Q)), 0.0)
                ds = (p * (_dot_nt(do, v2) - _per_row(dl_ref[:, hs], 2 * Q))).astype(BF16)
                dqs.append(_dot(ds, k2).astype(BF16))
                dks.append(_dot_tn(ds, q))
                dvs.append(_dot_tn(p.astype(BF16), do))
            cat = lambda parts: jnp.concatenate(parts, axis=1)
            dq_ref[...] = cat(dqs)
            dk_ref[...] = (ck_old + cat([d[:Q] for d in dks])).astype(BF16)
            dv_ref[...] = (cv_old + cat([d[:Q] for d in dvs])).astype(BF16)
            ck_sc[...] = cat([d[Q:] for d in dks])
            cv_sc[...] = cat([d[Q:] for d in dvs])

        @pl.when(n == nb)
        def _():
            dk_ref[...] = ck_sc[...].astype(BF16)
            dv_ref[...] = cv_sc[...].astype(BF16)

    nq = lambda n: jnp.minimum(n, nb - 1)
    blk = lambda s, back: pl.BlockSpec((Q, HE), lambda r, n: (jnp.maximum(nq(n) - back, 0) * dil + r, base + s))
    qblk = pl.BlockSpec((Q, HE), lambda r, n: (nq(n) * dil + r, 0))
    kblk = pl.BlockSpec((Q, HE), lambda r, n: (jnp.maximum(n - 1, 0) * dil + r, 0))
    return _call(
        body, name, (dil, nb + 1),
        [blk(0, 0), blk(1, 0), blk(1, 1), blk(2, 0), blk(2, 1), qblk, qblk, qblk],
        [qblk, kblk, kblk],
        [_sds((T, HE), BF16)] * 3,
        scratch=[pltpu.VMEM((Q, HE), F32), pltpu.VMEM((Q, HE), F32)],
    )(qkv, qkv, qkv, qkv, qkv, dom, lse, delta)


def _qk_norm_bwd(dq, dk, qkv, base, HE, gq, gk, name):
    T = dq.shape[0]
    E = HEAD_DIM
    tm = _pick(T, 512, 8)
    scale = E ** -0.5

    def body(dq_ref, dk_ref, q_ref, k_ref, gq_ref, gk_ref, oq_ref, ok_ref, dgq_ref, dgk_ref):
        @pl.when(pl.program_id(0) == 0)
        def _():
            dgq_ref[...] = jnp.zeros_like(dgq_ref)
            dgk_ref[...] = jnp.zeros_like(dgk_ref)

        gqv = gq_ref[...]
        gkv = gk_ref[...]
        c = gqv * gkv * scale
        dc = jnp.zeros((1, E), F32)
        for h in range(HE // E):
            hs = slice(h * E, (h + 1) * E)
            q = q_ref[:, hs].astype(F32)
            rq = _head_rstd(q)
            qh = q * rq
            a = dq_ref[:, hs].astype(F32)
            dc = dc + jnp.sum(a * qh, axis=0, keepdims=True)
            dqh = a * c
            oq_ref[:, hs] = (rq * (dqh - qh * (_row_sums(dqh * qh) * (1.0 / E)))).astype(BF16)
            k = k_ref[:, hs].astype(F32)
            rk = _head_rstd(k)
            kh = k * rk
            b = dk_ref[:, hs].astype(F32)
            ok_ref[:, hs] = (rk * (b - kh * (_row_sums(b * kh) * (1.0 / E)))).astype(BF16)
        dgq_ref[...] += dc * (gkv * scale)
        dgk_ref[...] += dc * (gqv * scale)

    row = lambda i: (i, 0)
    vec = pl.BlockSpec((1, E), lambda i: (0, 0))
    return _call(
        body, name, (T // tm,),
        [pl.BlockSpec((tm, HE), row), pl.BlockSpec((tm, HE), row), pl.BlockSpec((tm, HE), lambda i: (i, base)),
         pl.BlockSpec((tm, HE), lambda i: (i, base + 1)), vec, vec],
        [pl.BlockSpec((tm, HE), row), pl.BlockSpec((tm, HE), row), vec, vec],
        [_sds((T, HE), BF16), _sds((T, HE), BF16), _sds((1, E), F32), _sds((1, E), F32)],
    )(dq, dk, qkv, qkv, gq, gk)


def _attn_qkv_bwd(dqkv, x, g, dy, wqkv, name):
    T, D = x.shape
    HE = wqkv.shape[1] // 9
    tm = _pick(T, 256, 8)

    def body(*refs):
        d_refs = refs[:9]
        x_ref, g_ref, dy_ref, w_ref, dx_ref, dg_ref = refs[9:]

        @pl.when(pl.program_id(0) == 0)
        def _():
            dg_ref[...] = jnp.zeros_like(dg_ref)

        dh = _dot_nt(d_refs[0][...], w_ref[:, :HE])
        for s in range(1, 9):
            dh = dh + _dot_nt(d_refs[s][...], w_ref[:, s * HE:(s + 1) * HE])
        xv = x_ref[...]
        r = _rstd(xv)
        xh = xv * r
        dx_ref[...] = dy_ref[...] + _norm_bwd(dh, xh, r, g_ref[...])
        dg_ref[...] += jnp.sum(dh * xh, axis=0, keepdims=True)

    row = lambda i: (i, 0)
    fix = lambda i: (0, 0)
    return _call(
        body, name, (T // tm,),
        [pl.BlockSpec((tm, HE), row)] * 9 + [pl.BlockSpec((tm, D), row), pl.BlockSpec((1, D), fix), pl.BlockSpec((tm, D), row),
                                           pl.BlockSpec((D, 9 * HE), fix, pipeline_mode=pl.Buffered(1))],
        [pl.BlockSpec((tm, D), row), pl.BlockSpec((1, D), fix)],
        [_sds((T, D), F32), _sds((1, D), F32)],
    )(*dqkv, x, g, dy, wqkv)


def _loss_head(y, target, name):
    T, D = y.shape
    tm = _pick(T, 512, 8)

    def body(y_ref, t_ref, dy_ref, sq_ref):
        @pl.when(pl.program_id(0) == 0)
        def _():
            sq_ref[...] = jnp.zeros_like(sq_ref)

        err = y_ref[...] - t_ref[...]
        dy_ref[...] = err * (1.0 / D)
        sq_ref[...] += jnp.sum(err * err, axis=0, keepdims=True)

    row = lambda i: (i, 0)
    return _call(
        body, name, (T // tm,),
        [pl.BlockSpec((tm, D), row), pl.BlockSpec((tm, D), row)],
        [pl.BlockSpec((tm, D), row), pl.BlockSpec((1, D), lambda i: (0, 0))],
        [_sds((T, D), F32), _sds((1, D), F32)],
    )(y, target)


def _local_step(x, target, norm_g, b_pw1, w_dw, b_dw, cng, b_pw2, qn, kn, block_weights, grads_ready):
    T, D = x.shape
    ng = lambda l, k: norm_g[l, k][None, :]
    bn = _pick(D, 256, 128)

    w_in, w_out = [None] * 4, [None] * 4
    w_in[0], w_out[0] = block_weights(0, x)
    x1, *gu0 = _ffn_fwd(x, ng(0, 0), w_in[0], w_out[0], "ffn_fwd_0")
    pw1, pw2 = block_weights(1, x1)
    ag, u, hc = _conv_pre(x1, ng(0, 1), pw1, b_pw1, "conv_pre")
    c = _dwconv(u, w_dw, b_dw, "dwconv")
    x2, s = _conv_post(c, x1, cng, pw2, b_pw2, "conv_post")
    w_in[1], w_out[1] = block_weights(2, x2)
    x3, *gu1 = _ffn_fwd(x2, ng(0, 2), w_in[1], w_out[1], "ffn_fwd_1")
    w_in[2], w_out[2] = block_weights(3, x3)
    x4, *gu2 = _ffn_fwd(x3, ng(1, 0), w_in[2], w_out[2], "ffn_fwd_2")
    wqkv, wo = block_weights(4, x4)
    HE = wo.shape[0]
    F = w_out[0].shape[0]
    bf = _pick(F, 256, 128)
    bh = _pick(HE, 512, 128)
    qkv, att, ha = _attn_qkv(x4, ng(1, 1), wqkv, qn, kn, "attn_qkv")
    qkv_s = [(att, 3 * gi) if dil == 1 else (_streams(att, dil, True, f"qkv_streams_{gi}", 3 * gi * HE, 3 * HE), 0)
             for gi, (_, dil) in enumerate(ATTN_GROUPS)]
    tokens = lambda a, dil, name: a if dil == 1 else _streams(a, dil, False, name)
    streams = lambda a, dil, name: a if dil == 1 else _streams(a, dil, True, name)
    os, lses = [], []
    for gi, (window, dil) in enumerate(ATTN_GROUPS):
        o, l = _attn_fwd(*qkv_s[gi], HE, window, dil, f"attn_fwd_{gi}")
        os.append(tokens(o, dil, f"o_tokens_{gi}"))
        lses.append(tokens(l, dil, f"lse_tokens_{gi}"))
    x5, om, lse = _attn_merge(os, lses, x4, wo, "attn_merge")
    w_in[3], w_out[3] = block_weights(5, x5)
    x6, *gu3 = _ffn_fwd(x5, ng(1, 2), w_in[3], w_out[3], "ffn_fwd_3")
    dy, sq = _loss_head(x6, target, "loss_head")

    grads = {"ffn_w_in": [None] * 4, "ffn_w_out": [None] * 4}
    dnorm = [[None] * 3 for _ in range(2)]

    def ffn_back(lf, k, xin, gvec, dy, gu, after):
        dx, dg, dgate, dup, a, h, dyb = _ffn_bwd(xin, gvec, dy, gu[0], gu[1], w_in[lf], w_out[lf], f"ffn_bwd_{lf}", after=after)
        d_in = _mm_tn(h, dgate, D, bf, (D, 2 * F), (D, bf), lambda mi, ni: (0, ni), 1.0, f"ffn_dw_gate_{lf}")
        grads["ffn_w_in"][lf] = _mm_tn(h, dup, D, bf, (D, 2 * F), (D, bf), lambda mi, ni: (0, F // bf + ni), 1.0,
                                       f"ffn_dw_up_{lf}", prev=d_in)
        grads["ffn_w_out"][lf] = _mm_tn(a, dyb, bf, D, (F, D), (bf, D), lambda mi, ni: (mi, 0), 1.0, f"ffn_dw_out_{lf}")
        return dx, dg, grads_ready(k, (grads["ffn_w_in"][lf], grads["ffn_w_out"][lf]), dx)

    dx, dnorm[1][2], tok = ffn_back(3, 5, x5, ng(1, 2), dy, gu3, ())
    dom, delta = _attn_out_bwd(dx, om, wo, "attn_out_bwd", after=tok)
    grads["attn_w_o"] = _mm_tn(om, dx, HE, bn, (HE, D), (HE, bn), lambda mi, ni: (0, ni), 1.0, "attn_dw_o")
    dqkv, dgq, dgk = [], [], []
    for gi, (window, dil) in enumerate(ATTN_GROUPS):
        ds = _attn_bwd(*qkv_s[gi], HE, streams(dom, dil, f"dom_streams_{gi}"), streams(lse, dil, f"lse_streams_{gi}"),
                       streams(delta, dil, f"delta_streams_{gi}"), window, dil, f"attn_bwd_{gi}")
        dq, dk, dv = [tokens(d, dil, f"d{nm}_tokens_{gi}") for d, nm in zip(ds, "qkv")]
        dq, dk, a_, b_ = _qk_norm_bwd(dq, dk, qkv, 3 * gi, HE, qn[gi][None, :], kn[gi][None, :], f"qk_norm_bwd_{gi}")
        dqkv += [dq, dk, dv]
        dgq.append(a_)
        dgk.append(b_)
    grads["attn_q_norm"] = jnp.concatenate(dgq, axis=0)
    grads["attn_k_norm"] = jnp.concatenate(dgk, axis=0)
    d_qkv = None
    for s9 in range(9):
        d_qkv = _mm_tn(ha, dqkv[s9], D, bh, (D, 9 * HE), (D, bh), lambda mi, ni, s9=s9: (0, s9 * (HE // bh) + ni), 1.0,
                       f"attn_dw_qkv_{s9}", prev=d_qkv)
    grads["attn_w_qkv"] = d_qkv
    dx, dnorm[1][1] = _attn_qkv_bwd(dqkv, x4, ng(1, 1), dx, wqkv, "attn_qkv_bwd")
    tok = grads_ready(4, (grads["attn_w_qkv"], grads["attn_w_o"]), dx)
    dx, dnorm[1][0], tok = ffn_back(2, 3, x3, ng(1, 0), dx, gu2, tok)

    dx, dnorm[0][2], tok = ffn_back(1, 2, x2, ng(0, 2), dx, gu1, tok)
    dc, grads["conv_norm_g"], grads["conv_b_pw2"] = _conv_post_bwd(dx, c, cng, pw2, "conv_post_bwd", after=tok)
    grads["conv_w_pw2"] = _mm_tn(s, dx, D, bn, (D, D), (D, bn), lambda mi, ni: (0, ni), 1.0, "conv_dw_pw2")
    du, grads["conv_w_dw"], grads["conv_b_dw"] = _dwconv_bwd(dc, u, w_dw, "dwconv_bwd")
    dx, dnorm[0][1], dag, grads["conv_b_pw1"] = _conv_pre_bwd(du, ag, x1, ng(0, 1), dx, pw1, "conv_pre_bwd")
    grads["conv_w_pw1"] = _mm_tn(hc, dag, D, 2 * bn, (D, 2 * D), (D, 2 * bn), lambda mi, ni: (0, ni), 1.0, "conv_dw_pw1")
    tok = grads_ready(1, (grads["conv_w_pw1"], grads["conv_w_pw2"]), dx)
    dx, dnorm[0][0], _ = ffn_back(0, 0, x, ng(0, 0), dx, gu0, tok)

    grads["norm_g"] = jnp.concatenate([jnp.concatenate(r, axis=0)[None] for r in dnorm], axis=0)
    return sq, dx, grads


class _Sharded:
    def __init__(self, name, full3, half_axis, shard_axis, src, slab=None):
        self.name, self.full3, self.half_axis, self.shard_axis = name, tuple(full3), half_axis, shard_axis
        self.src, self.slab = src, slab

    def source(self, refs):
        return refs[self.src] if self.slab is None else refs[self.src].at[self.slab]

    def _cut(self, shape, axis, parts):
        s = list(shape)
        s[axis] //= parts
        return tuple(s)

    @property
    def shard3(self):
        return self._cut(self.full3, self.shard_axis, N_CHIPS)

    @property
    def pair3(self):
        return self._cut(self.full3, self.half_axis, 2)

    @property
    def part3(self):
        return self._cut(self.shard3, self.half_axis, 2)

    @staticmethod
    def _slice(ref, axis, idx, parts):
        n = ref.shape[axis] // parts
        start = idx * n
        minor = len(ref.shape) - 1 - axis
        if minor < 2 and not isinstance(start, int):
            start = pl.multiple_of(start, 128 if minor == 0 else (16 if n % 16 == 0 else 8))
        sl = [slice(None)] * len(ref.shape)
        sl[axis] = pl.ds(start, n)
        return ref.at[tuple(sl)]

    def half(self, ref, h):
        return self._slice(ref, self.half_axis, h, 2)

    def shard(self, ref, j):
        return self._slice(ref, self.shard_axis, j, N_CHIPS)


def _place():
    x, y, c = lax.axis_index("x"), lax.axis_index("y"), lax.axis_index("c")
    return x, y, c, 2 * x + y


_RELS = (1, 2, 3)


def _peer(x, y, rel):
    px = 1 - x if rel & 2 else x
    py = 1 - y if rel & 1 else y
    return px, py, 2 * px + py


ANY = pl.BlockSpec(memory_space=pl.ANY)


def _comm_call(body, name, n_in, out_shape, n_sems, aliases=None):
    return pl.pallas_call(
        body, name=name, in_specs=[ANY] * n_in, out_specs=[ANY] * len(out_shape), out_shape=out_shape,
        scratch_shapes=[pltpu.SemaphoreType.DMA((n,)) for n in n_sems],
        input_output_aliases=aliases or {},
        compiler_params=pltpu.CompilerParams(has_side_effects=True),
    )


def _remote(src, dst, send_sem, recv_sem, dev):
    return pltpu.make_async_remote_copy(src_ref=src, dst_ref=dst, send_sem=send_sem, recv_sem=recv_sem, device_id=dev,
                                        device_id_type=MESH)


def _gather_small(small_shards):
    ns = len(small_shards)

    def body(*refs):
        ins, outs = refs[:ns], refs[ns:2 * ns]
        lsem, ssem, rsem = refs[2 * ns:]
        x, y, c, me = _place()
        cols = lambda ref, j: _Sharded._slice(ref, 1, j, N_CHIPS)
        local = [pltpu.make_async_copy(ins[i], cols(outs[i], me), lsem.at[i]) for i in range(ns)]
        sends = []
        for i in range(ns):
            for k, rel in enumerate(_RELS):
                px, py, _ = _peer(x, y, rel)
                sends.append(_remote(ins[i], cols(outs[i], me), ssem.at[3 * i + k], rsem.at[3 * i + k], (px, py, c)))
        for cp in local + sends:
            cp.start()
        for i in range(ns):
            for k, rel in enumerate(_RELS):
                _, _, pj = _peer(x, y, rel)
                got = cols(outs[i], pj)
                _remote(got, got, ssem.at[3 * i + k], rsem.at[3 * i + k], (x, y, c)).wait_recv()
        for cp in sends:
            cp.wait_send()
        for cp in local:
            cp.wait()

    out_shape = [_sds((s.shape[0], s.shape[1] * N_CHIPS), F32) for s in small_shards]
    return _comm_call(body, "gather_small", ns, out_shape, [ns, 3 * ns, 3 * ns])(*small_shards)


HBM = pl.BlockSpec(memory_space=pltpu.HBM)
SEM = pl.BlockSpec(memory_space=pltpu.SEMAPHORE)
DATAFLOW = pltpu.SideEffectType.DATAFLOW_SIDE_EFFECTING


def _in_hbm(a):
    return pltpu.with_memory_space_constraint(a, pltpu.HBM)


def _cast_place(it, shard, scal, after=()):
    a_n, r_n, c_n = it.shard3
    tr = _pick(r_n, 256, 16)
    sa = it.shard_axis

    def body(sc_ref, s_ref, o_ref):
        o_ref[...] = s_ref[...].astype(BF16)

    if it.slab is None:
        src = pl.BlockSpec((1, tr, c_n), lambda a, rb, sc: (a, rb, 0))
    else:
        src = pl.BlockSpec((None, 1, tr, c_n), lambda a, rb, sc: (it.slab, a, rb, 0))
    dst = pl.BlockSpec((1, tr, c_n), lambda a, rb, sc: (a + sc[1] * (a_n if sa == 0 else 0), rb + sc[1] * (r_n // tr if sa == 1 else 0),
                                                       sc[1] if sa == 2 else 0))
    return _call(body, f"cast_place_{it.name}", (a_n, r_n // tr), [src], [dst], [_sds(it.full3, BF16)], prefetch=1,
                 after=after)(scal, shard)[0]


def _gather_start(items, fulls, name):
    ni = len(items)

    def body(*refs):
        outs = refs[ni:]
        ssem, rsem, full = outs[:ni], outs[ni:2 * ni], outs[2 * ni:3 * ni]
        x, y, c, me = _place()
        for i, it in enumerate(items):
            mine = it.half(it.shard(full[i], me), c)
            for k, rel in enumerate(_RELS):
                px, py, _ = _peer(x, y, rel)
                _remote(mine, mine, ssem[i].at[k], rsem[i].at[k], (px, py, c)).start()

    outs = pl.pallas_call(
        body, name=name, in_specs=[HBM] * ni, out_specs=[SEM] * (2 * ni) + [HBM] * ni,
        out_shape=[pltpu.SemaphoreType.DMA((3,))] * (2 * ni) + [pltpu.HBM(it.full3, BF16) for it in items],
        input_output_aliases={j: 2 * ni + j for j in range(ni)},
        compiler_params=pltpu.CompilerParams(has_side_effects=DATAFLOW),
    )(*[_in_hbm(f) for f in fulls])
    return outs[:ni], outs[ni:2 * ni], outs[2 * ni:]


def _gather_forward(items, fulls, ssems, rsems, after, name):
    ni = len(items)

    def body(*refs):
        ssem, rsem = refs[ni:2 * ni], refs[2 * ni:3 * ni]
        outs = refs[3 * ni + 1:]
        full, fsem, gsem = outs[:ni], outs[ni:2 * ni], outs[2 * ni:3 * ni]
        x, y, c, me = _place()
        sib = (x, y, 1 - c)
        for i, it in enumerate(items):
            for k, rel in enumerate(_RELS):
                _, _, pj = _peer(x, y, rel)
                got = it.half(it.shard(full[i], pj), c)
                _remote(got, got, ssem[i].at[k], rsem[i].at[k], sib).wait_recv()
                _remote(got, got, fsem[i].at[k], gsem[i].at[k], sib).start()
        for i, it in enumerate(items):
            mine = it.half(it.shard(full[i], me), c)
            for k in range(3):
                _remote(mine, mine, ssem[i].at[k], rsem[i].at[k], sib).wait_send()

    outs = pl.pallas_call(
        body, name=name, in_specs=[HBM] * ni + [SEM] * (2 * ni) + [ANY],
        out_specs=[HBM] * ni + [SEM] * (2 * ni),
        out_shape=[pltpu.HBM(it.full3, BF16) for it in items] + [pltpu.SemaphoreType.DMA((3,))] * (2 * ni),
        input_output_aliases={i: i for i in range(ni)},
        compiler_params=pltpu.CompilerParams(has_side_effects=DATAFLOW),
    )(*fulls, *ssems, *rsems, after)
    return outs[:ni], outs[ni:2 * ni], outs[2 * ni:]


def _gather_finish(items, fulls, fsems, gsems, name):
    ni = len(items)

    def body(*refs):
        fsem, gsem = refs[ni:2 * ni], refs[2 * ni:3 * ni]
        full = refs[3 * ni:]
        x, y, c, _ = _place()
        sib = (x, y, 1 - c)
        for i, it in enumerate(items):
            for k, rel in enumerate(_RELS):
                _, _, pj = _peer(x, y, rel)
                got = it.half(it.shard(full[i], pj), 1 - c)
                _remote(got, got, fsem[i].at[k], gsem[i].at[k], sib).wait_recv()
                sent = it.half(it.shard(full[i], pj), c)
                _remote(sent, sent, fsem[i].at[k], gsem[i].at[k], sib).wait_send()

    return pl.pallas_call(
        body, name=name, in_specs=[HBM] * ni + [SEM] * (2 * ni), out_specs=[HBM] * ni,
        out_shape=[pltpu.HBM(it.full3, BF16) for it in items],
        input_output_aliases={i: i for i in range(ni)},
        compiler_params=pltpu.CompilerParams(has_side_effects=DATAFLOW),
    )(*fulls, *fsems, *gsems)


def _exchange_start(name, arrays, n_sems, copies):
    na, ns = len(arrays), len(n_sems)

    def body(*refs):
        outs = refs[na:]
        ssem, rsem, thru, token = outs[:ns], outs[ns:2 * ns], outs[2 * ns:2 * ns + na], outs[-1]
        for send, _ in copies(thru, ssem, rsem):
            send.start()
        token[...] = jnp.zeros_like(token)

    outs = pl.pallas_call(
        body, name=name, in_specs=[HBM] * na,
        out_specs=[SEM] * (2 * ns) + [HBM] * na + [pl.BlockSpec(memory_space=pltpu.VMEM)],
        out_shape=[pltpu.SemaphoreType.DMA((n,)) for n in n_sems] * 2 + [pltpu.HBM(a.shape, a.dtype) for a in arrays]
        + [_sds((8, 128), F32)],
        input_output_aliases={j: 2 * ns + j for j in range(na)},
        compiler_params=pltpu.CompilerParams(has_side_effects=DATAFLOW),
    )(*[_in_hbm(a) for a in arrays])
    return outs[:ns], outs[ns:2 * ns], outs[2 * ns:2 * ns + na], outs[-1]


def _exchange_wait(name, arrays, ssems, rsems, copies, after):
    na, ns, nw = len(arrays), len(ssems), len(after)

    def body(*refs):
        ssem, rsem = refs[na:na + ns], refs[na + ns:na + 2 * ns]
        thru = refs[na + 2 * ns + nw:]
        for send, recv in copies(thru, ssem, rsem):
            recv.wait_recv()
            send.wait_send()

    return pl.pallas_call(
        body, name=name, in_specs=[HBM] * na + [SEM] * (2 * ns) + [ANY] * nw, out_specs=[HBM] * na,
        out_shape=[pltpu.HBM(a.shape, a.dtype) for a in arrays],
        input_output_aliases={i: i for i in range(na)},
        compiler_params=pltpu.CompilerParams(has_side_effects=DATAFLOW),
    )(*arrays, *ssems, *rsems, *after)


def _pair_copies(items):
    n = len(items)

    def copies(a, ssem, rsem):
        x, y, c, _ = _place()
        cps = [_remote(it.half(a[i], 1 - c), a[n + i], ssem[i].at[0], rsem[i].at[0], (x, y, 1 - c)) for i, it in enumerate(items)]
        return [(cp, cp) for cp in cps]

    return copies


def _chip_copies(items):
    n = len(items)

    def copies(a, ssem, rsem):
        x, y, c, _ = _place()
        cps = []
        for i, it in enumerate(items):
            for k, rel in enumerate(_RELS):
                px, py, pj = _peer(x, y, rel)
                cps.append(_remote(it.shard(a[i], pj), a[n + i].at[k], ssem[i].at[k], rsem[i].at[k], (px, py, c)))
        return [(cp, cp) for cp in cps]

    return copies


def _fill_copies(items):
    def copies(a, ssem, rsem):
        x, y, c, _ = _place()
        sib = (x, y, 1 - c)
        out = []
        for i, it in enumerate(items):
            mine, other = it.half(a[i], c), it.half(a[i], 1 - c)
            out.append((_remote(mine, mine, ssem[i].at[0], rsem[i].at[0], sib), _remote(other, other, ssem[i].at[0], rsem[i].at[0], sib)))
        return out

    return copies


def _ew_tiles(d):
    _, rows, cols = d.part3
    return _pick(rows, 256, 16), cols


def _pair_add(d, g_full, got, scal):
    tr, tc = _ew_tiles(d)
    a_n, r_n, c_n = d.pair3
    ha = d.half_axis

    def body(sc_ref, g_ref, r_ref, o_ref, ob_ref):
        s = g_ref[...] + r_ref[...].astype(F32)
        o_ref[...] = s
        ob_ref[...] = s.astype(BF16)

    blk = (1, tr, tc)
    same = pl.BlockSpec(blk, lambda a, rb, cb, sc: (a, rb, cb))
    mine = pl.BlockSpec(blk, lambda a, rb, cb, sc: (a + sc[0] * (a_n if ha == 0 else 0), rb + sc[0] * (r_n // tr if ha == 1 else 0), cb))
    return _call(body, f"pair_add_{d.name}", (a_n, r_n // tr, c_n // tc), [mine, same], [same, same],
                 [_sds(d.pair3, F32), _sds(d.pair3, BF16)], prefetch=1)(scal, g_full, got)


def _chip_reduce(d, pair_f32, got, scal):
    tr, tc = _ew_tiles(d)
    a_n, r_n, c_n = d.part3
    ha, sa = d.half_axis, d.shard_axis

    def body(sc_ref, p_ref, r0, r1, r2, o_ref):
        o_ref[...] = ((p_ref[...] + r0[...].astype(F32)) + r1[...].astype(F32)) + r2[...].astype(F32)

    blk = (1, tr, tc)
    own = pl.BlockSpec(blk, lambda a, rb, sc: (a + sc[1] * (a_n if sa == 0 else 0), rb + sc[1] * (r_n // tr if sa == 1 else 0),
                                               sc[1] if sa == 2 else 0))
    slot = lambda k: pl.BlockSpec((None,) + blk, lambda a, rb, sc: (k, a, rb, 0))
    out = pl.BlockSpec(blk, lambda a, rb, sc: (a + sc[0] * (a_n if ha == 0 else 0), rb + sc[0] * (r_n // tr if ha == 1 else 0), 0))
    return _call(body, f"chip_reduce_{d.name}", (a_n, r_n // tr), [own, slot(0), slot(1), slot(2)], [out],
                 [_sds(d.shard3, F32)], prefetch=1)(scal, pair_f32, got, got, got)[0]


def _adam_math(g, w, m, v):
    m = ADAM_B1 * m + (1.0 - ADAM_B1) * g
    v = ADAM_B2 * v + (1.0 - ADAM_B2) * (g * g)
    m_hat = m / (1.0 - ADAM_B1 ** ADAM_STEP)
    v_hat = v / (1.0 - ADAM_B2 ** ADAM_STEP)
    delta = -ADAM_LR * (m_hat / (jnp.sqrt(v_hat) + ADAM_EPS) + ADAM_WD * w)
    return delta, m, v


def _adam(d, g, w, m, v, prev=None):
    tr, tc = _ew_tiles(d)
    a_n, r_n, c_n = d.shard3
    n_prev = 0 if prev is None else 4

    def body(g_ref, w_ref, m_ref, v_ref, *rest):
        go_ref, d_ref, mo_ref, vo_ref = rest[n_prev:]
        gv = g_ref[...]
        go_ref[...] = gv
        d_ref[...], mo_ref[...], vo_ref[...] = _adam_math(gv, w_ref[...], m_ref[...], v_ref[...])

    plain = pl.BlockSpec((1, tr, tc), lambda a, rb: (a, rb, 0))
    if d.slab is None:
        wspec, shape = plain, d.shard3
    else:
        wspec, shape = pl.BlockSpec((None, 1, tr, tc), lambda a, rb: (d.slab, a, rb, 0)), (4,) + d.shard3
    in_specs = [plain] + [wspec] * 3
    args = [g, w, m, v]
    aliases = None
    if prev is not None:
        in_specs += [pl.BlockSpec(memory_space=pl.ANY)] * 4
        args += list(prev)
        aliases = {4 + k: k for k in range(4)}
    return _call(body, f"adam_{d.name}", (a_n, r_n // tr), in_specs, [wspec] * 4, [_sds(shape, F32)] * 4, aliases=aliases)(*args)


def _adam_small(gs, ws, ms, vs):
    n = len(gs)

    def body(*refs):
        for i in range(n):
            g, w, m, v = (refs[k * n + i][...] for k in range(4))
            d, mo, vo = _adam_math(g, w, m, v)
            refs[4 * n + i][...] = d
            refs[5 * n + i][...] = mo
            refs[6 * n + i][...] = vo

    vm = pl.BlockSpec(memory_space=pltpu.VMEM)
    outs = pl.pallas_call(body, name="adam_small", in_specs=[vm] * (4 * n), out_specs=[vm] * (3 * n),
                          out_shape=[_sds(g.shape, F32) for g in gs] * 3)(*gs, *ws, *ms, *vs)
    return outs[:n], outs[n:2 * n], outs[2 * n:]


def _allreduce_small(packed):
    rows, cols = packed.shape
    others = [(dx, dy, dc) for dx in (0, 1) for dy in (0, 1) for dc in (0, 1) if (dx, dy, dc) != (0, 0, 0)]

    def body(in_ref, out_ref, buf, ssem, rsem):
        x, y, c, _ = _place()
        lin = 4 * x + 2 * y + c
        buf[lin] = in_ref[...]
        cps = []
        for k, (dx, dy, dc) in enumerate(others):
            px = 1 - x if dx else x
            py = 1 - y if dy else y
            pc = 1 - c if dc else c
            cps.append((pltpu.make_async_remote_copy(src_ref=in_ref, dst_ref=buf.at[lin], send_sem=ssem.at[k], recv_sem=rsem.at[k],
                                                     device_id=(px, py, pc), device_id_type=MESH), 4 * px + 2 * py + pc))
        for cp, _ in cps:
            cp.start()
        for k, (cp, plin) in enumerate(cps):
            pltpu.make_async_remote_copy(src_ref=in_ref, dst_ref=buf.at[plin], send_sem=ssem.at[k], recv_sem=rsem.at[k],
                                         device_id=(x, y, c), device_id_type=MESH).wait_recv()
        for cp, _ in cps:
            cp.wait_send()
        acc = buf[0]
        for dev in range(1, 8):
            acc = acc + buf[dev]
        out_ref[...] = acc

    vm = pl.BlockSpec(memory_space=pltpu.VMEM)
    return pl.pallas_call(
        body, name="allreduce_small", in_specs=[vm], out_specs=vm, out_shape=_sds((rows, cols), F32),
        scratch_shapes=[pltpu.VMEM((8, rows, cols), F32), pltpu.SemaphoreType.DMA((7,)), pltpu.SemaphoreType.DMA((7,))],
        compiler_params=pltpu.CompilerParams(has_side_effects=True),
    )(packed)


_BIG = ("ffn_w_in", "ffn_w_out", "conv_w_pw1", "conv_w_pw2", "attn_w_qkv", "attn_w_o")
_SMALL = ("norm_g", "conv_b_pw1", "conv_w_dw", "conv_b_dw", "conv_norm_g", "conv_b_pw2", "attn_q_norm", "attn_k_norm")
_NAMES = ("norm_g", "ffn_w_in", "ffn_w_out", "conv_w_pw1", "conv_b_pw1", "conv_w_dw", "conv_b_dw", "conv_norm_g", "conv_w_pw2",
          "conv_b_pw2", "attn_w_qkv", "attn_q_norm", "attn_k_norm", "attn_w_o")


def _rows8(a, width):
    a = a.reshape(-1, min(a.shape[-1], width))
    return jnp.pad(a, ((0, -a.shape[0] % 8), (0, width - a.shape[1])))


def kernel(x, norm_g, ffn_w_in, ffn_w_out, conv_w_pw1, conv_b_pw1, conv_w_dw, conv_b_dw, conv_norm_g, conv_w_pw2, conv_b_pw2, attn_w_qkv, attn_q_norm, attn_k_norm, attn_w_o, loss_target, m_norm_g, m_ffn_w_in, m_ffn_w_out, m_conv_w_pw1, m_conv_b_pw1, m_conv_w_dw, m_conv_b_dw, m_conv_norm_g, m_conv_w_pw2, m_conv_b_pw2, m_attn_w_qkv, m_attn_q_norm, m_attn_k_norm, m_attn_w_o, v_norm_g, v_ffn_w_in, v_ffn_w_out, v_conv_w_pw1, v_conv_b_pw1, v_conv_w_dw, v_conv_b_dw, v_conv_norm_g, v_conv_w_pw2, v_conv_b_pw2, v_attn_w_qkv, v_attn_q_norm, v_attn_k_norm, v_attn_w_o):
    w = dict(zip(_NAMES, (norm_g, ffn_w_in, ffn_w_out, conv_w_pw1, conv_b_pw1, conv_w_dw, conv_b_dw, conv_norm_g, conv_w_pw2,
                          conv_b_pw2, attn_w_qkv, attn_q_norm, attn_k_norm, attn_w_o)))
    m = dict(zip(_NAMES, (m_norm_g, m_ffn_w_in, m_ffn_w_out, m_conv_w_pw1, m_conv_b_pw1, m_conv_w_dw, m_conv_b_dw, m_conv_norm_g,
                          m_conv_w_pw2, m_conv_b_pw2, m_attn_w_qkv, m_attn_q_norm, m_attn_k_norm, m_attn_w_o)))
    v = dict(zip(_NAMES, (v_norm_g, v_ffn_w_in, v_ffn_w_out, v_conv_w_pw1, v_conv_b_pw1, v_conv_w_dw, v_conv_b_dw, v_conv_norm_g,
                          v_conv_w_pw2, v_conv_b_pw2, v_attn_w_qkv, v_attn_q_norm, v_attn_k_norm, v_attn_w_o)))
    T, D = x.shape[1:]
    F = ffn_w_out.shape[2] * N_CHIPS
    HE = attn_w_o.shape[1] * N_CHIPS
    cx, cy, cc = lax.axis_index("x"), lax.axis_index("y"), lax.axis_index("c")
    me = 2 * cx + cy
    scal = jnp.stack([cc, me]).astype(jnp.int32)

    ffn_in = lambda lf: _Sharded(f"ffn_w_in_{lf}", (2, D // 2, 2 * F), 0, 2, "ffn_w_in", lf)
    ffn_out = lambda lf: _Sharded(f"ffn_w_out_{lf}", (4, F // 4, D), 1, 0, "ffn_w_out", lf)
    items = [
        ffn_in(0), ffn_out(0),
        _Sharded("conv_w_pw1", (2, D // 2, 2 * D), 0, 2, "conv_w_pw1"), _Sharded("conv_w_pw2", (4, D // 4, D), 1, 0, "conv_w_pw2"),
        ffn_in(1), ffn_out(1), ffn_in(2), ffn_out(2),
        _Sharded("attn_w_qkv", (2, D // 2, 9 * HE), 0, 2, "attn_w_qkv"), _Sharded("attn_w_o", (4, HE // 4, D), 1, 0, "attn_w_o"),
        ffn_in(3), ffn_out(3),
    ]
    mat_shapes = {"ffn_w_in": (D, 2 * F), "ffn_w_out": (F, D), "conv_w_pw1": (D, 2 * D), "conv_w_pw2": (D, D),
                  "attn_w_qkv": (D, 9 * HE), "attn_w_o": (HE, D)}

    def as_shards(a, n):
        it = next(i for i in items if i.src == n)
        return a.reshape(((4,) if it.slab is not None else ()) + it.shard3)

    norm_full, dw_full = _gather_small([norm_g.reshape(6, D // 4), conv_w_dw.reshape(CONV_WIDTH, D // 4)])
    place = lambda its, after: [_cast_place(it, as_shards(w[it.src], it.src), scal, after) for it in its]
    ssems, rsems, fulls = _gather_start(items[:2], place(items[:2], ()), "gather_start_first")
    more = _gather_start(items[2:], place(items[2:], fulls[:1]), "gather_start_rest")
    ssems, rsems, fulls = [list(a) + list(b) for a, b in zip((ssems, rsems, fulls), more)]

    def block_weights(k, after):
        sel = slice(2 * k, 2 * k + 2)
        got, fsems, gsems = _gather_forward(items[sel], fulls[sel], ssems[sel], rsems[sel], after, f"gather_forward_{k}")
        done = _gather_finish(items[sel], got, fsems, gsems, f"gather_finish_{k}")
        return [a.reshape(mat_shapes[it.src]) for a, it in zip(done, items[sel])]

    res = {}
    flight = []

    def advance(entry, k, dx):
        stage, its, st = entry
        n = len(its)
        if stage == 1:
            ssem, rsem, arrs, g32 = st
            got = _exchange_wait(f"grads_pair_wait_{k}", arrs, ssem, rsem, _pair_copies(its), dx)[n:]
            sums = [_pair_add(it, g, r, scal) for it, g, r in zip(its, g32, got)]
            land = [lax.empty((3,) + it.part3, BF16) for it in its]
            ssem, rsem, arrs, tok = _exchange_start(f"grads_chip_start_{k}", [p[1] for p in sums] + land, [3] * n, _chip_copies(its))
            return (2, its, (ssem, rsem, arrs, [p[0] for p in sums])), tok
        if stage == 2:
            ssem, rsem, arrs, p32 = st
            got = _exchange_wait(f"grads_chip_wait_{k}", arrs, ssem, rsem, _chip_copies(its), dx)[n:]
            red = [_chip_reduce(it, p, r, scal) for it, p, r in zip(its, p32, got)]
            ssem, rsem, arrs, tok = _exchange_start(f"grads_fill_start_{k}", red, [1] * n, _fill_copies(its))
            return (3, its, (ssem, rsem, arrs)), tok
        ssem, rsem, arrs = st
        for it, g in zip(its, _exchange_wait(f"grads_fill_wait_{k}", arrs, ssem, rsem, _fill_copies(its), dx)):
            nm = it.src
            res[nm] = _adam(it, g, as_shards(w[nm], nm), as_shards(m[nm], nm), as_shards(v[nm], nm), prev=res.get(nm))
        return None, None

    def step_flight(dx):
        toks, left = [], []
        for k, entry in flight:
            entry, tok = advance(entry, k, dx)
            if entry is not None:
                left.append((k, entry))
                toks.append(tok)
        flight[:] = left
        return toks

    def grads_ready(k, pairs, dx):
        its = items[2 * k:2 * k + 2]
        toks = step_flight((dx,))
        g32 = [p[0].reshape(it.full3) for p, it in zip(pairs, its)]
        g16 = [p[1].reshape(it.full3) for p, it in zip(pairs, its)]
        land = [lax.empty(it.pair3, BF16) for it in its]
        ssem, rsem, arrs, tok = _exchange_start(f"grads_pair_start_{k}", g16 + land, [1] * len(its), _pair_copies(its))
        flight.append((k, (1, its, (ssem, rsem, arrs, g32))))
        return toks + [tok]

    sq, dx, grads = _local_step(x[0], loss_target[0], norm_full.reshape(2, 3, D), conv_b_pw1, dw_full, conv_b_dw, conv_norm_g,
                                conv_b_pw2, attn_q_norm[0], attn_k_norm[0], block_weights, grads_ready)
    loss = lax.psum(0.5 * jnp.sum(sq) / D, ("x", "y", "c"))
    k_last, entry = flight.pop()
    entry, tok = advance(entry, k_last, (dx,))
    while flight:
        step_flight((dx, tok))

    out_g, out_d, out_m, out_v = {}, {}, {}, {}

    parts = [_rows8(grads[n], D) for n in _SMALL]
    tot = _allreduce_small(jnp.concatenate(parts, axis=0))
    sg, r0 = {}, 0
    for n, p in zip(_SMALL, parts):
        last = grads[n].shape[-1]
        g = tot[r0:r0 + grads[n].size // min(last, D), :min(last, D)].reshape(-1, last)
        r0 += p.shape[0]
        if n in ("norm_g", "conv_w_dw"):
            g = lax.dynamic_slice_in_dim(g, me * (D // 4), D // 4, axis=1)
        sg[n] = g
    flat = lambda a: a.reshape(-1, a.shape[-1])
    ds, ms, vs = _adam_small([sg[n] for n in _SMALL], [flat(w[n]) for n in _SMALL], [flat(m[n]) for n in _SMALL],
                             [flat(v[n]) for n in _SMALL])
    for i, n in enumerate(_SMALL):
        out_g[n], out_d[n], out_m[n], out_v[n] = (a.reshape(w[n].shape) for a in (sg[n], ds[i], ms[i], vs[i]))

    behind = tuple(r[1] for r in res.values()) + tuple(ds)
    while entry is not None:
        entry, _ = advance(entry, k_last, behind)
    for n in _BIG:
        out_g[n], out_d[n], out_m[n], out_v[n] = (a.reshape(w[n].shape) for a in res[n])

    return (loss, dx[None], *[out_g[n] for n in _NAMES], *[out_d[n] for n in _NAMES], *[out_m[n] for n in _NAMES],
            *[out_v[n] for n in _NAMES])
```

```python
import functools

import jax
import jax.numpy as jnp
from jax import lax
from jax.experimental import pallas as pl
from jax.experimental.pallas import tpu as pltpu

F32 = jnp.float32
BF16 = jnp.bfloat16
MESH = pl.DeviceIdType.MESH

NORM_EPS = 1e-6
CONV_WIDTH = 31
ATTN_GROUPS = ((128, 1), (512, 4), (2048, 16))
ATTN_BLOCK = 128
HEAD_DIM = 128
N_CHIPS = 4

ADAM_LR = 0.001
ADAM_B1 = 0.9
ADAM_B2 = 0.999
ADAM_EPS = 1e-08
ADAM_WD = 0.01
ADAM_STEP = 10

VMEM_LIMIT = 56 * 1024 * 1024
NT_DIMS = (((1,), (1,)), ((), ()))
TN_DIMS = (((0,), (0,)), ((), ()))


def _pick(n, pref, mult):
    t = (min(n, pref) // mult) * mult
    while t >= mult:
        if n % t == 0:
            return t
        t -= mult
    return n


def _call(body, name, grid, in_specs, out_specs, out_shape, scratch=(), aliases=None, prefetch=0, after=()):
    params = pltpu.CompilerParams(dimension_semantics=("arbitrary",) * len(grid), vmem_limit_bytes=VMEM_LIMIT)
    after = tuple(after)
    if after:
        inner, n_in = body, prefetch + len(in_specs)

        def body(*refs):
            return inner(*refs[:n_in], *refs[n_in + len(after):])

        in_specs = list(in_specs) + [pl.BlockSpec(memory_space=pl.ANY)] * len(after)
    if prefetch:
        spec = pltpu.PrefetchScalarGridSpec(
            num_scalar_prefetch=prefetch, grid=grid, in_specs=in_specs, out_specs=out_specs, scratch_shapes=list(scratch)
        )
        call = pl.pallas_call(body, name=name, grid_spec=spec, out_shape=out_shape, compiler_params=params,
                              input_output_aliases=aliases or {})
    else:
        call = pl.pallas_call(body, name=name, grid=grid, in_specs=in_specs, out_specs=out_specs, out_shape=out_shape,
                              scratch_shapes=list(scratch), compiler_params=params, input_output_aliases=aliases or {})
    return lambda *args: call(*args, *after)


def _sds(shape, dtype):
    return jax.ShapeDtypeStruct(shape, dtype)


def _sig(x):
    return 1.0 / (1.0 + jnp.exp(-x))


def _rstd(x):
    return lax.rsqrt(jnp.mean(x * x, axis=-1, keepdims=True) + NORM_EPS)


def _norm_bwd(dy, xhat, r, g):
    dxh = dy * g
    return r * (dxh - xhat * jnp.mean(dxh * xhat, axis=-1, keepdims=True))


def _dot(a, b):
    return jnp.dot(a, b, preferred_element_type=F32)


def _dot_nt(a, b):
    return lax.dot_general(a, b, NT_DIMS, preferred_element_type=F32)


def _dot_tn(a, b):
    return lax.dot_general(a, b, TN_DIMS, preferred_element_type=F32)


def _ffn_fwd(x, g, w_in, w_out, name):
    T, D = x.shape
    F = w_out.shape[0]
    tm = _pick(T, 256, 8)

    def body(x_ref, g_ref, wg_ref, wu_ref, wo_ref, xo_ref, gate_ref, up_ref):
        xv = x_ref[...]
        h = (xv * _rstd(xv) * g_ref[...]).astype(BF16)
        gate = _dot(h, wg_ref[...])
        up = _dot(h, wu_ref[...])
        gate_ref[...] = gate.astype(BF16)
        up_ref[...] = up.astype(BF16)
        a = (gate * _sig(gate) * up).astype(BF16)
        xo_ref[...] = xv + 0.5 * _dot(a, wo_ref[...])

    row = lambda i: (i, 0)
    held = lambda shape, k: pl.BlockSpec(shape, lambda i: (0, k), pipeline_mode=pl.Buffered(1))
    return _call(
        body, name, (T // tm,),
        [pl.BlockSpec((tm, D), row), pl.BlockSpec((1, D), lambda i: (0, 0)), held((D, F), 0), held((D, F), 1), held((F, D), 0)],
        [pl.BlockSpec((tm, D), row), pl.BlockSpec((tm, F), row), pl.BlockSpec((tm, F), row)],
        [_sds((T, D), F32), _sds((T, F), BF16), _sds((T, F), BF16)],
    )(x, g, w_in, w_in, w_out)


def _ffn_bwd(x, g, dy, gate, up, w_in, w_out, name, after=()):
    T, D = x.shape
    F = w_out.shape[0]

    def body_a(dy_ref, gate_ref, up_ref, wo_ref, dgate_ref, dup_ref, a_ref):
        dyb = (0.5 * dy_ref[...]).astype(BF16)
        gate = gate_ref[...].astype(F32)
        up = up_ref[...].astype(F32)
        sg = _sig(gate)
        sl = gate * sg
        a_ref[...] = (sl * up).astype(BF16)
        da = _dot_nt(dyb, wo_ref[...])
        dgate_ref[...] = (da * up * (sg * (1.0 + gate * (1.0 - sg)))).astype(BF16)
        dup_ref[...] = (da * sl).astype(BF16)

    ta = _pick(T, 256, 8)
    tile = pl.BlockSpec((ta, F), lambda i: (i, 0))
    dgate, dup, a = _call(
        body_a, name + "_hidden", (T // ta,),
        [pl.BlockSpec((ta, D), lambda i: (i, 0)), tile, tile,
         pl.BlockSpec((F, D), lambda i: (0, 0), pipeline_mode=pl.Buffered(1))],
        [tile, tile, tile],
        [_sds((T, F), BF16)] * 3, after=after,
    )(dy, gate, up, w_out)

    def body_b(x_ref, g_ref, dy_ref, dgate_ref, dup_ref, wg_ref, wu_ref, dx_ref, dg_ref, h_ref, dyb_ref):
        @pl.when(pl.program_id(0) == 0)
        def _():
            dg_ref[...] = jnp.zeros_like(dg_ref)

        xv = x_ref[...]
        r = _rstd(xv)
        xh = xv * r
        h_ref[...] = (xh * g_ref[...]).astype(BF16)
        dyb_ref[...] = (0.5 * dy_ref[...]).astype(BF16)
        dh = _dot_nt(dgate_ref[...], wg_ref[...]) + _dot_nt(dup_ref[...], wu_ref[...])
        dx_ref[...] = dy_ref[...] + _norm_bwd(dh, xh, r, g_ref[...])
        dg_ref[...] += jnp.sum(dh * xh, axis=0, keepdims=True)

    tb = _pick(T, 512, 8)
    row = lambda i: (i, 0)
    held = lambda k: pl.BlockSpec((D, F), lambda i: (0, k), pipeline_mode=pl.Buffered(1))
    dx, dg, h, dyb = _call(
        body_b, name, (T // tb,),
        [pl.BlockSpec((tb, D), row), pl.BlockSpec((1, D), lambda i: (0, 0)), pl.BlockSpec((tb, D), row),
         pl.BlockSpec((tb, F), row), pl.BlockSpec((tb, F), row), held(0), held(1)],
        [pl.BlockSpec((tb, D), row), pl.BlockSpec((1, D), lambda i: (0, 0)), pl.BlockSpec((tb, D), row), pl.BlockSpec((tb, D), row)],
        [_sds((T, D), F32), _sds((1, D), F32), _sds((T, D), BF16), _sds((T, D), BF16)],
    )(x, g, dy, dgate, dup, w_in, w_in)
    return dx, dg, dgate, dup, a, h, dyb


def _mm_tn(a, b, bm, bn, out_shape, out_block, out_map, scale, name, prev=None):
    K, M = a.shape
    N = b.shape[1]

    def body(a_ref, b_ref, *rest):
        rest[-1][...] = (_dot_tn(a_ref[...].astype(BF16), b_ref[...].astype(BF16)) * scale).astype(BF16)

    in_specs = [pl.BlockSpec((K, bm), lambda mi, ni: (0, mi)), pl.BlockSpec((K, bn), lambda mi, ni: (0, ni))]
    args = [a, b]
    aliases = None
    if prev is not None:
        in_specs += [pl.BlockSpec(memory_space=pl.ANY)]
        args += [prev]
        aliases = {2: 0}
    return _call(body, name, (M // bm, N // bn), in_specs, [pl.BlockSpec(out_block, out_map)],
                 [_sds(out_shape, BF16)], aliases=aliases)(*args)[0]


def _mm_tn_parts(a, bs, bn, cols, col0, name, prev=None):
    K, M = a.shape
    N = bs[0].shape[1]
    nt = N // bn
    ns = len(bs)

    def body(a_ref, *refs):
        o_ref = refs[-1]
        part = pl.program_id(0) // nt
        for s in range(ns):
            @pl.when(part == s)
            def _(s=s):
                o_ref[...] = _dot_tn(a_ref[...], refs[s][...]).astype(BF16)

    part_spec = lambda s: pl.BlockSpec((K, bn), lambda j: (0, jnp.clip(j - s * nt, 0, nt - 1)))
    in_specs = [pl.BlockSpec((K, M), lambda j: (0, 0))] + [part_spec(s) for s in range(ns)]
    args = [a, *bs]
    aliases = None
    if prev is not None:
        in_specs += [pl.BlockSpec(memory_space=pl.ANY)]
        args += [prev]
        aliases = {ns + 1: 0}
    return _call(body, name, (ns * nt,), in_specs, [pl.BlockSpec((M, bn), lambda j: (0, col0 // bn + j))],
                 [_sds((M, cols), BF16)], aliases=aliases)(*args)[0]


def _conv_pre(x, g, w1, b1, name):
    T, D = x.shape
    tm = _pick(T, 512, 8)

    def body(x_ref, g_ref, w_ref, b_ref, ag_ref, u_ref, h_ref):
        xv = x_ref[...]
        h = (xv * _rstd(xv) * g_ref[...]).astype(BF16)
        h_ref[...] = h
        ag = _dot(h, w_ref[...]) + b_ref[...]
        ag_ref[...] = ag.astype(BF16)
        u_ref[...] = ag[:, :D] * _sig(ag[:, D:])

    return _call(
        body, name, (T // tm,),
        [pl.BlockSpec((tm, D), lambda i: (i, 0)), pl.BlockSpec((1, D), lambda i: (0, 0)),
         pl.BlockSpec((D, 2 * D), lambda i: (0, 0)), pl.BlockSpec((1, 2 * D), lambda i: (0, 0))],
        [pl.BlockSpec((tm, 2 * D), lambda i: (i, 0)), pl.BlockSpec((tm, D), lambda i: (i, 0)),
         pl.BlockSpec((tm, D), lambda i: (i, 0))],
        [_sds((T, 2 * D), BF16), _sds((T, D), F32), _sds((T, D), BF16)],
    )(x, g, w1, b1)


_DW_PAD = 32
_DW_CHUNK = 256


def _dwconv(u, w, b, name):
    T, D = u.shape
    K = w.shape[0]
    ch = _pick(T, _DW_CHUNK, 8)
    lead = _DW_PAD - (K - 1)

    def body(u_ref, w_ref, b_ref, c_ref, ext):
        ext[pl.ds(0, _DW_PAD), :] = jnp.zeros((_DW_PAD, 128), F32)
        ext[pl.ds(_DW_PAD, T), :] = u_ref[...]
        for c0 in range(0, T, ch):
            acc = jnp.zeros((ch, 128), F32) + b_ref[...]
            for k in range(K):
                acc = acc + w_ref[pl.ds(k, 1), :] * ext[pl.ds(c0 + lead + k, ch), :]
            c_ref[pl.ds(c0, ch), :] = acc

    return _call(
        body, name, (D // 128,),
        [pl.BlockSpec((T, 128), lambda i: (0, i)), pl.BlockSpec((K, 128), lambda i: (0, i)),
         pl.BlockSpec((1, 128), lambda i: (0, i))],
        [pl.BlockSpec((T, 128), lambda i: (0, i))],
        [_sds((T, D), F32)],
        scratch=[pltpu.VMEM((T + _DW_PAD, 128), F32)],
    )(u, w, b)[0]


def _dwconv_bwd(dc, u, w, name):
    T, D = u.shape
    K = w.shape[0]
    ch = _pick(T, _DW_CHUNK, 8)
    lead = _DW_PAD - (K - 1)

    def body(dc_ref, u_ref, w_ref, du_ref, dw_ref, db_ref, uext, dext):
        uext[pl.ds(0, _DW_PAD), :] = jnp.zeros((_DW_PAD, 128), F32)
        uext[pl.ds(_DW_PAD, T), :] = u_ref[...]
        dext[pl.ds(0, T), :] = dc_ref[...]
        dext[pl.ds(T, _DW_PAD), :] = jnp.zeros((_DW_PAD, 128), F32)
        dws =[jnp.zeros((8, 128), F32) for _ in range(K)]
        dbs = jnp.zeros((8, 128), F32)
        for c0 in range(0, T, ch):
            dcv = dext[pl.ds(c0, ch), :]
            dbs = dbs + jnp.sum(dcv.reshape(ch // 8, 8, 128), axis=0)
            acc = jnp.zeros((ch, 128), F32)
            for k in range(K):
                acc = acc + w_ref[pl.ds(k, 1), :] * dext[pl.ds(c0 + (K - 1) - k, ch), :]
                prod = dcv * uext[pl.ds(c0 + lead + k, ch), :]
                dws[k] = dws[k] + jnp.sum(prod.reshape(ch // 8, 8, 128), axis=0)
            du_ref[pl.ds(c0, ch), :] = acc
        for k in range(K):
            dw_ref[pl.ds(k, 1), :] = jnp.sum(dws[k], axis=0, keepdims=True)
        db_ref[...] = jnp.sum(dbs, axis=0, keepdims=True)

    return _call(
        body, name, (D // 128,),
        [pl.BlockSpec((T, 128), lambda i: (0, i)), pl.BlockSpec((T, 128), lambda i: (0, i)),
         pl.BlockSpec((K, 128), lambda i: (0, i))],
        [pl.BlockSpec((T, 128), lambda i: (0, i)), pl.BlockSpec((K, 128), lambda i: (0, i)),
         pl.BlockSpec((1, 128), lambda i: (0, i))],
        [_sds((T, D), F32), _sds((K, D), F32), _sds((1, D), F32)],
        scratch=[pltpu.VMEM((T + _DW_PAD, 128), F32), pltpu.VMEM((T + _DW_PAD, 128), F32)],
    )(dc, u, w)


def _conv_post(c, x, ng, w2, b2, name):
    T, D = x.shape
    tm = _pick(T, 512, 8)

    def body(c_ref, x_ref, ng_ref, w_ref, b_ref, xo_ref, s_ref):
        cv = c_ref[...]
        n = cv * _rstd(cv) * ng_ref[...]
        s = (n * _sig(n)).astype(BF16)
        s_ref[...] = s
        xo_ref[...] = x_ref[...] + _dot(s, w_ref[...]) + b_ref[...]

    row = lambda i: (i, 0)
    fix = lambda i: (0, 0)
    return _call(
        body, name, (T // tm,),
        [pl.BlockSpec((tm, D), row), pl.BlockSpec((tm, D), row), pl.BlockSpec((1, D), fix),
         pl.BlockSpec((D, D), fix), pl.BlockSpec((1, D), fix)],
        [pl.BlockSpec((tm, D), row), pl.BlockSpec((tm, D), row)],
        [_sds((T, D), F32), _sds((T, D), BF16)],
    )(c, x, ng, w2, b2)


def _conv_post_bwd(dy, c, ng, w2, name, after=()):
    T, D = dy.shape
    tm = _pick(T, 512, 8)

    def body(dy_ref, c_ref, ng_ref, w_ref, dc_ref, dng_ref, db_ref):
        @pl.when(pl.program_id(0) == 0)
        def _():
            dng_ref[...] = jnp.zeros_like(dng_ref)
            db_ref[...] = jnp.zeros_like(db_ref)

        dyv = dy_ref[...]
        ds = _dot_nt(dyv.astype(BF16), w_ref[...])
        cv = c_ref[...]
        r = _rstd(cv)
        ch = cv * r
        n = ch * ng_ref[...]
        sg = _sig(n)
        dn = ds * (sg * (1.0 + n * (1.0 - sg)))
        dc_ref[...] = _norm_bwd(dn, ch, r, ng_ref[...])
        dng_ref[...] += jnp.sum(dn * ch, axis=0, keepdims=True)
        db_ref[...] += jnp.sum(dyv, axis=0, keepdims=True)

    row = lambda i: (i, 0)
    fix = lambda i: (0, 0)
    return _call(
        body, name, (T // tm,),
        [pl.BlockSpec((tm, D), row), pl.BlockSpec((tm, D), row), pl.BlockSpec((1, D), fix), pl.BlockSpec((D, D), fix)],
        [pl.BlockSpec((tm, D), row), pl.BlockSpec((1, D), fix), pl.BlockSpec((1, D), fix)],
        [_sds((T, D), F32), _sds((1, D), F32), _sds((1, D), F32)], after=after,
    )(dy, c, ng, w2)


def _conv_pre_bwd(du, ag, x, g, dy, w1, name):
    T, D = x.shape
    tm = _pick(T, 512, 8)

    def body(du_ref, ag_ref, x_ref, g_ref, dy_ref, w_ref, dx_ref, dg_ref, dag_ref, db_ref):
        @pl.when(pl.program_id(0) == 0)
        def _():
            dg_ref[...] = jnp.zeros_like(dg_ref)
            db_ref[...] = jnp.zeros_like(db_ref)

        duv = du_ref[...]
        a = ag_ref[:, :D].astype(F32)
        gt = ag_ref[:, D:].astype(F32)
        sg = _sig(gt)
        da = duv * sg
        dgt = duv * a * sg * (1.0 - sg)
        db_ref[:, :D] += jnp.sum(da, axis=0, keepdims=True)
        db_ref[:, D:] += jnp.sum(dgt, axis=0, keepdims=True)
        dab = da.astype(BF16)
        dgb = dgt.astype(BF16)
        dag_ref[:, :D] = dab
        dag_ref[:, D:] = dgb
        dh = _dot_nt(dab, w_ref[:, :D]) + _dot_nt(dgb, w_ref[:, D:])
        xv = x_ref[...]
        r = _rstd(xv)
        xh = xv * r
        dx_ref[...] = dy_ref[...] + _norm_bwd(dh, xh, r, g_ref[...])
        dg_ref[...] += jnp.sum(dh * xh, axis=0, keepdims=True)

    row = lambda i: (i, 0)
    fix = lambda i: (0, 0)
    return _call(
        body, name, (T // tm,),
        [pl.BlockSpec((tm, D), row), pl.BlockSpec((tm, 2 * D), row), pl.BlockSpec((tm, D), row), pl.BlockSpec((1, D), fix),
         pl.BlockSpec((tm, D), row), pl.BlockSpec((D, 2 * D), fix)],
        [pl.BlockSpec((tm, D), row), pl.BlockSpec((1, D), fix), pl.BlockSpec((tm, 2 * D), row),
         pl.BlockSpec((1, 2 * D), fix)],
        [_sds((T, D), F32), _sds((1, D), F32), _sds((T, 2 * D), BF16), _sds((1, 2 * D), F32)],
    )(du, ag, x, g, dy, w1)


def _row_sums(a):
    return _dot(a.astype(BF16), jnp.ones((a.shape[1], 128), BF16))


def _head_rstd(x):
    return lax.rsqrt(_row_sums(x * x) * (1.0 / x.shape[1]) + NORM_EPS)


def _attn_qkv(x, g, wqkv, qn, kn, name):
    T, D = x.shape
    N = wqkv.shape[1]
    tn = N // 9
    E = HEAD_DIM
    tm = _pick(T, 256, 8)
    ng = qn.shape[0]

    def body(x_ref, g_ref, w_ref, qn_ref, kn_ref, o_ref, a_ref, h_ref):
        xv = x_ref[...]
        h = (xv * _rstd(xv) * g_ref[...]).astype(BF16)
        h_ref[...] = h
        for j in range(9):
            cs = slice(j * tn, (j + 1) * tn)
            res = _dot(h, w_ref[:, cs])
            o_ref[:, cs] = res.astype(BF16)
            if j % 3 == 2:
                a_ref[:, cs] = res.astype(BF16)
                continue
            grp = j // 3
            fac = qn_ref[grp:grp + 1, :] * kn_ref[grp:grp + 1, :] * (E ** -0.5) if j % 3 == 0 else None
            for h_i in range(tn // E):
                hs = slice(h_i * E, (h_i + 1) * E)
                xh = res[:, hs]
                hat = xh * _head_rstd(xh)
                a_ref[:, j * tn + h_i * E:j * tn + (h_i + 1) * E] = (hat if fac is None else hat * fac).astype(BF16)

    row = lambda i: (i, 0)
    fix = lambda i: (0, 0)
    return _call(
        body, name, (T // tm,),
        [pl.BlockSpec((tm, D), row), pl.BlockSpec((1, D), fix), pl.BlockSpec((D, N), fix, pipeline_mode=pl.Buffered(1)),
         pl.BlockSpec((ng, E), fix), pl.BlockSpec((ng, E), fix)],
        [pl.BlockSpec((tm, N), row), pl.BlockSpec((tm, N), row), pl.BlockSpec((tm, D), row)],
        [_sds((T, N), BF16), _sds((T, N), BF16), _sds((T, D), BF16)],
    )(x, g, wqkv, qn, kn)


def _band_mask(q, steps, nblk):
    i = lax.broadcasted_iota(jnp.int32, (q, 2 * q), 0)
    j = lax.broadcasted_iota(jnp.int32, (q, 2 * q), 1)
    diff = q + i - j
    first_key = jnp.where(nblk > 0, 0, q)
    return (diff >= 0) & (diff <= steps) & (j >= first_key)


def _per_row(blk, width):
    e = blk.shape[1]
    if width % e == 0:
        return jnp.concatenate([blk] * (width // e), axis=1)
    return jnp.broadcast_to(blk[:, :1], (blk.shape[0], width))


def _streams(a, dil, to_streams, name, col0=0, ncols=None):
    T, C = a.shape
    ncols = C if ncols is None else ncols
    Q = ATTN_BLOCK
    run = Q * dil
    reps = max(1, min(2048 // run, T // run))
    while T % (run * reps):
        reps -= 1
    rows = run * reps
    cw = _pick(ncols, 512, 128)
    ns = cw // 128

    def body(a_ref, o_ref, scr):
        for s in range(ns):
            ls = slice(s * 128, (s + 1) * 128)
            slab = scr.at[s]
            if to_streams:
                slab[...] = a_ref[:, ls].astype(F32)
                for u in range(reps):
                    for r in range(dil):
                        o_ref[pl.ds(u * run + r * Q, Q), ls] = slab[pl.ds(u * run + r, Q, stride=dil), :].astype(a.dtype)
            else:
                for u in range(reps):
                    for r in range(dil):
                        slab[pl.ds(u * run + r, Q, stride=dil), :] = a_ref[pl.ds(u * run + r * Q, Q), ls].astype(F32)
                o_ref[:, ls] = slab[...].astype(a.dtype)

    return _call(
        body, name, (T // rows, ncols // cw),
        [pl.BlockSpec((rows, cw), lambda i, j: (i, col0 // cw + j))],
        [pl.BlockSpec((rows, cw), lambda i, j: (i, j))],
        [_sds((T, ncols), a.dtype)],
        scratch=[pltpu.VMEM((ns, rows, 128), F32)],
    )(a)[0]


def _attn_fwd(qkv, base, HE, window, dil, name):
    T = qkv.shape[0]
    H = HE // HEAD_DIM
    E = HEAD_DIM
    Q = ATTN_BLOCK
    nb = T // dil // Q
    steps = window // dil

    def body(q_ref, kc_ref, kp_ref, vc_ref, vp_ref, o_ref, l_ref):
        n = pl.program_id(1)
        valid = _band_mask(Q, steps, n)
        ones = jnp.ones((2 * Q, E), BF16)
        outs, lses = [], []
        for h in range(H):
            hs = slice(h * E, (h + 1) * E)
            k2 = jnp.concatenate([kp_ref[:, hs], kc_ref[:, hs]], axis=0)
            v2 = jnp.concatenate([vp_ref[:, hs], vc_ref[:, hs]], axis=0)
            s = jnp.where(valid, _dot_nt(q_ref[:, hs], k2), -1e30)
            m = jnp.max(s, axis=-1, keepdims=True)
            p = jnp.exp(s - m).astype(BF16)
            acc = _dot(p, jnp.concatenate([v2, ones], axis=1))
            l = acc[:, E:]
            outs.append((acc[:, :E] * (1.0 / l)).astype(o_ref.dtype))
            lses.append(m + jnp.log(l))
        o_ref[...] = jnp.concatenate(outs, axis=1)
        l_ref[...] = jnp.concatenate(lses, axis=1)

    blk = lambda s, back: pl.BlockSpec((Q, HE), lambda r, n: (jnp.maximum(n - back, 0) * dil + r, base + s))
    out = pl.BlockSpec((Q, HE), lambda r, n: (n * dil + r, 0))
    return _call(
        body, name, (dil, nb),
        [blk(0, 0), blk(1, 0), blk(1, 1), blk(2, 0), blk(2, 1)],
        [out, out],
        [_sds((T, HE), BF16), _sds((T, HE), F32)],
    )(qkv, qkv, qkv, qkv, qkv)


def _attn_merge(os, lses, x, wo, name):
    T, D = x.shape
    HE = wo.shape[0]
    tm = _pick(T, 512, 8)
    ng = len(os)

    def body(*refs):
        o_refs = refs[:ng]
        l_refs = refs[ng:2 * ng]
        x_ref, w_ref, xo_ref, om_ref, lt_ref = refs[2 * ng:]
        ls = [r[...] for r in l_refs]
        m = functools.reduce(jnp.maximum, ls)
        es = [jnp.exp(l - m) for l in ls]
        tot = functools.reduce(lambda a, b: a + b, es)
        inv = 1.0 / tot
        om = functools.reduce(lambda a, b: a + b, [e * inv * r[...] for e, r in zip(es, o_refs)])
        omb = om.astype(BF16)
        om_ref[...] = omb
        lt_ref[...] = m + jnp.log(tot)
        xo_ref[...] = x_ref[...] + _dot(omb, w_ref[...])

    row = lambda i: (i, 0)
    fix = lambda i: (0, 0)
    return _call(
        body, name, (T // tm,),
        [pl.BlockSpec((tm, HE), row)] * (2 * ng) + [pl.BlockSpec((tm, D), row), pl.BlockSpec((HE, D), fix)],
        [pl.BlockSpec((tm, D), row), pl.BlockSpec((tm, HE), row), pl.BlockSpec((tm, HE), row)],
        [_sds((T, D), F32), _sds((T, HE), BF16), _sds((T, HE), F32)],
    )(*os, *lses, x, wo)


def _attn_out_bwd(dy, om, wo, name, after=()):
    T, D = dy.shape
    HE = wo.shape[0]
    E = HEAD_DIM
    tm = _pick(T, 512, 8)

    def body(dy_ref, om_ref, w_ref, dom_ref, dl_ref):
        dom = _dot_nt(dy_ref[...].astype(BF16), w_ref[...])
        dom_ref[...] = dom.astype(BF16)
        prod = dom * om_ref[...].astype(F32)
        for h in range(HE // E):
            hs = slice(h * E, (h + 1) * E)
            dl_ref[:, hs] = jnp.broadcast_to(jnp.sum(prod[:, hs], axis=-1, keepdims=True), (tm, E))

    row = lambda i: (i, 0)
    return _call(
        body, name, (T // tm,),
        [pl.BlockSpec((tm, D), row), pl.BlockSpec((tm, HE), row), pl.BlockSpec((HE, D), lambda i: (0, 0))],
        [pl.BlockSpec((tm, HE), row), pl.BlockSpec((tm, HE), row)],
        [_sds((T, HE), BF16), _sds((T, HE), F32)], after=after,
    )(dy, om, wo)


def _attn_bwd(qkv, base, HE, dom, lse, delta, window, dil, name):
    T = qkv.shape[0]
    H = HE // HEAD_DIM
    E = HEAD_DIM
    Q = ATTN_BLOCK
    nb = T // dil // Q
    steps = window // dil

    def body(q_ref, kc_ref, kp_ref, vc_ref, vp_ref, do_ref, l_ref, dl_ref, dq_ref, dk_ref, dv_ref, ck_sc, cv_sc):
        n = pl.program_id(1)

        @pl.when(n == 0)
        def _():
            ck_sc[...] = jnp.zeros_like(ck_sc)
            cv_sc[...] = jnp.zeros_like(cv_sc)

        @pl.when(n < nb)
        def _():
            valid = _band_mask(Q, steps, n)
            ck_old = ck_sc[...]
            cv_old = cv_sc[...]
            dqs, dks, dvs = [], [], []
            for h in range(H):
                hs = slice(h * E, (h + 1) * E)
                q = q_ref[:, hs]
                do = do_ref[:, hs]
                k2 = jnp.concatenate([kp_ref[:, hs], kc_ref[:, hs]], axis=0)
                v2 = jnp.concatenate([vp_ref[:, hs], vc_ref[:, hs]], axis=0)
                p = jnp.where(valid, jnp.exp(_dot_nt(q, k2) - _per_row(l_ref[:, hs], 2 * Q)), 0.0)
                ds = (p * (_dot_nt(do, v2) - _per_row(dl_ref[:, hs], 2 * Q))).astype(BF16)
                dqs.append(_dot(ds, k2).astype(BF16))
                dks.append(_dot_tn(ds, q))
                dvs.append(_dot_tn(p.astype(BF16), do))
            cat = lambda parts: jnp.concatenate(parts, axis=1)
            dq_ref[...] = cat(dqs)
            dk_ref[...] = (ck_old + cat([d[:Q] for d in dks])).astype(BF16)
            dv_ref[...] = (cv_old + cat([d[:Q] for d in dvs])).astype(BF16)
            ck_sc[...] = cat([d[Q:] for d in dks])
            cv_sc[...] = cat([d[Q:] for d in dvs])

        @pl.when(n == nb)
        def _():
            dk_ref[...] = ck_sc[...].astype(BF16)
            dv_ref[...] = cv_sc[...].astype(BF16)

    nq = lambda n: jnp.minimum(n, nb - 1)
    blk = lambda s, back: pl.BlockSpec((Q, HE), lambda r, n: (jnp.maximum(nq(n) - back, 0) * dil + r, base + s))
    qblk = pl.BlockSpec((Q, HE), lambda r, n: (nq(n) * dil + r, 0))
    kblk = pl.BlockSpec((Q, HE), lambda r, n: (jnp.maximum(n - 1, 0) * dil + r, 0))
    return _call(
        body, name, (dil, nb + 1),
        [blk(0, 0), blk(1, 0), blk(1, 1), blk(2, 0), blk(2, 1), qblk, qblk, qblk],
        [qblk, kblk, kblk],
        [_sds((T, HE), BF16)] * 3,
        scratch=[pltpu.VMEM((Q, HE), F32), pltpu.VMEM((Q, HE), F32)],
    )(qkv, qkv, qkv, qkv, qkv, dom, lse, delta)


def _qk_norm_bwd(dq, dk, qkv, base, HE, gq, gk, name):
    T = dq.shape[0]
    E = HEAD_DIM
    tm = _pick(T, 512, 8)
    scale = E ** -0.5

    def body(dq_ref, dk_ref, q_ref, k_ref, gq_ref, gk_ref, oq_ref, ok_ref, dgq_ref, dgk_ref):
        @pl.when(pl.program_id(0) == 0)
        def _():
            dgq_ref[...] = jnp.zeros_like(dgq_ref)
            dgk_ref[...] = jnp.zeros_like(dgk_ref)

        gqv = gq_ref[...]
        gkv = gk_ref[...]
        c = gqv * gkv * scale
        dc = jnp.zeros((1, E), F32)
        for h in range(HE // E):
            hs = slice(h * E, (h + 1) * E)
            q = q_ref[:, hs].astype(F32)
            rq = _head_rstd(q)
            qh = q * rq
            a = dq_ref[:, hs].astype(F32)
            dc = dc + jnp.sum(a * qh, axis=0, keepdims=True)
            dqh = a * c
            oq_ref[:, hs] = (rq * (dqh - qh * (_row_sums(dqh * qh) * (1.0 / E)))).astype(BF16)
            k = k_ref[:, hs].astype(F32)
            rk = _head_rstd(k)
            kh = k * rk
            b = dk_ref[:, hs].astype(F32)
            ok_ref[:, hs] = (rk * (b - kh * (_row_sums(b * kh) * (1.0 / E)))).astype(BF16)
        dgq_ref[...] += dc * (gkv * scale)
        dgk_ref[...] += dc * (gqv * scale)

    row = lambda i: (i, 0)
    vec = pl.BlockSpec((1, E), lambda i: (0, 0))
    return _call(
        body, name, (T // tm,),
        [pl.BlockSpec((tm, HE), row), pl.BlockSpec((tm, HE), row), pl.BlockSpec((tm, HE), lambda i: (i, base)),
         pl.BlockSpec((tm, HE), lambda i: (i, base + 1)), vec, vec],
        [pl.BlockSpec((tm, HE), row), pl.BlockSpec((tm, HE), row), vec, vec],
        [_sds((T, HE), BF16), _sds((T, HE), BF16), _sds((1, E), F32), _sds((1, E), F32)],
    )(dq, dk, qkv, qkv, gq, gk)


def _attn_qkv_bwd(dqkv, x, g, dy, wqkv, name):
    T, D = x.shape
    HE = wqkv.shape[1] // 9
    tm = _pick(T, 256, 8)

    def body(*refs):
        d_refs = refs[:9]
        x_ref, g_ref, dy_ref, w_ref, dx_ref, dg_ref = refs[9:]

        @pl.when(pl.program_id(0) == 0)
        def _():
            dg_ref[...] = jnp.zeros_like(dg_ref)

        dh = _dot_nt(d_refs[0][...], w_ref[:, :HE])
        for s in range(1, 9):
            dh = dh + _dot_nt(d_refs[s][...], w_ref[:, s * HE:(s + 1) * HE])
        xv = x_ref[...]
        r = _rstd(xv)
        xh = xv * r
        dx_ref[...] = dy_ref[...] + _norm_bwd(dh, xh, r, g_ref[...])
        dg_ref[...] += jnp.sum(dh * xh, axis=0, keepdims=True)

    row = lambda i: (i, 0)
    fix = lambda i: (0, 0)
    return _call(
        body, name, (T // tm,),
        [pl.BlockSpec((tm, HE), row)] * 9 + [pl.BlockSpec((tm, D), row), pl.BlockSpec((1, D), fix), pl.BlockSpec((tm, D), row),
                                           pl.BlockSpec((D, 9 * HE), fix, pipeline_mode=pl.Buffered(1))],
        [pl.BlockSpec((tm, D), row), pl.BlockSpec((1, D), fix)],
        [_sds((T, D), F32), _sds((1, D), F32)],
    )(*dqkv, x, g, dy, wqkv)


def _loss_head(y, target, name):
    T, D = y.shape
    tm = _pick(T, 512, 8)

    def body(y_ref, t_ref, dy_ref, sq_ref):
        @pl.when(pl.program_id(0) == 0)
        def _():
            sq_ref[...] = jnp.zeros_like(sq_ref)

        err = y_ref[...] - t_ref[...]
        dy_ref[...] = err * (1.0 / D)
        sq_ref[...] += jnp.sum(err * err, axis=0, keepdims=True)

    row = lambda i: (i, 0)
    return _call(
        body, name, (T // tm,),
        [pl.BlockSpec((tm, D), row), pl.BlockSpec((tm, D), row)],
        [pl.BlockSpec((tm, D), row), pl.BlockSpec((1, D), lambda i: (0, 0))],
        [_sds((T, D), F32), _sds((1, D), F32)],
    )(y, target)


def _local_step(x, target, norm_g, b_pw1, w_dw, b_dw, cng, b_pw2, qn, kn, block_weights, grads_ready):
    T, D = x.shape
    ng = lambda l, k: norm_g[l, k][None, :]
    bn = _pick(D, 256, 128)

    w_in, w_out = [None] * 4, [None] * 4
    w_in[0], w_out[0] = block_weights(0, x)
    x1, *gu0 = _ffn_fwd(x, ng(0, 0), w_in[0], w_out[0], "ffn_fwd_0")
    pw1, pw2 = block_weights(1, x1)
    ag, u, hc = _conv_pre(x1, ng(0, 1), pw1, b_pw1, "conv_pre")
    c = _dwconv(u, w_dw, b_dw, "dwconv")
    x2, s = _conv_post(c, x1, cng, pw2, b_pw2, "conv_post")
    w_in[1], w_out[1] = block_weights(2, x2)
    x3, *gu1 = _ffn_fwd(x2, ng(0, 2), w_in[1], w_out[1], "ffn_fwd_1")
    w_in[2], w_out[2] = block_weights(3, x3)
    x4, *gu2 = _ffn_fwd(x3, ng(1, 0), w_in[2], w_out[2], "ffn_fwd_2")
    wqkv, wo = block_weights(4, x4)
    HE = wo.shape[0]
    F = w_out[0].shape[0]
    bf = _pick(F, 256, 128)
    bh = _pick(HE, 512, 128)
    qkv, att, ha = _attn_qkv(x4, ng(1, 1), wqkv, qn, kn, "attn_qkv")
    qkv_s = [(att, 3 * gi) if dil == 1 else (_streams(att, dil, True, f"qkv_streams_{gi}", 3 * gi * HE, 3 * HE), 0)
             for gi, (_, dil) in enumerate(ATTN_GROUPS)]
    tokens = lambda a, dil, name: a if dil == 1 else _streams(a, dil, False, name)
    streams = lambda a, dil, name: a if dil == 1 else _streams(a, dil, True, name)
    os, lses = [], []
    for gi, (window, dil) in enumerate(ATTN_GROUPS):
        o, l = _attn_fwd(*qkv_s[gi], HE, window, dil, f"attn_fwd_{gi}")
        os.append(tokens(o, dil, f"o_tokens_{gi}"))
        lses.append(tokens(l, dil, f"lse_tokens_{gi}"))
    x5, om, lse = _attn_merge(os, lses, x4, wo, "attn_merge")
    w_in[3], w_out[3] = block_weights(5, x5)
    x6, *gu3 = _ffn_fwd(x5, ng(1, 2), w_in[3], w_out[3], "ffn_fwd_3")
    dy, sq = _loss_head(x6, target, "loss_head")

    grads = {"ffn_w_in": [None] * 4, "ffn_w_out": [None] * 4}
    dnorm = [[None] * 3 for _ in range(2)]

    def ffn_back(lf, k, xin, gvec, dy, gu, after):
        dx, dg, dgate, dup, a, h, dyb = _ffn_bwd(xin, gvec, dy, gu[0], gu[1], w_in[lf], w_out[lf], f"ffn_bwd_{lf}", after=after)
        grads["ffn_w_in"][lf] = _mm_tn_parts(h, [dgate, dup], bf, 2 * F, 0, f"ffn_dw_in_{lf}")
        grads["ffn_w_out"][lf] = _mm_tn(a, dyb, bf, D, (F, D), (bf, D), lambda mi, ni: (mi, 0), 1.0, f"ffn_dw_out_{lf}")
        return dx, dg, grads_ready(k, (grads["ffn_w_in"][lf], grads["ffn_w_out"][lf]), dx)

    dx, dnorm[1][2], tok = ffn_back(3, 5, x5, ng(1, 2), dy, gu3, ())
    dom, delta = _attn_out_bwd(dx, om, wo, "attn_out_bwd", after=tok)
    grads["attn_w_o"] = _mm_tn(om, dx, HE, bn, (HE, D), (HE, bn), lambda mi, ni: (0, ni), 1.0, "attn_dw_o")
    dqkv, dgq, dgk = [], [], []
    for gi, (window, dil) in enumerate(ATTN_GROUPS):
        ds = _attn_bwd(*qkv_s[gi], HE, streams(dom, dil, f"dom_streams_{gi}"), streams(lse, dil, f"lse_streams_{gi}"),
                       streams(delta, dil, f"delta_streams_{gi}"), window, dil, f"attn_bwd_{gi}")
        dq, dk, dv = [tokens(d, dil, f"d{nm}_tokens_{gi}") for d, nm in zip(ds, "qkv")]
        dq, dk, a_, b_ = _qk_norm_bwd(dq, dk, qkv, 3 * gi, HE, qn[gi][None, :], kn[gi][None, :], f"qk_norm_bwd_{gi}")
        dqkv += [dq, dk, dv]
        dgq.append(a_)
        dgk.append(b_)
    grads["attn_q_norm"] = jnp.concatenate(dgq, axis=0)
    grads["attn_k_norm"] = jnp.concatenate(dgk, axis=0)
    d_qkv = None
    for gi in range(len(ATTN_GROUPS)):
        d_qkv = _mm_tn_parts(ha, dqkv[3 * gi:3 * gi + 3], bh, 9 * HE, 3 * gi * HE, f"attn_dw_qkv_{gi}", prev=d_qkv)
    grads["attn_w_qkv"] = d_qkv
    dx, dnorm[1][1] = _attn_qkv_bwd(dqkv, x4, ng(1, 1), dx, wqkv, "attn_qkv_bwd")
    tok = grads_ready(4, (grads["attn_w_qkv"], grads["attn_w_o"]), dx)
    dx, dnorm[1][0], tok = ffn_back(2, 3, x3, ng(1, 0), dx, gu2, tok)

    dx, dnorm[0][2], tok = ffn_back(1, 2, x2, ng(0, 2), dx, gu1, tok)
    dc, grads["conv_norm_g"], grads["conv_b_pw2"] = _conv_post_bwd(dx, c, cng, pw2, "conv_post_bwd", after=tok)
    grads["conv_w_pw2"] = _mm_tn(s, dx, D, bn, (D, D), (D, bn), lambda mi, ni: (0, ni), 1.0, "conv_dw_pw2")
    du, grads["conv_w_dw"], grads["conv_b_dw"] = _dwconv_bwd(dc, u, w_dw, "dwconv_bwd")
    dx, dnorm[0][1], dag, grads["conv_b_pw1"] = _conv_pre_bwd(du, ag, x1, ng(0, 1), dx, pw1, "conv_pre_bwd")
    grads["conv_w_pw1"] = _mm_tn(hc, dag, D, 2 * bn, (D, 2 * D), (D, 2 * bn), lambda mi, ni: (0, ni), 1.0, "conv_dw_pw1")
    tok = grads_ready(1, (grads["conv_w_pw1"], grads["conv_w_pw2"]), dx)
    dx, dnorm[0][0], _ = ffn_back(0, 0, x, ng(0, 0), dx, gu0, tok)

    grads["norm_g"] = jnp.concatenate([jnp.concatenate(r, axis=0)[None] for r in dnorm], axis=0)
    return sq, dx, grads


class _Sharded:
    def __init__(self, name, full3, half_axis, shard_axis, src, slab=None):
        self.name, self.full3, self.half_axis, self.shard_axis = name, tuple(full3), half_axis, shard_axis
        self.src, self.slab = src, slab

    def source(self, refs):
        return refs[self.src] if self.slab is None else refs[self.src].at[self.slab]

    def _cut(self, shape, axis, parts):
        s = list(shape)
        s[axis] //= parts
        return tuple(s)

    @property
    def shard3(self):
        return self._cut(self.full3, self.shard_axis, N_CHIPS)

    @property
    def pair3(self):
        return self._cut(self.full3, self.half_axis, 2)

    @property
    def part3(self):
        return self._cut(self.shard3, self.half_axis, 2)

    @staticmethod
    def _slice(ref, axis, idx, parts):
        n = ref.shape[axis] // parts
        start = idx * n
        minor = len(ref.shape) - 1 - axis
        if minor < 2 and not isinstance(start, int):
            start = pl.multiple_of(start, 128 if minor == 0 else (16 if n % 16 == 0 else 8))
        sl = [slice(None)] * len(ref.shape)
        sl[axis] = pl.ds(start, n)
        return ref.at[tuple(sl)]

    def half(self, ref, h):
        return self._slice(ref, self.half_axis, h, 2)

    def shard(self, ref, j):
        return self._slice(ref, self.shard_axis, j, N_CHIPS)


def _place():
    x, y, c = lax.axis_index("x"), lax.axis_index("y"), lax.axis_index("c")
    return x, y, c, 2 * x + y


_RELS = (1, 2, 3)


def _peer(x, y, rel):
    px = 1 - x if rel & 2 else x
    py = 1 - y if rel & 1 else y
    return px, py, 2 * px + py


ANY = pl.BlockSpec(memory_space=pl.ANY)


def _comm_call(body, name, n_in, out_shape, n_sems, aliases=None):
    return pl.pallas_call(
        body, name=name, in_specs=[ANY] * n_in, out_specs=[ANY] * len(out_shape), out_shape=out_shape,
        scratch_shapes=[pltpu.SemaphoreType.DMA((n,)) for n in n_sems],
        input_output_aliases=aliases or {},
        compiler_params=pltpu.CompilerParams(has_side_effects=True),
    )


def _remote(src, dst, send_sem, recv_sem, dev):
    return pltpu.make_async_remote_copy(src_ref=src, dst_ref=dst, send_sem=send_sem, recv_sem=recv_sem, device_id=dev,
                                        device_id_type=MESH)


def _gather_small(small_shards):
    ns = len(small_shards)

    def body(*refs):
        ins, outs = refs[:ns], refs[ns:2 * ns]
        lsem, ssem, rsem = refs[2 * ns:]
        x, y, c, me = _place()
        cols = lambda ref, j: _Sharded._slice(ref, 1, j, N_CHIPS)
        local = [pltpu.make_async_copy(ins[i], cols(outs[i], me), lsem.at[i]) for i in range(ns)]
        sends = []
        for i in range(ns):
            for k, rel in enumerate(_RELS):
                px, py, _ = _peer(x, y, rel)
                sends.append(_remote(ins[i], cols(outs[i], me), ssem.at[3 * i + k], rsem.at[3 * i + k], (px, py, c)))
        for cp in local + sends:
            cp.start()
        for i in range(ns):
            for k, rel in enumerate(_RELS):
                _, _, pj = _peer(x, y, rel)
                got = cols(outs[i], pj)
                _remote(got, got, ssem.at[3 * i + k], rsem.at[3 * i + k], (x, y, c)).wait_recv()
        for cp in sends:
            cp.wait_send()
        for cp in local:
            cp.wait()

    out_shape = [_sds((s.shape[0], s.shape[1] * N_CHIPS), F32) for s in small_shards]
    return _comm_call(body, "gather_small", ns, out_shape, [ns, 3 * ns, 3 * ns])(*small_shards)


HBM = pl.BlockSpec(memory_space=pltpu.HBM)
SEM = pl.BlockSpec(memory_space=pltpu.SEMAPHORE)
DATAFLOW = pltpu.SideEffectType.DATAFLOW_SIDE_EFFECTING


def _in_hbm(a):
    return pltpu.with_memory_space_constraint(a, pltpu.HBM)


def _cast_place(it, shard, scal, after=()):
    a_n, r_n, c_n = it.shard3
    tr = _pick(r_n, 256, 16)
    sa = it.shard_axis

    def body(sc_ref, s_ref, o_ref):
        o_ref[...] = s_ref[...].astype(BF16)

    if it.slab is None:
        src = pl.BlockSpec((1, tr, c_n), lambda a, rb, sc: (a, rb, 0))
    else:
        src = pl.BlockSpec((None, 1, tr, c_n), lambda a, rb, sc: (it.slab, a, rb, 0))
    dst = pl.BlockSpec((1, tr, c_n), lambda a, rb, sc: (a + sc[1] * (a_n if sa == 0 else 0), rb + sc[1] * (r_n // tr if sa == 1 else 0),
                                                       sc[1] if sa == 2 else 0))
    return _call(body, f"cast_place_{it.name}", (a_n, r_n // tr), [src], [dst], [_sds(it.full3, BF16)], prefetch=1,
                 after=after)(scal, shard)[0]


def _gather_start(items, fulls, name):
    ni = len(items)

    def body(*refs):
        outs = refs[ni:]
        ssem, rsem, full = outs[:ni], outs[ni:2 * ni], outs[2 * ni:3 * ni]
        x, y, c, me = _place()
        for i, it in enumerate(items):
            mine = it.half(it.shard(full[i], me), c)
            for k, rel in enumerate(_RELS):
                px, py, _ = _peer(x, y, rel)
                _remote(mine, mine, ssem[i].at[k], rsem[i].at[k], (px, py, c)).start()

    outs = pl.pallas_call(
        body, name=name, in_specs=[HBM] * ni, out_specs=[SEM] * (2 * ni) + [HBM] * ni,
        out_shape=[pltpu.SemaphoreType.DMA((3,))] * (2 * ni) + [pltpu.HBM(it.full3, BF16) for it in items],
        input_output_aliases={j: 2 * ni + j for j in range(ni)},
        compiler_params=pltpu.CompilerParams(has_side_effects=DATAFLOW),
    )(*[_in_hbm(f) for f in fulls])
    return outs[:ni], outs[ni:2 * ni], outs[2 * ni:]


def _gather_forward(items, fulls, ssems, rsems, after, name):
    ni = len(items)

    def body(*refs):
        ssem, rsem = refs[ni:2 * ni], refs[2 * ni:3 * ni]
        outs = refs[3 * ni + 1:]
        full, fsem, gsem = outs[:ni], outs[ni:2 * ni], outs[2 * ni:3 * ni]
        x, y, c, me = _place()
        sib = (x, y, 1 - c)
        for i, it in enumerate(items):
            for k, rel in enumerate(_RELS):
                _, _, pj = _peer(x, y, rel)
                got = it.half(it.shard(full[i], pj), c)
                _remote(got, got, ssem[i].at[k], rsem[i].at[k], sib).wait_recv()
                _remote(got, got, fsem[i].at[k], gsem[i].at[k], sib).start()
        for i, it in enumerate(items):
            mine = it.half(it.shard(full[i], me), c)
            for k in range(3):
                _remote(mine, mine, ssem[i].at[k], rsem[i].at[k], sib).wait_send()

    outs = pl.pallas_call(
        body, name=name, in_specs=[HBM] * ni + [SEM] * (2 * ni) + [ANY],
        out_specs=[HBM] * ni + [SEM] * (2 * ni),
        out_shape=[pltpu.HBM(it.full3, BF16) for it in items] + [pltpu.SemaphoreType.DMA((3,))] * (2 * ni),
        input_output_aliases={i: i for i in range(ni)},
        compiler_params=pltpu.CompilerParams(has_side_effects=DATAFLOW),
    )(*fulls, *ssems, *rsems, after)
    return outs[:ni], outs[ni:2 * ni], outs[2 * ni:]


def _gather_finish(items, fulls, fsems, gsems, name):
    ni = len(items)

    def body(*refs):
        fsem, gsem = refs[ni:2 * ni], refs[2 * ni:3 * ni]
        full = refs[3 * ni:]
        x, y, c, _ = _place()
        sib = (x, y, 1 - c)
        for i, it in enumerate(items):
            for k, rel in enumerate(_RELS):
                _, _, pj = _peer(x, y, rel)
                got = it.half(it.shard(full[i], pj), 1 - c)
                _remote(got, got, fsem[i].at[k], gsem[i].at[k], sib).wait_recv()
                sent = it.half(it.shard(full[i], pj), c)
                _remote(sent, sent, fsem[i].at[k], gsem[i].at[k], sib).wait_send()

    return pl.pallas_call(
        body, name=name, in_specs=[HBM] * ni + [SEM] * (2 * ni), out_specs=[HBM] * ni,
        out_shape=[pltpu.HBM(it.full3, BF16) for it in items],
        input_output_aliases={i: i for i in range(ni)},
        compiler_params=pltpu.CompilerParams(has_side_effects=DATAFLOW),
    )(*fulls, *fsems, *gsems)


def _exchange_start(name, arrays, n_sems, copies):
    na, ns = len(arrays), len(n_sems)

    def body(*refs):
        outs = refs[na:]
        ssem, rsem, thru, token = outs[:ns], outs[ns:2 * ns], outs[2 * ns:2 * ns + na], outs[-1]
        for send, _ in copies(thru, ssem, rsem):
            send.start()
        token[...] = jnp.zeros_like(token)

    outs = pl.pallas_call(
        body, name=name, in_specs=[HBM] * na,
        out_specs=[SEM] * (2 * ns) + [HBM] * na + [pl.BlockSpec(memory_space=pltpu.VMEM)],
        out_shape=[pltpu.SemaphoreType.DMA((n,)) for n in n_sems] * 2 + [pltpu.HBM(a.shape, a.dtype) for a in arrays]
        + [_sds((8, 128), F32)],
        input_output_aliases={j: 2 * ns + j for j in range(na)},
        compiler_params=pltpu.CompilerParams(has_side_effects=DATAFLOW),
    )(*[_in_hbm(a) for a in arrays])
    return outs[:ns], outs[ns:2 * ns], outs[2 * ns:2 * ns + na], outs[-1]


def _exchange_wait(name, arrays, ssems, rsems, copies, after):
    na, ns, nw = len(arrays), len(ssems), len(after)

    def body(*refs):
        ssem, rsem = refs[na:na + ns], refs[na + ns:na + 2 * ns]
        thru = refs[na + 2 * ns + nw:]
        for send, recv in copies(thru, ssem, rsem):
            recv.wait_recv()
            send.wait_send()

    return pl.pallas_call(
        body, name=name, in_specs=[HBM] * na + [SEM] * (2 * ns) + [ANY] * nw, out_specs=[HBM] * na,
        out_shape=[pltpu.HBM(a.shape, a.dtype) for a in arrays],
        input_output_aliases={i: i for i in range(na)},
        compiler_params=pltpu.CompilerParams(has_side_effects=DATAFLOW),
    )(*arrays, *ssems, *rsems, *after)


def _pair_copies(items):
    n = len(items)

    def copies(a, ssem, rsem):
        x, y, c, _ = _place()
        cps = [_remote(it.half(a[i], 1 - c), a[n + i], ssem[i].at[0], rsem[i].at[0], (x, y, 1 - c)) for i, it in enumerate(items)]
        return [(cp, cp) for cp in cps]

    return copies


def _chip_copies(items):
    n = len(items)

    def copies(a, ssem, rsem):
        x, y, c, _ = _place()
        cps = []
        for i, it in enumerate(items):
            for k, rel in enumerate(_RELS):
                px, py, pj = _peer(x, y, rel)
                cps.append(_remote(it.shard(a[i], pj), a[n + i].at[k], ssem[i].at[k], rsem[i].at[k], (px, py, c)))
        return [(cp, cp) for cp in cps]

    return copies


def _fill_copies(items):
    def copies(a, ssem, rsem):
        x, y, c, _ = _place()
        sib = (x, y, 1 - c)
        out = []
        for i, it in enumerate(items):
            mine, other = it.half(a[i], c), it.half(a[i], 1 - c)
            out.append((_remote(mine, mine, ssem[i].at[0], rsem[i].at[0], sib), _remote(other, other, ssem[i].at[0], rsem[i].at[0], sib)))
        return out

    return copies


def _ew_tiles(d):
    _, rows, cols = d.part3
    return _pick(rows, 256, 16), cols


def _pair_add(d, g_full, got, scal):
    tr, tc = _ew_tiles(d)
    a_n, r_n, c_n = d.pair3
    ha = d.half_axis

    def body(sc_ref, g_ref, r_ref, o_ref, ob_ref):
        s = g_ref[...].astype(F32) + r_ref[...].astype(F32)
        o_ref[...] = s
        ob_ref[...] = s.astype(BF16)

    blk = (1, tr, tc)
    same = pl.BlockSpec(blk, lambda a, rb, cb, sc: (a, rb, cb))
    mine = pl.BlockSpec(blk, lambda a, rb, cb, sc: (a + sc[0] * (a_n if ha == 0 else 0), rb + sc[0] * (r_n // tr if ha == 1 else 0), cb))
    return _call(body, f"pair_add_{d.name}", (a_n, r_n // tr, c_n // tc), [mine, same], [same, same],
                 [_sds(d.pair3, F32), _sds(d.pair3, BF16)], prefetch=1)(scal, g_full, got)


def _chip_reduce(d, pair_f32, got, scal):
    tr, tc = _ew_tiles(d)
    a_n, r_n, c_n = d.part3
    ha, sa = d.half_axis, d.shard_axis

    def body(sc_ref, p_ref, r0, r1, r2, o_ref):
        o_ref[...] = ((p_ref[...] + r0[...].astype(F32)) + r1[...].astype(F32)) + r2[...].astype(F32)

    blk = (1, tr, tc)
    own = pl.BlockSpec(blk, lambda a, rb, sc: (a + sc[1] * (a_n if sa == 0 else 0), rb + sc[1] * (r_n // tr if sa == 1 else 0),
                                               sc[1] if sa == 2 else 0))
    slot = lambda k: pl.BlockSpec((None,) + blk, lambda a, rb, sc: (k, a, rb, 0))
    out = pl.BlockSpec(blk, lambda a, rb, sc: (a + sc[0] * (a_n if ha == 0 else 0), rb + sc[0] * (r_n // tr if ha == 1 else 0), 0))
    return _call(body, f"chip_reduce_{d.name}", (a_n, r_n // tr), [own, slot(0), slot(1), slot(2)], [out],
                 [_sds(d.shard3, F32)], prefetch=1)(scal, pair_f32, got, got, got)[0]


def _adam_math(g, w, m, v):
    m = ADAM_B1 * m + (1.0 - ADAM_B1) * g
    v = ADAM_B2 * v + (1.0 - ADAM_B2) * (g * g)
    m_hat = m / (1.0 - ADAM_B1 ** ADAM_STEP)
    v_hat = v / (1.0 - ADAM_B2 ** ADAM_STEP)
    delta = -ADAM_LR * (m_hat / (jnp.sqrt(v_hat) + ADAM_EPS) + ADAM_WD * w)
    return delta, m, v


def _adam(d, g, w, m, v, prev=None):
    tr, tc = _ew_tiles(d)
    a_n, r_n, c_n = d.shard3
    n_prev = 0 if prev is None else 4

    def body(g_ref, w_ref, m_ref, v_ref, *rest):
        go_ref, d_ref, mo_ref, vo_ref = rest[n_prev:]
        gv = g_ref[...]
        go_ref[...] = gv
        d_ref[...], mo_ref[...], vo_ref[...] = _adam_math(gv, w_ref[...], m_ref[...], v_ref[...])

    plain = pl.BlockSpec((1, tr, tc), lambda a, rb: (a, rb, 0))
    if d.slab is None:
        wspec, shape = plain, d.shard3
    else:
        wspec, shape = pl.BlockSpec((None, 1, tr, tc), lambda a, rb: (d.slab, a, rb, 0)), (4,) + d.shard3
    in_specs = [plain] + [wspec] * 3
    args = [g, w, m, v]
    aliases = None
    if prev is not None:
        in_specs += [pl.BlockSpec(memory_space=pl.ANY)] * 4
        args += list(prev)
        aliases = {4 + k: k for k in range(4)}
    return _call(body, f"adam_{d.name}", (a_n, r_n // tr), in_specs, [wspec] * 4, [_sds(shape, F32)] * 4, aliases=aliases)(*args)


def _adam_small(gs, ws, ms, vs):
    n = len(gs)

    def body(*refs):
        for i in range(n):
            g, w, m, v = (refs[k * n + i][...] for k in range(4))
            d, mo, vo = _adam_math(g, w, m, v)
            refs[4 * n + i][...] = d
            refs[5 * n + i][...] = mo
            refs[6 * n + i][...] = vo

    vm = pl.BlockSpec(memory_space=pltpu.VMEM)
    outs = pl.pallas_call(body, name="adam_small", in_specs=[vm] * (4 * n), out_specs=[vm] * (3 * n),
                          out_shape=[_sds(g.shape, F32) for g in gs] * 3)(*gs, *ws, *ms, *vs)
    return outs[:n], outs[n:2 * n], outs[2 * n:]


def _allreduce_small(packed):
    rows, cols = packed.shape
    others = [(dx, dy, dc) for dx in (0, 1) for dy in (0, 1) for dc in (0, 1) if (dx, dy, dc) != (0, 0, 0)]

    def body(in_ref, out_ref, buf, ssem, rsem):
        x, y, c, _ = _place()
        lin = 4 * x + 2 * y + c
        buf[lin] = in_ref[...]
        cps = []
        for k, (dx, dy, dc) in enumerate(others):
            px = 1 - x if dx else x
            py = 1 - y if dy else y
            pc = 1 - c if dc else c
            cps.append((pltpu.make_async_remote_copy(src_ref=in_ref, dst_ref=buf.at[lin], send_sem=ssem.at[k], recv_sem=rsem.at[k],
                                                     device_id=(px, py, pc), device_id_type=MESH), 4 * px + 2 * py + pc))
        for cp, _ in cps:
            cp.start()
        for k, (cp, plin) in enumerate(cps):
            pltpu.make_async_remote_copy(src_ref=in_ref, dst_ref=buf.at[plin], send_sem=ssem.at[k], recv_sem=rsem.at[k],
                                         device_id=(x, y, c), device_id_type=MESH).wait_recv()
        for cp, _ in cps:
            cp.wait_send()
        acc = buf[0]
        for dev in range(1, 8):
            acc = acc + buf[dev]
        out_ref[...] = acc

    vm = pl.BlockSpec(memory_space=pltpu.VMEM)
    return pl.pallas_call(
        body, name="allreduce_small", in_specs=[vm], out_specs=vm, out_shape=_sds((rows, cols), F32),
        scratch_shapes=[pltpu.VMEM((8, rows, cols), F32), pltpu.SemaphoreType.DMA((7,)), pltpu.SemaphoreType.DMA((7,))],
        compiler_params=pltpu.CompilerParams(has_side_effects=True),
    )(packed)


_BIG = ("ffn_w_in", "ffn_w_out", "conv_w_pw1", "conv_w_pw2", "attn_w_qkv", "attn_w_o")
_SMALL = ("norm_g", "conv_b_pw1", "conv_w_dw", "conv_b_dw", "conv_norm_g", "conv_b_pw2", "attn_q_norm", "attn_k_norm")
_NAMES = ("norm_g", "ffn_w_in", "ffn_w_out", "conv_w_pw1", "conv_b_pw1", "conv_w_dw", "conv_b_dw", "conv_norm_g", "conv_w_pw2",
          "conv_b_pw2", "attn_w_qkv", "attn_q_norm", "attn_k_norm", "attn_w_o")


def _rows8(a, width):
    a = a.reshape(-1, min(a.shape[-1], width))
    return jnp.pad(a, ((0, -a.shape[0] % 8), (0, width - a.shape[1])))


def kernel(x, norm_g, ffn_w_in, ffn_w_out, conv_w_pw1, conv_b_pw1, conv_w_dw, conv_b_dw, conv_norm_g, conv_w_pw2, conv_b_pw2, attn_w_qkv, attn_q_norm, attn_k_norm, attn_w_o, loss_target, m_norm_g, m_ffn_w_in, m_ffn_w_out, m_conv_w_pw1, m_conv_b_pw1, m_conv_w_dw, m_conv_b_dw, m_conv_norm_g, m_conv_w_pw2, m_conv_b_pw2, m_attn_w_qkv, m_attn_q_norm, m_attn_k_norm, m_attn_w_o, v_norm_g, v_ffn_w_in, v_ffn_w_out, v_conv_w_pw1, v_conv_b_pw1, v_conv_w_dw, v_conv_b_dw, v_conv_norm_g, v_conv_w_pw2, v_conv_b_pw2, v_attn_w_qkv, v_attn_q_norm, v_attn_k_norm, v_attn_w_o):
    w = dict(zip(_NAMES, (norm_g, ffn_w_in, ffn_w_out, conv_w_pw1, conv_b_pw1, conv_w_dw, conv_b_dw, conv_norm_g, conv_w_pw2,
                          conv_b_pw2, attn_w_qkv, attn_q_norm, attn_k_norm, attn_w_o)))
    m = dict(zip(_NAMES, (m_norm_g, m_ffn_w_in, m_ffn_w_out, m_conv_w_pw1, m_conv_b_pw1, m_conv_w_dw, m_conv_b_dw, m_conv_norm_g,
                          m_conv_w_pw2, m_conv_b_pw2, m_attn_w_qkv, m_attn_q_norm, m_attn_k_norm, m_attn_w_o)))
    v = dict(zip(_NAMES, (v_norm_g, v_ffn_w_in, v_ffn_w_out, v_conv_w_pw1, v_conv_b_pw1, v_conv_w_dw, v_conv_b_dw, v_conv_norm_g,
                          v_conv_w_pw2, v_conv_b_pw2, v_attn_w_qkv, v_attn_q_norm, v_attn_k_norm, v_attn_w_o)))
    T, D = x.shape[1:]
    F = ffn_w_out.shape[2] * N_CHIPS
    HE = attn_w_o.shape[1] * N_CHIPS
    cx, cy, cc = lax.axis_index("x"), lax.axis_index("y"), lax.axis_index("c")
    me = 2 * cx + cy
    scal = jnp.stack([cc, me]).astype(jnp.int32)

    ffn_in = lambda lf: _Sharded(f"ffn_w_in_{lf}", (2, D // 2, 2 * F), 0, 2, "ffn_w_in", lf)
    ffn_out = lambda lf: _Sharded(f"ffn_w_out_{lf}", (4, F // 4, D), 1, 0, "ffn_w_out", lf)
    items = [
        ffn_in(0), ffn_out(0),
        _Sharded("conv_w_pw1", (2, D // 2, 2 * D), 0, 2, "conv_w_pw1"), _Sharded("conv_w_pw2", (4, D // 4, D), 1, 0, "conv_w_pw2"),
        ffn_in(1), ffn_out(1), ffn_in(2), ffn_out(2),
        _Sharded("attn_w_qkv", (2, D // 2, 9 * HE), 0, 2, "attn_w_qkv"), _Sharded("attn_w_o", (4, HE // 4, D), 1, 0, "attn_w_o"),
        ffn_in(3), ffn_out(3),
    ]
    mat_shapes = {"ffn_w_in": (D, 2 * F), "ffn_w_out": (F, D), "conv_w_pw1": (D, 2 * D), "conv_w_pw2": (D, D),
                  "attn_w_qkv": (D, 9 * HE), "attn_w_o": (HE, D)}

    def as_shards(a, n):
        it = next(i for i in items if i.src == n)
        return a.reshape(((4,) if it.slab is not None else ()) + it.shard3)

    norm_full, dw_full = _gather_small([norm_g.reshape(6, D // 4), conv_w_dw.reshape(CONV_WIDTH, D // 4)])
    place = lambda its, after: [_cast_place(it, as_shards(w[it.src], it.src), scal, after) for it in its]
    ssems, rsems, fulls = _gather_start(items[:2], place(items[:2], ()), "gather_start_first")
    more = _gather_start(items[2:], place(items[2:], fulls[:1]), "gather_start_rest")
    ssems, rsems, fulls = [list(a) + list(b) for a, b in zip((ssems, rsems, fulls), more)]

    def block_weights(k, after):
        sel = slice(2 * k, 2 * k + 2)
        got, fsems, gsems = _gather_forward(items[sel], fulls[sel], ssems[sel], rsems[sel], after, f"gather_forward_{k}")
        done = _gather_finish(items[sel], got, fsems, gsems, f"gather_finish_{k}")
        return [a.reshape(mat_shapes[it.src]) for a, it in zip(done, items[sel])]

    res = {}
    flight = []

    def advance(entry, k, dx):
        stage, its, st = entry
        n = len(its)
        if stage == 1:
            ssem, rsem, arrs = st
            done = _exchange_wait(f"grads_pair_wait_{k}", arrs, ssem, rsem, _pair_copies(its), dx)
            sums = [_pair_add(it, g, r, scal) for it, g, r in zip(its, done[:n], done[n:])]
            land = [lax.empty((3,) + it.part3, BF16) for it in its]
            ssem, rsem, arrs, tok = _exchange_start(f"grads_chip_start_{k}", [p[1] for p in sums] + land, [3] * n, _chip_copies(its))
            return (2, its, (ssem, rsem, arrs, [p[0] for p in sums])), tok
        if stage == 2:
            ssem, rsem, arrs, p32 = st
            got = _exchange_wait(f"grads_chip_wait_{k}", arrs, ssem, rsem, _chip_copies(its), dx)[n:]
            red = [_chip_reduce(it, p, r, scal) for it, p, r in zip(its, p32, got)]
            ssem, rsem, arrs, tok = _exchange_start(f"grads_fill_start_{k}", red, [1] * n, _fill_copies(its))
            return (3, its, (ssem, rsem, arrs)), tok
        ssem, rsem, arrs = st
        for it, g in zip(its, _exchange_wait(f"grads_fill_wait_{k}", arrs, ssem, rsem, _fill_copies(its), dx)):
            nm = it.src
            res[nm] = _adam(it, g, as_shards(w[nm], nm), as_shards(m[nm], nm), as_shards(v[nm], nm), prev=res.get(nm))
        return None, None

    def step_flight(dx):
        toks, left = [], []
        for k, entry in flight:
            entry, tok = advance(entry, k, dx)
            if entry is not None:
                left.append((k, entry))
                toks.append(tok)
        flight[:] = left
        return toks

    def grads_ready(k, pairs, dx):
        its = items[2 * k:2 * k + 2]
        toks = step_flight((dx,))
        g16 = [p.reshape(it.full3) for p, it in zip(pairs, its)]
        land = [lax.empty(it.pair3, BF16) for it in its]
        ssem, rsem, arrs, tok = _exchange_start(f"grads_pair_start_{k}", g16 + land, [1] * len(its), _pair_copies(its))
        flight.append((k, (1, its, (ssem, rsem, arrs))))
        return toks + [tok]

    sq, dx, grads = _local_step(x[0], loss_target[0], norm_full.reshape(2, 3, D), conv_b_pw1, dw_full, conv_b_dw, conv_norm_g,
                                conv_b_pw2, attn_q_norm[0], attn_k_norm[0], block_weights, grads_ready)
    loss = lax.psum(0.5 * jnp.sum(sq) / D, ("x", "y", "c"))
    k_last, entry = flight.pop()
    entry, tok = advance(entry, k_last, (dx,))
    while flight:
        step_flight((dx, tok))

    out_g, out_d, out_m, out_v = {}, {}, {}, {}

    parts = [_rows8(grads[n], D) for n in _SMALL]
    tot = _allreduce_small(jnp.concatenate(parts, axis=0))
    sg, r0 = {}, 0
    for n, p in zip(_SMALL, parts):
        last = grads[n].shape[-1]
        g = tot[r0:r0 + grads[n].size // min(last, D), :min(last, D)].reshape(-1, last)
        r0 += p.shape[0]
        if n in ("norm_g", "conv_w_dw"):
            g = lax.dynamic_slice_in_dim(g, me * (D // 4), D // 4, axis=1)
        sg[n] = g
    flat = lambda a: a.reshape(-1, a.shape[-1])
    ds, ms, vs = _adam_small([sg[n] for n in _SMALL], [flat(w[n]) for n in _SMALL], [flat(m[n]) for n in _SMALL],
                             [flat(v[n]) for n in _SMALL])
    for i, n in enumerate(_SMALL):
        out_g[n], out_d[n], out_m[n], out_v[n] = (a.reshape(w[n].shape) for a in (sg[n], ds[i], ms[i], vs[i]))

    behind = tuple(r[1] for r in res.values()) + tuple(ds)
    while entry is not None:
        entry, _ = advance(entry, k_last, behind)
    for n in _BIG:
        out_g[n], out_d[n], out_m[n], out_v[n] = (a.reshape(w[n].shape) for a in res[n])

    return (loss, dx[None], *[out_g[n] for n in _NAMES], *[out_d[n] for n in _NAMES], *[out_m[n] for n in _NAMES],
            *[out_v[n] for n in _NAMES])
```

```python
import functools

import jax
import jax.numpy as jnp
from jax import lax
from jax.experimental import pallas as pl
from jax.experimental.pallas import tpu as pltpu

F32 = jnp.float32
BF16 = jnp.bfloat16
MESH = pl.DeviceIdType.MESH

NORM_EPS = 1e-6
CONV_WIDTH = 31
ATTN_GROUPS = ((128, 1), (512, 4), (2048, 16))
ATTN_BLOCK = 128
HEAD_DIM = 128
N_CHIPS = 4

ADAM_LR = 0.001
ADAM_B1 = 0.9
ADAM_B2 = 0.999
ADAM_EPS = 1e-08
ADAM_WD = 0.01
ADAM_STEP = 10

VMEM_LIMIT = 56 * 1024 * 1024
NT_DIMS = (((1,), (1,)), ((), ()))
TN_DIMS = (((0,), (0,)), ((), ()))


def _pick(n, pref, mult):
    t = (min(n, pref) // mult) * mult
    while t >= mult:
        if n % t == 0:
            return t
        t -= mult
    return n


def _call(body, name, grid, in_specs, out_specs, out_shape, scratch=(), aliases=None, prefetch=0, after=()):
    params = pltpu.CompilerParams(dimension_semantics=("arbitrary",) * len(grid), vmem_limit_bytes=VMEM_LIMIT)
    after = tuple(after)
    if after:
        inner, n_in = body, prefetch + len(in_specs)

        def body(*refs):
            return inner(*refs[:n_in], *refs[n_in + len(after):])

        in_specs = list(in_specs) + [pl.BlockSpec(memory_space=pl.ANY)] * len(after)
    if prefetch:
        spec = pltpu.PrefetchScalarGridSpec(
            num_scalar_prefetch=prefetch, grid=grid, in_specs=in_specs, out_specs=out_specs, scratch_shapes=list(scratch)
        )
        call = pl.pallas_call(body, name=name, grid_spec=spec, out_shape=out_shape, compiler_params=params,
                              input_output_aliases=aliases or {})
    else:
        call = pl.pallas_call(body, name=name, grid=grid, in_specs=in_specs, out_specs=out_specs, out_shape=out_shape,
                              scratch_shapes=list(scratch), compiler_params=params, input_output_aliases=aliases or {})
    return lambda *args: call(*args, *after)


def _sds(shape, dtype):
    return jax.ShapeDtypeStruct(shape, dtype)


def _sig(x):
    return 1.0 / (1.0 + jnp.exp(-x))


def _rstd(x):
    return lax.rsqrt(jnp.mean(x * x, axis=-1, keepdims=True) + NORM_EPS)


def _norm_bwd(dy, xhat, r, g):
    dxh = dy * g
    return r * (dxh - xhat * jnp.mean(dxh * xhat, axis=-1, keepdims=True))


def _dot(a, b):
    return jnp.dot(a, b, preferred_element_type=F32)


def _dot_nt(a, b):
    return lax.dot_general(a, b, NT_DIMS, preferred_element_type=F32)


def _dot_tn(a, b):
    return lax.dot_general(a, b, TN_DIMS, preferred_element_type=F32)


def _ffn_fwd(x, g, w_in, w_out, name):
    T, D = x.shape
    F = w_out.shape[0]
    tm = _pick(T, 256, 8)

    def body(x_ref, g_ref, wg_ref, wu_ref, wo_ref, xo_ref, gate_ref, up_ref):
        xv = x_ref[...]
        h = (xv * _rstd(xv) * g_ref[...]).astype(BF16)
        gate = _dot(h, wg_ref[...])
        up = _dot(h, wu_ref[...])
        gate_ref[...] = gate.astype(BF16)
        up_ref[...] = up.astype(BF16)
        a = (gate * _sig(gate) * up).astype(BF16)
        xo_ref[...] = xv + 0.5 * _dot(a, wo_ref[...])

    row = lambda i: (i, 0)
    held = lambda shape, k: pl.BlockSpec(shape, lambda i: (0, k), pipeline_mode=pl.Buffered(1))
    return _call(
        body, name, (T // tm,),
        [pl.BlockSpec((tm, D), row), pl.BlockSpec((1, D), lambda i: (0, 0)), held((D, F), 0), held((D, F), 1), held((F, D), 0)],
        [pl.BlockSpec((tm, D), row), pl.BlockSpec((tm, F), row), pl.BlockSpec((tm, F), row)],
        [_sds((T, D), F32), _sds((T, F), BF16), _sds((T, F), BF16)],
    )(x, g, w_in, w_in, w_out)


def _ffn_bwd(x, g, dy, gate, up, w_in, w_out, name, after=()):
    T, D = x.shape
    F = w_out.shape[0]

    def body_a(dy_ref, gate_ref, up_ref, wo_ref, dgate_ref, dup_ref, a_ref):
        dyb = (0.5 * dy_ref[...]).astype(BF16)
        gate = gate_ref[...].astype(F32)
        up = up_ref[...].astype(F32)
        sg = _sig(gate)
        sl = gate * sg
        a_ref[...] = (sl * up).astype(BF16)
        da = _dot_nt(dyb, wo_ref[...])
        dgate_ref[...] = (da * up * (sg * (1.0 + gate * (1.0 - sg)))).astype(BF16)
        dup_ref[...] = (da * sl).astype(BF16)

    ta = _pick(T, 256, 8)
    tile = pl.BlockSpec((ta, F), lambda i: (i, 0))
    dgate, dup, a = _call(
        body_a, name + "_hidden", (T // ta,),
        [pl.BlockSpec((ta, D), lambda i: (i, 0)), tile, tile,
         pl.BlockSpec((F, D), lambda i: (0, 0), pipeline_mode=pl.Buffered(1))],
        [tile, tile, tile],
        [_sds((T, F), BF16)] * 3, after=after,
    )(dy, gate, up, w_out)

    def body_b(x_ref, g_ref, dy_ref, dgate_ref, dup_ref, wg_ref, wu_ref, dx_ref, dg_ref, h_ref, dyb_ref):
        @pl.when(pl.program_id(0) == 0)
        def _():
            dg_ref[...] = jnp.zeros_like(dg_ref)

        xv = x_ref[...]
        r = _rstd(xv)
        xh = xv * r
        h_ref[...] = (xh * g_ref[...]).astype(BF16)
        dyb_ref[...] = (0.5 * dy_ref[...]).astype(BF16)
        dh = _dot_nt(dgate_ref[...], wg_ref[...]) + _dot_nt(dup_ref[...], wu_ref[...])
        dx_ref[...] = dy_ref[...] + _norm_bwd(dh, xh, r, g_ref[...])
        dg_ref[...] += jnp.sum(dh * xh, axis=0, keepdims=True)

    tb = _pick(T, 512, 8)
    row = lambda i: (i, 0)
    held = lambda k: pl.BlockSpec((D, F), lambda i: (0, k), pipeline_mode=pl.Buffered(1))
    dx, dg, h, dyb = _call(
        body_b, name, (T // tb,),
        [pl.BlockSpec((tb, D), row), pl.BlockSpec((1, D), lambda i: (0, 0)), pl.BlockSpec((tb, D), row),
         pl.BlockSpec((tb, F), row), pl.BlockSpec((tb, F), row), held(0), held(1)],
        [pl.BlockSpec((tb, D), row), pl.BlockSpec((1, D), lambda i: (0, 0)), pl.BlockSpec((tb, D), row), pl.BlockSpec((tb, D), row)],
        [_sds((T, D), F32), _sds((1, D), F32), _sds((T, D), BF16), _sds((T, D), BF16)],
    )(x, g, dy, dgate, dup, w_in, w_in)
    return dx, dg, dgate, dup, a, h, dyb


def _mm_tn(a, b, bm, bn, out_shape, out_block, out_map, scale, name, prev=None):
    K, M = a.shape
    N = b.shape[1]

    def body(a_ref, b_ref, *rest):
        rest[-1][...] = (_dot_tn(a_ref[...].astype(BF16), b_ref[...].astype(BF16)) * scale).astype(BF16)

    in_specs = [pl.BlockSpec((K, bm), lambda mi, ni: (0, mi)), pl.BlockSpec((K, bn), lambda mi, ni: (0, ni))]
    args = [a, b]
    aliases = None
    if prev is not None:
        in_specs += [pl.BlockSpec(memory_space=pl.ANY)]
        args += [prev]
        aliases = {2: 0}
    return _call(body, name, (M // bm, N // bn), in_specs, [pl.BlockSpec(out_block, out_map)],
                 [_sds(out_shape, BF16)], aliases=aliases)(*args)[0]


def _mm_tn_parts(a, bs, bn, cols, col0, name, prev=None):
    K, M = a.shape
    N = bs[0].shape[1]
    nt = N // bn
    ns = len(bs)

    def body(a_ref, *refs):
        o_ref = refs[-1]
        part = pl.program_id(0) // nt
        for s in range(ns):
            @pl.when(part == s)
            def _(s=s):
                o_ref[...] = _dot_tn(a_ref[...], refs[s][...]).astype(BF16)

    part_spec = lambda s: pl.BlockSpec((K, bn), lambda j: (0, jnp.clip(j - s * nt, 0, nt - 1)))
    in_specs = [pl.BlockSpec((K, M), lambda j: (0, 0))] + [part_spec(s) for s in range(ns)]
    args = [a, *bs]
    aliases = None
    if prev is not None:
        in_specs += [pl.BlockSpec(memory_space=pl.ANY)]
        args += [prev]
        aliases = {ns + 1: 0}
    return _call(body, name, (ns * nt,), in_specs, [pl.BlockSpec((M, bn), lambda j: (0, col0 // bn + j))],
                 [_sds((M, cols), BF16)], aliases=aliases)(*args)[0]


def _conv_pre(x, g, w1, b1, name):
    T, D = x.shape
    tm = _pick(T, 512, 8)

    def body(x_ref, g_ref, w_ref, b_ref, ag_ref, u_ref, h_ref):
        xv = x_ref[...]
        h = (xv * _rstd(xv) * g_ref[...]).astype(BF16)
        h_ref[...] = h
        ag = _dot(h, w_ref[...]) + b_ref[...]
        ag_ref[...] = ag.astype(BF16)
        u_ref[...] = ag[:, :D] * _sig(ag[:, D:])

    return _call(
        body, name, (T // tm,),
        [pl.BlockSpec((tm, D), lambda i: (i, 0)), pl.BlockSpec((1, D), lambda i: (0, 0)),
         pl.BlockSpec((D, 2 * D), lambda i: (0, 0)), pl.BlockSpec((1, 2 * D), lambda i: (0, 0))],
        [pl.BlockSpec((tm, 2 * D), lambda i: (i, 0)), pl.BlockSpec((tm, D), lambda i: (i, 0)),
         pl.BlockSpec((tm, D), lambda i: (i, 0))],
        [_sds((T, 2 * D), BF16), _sds((T, D), F32), _sds((T, D), BF16)],
    )(x, g, w1, b1)


_DW_PAD = 32
_DW_CHUNK = 256


def _dwconv(u, w, b, name):
    T, D = u.shape
    K = w.shape[0]
    ch = _pick(T, _DW_CHUNK, 8)
    lead = _DW_PAD - (K - 1)

    def body(u_ref, w_ref, b_ref, c_ref, ext):
        ext[pl.ds(0, _DW_PAD), :] = jnp.zeros((_DW_PAD, 128), F32)
        ext[pl.ds(_DW_PAD, T), :] = u_ref[...]
        for c0 in range(0, T, ch):
            acc = jnp.zeros((ch, 128), F32) + b_ref[...]
            for k in range(K):
                acc = acc + w_ref[pl.ds(k, 1), :] * ext[pl.ds(c0 + lead + k, ch), :]
            c_ref[pl.ds(c0, ch), :] = acc

    return _call(
        body, name, (D // 128,),
        [pl.BlockSpec((T, 128), lambda i: (0, i)), pl.BlockSpec((K, 128), lambda i: (0, i)),
         pl.BlockSpec((1, 128), lambda i: (0, i))],
        [pl.BlockSpec((T, 128), lambda i: (0, i))],
        [_sds((T, D), F32)],
        scratch=[pltpu.VMEM((T + _DW_PAD, 128), F32)],
    )(u, w, b)[0]


def _dwconv_bwd(dc, u, w, name):
    T, D = u.shape
    K = w.shape[0]
    ch = _pick(T, _DW_CHUNK, 8)

    def body(dc_ref, u_ref, w_ref, du_ref, dw_ref, db_ref, dext):
        dext[pl.ds(0, T), :] = dc_ref[...]
        dext[pl.ds(T, _DW_PAD), :] = jnp.zeros((_DW_PAD, 128), F32)
        dws =[jnp.zeros((8, 128), F32) for _ in range(K)]
        dbs = jnp.zeros((8, 128), F32)
        for c0 in range(0, T, ch):
            uv = u_ref[pl.ds(c0, ch), :]
            dbs = dbs + jnp.sum(dc_ref[pl.ds(c0, ch), :].reshape(ch // 8, 8, 128), axis=0)
            acc = jnp.zeros((ch, 128), F32)
            for k in range(K):
                win = dext[pl.ds(c0 + (K - 1) - k, ch), :]
                acc = acc + w_ref[pl.ds(k, 1), :] * win
                dws[k] = dws[k] + jnp.sum((win * uv).reshape(ch // 8, 8, 128), axis=0)
            du_ref[pl.ds(c0, ch), :] = acc
        for k in range(K):
            dw_ref[pl.ds(k, 1), :] = jnp.sum(dws[k], axis=0, keepdims=True)
        db_ref[...] = jnp.sum(dbs, axis=0, keepdims=True)

    return _call(
        body, name, (D // 128,),
        [pl.BlockSpec((T, 128), lambda i: (0, i)), pl.BlockSpec((T, 128), lambda i: (0, i)),
         pl.BlockSpec((K, 128), lambda i: (0, i))],
        [pl.BlockSpec((T, 128), lambda i: (0, i)), pl.BlockSpec((K, 128), lambda i: (0, i)),
         pl.BlockSpec((1, 128), lambda i: (0, i))],
        [_sds((T, D), F32), _sds((K, D), F32), _sds((1, D), F32)],
        scratch=[pltpu.VMEM((T + _DW_PAD, 128), F32)],
    )(dc, u, w)


def _conv_post(c, x, ng, w2, b2, name):
    T, D = x.shape
    tm = _pick(T, 512, 8)

    def body(c_ref, x_ref, ng_ref, w_ref, b_ref, xo_ref, s_ref):
        cv = c_ref[...]
        n = cv * _rstd(cv) * ng_ref[...]
        s = (n * _sig(n)).astype(BF16)
        s_ref[...] = s
        xo_ref[...] = x_ref[...] + _dot(s, w_ref[...]) + b_ref[...]

    row = lambda i: (i, 0)
    fix = lambda i: (0, 0)
    return _call(
        body, name, (T // tm,),
        [pl.BlockSpec((tm, D), row), pl.BlockSpec((tm, D), row), pl.BlockSpec((1, D), fix),
         pl.BlockSpec((D, D), fix), pl.BlockSpec((1, D), fix)],
        [pl.BlockSpec((tm, D), row), pl.BlockSpec((tm, D), row)],
        [_sds((T, D), F32), _sds((T, D), BF16)],
    )(c, x, ng, w2, b2)


def _conv_post_bwd(dy, c, ng, w2, name, after=()):
    T, D = dy.shape
    tm = _pick(T, 512, 8)

    def body(dy_ref, c_ref, ng_ref, w_ref, dc_ref, dng_ref, db_ref):
        @pl.when(pl.program_id(0) == 0)
        def _():
            dng_ref[...] = jnp.zeros_like(dng_ref)
            db_ref[...] = jnp.zeros_like(db_ref)

        dyv = dy_ref[...]
        ds = _dot_nt(dyv.astype(BF16), w_ref[...])
        cv = c_ref[...]
        r = _rstd(cv)
        ch = cv * r
        n = ch * ng_ref[...]
        sg = _sig(n)
        dn = ds * (sg * (1.0 + n * (1.0 - sg)))
        dc_ref[...] = _norm_bwd(dn, ch, r, ng_ref[...])
        dng_ref[...] += jnp.sum(dn * ch, axis=0, keepdims=True)
        db_ref[...] += jnp.sum(dyv, axis=0, keepdims=True)

    row = lambda i: (i, 0)
    fix = lambda i: (0, 0)
    return _call(
        body, name, (T // tm,),
        [pl.BlockSpec((tm, D), row), pl.BlockSpec((tm, D), row), pl.BlockSpec((1, D), fix), pl.BlockSpec((D, D), fix)],
        [pl.BlockSpec((tm, D), row), pl.BlockSpec((1, D), fix), pl.BlockSpec((1, D), fix)],
        [_sds((T, D), F32), _sds((1, D), F32), _sds((1, D), F32)], after=after,
    )(dy, c, ng, w2)


def _conv_pre_bwd(du, ag, x, g, dy, w1, name):
    T, D = x.shape
    tm = _pick(T, 512, 8)

    def body(du_ref, ag_ref, x_ref, g_ref, dy_ref, w_ref, dx_ref, dg_ref, dag_ref, db_ref):
        @pl.when(pl.program_id(0) == 0)
        def _():
            dg_ref[...] = jnp.zeros_like(dg_ref)
            db_ref[...] = jnp.zeros_like(db_ref)

        duv = du_ref[...]
        a = ag_ref[:, :D].astype(F32)
        gt = ag_ref[:, D:].astype(F32)
        sg = _sig(gt)
        da = duv * sg
        dgt = duv * a * sg * (1.0 - sg)
        db_ref[:, :D] += jnp.sum(da, axis=0, keepdims=True)
        db_ref[:, D:] += jnp.sum(dgt, axis=0, keepdims=True)
        dab = da.astype(BF16)
        dgb = dgt.astype(BF16)
        dag_ref[:, :D] = dab
        dag_ref[:, D:] = dgb
        dh = _dot_nt(dab, w_ref[:, :D]) + _dot_nt(dgb, w_ref[:, D:])
        xv = x_ref[...]
        r = _rstd(xv)
        xh = xv * r
        dx_ref[...] = dy_ref[...] + _norm_bwd(dh, xh, r, g_ref[...])
        dg_ref[...] += jnp.sum(dh * xh, axis=0, keepdims=True)

    row = lambda i: (i, 0)
    fix = lambda i: (0, 0)
    return _call(
        body, name, (T // tm,),
        [pl.BlockSpec((tm, D), row), pl.BlockSpec((tm, 2 * D), row), pl.BlockSpec((tm, D), row), pl.BlockSpec((1, D), fix),
         pl.BlockSpec((tm, D), row), pl.BlockSpec((D, 2 * D), fix)],
        [pl.BlockSpec((tm, D), row), pl.BlockSpec((1, D), fix), pl.BlockSpec((tm, 2 * D), row),
         pl.BlockSpec((1, 2 * D), fix)],
        [_sds((T, D), F32), _sds((1, D), F32), _sds((T, 2 * D), BF16), _sds((1, 2 * D), F32)],
    )(du, ag, x, g, dy, w1)


def _row_sums(a):
    return _dot(a.astype(BF16), jnp.ones((a.shape[1], 128), BF16))


def _head_rstd(x):
    return lax.rsqrt(_row_sums(x * x) * (1.0 / x.shape[1]) + NORM_EPS)


def _attn_qkv(x, g, wqkv, qn, kn, name):
    T, D = x.shape
    N = wqkv.shape[1]
    tn = N // 9
    E = HEAD_DIM
    tm = _pick(T, 256, 8)
    ng = qn.shape[0]

    def body(x_ref, g_ref, w_ref, qn_ref, kn_ref, o_ref, a_ref, h_ref):
        xv = x_ref[...]
        h = (xv * _rstd(xv) * g_ref[...]).astype(BF16)
        h_ref[...] = h
        for j in range(9):
            cs = slice(j * tn, (j + 1) * tn)
            res = _dot(h, w_ref[:, cs])
            o_ref[:, cs] = res.astype(BF16)
            if j % 3 == 2:
                a_ref[:, cs] = res.astype(BF16)
                continue
            grp = j // 3
            fac = qn_ref[grp:grp + 1, :] * kn_ref[grp:grp + 1, :] * (E ** -0.5) if j % 3 == 0 else None
            for h_i in range(tn // E):
                hs = slice(h_i * E, (h_i + 1) * E)
                xh = res[:, hs]
                hat = xh * _head_rstd(xh)
                a_ref[:, j * tn + h_i * E:j * tn + (h_i + 1) * E] = (hat if fac is None else hat * fac).astype(BF16)

    row = lambda i: (i, 0)
    fix = lambda i: (0, 0)
    return _call(
        body, name, (T // tm,),
        [pl.BlockSpec((tm, D), row), pl.BlockSpec((1, D), fix), pl.BlockSpec((D, N), fix, pipeline_mode=pl.Buffered(1)),
         pl.BlockSpec((ng, E), fix), pl.BlockSpec((ng, E), fix)],
        [pl.BlockSpec((tm, N), row), pl.BlockSpec((tm, N), row), pl.BlockSpec((tm, D), row)],
        [_sds((T, N), BF16), _sds((T, N), BF16), _sds((T, D), BF16)],
    )(x, g, wqkv, qn, kn)


def _band_mask(q, steps, nblk):
    i = lax.broadcasted_iota(jnp.int32, (q, 2 * q), 0)
    j = lax.broadcasted_iota(jnp.int32, (q, 2 * q), 1)
    diff = q + i - j
    first_key = jnp.where(nblk > 0, 0, q)
    return (diff >= 0) & (diff <= steps) & (j >= first_key)


def _per_row(blk, width):
    e = blk.shape[1]
    if width % e == 0:
        return jnp.concatenate([blk] * (width // e), axis=1)
    return jnp.broadcast_to(blk[:, :1], (blk.shape[0], width))


def _streams(a, dil, to_streams, name, col0=0, ncols=None):
    T, C = a.shape
    ncols = C if ncols is None else ncols
    Q = ATTN_BLOCK
    run = Q * dil
    reps = max(1, min(2048 // run, T // run))
    while T % (run * reps):
        reps -= 1
    rows = run * reps
    cw = _pick(ncols, 512, 128)
    ns = cw // 128

    def body(a_ref, o_ref, scr):
        for s in range(ns):
            ls = slice(s * 128, (s + 1) * 128)
            slab = scr.at[s]
            if to_streams:
                slab[...] = a_ref[:, ls].astype(F32)
                for u in range(reps):
                    for r in range(dil):
                        o_ref[pl.ds(u * run + r * Q, Q), ls] = slab[pl.ds(u * run + r, Q, stride=dil), :].astype(a.dtype)
            else:
                for u in range(reps):
                    for r in range(dil):
                        slab[pl.ds(u * run + r, Q, stride=dil), :] = a_ref[pl.ds(u * run + r * Q, Q), ls].astype(F32)
                o_ref[:, ls] = slab[...].astype(a.dtype)

    return _call(
        body, name, (T // rows, ncols // cw),
        [pl.BlockSpec((rows, cw), lambda i, j: (i, col0 // cw + j))],
        [pl.BlockSpec((rows, cw), lambda i, j: (i, j))],
        [_sds((T, ncols), a.dtype)],
        scratch=[pltpu.VMEM((ns, rows, 128), F32)],
    )(a)[0]


def _attn_fwd(qkv, base, HE, window, dil, name):
    T = qkv.shape[0]
    H = HE // HEAD_DIM
    E = HEAD_DIM
    Q = ATTN_BLOCK
    nb = T // dil // Q
    steps = window // dil

    def body(q_ref, kc_ref, kp_ref, vc_ref, vp_ref, o_ref, l_ref):
        n = pl.program_id(1)
        valid = _band_mask(Q, steps, n)
        ones = jnp.ones((2 * Q, E), BF16)
        outs, lses = [], []
        for h in range(H):
            hs = slice(h * E, (h + 1) * E)
            k2 = jnp.concatenate([kp_ref[:, hs], kc_ref[:, hs]], axis=0)
            v2 = jnp.concatenate([vp_ref[:, hs], vc_ref[:, hs]], axis=0)
            s = jnp.where(valid, _dot_nt(q_ref[:, hs], k2), -1e30)
            m = jnp.max(s, axis=-1, keepdims=True)
            p = jnp.exp(s - m).astype(BF16)
            acc = _dot(p, jnp.concatenate([v2, ones], axis=1))
            l = acc[:, E:]
            outs.append((acc[:, :E] * (1.0 / l)).astype(o_ref.dtype))
            lses.append(m + jnp.log(l))
        o_ref[...] = jnp.concatenate(outs, axis=1)
        l_ref[...] = jnp.concatenate(lses, axis=1)

    blk = lambda s, back: pl.BlockSpec((Q, HE), lambda r, n: (jnp.maximum(n - back, 0) * dil + r, base + s))
    out = pl.BlockSpec((Q, HE), lambda r, n: (n * dil + r, 0))
    return _call(
        body, name, (dil, nb),
        [blk(0, 0), blk(1, 0), blk(1, 1), blk(2, 0), blk(2, 1)],
        [out, out],
        [_sds((T, HE), BF16), _sds((T, HE), F32)],
    )(qkv, qkv, qkv, qkv, qkv)


def _attn_merge(os, lses, x, wo, name):
    T, D = x.shape
    HE = wo.shape[0]
    tm = _pick(T, 512, 8)
    ng = len(os)

    def body(*refs):
        o_refs = refs[:ng]
        l_refs = refs[ng:2 * ng]
        x_ref, w_ref, xo_ref, om_ref, lt_ref = refs[2 * ng:]
        ls = [r[...] for r in l_refs]
        m = functools.reduce(jnp.maximum, ls)
        es = [jnp.exp(l - m) for l in ls]
        tot = functools.reduce(lambda a, b: a + b, es)
        inv = 1.0 / tot
        om = functools.reduce(lambda a, b: a + b, [e * inv * r[...] for e, r in zip(es, o_refs)])
        omb = om.astype(BF16)
        om_ref[...] = omb
        lt_ref[...] = m + jnp.log(tot)
        xo_ref[...] = x_ref[...] + _dot(omb, w_ref[...])

    row = lambda i: (i, 0)
    fix = lambda i: (0, 0)
    return _call(
        body, name, (T // tm,),
        [pl.BlockSpec((tm, HE), row)] * (2 * ng) + [pl.BlockSpec((tm, D), row), pl.BlockSpec((HE, D), fix)],
        [pl.BlockSpec((tm, D), row), pl.BlockSpec((tm, HE), row), pl.BlockSpec((tm, HE), row)],
        [_sds((T, D), F32), _sds((T, HE), BF16), _sds((T, HE), F32)],
    )(*os, *lses, x, wo)


def _attn_out_bwd(dy, om, wo, name, after=()):
    T, D = dy.shape
    HE = wo.shape[0]
    E = HEAD_DIM
    tm = _pick(T, 512, 8)

    def body(dy_ref, om_ref, w_ref, dom_ref, dl_ref):
        dom = _dot_nt(dy_ref[...].astype(BF16), w_ref[...])
        dom_ref[...] = dom.astype(BF16)
        prod = dom * om_ref[...].astype(F32)
        for h in range(HE // E):
            hs = slice(h * E, (h + 1) * E)
            dl_ref[:, hs] = jnp.broadcast_to(jnp.sum(prod[:, hs], axis=-1, keepdims=True), (tm, E))

    row = lambda i: (i, 0)
    return _call(
        body, name, (T // tm,),
        [pl.BlockSpec((tm, D), row), pl.BlockSpec((tm, HE), row), pl.BlockSpec((HE, D), lambda i: (0, 0))],
        [pl.BlockSpec((tm, HE), row), pl.BlockSpec((tm, HE), row)],
        [_sds((T, HE), BF16), _sds((T, HE), F32)], after=after,
    )(dy, om, wo)


def _attn_bwd(qkv, base, HE, dom, lse, delta, window, dil, name):
    T = qkv.shape[0]
    H = HE // HEAD_DIM
    E = HEAD_DIM
    Q = ATTN_BLOCK
    nb = T // dil // Q
    steps = window // dil

    def body(q_ref, kc_ref, kp_ref, vc_ref, vp_ref, do_ref, l_ref, dl_ref, dq_ref, dk_ref, dv_ref, ck_sc, cv_sc):
        n = pl.program_id(1)

        @pl.when(n == 0)
        def _():
            ck_sc[...] = jnp.zeros_like(ck_sc)
            cv_sc[...] = jnp.zeros_like(cv_sc)

        @pl.when(n < nb)
        def _():
            valid = _band_mask(Q, steps, n)
            ck_old = ck_sc[...]
            cv_old = cv_sc[...]
            dqs, dks, dvs = [], [], []
            for h in range(H):
                hs = slice(h * E, (h + 1) * E)
                q = q_ref[:, hs]
                do = do_ref[:, hs]
                k2 = jnp.concatenate([kp_ref[:, hs], kc_ref[:, hs]], axis=0)
                v2 = jnp.concatenate([vp_ref[:, hs], vc_ref[:, hs]], axis=0)
                p = jnp.where(valid, jnp.exp(_dot_nt(q, k2) - _per_row(l_ref[:, hs], 2 * Q)), 0.0)
                ds = (p * (_dot_nt(do, v2) - _per_row(dl_ref[:, hs], 2 * Q))).astype(BF16)
                dqs.append(_dot(ds, k2).astype(BF16))
                dks.append(_dot_tn(q, ds).T)
                dvs.append(_dot_tn(do, p.astype(BF16)).T)
            cat = lambda parts: jnp.concatenate(parts, axis=1)
            dq_ref[...] = cat(dqs)
            dk_ref[...] = (ck_old + cat([d[:Q] for d in dks])).astype(BF16)
            dv_ref[...] = (cv_old + cat([d[:Q] for d in dvs])).astype(BF16)
            ck_sc[...] = cat([d[Q:] for d in dks])
            cv_sc[...] = cat([d[Q:] for d in dvs])

        @pl.when(n == nb)
        def _():
            dk_ref[...] = ck_sc[...].astype(BF16)
            dv_ref[...] = cv_sc[...].astype(BF16)

    nq = lambda n: jnp.minimum(n, nb - 1)
    blk = lambda s, back: pl.BlockSpec((Q, HE), lambda r, n: (jnp.maximum(nq(n) - back, 0) * dil + r, base + s))
    qblk = pl.BlockSpec((Q, HE), lambda r, n: (nq(n) * dil + r, 0))
    kblk = pl.BlockSpec((Q, HE), lambda r, n: (jnp.maximum(n - 1, 0) * dil + r, 0))
    return _call(
        body, name, (dil, nb + 1),
        [blk(0, 0), blk(1, 0), blk(1, 1), blk(2, 0), blk(2, 1), qblk, qblk, qblk],
        [qblk, kblk, kblk],
        [_sds((T, HE), BF16)] * 3,
        scratch=[pltpu.VMEM((Q, HE), F32), pltpu.VMEM((Q, HE), F32)],
    )(qkv, qkv, qkv, qkv, qkv, dom, lse, delta)


def _qk_norm_bwd(dq, dk, qkv, base, HE, gq, gk, name):
    T = dq.shape[0]
    E = HEAD_DIM
    tm = _pick(T, 512, 8)
    scale = E ** -0.5

    def body(dq_ref, dk_ref, q_ref, k_ref, gq_ref, gk_ref, oq_ref, ok_ref, dgq_ref, dgk_ref):
        @pl.when(pl.program_id(0) == 0)
        def _():
            dgq_ref[...] = jnp.zeros_like(dgq_ref)
            dgk_ref[...] = jnp.zeros_like(dgk_ref)

        gqv = gq_ref[...]
        gkv = gk_ref[...]
        c = gqv * gkv * scale
        dc = jnp.zeros((1, E), F32)
        for h in range(HE // E):
            hs = slice(h * E, (h + 1) * E)
            q = q_ref[:, hs].astype(F32)
            rq = _head_rstd(q)
            qh = q * rq
            a = dq_ref[:, hs].astype(F32)
            dc = dc + jnp.sum(a * qh, axis=0, keepdims=True)
            dqh = a * c
            oq_ref[:, hs] = (rq * (dqh - qh * (_row_sums(dqh * qh) * (1.0 / E)))).astype(BF16)
            k = k_ref[:, hs].astype(F32)
            rk = _head_rstd(k)
            kh = k * rk
            b = dk_ref[:, hs].astype(F32)
            ok_ref[:, hs] = (rk * (b - kh * (_row_sums(b * kh) * (1.0 / E)))).astype(BF16)
        dgq_ref[...] += dc * (gkv * scale)
        dgk_ref[...] += dc * (gqv * scale)

    row = lambda i: (i, 0)
    vec = pl.BlockSpec((1, E), lambda i: (0, 0))
    return _call(
        body, name, (T // tm,),
        [pl.BlockSpec((tm, HE), row), pl.BlockSpec((tm, HE), row), pl.BlockSpec((tm, HE), lambda i: (i, base)),
         pl.BlockSpec((tm, HE), lambda i: (i, base + 1)), vec, vec],
        [pl.BlockSpec((tm, HE), row), pl.BlockSpec((tm, HE), row), vec, vec],
        [_sds((T, HE), BF16), _sds((T, HE), BF16), _sds((1, E), F32), _sds((1, E), F32)],
    )(dq, dk, qkv, qkv, gq, gk)


def _attn_qkv_bwd(dqkv, x, g, dy, wqkv, name):
    T, D = x.shape
    HE = wqkv.shape[1] // 9
    tm = _pick(T, 256, 8)

    def body(*refs):
        d_refs = refs[:9]
        x_ref, g_ref, dy_ref, w_ref, dx_ref, dg_ref = refs[9:]

        @pl.when(pl.program_id(0) == 0)
        def _():
            dg_ref[...] = jnp.zeros_like(dg_ref)

        dh = _dot_nt(d_refs[0][...], w_ref[:, :HE])
        for s in range(1, 9):
            dh = dh + _dot_nt(d_refs[s][...], w_ref[:, s * HE:(s + 1) * HE])
        xv = x_ref[...]
        r = _rstd(xv)
        xh = xv * r
        dx_ref[...] = dy_ref[...] + _norm_bwd(dh, xh, r, g_ref[...])
        dg_ref[...] += jnp.sum(dh * xh, axis=0, keepdims=True)

    row = lambda i: (i, 0)
    fix = lambda i: (0, 0)
    return _call(
        body, name, (T // tm,),
        [pl.BlockSpec((tm, HE), row)] * 9 + [pl.BlockSpec((tm, D), row), pl.BlockSpec((1, D), fix), pl.BlockSpec((tm, D), row),
                                           pl.BlockSpec((D, 9 * HE), fix, pipeline_mode=pl.Buffered(1))],
        [pl.BlockSpec((tm, D), row), pl.BlockSpec((1, D), fix)],
        [_sds((T, D), F32), _sds((1, D), F32)],
    )(*dqkv, x, g, dy, wqkv)


def _loss_head(y, target, name):
    T, D = y.shape
    tm = _pick(T, 512, 8)

    def body(y_ref, t_ref, dy_ref, sq_ref):
        @pl.when(pl.program_id(0) == 0)
        def _():
            sq_ref[...] = jnp.zeros_like(sq_ref)

        err = y_ref[...] - t_ref[...]
        dy_ref[...] = err * (1.0 / D)
        sq_ref[...] += jnp.sum(err * err, axis=0, keepdims=True)

    row = lambda i: (i, 0)
    return _call(
        body, name, (T // tm,),
        [pl.BlockSpec((tm, D), row), pl.BlockSpec((tm, D), row)],
        [pl.BlockSpec((tm, D), row), pl.BlockSpec((1, D), lambda i: (0, 0))],
        [_sds((T, D), F32), _sds((1, D), F32)],
    )(y, target)


def _local_step(x, target, norm_g, b_pw1, w_dw, b_dw, cng, b_pw2, qn, kn, block_weights, grads_ready):
    T, D = x.shape
    ng = lambda l, k: norm_g[l, k][None, :]
    bn = _pick(D, 256, 128)

    w_in, w_out = [None] * 4, [None] * 4
    w_in[0], w_out[0] = block_weights(0, x)
    x1, *gu0 = _ffn_fwd(x, ng(0, 0), w_in[0], w_out[0], "ffn_fwd_0")
    pw1, pw2 = block_weights(1, x1)
    ag, u, hc = _conv_pre(x1, ng(0, 1), pw1, b_pw1, "conv_pre")
    c = _dwconv(u, w_dw, b_dw, "dwconv")
    x2, s = _conv_post(c, x1, cng, pw2, b_pw2, "conv_post")
    w_in[1], w_out[1] = block_weights(2, x2)
    x3, *gu1 = _ffn_fwd(x2, ng(0, 2), w_in[1], w_out[1], "ffn_fwd_1")
    w_in[2], w_out[2] = block_weights(3, x3)
    x4, *gu2 = _ffn_fwd(x3, ng(1, 0), w_in[2], w_out[2], "ffn_fwd_2")
    wqkv, wo = block_weights(4, x4)
    HE = wo.shape[0]
    F = w_out[0].shape[0]
    bf = _pick(F, 256, 128)
    bh = _pick(HE, 512, 128)
    qkv, att, ha = _attn_qkv(x4, ng(1, 1), wqkv, qn, kn, "attn_qkv")
    qkv_s = [(att, 3 * gi) if dil == 1 else (_streams(att, dil, True, f"qkv_streams_{gi}", 3 * gi * HE, 3 * HE), 0)
             for gi, (_, dil) in enumerate(ATTN_GROUPS)]
    tokens = lambda a, dil, name: a if dil == 1 else _streams(a, dil, False, name)
    streams = lambda a, dil, name: a if dil == 1 else _streams(a, dil, True, name)
    os, lses = [], []
    for gi, (window, dil) in enumerate(ATTN_GROUPS):
        o, l = _attn_fwd(*qkv_s[gi], HE, window, dil, f"attn_fwd_{gi}")
        os.append(tokens(o, dil, f"o_tokens_{gi}"))
        lses.append(tokens(l, dil, f"lse_tokens_{gi}"))
    x5, om, lse = _attn_merge(os, lses, x4, wo, "attn_merge")
    w_in[3], w_out[3] = block_weights(5, x5)
    x6, *gu3 = _ffn_fwd(x5, ng(1, 2), w_in[3], w_out[3], "ffn_fwd_3")
    dy, sq = _loss_head(x6, target, "loss_head")

    grads = {"ffn_w_in": [None] * 4, "ffn_w_out": [None] * 4}
    dnorm = [[None] * 3 for _ in range(2)]

    def ffn_back(lf, k, xin, gvec, dy, gu, after):
        dx, dg, dgate, dup, a, h, dyb = _ffn_bwd(xin, gvec, dy, gu[0], gu[1], w_in[lf], w_out[lf], f"ffn_bwd_{lf}", after=after)
        grads["ffn_w_in"][lf] = _mm_tn_parts(h, [dgate, dup], bf, 2 * F, 0, f"ffn_dw_in_{lf}")
        grads["ffn_w_out"][lf] = _mm_tn(a, dyb, bf, D, (F, D), (bf, D), lambda mi, ni: (mi, 0), 1.0, f"ffn_dw_out_{lf}")
        return dx, dg, grads_ready(k, (grads["ffn_w_in"][lf], grads["ffn_w_out"][lf]), dx)

    dx, dnorm[1][2], tok = ffn_back(3, 5, x5, ng(1, 2), dy, gu3, ())
    dom, delta = _attn_out_bwd(dx, om, wo, "attn_out_bwd", after=tok)
    grads["attn_w_o"] = _mm_tn(om, dx, HE, bn, (HE, D), (HE, bn), lambda mi, ni: (0, ni), 1.0, "attn_dw_o")
    dqkv, dgq, dgk = [], [], []
    for gi, (window, dil) in enumerate(ATTN_GROUPS):
        ds = _attn_bwd(*qkv_s[gi], HE, streams(dom, dil, f"dom_streams_{gi}"), streams(lse, dil, f"lse_streams_{gi}"),
                       streams(delta, dil, f"delta_streams_{gi}"), window, dil, f"attn_bwd_{gi}")
        dq, dk, dv = [tokens(d, dil, f"d{nm}_tokens_{gi}") for d, nm in zip(ds, "qkv")]
        dq, dk, a_, b_ = _qk_norm_bwd(dq, dk, qkv, 3 * gi, HE, qn[gi][None, :], kn[gi][None, :], f"qk_norm_bwd_{gi}")
        dqkv += [dq, dk, dv]
        dgq.append(a_)
        dgk.append(b_)
    grads["attn_q_norm"] = jnp.concatenate(dgq, axis=0)
    grads["attn_k_norm"] = jnp.concatenate(dgk, axis=0)
    d_qkv = None
    for gi in range(len(ATTN_GROUPS)):
        d_qkv = _mm_tn_parts(ha, dqkv[3 * gi:3 * gi + 3], bh, 9 * HE, 3 * gi * HE, f"attn_dw_qkv_{gi}", prev=d_qkv)
    grads["attn_w_qkv"] = d_qkv
    dx, dnorm[1][1] = _attn_qkv_bwd(dqkv, x4, ng(1, 1), dx, wqkv, "attn_qkv_bwd")
    tok = grads_ready(4, (grads["attn_w_qkv"], grads["attn_w_o"]), dx)
    dx, dnorm[1][0], tok = ffn_back(2, 3, x3, ng(1, 0), dx, gu2, tok)

    dx, dnorm[0][2], tok = ffn_back(1, 2, x2, ng(0, 2), dx, gu1, tok)
    dc, grads["conv_norm_g"], grads["conv_b_pw2"] = _conv_post_bwd(dx, c, cng, pw2, "conv_post_bwd", after=tok)
    grads["conv_w_pw2"] = _mm_tn(s, dx, D, bn, (D, D), (D, bn), lambda mi, ni: (0, ni), 1.0, "conv_dw_pw2")
    du, grads["conv_w_dw"], grads["conv_b_dw"] = _dwconv_bwd(dc, u, w_dw, "dwconv_bwd")
    dx, dnorm[0][1], dag, grads["conv_b_pw1"] = _conv_pre_bwd(du, ag, x1, ng(0, 1), dx, pw1, "conv_pre_bwd")
    grads["conv_w_pw1"] = _mm_tn(hc, dag, D, 2 * bn, (D, 2 * D), (D, 2 * bn), lambda mi, ni: (0, ni), 1.0, "conv_dw_pw1")
    tok = grads_ready(1, (grads["conv_w_pw1"], grads["conv_w_pw2"]), dx)
    dx, dnorm[0][0], _ = ffn_back(0, 0, x, ng(0, 0), dx, gu0, tok)

    grads["norm_g"] = jnp.concatenate([jnp.concatenate(r, axis=0)[None] for r in dnorm], axis=0)
    return sq, dx, grads


class _Sharded:
    def __init__(self, name, full3, half_axis, shard_axis, src, slab=None):
        self.name, self.full3, self.half_axis, self.shard_axis = name, tuple(full3), half_axis, shard_axis
        self.src, self.slab = src, slab

    def source(self, refs):
        return refs[self.src] if self.slab is None else refs[self.src].at[self.slab]

    def _cut(self, shape, axis, parts):
        s = list(shape)
        s[axis] //= parts
        return tuple(s)

    @property
    def shard3(self):
        return self._cut(self.full3, self.shard_axis, N_CHIPS)

    @property
    def pair3(self):
        return self._cut(self.full3, self.half_axis, 2)

    @property
    def part3(self):
        return self._cut(self.shard3, self.half_axis, 2)

    @staticmethod
    def _slice(ref, axis, idx, parts):
        n = ref.shape[axis] // parts
        start = idx * n
        minor = len(ref.shape) - 1 - axis
        if minor < 2 and not isinstance(start, int):
            start = pl.multiple_of(start, 128 if minor == 0 else (16 if n % 16 == 0 else 8))
        sl = [slice(None)] * len(ref.shape)
        sl[axis] = pl.ds(start, n)
        return ref.at[tuple(sl)]

    def half(self, ref, h):
        return self._slice(ref, self.half_axis, h, 2)

    def shard(self, ref, j):
        return self._slice(ref, self.shard_axis, j, N_CHIPS)


def _place():
    x, y, c = lax.axis_index("x"), lax.axis_index("y"), lax.axis_index("c")
    return x, y, c, 2 * x + y


_RELS = (1, 2, 3)


def _peer(x, y, rel):
    px = 1 - x if rel & 2 else x
    py = 1 - y if rel & 1 else y
    return px, py, 2 * px + py


ANY = pl.BlockSpec(memory_space=pl.ANY)


def _comm_call(body, name, n_in, out_shape, n_sems, aliases=None):
    return pl.pallas_call(
        body, name=name, in_specs=[ANY] * n_in, out_specs=[ANY] * len(out_shape), out_shape=out_shape,
        scratch_shapes=[pltpu.SemaphoreType.DMA((n,)) for n in n_sems],
        input_output_aliases=aliases or {},
        compiler_params=pltpu.CompilerParams(has_side_effects=True),
    )


def _remote(src, dst, send_sem, recv_sem, dev):
    return pltpu.make_async_remote_copy(src_ref=src, dst_ref=dst, send_sem=send_sem, recv_sem=recv_sem, device_id=dev,
                                        device_id_type=MESH)


def _gather_small(small_shards):
    ns = len(small_shards)

    def body(*refs):
        ins, outs = refs[:ns], refs[ns:2 * ns]
        lsem, ssem, rsem = refs[2 * ns:]
        x, y, c, me = _place()
        cols = lambda ref, j: _Sharded._slice(ref, 1, j, N_CHIPS)
        local = [pltpu.make_async_copy(ins[i], cols(outs[i], me), lsem.at[i]) for i in range(ns)]
        sends = []
        for i in range(ns):
            for k, rel in enumerate(_RELS):
                px, py, _ = _peer(x, y, rel)
                sends.append(_remote(ins[i], cols(outs[i], me), ssem.at[3 * i + k], rsem.at[3 * i + k], (px, py, c)))
        for cp in local + sends:
            cp.start()
        for i in range(ns):
            for k, rel in enumerate(_RELS):
                _, _, pj = _peer(x, y, rel)
                got = cols(outs[i], pj)
                _remote(got, got, ssem.at[3 * i + k], rsem.at[3 * i + k], (x, y, c)).wait_recv()
        for cp in sends:
            cp.wait_send()
        for cp in local:
            cp.wait()

    out_shape = [_sds((s.shape[0], s.shape[1] * N_CHIPS), F32) for s in small_shards]
    return _comm_call(body, "gather_small", ns, out_shape, [ns, 3 * ns, 3 * ns])(*small_shards)


HBM = pl.BlockSpec(memory_space=pltpu.HBM)
SEM = pl.BlockSpec(memory_space=pltpu.SEMAPHORE)
DATAFLOW = pltpu.SideEffectType.DATAFLOW_SIDE_EFFECTING


def _in_hbm(a):
    return pltpu.with_memory_space_constraint(a, pltpu.HBM)


def _cast_place(it, shard, scal, after=()):
    a_n, r_n, c_n = it.shard3
    tr = _pick(r_n, 256, 16)
    sa = it.shard_axis

    def body(sc_ref, s_ref, o_ref):
        o_ref[...] = s_ref[...].astype(BF16)

    if it.slab is None:
        src = pl.BlockSpec((1, tr, c_n), lambda a, rb, sc: (a, rb, 0))
    else:
        src = pl.BlockSpec((None, 1, tr, c_n), lambda a, rb, sc: (it.slab, a, rb, 0))
    dst = pl.BlockSpec((1, tr, c_n), lambda a, rb, sc: (a + sc[1] * (a_n if sa == 0 else 0), rb + sc[1] * (r_n // tr if sa == 1 else 0),
                                                       sc[1] if sa == 2 else 0))
    return _call(body, f"cast_place_{it.name}", (a_n, r_n // tr), [src], [dst], [_sds(it.full3, BF16)], prefetch=1,
                 after=after)(scal, shard)[0]


def _gather_start(items, fulls, name):
    ni = len(items)

    def body(*refs):
        outs = refs[ni:]
        ssem, rsem, full = outs[:ni], outs[ni:2 * ni], outs[2 * ni:3 * ni]
        x, y, c, me = _place()
        for i, it in enumerate(items):
            mine = it.half(it.shard(full[i], me), c)
            for k, rel in enumerate(_RELS):
                px, py, _ = _peer(x, y, rel)
                _remote(mine, mine, ssem[i].at[k], rsem[i].at[k], (px, py, c)).start()

    outs = pl.pallas_call(
        body, name=name, in_specs=[HBM] * ni, out_specs=[SEM] * (2 * ni) + [HBM] * ni,
        out_shape=[pltpu.SemaphoreType.DMA((3,))] * (2 * ni) + [pltpu.HBM(it.full3, BF16) for it in items],
        input_output_aliases={j: 2 * ni + j for j in range(ni)},
        compiler_params=pltpu.CompilerParams(has_side_effects=DATAFLOW),
    )(*[_in_hbm(f) for f in fulls])
    return outs[:ni], outs[ni:2 * ni], outs[2 * ni:]


def _gather_forward(items, fulls, ssems, rsems, after, name):
    ni = len(items)

    def body(*refs):
        ssem, rsem = refs[ni:2 * ni], refs[2 * ni:3 * ni]
        outs = refs[3 * ni + 1:]
        full, fsem, gsem = outs[:ni], outs[ni:2 * ni], outs[2 * ni:3 * ni]
        x, y, c, me = _place()
        sib = (x, y, 1 - c)
        for i, it in enumerate(items):
            for k, rel in enumerate(_RELS):
                _, _, pj = _peer(x, y, rel)
                got = it.half(it.shard(full[i], pj), c)
                _remote(got, got, ssem[i].at[k], rsem[i].at[k], sib).wait_recv()
                _remote(got, got, fsem[i].at[k], gsem[i].at[k], sib).start()
        for i, it in enumerate(items):
            mine = it.half(it.shard(full[i], me), c)
            for k in range(3):
                _remote(mine, mine, ssem[i].at[k], rsem[i].at[k], sib).wait_send()

    outs = pl.pallas_call(
        body, name=name, in_specs=[HBM] * ni + [SEM] * (2 * ni) + [ANY],
        out_specs=[HBM] * ni + [SEM] * (2 * ni),
        out_shape=[pltpu.HBM(it.full3, BF16) for it in items] + [pltpu.SemaphoreType.DMA((3,))] * (2 * ni),
        input_output_aliases={i: i for i in range(ni)},
        compiler_params=pltpu.CompilerParams(has_side_effects=DATAFLOW),
    )(*fulls, *ssems, *rsems, after)
    return outs[:ni], outs[ni:2 * ni], outs[2 * ni:]


def _gather_finish(items, fulls, fsems, gsems, name):
    ni = len(items)

    def body(*refs):
        fsem, gsem = refs[ni:2 * ni], refs[2 * ni:3 * ni]
        full = refs[3 * ni:]
        x, y, c, _ = _place()
        sib = (x, y, 1 - c)
        for i, it in enumerate(items):
            for k, rel in enumerate(_RELS):
                _, _, pj = _peer(x, y, rel)
                got = it.half(it.shard(full[i], pj), 1 - c)
                _remote(got, got, fsem[i].at[k], gsem[i].at[k], sib).wait_recv()
                sent = it.half(it.shard(full[i], pj), c)
                _remote(sent, sent, fsem[i].at[k], gsem[i].at[k], sib).wait_send()

    return pl.pallas_call(
        body, name=name, in_specs=[HBM] * ni + [SEM] * (2 * ni), out_specs=[HBM] * ni,
        out_shape=[pltpu.HBM(it.full3, BF16) for it in items],
        input_output_aliases={i: i for i in range(ni)},
        compiler_params=pltpu.CompilerParams(has_side_effects=DATAFLOW),
    )(*fulls, *fsems, *gsems)


def _exchange_start(name, arrays, n_sems, copies):
    na, ns = len(arrays), len(n_sems)

    def body(*refs):
        outs = refs[na:]
        ssem, rsem, thru, token = outs[:ns], outs[ns:2 * ns], outs[2 * ns:2 * ns + na], outs[-1]
        for send, _ in copies(thru, ssem, rsem):
            send.start()
        token[...] = jnp.zeros_like(token)

    outs = pl.pallas_call(
        body, name=name, in_specs=[HBM] * na,
        out_specs=[SEM] * (2 * ns) + [HBM] * na + [pl.BlockSpec(memory_space=pltpu.VMEM)],
        out_shape=[pltpu.SemaphoreType.DMA((n,)) for n in n_sems] * 2 + [pltpu.HBM(a.shape, a.dtype) for a in arrays]
        + [_sds((8, 128), F32)],
        input_output_aliases={j: 2 * ns + j for j in range(na)},
        compiler_params=pltpu.CompilerParams(has_side_effects=DATAFLOW),
    )(*[_in_hbm(a) for a in arrays])
    return outs[:ns], outs[ns:2 * ns], outs[2 * ns:2 * ns + na], outs[-1]


def _exchange_wait(name, arrays, ssems, rsems, copies, after):
    na, ns, nw = len(arrays), len(ssems), len(after)

    def body(*refs):
        ssem, rsem = refs[na:na + ns], refs[na + ns:na + 2 * ns]
        thru = refs[na + 2 * ns + nw:]
        for send, recv in copies(thru, ssem, rsem):
            recv.wait_recv()
            send.wait_send()

    return pl.pallas_call(
        body, name=name, in_specs=[HBM] * na + [SEM] * (2 * ns) + [ANY] * nw, out_specs=[HBM] * na,
        out_shape=[pltpu.HBM(a.shape, a.dtype) for a in arrays],
        input_output_aliases={i: i for i in range(na)},
        compiler_params=pltpu.CompilerParams(has_side_effects=DATAFLOW),
    )(*arrays, *ssems, *rsems, *after)


def _pair_copies(items):
    n = len(items)

    def copies(a, ssem, rsem):
        x, y, c, _ = _place()
        cps = [_remote(it.half(a[i], 1 - c), a[n + i], ssem[i].at[0], rsem[i].at[0], (x, y, 1 - c)) for i, it in enumerate(items)]
        return [(cp, cp) for cp in cps]

    return copies


def _chip_copies(items):
    n = len(items)

    def copies(a, ssem, rsem):
        x, y, c, _ = _place()
        cps = []
        for i, it in enumerate(items):
            for k, rel in enumerate(_RELS):
                px, py, pj = _peer(x, y, rel)
                cps.append(_remote(it.shard(a[i], pj), a[n + i].at[k], ssem[i].at[k], rsem[i].at[k], (px, py, c)))
        return [(cp, cp) for cp in cps]

    return copies


def _fill_copies(items):
    def copies(a, ssem, rsem):
        x, y, c, _ = _place()
        sib = (x, y, 1 - c)
        out = []
        for i, it in enumerate(items):
            mine, other = it.half(a[i], c), it.half(a[i], 1 - c)
            out.append((_remote(mine, mine, ssem[i].at[0], rsem[i].at[0], sib), _remote(other, other, ssem[i].at[0], rsem[i].at[0], sib)))
        return out

    return copies


def _ew_tiles(d):
    _, rows, cols = d.part3
    return _pick(rows, 256, 16), cols


def _pair_add(d, g_full, got, scal):
    tr, tc = _ew_tiles(d)
    a_n, r_n, c_n = d.pair3
    ha = d.half_axis

    def body(sc_ref, g_ref, r_ref, o_ref, ob_ref):
        s = g_ref[...].astype(F32) + r_ref[...].astype(F32)
        o_ref[...] = s
        ob_ref[...] = s.astype(BF16)

    blk = (1, tr, tc)
    same = pl.BlockSpec(blk, lambda a, rb, cb, sc: (a, rb, cb))
    mine = pl.BlockSpec(blk, lambda a, rb, cb, sc: (a + sc[0] * (a_n if ha == 0 else 0), rb + sc[0] * (r_n // tr if ha == 1 else 0), cb))
    return _call(body, f"pair_add_{d.name}", (a_n, r_n // tr, c_n // tc), [mine, same], [same, same],
                 [_sds(d.pair3, F32), _sds(d.pair3, BF16)], prefetch=1)(scal, g_full, got)


def _chip_reduce(d, pair_f32, got, scal):
    tr, tc = _ew_tiles(d)
    a_n, r_n, c_n = d.part3
    ha, sa = d.half_axis, d.shard_axis

    def body(sc_ref, p_ref, r0, r1, r2, o_ref):
        o_ref[...] = ((p_ref[...] + r0[...].astype(F32)) + r1[...].astype(F32)) + r2[...].astype(F32)

    blk = (1, tr, tc)
    own = pl.BlockSpec(blk, lambda a, rb, sc: (a + sc[1] * (a_n if sa == 0 else 0), rb + sc[1] * (r_n // tr if sa == 1 else 0),
                                               sc[1] if sa == 2 else 0))
    slot = lambda k: pl.BlockSpec((None,) + blk, lambda a, rb, sc: (k, a, rb, 0))
    out = pl.BlockSpec(blk, lambda a, rb, sc: (a + sc[0] * (a_n if ha == 0 else 0), rb + sc[0] * (r_n // tr if ha == 1 else 0), 0))
    return _call(body, f"chip_reduce_{d.name}", (a_n, r_n // tr), [own, slot(0), slot(1), slot(2)], [out],
                 [_sds(d.shard3, F32)], prefetch=1)(scal, pair_f32, got, got, got)[0]


def _adam_math(g, w, m, v):
    m = ADAM_B1 * m + (1.0 - ADAM_B1) * g
    v = ADAM_B2 * v + (1.0 - ADAM_B2) * (g * g)
    m_hat = m / (1.0 - ADAM_B1 ** ADAM_STEP)
    v_hat = v / (1.0 - ADAM_B2 ** ADAM_STEP)
    delta = -ADAM_LR * (m_hat / (jnp.sqrt(v_hat) + ADAM_EPS) + ADAM_WD * w)
    return delta, m, v


def _adam(d, g, w, m, v, prev=None):
    tr, tc = _ew_tiles(d)
    a_n, r_n, c_n = d.shard3
    n_prev = 0 if prev is None else 4

    def body(g_ref, w_ref, m_ref, v_ref, *rest):
        go_ref, d_ref, mo_ref, vo_ref = rest[n_prev:]
        gv = g_ref[...]
        go_ref[...] = gv
        d_ref[...], mo_ref[...], vo_ref[...] = _adam_math(gv, w_ref[...], m_ref[...], v_ref[...])

    plain = pl.BlockSpec((1, tr, tc), lambda a, rb: (a, rb, 0))
    if d.slab is None:
        wspec, shape = plain, d.shard3
    else:
        wspec, shape = pl.BlockSpec((None, 1, tr, tc), lambda a, rb: (d.slab, a, rb, 0)), (4,) + d.shard3
    in_specs = [plain] + [wspec] * 3
    args = [g, w, m, v]
    aliases = None
    if prev is not None:
        in_specs += [pl.BlockSpec(memory_space=pl.ANY)] * 4
        args += list(prev)
        aliases = {4 + k: k for k in range(4)}
    return _call(body, f"adam_{d.name}", (a_n, r_n // tr), in_specs, [wspec] * 4, [_sds(shape, F32)] * 4, aliases=aliases)(*args)


def _adam_small(gs, ws, ms, vs):
    n = len(gs)

    def body(*refs):
        for i in range(n):
            g, w, m, v = (refs[k * n + i][...] for k in range(4))
            d, mo, vo = _adam_math(g, w, m, v)
            refs[4 * n + i][...] = d
            refs[5 * n + i][...] = mo
            refs[6 * n + i][...] = vo

    vm = pl.BlockSpec(memory_space=pltpu.VMEM)
    outs = pl.pallas_call(body, name="adam_small", in_specs=[vm] * (4 * n), out_specs=[vm] * (3 * n),
                          out_shape=[_sds(g.shape, F32) for g in gs] * 3)(*gs, *ws, *ms, *vs)
    return outs[:n], outs[n:2 * n], outs[2 * n:]


def _allreduce_small(packed):
    rows, cols = packed.shape
    others = [(dx, dy, dc) for dx in (0, 1) for dy in (0, 1) for dc in (0, 1) if (dx, dy, dc) != (0, 0, 0)]

    def body(in_ref, out_ref, buf, ssem, rsem):
        x, y, c, _ = _place()
        lin = 4 * x + 2 * y + c
        buf[lin] = in_ref[...]
        cps = []
        for k, (dx, dy, dc) in enumerate(others):
            px = 1 - x if dx else x
            py = 1 - y if dy else y
            pc = 1 - c if dc else c
            cps.append((pltpu.make_async_remote_copy(src_ref=in_ref, dst_ref=buf.at[lin], send_sem=ssem.at[k], recv_sem=rsem.at[k],
                                                     device_id=(px, py, pc), device_id_type=MESH), 4 * px + 2 * py + pc))
        for cp, _ in cps:
            cp.start()
        for k, (cp, plin) in enumerate(cps):
            pltpu.make_async_remote_copy(src_ref=in_ref, dst_ref=buf.at[plin], send_sem=ssem.at[k], recv_sem=rsem.at[k],
                                         device_id=(x, y, c), device_id_type=MESH).wait_recv()
        for cp, _ in cps:
            cp.wait_send()
        acc = buf[0]
        for dev in range(1, 8):
            acc = acc + buf[dev]
        out_ref[...] = acc

    vm = pl.BlockSpec(memory_space=pltpu.VMEM)
    return pl.pallas_call(
        body, name="allreduce_small", in_specs=[vm], out_specs=vm, out_shape=_sds((rows, cols), F32),
        scratch_shapes=[pltpu.VMEM((8, rows, cols), F32), pltpu.SemaphoreType.DMA((7,)), pltpu.SemaphoreType.DMA((7,))],
        compiler_params=pltpu.CompilerParams(has_side_effects=True),
    )(packed)


_BIG = ("ffn_w_in", "ffn_w_out", "conv_w_pw1", "conv_w_pw2", "attn_w_qkv", "attn_w_o")
_SMALL = ("norm_g", "conv_b_pw1", "conv_w_dw", "conv_b_dw", "conv_norm_g", "conv_b_pw2", "attn_q_norm", "attn_k_norm")
_NAMES = ("norm_g", "ffn_w_in", "ffn_w_out", "conv_w_pw1", "conv_b_pw1", "conv_w_dw", "conv_b_dw", "conv_norm_g", "conv_w_pw2",
          "conv_b_pw2", "attn_w_qkv", "attn_q_norm", "attn_k_norm", "attn_w_o")


def _rows8(a, width):
    a = a.reshape(-1, min(a.shape[-1], width))
    return jnp.pad(a, ((0, -a.shape[0] % 8), (0, width - a.shape[1])))


def kernel(x, norm_g, ffn_w_in, ffn_w_out, conv_w_pw1, conv_b_pw1, conv_w_dw, conv_b_dw, conv_norm_g, conv_w_pw2, conv_b_pw2, attn_w_qkv, attn_q_norm, attn_k_norm, attn_w_o, loss_target, m_norm_g, m_ffn_w_in, m_ffn_w_out, m_conv_w_pw1, m_conv_b_pw1, m_conv_w_dw, m_conv_b_dw, m_conv_norm_g, m_conv_w_pw2, m_conv_b_pw2, m_attn_w_qkv, m_attn_q_norm, m_attn_k_norm, m_attn_w_o, v_norm_g, v_ffn_w_in, v_ffn_w_out, v_conv_w_pw1, v_conv_b_pw1, v_conv_w_dw, v_conv_b_dw, v_conv_norm_g, v_conv_w_pw2, v_conv_b_pw2, v_attn_w_qkv, v_attn_q_norm, v_attn_k_norm, v_attn_w_o):
    w = dict(zip(_NAMES, (norm_g, ffn_w_in, ffn_w_out, conv_w_pw1, conv_b_pw1, conv_w_dw, conv_b_dw, conv_norm_g, conv_w_pw2,
                          conv_b_pw2, attn_w_qkv, attn_q_norm, attn_k_norm, attn_w_o)))
    m = dict(zip(_NAMES, (m_norm_g, m_ffn_w_in, m_ffn_w_out, m_conv_w_pw1, m_conv_b_pw1, m_conv_w_dw, m_conv_b_dw, m_conv_norm_g,
                          m_conv_w_pw2, m_conv_b_pw2, m_attn_w_qkv, m_attn_q_norm, m_attn_k_norm, m_attn_w_o)))
    v = dict(zip(_NAMES, (v_norm_g, v_ffn_w_in, v_ffn_w_out, v_conv_w_pw1, v_conv_b_pw1, v_conv_w_dw, v_conv_b_dw, v_conv_norm_g,
                          v_conv_w_pw2, v_conv_b_pw2, v_attn_w_qkv, v_attn_q_norm, v_attn_k_norm, v_attn_w_o)))
    T, D = x.shape[1:]
    F = ffn_w_out.shape[2] * N_CHIPS
    HE = attn_w_o.shape[1] * N_CHIPS
    cx, cy, cc = lax.axis_index("x"), lax.axis_index("y"), lax.axis_index("c")
    me = 2 * cx + cy
    scal = jnp.stack([cc, me]).astype(jnp.int32)

    ffn_in = lambda lf: _Sharded(f"ffn_w_in_{lf}", (2, D // 2, 2 * F), 0, 2, "ffn_w_in", lf)
    ffn_out = lambda lf: _Sharded(f"ffn_w_out_{lf}", (4, F // 4, D), 1, 0, "ffn_w_out", lf)
    items = [
        ffn_in(0), ffn_out(0),
        _Sharded("conv_w_pw1", (2, D // 2, 2 * D), 0, 2, "conv_w_pw1"), _Sharded("conv_w_pw2", (4, D // 4, D), 1, 0, "conv_w_pw2"),
        ffn_in(1), ffn_out(1), ffn_in(2), ffn_out(2),
        _Sharded("attn_w_qkv", (2, D // 2, 9 * HE), 0, 2, "attn_w_qkv"), _Sharded("attn_w_o", (4, HE // 4, D), 1, 0, "attn_w_o"),
        ffn_in(3), ffn_out(3),
    ]
    mat_shapes = {"ffn_w_in": (D, 2 * F), "ffn_w_out": (F, D), "conv_w_pw1": (D, 2 * D), "conv_w_pw2": (D, D),
                  "attn_w_qkv": (D, 9 * HE), "attn_w_o": (HE, D)}

    def as_shards(a, n):
        it = next(i for i in items if i.src == n)
        return a.reshape(((4,) if it.slab is not None else ()) + it.shard3)

    norm_full, dw_full = _gather_small([norm_g.reshape(6, D // 4), conv_w_dw.reshape(CONV_WIDTH, D // 4)])
    place = lambda its, after: [_cast_place(it, as_shards(w[it.src], it.src), scal, after) for it in its]
    ssems, rsems, fulls = _gather_start(items[:2], place(items[:2], ()), "gather_start_first")
    more = _gather_start(items[2:], place(items[2:], fulls[:1]), "gather_start_rest")
    ssems, rsems, fulls = [list(a) + list(b) for a, b in zip((ssems, rsems, fulls), more)]

    def block_weights(k, after):
        sel = slice(2 * k, 2 * k + 2)
        got, fsems, gsems = _gather_forward(items[sel], fulls[sel], ssems[sel], rsems[sel], after, f"gather_forward_{k}")
        done = _gather_finish(items[sel], got, fsems, gsems, f"gather_finish_{k}")
        return [a.reshape(mat_shapes[it.src]) for a, it in zip(done, items[sel])]

    res = {}
    flight = []

    def advance(entry, k, dx):
        stage, its, st = entry
        n = len(its)
        if stage == 1:
            ssem, rsem, arrs = st
            done = _exchange_wait(f"grads_pair_wait_{k}", arrs, ssem, rsem, _pair_copies(its), dx)
            sums = [_pair_add(it, g, r, scal) for it, g, r in zip(its, done[:n], done[n:])]
            land = [lax.empty((3,) + it.part3, BF16) for it in its]
            ssem, rsem, arrs, tok = _exchange_start(f"grads_chip_start_{k}", [p[1] for p in sums] + land, [3] * n, _chip_copies(its))
            return (2, its, (ssem, rsem, arrs, [p[0] for p in sums])), tok
        if stage == 2:
            ssem, rsem, arrs, p32 = st
            got = _exchange_wait(f"grads_chip_wait_{k}", arrs, ssem, rsem, _chip_copies(its), dx)[n:]
            red = [_chip_reduce(it, p, r, scal) for it, p, r in zip(its, p32, got)]
            ssem, rsem, arrs, tok = _exchange_start(f"grads_fill_start_{k}", red, [1] * n, _fill_copies(its))
            return (3, its, (ssem, rsem, arrs)), tok
        ssem, rsem, arrs = st
        for it, g in zip(its, _exchange_wait(f"grads_fill_wait_{k}", arrs, ssem, rsem, _fill_copies(its), dx)):
            nm = it.src
            res[nm] = _adam(it, g, as_shards(w[nm], nm), as_shards(m[nm], nm), as_shards(v[nm], nm), prev=res.get(nm))
        return None, None

    def step_flight(dx):
        toks, left = [], []
        for k, entry in flight:
            entry, tok = advance(entry, k, dx)
            if entry is not None:
                left.append((k, entry))
                toks.append(tok)
        flight[:] = left
        return toks

    def grads_ready(k, pairs, dx):
        its = items[2 * k:2 * k + 2]
        toks = step_flight((dx,))
        g16 = [p.reshape(it.full3) for p, it in zip(pairs, its)]
        land = [lax.empty(it.pair3, BF16) for it in its]
        ssem, rsem, arrs, tok = _exchange_start(f"grads_pair_start_{k}", g16 + land, [1] * len(its), _pair_copies(its))
        flight.append((k, (1, its, (ssem, rsem, arrs))))
        return toks + [tok]

    sq, dx, grads = _local_step(x[0], loss_target[0], norm_full.reshape(2, 3, D), conv_b_pw1, dw_full, conv_b_dw, conv_norm_g,
                                conv_b_pw2, attn_q_norm[0], attn_k_norm[0], block_weights, grads_ready)
    loss = lax.psum(0.5 * jnp.sum(sq) / D, ("x", "y", "c"))
    k_last, entry = flight.pop()
    entry, tok = advance(entry, k_last, (dx,))
    while flight:
        step_flight((dx, tok))

    out_g, out_d, out_m, out_v = {}, {}, {}, {}

    parts = [_rows8(grads[n], D) for n in _SMALL]
    tot = _allreduce_small(jnp.concatenate(parts, axis=0))
    sg, r0 = {}, 0
    for n, p in zip(_SMALL, parts):
        last = grads[n].shape[-1]
        g = tot[r0:r0 + grads[n].size // min(last, D), :min(last, D)].reshape(-1, last)
        r0 += p.shape[0]
        if n in ("norm_g", "conv_w_dw"):
            g = lax.dynamic_slice_in_dim(g, me * (D // 4), D // 4, axis=1)
        sg[n] = g
    flat = lambda a: a.reshape(-1, a.shape[-1])
    ds, ms, vs = _adam_small([sg[n] for n in _SMALL], [flat(w[n]) for n in _SMALL], [flat(m[n]) for n in _SMALL],
                             [flat(v[n]) for n in _SMALL])
    for i, n in enumerate(_SMALL):
        out_g[n], out_d[n], out_m[n], out_v[n] = (a.reshape(w[n].shape) for a in (sg[n], ds[i], ms[i], vs[i]))

    behind = tuple(r[1] for r in res.values()) + tuple(ds)
    while entry is not None:
        entry, _ = advance(entry, k_last, behind)
    for n in _BIG:
        out_g[n], out_d[n], out_m[n], out_v[n] = (a.reshape(w[n].shape) for a in res[n])

    return (loss, dx[None], *[out_g[n] for n in _NAMES], *[out_d[n] for n in _NAMES], *[out_m[n] for n in _NAMES],
            *[out_v[n] for n in _NAMES])
```

```python
import functools

import jax
import jax.numpy as jnp
from jax import lax
from jax.experimental import pallas as pl
from jax.experimental.pallas import tpu as pltpu

F32 = jnp.float32
BF16 = jnp.bfloat16
MESH = pl.DeviceIdType.MESH

NORM_EPS = 1e-6
CONV_WIDTH = 31
ATTN_GROUPS = ((128, 1), (512, 4), (2048, 16))
ATTN_BLOCK = 128
HEAD_DIM = 128
N_CHIPS = 4

ADAM_LR = 0.001
ADAM_B1 = 0.9
ADAM_B2 = 0.999
ADAM_EPS = 1e-08
ADAM_WD = 0.01
ADAM_STEP = 10

VMEM_LIMIT = 56 * 1024 * 1024
NT_DIMS = (((1,), (1,)), ((), ()))
TN_DIMS = (((0,), (0,)), ((), ()))


def _pick(n, pref, mult):
    t = (min(n, pref) // mult) * mult
    while t >= mult:
        if n % t == 0:
            return t
        t -= mult
    return n


def _call(body, name, grid, in_specs, out_specs, out_shape, scratch=(), aliases=None, prefetch=0, after=()):
    params = pltpu.CompilerParams(dimension_semantics=("arbitrary",) * len(grid), vmem_limit_bytes=VMEM_LIMIT)
    after = tuple(after)
    if after:
        inner, n_in = body, prefetch + len(in_specs)

        def body(*refs):
            return inner(*refs[:n_in], *refs[n_in + len(after):])

        in_specs = list(in_specs) + [pl.BlockSpec(memory_space=pl.ANY)] * len(after)
    if prefetch:
        spec = pltpu.PrefetchScalarGridSpec(
            num_scalar_prefetch=prefetch, grid=grid, in_specs=in_specs, out_specs=out_specs, scratch_shapes=list(scratch)
        )
        call = pl.pallas_call(body, name=name, grid_spec=spec, out_shape=out_shape, compiler_params=params,
                              input_output_aliases=aliases or {})
    else:
        call = pl.pallas_call(body, name=name, grid=grid, in_specs=in_specs, out_specs=out_specs, out_shape=out_shape,
                              scratch_shapes=list(scratch), compiler_params=params, input_output_aliases=aliases or {})
    return lambda *args: call(*args, *after)


def _sds(shape, dtype):
    return jax.ShapeDtypeStruct(shape, dtype)


def _sig(x):
    return 1.0 / (1.0 + jnp.exp(-x))


def _rstd(x):
    return lax.rsqrt(jnp.mean(x * x, axis=-1, keepdims=True) + NORM_EPS)


def _norm_bwd(dy, xhat, r, g):
    dxh = dy * g
    return r * (dxh - xhat * jnp.mean(dxh * xhat, axis=-1, keepdims=True))


def _dot(a, b):
    return jnp.dot(a, b, preferred_element_type=F32)


def _dot_nt(a, b):
    return lax.dot_general(a, b, NT_DIMS, preferred_element_type=F32)


def _dot_tn(a, b):
    return lax.dot_general(a, b, TN_DIMS, preferred_element_type=F32)


def _ffn_fwd(x, g, w_in, w_out, name):
    T, D = x.shape
    F = w_out.shape[0]
    tm = _pick(T, 256, 8)

    def body(x_ref, g_ref, wg_ref, wu_ref, wo_ref, xo_ref, gate_ref, up_ref):
        xv = x_ref[...]
        h = (xv * _rstd(xv) * g_ref[...]).astype(BF16)
        gate = _dot(h, wg_ref[...])
        up = _dot(h, wu_ref[...])
        gate_ref[...] = gate.astype(BF16)
        up_ref[...] = up.astype(BF16)
        a = (gate * _sig(gate) * up).astype(BF16)
        xo_ref[...] = xv + 0.5 * _dot(a, wo_ref[...])

    row = lambda i: (i, 0)
    held = lambda shape, k: pl.BlockSpec(shape, lambda i: (0, k), pipeline_mode=pl.Buffered(1))
    return _call(
        body, name, (T // tm,),
        [pl.BlockSpec((tm, D), row), pl.BlockSpec((1, D), lambda i: (0, 0)), held((D, F), 0), held((D, F), 1), held((F, D), 0)],
        [pl.BlockSpec((tm, D), row), pl.BlockSpec((tm, F), row), pl.BlockSpec((tm, F), row)],
        [_sds((T, D), F32), _sds((T, F), BF16), _sds((T, F), BF16)],
    )(x, g, w_in, w_in, w_out)


def _ffn_bwd(x, g, dy, gate, up, w_in, w_out, name, after=()):
    T, D = x.shape
    F = w_out.shape[0]

    def body_a(dy_ref, gate_ref, up_ref, wo_ref, dgate_ref, dup_ref, a_ref):
        dyb = (0.5 * dy_ref[...]).astype(BF16)
        gate = gate_ref[...].astype(F32)
        up = up_ref[...].astype(F32)
        sg = _sig(gate)
        sl = gate * sg
        a_ref[...] = (sl * up).astype(BF16)
        da = _dot_nt(dyb, wo_ref[...])
        dgate_ref[...] = (da * up * (sg * (1.0 + gate * (1.0 - sg)))).astype(BF16)
        dup_ref[...] = (da * sl).astype(BF16)

    ta = _pick(T, 256, 8)
    tile = pl.BlockSpec((ta, F), lambda i: (i, 0))
    dgate, dup, a = _call(
        body_a, name + "_hidden", (T // ta,),
        [pl.BlockSpec((ta, D), lambda i: (i, 0)), tile, tile,
         pl.BlockSpec((F, D), lambda i: (0, 0), pipeline_mode=pl.Buffered(1))],
        [tile, tile, tile],
        [_sds((T, F), BF16)] * 3, after=after,
    )(dy, gate, up, w_out)

    def body_b(x_ref, g_ref, dy_ref, dgate_ref, dup_ref, wg_ref, wu_ref, dx_ref, dg_ref, h_ref, dyb_ref):
        @pl.when(pl.program_id(0) == 0)
        def _():
            dg_ref[...] = jnp.zeros_like(dg_ref)

        xv = x_ref[...]
        r = _rstd(xv)
        xh = xv * r
        h_ref[...] = (xh * g_ref[...]).astype(BF16)
        dyb_ref[...] = (0.5 * dy_ref[...]).astype(BF16)
        dh = _dot_nt(dgate_ref[...], wg_ref[...]) + _dot_nt(dup_ref[...], wu_ref[...])
        dx_ref[...] = dy_ref[...] + _norm_bwd(dh, xh, r, g_ref[...])
        dg_ref[...] += jnp.sum(dh * xh, axis=0, keepdims=True)

    tb = _pick(T, 512, 8)
    row = lambda i: (i, 0)
    held = lambda k: pl.BlockSpec((D, F), lambda i: (0, k), pipeline_mode=pl.Buffered(1))
    dx, dg, h, dyb = _call(
        body_b, name, (T // tb,),
        [pl.BlockSpec((tb, D), row), pl.BlockSpec((1, D), lambda i: (0, 0)), pl.BlockSpec((tb, D), row),
         pl.BlockSpec((tb, F), row), pl.BlockSpec((tb, F), row), held(0), held(1)],
        [pl.BlockSpec((tb, D), row), pl.BlockSpec((1, D), lambda i: (0, 0)), pl.BlockSpec((tb, D), row), pl.BlockSpec((tb, D), row)],
        [_sds((T, D), F32), _sds((1, D), F32), _sds((T, D), BF16), _sds((T, D), BF16)],
    )(x, g, dy, dgate, dup, w_in, w_in)
    return dx, dg, dgate, dup, a, h, dyb


def _mm_tn(a, b, bm, bn, out_shape, out_block, out_map, scale, name, prev=None):
    K, M = a.shape
    N = b.shape[1]

    def body(a_ref, b_ref, *rest):
        rest[-1][...] = (_dot_tn(a_ref[...].astype(BF16), b_ref[...].astype(BF16)) * scale).astype(BF16)

    in_specs = [pl.BlockSpec((K, bm), lambda mi, ni: (0, mi)), pl.BlockSpec((K, bn), lambda mi, ni: (0, ni))]
    args = [a, b]
    aliases = None
    if prev is not None:
        in_specs += [pl.BlockSpec(memory_space=pl.ANY)]
        args += [prev]
        aliases = {2: 0}
    return _call(body, name, (M // bm, N // bn), in_specs, [pl.BlockSpec(out_block, out_map)],
                 [_sds(out_shape, BF16)], aliases=aliases)(*args)[0]


def _mm_tn_parts(a, bs, bn, cols, col0, name, prev=None, after=()):
    K, M = a.shape
    N = bs[0].shape[1]
    nt = N // bn
    ns = len(bs)

    def body(a_ref, *refs):
        o_ref = refs[-1]
        part = pl.program_id(0) // nt
        for s in range(ns):
            @pl.when(part == s)
            def _(s=s):
                o_ref[...] = _dot_tn(a_ref[...], refs[s][...]).astype(BF16)

    part_spec = lambda s: pl.BlockSpec((K, bn), lambda j: (0, jnp.clip(j - s * nt, 0, nt - 1)))
    in_specs = [pl.BlockSpec((K, M), lambda j: (0, 0))] + [part_spec(s) for s in range(ns)]
    args = [a, *bs]
    aliases = None
    if prev is not None:
        in_specs += [pl.BlockSpec(memory_space=pl.ANY)]
        args += [prev]
        aliases = {ns + 1: 0}
    return _call(body, name, (ns * nt,), in_specs, [pl.BlockSpec((M, bn), lambda j: (0, col0 // bn + j))],
                 [_sds((M, cols), BF16)], aliases=aliases, after=after)(*args)[0]


def _conv_pre(x, g, w1, b1, name):
    T, D = x.shape
    tm = _pick(T, 512, 8)

    def body(x_ref, g_ref, w_ref, b_ref, ag_ref, u_ref, h_ref):
        xv = x_ref[...]
        h = (xv * _rstd(xv) * g_ref[...]).astype(BF16)
        h_ref[...] = h
        ag = _dot(h, w_ref[...]) + b_ref[...]
        ag_ref[...] = ag.astype(BF16)
        u_ref[...] = ag[:, :D] * _sig(ag[:, D:])

    return _call(
        body, name, (T // tm,),
        [pl.BlockSpec((tm, D), lambda i: (i, 0)), pl.BlockSpec((1, D), lambda i: (0, 0)),
         pl.BlockSpec((D, 2 * D), lambda i: (0, 0)), pl.BlockSpec((1, 2 * D), lambda i: (0, 0))],
        [pl.BlockSpec((tm, 2 * D), lambda i: (i, 0)), pl.BlockSpec((tm, D), lambda i: (i, 0)),
         pl.BlockSpec((tm, D), lambda i: (i, 0))],
        [_sds((T, 2 * D), BF16), _sds((T, D), F32), _sds((T, D), BF16)],
    )(x, g, w1, b1)


_DW_PAD = 32
_DW_CHUNK = 256


def _dwconv(u, w, b, name):
    T, D = u.shape
    K = w.shape[0]
    ch = _pick(T, _DW_CHUNK, 8)
    lead = _DW_PAD - (K - 1)

    def body(u_ref, w_ref, b_ref, c_ref, ext):
        ext[pl.ds(0, _DW_PAD), :] = jnp.zeros((_DW_PAD, 128), F32)
        ext[pl.ds(_DW_PAD, T), :] = u_ref[...]
        for c0 in range(0, T, ch):
            acc = jnp.zeros((ch, 128), F32) + b_ref[...]
            for k in range(K):
                acc = acc + w_ref[pl.ds(k, 1), :] * ext[pl.ds(c0 + lead + k, ch), :]
            c_ref[pl.ds(c0, ch), :] = acc

    return _call(
        body, name, (D // 128,),
        [pl.BlockSpec((T, 128), lambda i: (0, i)), pl.BlockSpec((K, 128), lambda i: (0, i)),
         pl.BlockSpec((1, 128), lambda i: (0, i))],
        [pl.BlockSpec((T, 128), lambda i: (0, i))],
        [_sds((T, D), F32)],
        scratch=[pltpu.VMEM((T + _DW_PAD, 128), F32)],
    )(u, w, b)[0]


def _dwconv_bwd(dc, u, w, name):
    T, D = u.shape
    K = w.shape[0]
    ch = _pick(T, _DW_CHUNK, 8)

    def body(dc_ref, u_ref, w_ref, du_ref, dw_ref, db_ref, dext):
        dext[pl.ds(0, T), :] = dc_ref[...]
        dext[pl.ds(T, _DW_PAD), :] = jnp.zeros((_DW_PAD, 128), F32)
        dws =[jnp.zeros((8, 128), F32) for _ in range(K)]
        dbs = jnp.zeros((8, 128), F32)
        for c0 in range(0, T, ch):
            uv = u_ref[pl.ds(c0, ch), :]
            dbs = dbs + jnp.sum(dc_ref[pl.ds(c0, ch), :].reshape(ch // 8, 8, 128), axis=0)
            acc = jnp.zeros((ch, 128), F32)
            for k in range(K):
                win = dext[pl.ds(c0 + (K - 1) - k, ch), :]
                acc = acc + w_ref[pl.ds(k, 1), :] * win
                dws[k] = dws[k] + jnp.sum((win * uv).reshape(ch // 8, 8, 128), axis=0)
            du_ref[pl.ds(c0, ch), :] = acc
        for k in range(K):
            dw_ref[pl.ds(k, 1), :] = jnp.sum(dws[k], axis=0, keepdims=True)
        db_ref[...] = jnp.sum(dbs, axis=0, keepdims=True)

    return _call(
        body, name, (D // 128,),
        [pl.BlockSpec((T, 128), lambda i: (0, i)), pl.BlockSpec((T, 128), lambda i: (0, i)),
         pl.BlockSpec((K, 128), lambda i: (0, i))],
        [pl.BlockSpec((T, 128), lambda i: (0, i)), pl.BlockSpec((K, 128), lambda i: (0, i)),
         pl.BlockSpec((1, 128), lambda i: (0, i))],
        [_sds((T, D), F32), _sds((K, D), F32), _sds((1, D), F32)],
        scratch=[pltpu.VMEM((T + _DW_PAD, 128), F32)],
    )(dc, u, w)


def _conv_post(c, x, ng, w2, b2, name):
    T, D = x.shape
    tm = _pick(T, 512, 8)

    def body(c_ref, x_ref, ng_ref, w_ref, b_ref, xo_ref, s_ref):
        cv = c_ref[...]
        n = cv * _rstd(cv) * ng_ref[...]
        s = (n * _sig(n)).astype(BF16)
        s_ref[...] = s
        xo_ref[...] = x_ref[...] + _dot(s, w_ref[...]) + b_ref[...]

    row = lambda i: (i, 0)
    fix = lambda i: (0, 0)
    return _call(
        body, name, (T // tm,),
        [pl.BlockSpec((tm, D), row), pl.BlockSpec((tm, D), row), pl.BlockSpec((1, D), fix),
         pl.BlockSpec((D, D), fix), pl.BlockSpec((1, D), fix)],
        [pl.BlockSpec((tm, D), row), pl.BlockSpec((tm, D), row)],
        [_sds((T, D), F32), _sds((T, D), BF16)],
    )(c, x, ng, w2, b2)


def _conv_post_bwd(dy, c, ng, w2, name, after=()):
    T, D = dy.shape
    tm = _pick(T, 512, 8)

    def body(dy_ref, c_ref, ng_ref, w_ref, dc_ref, dng_ref, db_ref):
        @pl.when(pl.program_id(0) == 0)
        def _():
            dng_ref[...] = jnp.zeros_like(dng_ref)
            db_ref[...] = jnp.zeros_like(db_ref)

        dyv = dy_ref[...]
        ds = _dot_nt(dyv.astype(BF16), w_ref[...])
        cv = c_ref[...]
        r = _rstd(cv)
        ch = cv * r
        n = ch * ng_ref[...]
        sg = _sig(n)
        dn = ds * (sg * (1.0 + n * (1.0 - sg)))
        dc_ref[...] = _norm_bwd(dn, ch, r, ng_ref[...])
        dng_ref[...] += jnp.sum(dn * ch, axis=0, keepdims=True)
        db_ref[...] += jnp.sum(dyv, axis=0, keepdims=True)

    row = lambda i: (i, 0)
    fix = lambda i: (0, 0)
    return _call(
        body, name, (T // tm,),
        [pl.BlockSpec((tm, D), row), pl.BlockSpec((tm, D), row), pl.BlockSpec((1, D), fix), pl.BlockSpec((D, D), fix)],
        [pl.BlockSpec((tm, D), row), pl.BlockSpec((1, D), fix), pl.BlockSpec((1, D), fix)],
        [_sds((T, D), F32), _sds((1, D), F32), _sds((1, D), F32)], after=after,
    )(dy, c, ng, w2)


def _conv_pre_bwd(du, ag, x, g, dy, w1, name):
    T, D = x.shape
    tm = _pick(T, 512, 8)

    def body(du_ref, ag_ref, x_ref, g_ref, dy_ref, w_ref, dx_ref, dg_ref, dag_ref, db_ref):
        @pl.when(pl.program_id(0) == 0)
        def _():
            dg_ref[...] = jnp.zeros_like(dg_ref)
            db_ref[...] = jnp.zeros_like(db_ref)

        duv = du_ref[...]
        a = ag_ref[:, :D].astype(F32)
        gt = ag_ref[:, D:].astype(F32)
        sg = _sig(gt)
        da = duv * sg
        dgt = duv * a * sg * (1.0 - sg)
        db_ref[:, :D] += jnp.sum(da, axis=0, keepdims=True)
        db_ref[:, D:] += jnp.sum(dgt, axis=0, keepdims=True)
        dab = da.astype(BF16)
        dgb = dgt.astype(BF16)
        dag_ref[:, :D] = dab
        dag_ref[:, D:] = dgb
        dh = _dot_nt(dab, w_ref[:, :D]) + _dot_nt(dgb, w_ref[:, D:])
        xv = x_ref[...]
        r = _rstd(xv)
        xh = xv * r
        dx_ref[...] = dy_ref[...] + _norm_bwd(dh, xh, r, g_ref[...])
        dg_ref[...] += jnp.sum(dh * xh, axis=0, keepdims=True)

    row = lambda i: (i, 0)
    fix = lambda i: (0, 0)
    return _call(
        body, name, (T // tm,),
        [pl.BlockSpec((tm, D), row), pl.BlockSpec((tm, 2 * D), row), pl.BlockSpec((tm, D), row), pl.BlockSpec((1, D), fix),
         pl.BlockSpec((tm, D), row), pl.BlockSpec((D, 2 * D), fix)],
        [pl.BlockSpec((tm, D), row), pl.BlockSpec((1, D), fix), pl.BlockSpec((tm, 2 * D), row),
         pl.BlockSpec((1, 2 * D), fix)],
        [_sds((T, D), F32), _sds((1, D), F32), _sds((T, 2 * D), BF16), _sds((1, 2 * D), F32)],
    )(du, ag, x, g, dy, w1)


def _row_sums(a):
    return _dot(a.astype(BF16), jnp.ones((a.shape[1], 128), BF16))


def _head_rstd(x):
    return lax.rsqrt(_row_sums(x * x) * (1.0 / x.shape[1]) + NORM_EPS)


def _attn_qkv(x, g, wqkv, qn, kn, name):
    T, D = x.shape
    N = wqkv.shape[1]
    tn = N // 9
    E = HEAD_DIM
    tm = _pick(T, 256, 8)
    ng = qn.shape[0]

    def body(x_ref, g_ref, w_ref, qn_ref, kn_ref, o_ref, a_ref, h_ref):
        xv = x_ref[...]
        h = (xv * _rstd(xv) * g_ref[...]).astype(BF16)
        h_ref[...] = h
        for j in range(9):
            cs = slice(j * tn, (j + 1) * tn)
            res = _dot(h, w_ref[:, cs])
            o_ref[:, cs] = res.astype(BF16)
            if j % 3 == 2:
                a_ref[:, cs] = res.astype(BF16)
                continue
            grp = j // 3
            fac = qn_ref[grp:grp + 1, :] * kn_ref[grp:grp + 1, :] * (E ** -0.5) if j % 3 == 0 else None
            for h_i in range(tn // E):
                hs = slice(h_i * E, (h_i + 1) * E)
                xh = res[:, hs]
                hat = xh * _head_rstd(xh)
                a_ref[:, j * tn + h_i * E:j * tn + (h_i + 1) * E] = (hat if fac is None else hat * fac).astype(BF16)

    row = lambda i: (i, 0)
    fix = lambda i: (0, 0)
    return _call(
        body, name, (T // tm,),
        [pl.BlockSpec((tm, D), row), pl.BlockSpec((1, D), fix), pl.BlockSpec((D, N), fix, pipeline_mode=pl.Buffered(1)),
         pl.BlockSpec((ng, E), fix), pl.BlockSpec((ng, E), fix)],
        [pl.BlockSpec((tm, N), row), pl.BlockSpec((tm, N), row), pl.BlockSpec((tm, D), row)],
        [_sds((T, N), BF16), _sds((T, N), BF16), _sds((T, D), BF16)],
    )(x, g, wqkv, qn, kn)


def _band_mask(q, steps, nblk):
    i = lax.broadcasted_iota(jnp.int32, (q, 2 * q), 0)
    j = lax.broadcasted_iota(jnp.int32, (q, 2 * q), 1)
    diff = q + i - j
    first_key = jnp.where(nblk > 0, 0, q)
    return (diff >= 0) & (diff <= steps) & (j >= first_key)


def _per_row(blk, width):
    e = blk.shape[1]
    if width % e == 0:
        return jnp.concatenate([blk] * (width // e), axis=1)
    return jnp.broadcast_to(blk[:, :1], (blk.shape[0], width))


def _streams(a, dil, to_streams, name, col0=0, ncols=None):
    T, C = a.shape
    ncols = C if ncols is None else ncols
    Q = ATTN_BLOCK
    run = Q * dil
    reps = max(1, min(2048 // run, T // run))
    while T % (run * reps):
        reps -= 1
    rows = run * reps
    cw = _pick(ncols, 512, 128)
    ns = cw // 128

    def body(a_ref, o_ref, scr):
        for s in range(ns):
            ls = slice(s * 128, (s + 1) * 128)
            slab = scr.at[s]
            if to_streams:
                slab[...] = a_ref[:, ls].astype(F32)
                for u in range(reps):
                    for r in range(dil):
                        o_ref[pl.ds(u * run + r * Q, Q), ls] = slab[pl.ds(u * run + r, Q, stride=dil), :].astype(a.dtype)
            else:
                for u in range(reps):
                    for r in range(dil):
                        slab[pl.ds(u * run + r, Q, stride=dil), :] = a_ref[pl.ds(u * run + r * Q, Q), ls].astype(F32)
                o_ref[:, ls] = slab[...].astype(a.dtype)

    return _call(
        body, name, (T // rows, ncols // cw),
        [pl.BlockSpec((rows, cw), lambda i, j: (i, col0 // cw + j))],
        [pl.BlockSpec((rows, cw), lambda i, j: (i, j))],
        [_sds((T, ncols), a.dtype)],
        scratch=[pltpu.VMEM((ns, rows, 128), F32)],
    )(a)[0]


def _attn_fwd(qkv, base, HE, window, dil, name):
    T = qkv.shape[0]
    H = HE // HEAD_DIM
    E = HEAD_DIM
    Q = ATTN_BLOCK
    nb = T // dil // Q
    steps = window // dil

    def body(q_ref, kc_ref, kp_ref, vc_ref, vp_ref, o_ref, l_ref):
        n = pl.program_id(1)
        valid = _band_mask(Q, steps, n)
        ones = jnp.ones((2 * Q, E), BF16)
        outs, lses = [], []
        for h in range(H):
            hs = slice(h * E, (h + 1) * E)
            k2 = jnp.concatenate([kp_ref[:, hs], kc_ref[:, hs]], axis=0)
            v2 = jnp.concatenate([vp_ref[:, hs], vc_ref[:, hs]], axis=0)
            s = jnp.where(valid, _dot_nt(q_ref[:, hs], k2), -1e30)
            m = jnp.max(s, axis=-1, keepdims=True)
            p = jnp.exp(s - m).astype(BF16)
            acc = _dot(p, jnp.concatenate([v2, ones], axis=1))
            l = acc[:, E:]
            outs.append((acc[:, :E] * (1.0 / l)).astype(o_ref.dtype))
            lses.append(m + jnp.log(l))
        o_ref[...] = jnp.concatenate(outs, axis=1)
        l_ref[...] = jnp.concatenate(lses, axis=1)

    blk = lambda s, back: pl.BlockSpec((Q, HE), lambda r, n: (jnp.maximum(n - back, 0) * dil + r, base + s))
    out = pl.BlockSpec((Q, HE), lambda r, n: (n * dil + r, 0))
    return _call(
        body, name, (dil, nb),
        [blk(0, 0), blk(1, 0), blk(1, 1), blk(2, 0), blk(2, 1)],
        [out, out],
        [_sds((T, HE), BF16), _sds((T, HE), F32)],
    )(qkv, qkv, qkv, qkv, qkv)


def _attn_merge(os, lses, x, wo, name):
    T, D = x.shape
    HE = wo.shape[0]
    tm = _pick(T, 512, 8)
    ng = len(os)

    def body(*refs):
        o_refs = refs[:ng]
        l_refs = refs[ng:2 * ng]
        x_ref, w_ref, xo_ref, om_ref, lt_ref = refs[2 * ng:]
        ls = [r[...] for r in l_refs]
        m = functools.reduce(jnp.maximum, ls)
        es = [jnp.exp(l - m) for l in ls]
        tot = functools.reduce(lambda a, b: a + b, es)
        inv = 1.0 / tot
        om = functools.reduce(lambda a, b: a + b, [e * inv * r[...] for e, r in zip(es, o_refs)])
        omb = om.astype(BF16)
        om_ref[...] = omb
        lt_ref[...] = m + jnp.log(tot)
        xo_ref[...] = x_ref[...] + _dot(omb, w_ref[...])

    row = lambda i: (i, 0)
    fix = lambda i: (0, 0)
    return _call(
        body, name, (T // tm,),
        [pl.BlockSpec((tm, HE), row)] * (2 * ng) + [pl.BlockSpec((tm, D), row), pl.BlockSpec((HE, D), fix)],
        [pl.BlockSpec((tm, D), row), pl.BlockSpec((tm, HE), row), pl.BlockSpec((tm, HE), row)],
        [_sds((T, D), F32), _sds((T, HE), BF16), _sds((T, HE), F32)],
    )(*os, *lses, x, wo)


def _attn_out_bwd(dy, om, wo, name, after=()):
    T, D = dy.shape
    HE = wo.shape[0]
    E = HEAD_DIM
    tm = _pick(T, 512, 8)

    def body(dy_ref, om_ref, w_ref, dom_ref, dl_ref):
        dom = _dot_nt(dy_ref[...].astype(BF16), w_ref[...])
        dom_ref[...] = dom.astype(BF16)
        prod = dom * om_ref[...].astype(F32)
        for h in range(HE // E):
            hs = slice(h * E, (h + 1) * E)
            dl_ref[:, hs] = jnp.broadcast_to(jnp.sum(prod[:, hs], axis=-1, keepdims=True), (tm, E))

    row = lambda i: (i, 0)
    return _call(
        body, name, (T // tm,),
        [pl.BlockSpec((tm, D), row), pl.BlockSpec((tm, HE), row), pl.BlockSpec((HE, D), lambda i: (0, 0))],
        [pl.BlockSpec((tm, HE), row), pl.BlockSpec((tm, HE), row)],
        [_sds((T, HE), BF16), _sds((T, HE), F32)], after=after,
    )(dy, om, wo)


def _attn_bwd(qkv, base, HE, dom, lse, delta, window, dil, name):
    T = qkv.shape[0]
    H = HE // HEAD_DIM
    E = HEAD_DIM
    Q = ATTN_BLOCK
    nb = T // dil // Q
    steps = window // dil

    def body(q_ref, kc_ref, kp_ref, vc_ref, vp_ref, do_ref, l_ref, dl_ref, dq_ref, dk_ref, dv_ref, ck_sc, cv_sc):
        n = pl.program_id(1)

        @pl.when(n == 0)
        def _():
            ck_sc[...] = jnp.zeros_like(ck_sc)
            cv_sc[...] = jnp.zeros_like(cv_sc)

        @pl.when(n < nb)
        def _():
            valid = _band_mask(Q, steps, n)
            ck_old = ck_sc[...]
            cv_old = cv_sc[...]
            dqs, dks, dvs = [], [], []
            for h in range(H):
                hs = slice(h * E, (h + 1) * E)
                q = q_ref[:, hs]
                do = do_ref[:, hs]
                k2 = jnp.concatenate([kp_ref[:, hs], kc_ref[:, hs]], axis=0)
                v2 = jnp.concatenate([vp_ref[:, hs], vc_ref[:, hs]], axis=0)
                p = jnp.where(valid, jnp.exp(_dot_nt(q, k2) - _per_row(l_ref[:, hs], 2 * Q)), 0.0)
                ds = (p * (_dot_nt(do, v2) - _per_row(dl_ref[:, hs], 2 * Q))).astype(BF16)
                dqs.append(_dot(ds, k2).astype(BF16))
                dks.append(_dot_tn(q, ds).T)
                dvs.append(_dot_tn(do, p.astype(BF16)).T)
            cat = lambda parts: jnp.concatenate(parts, axis=1)
            dq_ref[...] = cat(dqs)
            dk_ref[...] = (ck_old + cat([d[:Q] for d in dks])).astype(BF16)
            dv_ref[...] = (cv_old + cat([d[:Q] for d in dvs])).astype(BF16)
            ck_sc[...] = cat([d[Q:] for d in dks])
            cv_sc[...] = cat([d[Q:] for d in dvs])

        @pl.when(n == nb)
        def _():
            dk_ref[...] = ck_sc[...].astype(BF16)
            dv_ref[...] = cv_sc[...].astype(BF16)

    nq = lambda n: jnp.minimum(n, nb - 1)
    blk = lambda s, back: pl.BlockSpec((Q, HE), lambda r, n: (jnp.maximum(nq(n) - back, 0) * dil + r, base + s))
    qblk = pl.BlockSpec((Q, HE), lambda r, n: (nq(n) * dil + r, 0))
    kblk = pl.BlockSpec((Q, HE), lambda r, n: (jnp.maximum(n - 1, 0) * dil + r, 0))
    return _call(
        body, name, (dil, nb + 1),
        [blk(0, 0), blk(1, 0), blk(1, 1), blk(2, 0), blk(2, 1), qblk, qblk, qblk],
        [qblk, kblk, kblk],
        [_sds((T, HE), BF16)] * 3,
        scratch=[pltpu.VMEM((Q, HE), F32), pltpu.VMEM((Q, HE), F32)],
    )(qkv, qkv, qkv, qkv, qkv, dom, lse, delta)


def _qk_norm_bwd(dq, dk, qkv, base, HE, gq, gk, name):
    T = dq.shape[0]
    E = HEAD_DIM
    tm = _pick(T, 512, 8)
    scale = E ** -0.5

    def body(dq_ref, dk_ref, q_ref, k_ref, gq_ref, gk_ref, oq_ref, ok_ref, dgq_ref, dgk_ref):
        @pl.when(pl.program_id(0) == 0)
        def _():
            dgq_ref[...] = jnp.zeros_like(dgq_ref)
            dgk_ref[...] = jnp.zeros_like(dgk_ref)

        gqv = gq_ref[...]
        gkv = gk_ref[...]
        c = gqv * gkv * scale
        dc = jnp.zeros((1, E), F32)
        for h in range(HE // E):
            hs = slice(h * E, (h + 1) * E)
            q = q_ref[:, hs].astype(F32)
            rq = _head_rstd(q)
            qh = q * rq
            a = dq_ref[:, hs].astype(F32)
            dc = dc + jnp.sum(a * qh, axis=0, keepdims=True)
            dqh = a * c
            oq_ref[:, hs] = (rq * (dqh - qh * (_row_sums(dqh * qh) * (1.0 / E)))).astype(BF16)
            k = k_ref[:, hs].astype(F32)
            rk = _head_rstd(k)
            kh = k * rk
            b = dk_ref[:, hs].astype(F32)
            ok_ref[:, hs] = (rk * (b - kh * (_row_sums(b * kh) * (1.0 / E)))).astype(BF16)
        dgq_ref[...] += dc * (gkv * scale)
        dgk_ref[...] += dc * (gqv * scale)

    row = lambda i: (i, 0)
    vec = pl.BlockSpec((1, E), lambda i: (0, 0))
    return _call(
        body, name, (T // tm,),
        [pl.BlockSpec((tm, HE), row), pl.BlockSpec((tm, HE), row), pl.BlockSpec((tm, HE), lambda i: (i, base)),
         pl.BlockSpec((tm, HE), lambda i: (i, base + 1)), vec, vec],
        [pl.BlockSpec((tm, HE), row), pl.BlockSpec((tm, HE), row), vec, vec],
        [_sds((T, HE), BF16), _sds((T, HE), BF16), _sds((1, E), F32), _sds((1, E), F32)],
    )(dq, dk, qkv, qkv, gq, gk)


def _attn_qkv_bwd(dqkv, x, g, dy, wqkv, name):
    T, D = x.shape
    HE = wqkv.shape[1] // 9
    tm = _pick(T, 256, 8)

    def body(*refs):
        d_refs = refs[:9]
        x_ref, g_ref, dy_ref, w_ref, dx_ref, dg_ref = refs[9:]

        @pl.when(pl.program_id(0) == 0)
        def _():
            dg_ref[...] = jnp.zeros_like(dg_ref)

        dh = _dot_nt(d_refs[0][...], w_ref[:, :HE])
        for s in range(1, 9):
            dh = dh + _dot_nt(d_refs[s][...], w_ref[:, s * HE:(s + 1) * HE])
        xv = x_ref[...]
        r = _rstd(xv)
        xh = xv * r
        dx_ref[...] = dy_ref[...] + _norm_bwd(dh, xh, r, g_ref[...])
        dg_ref[...] += jnp.sum(dh * xh, axis=0, keepdims=True)

    row = lambda i: (i, 0)
    fix = lambda i: (0, 0)
    return _call(
        body, name, (T // tm,),
        [pl.BlockSpec((tm, HE), row)] * 9 + [pl.BlockSpec((tm, D), row), pl.BlockSpec((1, D), fix), pl.BlockSpec((tm, D), row),
                                           pl.BlockSpec((D, 9 * HE), fix, pipeline_mode=pl.Buffered(1))],
        [pl.BlockSpec((tm, D), row), pl.BlockSpec((1, D), fix)],
        [_sds((T, D), F32), _sds((1, D), F32)],
    )(*dqkv, x, g, dy, wqkv)


def _loss_head(y, target, name):
    T, D = y.shape
    tm = _pick(T, 512, 8)

    def body(y_ref, t_ref, dy_ref, sq_ref):
        @pl.when(pl.program_id(0) == 0)
        def _():
            sq_ref[...] = jnp.zeros_like(sq_ref)

        err = y_ref[...] - t_ref[...]
        dy_ref[...] = err * (1.0 / D)
        sq_ref[...] += jnp.sum(err * err, axis=0, keepdims=True)

    row = lambda i: (i, 0)
    return _call(
        body, name, (T // tm,),
        [pl.BlockSpec((tm, D), row), pl.BlockSpec((tm, D), row)],
        [pl.BlockSpec((tm, D), row), pl.BlockSpec((1, D), lambda i: (0, 0))],
        [_sds((T, D), F32), _sds((1, D), F32)],
    )(y, target)


def _local_step(x, target, norm_g, b_pw1, w_dw, b_dw, cng, b_pw2, qn, kn, block_weights, grads_ready, grads_tick):
    T, D = x.shape
    ng = lambda l, k: norm_g[l, k][None, :]
    bn = _pick(D, 256, 128)

    w_in, w_out = [None] * 4, [None] * 4
    w_in[0], w_out[0] = block_weights(0, x)
    x1, *gu0 = _ffn_fwd(x, ng(0, 0), w_in[0], w_out[0], "ffn_fwd_0")
    pw1, pw2 = block_weights(1, x1)
    ag, u, hc = _conv_pre(x1, ng(0, 1), pw1, b_pw1, "conv_pre")
    c = _dwconv(u, w_dw, b_dw, "dwconv")
    x2, s = _conv_post(c, x1, cng, pw2, b_pw2, "conv_post")
    w_in[1], w_out[1] = block_weights(2, x2)
    x3, *gu1 = _ffn_fwd(x2, ng(0, 2), w_in[1], w_out[1], "ffn_fwd_1")
    w_in[2], w_out[2] = block_weights(3, x3)
    x4, *gu2 = _ffn_fwd(x3, ng(1, 0), w_in[2], w_out[2], "ffn_fwd_2")
    wqkv, wo = block_weights(4, x4)
    HE = wo.shape[0]
    F = w_out[0].shape[0]
    bf = _pick(F, 256, 128)
    bh = _pick(HE, 512, 128)
    qkv, att, ha = _attn_qkv(x4, ng(1, 1), wqkv, qn, kn, "attn_qkv")
    qkv_s = [(att, 3 * gi) if dil == 1 else (_streams(att, dil, True, f"qkv_streams_{gi}", 3 * gi * HE, 3 * HE), 0)
             for gi, (_, dil) in enumerate(ATTN_GROUPS)]
    tokens = lambda a, dil, name: a if dil == 1 else _streams(a, dil, False, name)
    streams = lambda a, dil, name: a if dil == 1 else _streams(a, dil, True, name)
    os, lses = [], []
    for gi, (window, dil) in enumerate(ATTN_GROUPS):
        o, l = _attn_fwd(*qkv_s[gi], HE, window, dil, f"attn_fwd_{gi}")
        os.append(tokens(o, dil, f"o_tokens_{gi}"))
        lses.append(tokens(l, dil, f"lse_tokens_{gi}"))
    x5, om, lse = _attn_merge(os, lses, x4, wo, "attn_merge")
    w_in[3], w_out[3] = block_weights(5, x5)
    x6, *gu3 = _ffn_fwd(x5, ng(1, 2), w_in[3], w_out[3], "ffn_fwd_3")
    dy, sq = _loss_head(x6, target, "loss_head")

    grads = {"ffn_w_in": [None] * 4, "ffn_w_out": [None] * 4}
    dnorm = [[None] * 3 for _ in range(2)]

    def ffn_back(lf, k, xin, gvec, dy, gu, after):
        dx, dg, dgate, dup, a, h, dyb = _ffn_bwd(xin, gvec, dy, gu[0], gu[1], w_in[lf], w_out[lf], f"ffn_bwd_{lf}", after=after)
        grads["ffn_w_in"][lf] = _mm_tn_parts(h, [dgate, dup], bf, 2 * F, 0, f"ffn_dw_in_{lf}",
                                             after=grads_tick(dx) if lf == 0 else ())
        grads["ffn_w_out"][lf] = _mm_tn(a, dyb, bf, D, (F, D), (bf, D), lambda mi, ni: (mi, 0), 1.0, f"ffn_dw_out_{lf}")
        return dx, dg, grads_ready(k, (grads["ffn_w_in"][lf], grads["ffn_w_out"][lf]), dx)

    dx, dnorm[1][2], tok = ffn_back(3, 5, x5, ng(1, 2), dy, gu3, ())
    dom, delta = _attn_out_bwd(dx, om, wo, "attn_out_bwd", after=tok)
    grads["attn_w_o"] = _mm_tn(om, dx, HE, bn, (HE, D), (HE, bn), lambda mi, ni: (0, ni), 1.0, "attn_dw_o")
    dqkv, dgq, dgk = [], [], []
    for gi, (window, dil) in enumerate(ATTN_GROUPS):
        ds = _attn_bwd(*qkv_s[gi], HE, streams(dom, dil, f"dom_streams_{gi}"), streams(lse, dil, f"lse_streams_{gi}"),
                       streams(delta, dil, f"delta_streams_{gi}"), window, dil, f"attn_bwd_{gi}")
        dq, dk, dv = [tokens(d, dil, f"d{nm}_tokens_{gi}") for d, nm in zip(ds, "qkv")]
        dq, dk, a_, b_ = _qk_norm_bwd(dq, dk, qkv, 3 * gi, HE, qn[gi][None, :], kn[gi][None, :], f"qk_norm_bwd_{gi}")
        dqkv += [dq, dk, dv]
        dgq.append(a_)
        dgk.append(b_)
    grads["attn_q_norm"] = jnp.concatenate(dgq, axis=0)
    grads["attn_k_norm"] = jnp.concatenate(dgk, axis=0)
    d_qkv = None
    for gi in range(len(ATTN_GROUPS)):
        d_qkv = _mm_tn_parts(ha, dqkv[3 * gi:3 * gi + 3], bh, 9 * HE, 3 * gi * HE, f"attn_dw_qkv_{gi}", prev=d_qkv)
    grads["attn_w_qkv"] = d_qkv
    dx, dnorm[1][1] = _attn_qkv_bwd(dqkv, x4, ng(1, 1), dx, wqkv, "attn_qkv_bwd")
    tok = grads_ready(4, (grads["attn_w_qkv"], grads["attn_w_o"]), dx)
    dx, dnorm[1][0], tok = ffn_back(2, 3, x3, ng(1, 0), dx, gu2, tok)

    dx, dnorm[0][2], tok = ffn_back(1, 2, x2, ng(0, 2), dx, gu1, tok)
    dc, grads["conv_norm_g"], grads["conv_b_pw2"] = _conv_post_bwd(dx, c, cng, pw2, "conv_post_bwd", after=tok)
    grads["conv_w_pw2"] = _mm_tn(s, dx, D, bn, (D, D), (D, bn), lambda mi, ni: (0, ni), 1.0, "conv_dw_pw2")
    du, grads["conv_w_dw"], grads["conv_b_dw"] = _dwconv_bwd(dc, u, w_dw, "dwconv_bwd")
    dx, dnorm[0][1], dag, grads["conv_b_pw1"] = _conv_pre_bwd(du, ag, x1, ng(0, 1), dx, pw1, "conv_pre_bwd")
    grads["conv_w_pw1"] = _mm_tn(hc, dag, D, 2 * bn, (D, 2 * D), (D, 2 * bn), lambda mi, ni: (0, ni), 1.0, "conv_dw_pw1")
    tok = grads_ready(1, (grads["conv_w_pw1"], grads["conv_w_pw2"]), dx)
    dx, dnorm[0][0], _ = ffn_back(0, 0, x, ng(0, 0), dx, gu0, tok)

    grads["norm_g"] = jnp.concatenate([jnp.concatenate(r, axis=0)[None] for r in dnorm], axis=0)
    return sq, dx, grads


class _Sharded:
    def __init__(self, name, full3, half_axis, shard_axis, src, slab=None):
        self.name, self.full3, self.half_axis, self.shard_axis = name, tuple(full3), half_axis, shard_axis
        self.src, self.slab = src, slab

    def source(self, refs):
        return refs[self.src] if self.slab is None else refs[self.src].at[self.slab]

    def _cut(self, shape, axis, parts):
        s = list(shape)
        s[axis] //= parts
        return tuple(s)

    @property
    def shard3(self):
        return self._cut(self.full3, self.shard_axis, N_CHIPS)

    @property
    def pair3(self):
        return self._cut(self.full3, self.half_axis, 2)

    @property
    def part3(self):
        return self._cut(self.shard3, self.half_axis, 2)

    @staticmethod
    def _slice(ref, axis, idx, parts):
        n = ref.shape[axis] // parts
        start = idx * n
        minor = len(ref.shape) - 1 - axis
        if minor < 2 and not isinstance(start, int):
            start = pl.multiple_of(start, 128 if minor == 0 else (16 if n % 16 == 0 else 8))
        sl = [slice(None)] * len(ref.shape)
        sl[axis] = pl.ds(start, n)
        return ref.at[tuple(sl)]

    def half(self, ref, h):
        return self._slice(ref, self.half_axis, h, 2)

    def shard(self, ref, j):
        return self._slice(ref, self.shard_axis, j, N_CHIPS)


def _place():
    x, y, c = lax.axis_index("x"), lax.axis_index("y"), lax.axis_index("c")
    return x, y, c, 2 * x + y


_RELS = (1, 2, 3)


def _peer(x, y, rel):
    px = 1 - x if rel & 2 else x
    py = 1 - y if rel & 1 else y
    return px, py, 2 * px + py


ANY = pl.BlockSpec(memory_space=pl.ANY)


def _comm_call(body, name, n_in, out_shape, n_sems, aliases=None):
    return pl.pallas_call(
        body, name=name, in_specs=[ANY] * n_in, out_specs=[ANY] * len(out_shape), out_shape=out_shape,
        scratch_shapes=[pltpu.SemaphoreType.DMA((n,)) for n in n_sems],
        input_output_aliases=aliases or {},
        compiler_params=pltpu.CompilerParams(has_side_effects=True),
    )


def _remote(src, dst, send_sem, recv_sem, dev):
    return pltpu.make_async_remote_copy(src_ref=src, dst_ref=dst, send_sem=send_sem, recv_sem=recv_sem, device_id=dev,
                                        device_id_type=MESH)


def _gather_small(small_shards):
    ns = len(small_shards)

    def body(*refs):
        ins, outs = refs[:ns], refs[ns:2 * ns]
        lsem, ssem, rsem = refs[2 * ns:]
        x, y, c, me = _place()
        cols = lambda ref, j: _Sharded._slice(ref, 1, j, N_CHIPS)
        local = [pltpu.make_async_copy(ins[i], cols(outs[i], me), lsem.at[i]) for i in range(ns)]
        sends = []
        for i in range(ns):
            for k, rel in enumerate(_RELS):
                px, py, _ = _peer(x, y, rel)
                sends.append(_remote(ins[i], cols(outs[i], me), ssem.at[3 * i + k], rsem.at[3 * i + k], (px, py, c)))
        for cp in local + sends:
            cp.start()
        for i in range(ns):
            for k, rel in enumerate(_RELS):
                _, _, pj = _peer(x, y, rel)
                got = cols(outs[i], pj)
                _remote(got, got, ssem.at[3 * i + k], rsem.at[3 * i + k], (x, y, c)).wait_recv()
        for cp in sends:
            cp.wait_send()
        for cp in local:
            cp.wait()

    out_shape = [_sds((s.shape[0], s.shape[1] * N_CHIPS), F32) for s in small_shards]
    return _comm_call(body, "gather_small", ns, out_shape, [ns, 3 * ns, 3 * ns])(*small_shards)


HBM = pl.BlockSpec(memory_space=pltpu.HBM)
SEM = pl.BlockSpec(memory_space=pltpu.SEMAPHORE)
DATAFLOW = pltpu.SideEffectType.DATAFLOW_SIDE_EFFECTING


def _in_hbm(a):
    return pltpu.with_memory_space_constraint(a, pltpu.HBM)


def _cast_place(it, shard, scal, after=()):
    a_n, r_n, c_n = it.shard3
    tr = _pick(r_n, 256, 16)
    sa = it.shard_axis

    def body(sc_ref, s_ref, o_ref):
        o_ref[...] = s_ref[...].astype(BF16)

    if it.slab is None:
        src = pl.BlockSpec((1, tr, c_n), lambda a, rb, sc: (a, rb, 0))
    else:
        src = pl.BlockSpec((None, 1, tr, c_n), lambda a, rb, sc: (it.slab, a, rb, 0))
    dst = pl.BlockSpec((1, tr, c_n), lambda a, rb, sc: (a + sc[1] * (a_n if sa == 0 else 0), rb + sc[1] * (r_n // tr if sa == 1 else 0),
                                                       sc[1] if sa == 2 else 0))
    return _call(body, f"cast_place_{it.name}", (a_n, r_n // tr), [src], [dst], [_sds(it.full3, BF16)], prefetch=1,
                 after=after)(scal, shard)[0]


def _gather_start(items, fulls, name):
    ni = len(items)

    def body(*refs):
        outs = refs[ni:]
        ssem, rsem, full = outs[:ni], outs[ni:2 * ni], outs[2 * ni:3 * ni]
        x, y, c, me = _place()
        for i, it in enumerate(items):
            mine = it.half(it.shard(full[i], me), c)
            for k, rel in enumerate(_RELS):
                px, py, _ = _peer(x, y, rel)
                _remote(mine, mine, ssem[i].at[k], rsem[i].at[k], (px, py, c)).start()

    outs = pl.pallas_call(
        body, name=name, in_specs=[HBM] * ni, out_specs=[SEM] * (2 * ni) + [HBM] * ni,
        out_shape=[pltpu.SemaphoreType.DMA((3,))] * (2 * ni) + [pltpu.HBM(it.full3, BF16) for it in items],
        input_output_aliases={j: 2 * ni + j for j in range(ni)},
        compiler_params=pltpu.CompilerParams(has_side_effects=DATAFLOW),
    )(*[_in_hbm(f) for f in fulls])
    return outs[:ni], outs[ni:2 * ni], outs[2 * ni:]


def _gather_forward(items, fulls, ssems, rsems, after, name):
    ni = len(items)

    def body(*refs):
        ssem, rsem = refs[ni:2 * ni], refs[2 * ni:3 * ni]
        outs = refs[3 * ni + 1:]
        full, fsem, gsem = outs[:ni], outs[ni:2 * ni], outs[2 * ni:3 * ni]
        x, y, c, me = _place()
        sib = (x, y, 1 - c)
        for i, it in enumerate(items):
            for k, rel in enumerate(_RELS):
                _, _, pj = _peer(x, y, rel)
                got = it.half(it.shard(full[i], pj), c)
                _remote(got, got, ssem[i].at[k], rsem[i].at[k], sib).wait_recv()
                _remote(got, got, fsem[i].at[k], gsem[i].at[k], sib).start()
        for i, it in enumerate(items):
            mine = it.half(it.shard(full[i], me), c)
            for k in range(3):
                _remote(mine, mine, ssem[i].at[k], rsem[i].at[k], sib).wait_send()

    outs = pl.pallas_call(
        body, name=name, in_specs=[HBM] * ni + [SEM] * (2 * ni) + [ANY],
        out_specs=[HBM] * ni + [SEM] * (2 * ni),
        out_shape=[pltpu.HBM(it.full3, BF16) for it in items] + [pltpu.SemaphoreType.DMA((3,))] * (2 * ni),
        input_output_aliases={i: i for i in range(ni)},
        compiler_params=pltpu.CompilerParams(has_side_effects=DATAFLOW),
    )(*fulls, *ssems, *rsems, after)
    return outs[:ni], outs[ni:2 * ni], outs[2 * ni:]


def _gather_finish(items, fulls, fsems, gsems, name):
    ni = len(items)

    def body(*refs):
        fsem, gsem = refs[ni:2 * ni], refs[2 * ni:3 * ni]
        full = refs[3 * ni:]
        x, y, c, _ = _place()
        sib = (x, y, 1 - c)
        for i, it in enumerate(items):
            for k, rel in enumerate(_RELS):
                _, _, pj = _peer(x, y, rel)
                got = it.half(it.shard(full[i], pj), 1 - c)
                _remote(got, got, fsem[i].at[k], gsem[i].at[k], sib).wait_recv()
                sent = it.half(it.shard(full[i], pj), c)
                _remote(sent, sent, fsem[i].at[k], gsem[i].at[k], sib).wait_send()

    return pl.pallas_call(
        body, name=name, in_specs=[HBM] * ni + [SEM] * (2 * ni), out_specs=[HBM] * ni,
        out_shape=[pltpu.HBM(it.full3, BF16) for it in items],
        input_output_aliases={i: i for i in range(ni)},
        compiler_params=pltpu.CompilerParams(has_side_effects=DATAFLOW),
    )(*fulls, *fsems, *gsems)


def _exchange_start(name, arrays, n_sems, copies):
    na, ns = len(arrays), len(n_sems)

    def body(*refs):
        outs = refs[na:]
        ssem, rsem, thru, token = outs[:ns], outs[ns:2 * ns], outs[2 * ns:2 * ns + na], outs[-1]
        for send, _ in copies(thru, ssem, rsem):
            send.start()
        token[...] = jnp.zeros_like(token)

    outs = pl.pallas_call(
        body, name=name, in_specs=[HBM] * na,
        out_specs=[SEM] * (2 * ns) + [HBM] * na + [pl.BlockSpec(memory_space=pltpu.VMEM)],
        out_shape=[pltpu.SemaphoreType.DMA((n,)) for n in n_sems] * 2 + [pltpu.HBM(a.shape, a.dtype) for a in arrays]
        + [_sds((8, 128), F32)],
        input_output_aliases={j: 2 * ns + j for j in range(na)},
        compiler_params=pltpu.CompilerParams(has_side_effects=DATAFLOW),
    )(*[_in_hbm(a) for a in arrays])
    return outs[:ns], outs[ns:2 * ns], outs[2 * ns:2 * ns + na], outs[-1]


def _exchange_wait(name, arrays, ssems, rsems, copies, after):
    na, ns, nw = len(arrays), len(ssems), len(after)

    def body(*refs):
        ssem, rsem = refs[na:na + ns], refs[na + ns:na + 2 * ns]
        thru = refs[na + 2 * ns + nw:]
        for send, recv in copies(thru, ssem, rsem):
            recv.wait_recv()
            send.wait_send()

    return pl.pallas_call(
        body, name=name, in_specs=[HBM] * na + [SEM] * (2 * ns) + [ANY] * nw, out_specs=[HBM] * na,
        out_shape=[pltpu.HBM(a.shape, a.dtype) for a in arrays],
        input_output_aliases={i: i for i in range(na)},
        compiler_params=pltpu.CompilerParams(has_side_effects=DATAFLOW),
    )(*arrays, *ssems, *rsems, *after)


def _pair_copies(items):
    n = len(items)

    def copies(a, ssem, rsem):
        x, y, c, _ = _place()
        cps = [_remote(it.half(a[i], 1 - c), a[n + i], ssem[i].at[0], rsem[i].at[0], (x, y, 1 - c)) for i, it in enumerate(items)]
        return [(cp, cp) for cp in cps]

    return copies


def _chip_copies(items):
    n = len(items)

    def copies(a, ssem, rsem):
        x, y, c, _ = _place()
        cps = []
        for i, it in enumerate(items):
            for k, rel in enumerate(_RELS):
                px, py, pj = _peer(x, y, rel)
                cps.append(_remote(it.shard(a[i], pj), a[n + i].at[k], ssem[i].at[k], rsem[i].at[k], (px, py, c)))
        return [(cp, cp) for cp in cps]

    return copies


def _fill_copies(items):
    def copies(a, ssem, rsem):
        x, y, c, _ = _place()
        sib = (x, y, 1 - c)
        out = []
        for i, it in enumerate(items):
            mine, other = it.half(a[i], c), it.half(a[i], 1 - c)
            out.append((_remote(mine, mine, ssem[i].at[0], rsem[i].at[0], sib), _remote(other, other, ssem[i].at[0], rsem[i].at[0], sib)))
        return out

    return copies


def _ew_tiles(d):
    _, rows, cols = d.part3
    return _pick(rows, 256, 16), cols


def _pair_add(d, g_full, got, scal):
    tr, tc = _ew_tiles(d)
    a_n, r_n, c_n = d.pair3
    ha = d.half_axis

    def body(sc_ref, g_ref, r_ref, o_ref, ob_ref):
        s = g_ref[...].astype(F32) + r_ref[...].astype(F32)
        o_ref[...] = s
        ob_ref[...] = s.astype(BF16)

    blk = (1, tr, tc)
    same = pl.BlockSpec(blk, lambda a, rb, cb, sc: (a, rb, cb))
    mine = pl.BlockSpec(blk, lambda a, rb, cb, sc: (a + sc[0] * (a_n if ha == 0 else 0), rb + sc[0] * (r_n // tr if ha == 1 else 0), cb))
    return _call(body, f"pair_add_{d.name}", (a_n, r_n // tr, c_n // tc), [mine, same], [same, same],
                 [_sds(d.pair3, F32), _sds(d.pair3, BF16)], prefetch=1)(scal, g_full, got)


def _chip_reduce(d, pair_f32, got, scal):
    tr, tc = _ew_tiles(d)
    a_n, r_n, c_n = d.part3
    ha, sa = d.half_axis, d.shard_axis

    def body(sc_ref, p_ref, r0, r1, r2, o_ref):
        o_ref[...] = ((p_ref[...] + r0[...].astype(F32)) + r1[...].astype(F32)) + r2[...].astype(F32)

    blk = (1, tr, tc)
    own = pl.BlockSpec(blk, lambda a, rb, sc: (a + sc[1] * (a_n if sa == 0 else 0), rb + sc[1] * (r_n // tr if sa == 1 else 0),
                                               sc[1] if sa == 2 else 0))
    slot = lambda k: pl.BlockSpec((None,) + blk, lambda a, rb, sc: (k, a, rb, 0))
    out = pl.BlockSpec(blk, lambda a, rb, sc: (a + sc[0] * (a_n if ha == 0 else 0), rb + sc[0] * (r_n // tr if ha == 1 else 0), 0))
    return _call(body, f"chip_reduce_{d.name}", (a_n, r_n // tr), [own, slot(0), slot(1), slot(2)], [out],
                 [_sds(d.shard3, F32)], prefetch=1)(scal, pair_f32, got, got, got)[0]


def _adam_math(g, w, m, v):
    m = ADAM_B1 * m + (1.0 - ADAM_B1) * g
    v = ADAM_B2 * v + (1.0 - ADAM_B2) * (g * g)
    m_hat = m / (1.0 - ADAM_B1 ** ADAM_STEP)
    v_hat = v / (1.0 - ADAM_B2 ** ADAM_STEP)
    delta = -ADAM_LR * (m_hat / (jnp.sqrt(v_hat) + ADAM_EPS) + ADAM_WD * w)
    return delta, m, v


def _adam(d, g, w, m, v, prev=None):
    tr, tc = _ew_tiles(d)
    a_n, r_n, c_n = d.shard3
    n_prev = 0 if prev is None else 4

    def body(g_ref, w_ref, m_ref, v_ref, *rest):
        go_ref, d_ref, mo_ref, vo_ref = rest[n_prev:]
        gv = g_ref[...]
        go_ref[...] = gv
        d_ref[...], mo_ref[...], vo_ref[...] = _adam_math(gv, w_ref[...], m_ref[...], v_ref[...])

    plain = pl.BlockSpec((1, tr, tc), lambda a, rb: (a, rb, 0))
    if d.slab is None:
        wspec, shape = plain, d.shard3
    else:
        wspec, shape = pl.BlockSpec((None, 1, tr, tc), lambda a, rb: (d.slab, a, rb, 0)), (4,) + d.shard3
    in_specs = [plain] + [wspec] * 3
    args = [g, w, m, v]
    aliases = None
    if prev is not None:
        in_specs += [pl.BlockSpec(memory_space=pl.ANY)] * 4
        args += list(prev)
        aliases = {4 + k: k for k in range(4)}
    return _call(body, f"adam_{d.name}", (a_n, r_n // tr), in_specs, [wspec] * 4, [_sds(shape, F32)] * 4, aliases=aliases)(*args)


def _adam_small(gs, ws, ms, vs):
    n = len(gs)

    def body(*refs):
        for i in range(n):
            g, w, m, v = (refs[k * n + i][...] for k in range(4))
            d, mo, vo = _adam_math(g, w, m, v)
            refs[4 * n + i][...] = d
            refs[5 * n + i][...] = mo
            refs[6 * n + i][...] = vo

    vm = pl.BlockSpec(memory_space=pltpu.VMEM)
    outs = pl.pallas_call(body, name="adam_small", in_specs=[vm] * (4 * n), out_specs=[vm] * (3 * n),
                          out_shape=[_sds(g.shape, F32) for g in gs] * 3)(*gs, *ws, *ms, *vs)
    return outs[:n], outs[n:2 * n], outs[2 * n:]


def _allreduce_small(packed):
    rows, cols = packed.shape
    others = [(dx, dy, dc) for dx in (0, 1) for dy in (0, 1) for dc in (0, 1) if (dx, dy, dc) != (0, 0, 0)]

    def body(in_ref, out_ref, buf, ssem, rsem):
        x, y, c, _ = _place()
        lin = 4 * x + 2 * y + c
        buf[lin] = in_ref[...]
        cps = []
        for k, (dx, dy, dc) in enumerate(others):
            px = 1 - x if dx else x
            py = 1 - y if dy else y
            pc = 1 - c if dc else c
            cps.append((pltpu.make_async_remote_copy(src_ref=in_ref, dst_ref=buf.at[lin], send_sem=ssem.at[k], recv_sem=rsem.at[k],
                                                     device_id=(px, py, pc), device_id_type=MESH), 4 * px + 2 * py + pc))
        for cp, _ in cps:
            cp.start()
        for k, (cp, plin) in enumerate(cps):
            pltpu.make_async_remote_copy(src_ref=in_ref, dst_ref=buf.at[plin], send_sem=ssem.at[k], recv_sem=rsem.at[k],
                                         device_id=(x, y, c), device_id_type=MESH).wait_recv()
        for cp, _ in cps:
            cp.wait_send()
        acc = buf[0]
        for dev in range(1, 8):
            acc = acc + buf[dev]
        out_ref[...] = acc

    vm = pl.BlockSpec(memory_space=pltpu.VMEM)
    return pl.pallas_call(
        body, name="allreduce_small", in_specs=[vm], out_specs=vm, out_shape=_sds((rows, cols), F32),
        scratch_shapes=[pltpu.VMEM((8, rows, cols), F32), pltpu.SemaphoreType.DMA((7,)), pltpu.SemaphoreType.DMA((7,))],
        compiler_params=pltpu.CompilerParams(has_side_effects=True),
    )(packed)


_BIG = ("ffn_w_in", "ffn_w_out", "conv_w_pw1", "conv_w_pw2", "attn_w_qkv", "attn_w_o")
_SMALL = ("norm_g", "conv_b_pw1", "conv_w_dw", "conv_b_dw", "conv_norm_g", "conv_b_pw2", "attn_q_norm", "attn_k_norm")
_NAMES = ("norm_g", "ffn_w_in", "ffn_w_out", "conv_w_pw1", "conv_b_pw1", "conv_w_dw", "conv_b_dw", "conv_norm_g", "conv_w_pw2",
          "conv_b_pw2", "attn_w_qkv", "attn_q_norm", "attn_k_norm", "attn_w_o")


def _rows8(a, width):
    a = a.reshape(-1, min(a.shape[-1], width))
    return jnp.pad(a, ((0, -a.shape[0] % 8), (0, width - a.shape[1])))


def kernel(x, norm_g, ffn_w_in, ffn_w_out, conv_w_pw1, conv_b_pw1, conv_w_dw, conv_b_dw, conv_norm_g, conv_w_pw2, conv_b_pw2, attn_w_qkv, attn_q_norm, attn_k_norm, attn_w_o, loss_target, m_norm_g, m_ffn_w_in, m_ffn_w_out, m_conv_w_pw1, m_conv_b_pw1, m_conv_w_dw, m_conv_b_dw, m_conv_norm_g, m_conv_w_pw2, m_conv_b_pw2, m_attn_w_qkv, m_attn_q_norm, m_attn_k_norm, m_attn_w_o, v_norm_g, v_ffn_w_in, v_ffn_w_out, v_conv_w_pw1, v_conv_b_pw1, v_conv_w_dw, v_conv_b_dw, v_conv_norm_g, v_conv_w_pw2, v_conv_b_pw2, v_attn_w_qkv, v_attn_q_norm, v_attn_k_norm, v_attn_w_o):
    w = dict(zip(_NAMES, (norm_g, ffn_w_in, ffn_w_out, conv_w_pw1, conv_b_pw1, conv_w_dw, conv_b_dw, conv_norm_g, conv_w_pw2,
                          conv_b_pw2, attn_w_qkv, attn_q_norm, attn_k_norm, attn_w_o)))
    m = dict(zip(_NAMES, (m_norm_g, m_ffn_w_in, m_ffn_w_out, m_conv_w_pw1, m_conv_b_pw1, m_conv_w_dw, m_conv_b_dw, m_conv_norm_g,
                          m_conv_w_pw2, m_conv_b_pw2, m_attn_w_qkv, m_attn_q_norm, m_attn_k_norm, m_attn_w_o)))
    v = dict(zip(_NAMES, (v_norm_g, v_ffn_w_in, v_ffn_w_out, v_conv_w_pw1, v_conv_b_pw1, v_conv_w_dw, v_conv_b_dw, v_conv_norm_g,
                          v_conv_w_pw2, v_conv_b_pw2, v_attn_w_qkv, v_attn_q_norm, v_attn_k_norm, v_attn_w_o)))
    T, D = x.shape[1:]
    F = ffn_w_out.shape[2] * N_CHIPS
    HE = attn_w_o.shape[1] * N_CHIPS
    cx, cy, cc = lax.axis_index("x"), lax.axis_index("y"), lax.axis_index("c")
    me = 2 * cx + cy
    scal = jnp.stack([cc, me]).astype(jnp.int32)

    ffn_in = lambda lf: _Sharded(f"ffn_w_in_{lf}", (2, D // 2, 2 * F), 0, 2, "ffn_w_in", lf)
    ffn_out = lambda lf: _Sharded(f"ffn_w_out_{lf}", (4, F // 4, D), 1, 0, "ffn_w_out", lf)
    items = [
        ffn_in(0), ffn_out(0),
        _Sharded("conv_w_pw1", (2, D // 2, 2 * D), 0, 2, "conv_w_pw1"), _Sharded("conv_w_pw2", (4, D // 4, D), 1, 0, "conv_w_pw2"),
        ffn_in(1), ffn_out(1), ffn_in(2), ffn_out(2),
        _Sharded("attn_w_qkv", (2, D // 2, 9 * HE), 0, 2, "attn_w_qkv"), _Sharded("attn_w_o", (4, HE // 4, D), 1, 0, "attn_w_o"),
        ffn_in(3), ffn_out(3),
    ]
    mat_shapes = {"ffn_w_in": (D, 2 * F), "ffn_w_out": (F, D), "conv_w_pw1": (D, 2 * D), "conv_w_pw2": (D, D),
                  "attn_w_qkv": (D, 9 * HE), "attn_w_o": (HE, D)}

    def as_shards(a, n):
        it = next(i for i in items if i.src == n)
        return a.reshape(((4,) if it.slab is not None else ()) + it.shard3)

    norm_full, dw_full = _gather_small([norm_g.reshape(6, D // 4), conv_w_dw.reshape(CONV_WIDTH, D // 4)])
    place = lambda its, after: [_cast_place(it, as_shards(w[it.src], it.src), scal, after) for it in its]
    ssems, rsems, fulls = _gather_start(items[:2], place(items[:2], ()), "gather_start_first")
    more = _gather_start(items[2:], place(items[2:], fulls[:1]), "gather_start_rest")
    ssems, rsems, fulls = [list(a) + list(b) for a, b in zip((ssems, rsems, fulls), more)]

    def block_weights(k, after):
        sel = slice(2 * k, 2 * k + 2)
        got, fsems, gsems = _gather_forward(items[sel], fulls[sel], ssems[sel], rsems[sel], after, f"gather_forward_{k}")
        done = _gather_finish(items[sel], got, fsems, gsems, f"gather_finish_{k}")
        return [a.reshape(mat_shapes[it.src]) for a, it in zip(done, items[sel])]

    res = {}
    flight = []

    def advance(entry, k, dx):
        stage, its, st = entry
        n = len(its)
        if stage == 1:
            ssem, rsem, arrs = st
            done = _exchange_wait(f"grads_pair_wait_{k}", arrs, ssem, rsem, _pair_copies(its), dx)
            sums = [_pair_add(it, g, r, scal) for it, g, r in zip(its, done[:n], done[n:])]
            land = [lax.empty((3,) + it.part3, BF16) for it in its]
            ssem, rsem, arrs, tok = _exchange_start(f"grads_chip_start_{k}", [p[1] for p in sums] + land, [3] * n, _chip_copies(its))
            return (2, its, (ssem, rsem, arrs, [p[0] for p in sums])), tok
        if stage == 2:
            ssem, rsem, arrs, p32 = st
            got = _exchange_wait(f"grads_chip_wait_{k}", arrs, ssem, rsem, _chip_copies(its), dx)[n:]
            red = [_chip_reduce(it, p, r, scal) for it, p, r in zip(its, p32, got)]
            ssem, rsem, arrs, tok = _exchange_start(f"grads_fill_start_{k}", red, [1] * n, _fill_copies(its))
            return (3, its, (ssem, rsem, arrs)), tok
        ssem, rsem, arrs = st
        for it, g in zip(its, _exchange_wait(f"grads_fill_wait_{k}", arrs, ssem, rsem, _fill_copies(its), dx)):
            nm = it.src
            res[nm] = _adam(it, g, as_shards(w[nm], nm), as_shards(m[nm], nm), as_shards(v[nm], nm), prev=res.get(nm))
        return None, None

    def step_flight(dx):
        toks, left = [], []
        for k, entry in flight:
            entry, tok = advance(entry, k, dx)
            if entry is not None:
                left.append((k, entry))
                toks.append(tok)
        flight[:] = left
        return toks

    def grads_ready(k, pairs, dx):
        its = items[2 * k:2 * k + 2]
        toks = step_flight((dx,))
        g16 = [p.reshape(it.full3) for p, it in zip(pairs, its)]
        land = [lax.empty(it.pair3, BF16) for it in its]
        ssem, rsem, arrs, tok = _exchange_start(f"grads_pair_start_{k}", g16 + land, [1] * len(its), _pair_copies(its))
        flight.append((k, (1, its, (ssem, rsem, arrs))))
        return toks + [tok]

    sq, dx, grads = _local_step(x[0], loss_target[0], norm_full.reshape(2, 3, D), conv_b_pw1, dw_full, conv_b_dw, conv_norm_g,
                                conv_b_pw2, attn_q_norm[0], attn_k_norm[0], block_weights, grads_ready,
                                lambda dx: step_flight((dx,)))
    loss = lax.psum(0.5 * jnp.sum(sq) / D, ("x", "y", "c"))
    k_last, entry = flight.pop()
    entry, tok = advance(entry, k_last, (dx,))
    while flight:
        step_flight((dx, tok))

    out_g, out_d, out_m, out_v = {}, {}, {}, {}

    parts = [_rows8(grads[n], D) for n in _SMALL]
    tot = _allreduce_small(jnp.concatenate(parts, axis=0))
    sg, r0 = {}, 0
    for n, p in zip(_SMALL, parts):
        last = grads[n].shape[-1]
        g = tot[r0:r0 + grads[n].size // min(last, D), :min(last, D)].reshape(-1, last)
        r0 += p.shape[0]
        if n in ("norm_g", "conv_w_dw"):
            g = lax.dynamic_slice_in_dim(g, me * (D // 4), D // 4, axis=1)
        sg[n] = g
    flat = lambda a: a.reshape(-1, a.shape[-1])
    ds, ms, vs = _adam_small([sg[n] for n in _SMALL], [flat(w[n]) for n in _SMALL], [flat(m[n]) for n in _SMALL],
                             [flat(v[n]) for n in _SMALL])
    for i, n in enumerate(_SMALL):
        out_g[n], out_d[n], out_m[n], out_v[n] = (a.reshape(w[n].shape) for a in (sg[n], ds[i], ms[i], vs[i]))

    behind = tuple(r[1] for r in res.values()) + tuple(ds)
    while entry is not None:
        entry, _ = advance(entry, k_last, behind)
    for n in _BIG:
        out_g[n], out_d[n], out_m[n], out_v[n] = (a.reshape(w[n].shape) for a in res[n])

    return (loss, dx[None], *[out_g[n] for n in _NAMES], *[out_d[n] for n in _NAMES], *[out_m[n] for n in _NAMES],
            *[out_v[n] for n in _NAMES])
```

```python
import functools

import jax
import jax.numpy as jnp
from jax import lax
from jax.experimental import pallas as pl
from jax.experimental.pallas import tpu as pltpu

F32 = jnp.float32
BF16 = jnp.bfloat16
MESH = pl.DeviceIdType.MESH

NORM_EPS = 1e-6
CONV_WIDTH = 31
ATTN_GROUPS = ((128, 1), (512, 4), (2048, 16))
ATTN_BLOCK = 128
HEAD_DIM = 128
N_CHIPS = 4

ADAM_LR = 0.001
ADAM_B1 = 0.9
ADAM_B2 = 0.999
ADAM_EPS = 1e-08
ADAM_WD = 0.01
ADAM_STEP = 10

VMEM_LIMIT = 56 * 1024 * 1024
NT_DIMS = (((1,), (1,)), ((), ()))
TN_DIMS = (((0,), (0,)), ((), ()))


def _pick(n, pref, mult):
    t = (min(n, pref) // mult) * mult
    while t >= mult:
        if n % t == 0:
            return t
        t -= mult
    return n


def _call(body, name, grid, in_specs, out_specs, out_shape, scratch=(), aliases=None, prefetch=0, after=()):
    params = pltpu.CompilerParams(dimension_semantics=("arbitrary",) * len(grid), vmem_limit_bytes=VMEM_LIMIT)
    after = tuple(after)
    if after:
        inner, n_in = body, prefetch + len(in_specs)

        def body(*refs):
            return inner(*refs[:n_in], *refs[n_in + len(after):])

        in_specs = list(in_specs) + [pl.BlockSpec(memory_space=pl.ANY)] * len(after)
    if prefetch:
        spec = pltpu.PrefetchScalarGridSpec(
            num_scalar_prefetch=prefetch, grid=grid, in_specs=in_specs, out_specs=out_specs, scratch_shapes=list(scratch)
        )
        call = pl.pallas_call(body, name=name, grid_spec=spec, out_shape=out_shape, compiler_params=params,
                              input_output_aliases=aliases or {})
    else:
        call = pl.pallas_call(body, name=name, grid=grid, in_specs=in_specs, out_specs=out_specs, out_shape=out_shape,
                              scratch_shapes=list(scratch), compiler_params=params, input_output_aliases=aliases or {})
    return lambda *args: call(*args, *after)


def _sds(shape, dtype):
    return jax.ShapeDtypeStruct(shape, dtype)


def _sig(x):
    return 1.0 / (1.0 + jnp.exp(-x))


def _rstd(x):
    return lax.rsqrt(jnp.mean(x * x, axis=-1, keepdims=True) + NORM_EPS)


def _norm_bwd(dy, xhat, r, g):
    dxh = dy * g
    return r * (dxh - xhat * jnp.mean(dxh * xhat, axis=-1, keepdims=True))


def _dot(a, b):
    return jnp.dot(a, b, preferred_element_type=F32)


def _dot_nt(a, b):
    return lax.dot_general(a, b, NT_DIMS, preferred_element_type=F32)


def _dot_tn(a, b):
    return lax.dot_general(a, b, TN_DIMS, preferred_element_type=F32)


def _ffn_fwd(x, g, w_in, w_out, name, after=()):
    T, D = x.shape
    F = w_out.shape[0]
    tm = _pick(T, 256, 8)

    def body(x_ref, g_ref, wg_ref, wu_ref, wo_ref, xo_ref, gate_ref, up_ref):
        xv = x_ref[...]
        h = (xv * _rstd(xv) * g_ref[...]).astype(BF16)
        gate = _dot(h, wg_ref[...])
        up = _dot(h, wu_ref[...])
        gate_ref[...] = gate.astype(BF16)
        up_ref[...] = up.astype(BF16)
        a = (gate * _sig(gate) * up).astype(BF16)
        xo_ref[...] = xv + 0.5 * _dot(a, wo_ref[...])

    row = lambda i: (i, 0)
    held = lambda shape, k: pl.BlockSpec(shape, lambda i: (0, k), pipeline_mode=pl.Buffered(1))
    return _call(
        body, name, (T // tm,),
        [pl.BlockSpec((tm, D), row), pl.BlockSpec((1, D), lambda i: (0, 0)), held((D, F), 0), held((D, F), 1), held((F, D), 0)],
        [pl.BlockSpec((tm, D), row), pl.BlockSpec((tm, F), row), pl.BlockSpec((tm, F), row)],
        [_sds((T, D), F32), _sds((T, F), BF16), _sds((T, F), BF16)], after=after,
    )(x, g, w_in, w_in, w_out)


def _ffn_bwd(x, g, dy, gate, up, w_in, w_out, name, after=()):
    T, D = x.shape
    F = w_out.shape[0]

    def body_a(dy_ref, gate_ref, up_ref, wo_ref, dgate_ref, dup_ref, a_ref):
        dyb = (0.5 * dy_ref[...]).astype(BF16)
        gate = gate_ref[...].astype(F32)
        up = up_ref[...].astype(F32)
        sg = _sig(gate)
        sl = gate * sg
        a_ref[...] = (sl * up).astype(BF16)
        da = _dot_nt(dyb, wo_ref[...])
        dgate_ref[...] = (da * up * (sg * (1.0 + gate * (1.0 - sg)))).astype(BF16)
        dup_ref[...] = (da * sl).astype(BF16)

    ta = _pick(T, 256, 8)
    tile = pl.BlockSpec((ta, F), lambda i: (i, 0))
    dgate, dup, a = _call(
        body_a, name + "_hidden", (T // ta,),
        [pl.BlockSpec((ta, D), lambda i: (i, 0)), tile, tile,
         pl.BlockSpec((F, D), lambda i: (0, 0), pipeline_mode=pl.Buffered(1))],
        [tile, tile, tile],
        [_sds((T, F), BF16)] * 3, after=after,
    )(dy, gate, up, w_out)

    def body_b(x_ref, g_ref, dy_ref, dgate_ref, dup_ref, wg_ref, wu_ref, dx_ref, dg_ref, h_ref, dyb_ref):
        @pl.when(pl.program_id(0) == 0)
        def _():
            dg_ref[...] = jnp.zeros_like(dg_ref)

        xv = x_ref[...]
        r = _rstd(xv)
        xh = xv * r
        h_ref[...] = (xh * g_ref[...]).astype(BF16)
        dyb_ref[...] = (0.5 * dy_ref[...]).astype(BF16)
        dh = _dot_nt(dgate_ref[...], wg_ref[...]) + _dot_nt(dup_ref[...], wu_ref[...])
        dx_ref[...] = dy_ref[...] + _norm_bwd(dh, xh, r, g_ref[...])
        dg_ref[...] += jnp.sum(dh * xh, axis=0, keepdims=True)

    tb = _pick(T, 512, 8)
    row = lambda i: (i, 0)
    held = lambda k: pl.BlockSpec((D, F), lambda i: (0, k), pipeline_mode=pl.Buffered(1))
    dx, dg, h, dyb = _call(
        body_b, name, (T // tb,),
        [pl.BlockSpec((tb, D), row), pl.BlockSpec((1, D), lambda i: (0, 0)), pl.BlockSpec((tb, D), row),
         pl.BlockSpec((tb, F), row), pl.BlockSpec((tb, F), row), held(0), held(1)],
        [pl.BlockSpec((tb, D), row), pl.BlockSpec((1, D), lambda i: (0, 0)), pl.BlockSpec((tb, D), row), pl.BlockSpec((tb, D), row)],
        [_sds((T, D), F32), _sds((1, D), F32), _sds((T, D), BF16), _sds((T, D), BF16)],
    )(x, g, dy, dgate, dup, w_in, w_in)
    return dx, dg, dgate, dup, a, h, dyb


def _mm_tn(a, b, bm, bn, out_shape, out_block, out_map, scale, name, prev=None):
    K, M = a.shape
    N = b.shape[1]

    def body(a_ref, b_ref, *rest):
        rest[-1][...] = (_dot_tn(a_ref[...].astype(BF16), b_ref[...].astype(BF16)) * scale).astype(BF16)

    in_specs = [pl.BlockSpec((K, bm), lambda mi, ni: (0, mi)), pl.BlockSpec((K, bn), lambda mi, ni: (0, ni))]
    args = [a, b]
    aliases = None
    if prev is not None:
        in_specs += [pl.BlockSpec(memory_space=pl.ANY)]
        args += [prev]
        aliases = {2: 0}
    return _call(body, name, (M // bm, N // bn), in_specs, [pl.BlockSpec(out_block, out_map)],
                 [_sds(out_shape, BF16)], aliases=aliases)(*args)[0]


def _mm_tn_parts(a, bs, bn, cols, col0, name, prev=None, after=()):
    K, M = a.shape
    N = bs[0].shape[1]
    nt = N // bn
    ns = len(bs)

    def body(a_ref, *refs):
        o_ref = refs[-1]
        part = pl.program_id(0) // nt
        for s in range(ns):
            @pl.when(part == s)
            def _(s=s):
                o_ref[...] = _dot_tn(a_ref[...], refs[s][...]).astype(BF16)

    part_spec = lambda s: pl.BlockSpec((K, bn), lambda j: (0, jnp.clip(j - s * nt, 0, nt - 1)))
    in_specs = [pl.BlockSpec((K, M), lambda j: (0, 0))] + [part_spec(s) for s in range(ns)]
    args = [a, *bs]
    aliases = None
    if prev is not None:
        in_specs += [pl.BlockSpec(memory_space=pl.ANY)]
        args += [prev]
        aliases = {ns + 1: 0}
    return _call(body, name, (ns * nt,), in_specs, [pl.BlockSpec((M, bn), lambda j: (0, col0 // bn + j))],
                 [_sds((M, cols), BF16)], aliases=aliases, after=after)(*args)[0]


def _conv_pre(x, g, w1, b1, name):
    T, D = x.shape
    tm = _pick(T, 512, 8)

    def body(x_ref, g_ref, w_ref, b_ref, ag_ref, u_ref, h_ref):
        xv = x_ref[...]
        h = (xv * _rstd(xv) * g_ref[...]).astype(BF16)
        h_ref[...] = h
        ag = _dot(h, w_ref[...]) + b_ref[...]
        ag_ref[...] = ag.astype(BF16)
        u_ref[...] = ag[:, :D] * _sig(ag[:, D:])

    return _call(
        body, name, (T // tm,),
        [pl.BlockSpec((tm, D), lambda i: (i, 0)), pl.BlockSpec((1, D), lambda i: (0, 0)),
         pl.BlockSpec((D, 2 * D), lambda i: (0, 0)), pl.BlockSpec((1, 2 * D), lambda i: (0, 0))],
        [pl.BlockSpec((tm, 2 * D), lambda i: (i, 0)), pl.BlockSpec((tm, D), lambda i: (i, 0)),
         pl.BlockSpec((tm, D), lambda i: (i, 0))],
        [_sds((T, 2 * D), BF16), _sds((T, D), F32), _sds((T, D), BF16)],
    )(x, g, w1, b1)


_DW_PAD = 32
_DW_CHUNK = 256


def _dwconv(u, w, b, name):
    T, D = u.shape
    K = w.shape[0]
    ch = _pick(T, _DW_CHUNK, 8)
    lead = _DW_PAD - (K - 1)

    def body(u_ref, w_ref, b_ref, c_ref, ext):
        ext[pl.ds(0, _DW_PAD), :] = jnp.zeros((_DW_PAD, 128), F32)
        ext[pl.ds(_DW_PAD, T), :] = u_ref[...]
        for c0 in range(0, T, ch):
            acc = jnp.zeros((ch, 128), F32) + b_ref[...]
            for k in range(K):
                acc = acc + w_ref[pl.ds(k, 1), :] * ext[pl.ds(c0 + lead + k, ch), :]
            c_ref[pl.ds(c0, ch), :] = acc

    return _call(
        body, name, (D // 128,),
        [pl.BlockSpec((T, 128), lambda i: (0, i)), pl.BlockSpec((K, 128), lambda i: (0, i)),
         pl.BlockSpec((1, 128), lambda i: (0, i))],
        [pl.BlockSpec((T, 128), lambda i: (0, i))],
        [_sds((T, D), F32)],
        scratch=[pltpu.VMEM((T + _DW_PAD, 128), F32)],
    )(u, w, b)[0]


def _dwconv_bwd(dc, u, w, name):
    T, D = u.shape
    K = w.shape[0]
    ch = _pick(T, _DW_CHUNK, 8)

    def body(dc_ref, u_ref, w_ref, du_ref, dw_ref, db_ref, dext):
        dext[pl.ds(0, T), :] = dc_ref[...]
        dext[pl.ds(T, _DW_PAD), :] = jnp.zeros((_DW_PAD, 128), F32)
        dws =[jnp.zeros((8, 128), F32) for _ in range(K)]
        dbs = jnp.zeros((8, 128), F32)
        for c0 in range(0, T, ch):
            uv = u_ref[pl.ds(c0, ch), :]
            dbs = dbs + jnp.sum(dc_ref[pl.ds(c0, ch), :].reshape(ch // 8, 8, 128), axis=0)
            acc = jnp.zeros((ch, 128), F32)
            for k in range(K):
                win = dext[pl.ds(c0 + (K - 1) - k, ch), :]
                acc = acc + w_ref[pl.ds(k, 1), :] * win
                dws[k] = dws[k] + jnp.sum((win * uv).reshape(ch // 8, 8, 128), axis=0)
            du_ref[pl.ds(c0, ch), :] = acc
        for k in range(K):
            dw_ref[pl.ds(k, 1), :] = jnp.sum(dws[k], axis=0, keepdims=True)
        db_ref[...] = jnp.sum(dbs, axis=0, keepdims=True)

    return _call(
        body, name, (D // 128,),
        [pl.BlockSpec((T, 128), lambda i: (0, i)), pl.BlockSpec((T, 128), lambda i: (0, i)),
         pl.BlockSpec((K, 128), lambda i: (0, i))],
        [pl.BlockSpec((T, 128), lambda i: (0, i)), pl.BlockSpec((K, 128), lambda i: (0, i)),
         pl.BlockSpec((1, 128), lambda i: (0, i))],
        [_sds((T, D), F32), _sds((K, D), F32), _sds((1, D), F32)],
        scratch=[pltpu.VMEM((T + _DW_PAD, 128), F32)],
    )(dc, u, w)


def _conv_post(c, x, ng, w2, b2, name):
    T, D = x.shape
    tm = _pick(T, 512, 8)

    def body(c_ref, x_ref, ng_ref, w_ref, b_ref, xo_ref, s_ref):
        cv = c_ref[...]
        n = cv * _rstd(cv) * ng_ref[...]
        s = (n * _sig(n)).astype(BF16)
        s_ref[...] = s
        xo_ref[...] = x_ref[...] + _dot(s, w_ref[...]) + b_ref[...]

    row = lambda i: (i, 0)
    fix = lambda i: (0, 0)
    return _call(
        body, name, (T // tm,),
        [pl.BlockSpec((tm, D), row), pl.BlockSpec((tm, D), row), pl.BlockSpec((1, D), fix),
         pl.BlockSpec((D, D), fix), pl.BlockSpec((1, D), fix)],
        [pl.BlockSpec((tm, D), row), pl.BlockSpec((tm, D), row)],
        [_sds((T, D), F32), _sds((T, D), BF16)],
    )(c, x, ng, w2, b2)


def _conv_post_bwd(dy, c, ng, w2, name, after=()):
    T, D = dy.shape
    tm = _pick(T, 512, 8)

    def body(dy_ref, c_ref, ng_ref, w_ref, dc_ref, dng_ref, db_ref):
        @pl.when(pl.program_id(0) == 0)
        def _():
            dng_ref[...] = jnp.zeros_like(dng_ref)
            db_ref[...] = jnp.zeros_like(db_ref)

        dyv = dy_ref[...]
        ds = _dot_nt(dyv.astype(BF16), w_ref[...])
        cv = c_ref[...]
        r = _rstd(cv)
        ch = cv * r
        n = ch * ng_ref[...]
        sg = _sig(n)
        dn = ds * (sg * (1.0 + n * (1.0 - sg)))
        dc_ref[...] = _norm_bwd(dn, ch, r, ng_ref[...])
        dng_ref[...] += jnp.sum(dn * ch, axis=0, keepdims=True)
        db_ref[...] += jnp.sum(dyv, axis=0, keepdims=True)

    row = lambda i: (i, 0)
    fix = lambda i: (0, 0)
    return _call(
        body, name, (T // tm,),
        [pl.BlockSpec((tm, D), row), pl.BlockSpec((tm, D), row), pl.BlockSpec((1, D), fix), pl.BlockSpec((D, D), fix)],
        [pl.BlockSpec((tm, D), row), pl.BlockSpec((1, D), fix), pl.BlockSpec((1, D), fix)],
        [_sds((T, D), F32), _sds((1, D), F32), _sds((1, D), F32)], after=after,
    )(dy, c, ng, w2)


def _conv_pre_bwd(du, ag, x, g, dy, w1, name):
    T, D = x.shape
    tm = _pick(T, 512, 8)

    def body(du_ref, ag_ref, x_ref, g_ref, dy_ref, w_ref, dx_ref, dg_ref, dag_ref, db_ref):
        @pl.when(pl.program_id(0) == 0)
        def _():
            dg_ref[...] = jnp.zeros_like(dg_ref)
            db_ref[...] = jnp.zeros_like(db_ref)

        duv = du_ref[...]
        a = ag_ref[:, :D].astype(F32)
        gt = ag_ref[:, D:].astype(F32)
        sg = _sig(gt)
        da = duv * sg
        dgt = duv * a * sg * (1.0 - sg)
        db_ref[:, :D] += jnp.sum(da, axis=0, keepdims=True)
        db_ref[:, D:] += jnp.sum(dgt, axis=0, keepdims=True)
        dab = da.astype(BF16)
        dgb = dgt.astype(BF16)
        dag_ref[:, :D] = dab
        dag_ref[:, D:] = dgb
        dh = _dot_nt(dab, w_ref[:, :D]) + _dot_nt(dgb, w_ref[:, D:])
        xv = x_ref[...]
        r = _rstd(xv)
        xh = xv * r
        dx_ref[...] = dy_ref[...] + _norm_bwd(dh, xh, r, g_ref[...])
        dg_ref[...] += jnp.sum(dh * xh, axis=0, keepdims=True)

    row = lambda i: (i, 0)
    fix = lambda i: (0, 0)
    return _call(
        body, name, (T // tm,),
        [pl.BlockSpec((tm, D), row), pl.BlockSpec((tm, 2 * D), row), pl.BlockSpec((tm, D), row), pl.BlockSpec((1, D), fix),
         pl.BlockSpec((tm, D), row), pl.BlockSpec((D, 2 * D), fix)],
        [pl.BlockSpec((tm, D), row), pl.BlockSpec((1, D), fix), pl.BlockSpec((tm, 2 * D), row),
         pl.BlockSpec((1, 2 * D), fix)],
        [_sds((T, D), F32), _sds((1, D), F32), _sds((T, 2 * D), BF16), _sds((1, 2 * D), F32)],
    )(du, ag, x, g, dy, w1)


def _row_sums(a):
    return _dot(a.astype(BF16), jnp.ones((a.shape[1], 128), BF16))


def _head_rstd(x):
    return lax.rsqrt(_row_sums(x * x) * (1.0 / x.shape[1]) + NORM_EPS)


def _attn_qkv(x, g, wqkv, qn, kn, name, after=()):
    T, D = x.shape
    N = wqkv.shape[1]
    tn = N // 9
    E = HEAD_DIM
    tm = _pick(T, 256, 8)
    ng = qn.shape[0]

    def body(x_ref, g_ref, w_ref, qn_ref, kn_ref, o_ref, a_ref, h_ref):
        xv = x_ref[...]
        h = (xv * _rstd(xv) * g_ref[...]).astype(BF16)
        h_ref[...] = h
        for j in range(9):
            cs = slice(j * tn, (j + 1) * tn)
            res = _dot(h, w_ref[:, cs])
            o_ref[:, cs] = res.astype(BF16)
            if j % 3 == 2:
                a_ref[:, cs] = res.astype(BF16)
                continue
            grp = j // 3
            fac = qn_ref[grp:grp + 1, :] * kn_ref[grp:grp + 1, :] * (E ** -0.5) if j % 3 == 0 else None
            for h_i in range(tn // E):
                hs = slice(h_i * E, (h_i + 1) * E)
                xh = res[:, hs]
                hat = xh * _head_rstd(xh)
                a_ref[:, j * tn + h_i * E:j * tn + (h_i + 1) * E] = (hat if fac is None else hat * fac).astype(BF16)

    row = lambda i: (i, 0)
    fix = lambda i: (0, 0)
    return _call(
        body, name, (T // tm,),
        [pl.BlockSpec((tm, D), row), pl.BlockSpec((1, D), fix), pl.BlockSpec((D, N), fix, pipeline_mode=pl.Buffered(1)),
         pl.BlockSpec((ng, E), fix), pl.BlockSpec((ng, E), fix)],
        [pl.BlockSpec((tm, N), row), pl.BlockSpec((tm, N), row), pl.BlockSpec((tm, D), row)],
        [_sds((T, N), BF16), _sds((T, N), BF16), _sds((T, D), BF16)], after=after,
    )(x, g, wqkv, qn, kn)


def _band_mask(q, steps, nblk):
    i = lax.broadcasted_iota(jnp.int32, (q, 2 * q), 0)
    j = lax.broadcasted_iota(jnp.int32, (q, 2 * q), 1)
    diff = q + i - j
    first_key = jnp.where(nblk > 0, 0, q)
    return (diff >= 0) & (diff <= steps) & (j >= first_key)


def _per_row(blk, width):
    e = blk.shape[1]
    if width % e == 0:
        return jnp.concatenate([blk] * (width // e), axis=1)
    return jnp.broadcast_to(blk[:, :1], (blk.shape[0], width))


def _streams(a, dil, to_streams, name, col0=0, ncols=None):
    T, C = a.shape
    ncols = C if ncols is None else ncols
    Q = ATTN_BLOCK
    run = Q * dil
    reps = max(1, min(2048 // run, T // run))
    while T % (run * reps):
        reps -= 1
    rows = run * reps
    cw = _pick(ncols, 512, 128)
    ns = cw // 128

    def body(a_ref, o_ref, scr):
        for s in range(ns):
            ls = slice(s * 128, (s + 1) * 128)
            slab = scr.at[s]
            if to_streams:
                slab[...] = a_ref[:, ls].astype(F32)
                for u in range(reps):
                    for r in range(dil):
                        o_ref[pl.ds(u * run + r * Q, Q), ls] = slab[pl.ds(u * run + r, Q, stride=dil), :].astype(a.dtype)
            else:
                for u in range(reps):
                    for r in range(dil):
                        slab[pl.ds(u * run + r, Q, stride=dil), :] = a_ref[pl.ds(u * run + r * Q, Q), ls].astype(F32)
                o_ref[:, ls] = slab[...].astype(a.dtype)

    return _call(
        body, name, (T // rows, ncols // cw),
        [pl.BlockSpec((rows, cw), lambda i, j: (i, col0 // cw + j))],
        [pl.BlockSpec((rows, cw), lambda i, j: (i, j))],
        [_sds((T, ncols), a.dtype)],
        scratch=[pltpu.VMEM((ns, rows, 128), F32)],
    )(a)[0]


def _attn_fwd(qkv, base, HE, window, dil, name):
    T = qkv.shape[0]
    H = HE // HEAD_DIM
    E = HEAD_DIM
    Q = ATTN_BLOCK
    nb = T // dil // Q
    steps = window // dil

    def body(q_ref, kc_ref, kp_ref, vc_ref, vp_ref, o_ref, l_ref):
        n = pl.program_id(1)
        valid = _band_mask(Q, steps, n)
        ones = jnp.ones((2 * Q, E), BF16)
        outs, lses = [], []
        for h in range(H):
            hs = slice(h * E, (h + 1) * E)
            k2 = jnp.concatenate([kp_ref[:, hs], kc_ref[:, hs]], axis=0)
            v2 = jnp.concatenate([vp_ref[:, hs], vc_ref[:, hs]], axis=0)
            s = jnp.where(valid, _dot_nt(q_ref[:, hs], k2), -1e30)
            m = jnp.max(s, axis=-1, keepdims=True)
            p = jnp.exp(s - m).astype(BF16)
            acc = _dot(p, jnp.concatenate([v2, ones], axis=1))
            l = acc[:, E:]
            outs.append((acc[:, :E] * (1.0 / l)).astype(o_ref.dtype))
            lses.append(m + jnp.log(l))
        o_ref[...] = jnp.concatenate(outs, axis=1)
        l_ref[...] = jnp.concatenate(lses, axis=1)

    blk = lambda s, back: pl.BlockSpec((Q, HE), lambda r, n: (jnp.maximum(n - back, 0) * dil + r, base + s))
    out = pl.BlockSpec((Q, HE), lambda r, n: (n * dil + r, 0))
    return _call(
        body, name, (dil, nb),
        [blk(0, 0), blk(1, 0), blk(1, 1), blk(2, 0), blk(2, 1)],
        [out, out],
        [_sds((T, HE), BF16), _sds((T, HE), F32)],
    )(qkv, qkv, qkv, qkv, qkv)


def _attn_merge(os, lses, x, wo, name):
    T, D = x.shape
    HE = wo.shape[0]
    tm = _pick(T, 512, 8)
    ng = len(os)

    def body(*refs):
        o_refs = refs[:ng]
        l_refs = refs[ng:2 * ng]
        x_ref, w_ref, xo_ref, om_ref, lt_ref = refs[2 * ng:]
        ls = [r[...] for r in l_refs]
        m = functools.reduce(jnp.maximum, ls)
        es = [jnp.exp(l - m) for l in ls]
        tot = functools.reduce(lambda a, b: a + b, es)
        inv = 1.0 / tot
        om = functools.reduce(lambda a, b: a + b, [e * inv * r[...] for e, r in zip(es, o_refs)])
        omb = om.astype(BF16)
        om_ref[...] = omb
        lt_ref[...] = m + jnp.log(tot)
        xo_ref[...] = x_ref[...] + _dot(omb, w_ref[...])

    row = lambda i: (i, 0)
    fix = lambda i: (0, 0)
    return _call(
        body, name, (T // tm,),
        [pl.BlockSpec((tm, HE), row)] * (2 * ng) + [pl.BlockSpec((tm, D), row), pl.BlockSpec((HE, D), fix)],
        [pl.BlockSpec((tm, D), row), pl.BlockSpec((tm, HE), row), pl.BlockSpec((tm, HE), row)],
        [_sds((T, D), F32), _sds((T, HE), BF16), _sds((T, HE), F32)],
    )(*os, *lses, x, wo)


def _attn_out_bwd(dy, om, wo, name, after=()):
    T, D = dy.shape
    HE = wo.shape[0]
    E = HEAD_DIM
    tm = _pick(T, 512, 8)

    def body(dy_ref, om_ref, w_ref, dom_ref, dl_ref):
        dom = _dot_nt(dy_ref[...].astype(BF16), w_ref[...])
        dom_ref[...] = dom.astype(BF16)
        prod = dom * om_ref[...].astype(F32)
        for h in range(HE // E):
            hs = slice(h * E, (h + 1) * E)
            dl_ref[:, hs] = jnp.broadcast_to(jnp.sum(prod[:, hs], axis=-1, keepdims=True), (tm, E))

    row = lambda i: (i, 0)
    return _call(
        body, name, (T // tm,),
        [pl.BlockSpec((tm, D), row), pl.BlockSpec((tm, HE), row), pl.BlockSpec((HE, D), lambda i: (0, 0))],
        [pl.BlockSpec((tm, HE), row), pl.BlockSpec((tm, HE), row)],
        [_sds((T, HE), BF16), _sds((T, HE), F32)], after=after,
    )(dy, om, wo)


def _attn_bwd(qkv, base, HE, dom, lse, delta, window, dil, name):
    T = qkv.shape[0]
    H = HE // HEAD_DIM
    E = HEAD_DIM
    Q = ATTN_BLOCK
    nb = T // dil // Q
    steps = window // dil

    def body(q_ref, kc_ref, kp_ref, vc_ref, vp_ref, do_ref, l_ref, dl_ref, dq_ref, dk_ref, dv_ref, ck_sc, cv_sc):
        n = pl.program_id(1)

        @pl.when(n == 0)
        def _():
            ck_sc[...] = jnp.zeros_like(ck_sc)
            cv_sc[...] = jnp.zeros_like(cv_sc)

        @pl.when(n < nb)
        def _():
            valid = _band_mask(Q, steps, n)
            ck_old = ck_sc[...]
            cv_old = cv_sc[...]
            dqs, dks, dvs = [], [], []
            for h in range(H):
                hs = slice(h * E, (h + 1) * E)
                q = q_ref[:, hs]
                do = do_ref[:, hs]
                k2 = jnp.concatenate([kp_ref[:, hs], kc_ref[:, hs]], axis=0)
                v2 = jnp.concatenate([vp_ref[:, hs], vc_ref[:, hs]], axis=0)
                p = jnp.where(valid, jnp.exp(_dot_nt(q, k2) - _per_row(l_ref[:, hs], 2 * Q)), 0.0)
                ds = (p * (_dot_nt(do, v2) - _per_row(dl_ref[:, hs], 2 * Q))).astype(BF16)
                dqs.append(_dot(ds, k2).astype(BF16))
                dks.append(_dot_tn(q, ds).T)
                dvs.append(_dot_tn(do, p.astype(BF16)).T)
            cat = lambda parts: jnp.concatenate(parts, axis=1)
            dq_ref[...] = cat(dqs)
            dk_ref[...] = (ck_old + cat([d[:Q] for d in dks])).astype(BF16)
            dv_ref[...] = (cv_old + cat([d[:Q] for d in dvs])).astype(BF16)
            ck_sc[...] = cat([d[Q:] for d in dks])
            cv_sc[...] = cat([d[Q:] for d in dvs])

        @pl.when(n == nb)
        def _():
            dk_ref[...] = ck_sc[...].astype(BF16)
            dv_ref[...] = cv_sc[...].astype(BF16)

    nq = lambda n: jnp.minimum(n, nb - 1)
    blk = lambda s, back: pl.BlockSpec((Q, HE), lambda r, n: (jnp.maximum(nq(n) - back, 0) * dil + r, base + s))
    qblk = pl.BlockSpec((Q, HE), lambda r, n: (nq(n) * dil + r, 0))
    kblk = pl.BlockSpec((Q, HE), lambda r, n: (jnp.maximum(n - 1, 0) * dil + r, 0))
    return _call(
        body, name, (dil, nb + 1),
        [blk(0, 0), blk(1, 0), blk(1, 1), blk(2, 0), blk(2, 1), qblk, qblk, qblk],
        [qblk, kblk, kblk],
        [_sds((T, HE), BF16)] * 3,
        scratch=[pltpu.VMEM((Q, HE), F32), pltpu.VMEM((Q, HE), F32)],
    )(qkv, qkv, qkv, qkv, qkv, dom, lse, delta)


def _qk_norm_bwd(dq, dk, qkv, base, HE, gq, gk, name):
    T = dq.shape[0]
    E = HEAD_DIM
    tm = _pick(T, 512, 8)
    scale = E ** -0.5

    def body(dq_ref, dk_ref, q_ref, k_ref, gq_ref, gk_ref, oq_ref, ok_ref, dgq_ref, dgk_ref):
        @pl.when(pl.program_id(0) == 0)
        def _():
            dgq_ref[...] = jnp.zeros_like(dgq_ref)
            dgk_ref[...] = jnp.zeros_like(dgk_ref)

        gqv = gq_ref[...]
        gkv = gk_ref[...]
        c = gqv * gkv * scale
        dc = jnp.zeros((1, E), F32)
        for h in range(HE // E):
            hs = slice(h * E, (h + 1) * E)
            q = q_ref[:, hs].astype(F32)
            rq = _head_rstd(q)
            qh = q * rq
            a = dq_ref[:, hs].astype(F32)
            dc = dc + jnp.sum(a * qh, axis=0, keepdims=True)
            dqh = a * c
            oq_ref[:, hs] = (rq * (dqh - qh * (_row_sums(dqh * qh) * (1.0 / E)))).astype(BF16)
            k = k_ref[:, hs].astype(F32)
            rk = _head_rstd(k)
            kh = k * rk
            b = dk_ref[:, hs].astype(F32)
            ok_ref[:, hs] = (rk * (b - kh * (_row_sums(b * kh) * (1.0 / E)))).astype(BF16)
        dgq_ref[...] += dc * (gkv * scale)
        dgk_ref[...] += dc * (gqv * scale)

    row = lambda i: (i, 0)
    vec = pl.BlockSpec((1, E), lambda i: (0, 0))
    return _call(
        body, name, (T // tm,),
        [pl.BlockSpec((tm, HE), row), pl.BlockSpec((tm, HE), row), pl.BlockSpec((tm, HE), lambda i: (i, base)),
         pl.BlockSpec((tm, HE), lambda i: (i, base + 1)), vec, vec],
        [pl.BlockSpec((tm, HE), row), pl.BlockSpec((tm, HE), row), vec, vec],
        [_sds((T, HE), BF16), _sds((T, HE), BF16), _sds((1, E), F32), _sds((1, E), F32)],
    )(dq, dk, qkv, qkv, gq, gk)


def _attn_qkv_bwd(dqkv, x, g, dy, wqkv, name):
    T, D = x.shape
    HE = wqkv.shape[1] // 9
    tm = _pick(T, 256, 8)

    def body(*refs):
        d_refs = refs[:9]
        x_ref, g_ref, dy_ref, w_ref, dx_ref, dg_ref = refs[9:]

        @pl.when(pl.program_id(0) == 0)
        def _():
            dg_ref[...] = jnp.zeros_like(dg_ref)

        dh = _dot_nt(d_refs[0][...], w_ref[:, :HE])
        for s in range(1, 9):
            dh = dh + _dot_nt(d_refs[s][...], w_ref[:, s * HE:(s + 1) * HE])
        xv = x_ref[...]
        r = _rstd(xv)
        xh = xv * r
        dx_ref[...] = dy_ref[...] + _norm_bwd(dh, xh, r, g_ref[...])
        dg_ref[...] += jnp.sum(dh * xh, axis=0, keepdims=True)

    row = lambda i: (i, 0)
    fix = lambda i: (0, 0)
    return _call(
        body, name, (T // tm,),
        [pl.BlockSpec((tm, HE), row)] * 9 + [pl.BlockSpec((tm, D), row), pl.BlockSpec((1, D), fix), pl.BlockSpec((tm, D), row),
                                           pl.BlockSpec((D, 9 * HE), fix, pipeline_mode=pl.Buffered(1))],
        [pl.BlockSpec((tm, D), row), pl.BlockSpec((1, D), fix)],
        [_sds((T, D), F32), _sds((1, D), F32)],
    )(*dqkv, x, g, dy, wqkv)


def _loss_head(y, target, name):
    T, D = y.shape
    tm = _pick(T, 512, 8)

    def body(y_ref, t_ref, dy_ref, sq_ref):
        @pl.when(pl.program_id(0) == 0)
        def _():
            sq_ref[...] = jnp.zeros_like(sq_ref)

        err = y_ref[...] - t_ref[...]
        dy_ref[...] = err * (1.0 / D)
        sq_ref[...] += jnp.sum(err * err, axis=0, keepdims=True)

    row = lambda i: (i, 0)
    return _call(
        body, name, (T // tm,),
        [pl.BlockSpec((tm, D), row), pl.BlockSpec((tm, D), row)],
        [pl.BlockSpec((tm, D), row), pl.BlockSpec((1, D), lambda i: (0, 0))],
        [_sds((T, D), F32), _sds((1, D), F32)],
    )(y, target)


def _local_step(x, target, norm_g, b_pw1, w_dw, b_dw, cng, b_pw2, qn, kn, block_weights, grads_ready, grads_tick):
    T, D = x.shape
    ng = lambda l, k: norm_g[l, k][None, :]
    bn = _pick(D, 256, 128)

    w_in, w_out = [None] * 4, [None] * 4
    (w_in[0], w_out[0]), _ = block_weights(0, x)
    x1, *gu0 = _ffn_fwd(x, ng(0, 0), w_in[0], w_out[0], "ffn_fwd_0")
    (pw1, pw2), _ = block_weights(1, x1)
    ag, u, hc = _conv_pre(x1, ng(0, 1), pw1, b_pw1, "conv_pre")
    c = _dwconv(u, w_dw, b_dw, "dwconv")
    x2, s = _conv_post(c, x1, cng, pw2, b_pw2, "conv_post")
    (w_in[1], w_out[1]), ahead = block_weights(2, x2)
    x3, *gu1 = _ffn_fwd(x2, ng(0, 2), w_in[1], w_out[1], "ffn_fwd_1", after=ahead)
    (w_in[2], w_out[2]), ahead = block_weights(3, x3)
    x4, *gu2 = _ffn_fwd(x3, ng(1, 0), w_in[2], w_out[2], "ffn_fwd_2", after=ahead)
    (wqkv, wo), ahead = block_weights(4, x4)
    HE = wo.shape[0]
    F = w_out[0].shape[0]
    bf = _pick(F, 256, 128)
    bh = _pick(HE, 512, 128)
    qkv, att, ha = _attn_qkv(x4, ng(1, 1), wqkv, qn, kn, "attn_qkv", after=ahead)
    qkv_s = [(att, 3 * gi) if dil == 1 else (_streams(att, dil, True, f"qkv_streams_{gi}", 3 * gi * HE, 3 * HE), 0)
             for gi, (_, dil) in enumerate(ATTN_GROUPS)]
    tokens = lambda a, dil, name: a if dil == 1 else _streams(a, dil, False, name)
    streams = lambda a, dil, name: a if dil == 1 else _streams(a, dil, True, name)
    os, lses = [], []
    for gi, (window, dil) in enumerate(ATTN_GROUPS):
        o, l = _attn_fwd(*qkv_s[gi], HE, window, dil, f"attn_fwd_{gi}")
        os.append(tokens(o, dil, f"o_tokens_{gi}"))
        lses.append(tokens(l, dil, f"lse_tokens_{gi}"))
    x5, om, lse = _attn_merge(os, lses, x4, wo, "attn_merge")
    (w_in[3], w_out[3]), _ = block_weights(5, x5)
    x6, *gu3 = _ffn_fwd(x5, ng(1, 2), w_in[3], w_out[3], "ffn_fwd_3")
    dy, sq = _loss_head(x6, target, "loss_head")

    grads = {"ffn_w_in": [None] * 4, "ffn_w_out": [None] * 4}
    dnorm = [[None] * 3 for _ in range(2)]

    def ffn_back(lf, k, xin, gvec, dy, gu, after):
        dx, dg, dgate, dup, a, h, dyb = _ffn_bwd(xin, gvec, dy, gu[0], gu[1], w_in[lf], w_out[lf], f"ffn_bwd_{lf}", after=after)
        grads["ffn_w_in"][lf] = _mm_tn_parts(h, [dgate, dup], bf, 2 * F, 0, f"ffn_dw_in_{lf}",
                                             after=grads_tick(dx) if lf == 0 else ())
        grads["ffn_w_out"][lf] = _mm_tn(a, dyb, bf, D, (F, D), (bf, D), lambda mi, ni: (mi, 0), 1.0, f"ffn_dw_out_{lf}")
        return dx, dg, grads_ready(k, (grads["ffn_w_in"][lf], grads["ffn_w_out"][lf]), dx)

    dx, dnorm[1][2], tok = ffn_back(3, 5, x5, ng(1, 2), dy, gu3, ())
    dom, delta = _attn_out_bwd(dx, om, wo, "attn_out_bwd", after=tok)
    grads["attn_w_o"] = _mm_tn(om, dx, HE, bn, (HE, D), (HE, bn), lambda mi, ni: (0, ni), 1.0, "attn_dw_o")
    dqkv, dgq, dgk = [], [], []
    for gi, (window, dil) in enumerate(ATTN_GROUPS):
        ds = _attn_bwd(*qkv_s[gi], HE, streams(dom, dil, f"dom_streams_{gi}"), streams(lse, dil, f"lse_streams_{gi}"),
                       streams(delta, dil, f"delta_streams_{gi}"), window, dil, f"attn_bwd_{gi}")
        dq, dk, dv = [tokens(d, dil, f"d{nm}_tokens_{gi}") for d, nm in zip(ds, "qkv")]
        dq, dk, a_, b_ = _qk_norm_bwd(dq, dk, qkv, 3 * gi, HE, qn[gi][None, :], kn[gi][None, :], f"qk_norm_bwd_{gi}")
        dqkv += [dq, dk, dv]
        dgq.append(a_)
        dgk.append(b_)
    grads["attn_q_norm"] = jnp.concatenate(dgq, axis=0)
    grads["attn_k_norm"] = jnp.concatenate(dgk, axis=0)
    d_qkv = None
    for gi in range(len(ATTN_GROUPS)):
        d_qkv = _mm_tn_parts(ha, dqkv[3 * gi:3 * gi + 3], bh, 9 * HE, 3 * gi * HE, f"attn_dw_qkv_{gi}", prev=d_qkv)
    grads["attn_w_qkv"] = d_qkv
    dx, dnorm[1][1] = _attn_qkv_bwd(dqkv, x4, ng(1, 1), dx, wqkv, "attn_qkv_bwd")
    tok = grads_ready(4, (grads["attn_w_qkv"], grads["attn_w_o"]), dx)
    dx, dnorm[1][0], tok = ffn_back(2, 3, x3, ng(1, 0), dx, gu2, tok)

    dx, dnorm[0][2], tok = ffn_back(1, 2, x2, ng(0, 2), dx, gu1, tok)
    dc, grads["conv_norm_g"], grads["conv_b_pw2"] = _conv_post_bwd(dx, c, cng, pw2, "conv_post_bwd", after=tok)
    grads["conv_w_pw2"] = _mm_tn(s, dx, D, bn, (D, D), (D, bn), lambda mi, ni: (0, ni), 1.0, "conv_dw_pw2")
    du, grads["conv_w_dw"], grads["conv_b_dw"] = _dwconv_bwd(dc, u, w_dw, "dwconv_bwd")
    dx, dnorm[0][1], dag, grads["conv_b_pw1"] = _conv_pre_bwd(du, ag, x1, ng(0, 1), dx, pw1, "conv_pre_bwd")
    grads["conv_w_pw1"] = _mm_tn(hc, dag, D, 2 * bn, (D, 2 * D), (D, 2 * bn), lambda mi, ni: (0, ni), 1.0, "conv_dw_pw1")
    tok = grads_ready(1, (grads["conv_w_pw1"], grads["conv_w_pw2"]), dx)
    dx, dnorm[0][0], _ = ffn_back(0, 0, x, ng(0, 0), dx, gu0, tok)

    grads["norm_g"] = jnp.concatenate([jnp.concatenate(r, axis=0)[None] for r in dnorm], axis=0)
    return sq, dx, grads


class _Sharded:
    def __init__(self, name, full3, half_axis, shard_axis, src, slab=None):
        self.name, self.full3, self.half_axis, self.shard_axis = name, tuple(full3), half_axis, shard_axis
        self.src, self.slab = src, slab

    def source(self, refs):
        return refs[self.src] if self.slab is None else refs[self.src].at[self.slab]

    def _cut(self, shape, axis, parts):
        s = list(shape)
        s[axis] //= parts
        return tuple(s)

    @property
    def shard3(self):
        return self._cut(self.full3, self.shard_axis, N_CHIPS)

    @property
    def pair3(self):
        return self._cut(self.full3, self.half_axis, 2)

    @property
    def part3(self):
        return self._cut(self.shard3, self.half_axis, 2)

    @staticmethod
    def _slice(ref, axis, idx, parts):
        n = ref.shape[axis] // parts
        start = idx * n
        minor = len(ref.shape) - 1 - axis
        if minor < 2 and not isinstance(start, int):
            start = pl.multiple_of(start, 128 if minor == 0 else (16 if n % 16 == 0 else 8))
        sl = [slice(None)] * len(ref.shape)
        sl[axis] = pl.ds(start, n)
        return ref.at[tuple(sl)]

    def half(self, ref, h):
        return self._slice(ref, self.half_axis, h, 2)

    def shard(self, ref, j):
        return self._slice(ref, self.shard_axis, j, N_CHIPS)


def _place():
    x, y, c = lax.axis_index("x"), lax.axis_index("y"), lax.axis_index("c")
    return x, y, c, 2 * x + y


_RELS = (1, 2, 3)


def _peer(x, y, rel):
    px = 1 - x if rel & 2 else x
    py = 1 - y if rel & 1 else y
    return px, py, 2 * px + py


ANY = pl.BlockSpec(memory_space=pl.ANY)


def _comm_call(body, name, n_in, out_shape, n_sems, aliases=None):
    return pl.pallas_call(
        body, name=name, in_specs=[ANY] * n_in, out_specs=[ANY] * len(out_shape), out_shape=out_shape,
        scratch_shapes=[pltpu.SemaphoreType.DMA((n,)) for n in n_sems],
        input_output_aliases=aliases or {},
        compiler_params=pltpu.CompilerParams(has_side_effects=True),
    )


def _remote(src, dst, send_sem, recv_sem, dev):
    return pltpu.make_async_remote_copy(src_ref=src, dst_ref=dst, send_sem=send_sem, recv_sem=recv_sem, device_id=dev,
                                        device_id_type=MESH)


def _gather_small(small_shards):
    ns = len(small_shards)

    def body(*refs):
        ins, outs = refs[:ns], refs[ns:2 * ns]
        lsem, ssem, rsem = refs[2 * ns:]
        x, y, c, me = _place()
        cols = lambda ref, j: _Sharded._slice(ref, 1, j, N_CHIPS)
        local = [pltpu.make_async_copy(ins[i], cols(outs[i], me), lsem.at[i]) for i in range(ns)]
        sends = []
        for i in range(ns):
            for k, rel in enumerate(_RELS):
                px, py, _ = _peer(x, y, rel)
                sends.append(_remote(ins[i], cols(outs[i], me), ssem.at[3 * i + k], rsem.at[3 * i + k], (px, py, c)))
        for cp in local + sends:
            cp.start()
        for i in range(ns):
            for k, rel in enumerate(_RELS):
                _, _, pj = _peer(x, y, rel)
                got = cols(outs[i], pj)
                _remote(got, got, ssem.at[3 * i + k], rsem.at[3 * i + k], (x, y, c)).wait_recv()
        for cp in sends:
            cp.wait_send()
        for cp in local:
            cp.wait()

    out_shape = [_sds((s.shape[0], s.shape[1] * N_CHIPS), F32) for s in small_shards]
    return _comm_call(body, "gather_small", ns, out_shape, [ns, 3 * ns, 3 * ns])(*small_shards)


HBM = pl.BlockSpec(memory_space=pltpu.HBM)
SEM = pl.BlockSpec(memory_space=pltpu.SEMAPHORE)
DATAFLOW = pltpu.SideEffectType.DATAFLOW_SIDE_EFFECTING


def _in_hbm(a):
    return pltpu.with_memory_space_constraint(a, pltpu.HBM)


def _cast_place(it, shard, scal, after=()):
    a_n, r_n, c_n = it.shard3
    tr = _pick(r_n, 256, 16)
    sa = it.shard_axis

    def body(sc_ref, s_ref, o_ref):
        o_ref[...] = s_ref[...].astype(BF16)

    if it.slab is None:
        src = pl.BlockSpec((1, tr, c_n), lambda a, rb, sc: (a, rb, 0))
    else:
        src = pl.BlockSpec((None, 1, tr, c_n), lambda a, rb, sc: (it.slab, a, rb, 0))
    dst = pl.BlockSpec((1, tr, c_n), lambda a, rb, sc: (a + sc[1] * (a_n if sa == 0 else 0), rb + sc[1] * (r_n // tr if sa == 1 else 0),
                                                       sc[1] if sa == 2 else 0))
    return _call(body, f"cast_place_{it.name}", (a_n, r_n // tr), [src], [dst], [_sds(it.full3, BF16)], prefetch=1,
                 after=after)(scal, shard)[0]


def _gather_start(items, fulls, name):
    ni = len(items)

    def body(*refs):
        outs = refs[ni:]
        ssem, rsem, full = outs[:ni], outs[ni:2 * ni], outs[2 * ni:3 * ni]
        x, y, c, me = _place()
        for i, it in enumerate(items):
            mine = it.half(it.shard(full[i], me), c)
            for k, rel in enumerate(_RELS):
                px, py, _ = _peer(x, y, rel)
                _remote(mine, mine, ssem[i].at[k], rsem[i].at[k], (px, py, c)).start()

    outs = pl.pallas_call(
        body, name=name, in_specs=[HBM] * ni, out_specs=[SEM] * (2 * ni) + [HBM] * ni,
        out_shape=[pltpu.SemaphoreType.DMA((3,))] * (2 * ni) + [pltpu.HBM(it.full3, BF16) for it in items],
        input_output_aliases={j: 2 * ni + j for j in range(ni)},
        compiler_params=pltpu.CompilerParams(has_side_effects=DATAFLOW),
    )(*[_in_hbm(f) for f in fulls])
    return outs[:ni], outs[ni:2 * ni], outs[2 * ni:]


def _gather_forward(items, fulls, ssems, rsems, after, name):
    ni = len(items)

    def body(*refs):
        ssem, rsem = refs[ni:2 * ni], refs[2 * ni:3 * ni]
        outs = refs[3 * ni + 1:]
        full, fsem, gsem = outs[:ni], outs[ni:2 * ni], outs[2 * ni:3 * ni]
        x, y, c, me = _place()
        sib = (x, y, 1 - c)
        for i, it in enumerate(items):
            for k, rel in enumerate(_RELS):
                _, _, pj = _peer(x, y, rel)
                got = it.half(it.shard(full[i], pj), c)
                _remote(got, got, ssem[i].at[k], rsem[i].at[k], sib).wait_recv()
                _remote(got, got, fsem[i].at[k], gsem[i].at[k], sib).start()
        for i, it in enumerate(items):
            mine = it.half(it.shard(full[i], me), c)
            for k in range(3):
                _remote(mine, mine, ssem[i].at[k], rsem[i].at[k], sib).wait_send()

    outs = pl.pallas_call(
        body, name=name, in_specs=[HBM] * ni + [SEM] * (2 * ni) + [ANY],
        out_specs=[HBM] * ni + [SEM] * (2 * ni),
        out_shape=[pltpu.HBM(it.full3, BF16) for it in items] + [pltpu.SemaphoreType.DMA((3,))] * (2 * ni),
        input_output_aliases={i: i for i in range(ni)},
        compiler_params=pltpu.CompilerParams(has_side_effects=DATAFLOW),
    )(*fulls, *ssems, *rsems, after)
    return outs[:ni], outs[ni:2 * ni], outs[2 * ni:]


def _gather_finish(items, fulls, fsems, gsems, name):
    ni = len(items)

    def body(*refs):
        fsem, gsem = refs[ni:2 * ni], refs[2 * ni:3 * ni]
        full = refs[3 * ni:]
        x, y, c, _ = _place()
        sib = (x, y, 1 - c)
        for i, it in enumerate(items):
            for k, rel in enumerate(_RELS):
                _, _, pj = _peer(x, y, rel)
                got = it.half(it.shard(full[i], pj), 1 - c)
                _remote(got, got, fsem[i].at[k], gsem[i].at[k], sib).wait_recv()
                sent = it.half(it.shard(full[i], pj), c)
                _remote(sent, sent, fsem[i].at[k], gsem[i].at[k], sib).wait_send()

    return pl.pallas_call(
        body, name=name, in_specs=[HBM] * ni + [SEM] * (2 * ni), out_specs=[HBM] * ni,
        out_shape=[pltpu.HBM(it.full3, BF16) for it in items],
        input_output_aliases={i: i for i in range(ni)},
        compiler_params=pltpu.CompilerParams(has_side_effects=DATAFLOW),
    )(*fulls, *fsems, *gsems)


def _exchange_start(name, arrays, n_sems, copies):
    na, ns = len(arrays), len(n_sems)

    def body(*refs):
        outs = refs[na:]
        ssem, rsem, thru, token = outs[:ns], outs[ns:2 * ns], outs[2 * ns:2 * ns + na], outs[-1]
        for send, _ in copies(thru, ssem, rsem):
            send.start()
        token[...] = jnp.zeros_like(token)

    outs = pl.pallas_call(
        body, name=name, in_specs=[HBM] * na,
        out_specs=[SEM] * (2 * ns) + [HBM] * na + [pl.BlockSpec(memory_space=pltpu.VMEM)],
        out_shape=[pltpu.SemaphoreType.DMA((n,)) for n in n_sems] * 2 + [pltpu.HBM(a.shape, a.dtype) for a in arrays]
        + [_sds((8, 128), F32)],
        input_output_aliases={j: 2 * ns + j for j in range(na)},
        compiler_params=pltpu.CompilerParams(has_side_effects=DATAFLOW),
    )(*[_in_hbm(a) for a in arrays])
    return outs[:ns], outs[ns:2 * ns], outs[2 * ns:2 * ns + na], outs[-1]


def _exchange_wait(name, arrays, ssems, rsems, copies, after):
    na, ns, nw = len(arrays), len(ssems), len(after)

    def body(*refs):
        ssem, rsem = refs[na:na + ns], refs[na + ns:na + 2 * ns]
        thru = refs[na + 2 * ns + nw:]
        for send, recv in copies(thru, ssem, rsem):
            recv.wait_recv()
            send.wait_send()

    return pl.pallas_call(
        body, name=name, in_specs=[HBM] * na + [SEM] * (2 * ns) + [ANY] * nw, out_specs=[HBM] * na,
        out_shape=[pltpu.HBM(a.shape, a.dtype) for a in arrays],
        input_output_aliases={i: i for i in range(na)},
        compiler_params=pltpu.CompilerParams(has_side_effects=DATAFLOW),
    )(*arrays, *ssems, *rsems, *after)


def _pair_copies(items):
    n = len(items)

    def copies(a, ssem, rsem):
        x, y, c, _ = _place()
        cps = [_remote(it.half(a[i], 1 - c), a[n + i], ssem[i].at[0], rsem[i].at[0], (x, y, 1 - c)) for i, it in enumerate(items)]
        return [(cp, cp) for cp in cps]

    return copies


def _chip_copies(items):
    n = len(items)

    def copies(a, ssem, rsem):
        x, y, c, _ = _place()
        cps = []
        for i, it in enumerate(items):
            for k, rel in enumerate(_RELS):
                px, py, pj = _peer(x, y, rel)
                cps.append(_remote(it.shard(a[i], pj), a[n + i].at[k], ssem[i].at[k], rsem[i].at[k], (px, py, c)))
        return [(cp, cp) for cp in cps]

    return copies


def _fill_copies(items):
    def copies(a, ssem, rsem):
        x, y, c, _ = _place()
        sib = (x, y, 1 - c)
        out = []
        for i, it in enumerate(items):
            mine, other = it.half(a[i], c), it.half(a[i], 1 - c)
            out.append((_remote(mine, mine, ssem[i].at[0], rsem[i].at[0], sib), _remote(other, other, ssem[i].at[0], rsem[i].at[0], sib)))
        return out

    return copies


def _ew_tiles(d):
    _, rows, cols = d.part3
    return _pick(rows, 256, 16), cols


def _pair_add(d, g_full, got, scal):
    tr, tc = _ew_tiles(d)
    a_n, r_n, c_n = d.pair3
    ha = d.half_axis

    def body(sc_ref, g_ref, r_ref, o_ref, ob_ref):
        s = g_ref[...].astype(F32) + r_ref[...].astype(F32)
        o_ref[...] = s
        ob_ref[...] = s.astype(BF16)

    blk = (1, tr, tc)
    same = pl.BlockSpec(blk, lambda a, rb, cb, sc: (a, rb, cb))
    mine = pl.BlockSpec(blk, lambda a, rb, cb, sc: (a + sc[0] * (a_n if ha == 0 else 0), rb + sc[0] * (r_n // tr if ha == 1 else 0), cb))
    return _call(body, f"pair_add_{d.name}", (a_n, r_n // tr, c_n // tc), [mine, same], [same, same],
                 [_sds(d.pair3, F32), _sds(d.pair3, BF16)], prefetch=1)(scal, g_full, got)


def _chip_reduce(d, pair_f32, got, scal):
    tr, tc = _ew_tiles(d)
    a_n, r_n, c_n = d.part3
    ha, sa = d.half_axis, d.shard_axis

    def body(sc_ref, p_ref, r0, r1, r2, o_ref):
        o_ref[...] = ((p_ref[...] + r0[...].astype(F32)) + r1[...].astype(F32)) + r2[...].astype(F32)

    blk = (1, tr, tc)
    own = pl.BlockSpec(blk, lambda a, rb, sc: (a + sc[1] * (a_n if sa == 0 else 0), rb + sc[1] * (r_n // tr if sa == 1 else 0),
                                               sc[1] if sa == 2 else 0))
    slot = lambda k: pl.BlockSpec((None,) + blk, lambda a, rb, sc: (k, a, rb, 0))
    out = pl.BlockSpec(blk, lambda a, rb, sc: (a + sc[0] * (a_n if ha == 0 else 0), rb + sc[0] * (r_n // tr if ha == 1 else 0), 0))
    return _call(body, f"chip_reduce_{d.name}", (a_n, r_n // tr), [own, slot(0), slot(1), slot(2)], [out],
                 [_sds(d.shard3, F32)], prefetch=1)(scal, pair_f32, got, got, got)[0]


def _adam_math(g, w, m, v):
    m = ADAM_B1 * m + (1.0 - ADAM_B1) * g
    v = ADAM_B2 * v + (1.0 - ADAM_B2) * (g * g)
    m_hat = m / (1.0 - ADAM_B1 ** ADAM_STEP)
    v_hat = v / (1.0 - ADAM_B2 ** ADAM_STEP)
    delta = -ADAM_LR * (m_hat / (jnp.sqrt(v_hat) + ADAM_EPS) + ADAM_WD * w)
    return delta, m, v


def _adam(d, g, w, m, v, prev=None):
    tr, tc = _ew_tiles(d)
    a_n, r_n, c_n = d.shard3
    n_prev = 0 if prev is None else 4

    def body(g_ref, w_ref, m_ref, v_ref, *rest):
        go_ref, d_ref, mo_ref, vo_ref = rest[n_prev:]
        gv = g_ref[...]
        go_ref[...] = gv
        d_ref[...], mo_ref[...], vo_ref[...] = _adam_math(gv, w_ref[...], m_ref[...], v_ref[...])

    plain = pl.BlockSpec((1, tr, tc), lambda a, rb: (a, rb, 0))
    if d.slab is None:
        wspec, shape = plain, d.shard3
    else:
        wspec, shape = pl.BlockSpec((None, 1, tr, tc), lambda a, rb: (d.slab, a, rb, 0)), (4,) + d.shard3
    in_specs = [plain] + [wspec] * 3
    args = [g, w, m, v]
    aliases = None
    if prev is not None:
        in_specs += [pl.BlockSpec(memory_space=pl.ANY)] * 4
        args += list(prev)
        aliases = {4 + k: k for k in range(4)}
    return _call(body, f"adam_{d.name}", (a_n, r_n // tr), in_specs, [wspec] * 4, [_sds(shape, F32)] * 4, aliases=aliases)(*args)


def _adam_small(gs, ws, ms, vs):
    n = len(gs)

    def body(*refs):
        for i in range(n):
            g, w, m, v = (refs[k * n + i][...] for k in range(4))
            d, mo, vo = _adam_math(g, w, m, v)
            refs[4 * n + i][...] = d
            refs[5 * n + i][...] = mo
            refs[6 * n + i][...] = vo

    vm = pl.BlockSpec(memory_space=pltpu.VMEM)
    outs = pl.pallas_call(body, name="adam_small", in_specs=[vm] * (4 * n), out_specs=[vm] * (3 * n),
                          out_shape=[_sds(g.shape, F32) for g in gs] * 3)(*gs, *ws, *ms, *vs)
    return outs[:n], outs[n:2 * n], outs[2 * n:]


def _allreduce_small(packed):
    rows, cols = packed.shape
    others = [(dx, dy, dc) for dx in (0, 1) for dy in (0, 1) for dc in (0, 1) if (dx, dy, dc) != (0, 0, 0)]

    def body(in_ref, out_ref, buf, ssem, rsem):
        x, y, c, _ = _place()
        lin = 4 * x + 2 * y + c
        buf[lin] = in_ref[...]
        cps = []
        for k, (dx, dy, dc) in enumerate(others):
            px = 1 - x if dx else x
            py = 1 - y if dy else y
            pc = 1 - c if dc else c
            cps.append((pltpu.make_async_remote_copy(src_ref=in_ref, dst_ref=buf.at[lin], send_sem=ssem.at[k], recv_sem=rsem.at[k],
                                                     device_id=(px, py, pc), device_id_type=MESH), 4 * px + 2 * py + pc))
        for cp, _ in cps:
            cp.start()
        for k, (cp, plin) in enumerate(cps):
            pltpu.make_async_remote_copy(src_ref=in_ref, dst_ref=buf.at[plin], send_sem=ssem.at[k], recv_sem=rsem.at[k],
                                         device_id=(x, y, c), device_id_type=MESH).wait_recv()
        for cp, _ in cps:
            cp.wait_send()
        acc = buf[0]
        for dev in range(1, 8):
            acc = acc + buf[dev]
        out_ref[...] = acc

    vm = pl.BlockSpec(memory_space=pltpu.VMEM)
    return pl.pallas_call(
        body, name="allreduce_small", in_specs=[vm], out_specs=vm, out_shape=_sds((rows, cols), F32),
        scratch_shapes=[pltpu.VMEM((8, rows, cols), F32), pltpu.SemaphoreType.DMA((7,)), pltpu.SemaphoreType.DMA((7,))],
        compiler_params=pltpu.CompilerParams(has_side_effects=True),
    )(packed)


_BIG = ("ffn_w_in", "ffn_w_out", "conv_w_pw1", "conv_w_pw2", "attn_w_qkv", "attn_w_o")
_SMALL = ("norm_g", "conv_b_pw1", "conv_w_dw", "conv_b_dw", "conv_norm_g", "conv_b_pw2", "attn_q_norm", "attn_k_norm")
_NAMES = ("norm_g", "ffn_w_in", "ffn_w_out", "conv_w_pw1", "conv_b_pw1", "conv_w_dw", "conv_b_dw", "conv_norm_g", "conv_w_pw2",
          "conv_b_pw2", "attn_w_qkv", "attn_q_norm", "attn_k_norm", "attn_w_o")


def _rows8(a, width):
    a = a.reshape(-1, min(a.shape[-1], width))
    return jnp.pad(a, ((0, -a.shape[0] % 8), (0, width - a.shape[1])))


def kernel(x, norm_g, ffn_w_in, ffn_w_out, conv_w_pw1, conv_b_pw1, conv_w_dw, conv_b_dw, conv_norm_g, conv_w_pw2, conv_b_pw2, attn_w_qkv, attn_q_norm, attn_k_norm, attn_w_o, loss_target, m_norm_g, m_ffn_w_in, m_ffn_w_out, m_conv_w_pw1, m_conv_b_pw1, m_conv_w_dw, m_conv_b_dw, m_conv_norm_g, m_conv_w_pw2, m_conv_b_pw2, m_attn_w_qkv, m_attn_q_norm, m_attn_k_norm, m_attn_w_o, v_norm_g, v_ffn_w_in, v_ffn_w_out, v_conv_w_pw1, v_conv_b_pw1, v_conv_w_dw, v_conv_b_dw, v_conv_norm_g, v_conv_w_pw2, v_conv_b_pw2, v_attn_w_qkv, v_attn_q_norm, v_attn_k_norm, v_attn_w_o):
    w = dict(zip(_NAMES, (norm_g, ffn_w_in, ffn_w_out, conv_w_pw1, conv_b_pw1, conv_w_dw, conv_b_dw, conv_norm_g, conv_w_pw2,
                          conv_b_pw2, attn_w_qkv, attn_q_norm, attn_k_norm, attn_w_o)))
    m = dict(zip(_NAMES, (m_norm_g, m_ffn_w_in, m_ffn_w_out, m_conv_w_pw1, m_conv_b_pw1, m_conv_w_dw, m_conv_b_dw, m_conv_norm_g,
                          m_conv_w_pw2, m_conv_b_pw2, m_attn_w_qkv, m_attn_q_norm, m_attn_k_norm, m_attn_w_o)))
    v = dict(zip(_NAMES, (v_norm_g, v_ffn_w_in, v_ffn_w_out, v_conv_w_pw1, v_conv_b_pw1, v_conv_w_dw, v_conv_b_dw, v_conv_norm_g,
                          v_conv_w_pw2, v_conv_b_pw2, v_attn_w_qkv, v_attn_q_norm, v_attn_k_norm, v_attn_w_o)))
    T, D = x.shape[1:]
    F = ffn_w_out.shape[2] * N_CHIPS
    HE = attn_w_o.shape[1] * N_CHIPS
    cx, cy, cc = lax.axis_index("x"), lax.axis_index("y"), lax.axis_index("c")
    me = 2 * cx + cy
    scal = jnp.stack([cc, me]).astype(jnp.int32)

    ffn_in = lambda lf: _Sharded(f"ffn_w_in_{lf}", (2, D // 2, 2 * F), 0, 2, "ffn_w_in", lf)
    ffn_out = lambda lf: _Sharded(f"ffn_w_out_{lf}", (4, F // 4, D), 1, 0, "ffn_w_out", lf)
    items = [
        ffn_in(0), ffn_out(0),
        _Sharded("conv_w_pw1", (2, D // 2, 2 * D), 0, 2, "conv_w_pw1"), _Sharded("conv_w_pw2", (4, D // 4, D), 1, 0, "conv_w_pw2"),
        ffn_in(1), ffn_out(1), ffn_in(2), ffn_out(2),
        _Sharded("attn_w_qkv", (2, D // 2, 9 * HE), 0, 2, "attn_w_qkv"), _Sharded("attn_w_o", (4, HE // 4, D), 1, 0, "attn_w_o"),
        ffn_in(3), ffn_out(3),
    ]
    mat_shapes = {"ffn_w_in": (D, 2 * F), "ffn_w_out": (F, D), "conv_w_pw1": (D, 2 * D), "conv_w_pw2": (D, D),
                  "attn_w_qkv": (D, 9 * HE), "attn_w_o": (HE, D)}

    def as_shards(a, n):
        it = next(i for i in items if i.src == n)
        return a.reshape(((4,) if it.slab is not None else ()) + it.shard3)

    norm_full, dw_full = _gather_small([norm_g.reshape(6, D // 4), conv_w_dw.reshape(CONV_WIDTH, D // 4)])
    place = lambda its, after: [_cast_place(it, as_shards(w[it.src], it.src), scal, after) for it in its]
    ssems, rsems, fulls = _gather_start(items[:2], place(items[:2], ()), "gather_start_first")
    more = _gather_start(items[2:], place(items[2:], fulls[:1]), "gather_start_rest")
    ssems, rsems, fulls = [list(a) + list(b) for a, b in zip((ssems, rsems, fulls), more)]

    early = {}

    def forward(k, after):
        sel = slice(2 * k, 2 * k + 2)
        return _gather_forward(items[sel], fulls[sel], ssems[sel], rsems[sel], after, f"gather_forward_{k}")

    def block_weights(k, after):
        sel = slice(2 * k, 2 * k + 2)
        got, fsems, gsems = early.pop(k) if k in early else forward(k, after)
        done = _gather_finish(items[sel], got, fsems, gsems, f"gather_finish_{k}")
        ahead = ()
        if k + 1 == 5:
            early[k + 1] = forward(k + 1, done[0])
            ahead = (early[k + 1][0][0],)
        return [a.reshape(mat_shapes[it.src]) for a, it in zip(done, items[sel])], ahead

    res = {}
    flight = []

    def advance(entry, k, dx):
        stage, its, st = entry
        n = len(its)
        if stage == 1:
            ssem, rsem, arrs = st
            done = _exchange_wait(f"grads_pair_wait_{k}", arrs, ssem, rsem, _pair_copies(its), dx)
            sums = [_pair_add(it, g, r, scal) for it, g, r in zip(its, done[:n], done[n:])]
            land = [lax.empty((3,) + it.part3, BF16) for it in its]
            ssem, rsem, arrs, tok = _exchange_start(f"grads_chip_start_{k}", [p[1] for p in sums] + land, [3] * n, _chip_copies(its))
            return (2, its, (ssem, rsem, arrs, [p[0] for p in sums])), tok
        if stage == 2:
            ssem, rsem, arrs, p32 = st
            got = _exchange_wait(f"grads_chip_wait_{k}", arrs, ssem, rsem, _chip_copies(its), dx)[n:]
            red = [_chip_reduce(it, p, r, scal) for it, p, r in zip(its, p32, got)]
            ssem, rsem, arrs, tok = _exchange_start(f"grads_fill_start_{k}", red, [1] * n, _fill_copies(its))
            return (3, its, (ssem, rsem, arrs)), tok
        ssem, rsem, arrs = st
        for it, g in zip(its, _exchange_wait(f"grads_fill_wait_{k}", arrs, ssem, rsem, _fill_copies(its), dx)):
            nm = it.src
            res[nm] = _adam(it, g, as_shards(w[nm], nm), as_shards(m[nm], nm), as_shards(v[nm], nm), prev=res.get(nm))
        return None, None

    def step_flight(dx):
        toks, left = [], []
        for k, entry in flight:
            entry, tok = advance(entry, k, dx)
            if entry is not None:
                left.append((k, entry))
                toks.append(tok)
        flight[:] = left
        return toks

    def grads_ready(k, pairs, dx):
        its = items[2 * k:2 * k + 2]
        toks = step_flight((dx,))
        g16 = [p.reshape(it.full3) for p, it in zip(pairs, its)]
        land = [lax.empty(it.pair3, BF16) for it in its]
        ssem, rsem, arrs, tok = _exchange_start(f"grads_pair_start_{k}", g16 + land, [1] * len(its), _pair_copies(its))
        flight.append((k, (1, its, (ssem, rsem, arrs))))
        return toks + [tok]

    sq, dx, grads = _local_step(x[0], loss_target[0], norm_full.reshape(2, 3, D), conv_b_pw1, dw_full, conv_b_dw, conv_norm_g,
                                conv_b_pw2, attn_q_norm[0], attn_k_norm[0], block_weights, grads_ready,
                                lambda dx: step_flight((dx,)))
    loss = lax.psum(0.5 * jnp.sum(sq) / D, ("x", "y", "c"))
    k_last, entry = flight.pop()
    entry, tok = advance(entry, k_last, (dx,))
    while flight:
        step_flight((dx, tok))

    out_g, out_d, out_m, out_v = {}, {}, {}, {}

    parts = [_rows8(grads[n], D) for n in _SMALL]
    tot = _allreduce_small(jnp.concatenate(parts, axis=0))
    sg, r0 = {}, 0
    for n, p in zip(_SMALL, parts):
        last = grads[n].shape[-1]
        g = tot[r0:r0 + grads[n].size // min(last, D), :min(last, D)].reshape(-1, last)
        r0 += p.shape[0]
        if n in ("norm_g", "conv_w_dw"):
            g = lax.dynamic_slice_in_dim(g, me * (D // 4), D // 4, axis=1)
        sg[n] = g
    flat = lambda a: a.reshape(-1, a.shape[-1])
    ds, ms, vs = _adam_small([sg[n] for n in _SMALL], [flat(w[n]) for n in _SMALL], [flat(m[n]) for n in _SMALL],
                             [flat(v[n]) for n in _SMALL])
    for i, n in enumerate(_SMALL):
        out_g[n], out_d[n], out_m[n], out_v[n] = (a.reshape(w[n].shape) for a in (sg[n], ds[i], ms[i], vs[i]))

    behind = tuple(r[1] for r in res.values()) + tuple(ds)
    while entry is not None:
        entry, _ = advance(entry, k_last, behind)
    for n in _BIG:
        out_g[n], out_d[n], out_m[n], out_v[n] = (a.reshape(w[n].shape) for a in res[n])

    return (loss, dx[None], *[out_g[n] for n in _NAMES], *[out_d[n] for n in _NAMES], *[out_m[n] for n in _NAMES],
            *[out_v[n] for n in _NAMES])
```

```python
import functools

import jax
import jax.numpy as jnp
from jax import lax
from jax.experimental import pallas as pl
from jax.experimental.pallas import tpu as pltpu

F32 = jnp.float32
BF16 = jnp.bfloat16
MESH = pl.DeviceIdType.MESH

NORM_EPS = 1e-6
CONV_WIDTH = 31
ATTN_GROUPS = ((128, 1), (512, 4), (2048, 16))
ATTN_BLOCK = 128
HEAD_DIM = 128
N_CHIPS = 4

ADAM_LR = 0.001
ADAM_B1 = 0.9
ADAM_B2 = 0.999
ADAM_EPS = 1e-08
ADAM_WD = 0.01
ADAM_STEP = 10

VMEM_LIMIT = 56 * 1024 * 1024
NT_DIMS = (((1,), (1,)), ((), ()))
TN_DIMS = (((0,), (0,)), ((), ()))


def _pick(n, pref, mult):
    t = (min(n, pref) // mult) * mult
    while t >= mult:
        if n % t == 0:
            return t
        t -= mult
    return n


def _call(body, name, grid, in_specs, out_specs, out_shape, scratch=(), aliases=None, prefetch=0, after=()):
    params = pltpu.CompilerParams(dimension_semantics=("arbitrary",) * len(grid), vmem_limit_bytes=VMEM_LIMIT)
    after = tuple(after)
    if after:
        inner, n_in = body, prefetch + len(in_specs)

        def body(*refs):
            return inner(*refs[:n_in], *refs[n_in + len(after):])

        in_specs = list(in_specs) + [pl.BlockSpec(memory_space=pl.ANY)] * len(after)
    if prefetch:
        spec = pltpu.PrefetchScalarGridSpec(
            num_scalar_prefetch=prefetch, grid=grid, in_specs=in_specs, out_specs=out_specs, scratch_shapes=list(scratch)
        )
        call = pl.pallas_call(body, name=name, grid_spec=spec, out_shape=out_shape, compiler_params=params,
                              input_output_aliases=aliases or {})
    else:
        call = pl.pallas_call(body, name=name, grid=grid, in_specs=in_specs, out_specs=out_specs, out_shape=out_shape,
                              scratch_shapes=list(scratch), compiler_params=params, input_output_aliases=aliases or {})
    return lambda *args: call(*args, *after)


def _sds(shape, dtype):
    return jax.ShapeDtypeStruct(shape, dtype)


def _sig(x):
    return 1.0 / (1.0 + jnp.exp(-x))


def _rstd(x):
    return lax.rsqrt(jnp.mean(x * x, axis=-1, keepdims=True) + NORM_EPS)


def _norm_bwd(dy, xhat, r, g):
    dxh = dy * g
    return r * (dxh - xhat * jnp.mean(dxh * xhat, axis=-1, keepdims=True))


def _dot(a, b):
    return jnp.dot(a, b, preferred_element_type=F32)


def _dot_nt(a, b):
    return lax.dot_general(a, b, NT_DIMS, preferred_element_type=F32)


def _dot_tn(a, b):
    return lax.dot_general(a, b, TN_DIMS, preferred_element_type=F32)


def _ffn_fwd(x, g, w_in, w_out, name, target=None):
    T, D = x.shape
    F = w_out.shape[0]
    tm = _pick(T, 256, 8)
    last = target is not None

    def body(x_ref, g_ref, wg_ref, wu_ref, wo_ref, *rest):
        xo_ref, gate_ref, up_ref = rest[last:last + 3]
        xv = x_ref[...]
        h = (xv * _rstd(xv) * g_ref[...]).astype(BF16)
        gate = _dot(h, wg_ref[...])
        up = _dot(h, wu_ref[...])
        gate_ref[...] = gate.astype(BF16)
        up_ref[...] = up.astype(BF16)
        a = (gate * _sig(gate) * up).astype(BF16)
        y = xv + 0.5 * _dot(a, wo_ref[...])
        if not last:
            xo_ref[...] = y
            return
        t_ref, sq_ref = rest[0], rest[4]

        @pl.when(pl.program_id(0) == 0)
        def _():
            sq_ref[...] = jnp.zeros_like(sq_ref)

        err = y - t_ref[...]
        xo_ref[...] = err * (1.0 / D)
        sq_ref[...] += jnp.sum(err * err, axis=0, keepdims=True)

    row = lambda i: (i, 0)
    fix = pl.BlockSpec((1, D), lambda i: (0, 0))
    held = lambda shape, k: pl.BlockSpec(shape, lambda i: (0, k), pipeline_mode=pl.Buffered(1))
    return _call(
        body, name, (T // tm,),
        [pl.BlockSpec((tm, D), row), fix, held((D, F), 0), held((D, F), 1), held((F, D), 0)] + [pl.BlockSpec((tm, D), row)] * last,
        [pl.BlockSpec((tm, D), row), pl.BlockSpec((tm, F), row), pl.BlockSpec((tm, F), row)] + [fix] * last,
        [_sds((T, D), F32), _sds((T, F), BF16), _sds((T, F), BF16)] + [_sds((1, D), F32)] * last,
    )(x, g, w_in, w_in, w_out, *([target] if last else []))


def _ffn_bwd(x, g, dy, gate, up, w_in, w_out, name, after=()):
    T, D = x.shape
    F = w_out.shape[0]

    def body_a(dy_ref, gate_ref, up_ref, wo_ref, dgate_ref, dup_ref, a_ref):
        dyb = (0.5 * dy_ref[...]).astype(BF16)
        gate = gate_ref[...].astype(F32)
        up = up_ref[...].astype(F32)
        sg = _sig(gate)
        sl = gate * sg
        a_ref[...] = (sl * up).astype(BF16)
        da = _dot_nt(dyb, wo_ref[...])
        dgate_ref[...] = (da * up * (sg * (1.0 + gate * (1.0 - sg)))).astype(BF16)
        dup_ref[...] = (da * sl).astype(BF16)

    ta = _pick(T, 256, 8)
    tile = pl.BlockSpec((ta, F), lambda i: (i, 0))
    dgate, dup, a = _call(
        body_a, name + "_hidden", (T // ta,),
        [pl.BlockSpec((ta, D), lambda i: (i, 0)), tile, tile,
         pl.BlockSpec((F, D), lambda i: (0, 0), pipeline_mode=pl.Buffered(1))],
        [tile, tile, tile],
        [_sds((T, F), BF16)] * 3, after=after,
    )(dy, gate, up, w_out)

    def body_b(x_ref, g_ref, dy_ref, dgate_ref, dup_ref, wg_ref, wu_ref, dx_ref, dg_ref, h_ref, dyb_ref):
        @pl.when(pl.program_id(0) == 0)
        def _():
            dg_ref[...] = jnp.zeros_like(dg_ref)

        xv = x_ref[...]
        r = _rstd(xv)
        xh = xv * r
        h_ref[...] = (xh * g_ref[...]).astype(BF16)
        dyb_ref[...] = (0.5 * dy_ref[...]).astype(BF16)
        dh = _dot_nt(dgate_ref[...], wg_ref[...]) + _dot_nt(dup_ref[...], wu_ref[...])
        dx_ref[...] = dy_ref[...] + _norm_bwd(dh, xh, r, g_ref[...])
        dg_ref[...] += jnp.sum(dh * xh, axis=0, keepdims=True)

    tb = _pick(T, 512, 8)
    row = lambda i: (i, 0)
    held = lambda k: pl.BlockSpec((D, F), lambda i: (0, k), pipeline_mode=pl.Buffered(1))
    dx, dg, h, dyb = _call(
        body_b, name, (T // tb,),
        [pl.BlockSpec((tb, D), row), pl.BlockSpec((1, D), lambda i: (0, 0)), pl.BlockSpec((tb, D), row),
         pl.BlockSpec((tb, F), row), pl.BlockSpec((tb, F), row), held(0), held(1)],
        [pl.BlockSpec((tb, D), row), pl.BlockSpec((1, D), lambda i: (0, 0)), pl.BlockSpec((tb, D), row), pl.BlockSpec((tb, D), row)],
        [_sds((T, D), F32), _sds((1, D), F32), _sds((T, D), BF16), _sds((T, D), BF16)],
    )(x, g, dy, dgate, dup, w_in, w_in)
    return dx, dg, dgate, dup, a, h, dyb


def _mm_tn(a, b, bm, bn, out_shape, out_block, out_map, scale, name):
    K, M = a.shape
    N = b.shape[1]

    def body(a_ref, b_ref, o_ref):
        o_ref[...] = (_dot_tn(a_ref[...].astype(BF16), b_ref[...].astype(BF16)) * scale).astype(BF16)

    in_specs = [pl.BlockSpec((K, bm), lambda mi, ni: (0, mi)), pl.BlockSpec((K, bn), lambda mi, ni: (0, ni))]
    return _call(body, name, (M // bm, N // bn), in_specs, [pl.BlockSpec(out_block, out_map)],
                 [_sds(out_shape, BF16)])(a, b)[0]


def _mm_tn_parts(a, bs, bn, cols, col0, name, prev=None, after=()):
    K, M = a.shape
    N = bs[0].shape[1]
    nt = N // bn
    ns = len(bs)

    def body(a_ref, *refs):
        o_ref = refs[-1]
        part = pl.program_id(0) // nt
        for s in range(ns):
            @pl.when(part == s)
            def _(s=s):
                o_ref[...] = _dot_tn(a_ref[...], refs[s][...]).astype(BF16)

    part_spec = lambda s: pl.BlockSpec((K, bn), lambda j: (0, jnp.clip(j - s * nt, 0, nt - 1)))
    in_specs = [pl.BlockSpec((K, M), lambda j: (0, 0))] + [part_spec(s) for s in range(ns)]
    args = [a, *bs]
    aliases = None
    if prev is not None:
        in_specs += [pl.BlockSpec(memory_space=pl.ANY)]
        args += [prev]
        aliases = {ns + 1: 0}
    return _call(body, name, (ns * nt,), in_specs, [pl.BlockSpec((M, bn), lambda j: (0, col0 // bn + j))],
                 [_sds((M, cols), BF16)], aliases=aliases, after=after)(*args)[0]


def _conv_pre(x, g, w1, b1, name):
    T, D = x.shape
    tm = _pick(T, 512, 8)

    def body(x_ref, g_ref, w_ref, b_ref, ag_ref, u_ref, h_ref):
        xv = x_ref[...]
        h = (xv * _rstd(xv) * g_ref[...]).astype(BF16)
        h_ref[...] = h
        ag = _dot(h, w_ref[...]) + b_ref[...]
        ag_ref[...] = ag.astype(BF16)
        u_ref[...] = ag[:, :D] * _sig(ag[:, D:])

    return _call(
        body, name, (T // tm,),
        [pl.BlockSpec((tm, D), lambda i: (i, 0)), pl.BlockSpec((1, D), lambda i: (0, 0)),
         pl.BlockSpec((D, 2 * D), lambda i: (0, 0)), pl.BlockSpec((1, 2 * D), lambda i: (0, 0))],
        [pl.BlockSpec((tm, 2 * D), lambda i: (i, 0)), pl.BlockSpec((tm, D), lambda i: (i, 0)),
         pl.BlockSpec((tm, D), lambda i: (i, 0))],
        [_sds((T, 2 * D), BF16), _sds((T, D), F32), _sds((T, D), BF16)],
    )(x, g, w1, b1)


_DW_PAD = 32
_DW_CHUNK = 256


def _dwconv(u, w, b, name):
    T, D = u.shape
    K = w.shape[0]
    ch = _pick(T, _DW_CHUNK, 8)
    lead = _DW_PAD - (K - 1)

    def body(u_ref, w_ref, b_ref, c_ref, ext):
        ext[pl.ds(0, _DW_PAD), :] = jnp.zeros((_DW_PAD, 128), F32)
        ext[pl.ds(_DW_PAD, T), :] = u_ref[...]
        for c0 in range(0, T, ch):
            acc = jnp.zeros((ch, 128), F32) + b_ref[...]
            for k in range(K):
                acc = acc + w_ref[pl.ds(k, 1), :] * ext[pl.ds(c0 + lead + k, ch), :]
            c_ref[pl.ds(c0, ch), :] = acc

    return _call(
        body, name, (D // 128,),
        [pl.BlockSpec((T, 128), lambda i: (0, i)), pl.BlockSpec((K, 128), lambda i: (0, i)),
         pl.BlockSpec((1, 128), lambda i: (0, i))],
        [pl.BlockSpec((T, 128), lambda i: (0, i))],
        [_sds((T, D), F32)],
        scratch=[pltpu.VMEM((T + _DW_PAD, 128), F32)],
    )(u, w, b)[0]


def _dwconv_bwd(dc, u, w, name):
    T, D = u.shape
    K = w.shape[0]
    ch = _pick(T, _DW_CHUNK, 8)

    def body(dc_ref, u_ref, w_ref, du_ref, dw_ref, db_ref, dext):
        dext[pl.ds(0, T), :] = dc_ref[...]
        dext[pl.ds(T, _DW_PAD), :] = jnp.zeros((_DW_PAD, 128), F32)
        dws =[jnp.zeros((8, 128), F32) for _ in range(K)]
        dbs = jnp.zeros((8, 128), F32)
        for c0 in range(0, T, ch):
            uv = u_ref[pl.ds(c0, ch), :]
            dbs = dbs + jnp.sum(dc_ref[pl.ds(c0, ch), :].reshape(ch // 8, 8, 128), axis=0)
            acc = jnp.zeros((ch, 128), F32)
            for k in range(K):
                win = dext[pl.ds(c0 + (K - 1) - k, ch), :]
                acc = acc + w_ref[pl.ds(k, 1), :] * win
                dws[k] = dws[k] + jnp.sum((win * uv).reshape(ch // 8, 8, 128), axis=0)
            du_ref[pl.ds(c0, ch), :] = acc
        for k in range(K):
            dw_ref[pl.ds(k, 1), :] = jnp.sum(dws[k], axis=0, keepdims=True)
        db_ref[...] = jnp.sum(dbs, axis=0, keepdims=True)

    return _call(
        body, name, (D // 128,),
        [pl.BlockSpec((T, 128), lambda i: (0, i)), pl.BlockSpec((T, 128), lambda i: (0, i)),
         pl.BlockSpec((K, 128), lambda i: (0, i))],
        [pl.BlockSpec((T, 128), lambda i: (0, i)), pl.BlockSpec((K, 128), lambda i: (0, i)),
         pl.BlockSpec((1, 128), lambda i: (0, i))],
        [_sds((T, D), F32), _sds((K, D), F32), _sds((1, D), F32)],
        scratch=[pltpu.VMEM((T + _DW_PAD, 128), F32)],
    )(dc, u, w)


def _conv_post(c, x, ng, w2, b2, name):
    T, D = x.shape
    tm = _pick(T, 512, 8)

    def body(c_ref, x_ref, ng_ref, w_ref, b_ref, xo_ref, s_ref):
        cv = c_ref[...]
        n = cv * _rstd(cv) * ng_ref[...]
        s = (n * _sig(n)).astype(BF16)
        s_ref[...] = s
        xo_ref[...] = x_ref[...] + _dot(s, w_ref[...]) + b_ref[...]

    row = lambda i: (i, 0)
    fix = lambda i: (0, 0)
    return _call(
        body, name, (T // tm,),
        [pl.BlockSpec((tm, D), row), pl.BlockSpec((tm, D), row), pl.BlockSpec((1, D), fix),
         pl.BlockSpec((D, D), fix), pl.BlockSpec((1, D), fix)],
        [pl.BlockSpec((tm, D), row), pl.BlockSpec((tm, D), row)],
        [_sds((T, D), F32), _sds((T, D), BF16)],
    )(c, x, ng, w2, b2)


def _conv_post_bwd(dy, c, ng, w2, name, after=()):
    T, D = dy.shape
    tm = _pick(T, 512, 8)

    def body(dy_ref, c_ref, ng_ref, w_ref, dc_ref, dng_ref, db_ref):
        @pl.when(pl.program_id(0) == 0)
        def _():
            dng_ref[...] = jnp.zeros_like(dng_ref)
            db_ref[...] = jnp.zeros_like(db_ref)

        dyv = dy_ref[...]
        ds = _dot_nt(dyv.astype(BF16), w_ref[...])
        cv = c_ref[...]
        r = _rstd(cv)
        ch = cv * r
        n = ch * ng_ref[...]
        sg = _sig(n)
        dn = ds * (sg * (1.0 + n * (1.0 - sg)))
        dc_ref[...] = _norm_bwd(dn, ch, r, ng_ref[...])
        dng_ref[...] += jnp.sum(dn * ch, axis=0, keepdims=True)
        db_ref[...] += jnp.sum(dyv, axis=0, keepdims=True)

    row = lambda i: (i, 0)
    fix = lambda i: (0, 0)
    return _call(
        body, name, (T // tm,),
        [pl.BlockSpec((tm, D), row), pl.BlockSpec((tm, D), row), pl.BlockSpec((1, D), fix), pl.BlockSpec((D, D), fix)],
        [pl.BlockSpec((tm, D), row), pl.BlockSpec((1, D), fix), pl.BlockSpec((1, D), fix)],
        [_sds((T, D), F32), _sds((1, D), F32), _sds((1, D), F32)], after=after,
    )(dy, c, ng, w2)


def _conv_pre_bwd(du, ag, x, g, dy, w1, name):
    T, D = x.shape
    tm = _pick(T, 512, 8)

    def body(du_ref, ag_ref, x_ref, g_ref, dy_ref, w_ref, dx_ref, dg_ref, dag_ref, db_ref):
        @pl.when(pl.program_id(0) == 0)
        def _():
            dg_ref[...] = jnp.zeros_like(dg_ref)
            db_ref[...] = jnp.zeros_like(db_ref)

        duv = du_ref[...]
        a = ag_ref[:, :D].astype(F32)
        gt = ag_ref[:, D:].astype(F32)
        sg = _sig(gt)
        da = duv * sg
        dgt = duv * a * sg * (1.0 - sg)
        db_ref[:, :D] += jnp.sum(da, axis=0, keepdims=True)
        db_ref[:, D:] += jnp.sum(dgt, axis=0, keepdims=True)
        dab = da.astype(BF16)
        dgb = dgt.astype(BF16)
        dag_ref[:, :D] = dab
        dag_ref[:, D:] = dgb
        dh = _dot_nt(dab, w_ref[:, :D]) + _dot_nt(dgb, w_ref[:, D:])
        xv = x_ref[...]
        r = _rstd(xv)
        xh = xv * r
        dx_ref[...] = dy_ref[...] + _norm_bwd(dh, xh, r, g_ref[...])
        dg_ref[...] += jnp.sum(dh * xh, axis=0, keepdims=True)

    row = lambda i: (i, 0)
    fix = lambda i: (0, 0)
    return _call(
        body, name, (T // tm,),
        [pl.BlockSpec((tm, D), row), pl.BlockSpec((tm, 2 * D), row), pl.BlockSpec((tm, D), row), pl.BlockSpec((1, D), fix),
         pl.BlockSpec((tm, D), row), pl.BlockSpec((D, 2 * D), fix)],
        [pl.BlockSpec((tm, D), row), pl.BlockSpec((1, D), fix), pl.BlockSpec((tm, 2 * D), row),
         pl.BlockSpec((1, 2 * D), fix)],
        [_sds((T, D), F32), _sds((1, D), F32), _sds((T, 2 * D), BF16), _sds((1, 2 * D), F32)],
    )(du, ag, x, g, dy, w1)


def _row_sums(a):
    return _dot(a.astype(BF16), jnp.ones((a.shape[1], 128), BF16))


def _head_rstd(x):
    return lax.rsqrt(_row_sums(x * x) * (1.0 / x.shape[1]) + NORM_EPS)


def _attn_qkv(x, g, wqkv, qn, kn, name):
    T, D = x.shape
    N = wqkv.shape[1]
    tn = N // 9
    E = HEAD_DIM
    tm = _pick(T, 256, 8)
    ng = qn.shape[0]

    def body(x_ref, g_ref, w_ref, qn_ref, kn_ref, o_ref, a_ref, h_ref):
        xv = x_ref[...]
        h = (xv * _rstd(xv) * g_ref[...]).astype(BF16)
        h_ref[...] = h
        for j in range(9):
            cs = slice(j * tn, (j + 1) * tn)
            res = _dot(h, w_ref[:, cs])
            o_ref[:, cs] = res.astype(BF16)
            if j % 3 == 2:
                a_ref[:, cs] = res.astype(BF16)
                continue
            grp = j // 3
            fac = qn_ref[grp:grp + 1, :] * kn_ref[grp:grp + 1, :] * (E ** -0.5) if j % 3 == 0 else None
            for h_i in range(tn // E):
                hs = slice(h_i * E, (h_i + 1) * E)
                xh = res[:, hs]
                hat = xh * _head_rstd(xh)
                a_ref[:, j * tn + h_i * E:j * tn + (h_i + 1) * E] = (hat if fac is None else hat * fac).astype(BF16)

    row = lambda i: (i, 0)
    fix = lambda i: (0, 0)
    return _call(
        body, name, (T // tm,),
        [pl.BlockSpec((tm, D), row), pl.BlockSpec((1, D), fix), pl.BlockSpec((D, N), fix, pipeline_mode=pl.Buffered(1)),
         pl.BlockSpec((ng, E), fix), pl.BlockSpec((ng, E), fix)],
        [pl.BlockSpec((tm, N), row), pl.BlockSpec((tm, N), row), pl.BlockSpec((tm, D), row)],
        [_sds((T, N), BF16), _sds((T, N), BF16), _sds((T, D), BF16)],
    )(x, g, wqkv, qn, kn)


def _band_mask(q, steps, nblk):
    i = lax.broadcasted_iota(jnp.int32, (q, 2 * q), 0)
    j = lax.broadcasted_iota(jnp.int32, (q, 2 * q), 1)
    diff = q + i - j
    first_key = jnp.where(nblk > 0, 0, q)
    return (diff >= 0) & (diff <= steps) & (j >= first_key)


def _per_row(blk, width):
    e = blk.shape[1]
    if width % e == 0:
        return jnp.concatenate([blk] * (width // e), axis=1)
    return jnp.broadcast_to(blk[:, :1], (blk.shape[0], width))


def _streams(a, dil, to_streams, name, col0=0, ncols=None):
    T, C = a.shape
    ncols = C if ncols is None else ncols
    Q = ATTN_BLOCK
    run = Q * dil
    reps = max(1, min(2048 // run, T // run))
    while T % (run * reps):
        reps -= 1
    rows = run * reps
    cw = _pick(ncols, 512, 128)
    ns = cw // 128

    def body(a_ref, o_ref, scr):
        for s in range(ns):
            ls = slice(s * 128, (s + 1) * 128)
            slab = scr.at[s]
            if to_streams:
                slab[...] = a_ref[:, ls].astype(F32)
                for u in range(reps):
                    for r in range(dil):
                        o_ref[pl.ds(u * run + r * Q, Q), ls] = slab[pl.ds(u * run + r, Q, stride=dil), :].astype(a.dtype)
            else:
                for u in range(reps):
                    for r in range(dil):
                        slab[pl.ds(u * run + r, Q, stride=dil), :] = a_ref[pl.ds(u * run + r * Q, Q), ls].astype(F32)
                o_ref[:, ls] = slab[...].astype(a.dtype)

    return _call(
        body, name, (T // rows, ncols // cw),
        [pl.BlockSpec((rows, cw), lambda i, j: (i, col0 // cw + j))],
        [pl.BlockSpec((rows, cw), lambda i, j: (i, j))],
        [_sds((T, ncols), a.dtype)],
        scratch=[pltpu.VMEM((ns, rows, 128), F32)],
    )(a)[0]


def _attn_fwd(qkv, base, HE, window, dil, name):
    T = qkv.shape[0]
    H = HE // HEAD_DIM
    E = HEAD_DIM
    Q = ATTN_BLOCK
    nb = T // dil // Q
    steps = window // dil

    def body(q_ref, kc_ref, kp_ref, vc_ref, vp_ref, o_ref, l_ref):
        n = pl.program_id(1)
        valid = _band_mask(Q, steps, n)
        ones = jnp.ones((2 * Q, E), BF16)
        outs, lses = [], []
        for h in range(H):
            hs = slice(h * E, (h + 1) * E)
            k2 = jnp.concatenate([kp_ref[:, hs], kc_ref[:, hs]], axis=0)
            v2 = jnp.concatenate([vp_ref[:, hs], vc_ref[:, hs]], axis=0)
            s = jnp.where(valid, _dot_nt(q_ref[:, hs], k2), -1e30)
            m = jnp.max(s, axis=-1, keepdims=True)
            p = jnp.exp(s - m).astype(BF16)
            acc = _dot(p, jnp.concatenate([v2, ones], axis=1))
            l = acc[:, E:]
            outs.append((acc[:, :E] * (1.0 / l)).astype(o_ref.dtype))
            lses.append(m + jnp.log(l))
        o_ref[...] = jnp.concatenate(outs, axis=1)
        l_ref[...] = jnp.concatenate(lses, axis=1)

    blk = lambda s, back: pl.BlockSpec((Q, HE), lambda r, n: (jnp.maximum(n - back, 0) * dil + r, base + s))
    out = pl.BlockSpec((Q, HE), lambda r, n: (n * dil + r, 0))
    return _call(
        body, name, (dil, nb),
        [blk(0, 0), blk(1, 0), blk(1, 1), blk(2, 0), blk(2, 1)],
        [out, out],
        [_sds((T, HE), BF16), _sds((T, HE), F32)],
    )(qkv, qkv, qkv, qkv, qkv)


def _attn_merge(os, lses, x, wo, name):
    T, D = x.shape
    HE = wo.shape[0]
    tm = _pick(T, 512, 8)
    ng = len(os)

    def body(*refs):
        o_refs = refs[:ng]
        l_refs = refs[ng:2 * ng]
        x_ref, w_ref, xo_ref, om_ref, lt_ref = refs[2 * ng:]
        ls = [r[...] for r in l_refs]
        m = functools.reduce(jnp.maximum, ls)
        es = [jnp.exp(l - m) for l in ls]
        tot = functools.reduce(lambda a, b: a + b, es)
        inv = 1.0 / tot
        om = functools.reduce(lambda a, b: a + b, [e * inv * r[...] for e, r in zip(es, o_refs)])
        omb = om.astype(BF16)
        om_ref[...] = omb
        lt_ref[...] = m + jnp.log(tot)
        xo_ref[...] = x_ref[...] + _dot(omb, w_ref[...])

    row = lambda i: (i, 0)
    fix = lambda i: (0, 0)
    return _call(
        body, name, (T // tm,),
        [pl.BlockSpec((tm, HE), row)] * (2 * ng) + [pl.BlockSpec((tm, D), row), pl.BlockSpec((HE, D), fix)],
        [pl.BlockSpec((tm, D), row), pl.BlockSpec((tm, HE), row), pl.BlockSpec((tm, HE), row)],
        [_sds((T, D), F32), _sds((T, HE), BF16), _sds((T, HE), F32)],
    )(*os, *lses, x, wo)


def _attn_out_bwd(dy, om, wo, name, after=()):
    T, D = dy.shape
    HE = wo.shape[0]
    E = HEAD_DIM
    tm = _pick(T, 512, 8)

    def body(dy_ref, om_ref, w_ref, dom_ref, dl_ref):
        dom = _dot_nt(dy_ref[...].astype(BF16), w_ref[...])
        dom_ref[...] = dom.astype(BF16)
        prod = dom * om_ref[...].astype(F32)
        for h in range(HE // E):
            hs = slice(h * E, (h + 1) * E)
            dl_ref[:, hs] = jnp.broadcast_to(jnp.sum(prod[:, hs], axis=-1, keepdims=True), (tm, E))

    row = lambda i: (i, 0)
    return _call(
        body, name, (T // tm,),
        [pl.BlockSpec((tm, D), row), pl.BlockSpec((tm, HE), row), pl.BlockSpec((HE, D), lambda i: (0, 0))],
        [pl.BlockSpec((tm, HE), row), pl.BlockSpec((tm, HE), row)],
        [_sds((T, HE), BF16), _sds((T, HE), F32)], after=after,
    )(dy, om, wo)


def _attn_bwd(qkv, base, HE, dom, lse, delta, window, dil, name):
    T = qkv.shape[0]
    H = HE // HEAD_DIM
    E = HEAD_DIM
    Q = ATTN_BLOCK
    nb = T // dil // Q
    steps = window // dil

    def body(q_ref, kc_ref, kp_ref, vc_ref, vp_ref, do_ref, l_ref, dl_ref, dq_ref, dk_ref, dv_ref, ck_sc, cv_sc):
        n = pl.program_id(1)

        @pl.when(n == 0)
        def _():
            ck_sc[...] = jnp.zeros_like(ck_sc)
            cv_sc[...] = jnp.zeros_like(cv_sc)

        @pl.when(n < nb)
        def _():
            valid = _band_mask(Q, steps, n)
            ck_old = ck_sc[...]
            cv_old = cv_sc[...]
            dqs, dks, dvs = [], [], []
            for h in range(H):
                hs = slice(h * E, (h + 1) * E)
                q = q_ref[:, hs]
                do = do_ref[:, hs]
                k2 = jnp.concatenate([kp_ref[:, hs], kc_ref[:, hs]], axis=0)
                v2 = jnp.concatenate([vp_ref[:, hs], vc_ref[:, hs]], axis=0)
                p = jnp.where(valid, jnp.exp(_dot_nt(q, k2) - _per_row(l_ref[:, hs], 2 * Q)), 0.0)
                ds = (p * (_dot_nt(do, v2) - _per_row(dl_ref[:, hs], 2 * Q))).astype(BF16)
                dqs.append(_dot(ds, k2).astype(BF16))
                dks.append(_dot_tn(q, ds).T)
                dvs.append(_dot_tn(do, p.astype(BF16)).T)
            cat = lambda parts: jnp.concatenate(parts, axis=1)
            dq_ref[...] = cat(dqs)
            dk_ref[...] = (ck_old + cat([d[:Q] for d in dks])).astype(BF16)
            dv_ref[...] = (cv_old + cat([d[:Q] for d in dvs])).astype(BF16)
            ck_sc[...] = cat([d[Q:] for d in dks])
            cv_sc[...] = cat([d[Q:] for d in dvs])

        @pl.when(n == nb)
        def _():
            dk_ref[...] = ck_sc[...].astype(BF16)
            dv_ref[...] = cv_sc[...].astype(BF16)

    nq = lambda n: jnp.minimum(n, nb - 1)
    blk = lambda s, back: pl.BlockSpec((Q, HE), lambda r, n: (jnp.maximum(nq(n) - back, 0) * dil + r, base + s))
    qblk = pl.BlockSpec((Q, HE), lambda r, n: (nq(n) * dil + r, 0))
    kblk = pl.BlockSpec((Q, HE), lambda r, n: (jnp.maximum(n - 1, 0) * dil + r, 0))
    return _call(
        body, name, (dil, nb + 1),
        [blk(0, 0), blk(1, 0), blk(1, 1), blk(2, 0), blk(2, 1), qblk, qblk, qblk],
        [qblk, kblk, kblk],
        [_sds((T, HE), BF16)] * 3,
        scratch=[pltpu.VMEM((Q, HE), F32), pltpu.VMEM((Q, HE), F32)],
    )(qkv, qkv, qkv, qkv, qkv, dom, lse, delta)


def _qk_norm_bwd(dq, dk, qkv, base, HE, gq, gk, name):
    T = dq.shape[0]
    E = HEAD_DIM
    tm = _pick(T, 512, 8)
    scale = E ** -0.5

    def body(dq_ref, dk_ref, q_ref, k_ref, gq_ref, gk_ref, oq_ref, ok_ref, dgq_ref, dgk_ref):
        @pl.when(pl.program_id(0) == 0)
        def _():
            dgq_ref[...] = jnp.zeros_like(dgq_ref)
            dgk_ref[...] = jnp.zeros_like(dgk_ref)

        gqv = gq_ref[...]
        gkv = gk_ref[...]
        c = gqv * gkv * scale
        dc = jnp.zeros((1, E), F32)
        for h in range(HE // E):
            hs = slice(h * E, (h + 1) * E)
            q = q_ref[:, hs].astype(F32)
            rq = _head_rstd(q)
            qh = q * rq
            a = dq_ref[:, hs].astype(F32)
            dc = dc + jnp.sum(a * qh, axis=0, keepdims=True)
            dqh = a * c
            oq_ref[:, hs] = (rq * (dqh - qh * (_row_sums(dqh * qh) * (1.0 / E)))).astype(BF16)
            k = k_ref[:, hs].astype(F32)
            rk = _head_rstd(k)
            kh = k * rk
            b = dk_ref[:, hs].astype(F32)
            ok_ref[:, hs] = (rk * (b - kh * (_row_sums(b * kh) * (1.0 / E)))).astype(BF16)
        dgq_ref[...] += dc * (gkv * scale)
        dgk_ref[...] += dc * (gqv * scale)

    row = lambda i: (i, 0)
    vec = pl.BlockSpec((1, E), lambda i: (0, 0))
    return _call(
        body, name, (T // tm,),
        [pl.BlockSpec((tm, HE), row), pl.BlockSpec((tm, HE), row), pl.BlockSpec((tm, HE), lambda i: (i, base)),
         pl.BlockSpec((tm, HE), lambda i: (i, base + 1)), vec, vec],
        [pl.BlockSpec((tm, HE), row), pl.BlockSpec((tm, HE), row), vec, vec],
        [_sds((T, HE), BF16), _sds((T, HE), BF16), _sds((1, E), F32), _sds((1, E), F32)],
    )(dq, dk, qkv, qkv, gq, gk)


def _attn_qkv_bwd(dqkv, x, g, dy, wqkv, name):
    T, D = x.shape
    HE = wqkv.shape[1] // 9
    tm = _pick(T, 256, 8)

    def body(*refs):
        d_refs = refs[:9]
        x_ref, g_ref, dy_ref, w_ref, dx_ref, dg_ref = refs[9:]

        @pl.when(pl.program_id(0) == 0)
        def _():
            dg_ref[...] = jnp.zeros_like(dg_ref)

        dh = _dot_nt(d_refs[0][...], w_ref[:, :HE])
        for s in range(1, 9):
            dh = dh + _dot_nt(d_refs[s][...], w_ref[:, s * HE:(s + 1) * HE])
        xv = x_ref[...]
        r = _rstd(xv)
        xh = xv * r
        dx_ref[...] = dy_ref[...] + _norm_bwd(dh, xh, r, g_ref[...])
        dg_ref[...] += jnp.sum(dh * xh, axis=0, keepdims=True)

    row = lambda i: (i, 0)
    fix = lambda i: (0, 0)
    return _call(
        body, name, (T // tm,),
        [pl.BlockSpec((tm, HE), row)] * 9 + [pl.BlockSpec((tm, D), row), pl.BlockSpec((1, D), fix), pl.BlockSpec((tm, D), row),
                                           pl.BlockSpec((D, 9 * HE), fix, pipeline_mode=pl.Buffered(1))],
        [pl.BlockSpec((tm, D), row), pl.BlockSpec((1, D), fix)],
        [_sds((T, D), F32), _sds((1, D), F32)],
    )(*dqkv, x, g, dy, wqkv)


def _local_step(x, target, norm_g, b_pw1, w_dw, b_dw, cng, b_pw2, qn, kn, block_weights, grads_ready, grads_tick):
    T, D = x.shape
    ng = lambda l, k: norm_g[l, k][None, :]
    bn = _pick(D, 256, 128)

    w_in, w_out = [None] * 4, [None] * 4
    w_in[0], w_out[0] = block_weights(0, x)
    x1, *gu0 = _ffn_fwd(x, ng(0, 0), w_in[0], w_out[0], "ffn_fwd_0")
    pw1, pw2 = block_weights(1, x1)
    ag, u, hc = _conv_pre(x1, ng(0, 1), pw1, b_pw1, "conv_pre")
    c = _dwconv(u, w_dw, b_dw, "dwconv")
    x2, s = _conv_post(c, x1, cng, pw2, b_pw2, "conv_post")
    w_in[1], w_out[1] = block_weights(2, x2)
    x3, *gu1 = _ffn_fwd(x2, ng(0, 2), w_in[1], w_out[1], "ffn_fwd_1")
    w_in[2], w_out[2] = block_weights(3, x3)
    x4, *gu2 = _ffn_fwd(x3, ng(1, 0), w_in[2], w_out[2], "ffn_fwd_2")
    wqkv, wo = block_weights(4, x4)
    HE = wo.shape[0]
    F = w_out[0].shape[0]
    bf = _pick(F, 256, 128)
    bh = _pick(HE, 512, 128)
    qkv, att, ha = _attn_qkv(x4, ng(1, 1), wqkv, qn, kn, "attn_qkv")
    qkv_s = [(att, 3 * gi) if dil == 1 else (_streams(att, dil, True, f"qkv_streams_{gi}", 3 * gi * HE, 3 * HE), 0)
             for gi, (_, dil) in enumerate(ATTN_GROUPS)]
    tokens = lambda a, dil, name: a if dil == 1 else _streams(a, dil, False, name)
    streams = lambda a, dil, name: a if dil == 1 else _streams(a, dil, True, name)
    os, lses = [], []
    for gi, (window, dil) in enumerate(ATTN_GROUPS):
        o, l = _attn_fwd(*qkv_s[gi], HE, window, dil, f"attn_fwd_{gi}")
        os.append(tokens(o, dil, f"o_tokens_{gi}"))
        lses.append(tokens(l, dil, f"lse_tokens_{gi}"))
    x5, om, lse = _attn_merge(os, lses, x4, wo, "attn_merge")
    w_in[3], w_out[3] = block_weights(5, x5)
    dy, *gu3, sq = _ffn_fwd(x5, ng(1, 2), w_in[3], w_out[3], "ffn_fwd_3", target=target)

    grads = {"ffn_w_in": [None] * 4, "ffn_w_out": [None] * 4}
    dnorm = [[None] * 3 for _ in range(2)]

    def ffn_back(lf, k, xin, gvec, dy, gu, after):
        dx, dg, dgate, dup, a, h, dyb = _ffn_bwd(xin, gvec, dy, gu[0], gu[1], w_in[lf], w_out[lf], f"ffn_bwd_{lf}", after=after)
        grads["ffn_w_in"][lf] = _mm_tn_parts(h, [dgate, dup], bf, 2 * F, 0, f"ffn_dw_in_{lf}",
                                             after=grads_tick(dx) if lf == 0 else ())
        grads["ffn_w_out"][lf] = _mm_tn(a, dyb, bf, D, (F, D), (bf, D), lambda mi, ni: (mi, 0), 1.0, f"ffn_dw_out_{lf}")
        return dx, dg, grads_ready(k, (grads["ffn_w_in"][lf], grads["ffn_w_out"][lf]), dx)

    dx, dnorm[1][2], tok = ffn_back(3, 5, x5, ng(1, 2), dy, gu3, ())
    dom, delta = _attn_out_bwd(dx, om, wo, "attn_out_bwd", after=tok)
    grads["attn_w_o"] = _mm_tn(om, dx, HE, bn, (HE, D), (HE, bn), lambda mi, ni: (0, ni), 1.0, "attn_dw_o")
    dqkv, dgq, dgk = [], [], []
    for gi, (window, dil) in enumerate(ATTN_GROUPS):
        ds = _attn_bwd(*qkv_s[gi], HE, streams(dom, dil, f"dom_streams_{gi}"), streams(lse, dil, f"lse_streams_{gi}"),
                       streams(delta, dil, f"delta_streams_{gi}"), window, dil, f"attn_bwd_{gi}")
        dq, dk, dv = [tokens(d, dil, f"d{nm}_tokens_{gi}") for d, nm in zip(ds, "qkv")]
        dq, dk, a_, b_ = _qk_norm_bwd(dq, dk, qkv, 3 * gi, HE, qn[gi][None, :], kn[gi][None, :], f"qk_norm_bwd_{gi}")
        dqkv += [dq, dk, dv]
        dgq.append(a_)
        dgk.append(b_)
    grads["attn_q_norm"] = jnp.concatenate(dgq, axis=0)
    grads["attn_k_norm"] = jnp.concatenate(dgk, axis=0)
    d_qkv = None
    for gi in range(len(ATTN_GROUPS)):
        d_qkv = _mm_tn_parts(ha, dqkv[3 * gi:3 * gi + 3], bh, 9 * HE, 3 * gi * HE, f"attn_dw_qkv_{gi}", prev=d_qkv)
    grads["attn_w_qkv"] = d_qkv
    dx, dnorm[1][1] = _attn_qkv_bwd(dqkv, x4, ng(1, 1), dx, wqkv, "attn_qkv_bwd")
    tok = grads_ready(4, (grads["attn_w_qkv"], grads["attn_w_o"]), dx)
    dx, dnorm[1][0], tok = ffn_back(2, 3, x3, ng(1, 0), dx, gu2, tok)

    dx, dnorm[0][2], tok = ffn_back(1, 2, x2, ng(0, 2), dx, gu1, tok)
    dc, grads["conv_norm_g"], grads["conv_b_pw2"] = _conv_post_bwd(dx, c, cng, pw2, "conv_post_bwd", after=tok)
    grads["conv_w_pw2"] = _mm_tn(s, dx, D, bn, (D, D), (D, bn), lambda mi, ni: (0, ni), 1.0, "conv_dw_pw2")
    du, grads["conv_w_dw"], grads["conv_b_dw"] = _dwconv_bwd(dc, u, w_dw, "dwconv_bwd")
    dx, dnorm[0][1], dag, grads["conv_b_pw1"] = _conv_pre_bwd(du, ag, x1, ng(0, 1), dx, pw1, "conv_pre_bwd")
    grads["conv_w_pw1"] = _mm_tn(hc, dag, D, 2 * bn, (D, 2 * D), (D, 2 * bn), lambda mi, ni: (0, ni), 1.0, "conv_dw_pw1")
    tok = grads_ready(1, (grads["conv_w_pw1"], grads["conv_w_pw2"]), dx)
    dx, dnorm[0][0], _ = ffn_back(0, 0, x, ng(0, 0), dx, gu0, tok)

    grads["norm_g"] = jnp.concatenate([jnp.concatenate(r, axis=0)[None] for r in dnorm], axis=0)
    return sq, dx, grads


class _Sharded:
    def __init__(self, name, full3, half_axis, shard_axis, src, slab=None):
        self.name, self.full3, self.half_axis, self.shard_axis = name, tuple(full3), half_axis, shard_axis
        self.src, self.slab = src, slab

    def _cut(self, shape, axis, parts):
        s = list(shape)
        s[axis] //= parts
        return tuple(s)

    @property
    def shard3(self):
        return self._cut(self.full3, self.shard_axis, N_CHIPS)

    @property
    def pair3(self):
        return self._cut(self.full3, self.half_axis, 2)

    @property
    def part3(self):
        return self._cut(self.shard3, self.half_axis, 2)

    @staticmethod
    def _slice(ref, axis, idx, parts):
        n = ref.shape[axis] // parts
        start = idx * n
        minor = len(ref.shape) - 1 - axis
        if minor < 2 and not isinstance(start, int):
            start = pl.multiple_of(start, 128 if minor == 0 else (16 if n % 16 == 0 else 8))
        sl = [slice(None)] * len(ref.shape)
        sl[axis] = pl.ds(start, n)
        return ref.at[tuple(sl)]

    def half(self, ref, h):
        return self._slice(ref, self.half_axis, h, 2)

    def shard(self, ref, j):
        return self._slice(ref, self.shard_axis, j, N_CHIPS)


def _place():
    x, y, c = lax.axis_index("x"), lax.axis_index("y"), lax.axis_index("c")
    return x, y, c, 2 * x + y


_RELS = (1, 2, 3)


def _peer(x, y, rel):
    px = 1 - x if rel & 2 else x
    py = 1 - y if rel & 1 else y
    return px, py, 2 * px + py


ANY = pl.BlockSpec(memory_space=pl.ANY)


def _comm_call(body, name, n_in, out_shape, n_sems, aliases=None):
    return pl.pallas_call(
        body, name=name, in_specs=[ANY] * n_in, out_specs=[ANY] * len(out_shape), out_shape=out_shape,
        scratch_shapes=[pltpu.SemaphoreType.DMA((n,)) for n in n_sems],
        input_output_aliases=aliases or {},
        compiler_params=pltpu.CompilerParams(has_side_effects=True),
    )


def _remote(src, dst, send_sem, recv_sem, dev):
    return pltpu.make_async_remote_copy(src_ref=src, dst_ref=dst, send_sem=send_sem, recv_sem=recv_sem, device_id=dev,
                                        device_id_type=MESH)


def _gather_small(small_shards):
    ns = len(small_shards)

    def body(*refs):
        ins, outs = refs[:ns], refs[ns:2 * ns]
        lsem, ssem, rsem = refs[2 * ns:]
        x, y, c, me = _place()
        cols = lambda ref, j: _Sharded._slice(ref, 1, j, N_CHIPS)
        local = [pltpu.make_async_copy(ins[i], cols(outs[i], me), lsem.at[i]) for i in range(ns)]
        sends = []
        for i in range(ns):
            for k, rel in enumerate(_RELS):
                px, py, _ = _peer(x, y, rel)
                sends.append(_remote(ins[i], cols(outs[i], me), ssem.at[3 * i + k], rsem.at[3 * i + k], (px, py, c)))
        for cp in local + sends:
            cp.start()
        for i in range(ns):
            for k, rel in enumerate(_RELS):
                _, _, pj = _peer(x, y, rel)
                got = cols(outs[i], pj)
                _remote(got, got, ssem.at[3 * i + k], rsem.at[3 * i + k], (x, y, c)).wait_recv()
        for cp in sends:
            cp.wait_send()
        for cp in local:
            cp.wait()

    out_shape = [_sds((s.shape[0], s.shape[1] * N_CHIPS), F32) for s in small_shards]
    return _comm_call(body, "gather_small", ns, out_shape, [ns, 3 * ns, 3 * ns])(*small_shards)


HBM = pl.BlockSpec(memory_space=pltpu.HBM)
SEM = pl.BlockSpec(memory_space=pltpu.SEMAPHORE)
DATAFLOW = pltpu.SideEffectType.DATAFLOW_SIDE_EFFECTING


def _in_hbm(a):
    return pltpu.with_memory_space_constraint(a, pltpu.HBM)


def _cast_place(it, shard, scal, after=()):
    a_n, r_n, c_n = it.shard3
    tr = _pick(r_n, 256, 16)
    sa = it.shard_axis

    def body(sc_ref, s_ref, o_ref):
        o_ref[...] = s_ref[...].astype(BF16)

    if it.slab is None:
        src = pl.BlockSpec((1, tr, c_n), lambda a, rb, sc: (a, rb, 0))
    else:
        src = pl.BlockSpec((None, 1, tr, c_n), lambda a, rb, sc: (it.slab, a, rb, 0))
    dst = pl.BlockSpec((1, tr, c_n), lambda a, rb, sc: (a + sc[1] * (a_n if sa == 0 else 0), rb + sc[1] * (r_n // tr if sa == 1 else 0),
                                                       sc[1] if sa == 2 else 0))
    return _call(body, f"cast_place_{it.name}", (a_n, r_n // tr), [src], [dst], [_sds(it.full3, BF16)], prefetch=1,
                 after=after)(scal, shard)[0]


def _gather_start(items, fulls, name):
    ni = len(items)

    def body(*refs):
        outs = refs[ni:]
        ssem, rsem, full = outs[:ni], outs[ni:2 * ni], outs[2 * ni:3 * ni]
        x, y, c, me = _place()
        for i, it in enumerate(items):
            mine = it.half(it.shard(full[i], me), c)
            for k, rel in enumerate(_RELS):
                px, py, _ = _peer(x, y, rel)
                _remote(mine, mine, ssem[i].at[k], rsem[i].at[k], (px, py, c)).start()

    outs = pl.pallas_call(
        body, name=name, in_specs=[HBM] * ni, out_specs=[SEM] * (2 * ni) + [HBM] * ni,
        out_shape=[pltpu.SemaphoreType.DMA((3,))] * (2 * ni) + [pltpu.HBM(it.full3, BF16) for it in items],
        input_output_aliases={j: 2 * ni + j for j in range(ni)},
        compiler_params=pltpu.CompilerParams(has_side_effects=DATAFLOW),
    )(*[_in_hbm(f) for f in fulls])
    return outs[:ni], outs[ni:2 * ni], outs[2 * ni:]


def _gather_forward(items, fulls, ssems, rsems, after, name):
    ni = len(items)

    def body(*refs):
        ssem, rsem = refs[ni:2 * ni], refs[2 * ni:3 * ni]
        outs = refs[3 * ni + 1:]
        full, fsem, gsem = outs[:ni], outs[ni:2 * ni], outs[2 * ni:3 * ni]
        x, y, c, me = _place()
        sib = (x, y, 1 - c)
        for i, it in enumerate(items):
            for k, rel in enumerate(_RELS):
                _, _, pj = _peer(x, y, rel)
                got = it.half(it.shard(full[i], pj), c)
                _remote(got, got, ssem[i].at[k], rsem[i].at[k], sib).wait_recv()
                _remote(got, got, fsem[i].at[k], gsem[i].at[k], sib).start()
        for i, it in enumerate(items):
            mine = it.half(it.shard(full[i], me), c)
            for k in range(3):
                _remote(mine, mine, ssem[i].at[k], rsem[i].at[k], sib).wait_send()

    outs = pl.pallas_call(
        body, name=name, in_specs=[HBM] * ni + [SEM] * (2 * ni) + [ANY],
        out_specs=[HBM] * ni + [SEM] * (2 * ni),
        out_shape=[pltpu.HBM(it.full3, BF16) for it in items] + [pltpu.SemaphoreType.DMA((3,))] * (2 * ni),
        input_output_aliases={i: i for i in range(ni)},
        compiler_params=pltpu.CompilerParams(has_side_effects=DATAFLOW),
    )(*fulls, *ssems, *rsems, after)
    return outs[:ni], outs[ni:2 * ni], outs[2 * ni:]


def _gather_finish(items, fulls, fsems, gsems, name):
    ni = len(items)

    def body(*refs):
        fsem, gsem = refs[ni:2 * ni], refs[2 * ni:3 * ni]
        full = refs[3 * ni:]
        x, y, c, _ = _place()
        sib = (x, y, 1 - c)
        for i, it in enumerate(items):
            for k, rel in enumerate(_RELS):
                _, _, pj = _peer(x, y, rel)
                got = it.half(it.shard(full[i], pj), 1 - c)
                _remote(got, got, fsem[i].at[k], gsem[i].at[k], sib).wait_recv()
                sent = it.half(it.shard(full[i], pj), c)
                _remote(sent, sent, fsem[i].at[k], gsem[i].at[k], sib).wait_send()

    return pl.pallas_call(
        body, name=name, in_specs=[HBM] * ni + [SEM] * (2 * ni), out_specs=[HBM] * ni,
        out_shape=[pltpu.HBM(it.full3, BF16) for it in items],
        input_output_aliases={i: i for i in range(ni)},
        compiler_params=pltpu.CompilerParams(has_side_effects=DATAFLOW),
    )(*fulls, *fsems, *gsems)


def _exchange_start(name, arrays, n_sems, copies):
    na, ns = len(arrays), len(n_sems)

    def body(*refs):
        outs = refs[na:]
        ssem, rsem, thru, token = outs[:ns], outs[ns:2 * ns], outs[2 * ns:2 * ns + na], outs[-1]
        for send, _ in copies(thru, ssem, rsem):
            send.start()
        token[...] = jnp.zeros_like(token)

    outs = pl.pallas_call(
        body, name=name, in_specs=[HBM] * na,
        out_specs=[SEM] * (2 * ns) + [HBM] * na + [pl.BlockSpec(memory_space=pltpu.VMEM)],
        out_shape=[pltpu.SemaphoreType.DMA((n,)) for n in n_sems] * 2 + [pltpu.HBM(a.shape, a.dtype) for a in arrays]
        + [_sds((8, 128), F32)],
        input_output_aliases={j: 2 * ns + j for j in range(na)},
        compiler_params=pltpu.CompilerParams(has_side_effects=DATAFLOW),
    )(*[_in_hbm(a) for a in arrays])
    return outs[:ns], outs[ns:2 * ns], outs[2 * ns:2 * ns + na], outs[-1]


def _exchange_wait(name, arrays, ssems, rsems, copies, after):
    na, ns, nw = len(arrays), len(ssems), len(after)

    def body(*refs):
        ssem, rsem = refs[na:na + ns], refs[na + ns:na + 2 * ns]
        thru = refs[na + 2 * ns + nw:]
        for send, recv in copies(thru, ssem, rsem):
            recv.wait_recv()
            send.wait_send()

    return pl.pallas_call(
        body, name=name, in_specs=[HBM] * na + [SEM] * (2 * ns) + [ANY] * nw, out_specs=[HBM] * na,
        out_shape=[pltpu.HBM(a.shape, a.dtype) for a in arrays],
        input_output_aliases={i: i for i in range(na)},
        compiler_params=pltpu.CompilerParams(has_side_effects=DATAFLOW),
    )(*arrays, *ssems, *rsems, *after)


def _pair_copies(items):
    n = len(items)

    def copies(a, ssem, rsem):
        x, y, c, _ = _place()
        cps = [_remote(it.half(a[i], 1 - c), a[n + i], ssem[i].at[0], rsem[i].at[0], (x, y, 1 - c)) for i, it in enumerate(items)]
        return [(cp, cp) for cp in cps]

    return copies


def _chip_copies(items):
    n = len(items)

    def copies(a, ssem, rsem):
        x, y, c, _ = _place()
        cps = []
        for i, it in enumerate(items):
            for k, rel in enumerate(_RELS):
                px, py, pj = _peer(x, y, rel)
                cps.append(_remote(it.shard(a[i], pj), a[n + i].at[k], ssem[i].at[k], rsem[i].at[k], (px, py, c)))
        return [(cp, cp) for cp in cps]

    return copies


def _fill_copies(items):
    def copies(a, ssem, rsem):
        x, y, c, _ = _place()
        sib = (x, y, 1 - c)
        out = []
        for i, it in enumerate(items):
            mine, other = it.half(a[i], c), it.half(a[i], 1 - c)
            out.append((_remote(mine, mine, ssem[i].at[0], rsem[i].at[0], sib), _remote(other, other, ssem[i].at[0], rsem[i].at[0], sib)))
        return out

    return copies


def _ew_tiles(d):
    _, rows, cols = d.part3
    return _pick(rows, 256, 16), cols


def _pair_add(d, g_full, got, scal):
    tr, tc = _ew_tiles(d)
    a_n, r_n, c_n = d.pair3
    ha = d.half_axis

    def body(sc_ref, g_ref, r_ref, o_ref, ob_ref):
        s = g_ref[...].astype(F32) + r_ref[...].astype(F32)
        o_ref[...] = s
        ob_ref[...] = s.astype(BF16)

    blk = (1, tr, tc)
    same = pl.BlockSpec(blk, lambda a, rb, cb, sc: (a, rb, cb))
    mine = pl.BlockSpec(blk, lambda a, rb, cb, sc: (a + sc[0] * (a_n if ha == 0 else 0), rb + sc[0] * (r_n // tr if ha == 1 else 0), cb))
    return _call(body, f"pair_add_{d.name}", (a_n, r_n // tr, c_n // tc), [mine, same], [same, same],
                 [_sds(d.pair3, F32), _sds(d.pair3, BF16)], prefetch=1)(scal, g_full, got)


def _chip_reduce(d, pair_f32, got, scal):
    tr, tc = _ew_tiles(d)
    a_n, r_n, c_n = d.part3
    ha, sa = d.half_axis, d.shard_axis

    def body(sc_ref, p_ref, r0, r1, r2, o_ref):
        o_ref[...] = ((p_ref[...] + r0[...].astype(F32)) + r1[...].astype(F32)) + r2[...].astype(F32)

    blk = (1, tr, tc)
    own = pl.BlockSpec(blk, lambda a, rb, sc: (a + sc[1] * (a_n if sa == 0 else 0), rb + sc[1] * (r_n // tr if sa == 1 else 0),
                                               sc[1] if sa == 2 else 0))
    slot = lambda k: pl.BlockSpec((None,) + blk, lambda a, rb, sc: (k, a, rb, 0))
    out = pl.BlockSpec(blk, lambda a, rb, sc: (a + sc[0] * (a_n if ha == 0 else 0), rb + sc[0] * (r_n // tr if ha == 1 else 0), 0))
    return _call(body, f"chip_reduce_{d.name}", (a_n, r_n // tr), [own, slot(0), slot(1), slot(2)], [out],
                 [_sds(d.shard3, F32)], prefetch=1)(scal, pair_f32, got, got, got)[0]


def _adam_math(g, w, m, v):
    m = ADAM_B1 * m + (1.0 - ADAM_B1) * g
    v = ADAM_B2 * v + (1.0 - ADAM_B2) * (g * g)
    m_hat = m / (1.0 - ADAM_B1 ** ADAM_STEP)
    v_hat = v / (1.0 - ADAM_B2 ** ADAM_STEP)
    delta = -ADAM_LR * (m_hat / (jnp.sqrt(v_hat) + ADAM_EPS) + ADAM_WD * w)
    return delta, m, v


def _adam(d, g, w, m, v, prev=None):
    tr, tc = _ew_tiles(d)
    a_n, r_n, c_n = d.shard3
    n_prev = 0 if prev is None else 4

    def body(g_ref, w_ref, m_ref, v_ref, *rest):
        go_ref, d_ref, mo_ref, vo_ref = rest[n_prev:]
        gv = g_ref[...]
        go_ref[...] = gv
        d_ref[...], mo_ref[...], vo_ref[...] = _adam_math(gv, w_ref[...], m_ref[...], v_ref[...])

    plain = pl.BlockSpec((1, tr, tc), lambda a, rb: (a, rb, 0))
    if d.slab is None:
        wspec, shape = plain, d.shard3
    else:
        wspec, shape = pl.BlockSpec((None, 1, tr, tc), lambda a, rb: (d.slab, a, rb, 0)), (4,) + d.shard3
    in_specs = [plain] + [wspec] * 3
    args = [g, w, m, v]
    aliases = None
    if prev is not None:
        in_specs += [pl.BlockSpec(memory_space=pl.ANY)] * 4
        args += list(prev)
        aliases = {4 + k: k for k in range(4)}
    return _call(body, f"adam_{d.name}", (a_n, r_n // tr), in_specs, [wspec] * 4, [_sds(shape, F32)] * 4, aliases=aliases)(*args)


def _adam_small(gs, ws, ms, vs):
    n = len(gs)

    def body(*refs):
        for i in range(n):
            g, w, m, v = (refs[k * n + i][...] for k in range(4))
            d, mo, vo = _adam_math(g, w, m, v)
            refs[4 * n + i][...] = d
            refs[5 * n + i][...] = mo
            refs[6 * n + i][...] = vo

    vm = pl.BlockSpec(memory_space=pltpu.VMEM)
    outs = pl.pallas_call(body, name="adam_small", in_specs=[vm] * (4 * n), out_specs=[vm] * (3 * n),
                          out_shape=[_sds(g.shape, F32) for g in gs] * 3)(*gs, *ws, *ms, *vs)
    return outs[:n], outs[n:2 * n], outs[2 * n:]


def _allreduce_small(packed):
    rows, cols = packed.shape
    others = [(dx, dy, dc) for dx in (0, 1) for dy in (0, 1) for dc in (0, 1) if (dx, dy, dc) != (0, 0, 0)]

    def body(in_ref, out_ref, buf, ssem, rsem):
        x, y, c, _ = _place()
        lin = 4 * x + 2 * y + c
        buf[lin] = in_ref[...]
        cps = []
        for k, (dx, dy, dc) in enumerate(others):
            px = 1 - x if dx else x
            py = 1 - y if dy else y
            pc = 1 - c if dc else c
            cps.append((pltpu.make_async_remote_copy(src_ref=in_ref, dst_ref=buf.at[lin], send_sem=ssem.at[k], recv_sem=rsem.at[k],
                                                     device_id=(px, py, pc), device_id_type=MESH), 4 * px + 2 * py + pc))
        for cp, _ in cps:
            cp.start()
        for k, (cp, plin) in enumerate(cps):
            pltpu.make_async_remote_copy(src_ref=in_ref, dst_ref=buf.at[plin], send_sem=ssem.at[k], recv_sem=rsem.at[k],
                                         device_id=(x, y, c), device_id_type=MESH).wait_recv()
        for cp, _ in cps:
            cp.wait_send()
        acc = buf[0]
        for dev in range(1, 8):
            acc = acc + buf[dev]
        out_ref[...] = acc

    vm = pl.BlockSpec(memory_space=pltpu.VMEM)
    return pl.pallas_call(
        body, name="allreduce_small", in_specs=[vm], out_specs=vm, out_shape=_sds((rows, cols), F32),
        scratch_shapes=[pltpu.VMEM((8, rows, cols), F32), pltpu.SemaphoreType.DMA((7,)), pltpu.SemaphoreType.DMA((7,))],
        compiler_params=pltpu.CompilerParams(has_side_effects=True),
    )(packed)


_BIG = ("ffn_w_in", "ffn_w_out", "conv_w_pw1", "conv_w_pw2", "attn_w_qkv", "attn_w_o")
_SMALL = ("norm_g", "conv_b_pw1", "conv_w_dw", "conv_b_dw", "conv_norm_g", "conv_b_pw2", "attn_q_norm", "attn_k_norm")
_NAMES = ("norm_g", "ffn_w_in", "ffn_w_out", "conv_w_pw1", "conv_b_pw1", "conv_w_dw", "conv_b_dw", "conv_norm_g", "conv_w_pw2",
          "conv_b_pw2", "attn_w_qkv", "attn_q_norm", "attn_k_norm", "attn_w_o")


def _rows8(a, width):
    a = a.reshape(-1, min(a.shape[-1], width))
    return jnp.pad(a, ((0, -a.shape[0] % 8), (0, width - a.shape[1])))


def kernel(x, norm_g, ffn_w_in, ffn_w_out, conv_w_pw1, conv_b_pw1, conv_w_dw, conv_b_dw, conv_norm_g, conv_w_pw2, conv_b_pw2, attn_w_qkv, attn_q_norm, attn_k_norm, attn_w_o, loss_target, m_norm_g, m_ffn_w_in, m_ffn_w_out, m_conv_w_pw1, m_conv_b_pw1, m_conv_w_dw, m_conv_b_dw, m_conv_norm_g, m_conv_w_pw2, m_conv_b_pw2, m_attn_w_qkv, m_attn_q_norm, m_attn_k_norm, m_attn_w_o, v_norm_g, v_ffn_w_in, v_ffn_w_out, v_conv_w_pw1, v_conv_b_pw1, v_conv_w_dw, v_conv_b_dw, v_conv_norm_g, v_conv_w_pw2, v_conv_b_pw2, v_attn_w_qkv, v_attn_q_norm, v_attn_k_norm, v_attn_w_o):
    w = dict(zip(_NAMES, (norm_g, ffn_w_in, ffn_w_out, conv_w_pw1, conv_b_pw1, conv_w_dw, conv_b_dw, conv_norm_g, conv_w_pw2,
                          conv_b_pw2, attn_w_qkv, attn_q_norm, attn_k_norm, attn_w_o)))
    m = dict(zip(_NAMES, (m_norm_g, m_ffn_w_in, m_ffn_w_out, m_conv_w_pw1, m_conv_b_pw1, m_conv_w_dw, m_conv_b_dw, m_conv_norm_g,
                          m_conv_w_pw2, m_conv_b_pw2, m_attn_w_qkv, m_attn_q_norm, m_attn_k_norm, m_attn_w_o)))
    v = dict(zip(_NAMES, (v_norm_g, v_ffn_w_in, v_ffn_w_out, v_conv_w_pw1, v_conv_b_pw1, v_conv_w_dw, v_conv_b_dw, v_conv_norm_g,
                          v_conv_w_pw2, v_conv_b_pw2, v_attn_w_qkv, v_attn_q_norm, v_attn_k_norm, v_attn_w_o)))
    T, D = x.shape[1:]
    F = ffn_w_out.shape[2] * N_CHIPS
    HE = attn_w_o.shape[1] * N_CHIPS
    cx, cy, cc = lax.axis_index("x"), lax.axis_index("y"), lax.axis_index("c")
    me = 2 * cx + cy
    scal = jnp.stack([cc, me]).astype(jnp.int32)

    ffn_in = lambda lf: _Sharded(f"ffn_w_in_{lf}", (2, D // 2, 2 * F), 0, 2, "ffn_w_in", lf)
    ffn_out = lambda lf: _Sharded(f"ffn_w_out_{lf}", (4, F // 4, D), 1, 0, "ffn_w_out", lf)
    items = [
        ffn_in(0), ffn_out(0),
        _Sharded("conv_w_pw1", (2, D // 2, 2 * D), 0, 2, "conv_w_pw1"), _Sharded("conv_w_pw2", (4, D // 4, D), 1, 0, "conv_w_pw2"),
        ffn_in(1), ffn_out(1), ffn_in(2), ffn_out(2),
        _Sharded("attn_w_qkv", (2, D // 2, 9 * HE), 0, 2, "attn_w_qkv"), _Sharded("attn_w_o", (4, HE // 4, D), 1, 0, "attn_w_o"),
        ffn_in(3), ffn_out(3),
    ]
    mat_shapes = {"ffn_w_in": (D, 2 * F), "ffn_w_out": (F, D), "conv_w_pw1": (D, 2 * D), "conv_w_pw2": (D, D),
                  "attn_w_qkv": (D, 9 * HE), "attn_w_o": (HE, D)}

    def as_shards(a, n):
        it = next(i for i in items if i.src == n)
        return a.reshape(((4,) if it.slab is not None else ()) + it.shard3)

    norm_full, dw_full = _gather_small([norm_g.reshape(6, D // 4), conv_w_dw.reshape(CONV_WIDTH, D // 4)])
    place = lambda its, after: [_cast_place(it, as_shards(w[it.src], it.src), scal, after) for it in its]
    ssems, rsems, fulls = _gather_start(items[:2], place(items[:2], ()), "gather_start_first")
    more = _gather_start(items[2:], place(items[2:], fulls[:1]), "gather_start_rest")
    ssems, rsems, fulls = [list(a) + list(b) for a, b in zip((ssems, rsems, fulls), more)]

    def block_weights(k, after):
        sel = slice(2 * k, 2 * k + 2)
        got, fsems, gsems = _gather_forward(items[sel], fulls[sel], ssems[sel], rsems[sel], after, f"gather_forward_{k}")
        done = _gather_finish(items[sel], got, fsems, gsems, f"gather_finish_{k}")
        return [a.reshape(mat_shapes[it.src]) for a, it in zip(done, items[sel])]

    res = {}
    flight = []

    def advance(entry, k, dx):
        stage, its, st = entry
        n = len(its)
        if stage == 1:
            ssem, rsem, arrs = st
            done = _exchange_wait(f"grads_pair_wait_{k}", arrs, ssem, rsem, _pair_copies(its), dx)
            sums = [_pair_add(it, g, r, scal) for it, g, r in zip(its, done[:n], done[n:])]
            land = [lax.empty((3,) + it.part3, BF16) for it in its]
            ssem, rsem, arrs, tok = _exchange_start(f"grads_chip_start_{k}", [p[1] for p in sums] + land, [3] * n, _chip_copies(its))
            return (2, its, (ssem, rsem, arrs, [p[0] for p in sums])), tok
        if stage == 2:
            ssem, rsem, arrs, p32 = st
            got = _exchange_wait(f"grads_chip_wait_{k}", arrs, ssem, rsem, _chip_copies(its), dx)[n:]
            red = [_chip_reduce(it, p, r, scal) for it, p, r in zip(its, p32, got)]
            ssem, rsem, arrs, tok = _exchange_start(f"grads_fill_start_{k}", red, [1] * n, _fill_copies(its))
            return (3, its, (ssem, rsem, arrs)), tok
        ssem, rsem, arrs = st
        for it, g in zip(its, _exchange_wait(f"grads_fill_wait_{k}", arrs, ssem, rsem, _fill_copies(its), dx)):
            nm = it.src
            res[nm] = _adam(it, g, as_shards(w[nm], nm), as_shards(m[nm], nm), as_shards(v[nm], nm), prev=res.get(nm))
        return None, None

    def step_flight(dx):
        toks, left = [], []
        for k, entry in flight:
            entry, tok = advance(entry, k, dx)
            if entry is not None:
                left.append((k, entry))
                toks.append(tok)
        flight[:] = left
        return toks

    def grads_ready(k, pairs, dx):
        its = items[2 * k:2 * k + 2]
        toks = step_flight((dx,))
        g16 = [p.reshape(it.full3) for p, it in zip(pairs, its)]
        land = [lax.empty(it.pair3, BF16) for it in its]
        ssem, rsem, arrs, tok = _exchange_start(f"grads_pair_start_{k}", g16 + land, [1] * len(its), _pair_copies(its))
        flight.append((k, (1, its, (ssem, rsem, arrs))))
        return toks + [tok]

    sq, dx, grads = _local_step(x[0], loss_target[0], norm_full.reshape(2, 3, D), conv_b_pw1, dw_full, conv_b_dw, conv_norm_g,
                                conv_b_pw2, attn_q_norm[0], attn_k_norm[0], block_weights, grads_ready,
                                lambda dx: step_flight((dx,)))
    loss = lax.psum(0.5 * jnp.sum(sq) / D, ("x", "y", "c"))
    k_last, entry = flight.pop()
    entry, tok = advance(entry, k_last, (dx,))
    while flight:
        step_flight((dx, tok))

    out_g, out_d, out_m, out_v = {}, {}, {}, {}

    parts = [_rows8(grads[n], D) for n in _SMALL]
    tot = _allreduce_small(jnp.concatenate(parts, axis=0))
    sg, r0 = {}, 0
    for n, p in zip(_SMALL, parts):
        last = grads[n].shape[-1]
        g = tot[r0:r0 + grads[n].size // min(last, D), :min(last, D)].reshape(-1, last)
        r0 += p.shape[0]
        if n in ("norm_g", "conv_w_dw"):
            g = lax.dynamic_slice_in_dim(g, me * (D // 4), D // 4, axis=1)
        sg[n] = g
    flat = lambda a: a.reshape(-1, a.shape[-1])
    ds, ms, vs = _adam_small([sg[n] for n in _SMALL], [flat(w[n]) for n in _SMALL], [flat(m[n]) for n in _SMALL],
                             [flat(v[n]) for n in _SMALL])
    for i, n in enumerate(_SMALL):
        out_g[n], out_d[n], out_m[n], out_v[n] = (a.reshape(w[n].shape) for a in (sg[n], ds[i], ms[i], vs[i]))

    behind = tuple(r[1] for r in res.values()) + tuple(ds)
    while entry is not None:
        entry, _ = advance(entry, k_last, behind)
    for n in _BIG:
        out_g[n], out_d[n], out_m[n], out_v[n] = (a.reshape(w[n].shape) for a in res[n])

    return (loss, dx[None], *[out_g[n] for n in _NAMES], *[out_d[n] for n in _NAMES], *[out_m[n] for n in _NAMES],
            *[out_v[n] for n in _NAMES])
```

```python
import functools

import jax
import jax.numpy as jnp
from jax import lax
from jax.experimental import pallas as pl
from jax.experimental.pallas import tpu as pltpu

F32 = jnp.float32
BF16 = jnp.bfloat16
MESH = pl.DeviceIdType.MESH

NORM_EPS = 1e-6
CONV_WIDTH = 31
ATTN_GROUPS = ((128, 1), (512, 4), (2048, 16))
ATTN_BLOCK = 128
HEAD_DIM = 128
N_CHIPS = 4

ADAM_LR = 0.001
ADAM_B1 = 0.9
ADAM_B2 = 0.999
ADAM_EPS = 1e-08
ADAM_WD = 0.01
ADAM_STEP = 10

VMEM_LIMIT = 56 * 1024 * 1024
NT_DIMS = (((1,), (1,)), ((), ()))
TN_DIMS = (((0,), (0,)), ((), ()))


def _pick(n, pref, mult):
    t = (min(n, pref) // mult) * mult
    while t >= mult:
        if n % t == 0:
            return t
        t -= mult
    return n


def _call(body, name, grid, in_specs, out_specs, out_shape, scratch=(), aliases=None, prefetch=0, after=()):
    params = pltpu.CompilerParams(dimension_semantics=("arbitrary",) * len(grid), vmem_limit_bytes=VMEM_LIMIT)
    after = tuple(after)
    if after:
        inner, n_in = body, prefetch + len(in_specs)

        def body(*refs):
            return inner(*refs[:n_in], *refs[n_in + len(after):])

        in_specs = list(in_specs) + [pl.BlockSpec(memory_space=pl.ANY)] * len(after)
    if prefetch:
        spec = pltpu.PrefetchScalarGridSpec(
            num_scalar_prefetch=prefetch, grid=grid, in_specs=in_specs, out_specs=out_specs, scratch_shapes=list(scratch)
        )
        call = pl.pallas_call(body, name=name, grid_spec=spec, out_shape=out_shape, compiler_params=params,
                              input_output_aliases=aliases or {})
    else:
        call = pl.pallas_call(body, name=name, grid=grid, in_specs=in_specs, out_specs=out_specs, out_shape=out_shape,
                              scratch_shapes=list(scratch), compiler_params=params, input_output_aliases=aliases or {})
    return lambda *args: call(*args, *after)


def _sds(shape, dtype):
    return jax.ShapeDtypeStruct(shape, dtype)


def _sig(x):
    return 1.0 / (1.0 + jnp.exp(-x))


def _rstd(x):
    return lax.rsqrt(jnp.mean(x * x, axis=-1, keepdims=True) + NORM_EPS)


def _norm_bwd(dy, xhat, r, g):
    dxh = dy * g
    return r * (dxh - xhat * jnp.mean(dxh * xhat, axis=-1, keepdims=True))


def _dot(a, b):
    return jnp.dot(a, b, preferred_element_type=F32)


def _dot_nt(a, b):
    return lax.dot_general(a, b, NT_DIMS, preferred_element_type=F32)


def _dot_tn(a, b):
    return lax.dot_general(a, b, TN_DIMS, preferred_element_type=F32)


def _ffn_fwd(x, g, w_in, w_out, name, target=None):
    T, D = x.shape
    F = w_out.shape[0]
    tm = _pick(T, 256, 8)
    last = target is not None

    def body(x_ref, g_ref, wg_ref, wu_ref, wo_ref, *rest):
        xo_ref, gate_ref, up_ref = rest[last:last + 3]
        xv = x_ref[...]
        h = (xv * _rstd(xv) * g_ref[...]).astype(BF16)
        gate = _dot(h, wg_ref[...])
        up = _dot(h, wu_ref[...])
        gate_ref[...] = gate.astype(BF16)
        up_ref[...] = up.astype(BF16)
        a = (gate * _sig(gate) * up).astype(BF16)
        y = xv + 0.5 * _dot(a, wo_ref[...])
        if not last:
            xo_ref[...] = y
            return
        t_ref, sq_ref = rest[0], rest[4]

        @pl.when(pl.program_id(0) == 0)
        def _():
            sq_ref[...] = jnp.zeros_like(sq_ref)

        err = y - t_ref[...]
        xo_ref[...] = err * (1.0 / D)
        sq_ref[...] += jnp.sum(err * err, axis=0, keepdims=True)

    row = lambda i: (i, 0)
    fix = pl.BlockSpec((1, D), lambda i: (0, 0))
    held = lambda shape, k: pl.BlockSpec(shape, lambda i: (0, k), pipeline_mode=pl.Buffered(1))
    return _call(
        body, name, (T // tm,),
        [pl.BlockSpec((tm, D), row), fix, held((D, F), 0), held((D, F), 1), held((F, D), 0)] + [pl.BlockSpec((tm, D), row)] * last,
        [pl.BlockSpec((tm, D), row), pl.BlockSpec((tm, F), row), pl.BlockSpec((tm, F), row)] + [fix] * last,
        [_sds((T, D), F32), _sds((T, F), BF16), _sds((T, F), BF16)] + [_sds((1, D), F32)] * last,
    )(x, g, w_in, w_in, w_out, *([target] if last else []))


def _ffn_bwd(x, g, dy, gate, up, w_in, w_out, name, after=()):
    T, D = x.shape
    F = w_out.shape[0]

    def body_a(dy_ref, gate_ref, up_ref, wo_ref, dgate_ref, dup_ref, a_ref):
        dyb = (0.5 * dy_ref[...]).astype(BF16)
        gate = gate_ref[...].astype(F32)
        up = up_ref[...].astype(F32)
        sg = _sig(gate)
        sl = gate * sg
        a_ref[...] = (sl * up).astype(BF16)
        da = _dot_nt(dyb, wo_ref[...])
        dgate_ref[...] = (da * up * (sg * (1.0 + gate * (1.0 - sg)))).astype(BF16)
        dup_ref[...] = (da * sl).astype(BF16)

    ta = _pick(T, 256, 8)
    tile = pl.BlockSpec((ta, F), lambda i: (i, 0))
    dgate, dup, a = _call(
        body_a, name + "_hidden", (T // ta,),
        [pl.BlockSpec((ta, D), lambda i: (i, 0)), tile, tile,
         pl.BlockSpec((F, D), lambda i: (0, 0), pipeline_mode=pl.Buffered(1))],
        [tile, tile, tile],
        [_sds((T, F), BF16)] * 3, after=after,
    )(dy, gate, up, w_out)

    def body_b(x_ref, g_ref, dy_ref, dgate_ref, dup_ref, wg_ref, wu_ref, dx_ref, dg_ref, h_ref, dyb_ref):
        @pl.when(pl.program_id(0) == 0)
        def _():
            dg_ref[...] = jnp.zeros_like(dg_ref)

        xv = x_ref[...]
        r = _rstd(xv)
        xh = xv * r
        h_ref[...] = (xh * g_ref[...]).astype(BF16)
        dyb_ref[...] = (0.5 * dy_ref[...]).astype(BF16)
        dh = _dot_nt(dgate_ref[...], wg_ref[...]) + _dot_nt(dup_ref[...], wu_ref[...])
        dx_ref[...] = dy_ref[...] + _norm_bwd(dh, xh, r, g_ref[...])
        dg_ref[...] += jnp.sum(dh * xh, axis=0, keepdims=True)

    tb = _pick(T, 512, 8)
    row = lambda i: (i, 0)
    held = lambda k: pl.BlockSpec((D, F), lambda i: (0, k), pipeline_mode=pl.Buffered(1))
    dx, dg, h, dyb = _call(
        body_b, name, (T // tb,),
        [pl.BlockSpec((tb, D), row), pl.BlockSpec((1, D), lambda i: (0, 0)), pl.BlockSpec((tb, D), row),
         pl.BlockSpec((tb, F), row), pl.BlockSpec((tb, F), row), held(0), held(1)],
        [pl.BlockSpec((tb, D), row), pl.BlockSpec((1, D), lambda i: (0, 0)), pl.BlockSpec((tb, D), row), pl.BlockSpec((tb, D), row)],
        [_sds((T, D), F32), _sds((1, D), F32), _sds((T, D), BF16), _sds((T, D), BF16)],
    )(x, g, dy, dgate, dup, w_in, w_in)
    return dx, dg, dgate, dup, a, h, dyb


def _mm_tn(a, b, bm, bn, out_shape, out_block, out_map, scale, name, after=()):
    K, M = a.shape
    N = b.shape[1]

    def body(a_ref, b_ref, o_ref):
        o_ref[...] = (_dot_tn(a_ref[...].astype(BF16), b_ref[...].astype(BF16)) * scale).astype(BF16)

    in_specs = [pl.BlockSpec((K, bm), lambda mi, ni: (0, mi)), pl.BlockSpec((K, bn), lambda mi, ni: (0, ni))]
    return _call(body, name, (M // bm, N // bn), in_specs, [pl.BlockSpec(out_block, out_map)],
                 [_sds(out_shape, BF16)], after=after)(a, b)[0]


def _mm_tn_parts(a, bs, bn, cols, col0, name, prev=None, after=()):
    K, M = a.shape
    N = bs[0].shape[1]
    nt = N // bn
    ns = len(bs)

    def body(a_ref, *refs):
        o_ref = refs[-1]
        part = pl.program_id(0) // nt
        for s in range(ns):
            @pl.when(part == s)
            def _(s=s):
                o_ref[...] = _dot_tn(a_ref[...], refs[s][...]).astype(BF16)

    part_spec = lambda s: pl.BlockSpec((K, bn), lambda j: (0, jnp.clip(j - s * nt, 0, nt - 1)))
    in_specs = [pl.BlockSpec((K, M), lambda j: (0, 0))] + [part_spec(s) for s in range(ns)]
    args = [a, *bs]
    aliases = None
    if prev is not None:
        in_specs += [pl.BlockSpec(memory_space=pl.ANY)]
        args += [prev]
        aliases = {ns + 1: 0}
    return _call(body, name, (ns * nt,), in_specs, [pl.BlockSpec((M, bn), lambda j: (0, col0 // bn + j))],
                 [_sds((M, cols), BF16)], aliases=aliases, after=after)(*args)[0]


def _conv_pre(x, g, w1, b1, name):
    T, D = x.shape
    tm = _pick(T, 512, 8)

    def body(x_ref, g_ref, w_ref, b_ref, ag_ref, u_ref, h_ref):
        xv = x_ref[...]
        h = (xv * _rstd(xv) * g_ref[...]).astype(BF16)
        h_ref[...] = h
        ag = _dot(h, w_ref[...]) + b_ref[...]
        ag_ref[...] = ag.astype(BF16)
        u_ref[...] = ag[:, :D] * _sig(ag[:, D:])

    return _call(
        body, name, (T // tm,),
        [pl.BlockSpec((tm, D), lambda i: (i, 0)), pl.BlockSpec((1, D), lambda i: (0, 0)),
         pl.BlockSpec((D, 2 * D), lambda i: (0, 0)), pl.BlockSpec((1, 2 * D), lambda i: (0, 0))],
        [pl.BlockSpec((tm, 2 * D), lambda i: (i, 0)), pl.BlockSpec((tm, D), lambda i: (i, 0)),
         pl.BlockSpec((tm, D), lambda i: (i, 0))],
        [_sds((T, 2 * D), BF16), _sds((T, D), F32), _sds((T, D), BF16)],
    )(x, g, w1, b1)


_DW_PAD = 32
_DW_CHUNK = 256


def _dwconv(u, w, b, name):
    T, D = u.shape
    K = w.shape[0]
    ch = _pick(T, _DW_CHUNK, 8)
    lead = _DW_PAD - (K - 1)

    def body(u_ref, w_ref, b_ref, c_ref, ext):
        ext[pl.ds(0, _DW_PAD), :] = jnp.zeros((_DW_PAD, 128), F32)
        ext[pl.ds(_DW_PAD, T), :] = u_ref[...]
        for c0 in range(0, T, ch):
            acc = jnp.zeros((ch, 128), F32) + b_ref[...]
            for k in range(K):
                acc = acc + w_ref[pl.ds(k, 1), :] * ext[pl.ds(c0 + lead + k, ch), :]
            c_ref[pl.ds(c0, ch), :] = acc

    return _call(
        body, name, (D // 128,),
        [pl.BlockSpec((T, 128), lambda i: (0, i)), pl.BlockSpec((K, 128), lambda i: (0, i)),
         pl.BlockSpec((1, 128), lambda i: (0, i))],
        [pl.BlockSpec((T, 128), lambda i: (0, i))],
        [_sds((T, D), F32)],
        scratch=[pltpu.VMEM((T + _DW_PAD, 128), F32)],
    )(u, w, b)[0]


def _dwconv_bwd(dc, u, w, name):
    T, D = u.shape
    K = w.shape[0]
    ch = _pick(T, _DW_CHUNK, 8)

    def body(dc_ref, u_ref, w_ref, du_ref, dw_ref, db_ref, dext):
        dext[pl.ds(0, T), :] = dc_ref[...]
        dext[pl.ds(T, _DW_PAD), :] = jnp.zeros((_DW_PAD, 128), F32)
        dws =[jnp.zeros((8, 128), F32) for _ in range(K)]
        dbs = jnp.zeros((8, 128), F32)
        for c0 in range(0, T, ch):
            uv = u_ref[pl.ds(c0, ch), :]
            dbs = dbs + jnp.sum(dc_ref[pl.ds(c0, ch), :].reshape(ch // 8, 8, 128), axis=0)
            acc = jnp.zeros((ch, 128), F32)
            for k in range(K):
                win = dext[pl.ds(c0 + (K - 1) - k, ch), :]
                acc = acc + w_ref[pl.ds(k, 1), :] * win
                dws[k] = dws[k] + jnp.sum((win * uv).reshape(ch // 8, 8, 128), axis=0)
            du_ref[pl.ds(c0, ch), :] = acc
        for k in range(K):
            dw_ref[pl.ds(k, 1), :] = jnp.sum(dws[k], axis=0, keepdims=True)
        db_ref[...] = jnp.sum(dbs, axis=0, keepdims=True)

    return _call(
        body, name, (D // 128,),
        [pl.BlockSpec((T, 128), lambda i: (0, i)), pl.BlockSpec((T, 128), lambda i: (0, i)),
         pl.BlockSpec((K, 128), lambda i: (0, i))],
        [pl.BlockSpec((T, 128), lambda i: (0, i)), pl.BlockSpec((K, 128), lambda i: (0, i)),
         pl.BlockSpec((1, 128), lambda i: (0, i))],
        [_sds((T, D), F32), _sds((K, D), F32), _sds((1, D), F32)],
        scratch=[pltpu.VMEM((T + _DW_PAD, 128), F32)],
    )(dc, u, w)


def _conv_post(c, x, ng, w2, b2, name):
    T, D = x.shape
    tm = _pick(T, 512, 8)

    def body(c_ref, x_ref, ng_ref, w_ref, b_ref, xo_ref, s_ref):
        cv = c_ref[...]
        n = cv * _rstd(cv) * ng_ref[...]
        s = (n * _sig(n)).astype(BF16)
        s_ref[...] = s
        xo_ref[...] = x_ref[...] + _dot(s, w_ref[...]) + b_ref[...]

    row = lambda i: (i, 0)
    fix = lambda i: (0, 0)
    return _call(
        body, name, (T // tm,),
        [pl.BlockSpec((tm, D), row), pl.BlockSpec((tm, D), row), pl.BlockSpec((1, D), fix),
         pl.BlockSpec((D, D), fix), pl.BlockSpec((1, D), fix)],
        [pl.BlockSpec((tm, D), row), pl.BlockSpec((tm, D), row)],
        [_sds((T, D), F32), _sds((T, D), BF16)],
    )(c, x, ng, w2, b2)


def _conv_post_bwd(dy, c, ng, w2, name, after=()):
    T, D = dy.shape
    tm = _pick(T, 512, 8)

    def body(dy_ref, c_ref, ng_ref, w_ref, dc_ref, dng_ref, db_ref):
        @pl.when(pl.program_id(0) == 0)
        def _():
            dng_ref[...] = jnp.zeros_like(dng_ref)
            db_ref[...] = jnp.zeros_like(db_ref)

        dyv = dy_ref[...]
        ds = _dot_nt(dyv.astype(BF16), w_ref[...])
        cv = c_ref[...]
        r = _rstd(cv)
        ch = cv * r
        n = ch * ng_ref[...]
        sg = _sig(n)
        dn = ds * (sg * (1.0 + n * (1.0 - sg)))
        dc_ref[...] = _norm_bwd(dn, ch, r, ng_ref[...])
        dng_ref[...] += jnp.sum(dn * ch, axis=0, keepdims=True)
        db_ref[...] += jnp.sum(dyv, axis=0, keepdims=True)

    row = lambda i: (i, 0)
    fix = lambda i: (0, 0)
    return _call(
        body, name, (T // tm,),
        [pl.BlockSpec((tm, D), row), pl.BlockSpec((tm, D), row), pl.BlockSpec((1, D), fix), pl.BlockSpec((D, D), fix)],
        [pl.BlockSpec((tm, D), row), pl.BlockSpec((1, D), fix), pl.BlockSpec((1, D), fix)],
        [_sds((T, D), F32), _sds((1, D), F32), _sds((1, D), F32)], after=after,
    )(dy, c, ng, w2)


def _conv_pre_bwd(du, ag, x, g, dy, w1, name):
    T, D = x.shape
    tm = _pick(T, 512, 8)

    def body(du_ref, ag_ref, x_ref, g_ref, dy_ref, w_ref, dx_ref, dg_ref, dag_ref, db_ref):
        @pl.when(pl.program_id(0) == 0)
        def _():
            dg_ref[...] = jnp.zeros_like(dg_ref)
            db_ref[...] = jnp.zeros_like(db_ref)

        duv = du_ref[...]
        a = ag_ref[:, :D].astype(F32)
        gt = ag_ref[:, D:].astype(F32)
        sg = _sig(gt)
        da = duv * sg
        dgt = duv * a * sg * (1.0 - sg)
        db_ref[:, :D] += jnp.sum(da, axis=0, keepdims=True)
        db_ref[:, D:] += jnp.sum(dgt, axis=0, keepdims=True)
        dab = da.astype(BF16)
        dgb = dgt.astype(BF16)
        dag_ref[:, :D] = dab
        dag_ref[:, D:] = dgb
        dh = _dot_nt(dab, w_ref[:, :D]) + _dot_nt(dgb, w_ref[:, D:])
        xv = x_ref[...]
        r = _rstd(xv)
        xh = xv * r
        dx_ref[...] = dy_ref[...] + _norm_bwd(dh, xh, r, g_ref[...])
        dg_ref[...] += jnp.sum(dh * xh, axis=0, keepdims=True)

    row = lambda i: (i, 0)
    fix = lambda i: (0, 0)
    return _call(
        body, name, (T // tm,),
        [pl.BlockSpec((tm, D), row), pl.BlockSpec((tm, 2 * D), row), pl.BlockSpec((tm, D), row), pl.BlockSpec((1, D), fix),
         pl.BlockSpec((tm, D), row), pl.BlockSpec((D, 2 * D), fix)],
        [pl.BlockSpec((tm, D), row), pl.BlockSpec((1, D), fix), pl.BlockSpec((tm, 2 * D), row),
         pl.BlockSpec((1, 2 * D), fix)],
        [_sds((T, D), F32), _sds((1, D), F32), _sds((T, 2 * D), BF16), _sds((1, 2 * D), F32)],
    )(du, ag, x, g, dy, w1)


def _row_sums(a):
    return _dot(a.astype(BF16), jnp.ones((a.shape[1], 128), BF16))


def _head_rstd(x):
    return lax.rsqrt(_row_sums(x * x) * (1.0 / x.shape[1]) + NORM_EPS)


def _attn_qkv(x, g, wqkv, qn, kn, name):
    T, D = x.shape
    N = wqkv.shape[1]
    tn = N // 9
    E = HEAD_DIM
    tm = _pick(T, 256, 8)
    ng = qn.shape[0]

    def body(x_ref, g_ref, w_ref, qn_ref, kn_ref, o_ref, a_ref, h_ref):
        xv = x_ref[...]
        h = (xv * _rstd(xv) * g_ref[...]).astype(BF16)
        h_ref[...] = h
        for j in range(9):
            cs = slice(j * tn, (j + 1) * tn)
            res = _dot(h, w_ref[:, cs])
            o_ref[:, cs] = res.astype(BF16)
            if j % 3 == 2:
                a_ref[:, cs] = res.astype(BF16)
                continue
            grp = j // 3
            fac = qn_ref[grp:grp + 1, :] * kn_ref[grp:grp + 1, :] * (E ** -0.5) if j % 3 == 0 else None
            for h_i in range(tn // E):
                hs = slice(h_i * E, (h_i + 1) * E)
                xh = res[:, hs]
                hat = xh * _head_rstd(xh)
                a_ref[:, j * tn + h_i * E:j * tn + (h_i + 1) * E] = (hat if fac is None else hat * fac).astype(BF16)

    row = lambda i: (i, 0)
    fix = lambda i: (0, 0)
    return _call(
        body, name, (T // tm,),
        [pl.BlockSpec((tm, D), row), pl.BlockSpec((1, D), fix), pl.BlockSpec((D, N), fix, pipeline_mode=pl.Buffered(1)),
         pl.BlockSpec((ng, E), fix), pl.BlockSpec((ng, E), fix)],
        [pl.BlockSpec((tm, N), row), pl.BlockSpec((tm, N), row), pl.BlockSpec((tm, D), row)],
        [_sds((T, N), BF16), _sds((T, N), BF16), _sds((T, D), BF16)],
    )(x, g, wqkv, qn, kn)


def _band_mask(q, steps, nblk):
    i = lax.broadcasted_iota(jnp.int32, (q, 2 * q), 0)
    j = lax.broadcasted_iota(jnp.int32, (q, 2 * q), 1)
    diff = q + i - j
    first_key = jnp.where(nblk > 0, 0, q)
    return (diff >= 0) & (diff <= steps) & (j >= first_key)


def _per_row(blk, width):
    e = blk.shape[1]
    if width % e == 0:
        return jnp.concatenate([blk] * (width // e), axis=1)
    return jnp.broadcast_to(blk[:, :1], (blk.shape[0], width))


def _streams(a, dil, to_streams, name, col0=0, ncols=None):
    T, C = a.shape
    ncols = C if ncols is None else ncols
    Q = ATTN_BLOCK
    run = Q * dil
    reps = max(1, min(2048 // run, T // run))
    while T % (run * reps):
        reps -= 1
    rows = run * reps
    cw = _pick(ncols, 512, 128)
    ns = cw // 128

    def body(a_ref, o_ref, scr):
        for s in range(ns):
            ls = slice(s * 128, (s + 1) * 128)
            slab = scr.at[s]
            if to_streams:
                slab[...] = a_ref[:, ls].astype(F32)
                for u in range(reps):
                    for r in range(dil):
                        o_ref[pl.ds(u * run + r * Q, Q), ls] = slab[pl.ds(u * run + r, Q, stride=dil), :].astype(a.dtype)
            else:
                for u in range(reps):
                    for r in range(dil):
                        slab[pl.ds(u * run + r, Q, stride=dil), :] = a_ref[pl.ds(u * run + r * Q, Q), ls].astype(F32)
                o_ref[:, ls] = slab[...].astype(a.dtype)

    return _call(
        body, name, (T // rows, ncols // cw),
        [pl.BlockSpec((rows, cw), lambda i, j: (i, col0 // cw + j))],
        [pl.BlockSpec((rows, cw), lambda i, j: (i, j))],
        [_sds((T, ncols), a.dtype)],
        scratch=[pltpu.VMEM((ns, rows, 128), F32)],
    )(a)[0]


def _attn_fwd(qkv, base, HE, window, dil, name):
    T = qkv.shape[0]
    H = HE // HEAD_DIM
    E = HEAD_DIM
    Q = ATTN_BLOCK
    nb = T // dil // Q
    steps = window // dil

    def body(q_ref, kc_ref, kp_ref, vc_ref, vp_ref, o_ref, l_ref):
        n = pl.program_id(1)
        valid = _band_mask(Q, steps, n)
        ones = jnp.ones((2 * Q, E), BF16)
        outs, lses = [], []
        for h in range(H):
            hs = slice(h * E, (h + 1) * E)
            k2 = jnp.concatenate([kp_ref[:, hs], kc_ref[:, hs]], axis=0)
            v2 = jnp.concatenate([vp_ref[:, hs], vc_ref[:, hs]], axis=0)
            s = jnp.where(valid, _dot_nt(q_ref[:, hs], k2), -1e30)
            m = jnp.max(s, axis=-1, keepdims=True)
            p = jnp.exp(s - m).astype(BF16)
            acc = _dot(p, jnp.concatenate([v2, ones], axis=1))
            l = acc[:, E:]
            outs.append((acc[:, :E] * (1.0 / l)).astype(o_ref.dtype))
            lses.append(m + jnp.log(l))
        o_ref[...] = jnp.concatenate(outs, axis=1)
        l_ref[...] = jnp.concatenate(lses, axis=1)

    blk = lambda s, back: pl.BlockSpec((Q, HE), lambda r, n: (jnp.maximum(n - back, 0) * dil + r, base + s))
    out = pl.BlockSpec((Q, HE), lambda r, n: (n * dil + r, 0))
    return _call(
        body, name, (dil, nb),
        [blk(0, 0), blk(1, 0), blk(1, 1), blk(2, 0), blk(2, 1)],
        [out, out],
        [_sds((T, HE), BF16), _sds((T, HE), F32)],
    )(qkv, qkv, qkv, qkv, qkv)


def _attn_merge(os, lses, x, wo, name):
    T, D = x.shape
    HE = wo.shape[0]
    tm = _pick(T, 512, 8)
    ng = len(os)

    def body(*refs):
        o_refs = refs[:ng]
        l_refs = refs[ng:2 * ng]
        x_ref, w_ref, xo_ref, om_ref, lt_ref = refs[2 * ng:]
        ls = [r[...] for r in l_refs]
        m = functools.reduce(jnp.maximum, ls)
        es = [jnp.exp(l - m) for l in ls]
        tot = functools.reduce(lambda a, b: a + b, es)
        inv = 1.0 / tot
        om = functools.reduce(lambda a, b: a + b, [e * inv * r[...] for e, r in zip(es, o_refs)])
        omb = om.astype(BF16)
        om_ref[...] = omb
        lt_ref[...] = m + jnp.log(tot)
        xo_ref[...] = x_ref[...] + _dot(omb, w_ref[...])

    row = lambda i: (i, 0)
    fix = lambda i: (0, 0)
    return _call(
        body, name, (T // tm,),
        [pl.BlockSpec((tm, HE), row)] * (2 * ng) + [pl.BlockSpec((tm, D), row), pl.BlockSpec((HE, D), fix)],
        [pl.BlockSpec((tm, D), row), pl.BlockSpec((tm, HE), row), pl.BlockSpec((tm, HE), row)],
        [_sds((T, D), F32), _sds((T, HE), BF16), _sds((T, HE), F32)],
    )(*os, *lses, x, wo)


def _attn_out_bwd(dy, om, wo, name, after=()):
    T, D = dy.shape
    HE = wo.shape[0]
    E = HEAD_DIM
    tm = _pick(T, 512, 8)

    def body(dy_ref, om_ref, w_ref, dom_ref, dl_ref):
        dom = _dot_nt(dy_ref[...].astype(BF16), w_ref[...])
        dom_ref[...] = dom.astype(BF16)
        prod = dom * om_ref[...].astype(F32)
        for h in range(HE // E):
            hs = slice(h * E, (h + 1) * E)
            dl_ref[:, hs] = jnp.broadcast_to(jnp.sum(prod[:, hs], axis=-1, keepdims=True), (tm, E))

    row = lambda i: (i, 0)
    return _call(
        body, name, (T // tm,),
        [pl.BlockSpec((tm, D), row), pl.BlockSpec((tm, HE), row), pl.BlockSpec((HE, D), lambda i: (0, 0))],
        [pl.BlockSpec((tm, HE), row), pl.BlockSpec((tm, HE), row)],
        [_sds((T, HE), BF16), _sds((T, HE), F32)], after=after,
    )(dy, om, wo)


def _attn_bwd(qkv, base, HE, dom, lse, delta, window, dil, name):
    T = qkv.shape[0]
    H = HE // HEAD_DIM
    E = HEAD_DIM
    Q = ATTN_BLOCK
    nb = T // dil // Q
    steps = window // dil

    def body(q_ref, kc_ref, kp_ref, vc_ref, vp_ref, do_ref, l_ref, dl_ref, dq_ref, dk_ref, dv_ref, ck_sc, cv_sc):
        n = pl.program_id(1)

        @pl.when(n == 0)
        def _():
            ck_sc[...] = jnp.zeros_like(ck_sc)
            cv_sc[...] = jnp.zeros_like(cv_sc)

        @pl.when(n < nb)
        def _():
            valid = _band_mask(Q, steps, n)
            ck_old = ck_sc[...]
            cv_old = cv_sc[...]
            dqs, dks, dvs = [], [], []
            for h in range(H):
                hs = slice(h * E, (h + 1) * E)
                q = q_ref[:, hs]
                do = do_ref[:, hs]
                k2 = jnp.concatenate([kp_ref[:, hs], kc_ref[:, hs]], axis=0)
                v2 = jnp.concatenate([vp_ref[:, hs], vc_ref[:, hs]], axis=0)
                p = jnp.where(valid, jnp.exp(_dot_nt(q, k2) - _per_row(l_ref[:, hs], 2 * Q)), 0.0)
                ds = (p * (_dot_nt(do, v2) - _per_row(dl_ref[:, hs], 2 * Q))).astype(BF16)
                dqs.append(_dot(ds, k2).astype(BF16))
                dks.append(_dot_tn(q, ds).T)
                dvs.append(_dot_tn(do, p.astype(BF16)).T)
            cat = lambda parts: jnp.concatenate(parts, axis=1)
            dq_ref[...] = cat(dqs)
            dk_ref[...] = (ck_old + cat([d[:Q] for d in dks])).astype(BF16)
            dv_ref[...] = (cv_old + cat([d[:Q] for d in dvs])).astype(BF16)
            ck_sc[...] = cat([d[Q:] for d in dks])
            cv_sc[...] = cat([d[Q:] for d in dvs])

        @pl.when(n == nb)
        def _():
            dk_ref[...] = ck_sc[...].astype(BF16)
            dv_ref[...] = cv_sc[...].astype(BF16)

    nq = lambda n: jnp.minimum(n, nb - 1)
    blk = lambda s, back: pl.BlockSpec((Q, HE), lambda r, n: (jnp.maximum(nq(n) - back, 0) * dil + r, base + s))
    qblk = pl.BlockSpec((Q, HE), lambda r, n: (nq(n) * dil + r, 0))
    kblk = pl.BlockSpec((Q, HE), lambda r, n: (jnp.maximum(n - 1, 0) * dil + r, 0))
    return _call(
        body, name, (dil, nb + 1),
        [blk(0, 0), blk(1, 0), blk(1, 1), blk(2, 0), blk(2, 1), qblk, qblk, qblk],
        [qblk, kblk, kblk],
        [_sds((T, HE), BF16)] * 3,
        scratch=[pltpu.VMEM((Q, HE), F32), pltpu.VMEM((Q, HE), F32)],
    )(qkv, qkv, qkv, qkv, qkv, dom, lse, delta)


def _qk_norm_bwd(dq, dk, qkv, base, HE, gq, gk, name):
    T = dq.shape[0]
    E = HEAD_DIM
    tm = _pick(T, 512, 8)
    scale = E ** -0.5

    def body(dq_ref, dk_ref, q_ref, k_ref, gq_ref, gk_ref, oq_ref, ok_ref, dgq_ref, dgk_ref):
        @pl.when(pl.program_id(0) == 0)
        def _():
            dgq_ref[...] = jnp.zeros_like(dgq_ref)
            dgk_ref[...] = jnp.zeros_like(dgk_ref)

        gqv = gq_ref[...]
        gkv = gk_ref[...]
        c = gqv * gkv * scale
        dc = jnp.zeros((1, E), F32)
        for h in range(HE // E):
            hs = slice(h * E, (h + 1) * E)
            q = q_ref[:, hs].astype(F32)
            rq = _head_rstd(q)
            qh = q * rq
            a = dq_ref[:, hs].astype(F32)
            dc = dc + jnp.sum(a * qh, axis=0, keepdims=True)
            dqh = a * c
            oq_ref[:, hs] = (rq * (dqh - qh * (_row_sums(dqh * qh) * (1.0 / E)))).astype(BF16)
            k = k_ref[:, hs].astype(F32)
            rk = _head_rstd(k)
            kh = k * rk
            b = dk_ref[:, hs].astype(F32)
            ok_ref[:, hs] = (rk * (b - kh * (_row_sums(b * kh) * (1.0 / E)))).astype(BF16)
        dgq_ref[...] += dc * (gkv * scale)
        dgk_ref[...] += dc * (gqv * scale)

    row = lambda i: (i, 0)
    vec = pl.BlockSpec((1, E), lambda i: (0, 0))
    return _call(
        body, name, (T // tm,),
        [pl.BlockSpec((tm, HE), row), pl.BlockSpec((tm, HE), row), pl.BlockSpec((tm, HE), lambda i: (i, base)),
         pl.BlockSpec((tm, HE), lambda i: (i, base + 1)), vec, vec],
        [pl.BlockSpec((tm, HE), row), pl.BlockSpec((tm, HE), row), vec, vec],
        [_sds((T, HE), BF16), _sds((T, HE), BF16), _sds((1, E), F32), _sds((1, E), F32)],
    )(dq, dk, qkv, qkv, gq, gk)


def _attn_qkv_bwd(dqkv, x, g, dy, wqkv, name):
    T, D = x.shape
    HE = wqkv.shape[1] // 9
    tm = _pick(T, 256, 8)

    def body(*refs):
        d_refs = refs[:9]
        x_ref, g_ref, dy_ref, w_ref, dx_ref, dg_ref = refs[9:]

        @pl.when(pl.program_id(0) == 0)
        def _():
            dg_ref[...] = jnp.zeros_like(dg_ref)

        dh = _dot_nt(d_refs[0][...], w_ref[:, :HE])
        for s in range(1, 9):
            dh = dh + _dot_nt(d_refs[s][...], w_ref[:, s * HE:(s + 1) * HE])
        xv = x_ref[...]
        r = _rstd(xv)
        xh = xv * r
        dx_ref[...] = dy_ref[...] + _norm_bwd(dh, xh, r, g_ref[...])
        dg_ref[...] += jnp.sum(dh * xh, axis=0, keepdims=True)

    row = lambda i: (i, 0)
    fix = lambda i: (0, 0)
    return _call(
        body, name, (T // tm,),
        [pl.BlockSpec((tm, HE), row)] * 9 + [pl.BlockSpec((tm, D), row), pl.BlockSpec((1, D), fix), pl.BlockSpec((tm, D), row),
                                           pl.BlockSpec((D, 9 * HE), fix, pipeline_mode=pl.Buffered(1))],
        [pl.BlockSpec((tm, D), row), pl.BlockSpec((1, D), fix)],
        [_sds((T, D), F32), _sds((1, D), F32)],
    )(*dqkv, x, g, dy, wqkv)


def _local_step(x, target, norm_g, b_pw1, w_dw, b_dw, cng, b_pw2, qn, kn, block_weights, grads_ready, grads_tick):
    T, D = x.shape
    ng = lambda l, k: norm_g[l, k][None, :]
    bn = _pick(D, 256, 128)

    w_in, w_out = [None] * 4, [None] * 4
    w_in[0], w_out[0] = block_weights(0, x)
    x1, *gu0 = _ffn_fwd(x, ng(0, 0), w_in[0], w_out[0], "ffn_fwd_0")
    pw1, pw2 = block_weights(1, x1)
    ag, u, hc = _conv_pre(x1, ng(0, 1), pw1, b_pw1, "conv_pre")
    c = _dwconv(u, w_dw, b_dw, "dwconv")
    x2, s = _conv_post(c, x1, cng, pw2, b_pw2, "conv_post")
    w_in[1], w_out[1] = block_weights(2, x2)
    x3, *gu1 = _ffn_fwd(x2, ng(0, 2), w_in[1], w_out[1], "ffn_fwd_1")
    w_in[2], w_out[2] = block_weights(3, x3)
    x4, *gu2 = _ffn_fwd(x3, ng(1, 0), w_in[2], w_out[2], "ffn_fwd_2")
    wqkv, wo = block_weights(4, x4)
    HE = wo.shape[0]
    F = w_out[0].shape[0]
    bf = _pick(F, 256, 128)
    bh = _pick(HE, 512, 128)
    qkv, att, ha = _attn_qkv(x4, ng(1, 1), wqkv, qn, kn, "attn_qkv")
    qkv_s = [(att, 3 * gi) if dil == 1 else (_streams(att, dil, True, f"qkv_streams_{gi}", 3 * gi * HE, 3 * HE), 0)
             for gi, (_, dil) in enumerate(ATTN_GROUPS)]
    tokens = lambda a, dil, name: a if dil == 1 else _streams(a, dil, False, name)
    streams = lambda a, dil, name: a if dil == 1 else _streams(a, dil, True, name)
    os, lses = [], []
    for gi, (window, dil) in enumerate(ATTN_GROUPS):
        o, l = _attn_fwd(*qkv_s[gi], HE, window, dil, f"attn_fwd_{gi}")
        os.append(tokens(o, dil, f"o_tokens_{gi}"))
        lses.append(tokens(l, dil, f"lse_tokens_{gi}"))
    x5, om, lse = _attn_merge(os, lses, x4, wo, "attn_merge")
    w_in[3], w_out[3] = block_weights(5, x5)
    dy, *gu3, sq = _ffn_fwd(x5, ng(1, 2), w_in[3], w_out[3], "ffn_fwd_3", target=target)

    grads = {"ffn_w_in": [None] * 4, "ffn_w_out": [None] * 4}
    dnorm = [[None] * 3 for _ in range(2)]

    def ffn_back(lf, k, xin, gvec, dy, gu, after):
        dx, dg, dgate, dup, a, h, dyb = _ffn_bwd(xin, gvec, dy, gu[0], gu[1], w_in[lf], w_out[lf], f"ffn_bwd_{lf}", after=after)
        grads["ffn_w_in"][lf] = _mm_tn_parts(h, [dgate, dup], bf, 2 * F, 0, f"ffn_dw_in_{lf}",
                                             after=grads_tick(dx) if lf == 0 else ())
        early = grads_ready(k, (grads["ffn_w_in"][lf],), grads["ffn_w_in"][lf], 0) if lf == 0 else ()
        grads["ffn_w_out"][lf] = _mm_tn(a, dyb, bf, D, (F, D), (bf, D), lambda mi, ni: (mi, 0), 1.0, f"ffn_dw_out_{lf}",
                                        after=early)
        if lf == 0:
            return dx, dg, grads_ready(k, (grads["ffn_w_out"][lf],), grads["ffn_w_out"][lf], 1)
        return dx, dg, grads_ready(k, (grads["ffn_w_in"][lf], grads["ffn_w_out"][lf]), dx)

    dx, dnorm[1][2], tok = ffn_back(3, 5, x5, ng(1, 2), dy, gu3, ())
    dom, delta = _attn_out_bwd(dx, om, wo, "attn_out_bwd", after=tok)
    grads["attn_w_o"] = _mm_tn(om, dx, HE, bn, (HE, D), (HE, bn), lambda mi, ni: (0, ni), 1.0, "attn_dw_o")
    dqkv, dgq, dgk = [], [], []
    for gi, (window, dil) in enumerate(ATTN_GROUPS):
        ds = _attn_bwd(*qkv_s[gi], HE, streams(dom, dil, f"dom_streams_{gi}"), streams(lse, dil, f"lse_streams_{gi}"),
                       streams(delta, dil, f"delta_streams_{gi}"), window, dil, f"attn_bwd_{gi}")
        dq, dk, dv = [tokens(d, dil, f"d{nm}_tokens_{gi}") for d, nm in zip(ds, "qkv")]
        dq, dk, a_, b_ = _qk_norm_bwd(dq, dk, qkv, 3 * gi, HE, qn[gi][None, :], kn[gi][None, :], f"qk_norm_bwd_{gi}")
        dqkv += [dq, dk, dv]
        dgq.append(a_)
        dgk.append(b_)
    grads["attn_q_norm"] = jnp.concatenate(dgq, axis=0)
    grads["attn_k_norm"] = jnp.concatenate(dgk, axis=0)
    d_qkv = None
    for gi in range(len(ATTN_GROUPS)):
        d_qkv = _mm_tn_parts(ha, dqkv[3 * gi:3 * gi + 3], bh, 9 * HE, 3 * gi * HE, f"attn_dw_qkv_{gi}", prev=d_qkv)
    grads["attn_w_qkv"] = d_qkv
    dx, dnorm[1][1] = _attn_qkv_bwd(dqkv, x4, ng(1, 1), dx, wqkv, "attn_qkv_bwd")
    tok = grads_ready(4, (grads["attn_w_qkv"], grads["attn_w_o"]), dx)
    dx, dnorm[1][0], tok = ffn_back(2, 3, x3, ng(1, 0), dx, gu2, tok)

    dx, dnorm[0][2], tok = ffn_back(1, 2, x2, ng(0, 2), dx, gu1, tok)
    dc, grads["conv_norm_g"], grads["conv_b_pw2"] = _conv_post_bwd(dx, c, cng, pw2, "conv_post_bwd", after=tok)
    grads["conv_w_pw2"] = _mm_tn(s, dx, D, bn, (D, D), (D, bn), lambda mi, ni: (0, ni), 1.0, "conv_dw_pw2")
    du, grads["conv_w_dw"], grads["conv_b_dw"] = _dwconv_bwd(dc, u, w_dw, "dwconv_bwd")
    dx, dnorm[0][1], dag, grads["conv_b_pw1"] = _conv_pre_bwd(du, ag, x1, ng(0, 1), dx, pw1, "conv_pre_bwd")
    grads["conv_w_pw1"] = _mm_tn(hc, dag, D, 2 * bn, (D, 2 * D), (D, 2 * bn), lambda mi, ni: (0, ni), 1.0, "conv_dw_pw1")
    tok = grads_ready(1, (grads["conv_w_pw1"], grads["conv_w_pw2"]), dx)
    dx, dnorm[0][0], _ = ffn_back(0, 0, x, ng(0, 0), dx, gu0, tok)

    grads["norm_g"] = jnp.concatenate([jnp.concatenate(r, axis=0)[None] for r in dnorm], axis=0)
    return sq, dx, grads


class _Sharded:
    def __init__(self, name, full3, half_axis, shard_axis, src, slab=None):
        self.name, self.full3, self.half_axis, self.shard_axis = name, tuple(full3), half_axis, shard_axis
        self.src, self.slab = src, slab

    def _cut(self, shape, axis, parts):
        s = list(shape)
        s[axis] //= parts
        return tuple(s)

    @property
    def shard3(self):
        return self._cut(self.full3, self.shard_axis, N_CHIPS)

    @property
    def pair3(self):
        return self._cut(self.full3, self.half_axis, 2)

    @property
    def part3(self):
        return self._cut(self.shard3, self.half_axis, 2)

    @staticmethod
    def _slice(ref, axis, idx, parts):
        n = ref.shape[axis] // parts
        start = idx * n
        minor = len(ref.shape) - 1 - axis
        if minor < 2 and not isinstance(start, int):
            start = pl.multiple_of(start, 128 if minor == 0 else (16 if n % 16 == 0 else 8))
        sl = [slice(None)] * len(ref.shape)
        sl[axis] = pl.ds(start, n)
        return ref.at[tuple(sl)]

    def half(self, ref, h):
        return self._slice(ref, self.half_axis, h, 2)

    def shard(self, ref, j):
        return self._slice(ref, self.shard_axis, j, N_CHIPS)


def _place():
    x, y, c = lax.axis_index("x"), lax.axis_index("y"), lax.axis_index("c")
    return x, y, c, 2 * x + y


_RELS = (1, 2, 3)


def _peer(x, y, rel):
    px = 1 - x if rel & 2 else x
    py = 1 - y if rel & 1 else y
    return px, py, 2 * px + py


ANY = pl.BlockSpec(memory_space=pl.ANY)


def _comm_call(body, name, n_in, out_shape, n_sems, aliases=None):
    return pl.pallas_call(
        body, name=name, in_specs=[ANY] * n_in, out_specs=[ANY] * len(out_shape), out_shape=out_shape,
        scratch_shapes=[pltpu.SemaphoreType.DMA((n,)) for n in n_sems],
        input_output_aliases=aliases or {},
        compiler_params=pltpu.CompilerParams(has_side_effects=True),
    )


def _remote(src, dst, send_sem, recv_sem, dev):
    return pltpu.make_async_remote_copy(src_ref=src, dst_ref=dst, send_sem=send_sem, recv_sem=recv_sem, device_id=dev,
                                        device_id_type=MESH)


def _gather_small(small_shards):
    ns = len(small_shards)

    def body(*refs):
        ins, outs = refs[:ns], refs[ns:2 * ns]
        lsem, ssem, rsem = refs[2 * ns:]
        x, y, c, me = _place()
        cols = lambda ref, j: _Sharded._slice(ref, 1, j, N_CHIPS)
        local = [pltpu.make_async_copy(ins[i], cols(outs[i], me), lsem.at[i]) for i in range(ns)]
        sends = []
        for i in range(ns):
            for k, rel in enumerate(_RELS):
                px, py, _ = _peer(x, y, rel)
                sends.append(_remote(ins[i], cols(outs[i], me), ssem.at[3 * i + k], rsem.at[3 * i + k], (px, py, c)))
        for cp in local + sends:
            cp.start()
        for i in range(ns):
            for k, rel in enumerate(_RELS):
                _, _, pj = _peer(x, y, rel)
                got = cols(outs[i], pj)
                _remote(got, got, ssem.at[3 * i + k], rsem.at[3 * i + k], (x, y, c)).wait_recv()
        for cp in sends:
            cp.wait_send()
        for cp in local:
            cp.wait()

    out_shape = [_sds((s.shape[0], s.shape[1] * N_CHIPS), F32) for s in small_shards]
    return _comm_call(body, "gather_small", ns, out_shape, [ns, 3 * ns, 3 * ns])(*small_shards)


HBM = pl.BlockSpec(memory_space=pltpu.HBM)
SEM = pl.BlockSpec(memory_space=pltpu.SEMAPHORE)
DATAFLOW = pltpu.SideEffectType.DATAFLOW_SIDE_EFFECTING


def _in_hbm(a):
    return pltpu.with_memory_space_constraint(a, pltpu.HBM)


def _cast_place(it, shard, scal, after=()):
    a_n, r_n, c_n = it.shard3
    tr = _pick(r_n, 256, 16)
    sa = it.shard_axis

    def body(sc_ref, s_ref, o_ref):
        o_ref[...] = s_ref[...].astype(BF16)

    if it.slab is None:
        src = pl.BlockSpec((1, tr, c_n), lambda a, rb, sc: (a, rb, 0))
    else:
        src = pl.BlockSpec((None, 1, tr, c_n), lambda a, rb, sc: (it.slab, a, rb, 0))
    dst = pl.BlockSpec((1, tr, c_n), lambda a, rb, sc: (a + sc[1] * (a_n if sa == 0 else 0), rb + sc[1] * (r_n // tr if sa == 1 else 0),
                                                       sc[1] if sa == 2 else 0))
    return _call(body, f"cast_place_{it.name}", (a_n, r_n // tr), [src], [dst], [_sds(it.full3, BF16)], prefetch=1,
                 after=after)(scal, shard)[0]


def _gather_start(items, fulls, name):
    ni = len(items)

    def body(*refs):
        outs = refs[ni:]
        ssem, rsem, full = outs[:ni], outs[ni:2 * ni], outs[2 * ni:3 * ni]
        x, y, c, me = _place()
        for i, it in enumerate(items):
            mine = it.half(it.shard(full[i], me), c)
            for k, rel in enumerate(_RELS):
                px, py, _ = _peer(x, y, rel)
                _remote(mine, mine, ssem[i].at[k], rsem[i].at[k], (px, py, c)).start()

    outs = pl.pallas_call(
        body, name=name, in_specs=[HBM] * ni, out_specs=[SEM] * (2 * ni) + [HBM] * ni,
        out_shape=[pltpu.SemaphoreType.DMA((3,))] * (2 * ni) + [pltpu.HBM(it.full3, BF16) for it in items],
        input_output_aliases={j: 2 * ni + j for j in range(ni)},
        compiler_params=pltpu.CompilerParams(has_side_effects=DATAFLOW),
    )(*[_in_hbm(f) for f in fulls])
    return outs[:ni], outs[ni:2 * ni], outs[2 * ni:]


def _gather_forward(items, fulls, ssems, rsems, after, name):
    ni = len(items)

    def body(*refs):
        ssem, rsem = refs[ni:2 * ni], refs[2 * ni:3 * ni]
        outs = refs[3 * ni + 1:]
        full, fsem, gsem = outs[:ni], outs[ni:2 * ni], outs[2 * ni:3 * ni]
        x, y, c, me = _place()
        sib = (x, y, 1 - c)
        for i, it in enumerate(items):
            for k, rel in enumerate(_RELS):
                _, _, pj = _peer(x, y, rel)
                got = it.half(it.shard(full[i], pj), c)
                _remote(got, got, ssem[i].at[k], rsem[i].at[k], sib).wait_recv()
                _remote(got, got, fsem[i].at[k], gsem[i].at[k], sib).start()
        for i, it in enumerate(items):
            mine = it.half(it.shard(full[i], me), c)
            for k in range(3):
                _remote(mine, mine, ssem[i].at[k], rsem[i].at[k], sib).wait_send()

    outs = pl.pallas_call(
        body, name=name, in_specs=[HBM] * ni + [SEM] * (2 * ni) + [ANY],
        out_specs=[HBM] * ni + [SEM] * (2 * ni),
        out_shape=[pltpu.HBM(it.full3, BF16) for it in items] + [pltpu.SemaphoreType.DMA((3,))] * (2 * ni),
        input_output_aliases={i: i for i in range(ni)},
        compiler_params=pltpu.CompilerParams(has_side_effects=DATAFLOW),
    )(*fulls, *ssems, *rsems, after)
    return outs[:ni], outs[ni:2 * ni], outs[2 * ni:]


def _gather_finish(items, fulls, fsems, gsems, name):
    ni = len(items)

    def body(*refs):
        fsem, gsem = refs[ni:2 * ni], refs[2 * ni:3 * ni]
        full = refs[3 * ni:]
        x, y, c, _ = _place()
        sib = (x, y, 1 - c)
        for i, it in enumerate(items):
            for k, rel in enumerate(_RELS):
                _, _, pj = _peer(x, y, rel)
                got = it.half(it.shard(full[i], pj), 1 - c)
                _remote(got, got, fsem[i].at[k], gsem[i].at[k], sib).wait_recv()
                sent = it.half(it.shard(full[i], pj), c)
                _remote(sent, sent, fsem[i].at[k], gsem[i].at[k], sib).wait_send()

    return pl.pallas_call(
        body, name=name, in_specs=[HBM] * ni + [SEM] * (2 * ni), out_specs=[HBM] * ni,
        out_shape=[pltpu.HBM(it.full3, BF16) for it in items],
        input_output_aliases={i: i for i in range(ni)},
        compiler_params=pltpu.CompilerParams(has_side_effects=DATAFLOW),
    )(*fulls, *fsems, *gsems)


def _exchange_start(name, arrays, n_sems, copies):
    na, ns = len(arrays), len(n_sems)

    def body(*refs):
        outs = refs[na:]
        ssem, rsem, thru, token = outs[:ns], outs[ns:2 * ns], outs[2 * ns:2 * ns + na], outs[-1]
        for send, _ in copies(thru, ssem, rsem):
            send.start()
        token[...] = jnp.zeros_like(token)

    outs = pl.pallas_call(
        body, name=name, in_specs=[HBM] * na,
        out_specs=[SEM] * (2 * ns) + [HBM] * na + [pl.BlockSpec(memory_space=pltpu.VMEM)],
        out_shape=[pltpu.SemaphoreType.DMA((n,)) for n in n_sems] * 2 + [pltpu.HBM(a.shape, a.dtype) for a in arrays]
        + [_sds((8, 128), F32)],
        input_output_aliases={j: 2 * ns + j for j in range(na)},
        compiler_params=pltpu.CompilerParams(has_side_effects=DATAFLOW),
    )(*[_in_hbm(a) for a in arrays])
    return outs[:ns], outs[ns:2 * ns], outs[2 * ns:2 * ns + na], outs[-1]


def _exchange_wait(name, arrays, ssems, rsems, copies, after):
    na, ns, nw = len(arrays), len(ssems), len(after)

    def body(*refs):
        ssem, rsem = refs[na:na + ns], refs[na + ns:na + 2 * ns]
        thru = refs[na + 2 * ns + nw:]
        for send, recv in copies(thru, ssem, rsem):
            recv.wait_recv()
            send.wait_send()

    return pl.pallas_call(
        body, name=name, in_specs=[HBM] * na + [SEM] * (2 * ns) + [ANY] * nw, out_specs=[HBM] * na,
        out_shape=[pltpu.HBM(a.shape, a.dtype) for a in arrays],
        input_output_aliases={i: i for i in range(na)},
        compiler_params=pltpu.CompilerParams(has_side_effects=DATAFLOW),
    )(*arrays, *ssems, *rsems, *after)


def _pair_copies(items):
    n = len(items)

    def copies(a, ssem, rsem):
        x, y, c, _ = _place()
        cps = [_remote(it.half(a[i], 1 - c), a[n + i], ssem[i].at[0], rsem[i].at[0], (x, y, 1 - c)) for i, it in enumerate(items)]
        return [(cp, cp) for cp in cps]

    return copies


def _chip_copies(items):
    n = len(items)

    def copies(a, ssem, rsem):
        x, y, c, _ = _place()
        cps = []
        for i, it in enumerate(items):
            for k, rel in enumerate(_RELS):
                px, py, pj = _peer(x, y, rel)
                cps.append(_remote(it.shard(a[i], pj), a[n + i].at[k], ssem[i].at[k], rsem[i].at[k], (px, py, c)))
        return [(cp, cp) for cp in cps]

    return copies


def _fill_copies(items):
    def copies(a, ssem, rsem):
        x, y, c, _ = _place()
        sib = (x, y, 1 - c)
        out = []
        for i, it in enumerate(items):
            mine, other = it.half(a[i], c), it.half(a[i], 1 - c)
            out.append((_remote(mine, mine, ssem[i].at[0], rsem[i].at[0], sib), _remote(other, other, ssem[i].at[0], rsem[i].at[0], sib)))
        return out

    return copies


def _ew_tiles(d):
    _, rows, cols = d.part3
    return _pick(rows, 256, 16), cols


def _pair_add(d, g_full, got, scal):
    tr, tc = _ew_tiles(d)
    a_n, r_n, c_n = d.pair3
    ha = d.half_axis

    def body(sc_ref, g_ref, r_ref, o_ref, ob_ref):
        s = g_ref[...].astype(F32) + r_ref[...].astype(F32)
        o_ref[...] = s
        ob_ref[...] = s.astype(BF16)

    blk = (1, tr, tc)
    same = pl.BlockSpec(blk, lambda a, rb, cb, sc: (a, rb, cb))
    mine = pl.BlockSpec(blk, lambda a, rb, cb, sc: (a + sc[0] * (a_n if ha == 0 else 0), rb + sc[0] * (r_n // tr if ha == 1 else 0), cb))
    return _call(body, f"pair_add_{d.name}", (a_n, r_n // tr, c_n // tc), [mine, same], [same, same],
                 [_sds(d.pair3, F32), _sds(d.pair3, BF16)], prefetch=1)(scal, g_full, got)


def _chip_reduce(d, pair_f32, got, scal):
    tr, tc = _ew_tiles(d)
    a_n, r_n, c_n = d.part3
    ha, sa = d.half_axis, d.shard_axis

    def body(sc_ref, p_ref, r0, r1, r2, o_ref):
        o_ref[...] = ((p_ref[...] + r0[...].astype(F32)) + r1[...].astype(F32)) + r2[...].astype(F32)

    blk = (1, tr, tc)
    own = pl.BlockSpec(blk, lambda a, rb, sc: (a + sc[1] * (a_n if sa == 0 else 0), rb + sc[1] * (r_n // tr if sa == 1 else 0),
                                               sc[1] if sa == 2 else 0))
    slot = lambda k: pl.BlockSpec((None,) + blk, lambda a, rb, sc: (k, a, rb, 0))
    out = pl.BlockSpec(blk, lambda a, rb, sc: (a + sc[0] * (a_n if ha == 0 else 0), rb + sc[0] * (r_n // tr if ha == 1 else 0), 0))
    return _call(body, f"chip_reduce_{d.name}", (a_n, r_n // tr), [own, slot(0), slot(1), slot(2)], [out],
                 [_sds(d.shard3, F32)], prefetch=1)(scal, pair_f32, got, got, got)[0]


def _adam_math(g, w, m, v):
    m = ADAM_B1 * m + (1.0 - ADAM_B1) * g
    v = ADAM_B2 * v + (1.0 - ADAM_B2) * (g * g)
    m_hat = m / (1.0 - ADAM_B1 ** ADAM_STEP)
    v_hat = v / (1.0 - ADAM_B2 ** ADAM_STEP)
    delta = -ADAM_LR * (m_hat / (jnp.sqrt(v_hat) + ADAM_EPS) + ADAM_WD * w)
    return delta, m, v


def _adam(d, g, w, m, v, prev=None):
    tr, tc = _ew_tiles(d)
    a_n, r_n, c_n = d.shard3
    n_prev = 0 if prev is None else 4

    def body(g_ref, w_ref, m_ref, v_ref, *rest):
        go_ref, d_ref, mo_ref, vo_ref = rest[n_prev:]
        gv = g_ref[...]
        go_ref[...] = gv
        d_ref[...], mo_ref[...], vo_ref[...] = _adam_math(gv, w_ref[...], m_ref[...], v_ref[...])

    plain = pl.BlockSpec((1, tr, tc), lambda a, rb: (a, rb, 0))
    if d.slab is None:
        wspec, shape = plain, d.shard3
    else:
        wspec, shape = pl.BlockSpec((None, 1, tr, tc), lambda a, rb: (d.slab, a, rb, 0)), (4,) + d.shard3
    in_specs = [plain] + [wspec] * 3
    args = [g, w, m, v]
    aliases = None
    if prev is not None:
        in_specs += [pl.BlockSpec(memory_space=pl.ANY)] * 4
        args += list(prev)
        aliases = {4 + k: k for k in range(4)}
    return _call(body, f"adam_{d.name}", (a_n, r_n // tr), in_specs, [wspec] * 4, [_sds(shape, F32)] * 4, aliases=aliases)(*args)


def _adam_small(gs, ws, ms, vs):
    n = len(gs)

    def body(*refs):
        for i in range(n):
            g, w, m, v = (refs[k * n + i][...] for k in range(4))
            d, mo, vo = _adam_math(g, w, m, v)
            refs[4 * n + i][...] = d
            refs[5 * n + i][...] = mo
            refs[6 * n + i][...] = vo

    vm = pl.BlockSpec(memory_space=pltpu.VMEM)
    outs = pl.pallas_call(body, name="adam_small", in_specs=[vm] * (4 * n), out_specs=[vm] * (3 * n),
                          out_shape=[_sds(g.shape, F32) for g in gs] * 3)(*gs, *ws, *ms, *vs)
    return outs[:n], outs[n:2 * n], outs[2 * n:]


def _allreduce_small(packed):
    rows, cols = packed.shape
    others = [(dx, dy, dc) for dx in (0, 1) for dy in (0, 1) for dc in (0, 1) if (dx, dy, dc) != (0, 0, 0)]

    def body(in_ref, out_ref, buf, ssem, rsem):
        x, y, c, _ = _place()
        lin = 4 * x + 2 * y + c
        buf[lin] = in_ref[...]
        cps = []
        for k, (dx, dy, dc) in enumerate(others):
            px = 1 - x if dx else x
            py = 1 - y if dy else y
            pc = 1 - c if dc else c
            cps.append((pltpu.make_async_remote_copy(src_ref=in_ref, dst_ref=buf.at[lin], send_sem=ssem.at[k], recv_sem=rsem.at[k],
                                                     device_id=(px, py, pc), device_id_type=MESH), 4 * px + 2 * py + pc))
        for cp, _ in cps:
            cp.start()
        for k, (cp, plin) in enumerate(cps):
            pltpu.make_async_remote_copy(src_ref=in_ref, dst_ref=buf.at[plin], send_sem=ssem.at[k], recv_sem=rsem.at[k],
                                         device_id=(x, y, c), device_id_type=MESH).wait_recv()
        for cp, _ in cps:
            cp.wait_send()
        acc = buf[0]
        for dev in range(1, 8):
            acc = acc + buf[dev]
        out_ref[...] = acc

    vm = pl.BlockSpec(memory_space=pltpu.VMEM)
    return pl.pallas_call(
        body, name="allreduce_small", in_specs=[vm], out_specs=vm, out_shape=_sds((rows, cols), F32),
        scratch_shapes=[pltpu.VMEM((8, rows, cols), F32), pltpu.SemaphoreType.DMA((7,)), pltpu.SemaphoreType.DMA((7,))],
        compiler_params=pltpu.CompilerParams(has_side_effects=True),
    )(packed)


_BIG = ("ffn_w_in", "ffn_w_out", "conv_w_pw1", "conv_w_pw2", "attn_w_qkv", "attn_w_o")
_SMALL = ("norm_g", "conv_b_pw1", "conv_w_dw", "conv_b_dw", "conv_norm_g", "conv_b_pw2", "attn_q_norm", "attn_k_norm")
_NAMES = ("norm_g", "ffn_w_in", "ffn_w_out", "conv_w_pw1", "conv_b_pw1", "conv_w_dw", "conv_b_dw", "conv_norm_g", "conv_w_pw2",
          "conv_b_pw2", "attn_w_qkv", "attn_q_norm", "attn_k_norm", "attn_w_o")


def _rows8(a, width):
    a = a.reshape(-1, min(a.shape[-1], width))
    return jnp.pad(a, ((0, -a.shape[0] % 8), (0, width - a.shape[1])))


def kernel(x, norm_g, ffn_w_in, ffn_w_out, conv_w_pw1, conv_b_pw1, conv_w_dw, conv_b_dw, conv_norm_g, conv_w_pw2, conv_b_pw2, attn_w_qkv, attn_q_norm, attn_k_norm, attn_w_o, loss_target, m_norm_g, m_ffn_w_in, m_ffn_w_out, m_conv_w_pw1, m_conv_b_pw1, m_conv_w_dw, m_conv_b_dw, m_conv_norm_g, m_conv_w_pw2, m_conv_b_pw2, m_attn_w_qkv, m_attn_q_norm, m_attn_k_norm, m_attn_w_o, v_norm_g, v_ffn_w_in, v_ffn_w_out, v_conv_w_pw1, v_conv_b_pw1, v_conv_w_dw, v_conv_b_dw, v_conv_norm_g, v_conv_w_pw2, v_conv_b_pw2, v_attn_w_qkv, v_attn_q_norm, v_attn_k_norm, v_attn_w_o):
    w = dict(zip(_NAMES, (norm_g, ffn_w_in, ffn_w_out, conv_w_pw1, conv_b_pw1, conv_w_dw, conv_b_dw, conv_norm_g, conv_w_pw2,
                          conv_b_pw2, attn_w_qkv, attn_q_norm, attn_k_norm, attn_w_o)))
    m = dict(zip(_NAMES, (m_norm_g, m_ffn_w_in, m_ffn_w_out, m_conv_w_pw1, m_conv_b_pw1, m_conv_w_dw, m_conv_b_dw, m_conv_norm_g,
                          m_conv_w_pw2, m_conv_b_pw2, m_attn_w_qkv, m_attn_q_norm, m_attn_k_norm, m_attn_w_o)))
    v = dict(zip(_NAMES, (v_norm_g, v_ffn_w_in, v_ffn_w_out, v_conv_w_pw1, v_conv_b_pw1, v_conv_w_dw, v_conv_b_dw, v_conv_norm_g,
                          v_conv_w_pw2, v_conv_b_pw2, v_attn_w_qkv, v_attn_q_norm, v_attn_k_norm, v_attn_w_o)))
    T, D = x.shape[1:]
    F = ffn_w_out.shape[2] * N_CHIPS
    HE = attn_w_o.shape[1] * N_CHIPS
    cx, cy, cc = lax.axis_index("x"), lax.axis_index("y"), lax.axis_index("c")
    me = 2 * cx + cy
    scal = jnp.stack([cc, me]).astype(jnp.int32)

    ffn_in = lambda lf: _Sharded(f"ffn_w_in_{lf}", (2, D // 2, 2 * F), 0, 2, "ffn_w_in", lf)
    ffn_out = lambda lf: _Sharded(f"ffn_w_out_{lf}", (4, F // 4, D), 1, 0, "ffn_w_out", lf)
    items = [
        ffn_in(0), ffn_out(0),
        _Sharded("conv_w_pw1", (2, D // 2, 2 * D), 0, 2, "conv_w_pw1"), _Sharded("conv_w_pw2", (4, D // 4, D), 1, 0, "conv_w_pw2"),
        ffn_in(1), ffn_out(1), ffn_in(2), ffn_out(2),
        _Sharded("attn_w_qkv", (2, D // 2, 9 * HE), 0, 2, "attn_w_qkv"), _Sharded("attn_w_o", (4, HE // 4, D), 1, 0, "attn_w_o"),
        ffn_in(3), ffn_out(3),
    ]
    mat_shapes = {"ffn_w_in": (D, 2 * F), "ffn_w_out": (F, D), "conv_w_pw1": (D, 2 * D), "conv_w_pw2": (D, D),
                  "attn_w_qkv": (D, 9 * HE), "attn_w_o": (HE, D)}

    def as_shards(a, n):
        it = next(i for i in items if i.src == n)
        return a.reshape(((4,) if it.slab is not None else ()) + it.shard3)

    norm_full, dw_full = _gather_small([norm_g.reshape(6, D // 4), conv_w_dw.reshape(CONV_WIDTH, D // 4)])
    place = lambda its, after: [_cast_place(it, as_shards(w[it.src], it.src), scal, after) for it in its]
    ssems, rsems, fulls = _gather_start(items[:2], place(items[:2], ()), "gather_start_first")
    more = _gather_start(items[2:], place(items[2:], fulls[:1]), "gather_start_rest")
    ssems, rsems, fulls = [list(a) + list(b) for a, b in zip((ssems, rsems, fulls), more)]

    def block_weights(k, after):
        sel = slice(2 * k, 2 * k + 2)
        got, fsems, gsems = _gather_forward(items[sel], fulls[sel], ssems[sel], rsems[sel], after, f"gather_forward_{k}")
        done = _gather_finish(items[sel], got, fsems, gsems, f"gather_finish_{k}")
        return [a.reshape(mat_shapes[it.src]) for a, it in zip(done, items[sel])]

    res = {}
    flight = []

    def advance(entry, k, dx):
        stage, its, st = entry
        n = len(its)
        if stage == 1:
            ssem, rsem, arrs = st
            done = _exchange_wait(f"grads_pair_wait_{k}", arrs, ssem, rsem, _pair_copies(its), dx)
            sums = [_pair_add(it, g, r, scal) for it, g, r in zip(its, done[:n], done[n:])]
            land = [lax.empty((3,) + it.part3, BF16) for it in its]
            ssem, rsem, arrs, tok = _exchange_start(f"grads_chip_start_{k}", [p[1] for p in sums] + land, [3] * n, _chip_copies(its))
            return (2, its, (ssem, rsem, arrs, [p[0] for p in sums])), tok
        if stage == 2:
            ssem, rsem, arrs, p32 = st
            got = _exchange_wait(f"grads_chip_wait_{k}", arrs, ssem, rsem, _chip_copies(its), dx)[n:]
            red = [_chip_reduce(it, p, r, scal) for it, p, r in zip(its, p32, got)]
            ssem, rsem, arrs, tok = _exchange_start(f"grads_fill_start_{k}", red, [1] * n, _fill_copies(its))
            return (3, its, (ssem, rsem, arrs)), tok
        ssem, rsem, arrs = st
        for it, g in zip(its, _exchange_wait(f"grads_fill_wait_{k}", arrs, ssem, rsem, _fill_copies(its), dx)):
            nm = it.src
            res[nm] = _adam(it, g, as_shards(w[nm], nm), as_shards(m[nm], nm), as_shards(v[nm], nm), prev=res.get(nm))
        return None, None

    def step_flight(dx):
        toks, left = [], []
        for k, entry in flight:
            entry, tok = advance(entry, k, dx)
            if entry is not None:
                left.append((k, entry))
                toks.append(tok)
        flight[:] = left
        return toks

    def grads_ready(k, pairs, dx, part=None):
        its = items[2 * k:2 * k + 2] if part is None else items[2 * k + part:2 * k + part + 1]
        label = f"{k}" if part is None else f"{k}_{part}"
        toks = step_flight((dx,))
        g16 = [p.reshape(it.full3) for p, it in zip(pairs, its)]
        land = [lax.empty(it.pair3, BF16) for it in its]
        ssem, rsem, arrs, tok = _exchange_start(f"grads_pair_start_{label}", g16 + land, [1] * len(its), _pair_copies(its))
        flight.append((label, (1, its, (ssem, rsem, arrs))))
        return toks + [tok]

    sq, dx, grads = _local_step(x[0], loss_target[0], norm_full.reshape(2, 3, D), conv_b_pw1, dw_full, conv_b_dw, conv_norm_g,
                                conv_b_pw2, attn_q_norm[0], attn_k_norm[0], block_weights, grads_ready,
                                lambda dx: step_flight((dx,)))
    loss = lax.psum(0.5 * jnp.sum(sq) / D, ("x", "y", "c"))
    k_last, entry = flight.pop()
    entry, tok = advance(entry, k_last, (dx,))
    while flight:
        step_flight((dx, tok))

    out_g, out_d, out_m, out_v = {}, {}, {}, {}

    parts = [_rows8(grads[n], D) for n in _SMALL]
    tot = _allreduce_small(jnp.concatenate(parts, axis=0))
    sg, r0 = {}, 0
    for n, p in zip(_SMALL, parts):
        last = grads[n].shape[-1]
        g = tot[r0:r0 + grads[n].size // min(last, D), :min(last, D)].reshape(-1, last)
        r0 += p.shape[0]
        if n in ("norm_g", "conv_w_dw"):
            g = lax.dynamic_slice_in_dim(g, me * (D // 4), D // 4, axis=1)
        sg[n] = g
    flat = lambda a: a.reshape(-1, a.shape[-1])
    ds, ms, vs = _adam_small([sg[n] for n in _SMALL], [flat(w[n]) for n in _SMALL], [flat(m[n]) for n in _SMALL],
                             [flat(v[n]) for n in _SMALL])
    for i, n in enumerate(_SMALL):
        out_g[n], out_d[n], out_m[n], out_v[n] = (a.reshape(w[n].shape) for a in (sg[n], ds[i], ms[i], vs[i]))

    behind = tuple(r[1] for r in res.values()) + tuple(ds)
    while entry is not None:
        entry, _ = advance(entry, k_last, behind)
    for n in _BIG:
        out_g[n], out_d[n], out_m[n], out_v[n] = (a.reshape(w[n].shape) for a in res[n])

    return (loss, dx[None], *[out_g[n] for n in _NAMES], *[out_d[n] for n in _NAMES], *[out_m[n] for n in _NAMES],
            *[out_v[n] for n in _NAMES])
```

```python
import functools

import jax
import jax.numpy as jnp
from jax import lax
from jax.experimental import pallas as pl
from jax.experimental.pallas import tpu as pltpu

F32 = jnp.float32
BF16 = jnp.bfloat16
MESH = pl.DeviceIdType.MESH

NORM_EPS = 1e-6
CONV_WIDTH = 31
ATTN_GROUPS = ((128, 1), (512, 4), (2048, 16))
ATTN_BLOCK = 128
HEAD_DIM = 128
N_CHIPS = 4

ADAM_LR = 0.001
ADAM_B1 = 0.9
ADAM_B2 = 0.999
ADAM_EPS = 1e-08
ADAM_WD = 0.01
ADAM_STEP = 10

VMEM_LIMIT = 56 * 1024 * 1024
NT_DIMS = (((1,), (1,)), ((), ()))
TN_DIMS = (((0,), (0,)), ((), ()))


def _pick(n, pref, mult):
    t = (min(n, pref) // mult) * mult
    while t >= mult:
        if n % t == 0:
            return t
        t -= mult
    return n


def _call(body, name, grid, in_specs, out_specs, out_shape, scratch=(), aliases=None, prefetch=0, after=()):
    params = pltpu.CompilerParams(dimension_semantics=("arbitrary",) * len(grid), vmem_limit_bytes=VMEM_LIMIT)
    after = tuple(after)
    if after:
        inner, n_in = body, prefetch + len(in_specs)

        def body(*refs):
            return inner(*refs[:n_in], *refs[n_in + len(after):])

        in_specs = list(in_specs) + [pl.BlockSpec(memory_space=pl.ANY)] * len(after)
    if prefetch:
        spec = pltpu.PrefetchScalarGridSpec(
            num_scalar_prefetch=prefetch, grid=grid, in_specs=in_specs, out_specs=out_specs, scratch_shapes=list(scratch)
        )
        call = pl.pallas_call(body, name=name, grid_spec=spec, out_shape=out_shape, compiler_params=params,
                              input_output_aliases=aliases or {})
    else:
        call = pl.pallas_call(body, name=name, grid=grid, in_specs=in_specs, out_specs=out_specs, out_shape=out_shape,
                              scratch_shapes=list(scratch), compiler_params=params, input_output_aliases=aliases or {})
    return lambda *args: call(*args, *after)


def _sds(shape, dtype):
    return jax.ShapeDtypeStruct(shape, dtype)


def _sig(x):
    return 1.0 / (1.0 + jnp.exp(-x))


def _rstd(x):
    return lax.rsqrt(jnp.mean(x * x, axis=-1, keepdims=True) + NORM_EPS)


def _norm_bwd(dy, xhat, r, g):
    dxh = dy * g
    return r * (dxh - xhat * jnp.mean(dxh * xhat, axis=-1, keepdims=True))


def _dot(a, b):
    return jnp.dot(a, b, preferred_element_type=F32)


def _dot_nt(a, b):
    return lax.dot_general(a, b, NT_DIMS, preferred_element_type=F32)


def _dot_tn(a, b):
    return lax.dot_general(a, b, TN_DIMS, preferred_element_type=F32)


def _ffn_fwd(x, g, w_in, w_out, name, target=None):
    T, D = x.shape
    F = w_out.shape[0]
    tm = _pick(T, 256, 8)
    last = target is not None

    def body(x_ref, g_ref, wg_ref, wu_ref, wo_ref, *rest):
        xo_ref, gate_ref, up_ref = rest[last:last + 3]
        xv = x_ref[...]
        h = (xv * _rstd(xv) * g_ref[...]).astype(BF16)
        gate = _dot(h, wg_ref[...])
        up = _dot(h, wu_ref[...])
        gate_ref[...] = gate.astype(BF16)
        up_ref[...] = up.astype(BF16)
        a = (gate * _sig(gate) * up).astype(BF16)
        y = xv + 0.5 * _dot(a, wo_ref[...])
        if not last:
            xo_ref[...] = y
            return
        t_ref, sq_ref = rest[0], rest[4]

        @pl.when(pl.program_id(0) == 0)
        def _():
            sq_ref[...] = jnp.zeros_like(sq_ref)

        err = y - t_ref[...]
        xo_ref[...] = err * (1.0 / D)
        sq_ref[...] += jnp.sum(err * err, axis=0, keepdims=True)

    row = lambda i: (i, 0)
    fix = pl.BlockSpec((1, D), lambda i: (0, 0))
    held = lambda shape, k: pl.BlockSpec(shape, lambda i: (0, k), pipeline_mode=pl.Buffered(1))
    return _call(
        body, name, (T // tm,),
        [pl.BlockSpec((tm, D), row), fix, held((D, F), 0), held((D, F), 1), held((F, D), 0)] + [pl.BlockSpec((tm, D), row)] * last,
        [pl.BlockSpec((tm, D), row), pl.BlockSpec((tm, F), row), pl.BlockSpec((tm, F), row)] + [fix] * last,
        [_sds((T, D), F32), _sds((T, F), BF16), _sds((T, F), BF16)] + [_sds((1, D), F32)] * last,
    )(x, g, w_in, w_in, w_out, *([target] if last else []))


def _ffn_bwd(x, g, dy, gate, up, w_in, w_out, name, after=()):
    T, D = x.shape
    F = w_out.shape[0]

    def body_a(dy_ref, gate_ref, up_ref, wo_ref, dgate_ref, dup_ref, a_ref):
        dyb = (0.5 * dy_ref[...]).astype(BF16)
        gate = gate_ref[...].astype(F32)
        up = up_ref[...].astype(F32)
        sg = _sig(gate)
        sl = gate * sg
        a_ref[...] = (sl * up).astype(BF16)
        da = _dot_nt(dyb, wo_ref[...])
        dgate_ref[...] = (da * up * (sg * (1.0 + gate * (1.0 - sg)))).astype(BF16)
        dup_ref[...] = (da * sl).astype(BF16)

    ta = _pick(T, 256, 8)
    tile = pl.BlockSpec((ta, F), lambda i: (i, 0))
    dgate, dup, a = _call(
        body_a, name + "_hidden", (T // ta,),
        [pl.BlockSpec((ta, D), lambda i: (i, 0)), tile, tile,
         pl.BlockSpec((F, D), lambda i: (0, 0), pipeline_mode=pl.Buffered(1))],
        [tile, tile, tile],
        [_sds((T, F), BF16)] * 3, after=after,
    )(dy, gate, up, w_out)

    def body_b(x_ref, g_ref, dy_ref, dgate_ref, dup_ref, wg_ref, wu_ref, dx_ref, dg_ref, h_ref, dyb_ref):
        @pl.when(pl.program_id(0) == 0)
        def _():
            dg_ref[...] = jnp.zeros_like(dg_ref)

        xv = x_ref[...]
        r = _rstd(xv)
        xh = xv * r
        h_ref[...] = (xh * g_ref[...]).astype(BF16)
        dyb_ref[...] = (0.5 * dy_ref[...]).astype(BF16)
        dh = _dot_nt(dgate_ref[...], wg_ref[...]) + _dot_nt(dup_ref[...], wu_ref[...])
        dx_ref[...] = dy_ref[...] + _norm_bwd(dh, xh, r, g_ref[...])
        dg_ref[...] += jnp.sum(dh * xh, axis=0, keepdims=True)

    tb = _pick(T, 512, 8)
    row = lambda i: (i, 0)
    held = lambda k: pl.BlockSpec((D, F), lambda i: (0, k), pipeline_mode=pl.Buffered(1))
    dx, dg, h, dyb = _call(
        body_b, name, (T // tb,),
        [pl.BlockSpec((tb, D), row), pl.BlockSpec((1, D), lambda i: (0, 0)), pl.BlockSpec((tb, D), row),
         pl.BlockSpec((tb, F), row), pl.BlockSpec((tb, F), row), held(0), held(1)],
        [pl.BlockSpec((tb, D), row), pl.BlockSpec((1, D), lambda i: (0, 0)), pl.BlockSpec((tb, D), row), pl.BlockSpec((tb, D), row)],
        [_sds((T, D), F32), _sds((1, D), F32), _sds((T, D), BF16), _sds((T, D), BF16)],
    )(x, g, dy, dgate, dup, w_in, w_in)
    return dx, dg, dgate, dup, a, h, dyb


def _mm_tn(a, b, bm, bn, out_shape, out_block, out_map, scale, name):
    K, M = a.shape
    N = b.shape[1]

    def body(a_ref, b_ref, o_ref):
        o_ref[...] = (_dot_tn(a_ref[...].astype(BF16), b_ref[...].astype(BF16)) * scale).astype(BF16)

    in_specs = [pl.BlockSpec((K, bm), lambda mi, ni: (0, mi)), pl.BlockSpec((K, bn), lambda mi, ni: (0, ni))]
    return _call(body, name, (M // bm, N // bn), in_specs, [pl.BlockSpec(out_block, out_map)],
                 [_sds(out_shape, BF16)])(a, b)[0]


def _mm_tn_parts(a, bs, bn, cols, col0, name, prev=None, after=()):
    K, M = a.shape
    N = bs[0].shape[1]
    nt = N // bn
    ns = len(bs)

    def body(a_ref, *refs):
        o_ref = refs[-1]
        part = pl.program_id(0) // nt
        for s in range(ns):
            @pl.when(part == s)
            def _(s=s):
                o_ref[...] = _dot_tn(a_ref[...], refs[s][...]).astype(BF16)

    part_spec = lambda s: pl.BlockSpec((K, bn), lambda j: (0, jnp.clip(j - s * nt, 0, nt - 1)))
    in_specs = [pl.BlockSpec((K, M), lambda j: (0, 0))] + [part_spec(s) for s in range(ns)]
    args = [a, *bs]
    aliases = None
    if prev is not None:
        in_specs += [pl.BlockSpec(memory_space=pl.ANY)]
        args += [prev]
        aliases = {ns + 1: 0}
    return _call(body, name, (ns * nt,), in_specs, [pl.BlockSpec((M, bn), lambda j: (0, col0 // bn + j))],
                 [_sds((M, cols), BF16)], aliases=aliases, after=after)(*args)[0]


def _conv_pre(x, g, w1, b1, name):
    T, D = x.shape
    tm = _pick(T, 512, 8)

    def body(x_ref, g_ref, w_ref, b_ref, ag_ref, u_ref, h_ref):
        xv = x_ref[...]
        h = (xv * _rstd(xv) * g_ref[...]).astype(BF16)
        h_ref[...] = h
        ag = _dot(h, w_ref[...]) + b_ref[...]
        ag_ref[...] = ag.astype(BF16)
        u_ref[...] = ag[:, :D] * _sig(ag[:, D:])

    return _call(
        body, name, (T // tm,),
        [pl.BlockSpec((tm, D), lambda i: (i, 0)), pl.BlockSpec((1, D), lambda i: (0, 0)),
         pl.BlockSpec((D, 2 * D), lambda i: (0, 0)), pl.BlockSpec((1, 2 * D), lambda i: (0, 0))],
        [pl.BlockSpec((tm, 2 * D), lambda i: (i, 0)), pl.BlockSpec((tm, D), lambda i: (i, 0)),
         pl.BlockSpec((tm, D), lambda i: (i, 0))],
        [_sds((T, 2 * D), BF16), _sds((T, D), F32), _sds((T, D), BF16)],
    )(x, g, w1, b1)


_DW_PAD = 32
_DW_CHUNK = 256


def _dwconv(u, w, b, name):
    T, D = u.shape
    K = w.shape[0]
    ch = _pick(T, _DW_CHUNK, 8)
    lead = _DW_PAD - (K - 1)

    def body(u_ref, w_ref, b_ref, c_ref, ext):
        ext[pl.ds(0, _DW_PAD), :] = jnp.zeros((_DW_PAD, 128), F32)
        ext[pl.ds(_DW_PAD, T), :] = u_ref[...]
        for c0 in range(0, T, ch):
            acc = jnp.zeros((ch, 128), F32) + b_ref[...]
            for k in range(K):
                acc = acc + w_ref[pl.ds(k, 1), :] * ext[pl.ds(c0 + lead + k, ch), :]
            c_ref[pl.ds(c0, ch), :] = acc

    return _call(
        body, name, (D // 128,),
        [pl.BlockSpec((T, 128), lambda i: (0, i)), pl.BlockSpec((K, 128), lambda i: (0, i)),
         pl.BlockSpec((1, 128), lambda i: (0, i))],
        [pl.BlockSpec((T, 128), lambda i: (0, i))],
        [_sds((T, D), F32)],
        scratch=[pltpu.VMEM((T + _DW_PAD, 128), F32)],
    )(u, w, b)[0]


def _dwconv_bwd(dc, u, w, name):
    T, D = u.shape
    K = w.shape[0]
    ch = _pick(T, _DW_CHUNK, 8)

    def body(dc_ref, u_ref, w_ref, du_ref, dw_ref, db_ref, dext):
        dext[pl.ds(0, T), :] = dc_ref[...]
        dext[pl.ds(T, _DW_PAD), :] = jnp.zeros((_DW_PAD, 128), F32)
        dws =[jnp.zeros((8, 128), F32) for _ in range(K)]
        dbs = jnp.zeros((8, 128), F32)
        for c0 in range(0, T, ch):
            uv = u_ref[pl.ds(c0, ch), :]
            dbs = dbs + jnp.sum(dc_ref[pl.ds(c0, ch), :].reshape(ch // 8, 8, 128), axis=0)
            acc = jnp.zeros((ch, 128), F32)
            for k in range(K):
                win = dext[pl.ds(c0 + (K - 1) - k, ch), :]
                acc = acc + w_ref[pl.ds(k, 1), :] * win
                dws[k] = dws[k] + jnp.sum((win * uv).reshape(ch // 8, 8, 128), axis=0)
            du_ref[pl.ds(c0, ch), :] = acc
        for k in range(K):
            dw_ref[pl.ds(k, 1), :] = jnp.sum(dws[k], axis=0, keepdims=True)
        db_ref[...] = jnp.sum(dbs, axis=0, keepdims=True)

    return _call(
        body, name, (D // 128,),
        [pl.BlockSpec((T, 128), lambda i: (0, i)), pl.BlockSpec((T, 128), lambda i: (0, i)),
         pl.BlockSpec((K, 128), lambda i: (0, i))],
        [pl.BlockSpec((T, 128), lambda i: (0, i)), pl.BlockSpec((K, 128), lambda i: (0, i)),
         pl.BlockSpec((1, 128), lambda i: (0, i))],
        [_sds((T, D), F32), _sds((K, D), F32), _sds((1, D), F32)],
        scratch=[pltpu.VMEM((T + _DW_PAD, 128), F32)],
    )(dc, u, w)


def _conv_post(c, x, ng, w2, b2, name):
    T, D = x.shape
    tm = _pick(T, 512, 8)

    def body(c_ref, x_ref, ng_ref, w_ref, b_ref, xo_ref, s_ref):
        cv = c_ref[...]
        n = cv * _rstd(cv) * ng_ref[...]
        s = (n * _sig(n)).astype(BF16)
        s_ref[...] = s
        xo_ref[...] = x_ref[...] + _dot(s, w_ref[...]) + b_ref[...]

    row = lambda i: (i, 0)
    fix = lambda i: (0, 0)
    return _call(
        body, name, (T // tm,),
        [pl.BlockSpec((tm, D), row), pl.BlockSpec((tm, D), row), pl.BlockSpec((1, D), fix),
         pl.BlockSpec((D, D), fix), pl.BlockSpec((1, D), fix)],
        [pl.BlockSpec((tm, D), row), pl.BlockSpec((tm, D), row)],
        [_sds((T, D), F32), _sds((T, D), BF16)],
    )(c, x, ng, w2, b2)


def _conv_post_bwd(dy, c, ng, w2, name, after=()):
    T, D = dy.shape
    tm = _pick(T, 512, 8)

    def body(dy_ref, c_ref, ng_ref, w_ref, dc_ref, dng_ref, db_ref):
        @pl.when(pl.program_id(0) == 0)
        def _():
            dng_ref[...] = jnp.zeros_like(dng_ref)
            db_ref[...] = jnp.zeros_like(db_ref)

        dyv = dy_ref[...]
        ds = _dot_nt(dyv.astype(BF16), w_ref[...])
        cv = c_ref[...]
        r = _rstd(cv)
        ch = cv * r
        n = ch * ng_ref[...]
        sg = _sig(n)
        dn = ds * (sg * (1.0 + n * (1.0 - sg)))
        dc_ref[...] = _norm_bwd(dn, ch, r, ng_ref[...])
        dng_ref[...] += jnp.sum(dn * ch, axis=0, keepdims=True)
        db_ref[...] += jnp.sum(dyv, axis=0, keepdims=True)

    row = lambda i: (i, 0)
    fix = lambda i: (0, 0)
    return _call(
        body, name, (T // tm,),
        [pl.BlockSpec((tm, D), row), pl.BlockSpec((tm, D), row), pl.BlockSpec((1, D), fix), pl.BlockSpec((D, D), fix)],
        [pl.BlockSpec((tm, D), row), pl.BlockSpec((1, D), fix), pl.BlockSpec((1, D), fix)],
        [_sds((T, D), F32), _sds((1, D), F32), _sds((1, D), F32)], after=after,
    )(dy, c, ng, w2)


def _conv_pre_bwd(du, ag, x, g, dy, w1, name):
    T, D = x.shape
    tm = _pick(T, 512, 8)

    def body(du_ref, ag_ref, x_ref, g_ref, dy_ref, w_ref, dx_ref, dg_ref, dag_ref, db_ref):
        @pl.when(pl.program_id(0) == 0)
        def _():
            dg_ref[...] = jnp.zeros_like(dg_ref)
            db_ref[...] = jnp.zeros_like(db_ref)

        duv = du_ref[...]
        a = ag_ref[:, :D].astype(F32)
        gt = ag_ref[:, D:].astype(F32)
        sg = _sig(gt)
        da = duv * sg
        dgt = duv * a * sg * (1.0 - sg)
        db_ref[:, :D] += jnp.sum(da, axis=0, keepdims=True)
        db_ref[:, D:] += jnp.sum(dgt, axis=0, keepdims=True)
        dab = da.astype(BF16)
        dgb = dgt.astype(BF16)
        dag_ref[:, :D] = dab
        dag_ref[:, D:] = dgb
        dh = _dot_nt(dab, w_ref[:, :D]) + _dot_nt(dgb, w_ref[:, D:])
        xv = x_ref[...]
        r = _rstd(xv)
        xh = xv * r
        dx_ref[...] = dy_ref[...] + _norm_bwd(dh, xh, r, g_ref[...])
        dg_ref[...] += jnp.sum(dh * xh, axis=0, keepdims=True)

    row = lambda i: (i, 0)
    fix = lambda i: (0, 0)
    return _call(
        body, name, (T // tm,),
        [pl.BlockSpec((tm, D), row), pl.BlockSpec((tm, 2 * D), row), pl.BlockSpec((tm, D), row), pl.BlockSpec((1, D), fix),
         pl.BlockSpec((tm, D), row), pl.BlockSpec((D, 2 * D), fix)],
        [pl.BlockSpec((tm, D), row), pl.BlockSpec((1, D), fix), pl.BlockSpec((tm, 2 * D), row),
         pl.BlockSpec((1, 2 * D), fix)],
        [_sds((T, D), F32), _sds((1, D), F32), _sds((T, 2 * D), BF16), _sds((1, 2 * D), F32)],
    )(du, ag, x, g, dy, w1)


def _row_sums(a):
    return _dot(a.astype(BF16), jnp.ones((a.shape[1], 128), BF16))


def _head_rstd(x):
    return lax.rsqrt(_row_sums(x * x) * (1.0 / x.shape[1]) + NORM_EPS)


def _attn_qkv(x, g, wqkv, qn, kn, name):
    T, D = x.shape
    N = wqkv.shape[1]
    tn = N // 9
    E = HEAD_DIM
    tm = _pick(T, 256, 8)
    ng = qn.shape[0]

    def body(x_ref, g_ref, w_ref, qn_ref, kn_ref, o_ref, a_ref, h_ref):
        xv = x_ref[...]
        h = (xv * _rstd(xv) * g_ref[...]).astype(BF16)
        h_ref[...] = h
        for j in range(9):
            cs = slice(j * tn, (j + 1) * tn)
            res = _dot(h, w_ref[:, cs])
            o_ref[:, cs] = res.astype(BF16)
            if j % 3 == 2:
                a_ref[:, cs] = res.astype(BF16)
                continue
            grp = j // 3
            fac = qn_ref[grp:grp + 1, :] * kn_ref[grp:grp + 1, :] * (E ** -0.5) if j % 3 == 0 else None
            for h_i in range(tn // E):
                hs = slice(h_i * E, (h_i + 1) * E)
                xh = res[:, hs]
                hat = xh * _head_rstd(xh)
                a_ref[:, j * tn + h_i * E:j * tn + (h_i + 1) * E] = (hat if fac is None else hat * fac).astype(BF16)

    row = lambda i: (i, 0)
    fix = lambda i: (0, 0)
    return _call(
        body, name, (T // tm,),
        [pl.BlockSpec((tm, D), row), pl.BlockSpec((1, D), fix), pl.BlockSpec((D, N), fix, pipeline_mode=pl.Buffered(1)),
         pl.BlockSpec((ng, E), fix), pl.BlockSpec((ng, E), fix)],
        [pl.BlockSpec((tm, N), row), pl.BlockSpec((tm, N), row), pl.BlockSpec((tm, D), row)],
        [_sds((T, N), BF16), _sds((T, N), BF16), _sds((T, D), BF16)],
    )(x, g, wqkv, qn, kn)


def _band_mask(q, steps, nblk):
    i = lax.broadcasted_iota(jnp.int32, (q, 2 * q), 0)
    j = lax.broadcasted_iota(jnp.int32, (q, 2 * q), 1)
    diff = q + i - j
    first_key = jnp.where(nblk > 0, 0, q)
    return (diff >= 0) & (diff <= steps) & (j >= first_key)


def _per_row(blk, width):
    e = blk.shape[1]
    if width % e == 0:
        return jnp.concatenate([blk] * (width // e), axis=1)
    return jnp.broadcast_to(blk[:, :1], (blk.shape[0], width))


def _streams(a, dil, to_streams, name, col0=0, ncols=None):
    T, C = a.shape
    ncols = C if ncols is None else ncols
    Q = ATTN_BLOCK
    run = Q * dil
    reps = max(1, min(2048 // run, T // run))
    while T % (run * reps):
        reps -= 1
    rows = run * reps
    cw = _pick(ncols, 512, 128)
    ns = cw // 128

    def body(a_ref, o_ref, scr):
        for s in range(ns):
            ls = slice(s * 128, (s + 1) * 128)
            slab = scr.at[s]
            if to_streams:
                slab[...] = a_ref[:, ls].astype(F32)
                for u in range(reps):
                    for r in range(dil):
                        o_ref[pl.ds(u * run + r * Q, Q), ls] = slab[pl.ds(u * run + r, Q, stride=dil), :].astype(a.dtype)
            else:
                for u in range(reps):
                    for r in range(dil):
                        slab[pl.ds(u * run + r, Q, stride=dil), :] = a_ref[pl.ds(u * run + r * Q, Q), ls].astype(F32)
                o_ref[:, ls] = slab[...].astype(a.dtype)

    return _call(
        body, name, (T // rows, ncols // cw),
        [pl.BlockSpec((rows, cw), lambda i, j: (i, col0 // cw + j))],
        [pl.BlockSpec((rows, cw), lambda i, j: (i, j))],
        [_sds((T, ncols), a.dtype)],
        scratch=[pltpu.VMEM((ns, rows, 128), F32)],
    )(a)[0]


def _attn_fwd(qkv, base, HE, window, dil, name):
    T = qkv.shape[0]
    H = HE // HEAD_DIM
    E = HEAD_DIM
    Q = ATTN_BLOCK
    nb = T // dil // Q
    steps = window // dil

    def body(q_ref, kc_ref, kp_ref, vc_ref, vp_ref, o_ref, l_ref):
        n = pl.program_id(1)
        valid = _band_mask(Q, steps, n)
        ones = jnp.ones((2 * Q, E), BF16)
        outs, lses = [], []
        for h in range(H):
            hs = slice(h * E, (h + 1) * E)
            k2 = jnp.concatenate([kp_ref[:, hs], kc_ref[:, hs]], axis=0)
            v2 = jnp.concatenate([vp_ref[:, hs], vc_ref[:, hs]], axis=0)
            s = jnp.where(valid, _dot_nt(q_ref[:, hs], k2), -1e30)
            m = jnp.max(s, axis=-1, keepdims=True)
            p = jnp.exp(s - m).astype(BF16)
            acc = _dot(p, jnp.concatenate([v2, ones], axis=1))
            l = acc[:, E:]
            outs.append((acc[:, :E] * (1.0 / l)).astype(o_ref.dtype))
            lses.append(m + jnp.log(l))
        o_ref[...] = jnp.concatenate(outs, axis=1)
        l_ref[...] = jnp.concatenate(lses, axis=1)

    blk = lambda s, back: pl.BlockSpec((Q, HE), lambda r, n: (jnp.maximum(n - back, 0) * dil + r, base + s))
    out = pl.BlockSpec((Q, HE), lambda r, n: (n * dil + r, 0))
    return _call(
        body, name, (dil, nb),
        [blk(0, 0), blk(1, 0), blk(1, 1), blk(2, 0), blk(2, 1)],
        [out, out],
        [_sds((T, HE), BF16), _sds((T, HE), F32)],
    )(qkv, qkv, qkv, qkv, qkv)


def _attn_merge(os, lses, x, wo, name):
    T, D = x.shape
    HE = wo.shape[0]
    tm = _pick(T, 512, 8)
    ng = len(os)

    def body(*refs):
        o_refs = refs[:ng]
        l_refs = refs[ng:2 * ng]
        x_ref, w_ref, xo_ref, om_ref, lt_ref = refs[2 * ng:]
        ls = [r[...] for r in l_refs]
        m = functools.reduce(jnp.maximum, ls)
        es = [jnp.exp(l - m) for l in ls]
        tot = functools.reduce(lambda a, b: a + b, es)
        inv = 1.0 / tot
        om = functools.reduce(lambda a, b: a + b, [e * inv * r[...] for e, r in zip(es, o_refs)])
        omb = om.astype(BF16)
        om_ref[...] = omb
        lt_ref[...] = m + jnp.log(tot)
        xo_ref[...] = x_ref[...] + _dot(omb, w_ref[...])

    row = lambda i: (i, 0)
    fix = lambda i: (0, 0)
    return _call(
        body, name, (T // tm,),
        [pl.BlockSpec((tm, HE), row)] * (2 * ng) + [pl.BlockSpec((tm, D), row), pl.BlockSpec((HE, D), fix)],
        [pl.BlockSpec((tm, D), row), pl.BlockSpec((tm, HE), row), pl.BlockSpec((tm, HE), row)],
        [_sds((T, D), F32), _sds((T, HE), BF16), _sds((T, HE), F32)],
    )(*os, *lses, x, wo)


def _attn_out_bwd(dy, om, wo, name, after=()):
    T, D = dy.shape
    HE = wo.shape[0]
    E = HEAD_DIM
    tm = _pick(T, 512, 8)

    def body(dy_ref, om_ref, w_ref, dom_ref, dl_ref):
        dom = _dot_nt(dy_ref[...].astype(BF16), w_ref[...])
        dom_ref[...] = dom.astype(BF16)
        prod = dom * om_ref[...].astype(F32)
        for h in range(HE // E):
            hs = slice(h * E, (h + 1) * E)
            dl_ref[:, hs] = jnp.broadcast_to(jnp.sum(prod[:, hs], axis=-1, keepdims=True), (tm, E))

    row = lambda i: (i, 0)
    return _call(
        body, name, (T // tm,),
        [pl.BlockSpec((tm, D), row), pl.BlockSpec((tm, HE), row), pl.BlockSpec((HE, D), lambda i: (0, 0))],
        [pl.BlockSpec((tm, HE), row), pl.BlockSpec((tm, HE), row)],
        [_sds((T, HE), BF16), _sds((T, HE), F32)], after=after,
    )(dy, om, wo)


def _attn_bwd(qkv, base, HE, dom, lse, delta, window, dil, name):
    T = qkv.shape[0]
    H = HE // HEAD_DIM
    E = HEAD_DIM
    Q = ATTN_BLOCK
    nb = T // dil // Q
    steps = window // dil

    def body(q_ref, kc_ref, kp_ref, vc_ref, vp_ref, do_ref, l_ref, dl_ref, dq_ref, dk_ref, dv_ref, ck_sc, cv_sc):
        n = pl.program_id(1)

        @pl.when(n == 0)
        def _():
            ck_sc[...] = jnp.zeros_like(ck_sc)
            cv_sc[...] = jnp.zeros_like(cv_sc)

        @pl.when(n < nb)
        def _():
            valid = _band_mask(Q, steps, n)
            ck_old = ck_sc[...]
            cv_old = cv_sc[...]
            dqs, dks, dvs = [], [], []
            for h in range(H):
                hs = slice(h * E, (h + 1) * E)
                q = q_ref[:, hs]
                do = do_ref[:, hs]
                k2 = jnp.concatenate([kp_ref[:, hs], kc_ref[:, hs]], axis=0)
                v2 = jnp.concatenate([vp_ref[:, hs], vc_ref[:, hs]], axis=0)
                p = jnp.where(valid, jnp.exp(_dot_nt(q, k2) - _per_row(l_ref[:, hs], 2 * Q)), 0.0)
                ds = (p * (_dot_nt(do, v2) - _per_row(dl_ref[:, hs], 2 * Q))).astype(BF16)
                dqs.append(_dot(ds, k2).astype(BF16))
                dks.append(_dot_tn(q, ds).T)
                dvs.append(_dot_tn(do, p.astype(BF16)).T)
            cat = lambda parts: jnp.concatenate(parts, axis=1)
            dq_ref[...] = cat(dqs)
            dk_ref[...] = (ck_old + cat([d[:Q] for d in dks])).astype(BF16)
            dv_ref[...] = (cv_old + cat([d[:Q] for d in dvs])).astype(BF16)
            ck_sc[...] = cat([d[Q:] for d in dks])
            cv_sc[...] = cat([d[Q:] for d in dvs])

        @pl.when(n == nb)
        def _():
            dk_ref[...] = ck_sc[...].astype(BF16)
            dv_ref[...] = cv_sc[...].astype(BF16)

    nq = lambda n: jnp.minimum(n, nb - 1)
    blk = lambda s, back: pl.BlockSpec((Q, HE), lambda r, n: (jnp.maximum(nq(n) - back, 0) * dil + r, base + s))
    qblk = pl.BlockSpec((Q, HE), lambda r, n: (nq(n) * dil + r, 0))
    kblk = pl.BlockSpec((Q, HE), lambda r, n: (jnp.maximum(n - 1, 0) * dil + r, 0))
    return _call(
        body, name, (dil, nb + 1),
        [blk(0, 0), blk(1, 0), blk(1, 1), blk(2, 0), blk(2, 1), qblk, qblk, qblk],
        [qblk, kblk, kblk],
        [_sds((T, HE), BF16)] * 3,
        scratch=[pltpu.VMEM((Q, HE), F32), pltpu.VMEM((Q, HE), F32)],
    )(qkv, qkv, qkv, qkv, qkv, dom, lse, delta)


def _qk_norm_bwd(dq, dk, qkv, base, HE, gq, gk, name):
    T = dq.shape[0]
    E = HEAD_DIM
    tm = _pick(T, 512, 8)
    scale = E ** -0.5

    def body(dq_ref, dk_ref, q_ref, k_ref, gq_ref, gk_ref, oq_ref, ok_ref, dgq_ref, dgk_ref):
        @pl.when(pl.program_id(0) == 0)
        def _():
            dgq_ref[...] = jnp.zeros_like(dgq_ref)
            dgk_ref[...] = jnp.zeros_like(dgk_ref)

        gqv = gq_ref[...]
        gkv = gk_ref[...]
        c = gqv * gkv * scale
        dc = jnp.zeros((1, E), F32)
        for h in range(HE // E):
            hs = slice(h * E, (h + 1) * E)
            q = q_ref[:, hs].astype(F32)
            rq = _head_rstd(q)
            qh = q * rq
            a = dq_ref[:, hs].astype(F32)
            dc = dc + jnp.sum(a * qh, axis=0, keepdims=True)
            dqh = a * c
            oq_ref[:, hs] = (rq * (dqh - qh * (_row_sums(dqh * qh) * (1.0 / E)))).astype(BF16)
            k = k_ref[:, hs].astype(F32)
            rk = _head_rstd(k)
            kh = k * rk
            b = dk_ref[:, hs].astype(F32)
            ok_ref[:, hs] = (rk * (b - kh * (_row_sums(b * kh) * (1.0 / E)))).astype(BF16)
        dgq_ref[...] += dc * (gkv * scale)
        dgk_ref[...] += dc * (gqv * scale)

    row = lambda i: (i, 0)
    vec = pl.BlockSpec((1, E), lambda i: (0, 0))
    return _call(
        body, name, (T // tm,),
        [pl.BlockSpec((tm, HE), row), pl.BlockSpec((tm, HE), row), pl.BlockSpec((tm, HE), lambda i: (i, base)),
         pl.BlockSpec((tm, HE), lambda i: (i, base + 1)), vec, vec],
        [pl.BlockSpec((tm, HE), row), pl.BlockSpec((tm, HE), row), vec, vec],
        [_sds((T, HE), BF16), _sds((T, HE), BF16), _sds((1, E), F32), _sds((1, E), F32)],
    )(dq, dk, qkv, qkv, gq, gk)


def _attn_qkv_bwd(dqkv, x, g, dy, wqkv, name):
    T, D = x.shape
    HE = wqkv.shape[1] // 9
    tm = _pick(T, 256, 8)

    def body(*refs):
        d_refs = refs[:9]
        x_ref, g_ref, dy_ref, w_ref, dx_ref, dg_ref = refs[9:]

        @pl.when(pl.program_id(0) == 0)
        def _():
            dg_ref[...] = jnp.zeros_like(dg_ref)

        dh = _dot_nt(d_refs[0][...], w_ref[:, :HE])
        for s in range(1, 9):
            dh = dh + _dot_nt(d_refs[s][...], w_ref[:, s * HE:(s + 1) * HE])
        xv = x_ref[...]
        r = _rstd(xv)
        xh = xv * r
        dx_ref[...] = dy_ref[...] + _norm_bwd(dh, xh, r, g_ref[...])
        dg_ref[...] += jnp.sum(dh * xh, axis=0, keepdims=True)

    row = lambda i: (i, 0)
    fix = lambda i: (0, 0)
    return _call(
        body, name, (T // tm,),
        [pl.BlockSpec((tm, HE), row)] * 9 + [pl.BlockSpec((tm, D), row), pl.BlockSpec((1, D), fix), pl.BlockSpec((tm, D), row),
                                           pl.BlockSpec((D, 9 * HE), fix, pipeline_mode=pl.Buffered(1))],
        [pl.BlockSpec((tm, D), row), pl.BlockSpec((1, D), fix)],
        [_sds((T, D), F32), _sds((1, D), F32)],
    )(*dqkv, x, g, dy, wqkv)


def _local_step(x, target, norm_g, b_pw1, w_dw, b_dw, cng, b_pw2, qn, kn, block_weights, grads_ready, grads_tick):
    T, D = x.shape
    ng = lambda l, k: norm_g[l, k][None, :]
    bn = _pick(D, 256, 128)

    w_in, w_out = [None] * 4, [None] * 4
    w_in[0], w_out[0] = block_weights(0, x)
    x1, *gu0 = _ffn_fwd(x, ng(0, 0), w_in[0], w_out[0], "ffn_fwd_0")
    pw1, pw2 = block_weights(1, x1)
    ag, u, hc = _conv_pre(x1, ng(0, 1), pw1, b_pw1, "conv_pre")
    c = _dwconv(u, w_dw, b_dw, "dwconv")
    x2, s = _conv_post(c, x1, cng, pw2, b_pw2, "conv_post")
    w_in[1], w_out[1] = block_weights(2, x2)
    x3, *gu1 = _ffn_fwd(x2, ng(0, 2), w_in[1], w_out[1], "ffn_fwd_1")
    w_in[2], w_out[2] = block_weights(3, x3)
    x4, *gu2 = _ffn_fwd(x3, ng(1, 0), w_in[2], w_out[2], "ffn_fwd_2")
    wqkv, wo = block_weights(4, x4)
    HE = wo.shape[0]
    F = w_out[0].shape[0]
    bf = _pick(F, 256, 128)
    bh = _pick(HE, 512, 128)
    qkv, att, ha = _attn_qkv(x4, ng(1, 1), wqkv, qn, kn, "attn_qkv")
    qkv_s = [(att, 3 * gi) if dil == 1 else (_streams(att, dil, True, f"qkv_streams_{gi}", 3 * gi * HE, 3 * HE), 0)
             for gi, (_, dil) in enumerate(ATTN_GROUPS)]
    tokens = lambda a, dil, name: a if dil == 1 else _streams(a, dil, False, name)
    streams = lambda a, dil, name: a if dil == 1 else _streams(a, dil, True, name)
    os, lses = [], []
    for gi, (window, dil) in enumerate(ATTN_GROUPS):
        o, l = _attn_fwd(*qkv_s[gi], HE, window, dil, f"attn_fwd_{gi}")
        os.append(tokens(o, dil, f"o_tokens_{gi}"))
        lses.append(tokens(l, dil, f"lse_tokens_{gi}"))
    x5, om, lse = _attn_merge(os, lses, x4, wo, "attn_merge")
    w_in[3], w_out[3] = block_weights(5, x5)
    dy, *gu3, sq = _ffn_fwd(x5, ng(1, 2), w_in[3], w_out[3], "ffn_fwd_3", target=target)

    grads = {"ffn_w_in": [None] * 4, "ffn_w_out": [None] * 4}
    dnorm = [[None] * 3 for _ in range(2)]

    def ffn_back(lf, k, xin, gvec, dy, gu, after):
        dx, dg, dgate, dup, a, h, dyb = _ffn_bwd(xin, gvec, dy, gu[0], gu[1], w_in[lf], w_out[lf], f"ffn_bwd_{lf}", after=after)
        grads["ffn_w_in"][lf] = _mm_tn_parts(h, [dgate, dup], bf, 2 * F, 0, f"ffn_dw_in_{lf}",
                                             after=grads_tick(dx) if lf == 0 else ())
        grads["ffn_w_out"][lf] = _mm_tn(a, dyb, bf, D, (F, D), (bf, D), lambda mi, ni: (mi, 0), 1.0, f"ffn_dw_out_{lf}")
        return dx, dg, grads_ready(k, (grads["ffn_w_in"][lf], grads["ffn_w_out"][lf]), dx)

    dx, dnorm[1][2], tok = ffn_back(3, 5, x5, ng(1, 2), dy, gu3, ())
    dom, delta = _attn_out_bwd(dx, om, wo, "attn_out_bwd", after=tok)
    grads["attn_w_o"] = _mm_tn(om, dx, HE, bn, (HE, D), (HE, bn), lambda mi, ni: (0, ni), 1.0, "attn_dw_o")
    dqkv, dgq, dgk = [], [], []
    for gi, (window, dil) in enumerate(ATTN_GROUPS):
        ds = _attn_bwd(*qkv_s[gi], HE, streams(dom, dil, f"dom_streams_{gi}"), streams(lse, dil, f"lse_streams_{gi}"),
                       streams(delta, dil, f"delta_streams_{gi}"), window, dil, f"attn_bwd_{gi}")
        dq, dk, dv = [tokens(d, dil, f"d{nm}_tokens_{gi}") for d, nm in zip(ds, "qkv")]
        dq, dk, a_, b_ = _qk_norm_bwd(dq, dk, qkv, 3 * gi, HE, qn[gi][None, :], kn[gi][None, :], f"qk_norm_bwd_{gi}")
        dqkv += [dq, dk, dv]
        dgq.append(a_)
        dgk.append(b_)
    grads["attn_q_norm"] = jnp.concatenate(dgq, axis=0)
    grads["attn_k_norm"] = jnp.concatenate(dgk, axis=0)
    d_qkv = None
    for gi in range(len(ATTN_GROUPS)):
        d_qkv = _mm_tn_parts(ha, dqkv[3 * gi:3 * gi + 3], bh, 9 * HE, 3 * gi * HE, f"attn_dw_qkv_{gi}", prev=d_qkv)
    grads["attn_w_qkv"] = d_qkv
    dx, dnorm[1][1] = _attn_qkv_bwd(dqkv, x4, ng(1, 1), dx, wqkv, "attn_qkv_bwd")
    tok = grads_ready(4, (grads["attn_w_qkv"], grads["attn_w_o"]), dx)
    dx, dnorm[1][0], tok = ffn_back(2, 3, x3, ng(1, 0), dx, gu2, tok)

    dx, dnorm[0][2], tok = ffn_back(1, 2, x2, ng(0, 2), dx, gu1, tok)
    dc, grads["conv_norm_g"], grads["conv_b_pw2"] = _conv_post_bwd(dx, c, cng, pw2, "conv_post_bwd", after=tok)
    grads["conv_w_pw2"] = _mm_tn(s, dx, D, bn, (D, D), (D, bn), lambda mi, ni: (0, ni), 1.0, "conv_dw_pw2")
    du, grads["conv_w_dw"], grads["conv_b_dw"] = _dwconv_bwd(dc, u, w_dw, "dwconv_bwd")
    dx, dnorm[0][1], dag, grads["conv_b_pw1"] = _conv_pre_bwd(du, ag, x1, ng(0, 1), dx, pw1, "conv_pre_bwd")
    grads["conv_w_pw1"] = _mm_tn(hc, dag, D, 2 * bn, (D, 2 * D), (D, 2 * bn), lambda mi, ni: (0, ni), 1.0, "conv_dw_pw1")
    tok = grads_ready(1, (grads["conv_w_pw1"], grads["conv_w_pw2"]), dx)
    dx, dnorm[0][0], _ = ffn_back(0, 0, x, ng(0, 0), dx, gu0, tok)

    grads["norm_g"] = jnp.concatenate([jnp.concatenate(r, axis=0)[None] for r in dnorm], axis=0)
    return sq, dx, grads


class _Sharded:
    def __init__(self, name, full3, half_axis, shard_axis, src, slab=None):
        self.name, self.full3, self.half_axis, self.shard_axis = name, tuple(full3), half_axis, shard_axis
        self.src, self.slab = src, slab

    def _cut(self, shape, axis, parts):
        s = list(shape)
        s[axis] //= parts
        return tuple(s)

    @property
    def shard3(self):
        return self._cut(self.full3, self.shard_axis, N_CHIPS)

    @property
    def pair3(self):
        return self._cut(self.full3, self.half_axis, 2)

    @property
    def part3(self):
        return self._cut(self.shard3, self.half_axis, 2)

    @staticmethod
    def _slice(ref, axis, idx, parts):
        n = ref.shape[axis] // parts
        start = idx * n
        minor = len(ref.shape) - 1 - axis
        if minor < 2 and not isinstance(start, int):
            start = pl.multiple_of(start, 128 if minor == 0 else (16 if n % 16 == 0 else 8))
        sl = [slice(None)] * len(ref.shape)
        sl[axis] = pl.ds(start, n)
        return ref.at[tuple(sl)]

    def half(self, ref, h):
        return self._slice(ref, self.half_axis, h, 2)

    def shard(self, ref, j):
        return self._slice(ref, self.shard_axis, j, N_CHIPS)


def _place():
    x, y, c = lax.axis_index("x"), lax.axis_index("y"), lax.axis_index("c")
    return x, y, c, 2 * x + y


_RELS = (1, 2, 3)


def _peer(x, y, rel):
    px = 1 - x if rel & 2 else x
    py = 1 - y if rel & 1 else y
    return px, py, 2 * px + py


ANY = pl.BlockSpec(memory_space=pl.ANY)


def _comm_call(body, name, n_in, out_shape, n_sems, aliases=None):
    return pl.pallas_call(
        body, name=name, in_specs=[ANY] * n_in, out_specs=[ANY] * len(out_shape), out_shape=out_shape,
        scratch_shapes=[pltpu.SemaphoreType.DMA((n,)) for n in n_sems],
        input_output_aliases=aliases or {},
        compiler_params=pltpu.CompilerParams(has_side_effects=True),
    )


def _remote(src, dst, send_sem, recv_sem, dev):
    return pltpu.make_async_remote_copy(src_ref=src, dst_ref=dst, send_sem=send_sem, recv_sem=recv_sem, device_id=dev,
                                        device_id_type=MESH)


def _gather_small(small_shards):
    ns = len(small_shards)

    def body(*refs):
        ins, outs = refs[:ns], refs[ns:2 * ns]
        lsem, ssem, rsem = refs[2 * ns:]
        x, y, c, me = _place()
        cols = lambda ref, j: _Sharded._slice(ref, 1, j, N_CHIPS)
        local = [pltpu.make_async_copy(ins[i], cols(outs[i], me), lsem.at[i]) for i in range(ns)]
        sends = []
        for i in range(ns):
            for k, rel in enumerate(_RELS):
                px, py, _ = _peer(x, y, rel)
                sends.append(_remote(ins[i], cols(outs[i], me), ssem.at[3 * i + k], rsem.at[3 * i + k], (px, py, c)))
        for cp in local + sends:
            cp.start()
        for i in range(ns):
            for k, rel in enumerate(_RELS):
                _, _, pj = _peer(x, y, rel)
                got = cols(outs[i], pj)
                _remote(got, got, ssem.at[3 * i + k], rsem.at[3 * i + k], (x, y, c)).wait_recv()
        for cp in sends:
            cp.wait_send()
        for cp in local:
            cp.wait()

    out_shape = [_sds((s.shape[0], s.shape[1] * N_CHIPS), F32) for s in small_shards]
    return _comm_call(body, "gather_small", ns, out_shape, [ns, 3 * ns, 3 * ns])(*small_shards)


HBM = pl.BlockSpec(memory_space=pltpu.HBM)
SEM = pl.BlockSpec(memory_space=pltpu.SEMAPHORE)
DATAFLOW = pltpu.SideEffectType.DATAFLOW_SIDE_EFFECTING


def _in_hbm(a):
    return pltpu.with_memory_space_constraint(a, pltpu.HBM)


def _cast_place(it, shard, scal, after=()):
    a_n, r_n, c_n = it.shard3
    tr = _pick(r_n, 256, 16)
    sa = it.shard_axis

    def body(sc_ref, s_ref, o_ref):
        o_ref[...] = s_ref[...].astype(BF16)

    if it.slab is None:
        src = pl.BlockSpec((1, tr, c_n), lambda a, rb, sc: (a, rb, 0))
    else:
        src = pl.BlockSpec((None, 1, tr, c_n), lambda a, rb, sc: (it.slab, a, rb, 0))
    dst = pl.BlockSpec((1, tr, c_n), lambda a, rb, sc: (a + sc[1] * (a_n if sa == 0 else 0), rb + sc[1] * (r_n // tr if sa == 1 else 0),
                                                       sc[1] if sa == 2 else 0))
    return _call(body, f"cast_place_{it.name}", (a_n, r_n // tr), [src], [dst], [_sds(it.full3, BF16)], prefetch=1,
                 after=after)(scal, shard)[0]


def _gather_start(items, fulls, name):
    ni = len(items)

    def body(*refs):
        outs = refs[ni:]
        ssem, rsem, full = outs[:ni], outs[ni:2 * ni], outs[2 * ni:3 * ni]
        x, y, c, me = _place()
        for i, it in enumerate(items):
            mine = it.half(it.shard(full[i], me), c)
            for k, rel in enumerate(_RELS):
                px, py, _ = _peer(x, y, rel)
                _remote(mine, mine, ssem[i].at[k], rsem[i].at[k], (px, py, c)).start()

    outs = pl.pallas_call(
        body, name=name, in_specs=[HBM] * ni, out_specs=[SEM] * (2 * ni) + [HBM] * ni,
        out_shape=[pltpu.SemaphoreType.DMA((3,))] * (2 * ni) + [pltpu.HBM(it.full3, BF16) for it in items],
        input_output_aliases={j: 2 * ni + j for j in range(ni)},
        compiler_params=pltpu.CompilerParams(has_side_effects=DATAFLOW),
    )(*[_in_hbm(f) for f in fulls])
    return outs[:ni], outs[ni:2 * ni], outs[2 * ni:]


def _gather_forward(items, fulls, ssems, rsems, after, name):
    ni = len(items)

    def body(*refs):
        ssem, rsem = refs[ni:2 * ni], refs[2 * ni:3 * ni]
        outs = refs[3 * ni + 1:]
        full, fsem, gsem = outs[:ni], outs[ni:2 * ni], outs[2 * ni:3 * ni]
        x, y, c, me = _place()
        sib = (x, y, 1 - c)
        for i, it in enumerate(items):
            for k, rel in enumerate(_RELS):
                _, _, pj = _peer(x, y, rel)
                got = it.half(it.shard(full[i], pj), c)
                _remote(got, got, ssem[i].at[k], rsem[i].at[k], sib).wait_recv()
                _remote(got, got, fsem[i].at[k], gsem[i].at[k], sib).start()
        for i, it in enumerate(items):
            mine = it.half(it.shard(full[i], me), c)
            for k in range(3):
                _remote(mine, mine, ssem[i].at[k], rsem[i].at[k], sib).wait_send()

    outs = pl.pallas_call(
        body, name=name, in_specs=[HBM] * ni + [SEM] * (2 * ni) + [ANY],
        out_specs=[HBM] * ni + [SEM] * (2 * ni),
        out_shape=[pltpu.HBM(it.full3, BF16) for it in items] + [pltpu.SemaphoreType.DMA((3,))] * (2 * ni),
        input_output_aliases={i: i for i in range(ni)},
        compiler_params=pltpu.CompilerParams(has_side_effects=DATAFLOW),
    )(*fulls, *ssems, *rsems, after)
    return outs[:ni], outs[ni:2 * ni], outs[2 * ni:]


def _gather_finish(items, fulls, fsems, gsems, name):
    ni = len(items)

    def body(*refs):
        fsem, gsem = refs[ni:2 * ni], refs[2 * ni:3 * ni]
        full = refs[3 * ni:]
        x, y, c, _ = _place()
        sib = (x, y, 1 - c)
        for i, it in enumerate(items):
            for k, rel in enumerate(_RELS):
                _, _, pj = _peer(x, y, rel)
                got = it.half(it.shard(full[i], pj), 1 - c)
                _remote(got, got, fsem[i].at[k], gsem[i].at[k], sib).wait_recv()
                sent = it.half(it.shard(full[i], pj), c)
                _remote(sent, sent, fsem[i].at[k], gsem[i].at[k], sib).wait_send()

    return pl.pallas_call(
        body, name=name, in_specs=[HBM] * ni + [SEM] * (2 * ni), out_specs=[HBM] * ni,
        out_shape=[pltpu.HBM(it.full3, BF16) for it in items],
        input_output_aliases={i: i for i in range(ni)},
        compiler_params=pltpu.CompilerParams(has_side_effects=DATAFLOW),
    )(*fulls, *fsems, *gsems)


def _exchange_start(name, arrays, n_sems, copies):
    na, ns = len(arrays), len(n_sems)

    def body(*refs):
        outs = refs[na:]
        ssem, rsem, thru, token = outs[:ns], outs[ns:2 * ns], outs[2 * ns:2 * ns + na], outs[-1]
        for send, _ in copies(thru, ssem, rsem):
            send.start()
        token[...] = jnp.zeros_like(token)

    outs = pl.pallas_call(
        body, name=name, in_specs=[HBM] * na,
        out_specs=[SEM] * (2 * ns) + [HBM] * na + [pl.BlockSpec(memory_space=pltpu.VMEM)],
        out_shape=[pltpu.SemaphoreType.DMA((n,)) for n in n_sems] * 2 + [pltpu.HBM(a.shape, a.dtype) for a in arrays]
        + [_sds((8, 128), F32)],
        input_output_aliases={j: 2 * ns + j for j in range(na)},
        compiler_params=pltpu.CompilerParams(has_side_effects=DATAFLOW),
    )(*[_in_hbm(a) for a in arrays])
    return outs[:ns], outs[ns:2 * ns], outs[2 * ns:2 * ns + na], outs[-1]


def _exchange_wait(name, arrays, ssems, rsems, copies, after):
    na, ns, nw = len(arrays), len(ssems), len(after)

    def body(*refs):
        ssem, rsem = refs[na:na + ns], refs[na + ns:na + 2 * ns]
        thru = refs[na + 2 * ns + nw:]
        for send, recv in copies(thru, ssem, rsem):
            recv.wait_recv()
            send.wait_send()

    return pl.pallas_call(
        body, name=name, in_specs=[HBM] * na + [SEM] * (2 * ns) + [ANY] * nw, out_specs=[HBM] * na,
        out_shape=[pltpu.HBM(a.shape, a.dtype) for a in arrays],
        input_output_aliases={i: i for i in range(na)},
        compiler_params=pltpu.CompilerParams(has_side_effects=DATAFLOW),
    )(*arrays, *ssems, *rsems, *after)


def _pair_copies(items):
    n = len(items)

    def copies(a, ssem, rsem):
        x, y, c, _ = _place()
        cps = [_remote(it.half(a[i], 1 - c), a[n + i], ssem[i].at[0], rsem[i].at[0], (x, y, 1 - c)) for i, it in enumerate(items)]
        return [(cp, cp) for cp in cps]

    return copies


def _chip_copies(items):
    n = len(items)

    def copies(a, ssem, rsem):
        x, y, c, _ = _place()
        cps = []
        for i, it in enumerate(items):
            for k, rel in enumerate(_RELS):
                px, py, pj = _peer(x, y, rel)
                cps.append(_remote(it.shard(a[i], pj), a[n + i].at[k], ssem[i].at[k], rsem[i].at[k], (px, py, c)))
        return [(cp, cp) for cp in cps]

    return copies


def _fill_copies(items):
    def copies(a, ssem, rsem):
        x, y, c, _ = _place()
        sib = (x, y, 1 - c)
        out = []
        for i, it in enumerate(items):
            mine, other = it.half(a[i], c), it.half(a[i], 1 - c)
            out.append((_remote(mine, mine, ssem[i].at[0], rsem[i].at[0], sib), _remote(other, other, ssem[i].at[0], rsem[i].at[0], sib)))
        return out

    return copies


def _ew_tiles(d):
    _, rows, cols = d.part3
    return _pick(rows, 256, 16), cols


def _pair_add(d, g_full, got, scal):
    tr, tc = _ew_tiles(d)
    a_n, r_n, c_n = d.pair3
    ha = d.half_axis

    def body(sc_ref, g_ref, r_ref, ob_ref):
        ob_ref[...] = (g_ref[...].astype(F32) + r_ref[...].astype(F32)).astype(BF16)

    blk = (1, tr, tc)
    same = pl.BlockSpec(blk, lambda a, rb, cb, sc: (a, rb, cb))
    mine = pl.BlockSpec(blk, lambda a, rb, cb, sc: (a + sc[0] * (a_n if ha == 0 else 0), rb + sc[0] * (r_n // tr if ha == 1 else 0), cb))
    return _call(body, f"pair_add_{d.name}", (a_n, r_n // tr, c_n // tc), [mine, same], [same],
                 [_sds(d.pair3, BF16)], prefetch=1)(scal, g_full, got)[0]


def _chip_reduce(d, g_full, pair_got, got, scal):
    tr, tc = _ew_tiles(d)
    a_n, r_n, c_n = d.part3
    pa_n, pr_n, _ = d.pair3
    ha, sa = d.half_axis, d.shard_axis

    def body(sc_ref, g_ref, q_ref, r0, r1, r2, o_ref):
        p = g_ref[...].astype(F32) + q_ref[...].astype(F32)
        o_ref[...] = ((p + r0[...].astype(F32)) + r1[...].astype(F32)) + r2[...].astype(F32)

    blk = (1, tr, tc)
    own_a = lambda a, sc: a + sc[1] * (a_n if sa == 0 else 0)
    own_r = lambda rb, sc: rb + sc[1] * (r_n // tr if sa == 1 else 0)
    own_c = lambda sc: sc[1] if sa == 2 else 0
    own = pl.BlockSpec(blk, lambda a, rb, sc: (own_a(a, sc), own_r(rb, sc), own_c(sc)))
    mine = pl.BlockSpec(blk, lambda a, rb, sc: (own_a(a, sc) + sc[0] * (pa_n if ha == 0 else 0),
                                                own_r(rb, sc) + sc[0] * (pr_n // tr if ha == 1 else 0), own_c(sc)))
    slot = lambda k: pl.BlockSpec((None,) + blk, lambda a, rb, sc: (k, a, rb, 0))
    out = pl.BlockSpec(blk, lambda a, rb, sc: (a + sc[0] * (a_n if ha == 0 else 0), rb + sc[0] * (r_n // tr if ha == 1 else 0), 0))
    return _call(body, f"chip_reduce_{d.name}", (a_n, r_n // tr), [mine, own, slot(0), slot(1), slot(2)], [out],
                 [_sds(d.shard3, F32)], prefetch=1)(scal, g_full, pair_got, got, got, got)[0]


def _adam_math(g, w, m, v):
    m = ADAM_B1 * m + (1.0 - ADAM_B1) * g
    v = ADAM_B2 * v + (1.0 - ADAM_B2) * (g * g)
    m_hat = m / (1.0 - ADAM_B1 ** ADAM_STEP)
    v_hat = v / (1.0 - ADAM_B2 ** ADAM_STEP)
    delta = -ADAM_LR * (m_hat / (jnp.sqrt(v_hat) + ADAM_EPS) + ADAM_WD * w)
    return delta, m, v


def _adam(d, g, w, m, v, prev=None):
    tr, tc = _ew_tiles(d)
    a_n, r_n, c_n = d.shard3
    n_prev = 0 if prev is None else 4

    def body(g_ref, w_ref, m_ref, v_ref, *rest):
        go_ref, d_ref, mo_ref, vo_ref = rest[n_prev:]
        gv = g_ref[...]
        go_ref[...] = gv
        d_ref[...], mo_ref[...], vo_ref[...] = _adam_math(gv, w_ref[...], m_ref[...], v_ref[...])

    plain = pl.BlockSpec((1, tr, tc), lambda a, rb: (a, rb, 0))
    if d.slab is None:
        wspec, shape = plain, d.shard3
    else:
        wspec, shape = pl.BlockSpec((None, 1, tr, tc), lambda a, rb: (d.slab, a, rb, 0)), (4,) + d.shard3
    in_specs = [plain] + [wspec] * 3
    args = [g, w, m, v]
    aliases = None
    if prev is not None:
        in_specs += [pl.BlockSpec(memory_space=pl.ANY)] * 4
        args += list(prev)
        aliases = {4 + k: k for k in range(4)}
    return _call(body, f"adam_{d.name}", (a_n, r_n // tr), in_specs, [wspec] * 4, [_sds(shape, F32)] * 4, aliases=aliases)(*args)


def _adam_small(gs, ws, ms, vs):
    n = len(gs)

    def body(*refs):
        for i in range(n):
            g, w, m, v = (refs[k * n + i][...] for k in range(4))
            d, mo, vo = _adam_math(g, w, m, v)
            refs[4 * n + i][...] = d
            refs[5 * n + i][...] = mo
            refs[6 * n + i][...] = vo

    vm = pl.BlockSpec(memory_space=pltpu.VMEM)
    outs = pl.pallas_call(body, name="adam_small", in_specs=[vm] * (4 * n), out_specs=[vm] * (3 * n),
                          out_shape=[_sds(g.shape, F32) for g in gs] * 3)(*gs, *ws, *ms, *vs)
    return outs[:n], outs[n:2 * n], outs[2 * n:]


def _allreduce_small(packed):
    rows, cols = packed.shape
    others = [(dx, dy, dc) for dx in (0, 1) for dy in (0, 1) for dc in (0, 1) if (dx, dy, dc) != (0, 0, 0)]

    def body(in_ref, out_ref, buf, ssem, rsem):
        x, y, c, _ = _place()
        lin = 4 * x + 2 * y + c
        buf[lin] = in_ref[...]
        cps = []
        for k, (dx, dy, dc) in enumerate(others):
            px = 1 - x if dx else x
            py = 1 - y if dy else y
            pc = 1 - c if dc else c
            cps.append((pltpu.make_async_remote_copy(src_ref=in_ref, dst_ref=buf.at[lin], send_sem=ssem.at[k], recv_sem=rsem.at[k],
                                                     device_id=(px, py, pc), device_id_type=MESH), 4 * px + 2 * py + pc))
        for cp, _ in cps:
            cp.start()
        for k, (cp, plin) in enumerate(cps):
            pltpu.make_async_remote_copy(src_ref=in_ref, dst_ref=buf.at[plin], send_sem=ssem.at[k], recv_sem=rsem.at[k],
                                         device_id=(x, y, c), device_id_type=MESH).wait_recv()
        for cp, _ in cps:
            cp.wait_send()
        acc = buf[0]
        for dev in range(1, 8):
            acc = acc + buf[dev]
        out_ref[...] = acc

    vm = pl.BlockSpec(memory_space=pltpu.VMEM)
    return pl.pallas_call(
        body, name="allreduce_small", in_specs=[vm], out_specs=vm, out_shape=_sds((rows, cols), F32),
        scratch_shapes=[pltpu.VMEM((8, rows, cols), F32), pltpu.SemaphoreType.DMA((7,)), pltpu.SemaphoreType.DMA((7,))],
        compiler_params=pltpu.CompilerParams(has_side_effects=True),
    )(packed)


_BIG = ("ffn_w_in", "ffn_w_out", "conv_w_pw1", "conv_w_pw2", "attn_w_qkv", "attn_w_o")
_SMALL = ("norm_g", "conv_b_pw1", "conv_w_dw", "conv_b_dw", "conv_norm_g", "conv_b_pw2", "attn_q_norm", "attn_k_norm")
_NAMES = ("norm_g", "ffn_w_in", "ffn_w_out", "conv_w_pw1", "conv_b_pw1", "conv_w_dw", "conv_b_dw", "conv_norm_g", "conv_w_pw2",
          "conv_b_pw2", "attn_w_qkv", "attn_q_norm", "attn_k_norm", "attn_w_o")


def _rows8(a, width):
    a = a.reshape(-1, min(a.shape[-1], width))
    return jnp.pad(a, ((0, -a.shape[0] % 8), (0, width - a.shape[1])))


def kernel(x, norm_g, ffn_w_in, ffn_w_out, conv_w_pw1, conv_b_pw1, conv_w_dw, conv_b_dw, conv_norm_g, conv_w_pw2, conv_b_pw2, attn_w_qkv, attn_q_norm, attn_k_norm, attn_w_o, loss_target, m_norm_g, m_ffn_w_in, m_ffn_w_out, m_conv_w_pw1, m_conv_b_pw1, m_conv_w_dw, m_conv_b_dw, m_conv_norm_g, m_conv_w_pw2, m_conv_b_pw2, m_attn_w_qkv, m_attn_q_norm, m_attn_k_norm, m_attn_w_o, v_norm_g, v_ffn_w_in, v_ffn_w_out, v_conv_w_pw1, v_conv_b_pw1, v_conv_w_dw, v_conv_b_dw, v_conv_norm_g, v_conv_w_pw2, v_conv_b_pw2, v_attn_w_qkv, v_attn_q_norm, v_attn_k_norm, v_attn_w_o):
    w = dict(zip(_NAMES, (norm_g, ffn_w_in, ffn_w_out, conv_w_pw1, conv_b_pw1, conv_w_dw, conv_b_dw, conv_norm_g, conv_w_pw2,
                          conv_b_pw2, attn_w_qkv, attn_q_norm, attn_k_norm, attn_w_o)))
    m = dict(zip(_NAMES, (m_norm_g, m_ffn_w_in, m_ffn_w_out, m_conv_w_pw1, m_conv_b_pw1, m_conv_w_dw, m_conv_b_dw, m_conv_norm_g,
                          m_conv_w_pw2, m_conv_b_pw2, m_attn_w_qkv, m_attn_q_norm, m_attn_k_norm, m_attn_w_o)))
    v = dict(zip(_NAMES, (v_norm_g, v_ffn_w_in, v_ffn_w_out, v_conv_w_pw1, v_conv_b_pw1, v_conv_w_dw, v_conv_b_dw, v_conv_norm_g,
                          v_conv_w_pw2, v_conv_b_pw2, v_attn_w_qkv, v_attn_q_norm, v_attn_k_norm, v_attn_w_o)))
    T, D = x.shape[1:]
    F = ffn_w_out.shape[2] * N_CHIPS
    HE = attn_w_o.shape[1] * N_CHIPS
    cx, cy, cc = lax.axis_index("x"), lax.axis_index("y"), lax.axis_index("c")
    me = 2 * cx + cy
    scal = jnp.stack([cc, me]).astype(jnp.int32)

    ffn_in = lambda lf: _Sharded(f"ffn_w_in_{lf}", (2, D // 2, 2 * F), 0, 2, "ffn_w_in", lf)
    ffn_out = lambda lf: _Sharded(f"ffn_w_out_{lf}", (4, F // 4, D), 1, 0, "ffn_w_out", lf)
    items = [
        ffn_in(0), ffn_out(0),
        _Sharded("conv_w_pw1", (2, D // 2, 2 * D), 0, 2, "conv_w_pw1"), _Sharded("conv_w_pw2", (4, D // 4, D), 1, 0, "conv_w_pw2"),
        ffn_in(1), ffn_out(1), ffn_in(2), ffn_out(2),
        _Sharded("attn_w_qkv", (2, D // 2, 9 * HE), 0, 2, "attn_w_qkv"), _Sharded("attn_w_o", (4, HE // 4, D), 1, 0, "attn_w_o"),
        ffn_in(3), ffn_out(3),
    ]
    mat_shapes = {"ffn_w_in": (D, 2 * F), "ffn_w_out": (F, D), "conv_w_pw1": (D, 2 * D), "conv_w_pw2": (D, D),
                  "attn_w_qkv": (D, 9 * HE), "attn_w_o": (HE, D)}

    def as_shards(a, n):
        it = next(i for i in items if i.src == n)
        return a.reshape(((4,) if it.slab is not None else ()) + it.shard3)

    norm_full, dw_full = _gather_small([norm_g.reshape(6, D // 4), conv_w_dw.reshape(CONV_WIDTH, D // 4)])
    place = lambda its, after: [_cast_place(it, as_shards(w[it.src], it.src), scal, after) for it in its]
    ssems, rsems, fulls = _gather_start(items[:2], place(items[:2], ()), "gather_start_first")
    more = _gather_start(items[2:], place(items[2:], fulls[:1]), "gather_start_rest")
    ssems, rsems, fulls = [list(a) + list(b) for a, b in zip((ssems, rsems, fulls), more)]

    def block_weights(k, after):
        sel = slice(2 * k, 2 * k + 2)
        got, fsems, gsems = _gather_forward(items[sel], fulls[sel], ssems[sel], rsems[sel], after, f"gather_forward_{k}")
        done = _gather_finish(items[sel], got, fsems, gsems, f"gather_finish_{k}")
        return [a.reshape(mat_shapes[it.src]) for a, it in zip(done, items[sel])]

    res = {}
    flight = []

    def advance(entry, k, dx):
        stage, its, st = entry
        n = len(its)
        if stage == 1:
            ssem, rsem, arrs = st
            done = _exchange_wait(f"grads_pair_wait_{k}", arrs, ssem, rsem, _pair_copies(its), dx)
            sums = [_pair_add(it, g, r, scal) for it, g, r in zip(its, done[:n], done[n:])]
            land = [lax.empty((3,) + it.part3, BF16) for it in its]
            ssem, rsem, arrs, tok = _exchange_start(f"grads_chip_start_{k}", sums + land, [3] * n, _chip_copies(its))
            return (2, its, (ssem, rsem, arrs, list(zip(done[:n], done[n:])))), tok
        if stage == 2:
            ssem, rsem, arrs, pairs = st
            got = _exchange_wait(f"grads_chip_wait_{k}", arrs, ssem, rsem, _chip_copies(its), dx)[n:]
            red = [_chip_reduce(it, g, q, r, scal) for it, (g, q), r in zip(its, pairs, got)]
            ssem, rsem, arrs, tok = _exchange_start(f"grads_fill_start_{k}", red, [1] * n, _fill_copies(its))
            return (3, its, (ssem, rsem, arrs)), tok
        ssem, rsem, arrs = st
        for it, g in zip(its, _exchange_wait(f"grads_fill_wait_{k}", arrs, ssem, rsem, _fill_copies(its), dx)):
            nm = it.src
            res[nm] = _adam(it, g, as_shards(w[nm], nm), as_shards(m[nm], nm), as_shards(v[nm], nm), prev=res.get(nm))
        return None, None

    def step_flight(dx):
        toks, left = [], []
        for k, entry in flight:
            entry, tok = advance(entry, k, dx)
            if entry is not None:
                left.append((k, entry))
                toks.append(tok)
        flight[:] = left
        return toks

    def grads_ready(k, pairs, dx):
        its = items[2 * k:2 * k + 2]
        toks = step_flight((dx,))
        g16 = [p.reshape(it.full3) for p, it in zip(pairs, its)]
        land = [lax.empty(it.pair3, BF16) for it in its]
        ssem, rsem, arrs, tok = _exchange_start(f"grads_pair_start_{k}", g16 + land, [1] * len(its), _pair_copies(its))
        flight.append((k, (1, its, (ssem, rsem, arrs))))
        return toks + [tok]

    sq, dx, grads = _local_step(x[0], loss_target[0], norm_full.reshape(2, 3, D), conv_b_pw1, dw_full, conv_b_dw, conv_norm_g,
                                conv_b_pw2, attn_q_norm[0], attn_k_norm[0], block_weights, grads_ready,
                                lambda dx: step_flight((dx,)))
    loss = lax.psum(0.5 * jnp.sum(sq) / D, ("x", "y", "c"))
    k_last, entry = flight.pop()
    entry, tok = advance(entry, k_last, (dx,))
    while flight:
        step_flight((dx, tok))

    out_g, out_d, out_m, out_v = {}, {}, {}, {}

    parts = [_rows8(grads[n], D) for n in _SMALL]
    tot = _allreduce_small(jnp.concatenate(parts, axis=0))
    sg, r0 = {}, 0
    for n, p in zip(_SMALL, parts):
        last = grads[n].shape[-1]
        g = tot[r0:r0 + grads[n].size // min(last, D), :min(last, D)].reshape(-1, last)
        r0 += p.shape[0]
        if n in ("norm_g", "conv_w_dw"):
            g = lax.dynamic_slice_in_dim(g, me * (D // 4), D // 4, axis=1)
        sg[n] = g
    flat = lambda a: a.reshape(-1, a.shape[-1])
    ds, ms, vs = _adam_small([sg[n] for n in _SMALL], [flat(w[n]) for n in _SMALL], [flat(m[n]) for n in _SMALL],
                             [flat(v[n]) for n in _SMALL])
    for i, n in enumerate(_SMALL):
        out_g[n], out_d[n], out_m[n], out_v[n] = (a.reshape(w[n].shape) for a in (sg[n], ds[i], ms[i], vs[i]))

    behind = tuple(r[1] for r in res.values()) + tuple(ds)
    while entry is not None:
        entry, _ = advance(entry, k_last, behind)
    for n in _BIG:
        out_g[n], out_d[n], out_m[n], out_v[n] = (a.reshape(w[n].shape) for a in res[n])

    return (loss, dx[None], *[out_g[n] for n in _NAMES], *[out_d[n] for n in _NAMES], *[out_m[n] for n in _NAMES],
            *[out_v[n] for n in _NAMES])
```

```python
import functools

import jax
import jax.numpy as jnp
from jax import lax
from jax.experimental import pallas as pl
from jax.experimental.pallas import tpu as pltpu

F32 = jnp.float32
BF16 = jnp.bfloat16
MESH = pl.DeviceIdType.MESH

NORM_EPS = 1e-6
CONV_WIDTH = 31
ATTN_GROUPS = ((128, 1), (512, 4), (2048, 16))
ATTN_BLOCK = 128
HEAD_DIM = 128
N_CHIPS = 4

ADAM_LR = 0.001
ADAM_B1 = 0.9
ADAM_B2 = 0.999
ADAM_EPS = 1e-08
ADAM_WD = 0.01
ADAM_STEP = 10

VMEM_LIMIT = 56 * 1024 * 1024
NT_DIMS = (((1,), (1,)), ((), ()))
TN_DIMS = (((0,), (0,)), ((), ()))


def _pick(n, pref, mult):
    t = (min(n, pref) // mult) * mult
    while t >= mult:
        if n % t == 0:
            return t
        t -= mult
    return n


def _call(body, name, grid, in_specs, out_specs, out_shape, scratch=(), aliases=None, prefetch=0, after=()):
    params = pltpu.CompilerParams(dimension_semantics=("arbitrary",) * len(grid), vmem_limit_bytes=VMEM_LIMIT)
    after = tuple(after)
    if after:
        inner, n_in = body, prefetch + len(in_specs)

        def body(*refs):
            return inner(*refs[:n_in], *refs[n_in + len(after):])

        in_specs = list(in_specs) + [pl.BlockSpec(memory_space=pl.ANY)] * len(after)
    if prefetch:
        spec = pltpu.PrefetchScalarGridSpec(
            num_scalar_prefetch=prefetch, grid=grid, in_specs=in_specs, out_specs=out_specs, scratch_shapes=list(scratch)
        )
        call = pl.pallas_call(body, name=name, grid_spec=spec, out_shape=out_shape, compiler_params=params,
                              input_output_aliases=aliases or {})
    else:
        call = pl.pallas_call(body, name=name, grid=grid, in_specs=in_specs, out_specs=out_specs, out_shape=out_shape,
                              scratch_shapes=list(scratch), compiler_params=params, input_output_aliases=aliases or {})
    return lambda *args: call(*args, *after)


def _sds(shape, dtype):
    return jax.ShapeDtypeStruct(shape, dtype)


def _sig(x):
    return 1.0 / (1.0 + jnp.exp(-x))


def _rstd(x):
    return lax.rsqrt(jnp.mean(x * x, axis=-1, keepdims=True) + NORM_EPS)


def _norm_bwd(dy, xhat, r, g):
    dxh = dy * g
    return r * (dxh - xhat * jnp.mean(dxh * xhat, axis=-1, keepdims=True))


def _dot(a, b):
    return jnp.dot(a, b, preferred_element_type=F32)


def _dot_nt(a, b):
    return lax.dot_general(a, b, NT_DIMS, preferred_element_type=F32)


def _dot_tn(a, b):
    return lax.dot_general(a, b, TN_DIMS, preferred_element_type=F32)


def _ffn_fwd(x, g, w_in, w_out, name, target=None):
    T, D = x.shape
    F = w_out.shape[0]
    tm = _pick(T, 256, 8)
    last = target is not None

    def body(x_ref, g_ref, wg_ref, wu_ref, wo_ref, *rest):
        xo_ref, gate_ref, up_ref = rest[last:last + 3]
        xv = x_ref[...]
        h = (xv * _rstd(xv) * g_ref[...]).astype(BF16)
        gate = _dot(h, wg_ref[...])
        up = _dot(h, wu_ref[...])
        gate_ref[...] = gate.astype(BF16)
        up_ref[...] = up.astype(BF16)
        a = (gate * _sig(gate) * up).astype(BF16)
        y = xv + 0.5 * _dot(a, wo_ref[...])
        if not last:
            xo_ref[...] = y
            return
        t_ref, sq_ref = rest[0], rest[4]

        @pl.when(pl.program_id(0) == 0)
        def _():
            sq_ref[...] = jnp.zeros_like(sq_ref)

        err = y - t_ref[...]
        xo_ref[...] = err * (1.0 / D)
        sq_ref[...] += jnp.sum(err * err, axis=0, keepdims=True)

    row = lambda i: (i, 0)
    fix = pl.BlockSpec((1, D), lambda i: (0, 0))
    held = lambda shape, k: pl.BlockSpec(shape, lambda i: (0, k), pipeline_mode=pl.Buffered(1))
    return _call(
        body, name, (T // tm,),
        [pl.BlockSpec((tm, D), row), fix, held((D, F), 0), held((D, F), 1), held((F, D), 0)] + [pl.BlockSpec((tm, D), row)] * last,
        [pl.BlockSpec((tm, D), row), pl.BlockSpec((tm, F), row), pl.BlockSpec((tm, F), row)] + [fix] * last,
        [_sds((T, D), F32), _sds((T, F), BF16), _sds((T, F), BF16)] + [_sds((1, D), F32)] * last,
    )(x, g, w_in, w_in, w_out, *([target] if last else []))


def _ffn_bwd(x, g, dy, gate, up, w_in, w_out, name, after=()):
    T, D = x.shape
    F = w_out.shape[0]

    def body_a(dy_ref, gate_ref, up_ref, wo_ref, dgate_ref, dup_ref, a_ref):
        dyb = (0.5 * dy_ref[...]).astype(BF16)
        gate = gate_ref[...].astype(F32)
        up = up_ref[...].astype(F32)
        sg = _sig(gate)
        sl = gate * sg
        a_ref[...] = (sl * up).astype(BF16)
        da = _dot_nt(dyb, wo_ref[...])
        dgate_ref[...] = (da * up * (sg * (1.0 + gate * (1.0 - sg)))).astype(BF16)
        dup_ref[...] = (da * sl).astype(BF16)

    ta = _pick(T, 256, 8)
    tile = pl.BlockSpec((ta, F), lambda i: (i, 0))
    dgate, dup, a = _call(
        body_a, name + "_hidden", (T // ta,),
        [pl.BlockSpec((ta, D), lambda i: (i, 0)), tile, tile,
         pl.BlockSpec((F, D), lambda i: (0, 0), pipeline_mode=pl.Buffered(1))],
        [tile, tile, tile],
        [_sds((T, F), BF16)] * 3, after=after,
    )(dy, gate, up, w_out)

    def body_b(x_ref, g_ref, dy_ref, dgate_ref, dup_ref, wg_ref, wu_ref, dx_ref, dg_ref, h_ref, dyb_ref):
        @pl.when(pl.program_id(0) == 0)
        def _():
            dg_ref[...] = jnp.zeros_like(dg_ref)

        xv = x_ref[...]
        r = _rstd(xv)
        xh = xv * r
        h_ref[...] = (xh * g_ref[...]).astype(BF16)
        dyb_ref[...] = (0.5 * dy_ref[...]).astype(BF16)
        dh = _dot_nt(dgate_ref[...], wg_ref[...]) + _dot_nt(dup_ref[...], wu_ref[...])
        dx_ref[...] = dy_ref[...] + _norm_bwd(dh, xh, r, g_ref[...])
        dg_ref[...] += jnp.sum(dh * xh, axis=0, keepdims=True)

    tb = _pick(T, 512, 8)
    row = lambda i: (i, 0)
    held = lambda k: pl.BlockSpec((D, F), lambda i: (0, k), pipeline_mode=pl.Buffered(1))
    dx, dg, h, dyb = _call(
        body_b, name, (T // tb,),
        [pl.BlockSpec((tb, D), row), pl.BlockSpec((1, D), lambda i: (0, 0)), pl.BlockSpec((tb, D), row),
         pl.BlockSpec((tb, F), row), pl.BlockSpec((tb, F), row), held(0), held(1)],
        [pl.BlockSpec((tb, D), row), pl.BlockSpec((1, D), lambda i: (0, 0)), pl.BlockSpec((tb, D), row), pl.BlockSpec((tb, D), row)],
        [_sds((T, D), F32), _sds((1, D), F32), _sds((T, D), BF16), _sds((T, D), BF16)],
    )(x, g, dy, dgate, dup, w_in, w_in)
    return dx, dg, dgate, dup, a, h, dyb


def _mm_tn(a, b, bm, bn, out_shape, out_block, out_map, scale, name):
    K, M = a.shape
    N = b.shape[1]

    def body(a_ref, b_ref, o_ref):
        o_ref[...] = (_dot_tn(a_ref[...].astype(BF16), b_ref[...].astype(BF16)) * scale).astype(BF16)

    in_specs = [pl.BlockSpec((K, bm), lambda mi, ni: (0, mi)), pl.BlockSpec((K, bn), lambda mi, ni: (0, ni))]
    return _call(body, name, (M // bm, N // bn), in_specs, [pl.BlockSpec(out_block, out_map)],
                 [_sds(out_shape, BF16)])(a, b)[0]


def _mm_tn_parts(a, bs, bn, cols, col0, name, prev=None, after=()):
    K, M = a.shape
    N = bs[0].shape[1]
    nt = N // bn
    ns = len(bs)

    def body(a_ref, *refs):
        o_ref = refs[-1]
        part = pl.program_id(0) // nt
        for s in range(ns):
            @pl.when(part == s)
            def _(s=s):
                o_ref[...] = _dot_tn(a_ref[...], refs[s][...]).astype(BF16)

    part_spec = lambda s: pl.BlockSpec((K, bn), lambda j: (0, jnp.clip(j - s * nt, 0, nt - 1)))
    in_specs = [pl.BlockSpec((K, M), lambda j: (0, 0))] + [part_spec(s) for s in range(ns)]
    args = [a, *bs]
    aliases = None
    if prev is not None:
        in_specs += [pl.BlockSpec(memory_space=pl.ANY)]
        args += [prev]
        aliases = {ns + 1: 0}
    return _call(body, name, (ns * nt,), in_specs, [pl.BlockSpec((M, bn), lambda j: (0, col0 // bn + j))],
                 [_sds((M, cols), BF16)], aliases=aliases, after=after)(*args)[0]


def _conv_pre(x, g, w1, b1, name):
    T, D = x.shape
    tm = _pick(T, 512, 8)

    def body(x_ref, g_ref, w_ref, b_ref, ag_ref, u_ref, h_ref):
        xv = x_ref[...]
        h = (xv * _rstd(xv) * g_ref[...]).astype(BF16)
        h_ref[...] = h
        ag = _dot(h, w_ref[...]) + b_ref[...]
        ag_ref[...] = ag.astype(BF16)
        u_ref[...] = ag[:, :D] * _sig(ag[:, D:])

    return _call(
        body, name, (T // tm,),
        [pl.BlockSpec((tm, D), lambda i: (i, 0)), pl.BlockSpec((1, D), lambda i: (0, 0)),
         pl.BlockSpec((D, 2 * D), lambda i: (0, 0)), pl.BlockSpec((1, 2 * D), lambda i: (0, 0))],
        [pl.BlockSpec((tm, 2 * D), lambda i: (i, 0)), pl.BlockSpec((tm, D), lambda i: (i, 0)),
         pl.BlockSpec((tm, D), lambda i: (i, 0))],
        [_sds((T, 2 * D), BF16), _sds((T, D), F32), _sds((T, D), BF16)],
    )(x, g, w1, b1)


_DW_PAD = 32
_DW_CHUNK = 256


def _dwconv(u, w, b, name):
    T, D = u.shape
    K = w.shape[0]
    ch = _pick(T, _DW_CHUNK, 8)
    lead = _DW_PAD - (K - 1)

    def body(u_ref, w_ref, b_ref, c_ref, ext):
        ext[pl.ds(0, _DW_PAD), :] = jnp.zeros((_DW_PAD, 128), F32)
        ext[pl.ds(_DW_PAD, T), :] = u_ref[...]
        for c0 in range(0, T, ch):
            acc = jnp.zeros((ch, 128), F32) + b_ref[...]
            for k in range(K):
                acc = acc + w_ref[pl.ds(k, 1), :] * ext[pl.ds(c0 + lead + k, ch), :]
            c_ref[pl.ds(c0, ch), :] = acc

    return _call(
        body, name, (D // 128,),
        [pl.BlockSpec((T, 128), lambda i: (0, i)), pl.BlockSpec((K, 128), lambda i: (0, i)),
         pl.BlockSpec((1, 128), lambda i: (0, i))],
        [pl.BlockSpec((T, 128), lambda i: (0, i))],
        [_sds((T, D), F32)],
        scratch=[pltpu.VMEM((T + _DW_PAD, 128), F32)],
    )(u, w, b)[0]


def _dwconv_bwd(dc, u, w, name):
    T, D = u.shape
    K = w.shape[0]
    ch = _pick(T, _DW_CHUNK, 8)

    def body(dc_ref, u_ref, w_ref, du_ref, dw_ref, db_ref, dext):
        dext[pl.ds(0, T), :] = dc_ref[...]
        dext[pl.ds(T, _DW_PAD), :] = jnp.zeros((_DW_PAD, 128), F32)
        dws =[jnp.zeros((8, 128), F32) for _ in range(K)]
        dbs = jnp.zeros((8, 128), F32)
        for c0 in range(0, T, ch):
            uv = u_ref[pl.ds(c0, ch), :]
            dbs = dbs + jnp.sum(dc_ref[pl.ds(c0, ch), :].reshape(ch // 8, 8, 128), axis=0)
            acc = jnp.zeros((ch, 128), F32)
            for k in range(K):
                win = dext[pl.ds(c0 + (K - 1) - k, ch), :]
                acc = acc + w_ref[pl.ds(k, 1), :] * win
                dws[k] = dws[k] + jnp.sum((win * uv).reshape(ch // 8, 8, 128), axis=0)
            du_ref[pl.ds(c0, ch), :] = acc
        for k in range(K):
            dw_ref[pl.ds(k, 1), :] = jnp.sum(dws[k], axis=0, keepdims=True)
        db_ref[...] = jnp.sum(dbs, axis=0, keepdims=True)

    return _call(
        body, name, (D // 128,),
        [pl.BlockSpec((T, 128), lambda i: (0, i)), pl.BlockSpec((T, 128), lambda i: (0, i)),
         pl.BlockSpec((K, 128), lambda i: (0, i))],
        [pl.BlockSpec((T, 128), lambda i: (0, i)), pl.BlockSpec((K, 128), lambda i: (0, i)),
         pl.BlockSpec((1, 128), lambda i: (0, i))],
        [_sds((T, D), F32), _sds((K, D), F32), _sds((1, D), F32)],
        scratch=[pltpu.VMEM((T + _DW_PAD, 128), F32)],
    )(dc, u, w)


def _conv_post(c, x, ng, w2, b2, name):
    T, D = x.shape
    tm = _pick(T, 512, 8)

    def body(c_ref, x_ref, ng_ref, w_ref, b_ref, xo_ref, s_ref):
        cv = c_ref[...]
        n = cv * _rstd(cv) * ng_ref[...]
        s = (n * _sig(n)).astype(BF16)
        s_ref[...] = s
        xo_ref[...] = x_ref[...] + _dot(s, w_ref[...]) + b_ref[...]

    row = lambda i: (i, 0)
    fix = lambda i: (0, 0)
    return _call(
        body, name, (T // tm,),
        [pl.BlockSpec((tm, D), row), pl.BlockSpec((tm, D), row), pl.BlockSpec((1, D), fix),
         pl.BlockSpec((D, D), fix), pl.BlockSpec((1, D), fix)],
        [pl.BlockSpec((tm, D), row), pl.BlockSpec((tm, D), row)],
        [_sds((T, D), F32), _sds((T, D), BF16)],
    )(c, x, ng, w2, b2)


def _conv_post_bwd(dy, c, ng, w2, name, after=()):
    T, D = dy.shape
    tm = _pick(T, 512, 8)

    def body(dy_ref, c_ref, ng_ref, w_ref, dc_ref, dng_ref, db_ref):
        @pl.when(pl.program_id(0) == 0)
        def _():
            dng_ref[...] = jnp.zeros_like(dng_ref)
            db_ref[...] = jnp.zeros_like(db_ref)

        dyv = dy_ref[...]
        ds = _dot_nt(dyv.astype(BF16), w_ref[...])
        cv = c_ref[...]
        r = _rstd(cv)
        ch = cv * r
        n = ch * ng_ref[...]
        sg = _sig(n)
        dn = ds * (sg * (1.0 + n * (1.0 - sg)))
        dc_ref[...] = _norm_bwd(dn, ch, r, ng_ref[...])
        dng_ref[...] += jnp.sum(dn * ch, axis=0, keepdims=True)
        db_ref[...] += jnp.sum(dyv, axis=0, keepdims=True)

    row = lambda i: (i, 0)
    fix = lambda i: (0, 0)
    return _call(
        body, name, (T // tm,),
        [pl.BlockSpec((tm, D), row), pl.BlockSpec((tm, D), row), pl.BlockSpec((1, D), fix), pl.BlockSpec((D, D), fix)],
        [pl.BlockSpec((tm, D), row), pl.BlockSpec((1, D), fix), pl.BlockSpec((1, D), fix)],
        [_sds((T, D), F32), _sds((1, D), F32), _sds((1, D), F32)], after=after,
    )(dy, c, ng, w2)


def _conv_pre_bwd(du, ag, x, g, dy, w1, name):
    T, D = x.shape
    tm = _pick(T, 512, 8)

    def body(du_ref, ag_ref, x_ref, g_ref, dy_ref, w_ref, dx_ref, dg_ref, dag_ref, db_ref):
        @pl.when(pl.program_id(0) == 0)
        def _():
            dg_ref[...] = jnp.zeros_like(dg_ref)
            db_ref[...] = jnp.zeros_like(db_ref)

        duv = du_ref[...]
        a = ag_ref[:, :D].astype(F32)
        gt = ag_ref[:, D:].astype(F32)
        sg = _sig(gt)
        da = duv * sg
        dgt = duv * a * sg * (1.0 - sg)
        db_ref[:, :D] += jnp.sum(da, axis=0, keepdims=True)
        db_ref[:, D:] += jnp.sum(dgt, axis=0, keepdims=True)
        dab = da.astype(BF16)
        dgb = dgt.astype(BF16)
        dag_ref[:, :D] = dab
        dag_ref[:, D:] = dgb
        dh = _dot_nt(dab, w_ref[:, :D]) + _dot_nt(dgb, w_ref[:, D:])
        xv = x_ref[...]
        r = _rstd(xv)
        xh = xv * r
        dx_ref[...] = dy_ref[...] + _norm_bwd(dh, xh, r, g_ref[...])
        dg_ref[...] += jnp.sum(dh * xh, axis=0, keepdims=True)

    row = lambda i: (i, 0)
    fix = lambda i: (0, 0)
    return _call(
        body, name, (T // tm,),
        [pl.BlockSpec((tm, D), row), pl.BlockSpec((tm, 2 * D), row), pl.BlockSpec((tm, D), row), pl.BlockSpec((1, D), fix),
         pl.BlockSpec((tm, D), row), pl.BlockSpec((D, 2 * D), fix)],
        [pl.BlockSpec((tm, D), row), pl.BlockSpec((1, D), fix), pl.BlockSpec((tm, 2 * D), row),
         pl.BlockSpec((1, 2 * D), fix)],
        [_sds((T, D), F32), _sds((1, D), F32), _sds((T, 2 * D), BF16), _sds((1, 2 * D), F32)],
    )(du, ag, x, g, dy, w1)


def _row_sums(a):
    return _dot(a.astype(BF16), jnp.ones((a.shape[1], 128), BF16))


def _head_rstd(x):
    return lax.rsqrt(_row_sums(x * x) * (1.0 / x.shape[1]) + NORM_EPS)


def _attn_qkv(x, g, wqkv, qn, kn, name):
    T, D = x.shape
    N = wqkv.shape[1]
    tn = N // 9
    E = HEAD_DIM
    tm = _pick(T, 256, 8)
    ng = qn.shape[0]

    def body(x_ref, g_ref, w_ref, qn_ref, kn_ref, o_ref, a_ref, h_ref):
        xv = x_ref[...]
        h = (xv * _rstd(xv) * g_ref[...]).astype(BF16)
        h_ref[...] = h
        for j in range(9):
            cs = slice(j * tn, (j + 1) * tn)
            res = _dot(h, w_ref[:, cs])
            o_ref[:, cs] = res.astype(BF16)
            if j % 3 == 2:
                a_ref[:, cs] = res.astype(BF16)
                continue
            grp = j // 3
            fac = qn_ref[grp:grp + 1, :] * kn_ref[grp:grp + 1, :] * (E ** -0.5) if j % 3 == 0 else None
            for h_i in range(tn // E):
                hs = slice(h_i * E, (h_i + 1) * E)
                xh = res[:, hs]
                hat = xh * _head_rstd(xh)
                a_ref[:, j * tn + h_i * E:j * tn + (h_i + 1) * E] = (hat if fac is None else hat * fac).astype(BF16)

    row = lambda i: (i, 0)
    fix = lambda i: (0, 0)
    return _call(
        body, name, (T // tm,),
        [pl.BlockSpec((tm, D), row), pl.BlockSpec((1, D), fix), pl.BlockSpec((D, N), fix, pipeline_mode=pl.Buffered(1)),
         pl.BlockSpec((ng, E), fix), pl.BlockSpec((ng, E), fix)],
        [pl.BlockSpec((tm, N), row), pl.BlockSpec((tm, N), row), pl.BlockSpec((tm, D), row)],
        [_sds((T, N), BF16), _sds((T, N), BF16), _sds((T, D), BF16)],
    )(x, g, wqkv, qn, kn)


def _band_mask(q, steps, nblk):
    i = lax.broadcasted_iota(jnp.int32, (q, 2 * q), 0)
    j = lax.broadcasted_iota(jnp.int32, (q, 2 * q), 1)
    diff = q + i - j
    first_key = jnp.where(nblk > 0, 0, q)
    return (diff >= 0) & (diff <= steps) & (j >= first_key)


def _per_row(blk, width):
    e = blk.shape[1]
    if width % e == 0:
        return jnp.concatenate([blk] * (width // e), axis=1)
    return jnp.broadcast_to(blk[:, :1], (blk.shape[0], width))


def _streams(a, dil, to_streams, name, col0=0, ncols=None):
    T, C = a.shape
    ncols = C if ncols is None else ncols
    Q = ATTN_BLOCK
    run = Q * dil
    reps = max(1, min(2048 // run, T // run))
    while T % (run * reps):
        reps -= 1
    rows = run * reps
    cw = _pick(ncols, 512, 128)
    ns = cw // 128

    def body(a_ref, o_ref, scr):
        for s in range(ns):
            ls = slice(s * 128, (s + 1) * 128)
            slab = scr.at[s]
            if to_streams:
                slab[...] = a_ref[:, ls].astype(F32)
                for u in range(reps):
                    for r in range(dil):
                        o_ref[pl.ds(u * run + r * Q, Q), ls] = slab[pl.ds(u * run + r, Q, stride=dil), :].astype(a.dtype)
            else:
                for u in range(reps):
                    for r in range(dil):
                        slab[pl.ds(u * run + r, Q, stride=dil), :] = a_ref[pl.ds(u * run + r * Q, Q), ls].astype(F32)
                o_ref[:, ls] = slab[...].astype(a.dtype)

    return _call(
        body, name, (T // rows, ncols // cw),
        [pl.BlockSpec((rows, cw), lambda i, j: (i, col0 // cw + j))],
        [pl.BlockSpec((rows, cw), lambda i, j: (i, j))],
        [_sds((T, ncols), a.dtype)],
        scratch=[pltpu.VMEM((ns, rows, 128), F32)],
    )(a)[0]


def _attn_fwd(qkv, base, HE, window, dil, name):
    T = qkv.shape[0]
    H = HE // HEAD_DIM
    E = HEAD_DIM
    Q = ATTN_BLOCK
    nb = T // dil // Q
    steps = window // dil

    def body(q_ref, kc_ref, kp_ref, vc_ref, vp_ref, o_ref, l_ref):
        n = pl.program_id(1)
        valid = _band_mask(Q, steps, n)
        ones = jnp.ones((2 * Q, E), BF16)
        outs, lses = [], []
        for h in range(H):
            hs = slice(h * E, (h + 1) * E)
            k2 = jnp.concatenate([kp_ref[:, hs], kc_ref[:, hs]], axis=0)
            v2 = jnp.concatenate([vp_ref[:, hs], vc_ref[:, hs]], axis=0)
            s = jnp.where(valid, _dot_nt(q_ref[:, hs], k2), -1e30)
            m = jnp.max(s, axis=-1, keepdims=True)
            p = jnp.exp(s - m).astype(BF16)
            acc = _dot(p, jnp.concatenate([v2, ones], axis=1))
            l = acc[:, E:]
            outs.append((acc[:, :E] * (1.0 / l)).astype(o_ref.dtype))
            lses.append(m + jnp.log(l))
        o_ref[...] = jnp.concatenate(outs, axis=1)
        l_ref[...] = jnp.concatenate(lses, axis=1)

    blk = lambda s, back: pl.BlockSpec((Q, HE), lambda r, n: (jnp.maximum(n - back, 0) * dil + r, base + s))
    out = pl.BlockSpec((Q, HE), lambda r, n: (n * dil + r, 0))
    return _call(
        body, name, (dil, nb),
        [blk(0, 0), blk(1, 0), blk(1, 1), blk(2, 0), blk(2, 1)],
        [out, out],
        [_sds((T, HE), BF16), _sds((T, HE), F32)],
    )(qkv, qkv, qkv, qkv, qkv)


def _attn_merge(os, lses, x, wo, name):
    T, D = x.shape
    HE = wo.shape[0]
    tm = _pick(T, 512, 8)
    ng = len(os)

    def body(*refs):
        o_refs = refs[:ng]
        l_refs = refs[ng:2 * ng]
        x_ref, w_ref, xo_ref, om_ref, lt_ref = refs[2 * ng:]
        ls = [r[...] for r in l_refs]
        m = functools.reduce(jnp.maximum, ls)
        es = [jnp.exp(l - m) for l in ls]
        tot = functools.reduce(lambda a, b: a + b, es)
        inv = 1.0 / tot
        om = functools.reduce(lambda a, b: a + b, [e * inv * r[...] for e, r in zip(es, o_refs)])
        omb = om.astype(BF16)
        om_ref[...] = omb
        lt_ref[...] = m + jnp.log(tot)
        xo_ref[...] = x_ref[...] + _dot(omb, w_ref[...])

    row = lambda i: (i, 0)
    fix = lambda i: (0, 0)
    return _call(
        body, name, (T // tm,),
        [pl.BlockSpec((tm, HE), row)] * (2 * ng) + [pl.BlockSpec((tm, D), row), pl.BlockSpec((HE, D), fix)],
        [pl.BlockSpec((tm, D), row), pl.BlockSpec((tm, HE), row), pl.BlockSpec((tm, HE), row)],
        [_sds((T, D), F32), _sds((T, HE), BF16), _sds((T, HE), F32)],
    )(*os, *lses, x, wo)


def _attn_out_bwd(dy, om, wo, name, after=()):
    T, D = dy.shape
    HE = wo.shape[0]
    E = HEAD_DIM
    tm = _pick(T, 512, 8)

    def body(dy_ref, om_ref, w_ref, dom_ref, dl_ref):
        dom = _dot_nt(dy_ref[...].astype(BF16), w_ref[...])
        dom_ref[...] = dom.astype(BF16)
        prod = dom * om_ref[...].astype(F32)
        for h in range(HE // E):
            hs = slice(h * E, (h + 1) * E)
            dl_ref[:, hs] = jnp.broadcast_to(jnp.sum(prod[:, hs], axis=-1, keepdims=True), (tm, E))

    row = lambda i: (i, 0)
    return _call(
        body, name, (T // tm,),
        [pl.BlockSpec((tm, D), row), pl.BlockSpec((tm, HE), row), pl.BlockSpec((HE, D), lambda i: (0, 0))],
        [pl.BlockSpec((tm, HE), row), pl.BlockSpec((tm, HE), row)],
        [_sds((T, HE), BF16), _sds((T, HE), F32)], after=after,
    )(dy, om, wo)


def _attn_bwd(qkv, base, HE, dom, lse, delta, window, dil, name):
    T = qkv.shape[0]
    H = HE // HEAD_DIM
    E = HEAD_DIM
    Q = ATTN_BLOCK
    nb = T // dil // Q
    steps = window // dil

    def body(q_ref, kc_ref, kp_ref, vc_ref, vp_ref, do_ref, l_ref, dl_ref, dq_ref, dk_ref, dv_ref, ck_sc, cv_sc):
        n = pl.program_id(1)

        @pl.when(n == 0)
        def _():
            ck_sc[...] = jnp.zeros_like(ck_sc)
            cv_sc[...] = jnp.zeros_like(cv_sc)

        @pl.when(n < nb)
        def _():
            valid = _band_mask(Q, steps, n)
            ck_old = ck_sc[...]
            cv_old = cv_sc[...]
            dqs, dks, dvs = [], [], []
            for h in range(H):
                hs = slice(h * E, (h + 1) * E)
                q = q_ref[:, hs]
                do = do_ref[:, hs]
                k2 = jnp.concatenate([kp_ref[:, hs], kc_ref[:, hs]], axis=0)
                v2 = jnp.concatenate([vp_ref[:, hs], vc_ref[:, hs]], axis=0)
                p = jnp.where(valid, jnp.exp(_dot_nt(q, k2) - _per_row(l_ref[:, hs], 2 * Q)), 0.0)
                ds = (p * (_dot_nt(do, v2) - _per_row(dl_ref[:, hs], 2 * Q))).astype(BF16)
                dqs.append(_dot(ds, k2).astype(BF16))
                dks.append(_dot_tn(q, ds).T)
                dvs.append(_dot_tn(do, p.astype(BF16)).T)
            cat = lambda parts: jnp.concatenate(parts, axis=1)
            dq_ref[...] = cat(dqs)
            dk_ref[...] = (ck_old + cat([d[:Q] for d in dks])).astype(BF16)
            dv_ref[...] = (cv_old + cat([d[:Q] for d in dvs])).astype(BF16)
            ck_sc[...] = cat([d[Q:] for d in dks])
            cv_sc[...] = cat([d[Q:] for d in dvs])

        @pl.when(n == nb)
        def _():
            dk_ref[...] = ck_sc[...].astype(BF16)
            dv_ref[...] = cv_sc[...].astype(BF16)

    nq = lambda n: jnp.minimum(n, nb - 1)
    blk = lambda s, back: pl.BlockSpec((Q, HE), lambda r, n: (jnp.maximum(nq(n) - back, 0) * dil + r, base + s))
    qblk = pl.BlockSpec((Q, HE), lambda r, n: (nq(n) * dil + r, 0))
    kblk = pl.BlockSpec((Q, HE), lambda r, n: (jnp.maximum(n - 1, 0) * dil + r, 0))
    return _call(
        body, name, (dil, nb + 1),
        [blk(0, 0), blk(1, 0), blk(1, 1), blk(2, 0), blk(2, 1), qblk, qblk, qblk],
        [qblk, kblk, kblk],
        [_sds((T, HE), BF16)] * 3,
        scratch=[pltpu.VMEM((Q, HE), F32), pltpu.VMEM((Q, HE), F32)],
    )(qkv, qkv, qkv, qkv, qkv, dom, lse, delta)


def _qk_norm_bwd(dq, dk, qkv, base, HE, gq, gk, name):
    T = dq.shape[0]
    E = HEAD_DIM
    tm = _pick(T, 512, 8)
    scale = E ** -0.5

    def body(dq_ref, dk_ref, q_ref, k_ref, gq_ref, gk_ref, oq_ref, ok_ref, dgq_ref, dgk_ref):
        @pl.when(pl.program_id(0) == 0)
        def _():
            dgq_ref[...] = jnp.zeros_like(dgq_ref)
            dgk_ref[...] = jnp.zeros_like(dgk_ref)

        gqv = gq_ref[...]
        gkv = gk_ref[...]
        c = gqv * gkv * scale
        dc = jnp.zeros((1, E), F32)
        for h in range(HE // E):
            hs = slice(h * E, (h + 1) * E)
            q = q_ref[:, hs].astype(F32)
            rq = _head_rstd(q)
            qh = q * rq
            a = dq_ref[:, hs].astype(F32)
            dc = dc + jnp.sum(a * qh, axis=0, keepdims=True)
            dqh = a * c
            oq_ref[:, hs] = (rq * (dqh - qh * (_row_sums(dqh * qh) * (1.0 / E)))).astype(BF16)
            k = k_ref[:, hs].astype(F32)
            rk = _head_rstd(k)
            kh = k * rk
            b = dk_ref[:, hs].astype(F32)
            ok_ref[:, hs] = (rk * (b - kh * (_row_sums(b * kh) * (1.0 / E)))).astype(BF16)
        dgq_ref[...] += dc * (gkv * scale)
        dgk_ref[...] += dc * (gqv * scale)

    row = lambda i: (i, 0)
    vec = pl.BlockSpec((1, E), lambda i: (0, 0))
    return _call(
        body, name, (T // tm,),
        [pl.BlockSpec((tm, HE), row), pl.BlockSpec((tm, HE), row), pl.BlockSpec((tm, HE), lambda i: (i, base)),
         pl.BlockSpec((tm, HE), lambda i: (i, base + 1)), vec, vec],
        [pl.BlockSpec((tm, HE), row), pl.BlockSpec((tm, HE), row), vec, vec],
        [_sds((T, HE), BF16), _sds((T, HE), BF16), _sds((1, E), F32), _sds((1, E), F32)],
    )(dq, dk, qkv, qkv, gq, gk)


def _attn_qkv_bwd(dqkv, x, g, dy, wqkv, name):
    T, D = x.shape
    HE = wqkv.shape[1] // 9
    tm = _pick(T, 256, 8)

    def body(*refs):
        d_refs = refs[:9]
        x_ref, g_ref, dy_ref, w_ref, dx_ref, dg_ref = refs[9:]

        @pl.when(pl.program_id(0) == 0)
        def _():
            dg_ref[...] = jnp.zeros_like(dg_ref)

        dh = _dot_nt(d_refs[0][...], w_ref[:, :HE])
        for s in range(1, 9):
            dh = dh + _dot_nt(d_refs[s][...], w_ref[:, s * HE:(s + 1) * HE])
        xv = x_ref[...]
        r = _rstd(xv)
        xh = xv * r
        dx_ref[...] = dy_ref[...] + _norm_bwd(dh, xh, r, g_ref[...])
        dg_ref[...] += jnp.sum(dh * xh, axis=0, keepdims=True)

    row = lambda i: (i, 0)
    fix = lambda i: (0, 0)
    return _call(
        body, name, (T // tm,),
        [pl.BlockSpec((tm, HE), row)] * 9 + [pl.BlockSpec((tm, D), row), pl.BlockSpec((1, D), fix), pl.BlockSpec((tm, D), row),
                                           pl.BlockSpec((D, 9 * HE), fix, pipeline_mode=pl.Buffered(1))],
        [pl.BlockSpec((tm, D), row), pl.BlockSpec((1, D), fix)],
        [_sds((T, D), F32), _sds((1, D), F32)],
    )(*dqkv, x, g, dy, wqkv)


def _local_step(x, target, norm_g, b_pw1, w_dw, b_dw, cng, b_pw2, qn, kn, block_weights, grads_ready, grads_tick):
    T, D = x.shape
    ng = lambda l, k: norm_g[l, k][None, :]
    bn = _pick(D, 256, 128)

    w_in, w_out = [None] * 4, [None] * 4
    w_in[0], w_out[0] = block_weights(0, x)
    x1, *gu0 = _ffn_fwd(x, ng(0, 0), w_in[0], w_out[0], "ffn_fwd_0")
    pw1, pw2 = block_weights(1, x1)
    ag, u, hc = _conv_pre(x1, ng(0, 1), pw1, b_pw1, "conv_pre")
    c = _dwconv(u, w_dw, b_dw, "dwconv")
    x2, s = _conv_post(c, x1, cng, pw2, b_pw2, "conv_post")
    w_in[1], w_out[1] = block_weights(2, x2)
    x3, *gu1 = _ffn_fwd(x2, ng(0, 2), w_in[1], w_out[1], "ffn_fwd_1")
    w_in[2], w_out[2] = block_weights(3, x3)
    x4, *gu2 = _ffn_fwd(x3, ng(1, 0), w_in[2], w_out[2], "ffn_fwd_2")
    wqkv, wo = block_weights(4, x4)
    HE = wo.shape[0]
    F = w_out[0].shape[0]
    bf = _pick(F, 256, 128)
    bh = _pick(HE, 512, 128)
    qkv, att, ha = _attn_qkv(x4, ng(1, 1), wqkv, qn, kn, "attn_qkv")
    qkv_s = [(att, 3 * gi) if dil == 1 else (_streams(att, dil, True, f"qkv_streams_{gi}", 3 * gi * HE, 3 * HE), 0)
             for gi, (_, dil) in enumerate(ATTN_GROUPS)]
    tokens = lambda a, dil, name: a if dil == 1 else _streams(a, dil, False, name)
    streams = lambda a, dil, name: a if dil == 1 else _streams(a, dil, True, name)
    os, lses = [], []
    for gi, (window, dil) in enumerate(ATTN_GROUPS):
        o, l = _attn_fwd(*qkv_s[gi], HE, window, dil, f"attn_fwd_{gi}")
        os.append(tokens(o, dil, f"o_tokens_{gi}"))
        lses.append(tokens(l, dil, f"lse_tokens_{gi}"))
    x5, om, lse = _attn_merge(os, lses, x4, wo, "attn_merge")
    w_in[3], w_out[3] = block_weights(5, x5)
    dy, *gu3, sq = _ffn_fwd(x5, ng(1, 2), w_in[3], w_out[3], "ffn_fwd_3", target=target)

    grads = {"ffn_w_in": [None] * 4, "ffn_w_out": [None] * 4}
    dnorm = [[None] * 3 for _ in range(2)]

    def ffn_back(lf, k, xin, gvec, dy, gu, after):
        dx, dg, dgate, dup, a, h, dyb = _ffn_bwd(xin, gvec, dy, gu[0], gu[1], w_in[lf], w_out[lf], f"ffn_bwd_{lf}", after=after)
        grads["ffn_w_in"][lf] = _mm_tn_parts(h, [dgate, dup], bf, 2 * F, 0, f"ffn_dw_in_{lf}",
                                             after=grads_tick(dx) if lf == 0 else ())
        grads["ffn_w_out"][lf] = _mm_tn(a, dyb, bf, D, (F, D), (bf, D), lambda mi, ni: (mi, 0), 1.0, f"ffn_dw_out_{lf}")
        return dx, dg, grads_ready(k, (grads["ffn_w_in"][lf], grads["ffn_w_out"][lf]), dx)

    dx, dnorm[1][2], tok = ffn_back(3, 5, x5, ng(1, 2), dy, gu3, ())
    dom, delta = _attn_out_bwd(dx, om, wo, "attn_out_bwd", after=tok)
    grads["attn_w_o"] = _mm_tn(om, dx, HE, bn, (HE, D), (HE, bn), lambda mi, ni: (0, ni), 1.0, "attn_dw_o")
    dqkv, dgq, dgk = [], [], []
    for gi, (window, dil) in enumerate(ATTN_GROUPS):
        ds = _attn_bwd(*qkv_s[gi], HE, streams(dom, dil, f"dom_streams_{gi}"), streams(lse, dil, f"lse_streams_{gi}"),
                       streams(delta, dil, f"delta_streams_{gi}"), window, dil, f"attn_bwd_{gi}")
        dq, dk, dv = [tokens(d, dil, f"d{nm}_tokens_{gi}") for d, nm in zip(ds, "qkv")]
        dq, dk, a_, b_ = _qk_norm_bwd(dq, dk, qkv, 3 * gi, HE, qn[gi][None, :], kn[gi][None, :], f"qk_norm_bwd_{gi}")
        dqkv += [dq, dk, dv]
        dgq.append(a_)
        dgk.append(b_)
    grads["attn_q_norm"] = jnp.concatenate(dgq, axis=0)
    grads["attn_k_norm"] = jnp.concatenate(dgk, axis=0)
    d_qkv = None
    for gi in range(len(ATTN_GROUPS)):
        d_qkv = _mm_tn_parts(ha, dqkv[3 * gi:3 * gi + 3], bh, 9 * HE, 3 * gi * HE, f"attn_dw_qkv_{gi}", prev=d_qkv)
    grads["attn_w_qkv"] = d_qkv
    dx, dnorm[1][1] = _attn_qkv_bwd(dqkv, x4, ng(1, 1), dx, wqkv, "attn_qkv_bwd")
    tok = grads_ready(4, (grads["attn_w_qkv"], grads["attn_w_o"]), dx)
    dx, dnorm[1][0], tok = ffn_back(2, 3, x3, ng(1, 0), dx, gu2, tok)

    dx, dnorm[0][2], tok = ffn_back(1, 2, x2, ng(0, 2), dx, gu1, tok)
    dc, grads["conv_norm_g"], grads["conv_b_pw2"] = _conv_post_bwd(dx, c, cng, pw2, "conv_post_bwd", after=tok)
    grads["conv_w_pw2"] = _mm_tn(s, dx, D, bn, (D, D), (D, bn), lambda mi, ni: (0, ni), 1.0, "conv_dw_pw2")
    du, grads["conv_w_dw"], grads["conv_b_dw"] = _dwconv_bwd(dc, u, w_dw, "dwconv_bwd")
    dx, dnorm[0][1], dag, grads["conv_b_pw1"] = _conv_pre_bwd(du, ag, x1, ng(0, 1), dx, pw1, "conv_pre_bwd")
    grads["conv_w_pw1"] = _mm_tn(hc, dag, D, 2 * bn, (D, 2 * D), (D, 2 * bn), lambda mi, ni: (0, ni), 1.0, "conv_dw_pw1")
    tok = grads_ready(1, (grads["conv_w_pw1"], grads["conv_w_pw2"]), dx)
    dx, dnorm[0][0], _ = ffn_back(0, 0, x, ng(0, 0), dx, gu0, tok)

    grads["norm_g"] = jnp.concatenate([jnp.concatenate(r, axis=0)[None] for r in dnorm], axis=0)
    return sq, dx, grads


class _Sharded:
    def __init__(self, name, full3, half_axis, shard_axis, src, slab=None):
        self.name, self.full3, self.half_axis, self.shard_axis = name, tuple(full3), half_axis, shard_axis
        self.src, self.slab = src, slab

    def _cut(self, shape, axis, parts):
        s = list(shape)
        s[axis] //= parts
        return tuple(s)

    @property
    def shard3(self):
        return self._cut(self.full3, self.shard_axis, N_CHIPS)

    @property
    def pair3(self):
        return self._cut(self.full3, self.half_axis, 2)

    @property
    def part3(self):
        return self._cut(self.shard3, self.half_axis, 2)

    @staticmethod
    def _slice(ref, axis, idx, parts):
        n = ref.shape[axis] // parts
        start = idx * n
        minor = len(ref.shape) - 1 - axis
        if minor < 2 and not isinstance(start, int):
            start = pl.multiple_of(start, 128 if minor == 0 else (16 if n % 16 == 0 else 8))
        sl = [slice(None)] * len(ref.shape)
        sl[axis] = pl.ds(start, n)
        return ref.at[tuple(sl)]

    def half(self, ref, h):
        return self._slice(ref, self.half_axis, h, 2)

    def shard(self, ref, j):
        return self._slice(ref, self.shard_axis, j, N_CHIPS)


def _place():
    x, y, c = lax.axis_index("x"), lax.axis_index("y"), lax.axis_index("c")
    return x, y, c, 2 * x + y


_RELS = (1, 2, 3)


def _peer(x, y, rel):
    px = 1 - x if rel & 2 else x
    py = 1 - y if rel & 1 else y
    return px, py, 2 * px + py


ANY = pl.BlockSpec(memory_space=pl.ANY)


def _comm_call(body, name, n_in, out_shape, n_sems, aliases=None):
    return pl.pallas_call(
        body, name=name, in_specs=[ANY] * n_in, out_specs=[ANY] * len(out_shape), out_shape=out_shape,
        scratch_shapes=[pltpu.SemaphoreType.DMA((n,)) for n in n_sems],
        input_output_aliases=aliases or {},
        compiler_params=pltpu.CompilerParams(has_side_effects=True),
    )


def _remote(src, dst, send_sem, recv_sem, dev):
    return pltpu.make_async_remote_copy(src_ref=src, dst_ref=dst, send_sem=send_sem, recv_sem=recv_sem, device_id=dev,
                                        device_id_type=MESH)


def _gather_small(small_shards):
    ns = len(small_shards)

    def body(*refs):
        ins, outs = refs[:ns], refs[ns:2 * ns]
        lsem, ssem, rsem = refs[2 * ns:]
        x, y, c, me = _place()
        cols = lambda ref, j: _Sharded._slice(ref, 1, j, N_CHIPS)
        local = [pltpu.make_async_copy(ins[i], cols(outs[i], me), lsem.at[i]) for i in range(ns)]
        sends = []
        for i in range(ns):
            for k, rel in enumerate(_RELS):
                px, py, _ = _peer(x, y, rel)
                sends.append(_remote(ins[i], cols(outs[i], me), ssem.at[3 * i + k], rsem.at[3 * i + k], (px, py, c)))
        for cp in local + sends:
            cp.start()
        for i in range(ns):
            for k, rel in enumerate(_RELS):
                _, _, pj = _peer(x, y, rel)
                got = cols(outs[i], pj)
                _remote(got, got, ssem.at[3 * i + k], rsem.at[3 * i + k], (x, y, c)).wait_recv()
        for cp in sends:
            cp.wait_send()
        for cp in local:
            cp.wait()

    out_shape = [_sds((s.shape[0], s.shape[1] * N_CHIPS), F32) for s in small_shards]
    return _comm_call(body, "gather_small", ns, out_shape, [ns, 3 * ns, 3 * ns])(*small_shards)


HBM = pl.BlockSpec(memory_space=pltpu.HBM)
SEM = pl.BlockSpec(memory_space=pltpu.SEMAPHORE)
DATAFLOW = pltpu.SideEffectType.DATAFLOW_SIDE_EFFECTING


def _in_hbm(a):
    return pltpu.with_memory_space_constraint(a, pltpu.HBM)


def _cast_place(it, shard, scal, after=()):
    a_n, r_n, c_n = it.shard3
    tr = _pick(r_n, 256, 16)
    sa = it.shard_axis

    def body(sc_ref, s_ref, o_ref):
        o_ref[...] = s_ref[...].astype(BF16)

    if it.slab is None:
        src = pl.BlockSpec((1, tr, c_n), lambda a, rb, sc: (a, rb, 0))
    else:
        src = pl.BlockSpec((None, 1, tr, c_n), lambda a, rb, sc: (it.slab, a, rb, 0))
    dst = pl.BlockSpec((1, tr, c_n), lambda a, rb, sc: (a + sc[1] * (a_n if sa == 0 else 0), rb + sc[1] * (r_n // tr if sa == 1 else 0),
                                                       sc[1] if sa == 2 else 0))
    return _call(body, f"cast_place_{it.name}", (a_n, r_n // tr), [src], [dst], [_sds(it.full3, BF16)], prefetch=1,
                 after=after)(scal, shard)[0]


def _gather_start(items, fulls, name):
    ni = len(items)

    def body(*refs):
        outs = refs[ni:]
        ssem, rsem, full = outs[:ni], outs[ni:2 * ni], outs[2 * ni:3 * ni]
        x, y, c, me = _place()
        for i, it in enumerate(items):
            mine = it.half(it.shard(full[i], me), c)
            for k, rel in enumerate(_RELS):
                px, py, _ = _peer(x, y, rel)
                _remote(mine, mine, ssem[i].at[k], rsem[i].at[k], (px, py, c)).start()

    outs = pl.pallas_call(
        body, name=name, in_specs=[HBM] * ni, out_specs=[SEM] * (2 * ni) + [HBM] * ni,
        out_shape=[pltpu.SemaphoreType.DMA((3,))] * (2 * ni) + [pltpu.HBM(it.full3, BF16) for it in items],
        input_output_aliases={j: 2 * ni + j for j in range(ni)},
        compiler_params=pltpu.CompilerParams(has_side_effects=DATAFLOW),
    )(*[_in_hbm(f) for f in fulls])
    return outs[:ni], outs[ni:2 * ni], outs[2 * ni:]


def _gather_forward(items, fulls, ssems, rsems, after, name):
    ni = len(items)

    def body(*refs):
        ssem, rsem = refs[ni:2 * ni], refs[2 * ni:3 * ni]
        outs = refs[3 * ni + 1:]
        full, fsem, gsem = outs[:ni], outs[ni:2 * ni], outs[2 * ni:3 * ni]
        x, y, c, me = _place()
        sib = (x, y, 1 - c)
        for i, it in enumerate(items):
            for k, rel in enumerate(_RELS):
                _, _, pj = _peer(x, y, rel)
                got = it.half(it.shard(full[i], pj), c)
                _remote(got, got, ssem[i].at[k], rsem[i].at[k], sib).wait_recv()
                _remote(got, got, fsem[i].at[k], gsem[i].at[k], sib).start()
        for i, it in enumerate(items):
            mine = it.half(it.shard(full[i], me), c)
            for k in range(3):
                _remote(mine, mine, ssem[i].at[k], rsem[i].at[k], sib).wait_send()

    outs = pl.pallas_call(
        body, name=name, in_specs=[HBM] * ni + [SEM] * (2 * ni) + [ANY],
        out_specs=[HBM] * ni + [SEM] * (2 * ni),
        out_shape=[pltpu.HBM(it.full3, BF16) for it in items] + [pltpu.SemaphoreType.DMA((3,))] * (2 * ni),
        input_output_aliases={i: i for i in range(ni)},
        compiler_params=pltpu.CompilerParams(has_side_effects=DATAFLOW),
    )(*fulls, *ssems, *rsems, after)
    return outs[:ni], outs[ni:2 * ni], outs[2 * ni:]


def _gather_finish(items, fulls, fsems, gsems, name):
    ni = len(items)

    def body(*refs):
        fsem, gsem = refs[ni:2 * ni], refs[2 * ni:3 * ni]
        full = refs[3 * ni:]
        x, y, c, _ = _place()
        sib = (x, y, 1 - c)
        for i, it in enumerate(items):
            for k, rel in enumerate(_RELS):
                _, _, pj = _peer(x, y, rel)
                got = it.half(it.shard(full[i], pj), 1 - c)
                _remote(got, got, fsem[i].at[k], gsem[i].at[k], sib).wait_recv()
                sent = it.half(it.shard(full[i], pj), c)
                _remote(sent, sent, fsem[i].at[k], gsem[i].at[k], sib).wait_send()

    return pl.pallas_call(
        body, name=name, in_specs=[HBM] * ni + [SEM] * (2 * ni), out_specs=[HBM] * ni,
        out_shape=[pltpu.HBM(it.full3, BF16) for it in items],
        input_output_aliases={i: i for i in range(ni)},
        compiler_params=pltpu.CompilerParams(has_side_effects=DATAFLOW),
    )(*fulls, *fsems, *gsems)


def _exchange_start(name, arrays, n_sems, copies):
    na, ns = len(arrays), len(n_sems)

    def body(*refs):
        outs = refs[na:]
        ssem, rsem, thru, token = outs[:ns], outs[ns:2 * ns], outs[2 * ns:2 * ns + na], outs[-1]
        for send, _ in copies(thru, ssem, rsem):
            send.start()
        token[...] = jnp.zeros_like(token)

    outs = pl.pallas_call(
        body, name=name, in_specs=[HBM] * na,
        out_specs=[SEM] * (2 * ns) + [HBM] * na + [pl.BlockSpec(memory_space=pltpu.VMEM)],
        out_shape=[pltpu.SemaphoreType.DMA((n,)) for n in n_sems] * 2 + [pltpu.HBM(a.shape, a.dtype) for a in arrays]
        + [_sds((8, 128), F32)],
        input_output_aliases={j: 2 * ns + j for j in range(na)},
        compiler_params=pltpu.CompilerParams(has_side_effects=DATAFLOW),
    )(*[_in_hbm(a) for a in arrays])
    return outs[:ns], outs[ns:2 * ns], outs[2 * ns:2 * ns + na], outs[-1]


def _exchange_wait(name, arrays, ssems, rsems, copies, after):
    na, ns, nw = len(arrays), len(ssems), len(after)

    def body(*refs):
        ssem, rsem = refs[na:na + ns], refs[na + ns:na + 2 * ns]
        thru = refs[na + 2 * ns + nw:]
        for send, recv in copies(thru, ssem, rsem):
            recv.wait_recv()
            send.wait_send()

    return pl.pallas_call(
        body, name=name, in_specs=[HBM] * na + [SEM] * (2 * ns) + [ANY] * nw, out_specs=[HBM] * na,
        out_shape=[pltpu.HBM(a.shape, a.dtype) for a in arrays],
        input_output_aliases={i: i for i in range(na)},
        compiler_params=pltpu.CompilerParams(has_side_effects=DATAFLOW),
    )(*arrays, *ssems, *rsems, *after)


def _pair_copies(items):
    n = len(items)

    def copies(a, ssem, rsem):
        x, y, c, _ = _place()
        cps = [_remote(it.half(a[i], 1 - c), a[n + i], ssem[i].at[0], rsem[i].at[0], (x, y, 1 - c)) for i, it in enumerate(items)]
        return [(cp, cp) for cp in cps]

    return copies


def _chip_copies(items):
    n = len(items)

    def copies(a, ssem, rsem):
        x, y, c, _ = _place()
        cps = []
        for i, it in enumerate(items):
            for k, rel in enumerate(_RELS):
                px, py, pj = _peer(x, y, rel)
                cps.append(_remote(it.shard(a[i], pj), a[n + i].at[k], ssem[i].at[k], rsem[i].at[k], (px, py, c)))
        return [(cp, cp) for cp in cps]

    return copies


def _fill_copies(items):
    def copies(a, ssem, rsem):
        x, y, c, _ = _place()
        sib = (x, y, 1 - c)
        out = []
        for i, it in enumerate(items):
            mine, other = it.half(a[i], c), it.half(a[i], 1 - c)
            out.append((_remote(mine, mine, ssem[i].at[0], rsem[i].at[0], sib), _remote(other, other, ssem[i].at[0], rsem[i].at[0], sib)))
        return out

    return copies


def _ew_tiles(d, most=256):
    _, rows, cols = d.part3
    return _pick(rows, most, 16), cols


def _pair_add(d, g_full, got, scal):
    tr, tc = _ew_tiles(d, 512)
    a_n, r_n, c_n = d.pair3
    ha = d.half_axis

    def body(sc_ref, g_ref, r_ref, ob_ref):
        ob_ref[...] = (g_ref[...].astype(F32) + r_ref[...].astype(F32)).astype(BF16)

    blk = (1, tr, tc)
    same = pl.BlockSpec(blk, lambda a, rb, cb, sc: (a, rb, cb))
    mine = pl.BlockSpec(blk, lambda a, rb, cb, sc: (a + sc[0] * (a_n if ha == 0 else 0), rb + sc[0] * (r_n // tr if ha == 1 else 0), cb))
    return _call(body, f"pair_add_{d.name}", (a_n, r_n // tr, c_n // tc), [mine, same], [same],
                 [_sds(d.pair3, BF16)], prefetch=1)(scal, g_full, got)[0]


def _chip_reduce(d, g_full, pair_got, got, scal):
    tr, tc = _ew_tiles(d, 512)
    a_n, r_n, c_n = d.part3
    pa_n, pr_n, _ = d.pair3
    ha, sa = d.half_axis, d.shard_axis

    def body(sc_ref, g_ref, q_ref, r0, r1, r2, o_ref):
        p = g_ref[...].astype(F32) + q_ref[...].astype(F32)
        o_ref[...] = ((p + r0[...].astype(F32)) + r1[...].astype(F32)) + r2[...].astype(F32)

    blk = (1, tr, tc)
    own_a = lambda a, sc: a + sc[1] * (a_n if sa == 0 else 0)
    own_r = lambda rb, sc: rb + sc[1] * (r_n // tr if sa == 1 else 0)
    own_c = lambda sc: sc[1] if sa == 2 else 0
    own = pl.BlockSpec(blk, lambda a, rb, sc: (own_a(a, sc), own_r(rb, sc), own_c(sc)))
    mine = pl.BlockSpec(blk, lambda a, rb, sc: (own_a(a, sc) + sc[0] * (pa_n if ha == 0 else 0),
                                                own_r(rb, sc) + sc[0] * (pr_n // tr if ha == 1 else 0), own_c(sc)))
    slot = lambda k: pl.BlockSpec((None,) + blk, lambda a, rb, sc: (k, a, rb, 0))
    out = pl.BlockSpec(blk, lambda a, rb, sc: (a + sc[0] * (a_n if ha == 0 else 0), rb + sc[0] * (r_n // tr if ha == 1 else 0), 0))
    return _call(body, f"chip_reduce_{d.name}", (a_n, r_n // tr), [mine, own, slot(0), slot(1), slot(2)], [out],
                 [_sds(d.shard3, F32)], prefetch=1)(scal, g_full, pair_got, got, got, got)[0]


def _adam_math(g, w, m, v):
    m = ADAM_B1 * m + (1.0 - ADAM_B1) * g
    v = ADAM_B2 * v + (1.0 - ADAM_B2) * (g * g)
    m_hat = m / (1.0 - ADAM_B1 ** ADAM_STEP)
    v_hat = v / (1.0 - ADAM_B2 ** ADAM_STEP)
    delta = -ADAM_LR * (m_hat / (jnp.sqrt(v_hat) + ADAM_EPS) + ADAM_WD * w)
    return delta, m, v


def _adam(d, g, w, m, v, prev=None):
    tr, tc = _ew_tiles(d)
    a_n, r_n, c_n = d.shard3
    n_prev = 0 if prev is None else 4

    def body(g_ref, w_ref, m_ref, v_ref, *rest):
        go_ref, d_ref, mo_ref, vo_ref = rest[n_prev:]
        gv = g_ref[...]
        go_ref[...] = gv
        d_ref[...], mo_ref[...], vo_ref[...] = _adam_math(gv, w_ref[...], m_ref[...], v_ref[...])

    plain = pl.BlockSpec((1, tr, tc), lambda a, rb: (a, rb, 0))
    if d.slab is None:
        wspec, shape = plain, d.shard3
    else:
        wspec, shape = pl.BlockSpec((None, 1, tr, tc), lambda a, rb: (d.slab, a, rb, 0)), (4,) + d.shard3
    in_specs = [plain] + [wspec] * 3
    args = [g, w, m, v]
    aliases = None
    if prev is not None:
        in_specs += [pl.BlockSpec(memory_space=pl.ANY)] * 4
        args += list(prev)
        aliases = {4 + k: k for k in range(4)}
    return _call(body, f"adam_{d.name}", (a_n, r_n // tr), in_specs, [wspec] * 4, [_sds(shape, F32)] * 4, aliases=aliases)(*args)


def _adam_small(gs, ws, ms, vs):
    n = len(gs)

    def body(*refs):
        for i in range(n):
            g, w, m, v = (refs[k * n + i][...] for k in range(4))
            d, mo, vo = _adam_math(g, w, m, v)
            refs[4 * n + i][...] = d
            refs[5 * n + i][...] = mo
            refs[6 * n + i][...] = vo

    vm = pl.BlockSpec(memory_space=pltpu.VMEM)
    outs = pl.pallas_call(body, name="adam_small", in_specs=[vm] * (4 * n), out_specs=[vm] * (3 * n),
                          out_shape=[_sds(g.shape, F32) for g in gs] * 3)(*gs, *ws, *ms, *vs)
    return outs[:n], outs[n:2 * n], outs[2 * n:]


def _allreduce_small(packed):
    rows, cols = packed.shape
    others = [(dx, dy, dc) for dx in (0, 1) for dy in (0, 1) for dc in (0, 1) if (dx, dy, dc) != (0, 0, 0)]

    def body(in_ref, out_ref, buf, ssem, rsem):
        x, y, c, _ = _place()
        lin = 4 * x + 2 * y + c
        buf[lin] = in_ref[...]
        cps = []
        for k, (dx, dy, dc) in enumerate(others):
            px = 1 - x if dx else x
            py = 1 - y if dy else y
            pc = 1 - c if dc else c
            cps.append((pltpu.make_async_remote_copy(src_ref=in_ref, dst_ref=buf.at[lin], send_sem=ssem.at[k], recv_sem=rsem.at[k],
                                                     device_id=(px, py, pc), device_id_type=MESH), 4 * px + 2 * py + pc))
        for cp, _ in cps:
            cp.start()
        for k, (cp, plin) in enumerate(cps):
            pltpu.make_async_remote_copy(src_ref=in_ref, dst_ref=buf.at[plin], send_sem=ssem.at[k], recv_sem=rsem.at[k],
                                         device_id=(x, y, c), device_id_type=MESH).wait_recv()
        for cp, _ in cps:
            cp.wait_send()
        acc = buf[0]
        for dev in range(1, 8):
            acc = acc + buf[dev]
        out_ref[...] = acc

    vm = pl.BlockSpec(memory_space=pltpu.VMEM)
    return pl.pallas_call(
        body, name="allreduce_small", in_specs=[vm], out_specs=vm, out_shape=_sds((rows, cols), F32),
        scratch_shapes=[pltpu.VMEM((8, rows, cols), F32), pltpu.SemaphoreType.DMA((7,)), pltpu.SemaphoreType.DMA((7,))],
        compiler_params=pltpu.CompilerParams(has_side_effects=True),
    )(packed)


_BIG = ("ffn_w_in", "ffn_w_out", "conv_w_pw1", "conv_w_pw2", "attn_w_qkv", "attn_w_o")
_SMALL = ("norm_g", "conv_b_pw1", "conv_w_dw", "conv_b_dw", "conv_norm_g", "conv_b_pw2", "attn_q_norm", "attn_k_norm")
_NAMES = ("norm_g", "ffn_w_in", "ffn_w_out", "conv_w_pw1", "conv_b_pw1", "conv_w_dw", "conv_b_dw", "conv_norm_g", "conv_w_pw2",
          "conv_b_pw2", "attn_w_qkv", "attn_q_norm", "attn_k_norm", "attn_w_o")


def _rows8(a, width):
    a = a.reshape(-1, min(a.shape[-1], width))
    return jnp.pad(a, ((0, -a.shape[0] % 8), (0, width - a.shape[1])))


def kernel(x, norm_g, ffn_w_in, ffn_w_out, conv_w_pw1, conv_b_pw1, conv_w_dw, conv_b_dw, conv_norm_g, conv_w_pw2, conv_b_pw2, attn_w_qkv, attn_q_norm, attn_k_norm, attn_w_o, loss_target, m_norm_g, m_ffn_w_in, m_ffn_w_out, m_conv_w_pw1, m_conv_b_pw1, m_conv_w_dw, m_conv_b_dw, m_conv_norm_g, m_conv_w_pw2, m_conv_b_pw2, m_attn_w_qkv, m_attn_q_norm, m_attn_k_norm, m_attn_w_o, v_norm_g, v_ffn_w_in, v_ffn_w_out, v_conv_w_pw1, v_conv_b_pw1, v_conv_w_dw, v_conv_b_dw, v_conv_norm_g, v_conv_w_pw2, v_conv_b_pw2, v_attn_w_qkv, v_attn_q_norm, v_attn_k_norm, v_attn_w_o):
    w = dict(zip(_NAMES, (norm_g, ffn_w_in, ffn_w_out, conv_w_pw1, conv_b_pw1, conv_w_dw, conv_b_dw, conv_norm_g, conv_w_pw2,
                          conv_b_pw2, attn_w_qkv, attn_q_norm, attn_k_norm, attn_w_o)))
    m = dict(zip(_NAMES, (m_norm_g, m_ffn_w_in, m_ffn_w_out, m_conv_w_pw1, m_conv_b_pw1, m_conv_w_dw, m_conv_b_dw, m_conv_norm_g,
                          m_conv_w_pw2, m_conv_b_pw2, m_attn_w_qkv, m_attn_q_norm, m_attn_k_norm, m_attn_w_o)))
    v = dict(zip(_NAMES, (v_norm_g, v_ffn_w_in, v_ffn_w_out, v_conv_w_pw1, v_conv_b_pw1, v_conv_w_dw, v_conv_b_dw, v_conv_norm_g,
                          v_conv_w_pw2, v_conv_b_pw2, v_attn_w_qkv, v_attn_q_norm, v_attn_k_norm, v_attn_w_o)))
    T, D = x.shape[1:]
    F = ffn_w_out.shape[2] * N_CHIPS
    HE = attn_w_o.shape[1] * N_CHIPS
    cx, cy, cc = lax.axis_index("x"), lax.axis_index("y"), lax.axis_index("c")
    me = 2 * cx + cy
    scal = jnp.stack([cc, me]).astype(jnp.int32)

    ffn_in = lambda lf: _Sharded(f"ffn_w_in_{lf}", (2, D // 2, 2 * F), 0, 2, "ffn_w_in", lf)
    ffn_out = lambda lf: _Sharded(f"ffn_w_out_{lf}", (4, F // 4, D), 1, 0, "ffn_w_out", lf)
    items = [
        ffn_in(0), ffn_out(0),
        _Sharded("conv_w_pw1", (2, D // 2, 2 * D), 0, 2, "conv_w_pw1"), _Sharded("conv_w_pw2", (4, D // 4, D), 1, 0, "conv_w_pw2"),
        ffn_in(1), ffn_out(1), ffn_in(2), ffn_out(2),
        _Sharded("attn_w_qkv", (2, D // 2, 9 * HE), 0, 2, "attn_w_qkv"), _Sharded("attn_w_o", (4, HE // 4, D), 1, 0, "attn_w_o"),
        ffn_in(3), ffn_out(3),
    ]
    mat_shapes = {"ffn_w_in": (D, 2 * F), "ffn_w_out": (F, D), "conv_w_pw1": (D, 2 * D), "conv_w_pw2": (D, D),
                  "attn_w_qkv": (D, 9 * HE), "attn_w_o": (HE, D)}

    def as_shards(a, n):
        it = next(i for i in items if i.src == n)
        return a.reshape(((4,) if it.slab is not None else ()) + it.shard3)

    norm_full, dw_full = _gather_small([norm_g.reshape(6, D // 4), conv_w_dw.reshape(CONV_WIDTH, D // 4)])
    place = lambda its, after: [_cast_place(it, as_shards(w[it.src], it.src), scal, after) for it in its]
    ssems, rsems, fulls = _gather_start(items[:2], place(items[:2], ()), "gather_start_first")
    more = _gather_start(items[2:], place(items[2:], fulls[:1]), "gather_start_rest")
    ssems, rsems, fulls = [list(a) + list(b) for a, b in zip((ssems, rsems, fulls), more)]

    def block_weights(k, after):
        sel = slice(2 * k, 2 * k + 2)
        got, fsems, gsems = _gather_forward(items[sel], fulls[sel], ssems[sel], rsems[sel], after, f"gather_forward_{k}")
        done = _gather_finish(items[sel], got, fsems, gsems, f"gather_finish_{k}")
        return [a.reshape(mat_shapes[it.src]) for a, it in zip(done, items[sel])]

    res = {}
    flight = []

    def advance(entry, k, dx):
        stage, its, st = entry
        n = len(its)
        if stage == 1:
            ssem, rsem, arrs = st
            done = _exchange_wait(f"grads_pair_wait_{k}", arrs, ssem, rsem, _pair_copies(its), dx)
            sums = [_pair_add(it, g, r, scal) for it, g, r in zip(its, done[:n], done[n:])]
            land = [lax.empty((3,) + it.part3, BF16) for it in its]
            ssem, rsem, arrs, tok = _exchange_start(f"grads_chip_start_{k}", sums + land, [3] * n, _chip_copies(its))
            return (2, its, (ssem, rsem, arrs, list(zip(done[:n], done[n:])))), tok
        if stage == 2:
            ssem, rsem, arrs, pairs = st
            got = _exchange_wait(f"grads_chip_wait_{k}", arrs, ssem, rsem, _chip_copies(its), dx)[n:]
            red = [_chip_reduce(it, g, q, r, scal) for it, (g, q), r in zip(its, pairs, got)]
            ssem, rsem, arrs, tok = _exchange_start(f"grads_fill_start_{k}", red, [1] * n, _fill_copies(its))
            return (3, its, (ssem, rsem, arrs)), tok
        ssem, rsem, arrs = st
        for it, g in zip(its, _exchange_wait(f"grads_fill_wait_{k}", arrs, ssem, rsem, _fill_copies(its), dx)):
            nm = it.src
            res[nm] = _adam(it, g, as_shards(w[nm], nm), as_shards(m[nm], nm), as_shards(v[nm], nm), prev=res.get(nm))
        return None, None

    def step_flight(dx):
        toks, left = [], []
        for k, entry in flight:
            entry, tok = advance(entry, k, dx)
            if entry is not None:
                left.append((k, entry))
                toks.append(tok)
        flight[:] = left
        return toks

    def grads_ready(k, pairs, dx):
        its = items[2 * k:2 * k + 2]
        toks = step_flight((dx,))
        g16 = [p.reshape(it.full3) for p, it in zip(pairs, its)]
        land = [lax.empty(it.pair3, BF16) for it in its]
        ssem, rsem, arrs, tok = _exchange_start(f"grads_pair_start_{k}", g16 + land, [1] * len(its), _pair_copies(its))
        flight.append((k, (1, its, (ssem, rsem, arrs))))
        return toks + [tok]

    sq, dx, grads = _local_step(x[0], loss_target[0], norm_full.reshape(2, 3, D), conv_b_pw1, dw_full, conv_b_dw, conv_norm_g,
                                conv_b_pw2, attn_q_norm[0], attn_k_norm[0], block_weights, grads_ready,
                                lambda dx: step_flight((dx,)))
    loss = lax.psum(0.5 * jnp.sum(sq) / D, ("x", "y", "c"))
    k_last, entry = flight.pop()
    entry, tok = advance(entry, k_last, (dx,))
    while flight:
        step_flight((dx, tok))

    out_g, out_d, out_m, out_v = {}, {}, {}, {}

    parts = [_rows8(grads[n], D) for n in _SMALL]
    tot = _allreduce_small(jnp.concatenate(parts, axis=0))
    sg, r0 = {}, 0
    for n, p in zip(_SMALL, parts):
        last = grads[n].shape[-1]
        g = tot[r0:r0 + grads[n].size // min(last, D), :min(last, D)].reshape(-1, last)
        r0 += p.shape[0]
        if n in ("norm_g", "conv_w_dw"):
            g = lax.dynamic_slice_in_dim(g, me * (D // 4), D // 4, axis=1)
        sg[n] = g
    flat = lambda a: a.reshape(-1, a.shape[-1])
    ds, ms, vs = _adam_small([sg[n] for n in _SMALL], [flat(w[n]) for n in _SMALL], [flat(m[n]) for n in _SMALL],
                             [flat(v[n]) for n in _SMALL])
    for i, n in enumerate(_SMALL):
        out_g[n], out_d[n], out_m[n], out_v[n] = (a.reshape(w[n].shape) for a in (sg[n], ds[i], ms[i], vs[i]))

    behind = tuple(r[1] for r in res.values()) + tuple(ds)
    while entry is not None:
        entry, _ = advance(entry, k_last, behind)
    for n in _BIG:
        out_g[n], out_d[n], out_m[n], out_v[n] = (a.reshape(w[n].shape) for a in res[n])

    return (loss, dx[None], *[out_g[n] for n in _NAMES], *[out_d[n] for n in _NAMES], *[out_m[n] for n in _NAMES],
            *[out_v[n] for n in _NAMES])
```
